```python
import math
import jax, jax.numpy as jnp
from jax import lax
import numpy as np

D_MODEL = 1024
BATCH = 8
SEQ = 2048
DEPTH = 4
DEC_BATCH = 128
DEC_SEQ = 1
PAST_LEN = 16384
PAGE_SIZE = 128

N_MIXERS = 3
D_FF = 4 * D_MODEL
RMS_EPS = 1e-6
NEG_BIG = -1e30

GDN_HEADS = 8
GDN_DK = D_MODEL // GDN_HEADS
GDN_DV = D_MODEL // GDN_HEADS
GDN_KEY = GDN_HEADS * GDN_DK
GDN_VAL = GDN_HEADS * GDN_DV
GDN_CONV_CH = 2 * GDN_KEY + GDN_VAL
GDN_IN = GDN_CONV_CH + GDN_VAL + 2 * GDN_HEADS
CONV_W = 4
GDN_CHUNK = 64

ML_HEADS = 4
ML_DV = D_MODEL // ML_HEADS
ML_DK = ML_DV // 2
ML_IN = ML_HEADS * (2 * ML_DK + ML_DV) + D_MODEL + 2 * ML_HEADS
ML_CHUNK = 64
GATE_CAP = 15.0

RW_HEAD = 64
RW_HEADS = D_MODEL // RW_HEAD
RW_DECAY_LORA = 64
RW_A_LORA = 64
RW_GATE_LORA = 160
RW_GN_EPS = 64e-5

N_GDN_LAYERS = (DEPTH + 2) // 3
N_MLSTM_LAYERS = (DEPTH + 1) // 3
N_RWKV_LAYERS = DEPTH // 3

kernel_name = "hybrid_gdn_mlstm_rwkv7_decode_step"


def rmsnorm(x, g, eps=RMS_EPS):
    xf = x.astype(jnp.float32)
    y = xf * lax.rsqrt(jnp.mean(xf * xf, axis=-1, keepdims=True) + eps)
    return (y * g.astype(jnp.float32)).astype(x.dtype)


def to_chunks(a, size, pad_value=0.0):
    b, t = a.shape[0], a.shape[1]
    n = -(-t // size)
    pad = [(0, 0), (0, n * size - t)] + [(0, 0)] * (a.ndim - 2)
    a = jnp.pad(a.astype(jnp.float32), pad, constant_values=pad_value)
    a = a.reshape((b, n, size) + a.shape[2:])
    return jnp.moveaxis(a, 3, 1)


def from_chunks(o, t):
    n, b, h, l, d = o.shape
    return jnp.transpose(o, (1, 0, 3, 2, 4)).reshape(b, n * l, h, d)[:, :t]


def causal_conv(u, buf, w):
    t = u.shape[1]
    full = jnp.concatenate([buf.astype(u.dtype), u], axis=1)
    y = full[:, 0:t, :] * w[:, 0]
    for j in range(1, CONV_W):
        y = y + full[:, j:j + t, :] * w[:, j]
    return y, full[:, t:, :]


def sqrelu_mlp(h, w1, w2):
    return jnp.square(jax.nn.relu(h @ w1)) @ w2


def gated_delta_chunked(q, k, v, g, beta, s0):
    t = q.shape[1]
    l = GDN_CHUNK
    qc, kc, vc = to_chunks(q, l), to_chunks(k, l), to_chunks(v, l)
    gc = jnp.cumsum(to_chunks(g, l), axis=-1)
    bc = to_chunks(beta, l)
    incl = jnp.tril(jnp.ones((l, l), dtype=bool))
    strict = jnp.tril(jnp.ones((l, l), dtype=bool), -1)
    diff = gc[..., :, None] - gc[..., None, :]
    dmat = jnp.where(incl, jnp.exp(jnp.where(incl, diff, 0.0)), 0.0)
    kb = kc * bc[..., None]
    a_mat = jnp.where(strict, jnp.einsum('bhnid,bhnjd->bhnij', kb, kc) * dmat, 0.0)
    rhs = jnp.concatenate([vc * bc[..., None], kb * jnp.exp(gc)[..., None]], axis=-1)
    sol = lax.linalg.triangular_solve(a_mat + jnp.eye(l, dtype=jnp.float32), rhs,
                                      left_side=True, lower=True, unit_diagonal=True)
    dv = vc.shape[-1]
    u, w = sol[..., :dv], sol[..., dv:]
    qk = jnp.where(incl, jnp.einsum('bhnid,bhnjd->bhnij', qc, kc) * dmat, 0.0)
    q_dec = qc * jnp.exp(gc)[..., None]
    k_dec = kc * jnp.exp(gc[..., -1:] - gc)[..., None]
    g_last = jnp.exp(gc[..., -1])

    def step(s, inp):
        q_i, qk_i, u_i, w_i, k_i, gl = inp
        v_new = u_i - jnp.einsum('bhck,bhkv->bhcv', w_i, s)
        o = jnp.einsum('bhck,bhkv->bhcv', q_i, s) + jnp.einsum('bhij,bhjv->bhiv', qk_i, v_new)
        s = s * gl[..., None, None] + jnp.einsum('bhck,bhcv->bhkv', k_i, v_new)
        return s, o

    xs = tuple(jnp.moveaxis(a, 2, 0) for a in (q_dec, qk, u, w, k_dec, g_last))
    s, o = lax.scan(step, s0.astype(jnp.float32), xs)
    return from_chunks(o, t), s


def gdn_mixer(h, conv_buf, s0, w_in, conv_w, a_log, dt_bias, norm_w, w_out):
    b, t, _ = h.shape
    proj = h @ w_in
    qkv, z, beta_pre, a_pre = jnp.split(
        proj, [GDN_CONV_CH, GDN_CONV_CH + GDN_VAL, GDN_CONV_CH + GDN_VAL + GDN_HEADS], axis=-1)
    qkv, new_buf = causal_conv(qkv, conv_buf, conv_w)
    qkv = jax.nn.silu(qkv)
    q, k, v = jnp.split(qkv, [GDN_KEY, 2 * GDN_KEY], axis=-1)
    q = q.reshape(b, t, GDN_HEADS, GDN_DK).astype(jnp.float32)
    k = k.reshape(b, t, GDN_HEADS, GDN_DK).astype(jnp.float32)
    q = q * lax.rsqrt(jnp.sum(q * q, -1, keepdims=True) + 1e-6) * (GDN_DK ** -0.5)
    k = k * lax.rsqrt(jnp.sum(k * k, -1, keepdims=True) + 1e-6)
    v = v.reshape(b, t, GDN_HEADS, GDN_DV)
    beta = jax.nn.sigmoid(beta_pre.astype(jnp.float32))
    g = -jnp.exp(a_log.astype(jnp.float32)) * jax.nn.softplus(a_pre.astype(jnp.float32) + dt_bias.astype(jnp.float32))
    o, s = gated_delta_chunked(q, k, v, g, beta, s0)
    o = rmsnorm(o, norm_w) * jax.nn.silu(z.reshape(b, t, GDN_HEADS, GDN_DV).astype(jnp.float32))
    return o.reshape(b, t, GDN_VAL).astype(h.dtype) @ w_out, new_buf, s


def mlstm_chunked(q, k, v, i_pre, logf, c0, n0, m0):
    t = q.shape[1]
    l = ML_CHUNK
    qc, kc, vc = to_chunks(q, l), to_chunks(k, l), to_chunks(v, l)
    ic = to_chunks(i_pre, l, NEG_BIG)
    bcum = jnp.cumsum(to_chunks(logf, l), axis=-1)
    incl = jnp.tril(jnp.ones((l, l), dtype=bool))
    dlog = jnp.where(incl, bcum[..., :, None] - bcum[..., None, :] + ic[..., None, :], NEG_BIG)
    m_intra = jnp.max(dlog, axis=-1)
    p = jnp.where(incl, jnp.exp(dlog - m_intra[..., None]), 0.0) * jnp.einsum('bhnid,bhnjd->bhnij', qc, kc)
    num_intra = jnp.einsum('bhnij,bhnjv->bhniv', p, vc)
    den_intra = jnp.sum(p, axis=-1)
    a_log = bcum[..., -1:] - bcum + ic
    m_chunk = jnp.max(a_log, axis=-1)
    wts = jnp.exp(a_log - m_chunk[..., None])
    kv_chunk = jnp.einsum('bhnc,bhnck,bhncv->bhnkv', wts, kc, vc)
    k_chunk = jnp.einsum('bhnc,bhnck->bhnk', wts, kc)
    b_last = bcum[..., -1]

    def step(carry, inp):
        c_mat, n_vec, m_prev = carry
        q_i, b_i, mi_i, num_i, den_i, kv_i, kn_i, mc_i, bl_i = inp
        m_t = jnp.maximum(b_i + m_prev[..., None], mi_i)
        s_inter = jnp.exp(b_i + m_prev[..., None] - m_t)
        s_intra = jnp.exp(mi_i - m_t)
        num = s_inter[..., None] * jnp.einsum('bhlk,bhkv->bhlv', q_i, c_mat) + s_intra[..., None] * num_i
        den = s_inter * jnp.einsum('bhlk,bhk->bhl', q_i, n_vec) + s_intra * den_i
        h_t = num / jnp.maximum(jnp.abs(den), jnp.exp(-m_t))[..., None]
        m_new = jnp.maximum(bl_i + m_prev, mc_i)
        f_s = jnp.exp(bl_i + m_prev - m_new)
        i_s = jnp.exp(mc_i - m_new)
        c_mat = f_s[..., None, None] * c_mat + i_s[..., None, None] * kv_i
        n_vec = f_s[..., None] * n_vec + i_s[..., None] * kn_i
        return (c_mat, n_vec, m_new), h_t

    xs = tuple(jnp.moveaxis(a, 2, 0) for a in
               (qc, bcum, m_intra, num_intra, den_intra, kv_chunk, k_chunk, m_chunk, b_last))
    init = (c0.astype(jnp.float32), n0.astype(jnp.float32), m0.astype(jnp.float32))
    (c_mat, n_vec, m), hs = lax.scan(step, init, xs)
    return from_chunks(hs, t), c_mat, n_vec, m


def mlstm_mixer(h, c0, n0, m0, w_in, b_if, norm_w, w_out):
    b, t, d = h.shape
    proj = h @ w_in
    qk_w = ML_HEADS * ML_DK
    v_w = ML_HEADS * ML_DV
    q, k, v, o_pre, if_pre = jnp.split(proj, [qk_w, 2 * qk_w, 2 * qk_w + v_w, 2 * qk_w + v_w + d], axis=-1)
    q = q.reshape(b, t, ML_HEADS, ML_DK) * (ML_DK ** -0.5)
    k = k.reshape(b, t, ML_HEADS, ML_DK)
    v = v.reshape(b, t, ML_HEADS, ML_DV)
    gates = if_pre.astype(jnp.float32) + b_if.astype(jnp.float32)
    gates = GATE_CAP * jnp.tanh(gates / GATE_CAP)
    i_pre = gates[..., :ML_HEADS]
    logf = jax.nn.log_sigmoid(gates[..., ML_HEADS:])
    h_til, c_mat, n_vec, m = mlstm_chunked(q, k, v, i_pre, logf, c0, n0, m0)
    h_n = rmsnorm(h_til, norm_w.reshape(ML_HEADS, ML_DV))
    out = jax.nn.sigmoid(o_pre.astype(jnp.float32)) * h_n.reshape(b, t, d)
    return out.astype(h.dtype) @ w_out, c_mat, n_vec, m


def rwkv7_scan(r, w, k, v, a, bvec, s0):
    def step(s, inp):
        r_t, w_t, k_t, v_t, a_t, b_t = inp
        sa = jnp.einsum('bhvk,bhk->bhv', s, a_t)
        s = s * w_t[:, :, None, :] + sa[..., None] * b_t[:, :, None, :] + v_t[..., None] * k_t[:, :, None, :]
        return s, jnp.einsum('bhvk,bhk->bhv', s, r_t)
    xs = tuple(jnp.moveaxis(z, 1, 0) for z in (r, w, k, v, a, bvec))
    s, y = lax.scan(step, s0.astype(jnp.float32), xs)
    return jnp.moveaxis(y, 0, 1), s


def rwkv7_mixer(h, x_prev, s0, mu, w_rkv, w_o, w0, w1, w2, a0, a1, a2, g1, g2, k_k, k_a, r_k, lnx_w, lnx_b):
    b, t, d = h.shape
    prev = jnp.concatenate([x_prev[:, None, :].astype(h.dtype), h[:, :-1]], axis=1)
    xx = prev - h
    xr, xw, xk, xv, xa, xg = (h + xx * mu[j] for j in range(6))
    r = xr @ w_rkv[0]
    k = xk @ w_rkv[1]
    v = xv @ w_rkv[2]
    w_log = -jax.nn.softplus(-(w0 + jnp.tanh(xw @ w1) @ w2).astype(jnp.float32)) - 0.5
    decay = jnp.exp(-jnp.exp(w_log))
    a = jax.nn.sigmoid((a0 + (xa @ a1) @ a2).astype(jnp.float32))
    g = jax.nn.sigmoid(xg @ g1) @ g2

    def heads(z):
        return z.reshape(b, t, RW_HEADS, RW_HEAD).astype(jnp.float32)

    kk = heads(k * k_k)
    kk = kk * lax.rsqrt(jnp.maximum(jnp.sum(kk * kk, -1, keepdims=True), 1e-24))
    k = k.astype(jnp.float32) * (1.0 + (a - 1.0) * k_a.astype(jnp.float32))
    r_h, k_h, v_h, a_h = heads(r), heads(k), heads(v), heads(a)
    y, s = rwkv7_scan(r_h, heads(decay), k_h, v_h, -kk, kk * a_h, s0)
    mean = jnp.mean(y, -1, keepdims=True)
    var = jnp.mean(jnp.square(y - mean), -1, keepdims=True)
    y = (y - mean) * lax.rsqrt(var + RW_GN_EPS)
    y = y * lnx_w.reshape(RW_HEADS, RW_HEAD) + lnx_b.reshape(RW_HEADS, RW_HEAD)
    y = y + jnp.sum(r_h * k_h * r_k, -1, keepdims=True) * v_h
    out = (y.reshape(b, t, d).astype(h.dtype) * g) @ w_o
    return out, h[:, -1], s


def trunk(x, states, w):
    conv_in, gs_in, c_in, n_in, m_in, shift_in, rs_in = states
    o_conv, o_gs, o_c, o_n, o_m, o_shift, o_rs = [], [], [], [], [], [], []
    for i in range(DEPTH):
        j = i // N_MIXERS
        hn = rmsnorm(x, w['norm_mix'][i])
        if i % N_MIXERS == 0:
            out, cb, s = gdn_mixer(hn, conv_in[j], gs_in[j], w['gdn_w_in'][j], w['gdn_conv_w'][j],
                                   w['gdn_a_log'][j], w['gdn_dt_bias'][j], w['gdn_norm_w'][j], w['gdn_w_out'][j])
            o_conv.append(cb.astype(conv_in.dtype))
            o_gs.append(s.astype(gs_in.dtype))
        elif i % N_MIXERS == 1:
            out, c_mat, n_vec, m = mlstm_mixer(hn, c_in[j], n_in[j], m_in[j], w['ml_w_in'][j],
                                               w['ml_b_if'][j], w['ml_norm_w'][j], w['ml_w_out'][j])
            o_c.append(c_mat.astype(c_in.dtype))
            o_n.append(n_vec.astype(n_in.dtype))
            o_m.append(m.astype(m_in.dtype))
        else:
            out, xp, s = rwkv7_mixer(hn, shift_in[j], rs_in[j], w['rw_mu'][j], w['rw_w_rkv'][j], w['rw_w_o'][j],
                                     w['rw_w0'][j], w['rw_w1'][j], w['rw_w2'][j], w['rw_a0'][j], w['rw_a1'][j],
                                     w['rw_a2'][j], w['rw_g1'][j], w['rw_g2'][j], w['rw_k_k'][j], w['rw_k_a'][j],
                                     w['rw_r_k'][j], w['rw_lnx_w'][j], w['rw_lnx_b'][j])
            o_shift.append(xp.astype(shift_in.dtype))
            o_rs.append(s.astype(rs_in.dtype))
        x = x + out.astype(x.dtype)
        x = x + sqrelu_mlp(rmsnorm(x, w['norm_ffn'][i]), w['ffn_w1'][i], w['ffn_w2'][i])
    y = rmsnorm(x, w['norm_final'])
    new = tuple(jnp.stack(z, axis=0) for z in (o_conv, o_gs, o_c, o_n, o_m, o_shift, o_rs))
    return y, new


def setup_inputs(seed: int = 0) -> dict:
    key = jax.random.key(seed)
    ks = iter(jax.random.split(key, 64))
    f32 = jnp.float32

    def normal(shape, scale=1.0):
        return scale * jax.random.normal(next(ks), shape, f32)

    def uniform(shape, lo, hi):
        return jax.random.uniform(next(ks), shape, f32, lo, hi)

    def gain(shape):
        return 1.0 + normal(shape, 0.02)

    d = D_MODEL
    inp = {}
    inp['x_prompt'] = normal((BATCH, SEQ, d))
    inp['x_sample'] = normal((DEC_BATCH, DEC_SEQ, d))
    inp['state_gdn_conv'] = normal((N_GDN_LAYERS, DEC_BATCH, CONV_W - 1, GDN_CONV_CH))
    inp['state_gdn_S'] = normal((N_GDN_LAYERS, DEC_BATCH, GDN_HEADS, GDN_DK, GDN_DV), 0.3)
    inp['state_mlstm_C'] = normal((N_MLSTM_LAYERS, DEC_BATCH, ML_HEADS, ML_DK, ML_DV), 0.3)
    inp['state_mlstm_n'] = normal((N_MLSTM_LAYERS, DEC_BATCH, ML_HEADS, ML_DK), 0.3)
    inp['state_mlstm_m'] = normal((N_MLSTM_LAYERS, DEC_BATCH, ML_HEADS), 1.0)
    inp['state_rwkv_shift'] = normal((N_RWKV_LAYERS, DEC_BATCH, d))
    inp['state_rwkv_S'] = normal((N_RWKV_LAYERS, DEC_BATCH, RW_HEADS, RW_HEAD, RW_HEAD), 0.3)
    inp['norm_mix'] = gain((DEPTH, d))
    inp['norm_ffn'] = gain((DEPTH, d))
    inp['norm_final'] = gain((d,))
    inp['gdn_w_in'] = normal((N_GDN_LAYERS, d, GDN_IN), d ** -0.5)
    inp['gdn_conv_w'] = normal((N_GDN_LAYERS, GDN_CONV_CH, CONV_W), CONV_W ** -0.5)
    inp['gdn_a_log'] = jnp.log(uniform((N_GDN_LAYERS, GDN_HEADS), 1.0, 16.0))
    dt = jnp.exp(uniform((N_GDN_LAYERS, GDN_HEADS), math.log(1e-3), math.log(1e-1)))
    inp['gdn_dt_bias'] = dt + jnp.log(-jnp.expm1(-dt))
    inp['gdn_norm_w'] = gain((N_GDN_LAYERS, GDN_DV))
    inp['gdn_w_out'] = normal((N_GDN_LAYERS, GDN_VAL, d), GDN_VAL ** -0.5)
    inp['ml_w_in'] = normal((N_MLSTM_LAYERS, d, ML_IN), d ** -0.5)
    inp['ml_b_if'] = jnp.concatenate([normal((N_MLSTM_LAYERS, ML_HEADS), 0.5),
                                      uniform((N_MLSTM_LAYERS, ML_HEADS), 3.0, 6.0)], axis=-1)
    inp['ml_norm_w'] = gain((N_MLSTM_LAYERS, d))
    inp['ml_w_out'] = normal((N_MLSTM_LAYERS, d, d), d ** -0.5)
    inp['rw_mu'] = uniform((N_RWKV_LAYERS, 6, d), 0.0, 1.0)
    inp['rw_w_rkv'] = normal((N_RWKV_LAYERS, 3, d, d), d ** -0.5)
    inp['rw_w_o'] = normal((N_RWKV_LAYERS, d, d), d ** -0.5)
    inp['rw_w0'] = uniform((N_RWKV_LAYERS, d), -6.5, -1.5)
    inp['rw_w1'] = normal((N_RWKV_LAYERS, d, RW_DECAY_LORA), d ** -0.5)
    inp['rw_w2'] = normal((N_RWKV_LAYERS, RW_DECAY_LORA, d), 0.1 * RW_DECAY_LORA ** -0.5)
    inp['rw_a0'] = normal((N_RWKV_LAYERS, d), 0.1)
    inp['rw_a1'] = normal((N_RWKV_LAYERS, d, RW_A_LORA), d ** -0.5)
    inp['rw_a2'] = normal((N_RWKV_LAYERS, RW_A_LORA, d), 0.1 * RW_A_LORA ** -0.5)
    inp['rw_g1'] = normal((N_RWKV_LAYERS, d, RW_GATE_LORA), d ** -0.5)
    inp['rw_g2'] = normal((N_RWKV_LAYERS, RW_GATE_LORA, d), RW_GATE_LORA ** -0.5)
    inp['rw_k_k'] = 0.85 + normal((N_RWKV_LAYERS, d), 0.02)
    inp['rw_k_a'] = 1.0 + normal((N_RWKV_LAYERS, d), 0.02)
    inp['rw_r_k'] = normal((N_RWKV_LAYERS, RW_HEADS, RW_HEAD), 0.1)
    inp['rw_lnx_w'] = gain((N_RWKV_LAYERS, d))
    inp['rw_lnx_b'] = normal((N_RWKV_LAYERS, d), 0.02)
    inp['ffn_w1'] = normal((DEPTH, d, D_FF), d ** -0.5)
    inp['ffn_w2'] = normal((DEPTH, D_FF, d), D_FF ** -0.5)
    return inp


def reference(x_prompt, x_sample, state_gdn_conv, state_gdn_S, state_mlstm_C, state_mlstm_n, state_mlstm_m,
              state_rwkv_shift, state_rwkv_S, norm_mix, norm_ffn, norm_final,
              gdn_w_in, gdn_conv_w, gdn_a_log, gdn_dt_bias, gdn_norm_w, gdn_w_out,
              ml_w_in, ml_b_if, ml_norm_w, ml_w_out,
              rw_mu, rw_w_rkv, rw_w_o, rw_w0, rw_w1, rw_w2, rw_a0, rw_a1, rw_a2, rw_g1, rw_g2,
              rw_k_k, rw_k_a, rw_r_k, rw_lnx_w, rw_lnx_b, ffn_w1, ffn_w2):
    w = dict(norm_mix=norm_mix, norm_ffn=norm_ffn, norm_final=norm_final,
             gdn_w_in=gdn_w_in, gdn_conv_w=gdn_conv_w, gdn_a_log=gdn_a_log, gdn_dt_bias=gdn_dt_bias,
             gdn_norm_w=gdn_norm_w, gdn_w_out=gdn_w_out,
             ml_w_in=ml_w_in, ml_b_if=ml_b_if, ml_norm_w=ml_norm_w, ml_w_out=ml_w_out,
             rw_mu=rw_mu, rw_w_rkv=rw_w_rkv, rw_w_o=rw_w_o, rw_w0=rw_w0, rw_w1=rw_w1, rw_w2=rw_w2,
             rw_a0=rw_a0, rw_a1=rw_a1, rw_a2=rw_a2, rw_g1=rw_g1, rw_g2=rw_g2, rw_k_k=rw_k_k,
             rw_k_a=rw_k_a, rw_r_k=rw_r_k, rw_lnx_w=rw_lnx_w, rw_lnx_b=rw_lnx_b,
             ffn_w1=ffn_w1, ffn_w2=ffn_w2)
    sample_states = (state_gdn_conv, state_gdn_S, state_mlstm_C, state_mlstm_n, state_mlstm_m,
                     state_rwkv_shift, state_rwkv_S)
    n_prompt = x_prompt.shape[0]
    prompt_states = tuple(jnp.zeros((s.shape[0], n_prompt) + s.shape[2:], s.dtype) for s in sample_states)
    y_prompt, (p_conv, p_gs, p_c, p_n, p_m, p_shift, p_rs) = trunk(x_prompt, prompt_states, w)
    y_sample, (s_conv, s_gs, s_c, s_n, s_m, s_shift, s_rs) = trunk(x_sample, sample_states, w)
    return (y_prompt, y_sample, p_conv, s_conv, p_gs, s_gs, p_c, s_c, p_n, s_n, p_m, s_m,
            p_shift, s_shift, p_rs, s_rs)
```

```python
import functools
import math

import jax
import jax.numpy as jnp
from jax import lax
from jax.experimental import pallas as pl
from jax.experimental.pallas import tpu as pltpu

F32 = jnp.float32
BF16 = jnp.bfloat16

RMS_EPS = 1e-6
NEG_BIG = -1e30
GATE_CAP = 15.0
RW_GN_EPS = 64e-5
CONV_W = 4
CHUNK = 64
V7X_VMEM_LIMIT = 56 * 1024 * 1024
HI = lax.Precision.HIGHEST


def _cparams(sem):
    return pltpu.CompilerParams(dimension_semantics=sem, vmem_limit_bytes=V7X_VMEM_LIMIT)


def _dot(a, b):
    return jnp.dot(a.astype(BF16), b.astype(BF16), preferred_element_type=F32)


def _dot_nt(a, b):
    return lax.dot_general(a.astype(BF16), b.astype(BF16), (((1,), (1,)), ((), ())),
                           preferred_element_type=F32)


def _dot_tn(a, b):
    return lax.dot_general(a.astype(BF16), b.astype(BF16), (((0,), (0,)), ((), ())),
                           preferred_element_type=F32)


def _dot_hi(a, b):
    return jnp.dot(a, b, preferred_element_type=F32, precision=HI)


def _sigmoid(x):
    return 1.0 / (1.0 + jnp.exp(-x))


def _silu(x):
    return x * _sigmoid(x)


def _softplus(x):
    return jnp.maximum(x, 0.0) + jnp.log(1.0 + jnp.exp(-jnp.abs(x)))


def _log_sigmoid(x):
    return -_softplus(-x)


def _tri_masks(l):
    r = lax.broadcasted_iota(jnp.int32, (l, l), 0)
    c = lax.broadcasted_iota(jnp.int32, (l, l), 1)
    return r >= c, r > c


def _unit_lower_inverse(a, l):
    r = lax.broadcasted_iota(jnp.int32, (l, l), 0)
    c = lax.broadcasted_iota(jnp.int32, (l, l), 1)
    eye = (r == c).astype(F32)
    t = eye - a
    p = _dot_hi(a, a)
    n = 2
    while n < l:
        t = _dot_hi(t, eye + p)
        n *= 2
        if n < l:
            p = _dot_hi(p, p)
    return t


def _nmm_kernel(*refs, norm, residual):
    x_ref, g_ref, w_ref = refs[:3]
    res_ref = refs[3] if residual else None
    o_ref, xn_ref = refs[-2:]

    @pl.when(pl.program_id(1) == 0)
    def _():
        x = x_ref[...]
        if norm:
            x = x * lax.rsqrt(jnp.mean(x * x, axis=-1, keepdims=True) + RMS_EPS) * g_ref[...]
        xn_ref[...] = x.astype(BF16)

    y = jnp.dot(xn_ref[...], w_ref[...], preferred_element_type=F32)
    if residual:
        y = res_ref[...] + y
    o_ref[...] = y


def _nmm(x, g, w, res=None, *, norm, tm, tn):
    m, k = x.shape
    n = w.shape[1]
    tm, tn = min(tm, m), min(tn, n)
    assert m % tm == 0 and n % tn == 0
    in_specs = [pl.BlockSpec((tm, k), lambda i, j: (i, 0)),
                pl.BlockSpec((1, k), lambda i, j: (0, 0)),
                pl.BlockSpec((k, tn), lambda i, j: (0, j))]
    args = [x, g, w]
    if res is not None:
        in_specs.append(pl.BlockSpec((tm, tn), lambda i, j: (i, j)))
        args.append(res)
    return pl.pallas_call(
        functools.partial(_nmm_kernel, norm=norm, residual=res is not None),
        grid=(m // tm, n // tn),
        in_specs=in_specs,
        out_specs=pl.BlockSpec((tm, tn), lambda i, j: (i, j)),
        out_shape=jax.ShapeDtypeStruct((m, n), F32),
        scratch_shapes=[pltpu.VMEM((tm, k), BF16)],
        compiler_params=_cparams(("parallel", "arbitrary")),
        name="nmm",
    )(*args)


def _ffn_kernel(x_ref, g_ref, w1_ref, w2_ref, o_ref, xn_ref, acc_ref):
    f = pl.program_id(1)

    @pl.when(f == 0)
    def _():
        x = x_ref[...]
        x = x * lax.rsqrt(jnp.mean(x * x, axis=-1, keepdims=True) + RMS_EPS) * g_ref[...]
        xn_ref[...] = x.astype(BF16)
        acc_ref[...] = jnp.zeros_like(acc_ref)

    h = jnp.dot(xn_ref[...], w1_ref[...], preferred_element_type=F32)
    a = jnp.square(jnp.maximum(h, 0.0)).astype(BF16)
    acc_ref[...] += jnp.dot(a, w2_ref[...], preferred_element_type=F32)

    @pl.when(f == pl.num_programs(1) - 1)
    def _():
        o_ref[...] = x_ref[...] + acc_ref[...]


def _ffn(x, g, w1, w2, *, tm, tf):
    m, d = x.shape
    dff = w1.shape[1]
    tm, tf = min(tm, m), min(tf, dff)
    assert m % tm == 0 and dff % tf == 0
    return pl.pallas_call(
        _ffn_kernel,
        grid=(m // tm, dff // tf),
        in_specs=[pl.BlockSpec((tm, d), lambda i, j: (i, 0)),
                  pl.BlockSpec((1, d), lambda i, j: (0, 0)),
                  pl.BlockSpec((d, tf), lambda i, j: (0, j)),
                  pl.BlockSpec((tf, d), lambda i, j: (j, 0))],
        out_specs=pl.BlockSpec((tm, d), lambda i, j: (i, 0)),
        out_shape=jax.ShapeDtypeStruct((m, d), F32),
        scratch_shapes=[pltpu.VMEM((tm, d), BF16), pltpu.VMEM((tm, d), F32)],
        compiler_params=_cparams(("parallel", "arbitrary")),
        name="ffn",
    )(x, g, w1, w2)


def _norm_kernel(x_ref, g_ref, o_ref):
    x = x_ref[...]
    o_ref[...] = x * lax.rsqrt(jnp.mean(x * x, axis=-1, keepdims=True) + RMS_EPS) * g_ref[...]


def _rmsnorm(x, g, *, tm):
    m, d = x.shape
    tm = min(tm, m)
    assert m % tm == 0
    return pl.pallas_call(
        _norm_kernel,
        grid=(m // tm,),
        in_specs=[pl.BlockSpec((tm, d), lambda i: (i, 0)), pl.BlockSpec((1, d), lambda i: (0, 0))],
        out_specs=pl.BlockSpec((tm, d), lambda i: (i, 0)),
        out_shape=jax.ShapeDtypeStruct((m, d), F32),
        compiler_params=_cparams(("parallel",)),
        name="rmsnorm",
    )(x, g)


def _gdn_prompt_kernel(pm_ref, pba_ref, conv0_ref, s0_ref, cw_ref, gp_ref, nw_ref,
                       o_ref, conv_out_ref, s_out_ref,
                       full_ref, qkv_ref, s_ref, *, tb, heads, dk, dv):
    t = pl.program_id(1)
    key = heads * dk
    ch = 2 * key + heads * dv
    l = CHUNK

    @pl.when(t == 0)
    def _():
        full_ref[0:8, :] = conv0_ref[0]
        s_ref[...] = s0_ref[0]

    full_ref[8:8 + tb, :] = pm_ref[:, 0:ch]
    for c in range(ch // 128):
        cs = slice(c * 128, (c + 1) * 128)
        y = full_ref[5:5 + tb, cs] * cw_ref[0:1, cs]
        for j in range(1, CONV_W):
            y = y + full_ref[5 + j:5 + j + tb, cs] * cw_ref[j:j + 1, cs]
        y = _silu(y)
        if c * 128 < key:
            y = y * lax.rsqrt(jnp.sum(y * y, axis=-1, keepdims=True) + 1e-6) * (dk ** -0.5)
        elif c * 128 < 2 * key:
            y = y * lax.rsqrt(jnp.sum(y * y, axis=-1, keepdims=True) + 1e-6)
        qkv_ref[:, cs] = y
    full_ref[0:8, :] = full_ref[tb:tb + 8, :]

    incl, strict = _tri_masks(l)
    tril = incl.astype(F32)
    a_log = gp_ref[0:1, :]
    dt_bias = gp_ref[1:2, :]

    def chunk(ci, carry):
        r0 = pl.multiple_of(ci * l, l)
        rows = pl.ds(r0, l)
        ba = pba_ref[rows, :]
        beta_all = _sigmoid(ba)
        g_all = -jnp.exp(a_log) * _softplus(ba + dt_bias)
        gc = _dot_hi(tril, g_all)
        gc_t = gc.T
        for h in range(heads):
            b_col = beta_all[:, h:h + 1]
            gi = gc[:, heads + h:heads + h + 1]
            gj = gc_t[heads + h:heads + h + 1, :]
            g_last = gc[l - 1:l, heads + h:heads + h + 1]
            q = qkv_ref[rows, h * dk:(h + 1) * dk]
            k = qkv_ref[rows, key + h * dk:key + (h + 1) * dk]
            v = qkv_ref[rows, 2 * key + h * dv:2 * key + (h + 1) * dv]
            dmat = jnp.where(incl, jnp.exp(jnp.where(incl, gi - gj, 0.0)), 0.0)
            kb = k * b_col
            a_mat = jnp.where(strict, _dot_nt(kb, k) * dmat, 0.0)
            t_inv = _unit_lower_inverse(a_mat, l)
            egi = jnp.exp(gi)
            rhs = jnp.concatenate([v * b_col, kb * egi], axis=-1)
            sol = _dot(t_inv, rhs)
            u, w = sol[:, :dv], sol[:, dv:]
            qk = jnp.where(incl, _dot_nt(q, k) * dmat, 0.0)
            q_dec = q * egi
            k_dec = k * jnp.exp(g_last - gi)
            s = s_ref[h]
            v_new = u - _dot(w, s)
            o = _dot(q_dec, s) + _dot(qk, v_new)
            s_ref[h] = s * jnp.exp(g_last) + _dot_tn(k_dec, v_new)
            z = pm_ref[rows, ch + h * dv:ch + (h + 1) * dv]
            o = o * lax.rsqrt(jnp.mean(o * o, axis=-1, keepdims=True) + RMS_EPS) * nw_ref[...]
            o_ref[rows, h * dv:(h + 1) * dv] = o * _silu(z)
        return carry

    lax.fori_loop(0, tb // l, chunk, 0)

    @pl.when(t == pl.num_programs(1) - 1)
    def _():
        conv_out_ref[0] = full_ref[0:8, :]
        s_out_ref[0] = s_ref[...]


def _gdn_prompt(pm, pba, conv0, s0, cw_t, gp, nw, *, batch, seq, heads, dk, dv, tb):
    key, val = heads * dk, heads * dv
    ch = 2 * key + val
    tb = min(tb, seq)
    assert seq % tb == 0 and tb % CHUNK == 0
    nt = seq // tb
    return pl.pallas_call(
        functools.partial(_gdn_prompt_kernel, tb=tb, heads=heads, dk=dk, dv=dv),
        grid=(batch, nt),
        in_specs=[pl.BlockSpec((tb, ch + val), lambda b, t: (b * nt + t, 0)),
                  pl.BlockSpec((tb, 128), lambda b, t: (b * nt + t, 0)),
                  pl.BlockSpec((1, 8, ch), lambda b, t: (b, 0, 0)),
                  pl.BlockSpec((1, heads, dk, dv), lambda b, t: (b, 0, 0, 0)),
                  pl.BlockSpec((8, ch), lambda b, t: (0, 0)),
                  pl.BlockSpec((8, 128), lambda b, t: (0, 0)),
                  pl.BlockSpec((1, dv), lambda b, t: (0, 0))],
        out_specs=[pl.BlockSpec((tb, val), lambda b, t: (b * nt + t, 0)),
                   pl.BlockSpec((1, 8, ch), lambda b, t: (b, 0, 0)),
                   pl.BlockSpec((1, heads, dk, dv), lambda b, t: (b, 0, 0, 0))],
        out_shape=[jax.ShapeDtypeStruct((batch * seq, val), F32),
                   jax.ShapeDtypeStruct((batch, 8, ch), F32),
                   jax.ShapeDtypeStruct((batch, heads, dk, dv), F32)],
        scratch_shapes=[pltpu.VMEM((tb + 8, ch), F32), pltpu.VMEM((tb, ch), F32),
                        pltpu.VMEM((heads, dk, dv), F32)],
        compiler_params=_cparams(("parallel", "arbitrary")),
        name="gdn_prompt",
    )(pm, pba, conv0, s0, cw_t, gp, nw)


def _mlstm_prompt_kernel(pm_ref, pif_ref, bif_ref, c0_ref, n0_ref, m0_ref, nw_ref,
                         o_ref, c_out_ref, n_out_ref, m_out_ref,
                         c_ref, n_ref, m_ref, *, tb, heads, dk, dv):
    t = pl.program_id(1)
    l = CHUNK
    qk_w = heads * dk
    v_off = 2 * qk_w
    o_off = v_off + heads * dv

    @pl.when(t == 0)
    def _():
        c_ref[...] = c0_ref[0]
        n_ref[...] = n0_ref[0]
        m_ref[...] = m0_ref[0]

    incl, _ = _tri_masks(l)
    tril = incl.astype(F32)

    def chunk(ci, carry):
        r0 = pl.multiple_of(ci * l, l)
        rows = pl.ds(r0, l)
        gates = pif_ref[rows, :] + bif_ref[...]
        gates = GATE_CAP * jnp.tanh(gates / GATE_CAP)
        bcum = _dot_hi(tril, _log_sigmoid(gates))
        bcum_t = bcum.T
        gates_t = gates.T
        for h in range(heads):
            bi = bcum[:, heads + h:heads + h + 1]
            bj = bcum_t[heads + h:heads + h + 1, :]
            ii = gates[:, h:h + 1]
            ij = gates_t[h:h + 1, :]
            b_last = bcum[l - 1:l, heads + h:heads + h + 1]
            q = pm_ref[rows, h * dk:(h + 1) * dk] * (dk ** -0.5)
            k = pm_ref[rows, qk_w + h * dk:qk_w + (h + 1) * dk]
            v = pm_ref[rows, v_off + h * dv:v_off + (h + 1) * dv]
            dlog = jnp.where(incl, bi - bj + ij, NEG_BIG)
            m_intra = jnp.max(dlog, axis=-1, keepdims=True)
            p = jnp.where(incl, jnp.exp(dlog - m_intra), 0.0) * _dot_nt(q, k)
            num_intra = _dot(p, v)
            den_intra = jnp.sum(p, axis=-1, keepdims=True)
            a_log = b_last - bi + ii
            m_chunk = jnp.max(a_log, axis=0, keepdims=True)
            kw = k * jnp.exp(a_log - m_chunk)
            kv_chunk = _dot_tn(kw, v)
            k_chunk = jnp.sum(kw, axis=0, keepdims=True)
            m_prev = m_ref[h:h + 1, 0:1]
            c_mat = c_ref[h]
            n_vec = n_ref[h:h + 1, :]
            m_t = jnp.maximum(bi + m_prev, m_intra)
            s_inter = jnp.exp(bi + m_prev - m_t)
            s_intra = jnp.exp(m_intra - m_t)
            num = s_inter * _dot(q, c_mat) + s_intra * num_intra
            den = s_inter * jnp.sum(q * n_vec, axis=-1, keepdims=True) + s_intra * den_intra
            h_t = num / jnp.maximum(jnp.abs(den), jnp.exp(-m_t))
            m_new = jnp.maximum(b_last + m_prev, m_chunk)
            f_s = jnp.exp(b_last + m_prev - m_new)
            i_s = jnp.exp(m_chunk - m_new)
            c_ref[h] = f_s * c_mat + i_s * kv_chunk
            n_ref[h:h + 1, :] = f_s * n_vec + i_s * k_chunk
            m_ref[h:h + 1, :] = jnp.broadcast_to(m_new, (1, m_ref.shape[1]))
            h_n = h_t * lax.rsqrt(jnp.mean(h_t * h_t, axis=-1, keepdims=True) + RMS_EPS)
            h_n = h_n * nw_ref[:, h * dv:(h + 1) * dv]
            o_pre = pm_ref[rows, o_off + h * dv:o_off + (h + 1) * dv]
            o_ref[rows, h * dv:(h + 1) * dv] = _sigmoid(o_pre) * h_n
        return carry

    lax.fori_loop(0, tb // l, chunk, 0)

    @pl.when(t == pl.num_programs(1) - 1)
    def _():
        c_out_ref[0] = c_ref[...]
        n_out_ref[0] = n_ref[...]
        m_out_ref[0] = m_ref[...]


def _mlstm_prompt(pm, pif, bif, c0, n0, m0, nw, *, batch, seq, heads, dk, dv, tb):
    width = pm.shape[1]
    val = heads * dv
    tb = min(tb, seq)
    assert seq % tb == 0 and tb % CHUNK == 0
    nt = seq // tb
    return pl.pallas_call(
        functools.partial(_mlstm_prompt_kernel, tb=tb, heads=heads, dk=dk, dv=dv),
        grid=(batch, nt),
        in_specs=[pl.BlockSpec((tb, width), lambda b, t: (b * nt + t, 0)),
                  pl.BlockSpec((tb, 128), lambda b, t: (b * nt + t, 0)),
                  pl.BlockSpec((1, 128), lambda b, t: (0, 0)),
                  pl.BlockSpec((1, heads, dk, dv), lambda b, t: (b, 0, 0, 0)),
                  pl.BlockSpec((1, 8, dk), lambda b, t: (b, 0, 0)),
                  pl.BlockSpec((1, 8, 128), lambda b, t: (b, 0, 0)),
                  pl.BlockSpec((1, val), lambda b, t: (0, 0))],
        out_specs=[pl.BlockSpec((tb, val), lambda b, t: (b * nt + t, 0)),
                   pl.BlockSpec((1, heads, dk, dv), lambda b, t: (b, 0, 0, 0)),
                   pl.BlockSpec((1, 8, dk), lambda b, t: (b, 0, 0)),
                   pl.BlockSpec((1, 8, 128), lambda b, t: (b, 0, 0))],
        out_shape=[jax.ShapeDtypeStruct((batch * seq, val), F32),
                   jax.ShapeDtypeStruct((batch, heads, dk, dv), F32),
                   jax.ShapeDtypeStruct((batch, 8, dk), F32),
                   jax.ShapeDtypeStruct((batch, 8, 128), F32)],
        scratch_shapes=[pltpu.VMEM((heads, dk, dv), F32), pltpu.VMEM((8, dk), F32),
                        pltpu.VMEM((8, 128), F32)],
        compiler_params=_cparams(("parallel", "arbitrary")),
        name="mlstm_prompt",
    )(pm, pif, bif, c0, n0, m0, nw)


def _mlstm_prep(w_in, b_if, norm_w, w_out, *, heads, dk, dv):
    main = 2 * heads * dk + 2 * heads * dv
    return dict(w_main=w_in[:, :main].astype(BF16),
                w_if=_pad_cols(w_in[:, main:], 128).astype(BF16),
                bif=_pad_cols(b_if[None, :], 128), nw=norm_w[None, :], w_out=w_out.astype(BF16))


def _mlstm_prompt_layer(x, g, p, c0, n0, m0, *, batch, seq, heads, dk, dv, norm=True, residual=True):
    pm = _nmm(x, g, p["w_main"], norm=norm, tm=512, tn=512)
    pif = _nmm(x, g, p["w_if"], norm=norm, tm=512, tn=128)
    n0p = jnp.pad(n0, ((0, 0), (0, 8 - heads), (0, 0)))
    m0p = jnp.broadcast_to(jnp.pad(m0, ((0, 0), (0, 8 - heads)))[:, :, None], (batch, 8, 128))
    o, c, n, m = _mlstm_prompt(pm, pif, p["bif"], c0, n0p, m0p, p["nw"],
                               batch=batch, seq=seq, heads=heads, dk=dk, dv=dv, tb=256)
    ones = jnp.ones((1, o.shape[1]), F32)
    y = _nmm(o, ones, p["w_out"], x if residual else None, norm=False, tm=512, tn=512)
    return y, c, n[:, :heads, :], m[:, :heads, 0]


def _rwkv_proj_kernel(h_ref, prev_ref, mu_ref, wrkv_ref, w1_ref, w2_ref, a1_ref, a2_ref, g1_ref, g2_ref,
                      vec_ref, r_ref, k_ref, v_ref, lw_ref, kk_ref, a_ref, g_ref):
    h = h_ref[...]
    xx = prev_ref[...] - h

    def mix(j):
        return (h + xx * mu_ref[j:j + 1, :]).astype(BF16)

    w0, a0, k_k, k_a = (vec_ref[j:j + 1, :] for j in range(4))
    r_ref[...] = jnp.dot(mix(0), wrkv_ref[0], preferred_element_type=F32)
    lora_w = _dot(jnp.tanh(_dot(mix(1), w1_ref[...])), w2_ref[...])
    w_log = -_softplus(-(w0 + lora_w)) - 0.5
    lw_ref[...] = -jnp.exp(w_log)
    k = jnp.dot(mix(2), wrkv_ref[1], preferred_element_type=F32)
    v_ref[...] = jnp.dot(mix(3), wrkv_ref[2], preferred_element_type=F32)
    a = _sigmoid(a0 + _dot(_dot(mix(4), a1_ref[...]), a2_ref[...]))
    g_ref[...] = _dot(_sigmoid(_dot(mix(5), g1_ref[...])), g2_ref[...])
    kk_ref[...] = k * k_k
    k_ref[...] = k * (1.0 + (a - 1.0) * k_a)
    a_ref[...] = a


def _rwkv_proj(hn, prev, p, *, tm):
    m, d = hn.shape
    tm = min(tm, m)
    assert m % tm == 0
    row = pl.BlockSpec((tm, d), lambda i: (i, 0))

    def full(a):
        nd = a.ndim
        return pl.BlockSpec(a.shape, lambda i: (0,) * nd)

    consts = [p["mu"], p["w_rkv"], p["w1"], p["w2"], p["a1"], p["a2"], p["g1"], p["g2"], p["vec"]]
    return pl.pallas_call(
        _rwkv_proj_kernel,
        grid=(m // tm,),
        in_specs=[row, row] + [full(a) for a in consts],
        out_specs=[row] * 7,
        out_shape=[jax.ShapeDtypeStruct((m, d), F32)] * 7,
        compiler_params=_cparams(("parallel",)),
        name="rwkv_proj",
    )(hn, prev, *consts)


def _rwkv_post(y, r, k, v, g, r_k, lnx_w, lnx_b):
    mean = jnp.mean(y, axis=-1, keepdims=True)
    yc = y - mean
    var = jnp.mean(yc * yc, axis=-1, keepdims=True)
    y = yc * lax.rsqrt(var + RW_GN_EPS) * lnx_w + lnx_b
    y = y + jnp.sum(r * k * r_k, axis=-1, keepdims=True) * v
    return y * g


def _rwkv_prompt_kernel(r_ref, k_ref, v_ref, lw_ref, kk_ref, a_ref, g_ref, s0_ref, hp_ref,
                        o_ref, s_out_ref, s_ref, *, tb, heads, hd):
    t = pl.program_id(1)
    l = CHUNK

    @pl.when(t == 0)
    def _():
        s_ref[...] = s0_ref[0]

    incl, strict = _tri_masks(l)
    tril = incl.astype(F32)

    def chunk(ci, carry):
        r0 = pl.multiple_of(ci * l, l)
        rows = pl.ds(r0, l)
        lw = lw_ref[rows, :]
        lwc = _dot_hi(tril, lw)
        lw_last = lwc[l - 1:l, :]
        e_in = jnp.exp(lwc)
        e_prev = jnp.exp(lwc - lw)
        e_neg = jnp.exp(-lwc)
        e_end = jnp.exp(lw_last - lwc)
        e_last = jnp.exp(lw_last)
        for h in range(heads):
            hs = slice(h * hd, (h + 1) * hd)
            r = r_ref[rows, hs]
            k = k_ref[rows, hs]
            v = v_ref[rows, hs]
            kk = kk_ref[rows, hs]
            kk = kk * lax.rsqrt(jnp.maximum(jnp.sum(kk * kk, axis=-1, keepdims=True), 1e-24))
            bv = kk * a_ref[rows, hs]
            a_t = -kk * e_prev[:, hs]
            r_t = r * e_in[:, hs]
            b_t = bv * e_neg[:, hs]
            k_t = k * e_neg[:, hs]
            ab = jnp.where(strict, _dot_nt(a_t, b_t), 0.0)
            ak = jnp.where(strict, _dot_nt(a_t, k_t), 0.0)
            rb = jnp.where(incl, _dot_nt(r_t, b_t), 0.0)
            rk = jnp.where(incl, _dot_nt(r_t, k_t), 0.0)
            t_inv = _unit_lower_inverse(-ab, l)
            s = s_ref[h]
            u = _dot(t_inv, _dot_nt(a_t, s) + _dot(ak, v))
            y = _dot_nt(r_t, s) + _dot(rb, u) + _dot(rk, v)
            s_ref[h] = s * e_last[:, hs] + _dot_tn(u, bv * e_end[:, hs]) + _dot_tn(v, k * e_end[:, hs])
            o_ref[rows, hs] = _rwkv_post(y, r, k, v, g_ref[rows, hs],
                                         hp_ref[0:1, hs], hp_ref[1:2, hs], hp_ref[2:3, hs])
        return carry

    lax.fori_loop(0, tb // l, chunk, 0)

    @pl.when(t == pl.num_programs(1) - 1)
    def _():
        s_out_ref[0] = s_ref[...]


def _rwkv_prompt(r, k, v, lw, kk, a, g, s0, hp, *, batch, seq, heads, hd, tb):
    d = heads * hd
    tb = min(tb, seq)
    assert seq % tb == 0 and tb % CHUNK == 0
    nt = seq // tb
    row = pl.BlockSpec((tb, d), lambda b, t: (b * nt + t, 0))
    st = pl.BlockSpec((1, heads, hd, hd), lambda b, t: (b, 0, 0, 0))
    return pl.pallas_call(
        functools.partial(_rwkv_prompt_kernel, tb=tb, heads=heads, hd=hd),
        grid=(batch, nt),
        in_specs=[row] * 7 + [st, pl.BlockSpec((8, d), lambda b, t: (0, 0))],
        out_specs=[row, st],
        out_shape=[jax.ShapeDtypeStruct((batch * seq, d), F32),
                   jax.ShapeDtypeStruct((batch, heads, hd, hd), F32)],
        scratch_shapes=[pltpu.VMEM((heads, hd, hd), F32)],
        compiler_params=_cparams(("parallel", "arbitrary")),
        name="rwkv_prompt",
    )(r, k, v, lw, kk, a, g, s0, hp)


def _pad_rows(a, n):
    return jnp.pad(a, ((0, n - a.shape[0]), (0, 0)))


def _rwkv_prep(mu, w_rkv, w_o, w0, w1, w2, a0, a1, a2, g1, g2, k_k, k_a, r_k, lnx_w, lnx_b):
    d = w0.shape[0]
    lw = -(-w1.shape[1] // 128) * 128
    la = -(-a1.shape[1] // 128) * 128
    lg = -(-g1.shape[1] // 128) * 128
    return dict(mu=_pad_rows(mu, 8), w_rkv=w_rkv.astype(BF16), w_o=w_o.astype(BF16),
                w1=_pad_cols(w1, lw).astype(BF16), w2=_pad_rows(w2, lw).astype(BF16),
                a1=_pad_cols(a1, la).astype(BF16), a2=_pad_rows(a2, la).astype(BF16),
                g1=_pad_cols(g1, lg).astype(BF16), g2=_pad_rows(g2, lg).astype(BF16),
                vec=_pad_rows(jnp.stack([w0, a0, k_k, k_a]), 8),
                hp=_pad_rows(jnp.stack([r_k.reshape(d), lnx_w, lnx_b]), 8))


def _rwkv_prompt_layer(x, g_norm, p, shift0, s0, *, batch, seq, heads, hd, norm=True, residual=True):
    d = x.shape[1]
    hn = _rmsnorm(x, g_norm, tm=1024) if norm else x
    hn3 = hn.reshape(batch, seq, d)
    prev = jnp.concatenate([shift0[:, None, :], hn3[:, :-1]], axis=1).reshape(batch * seq, d)
    r, k, v, lw, kk, a, g = _rwkv_proj(hn, prev, p, tm=256)
    o, s = _rwkv_prompt(r, k, v, lw, kk, a, g, s0, p["hp"], batch=batch, seq=seq, heads=heads, hd=hd, tb=256)
    ones = jnp.ones((1, d), F32)
    y = _nmm(o, ones, p["w_o"], x if residual else None, norm=False, tm=512, tn=512)
    return y, hn3[:, -1], s


def _gdn_sample_pre_kernel(pm_ref, pba_ref, conv_ref, cw_ref, gp_ref, qkv_ref, conv_out_ref, sc_ref,
                           *, heads, dk, dv):
    key = heads * dk
    ch = 2 * key + heads * dv
    u = pm_ref[:, 0:ch]
    y = u * cw_ref[CONV_W - 1:CONV_W, :]
    for j in range(CONV_W - 1):
        y = y + conv_ref[j] * cw_ref[j:j + 1, :]
        conv_out_ref[j] = conv_ref[j + 1] if j + 1 < CONV_W - 1 else u
    y = _silu(y)
    for c in range(ch // 128):
        cs = slice(c * 128, (c + 1) * 128)
        yc = y[:, cs]
        if c * 128 < key:
            yc = yc * lax.rsqrt(jnp.sum(yc * yc, axis=-1, keepdims=True) + 1e-6) * (dk ** -0.5)
        elif c * 128 < 2 * key:
            yc = yc * lax.rsqrt(jnp.sum(yc * yc, axis=-1, keepdims=True) + 1e-6)
        qkv_ref[:, cs] = yc
    ba = pba_ref[...]
    lane = lax.broadcasted_iota(jnp.int32, ba.shape, 1)
    g = -jnp.exp(gp_ref[0:1, :]) * _softplus(ba + gp_ref[1:2, :])
    sc_ref[...] = jnp.where(lane < heads, _sigmoid(ba), jnp.exp(g))


def _gdn_sample_pre(pm, pba, conv_t, cw_t, gp, *, heads, dk, dv):
    n = pm.shape[0]
    ch = 2 * heads * dk + heads * dv
    return pl.pallas_call(
        functools.partial(_gdn_sample_pre_kernel, heads=heads, dk=dk, dv=dv),
        out_shape=[jax.ShapeDtypeStruct((n, ch), F32),
                   jax.ShapeDtypeStruct((CONV_W - 1, n, ch), F32),
                   jax.ShapeDtypeStruct((n, 128), F32)],
        compiler_params=pltpu.CompilerParams(vmem_limit_bytes=V7X_VMEM_LIMIT),
        name="gdn_sample_pre",
    )(pm, pba, conv_t, cw_t, gp)


def _gdn_sample_step_kernel(s0_ref, cols_ref, v_ref, z_ref, sc_ref, nw_ref, s_out_ref, o_ref, *, heads):
    for h in range(heads):
        kc = cols_ref[0, :, h:h + 1]
        qc = cols_ref[0, :, heads + h:heads + h + 1]
        s0 = s0_ref[0, h]
        beta = sc_ref[0, h:h + 1, 0:1]
        eg = sc_ref[0, h:h + 1, 1:2]
        ks = jnp.sum(kc * s0, axis=0, keepdims=True)
        s1 = eg * s0 + kc * (beta * (v_ref[0, h:h + 1, :] - eg * ks))
        s_out_ref[0, h] = s1
        o = jnp.sum(qc * s1, axis=0, keepdims=True)
        o = o * lax.rsqrt(jnp.mean(o * o, axis=-1, keepdims=True) + RMS_EPS) * nw_ref[...]
        o_ref[0, h:h + 1, :] = o * _silu(z_ref[0, h:h + 1, :])


def _gdn_sample_step(s0, cols, v, z, sc, nw):
    n, heads, dk, dv = s0.shape

    def blk(a):
        nd = a.ndim
        return pl.BlockSpec((1,) + a.shape[1:], lambda b: (b,) + (0,) * (nd - 1))

    return pl.pallas_call(
        functools.partial(_gdn_sample_step_kernel, heads=heads),
        grid=(n,),
        in_specs=[blk(s0), blk(cols), blk(v), blk(z), blk(sc), pl.BlockSpec((1, dv), lambda b: (0, 0))],
        out_specs=[blk(s0), blk(v)],
        out_shape=[jax.ShapeDtypeStruct(s0.shape, F32), jax.ShapeDtypeStruct(v.shape, F32)],
        compiler_params=_cparams(("parallel",)),
        name="gdn_sample_step",
    )(s0, cols, v, z, sc, nw)


def _gdn_sample_layer(x, g, p, conv0, s0, *, heads, dk, dv):
    n = x.shape[0]
    key, val = heads * dk, heads * dv
    ch = 2 * key + val
    pm = _nmm(x, g, p["w_main"], norm=True, tm=128, tn=1024)
    pba = _nmm(x, g, p["w_ba"], norm=True, tm=128, tn=128)
    qkv, conv_t, sc = _gdn_sample_pre(pm, pba, jnp.transpose(conv0, (1, 0, 2)), p["cw_t"], p["gp"],
                                      heads=heads, dk=dk, dv=dv)
    q_c = jnp.transpose(qkv[:, :key].reshape(n, heads, dk), (0, 2, 1))
    k_c = jnp.transpose(qkv[:, key:2 * key].reshape(n, heads, dk), (0, 2, 1))
    cols = jnp.concatenate([k_c, q_c], axis=-1)
    sc3 = jnp.stack([sc[:, :heads], sc[:, heads:2 * heads]], axis=-1)
    s1, o = _gdn_sample_step(s0, cols, qkv[:, 2 * key:].reshape(n, heads, dv),
                             pm[:, ch:].reshape(n, heads, dv), sc3, p["nw"])
    ones = jnp.ones((1, val), F32)
    y = _nmm(o.reshape(n, val), ones, p["w_out"], x, norm=False, tm=128, tn=1024)
    return y, jnp.transpose(conv_t, (1, 0, 2)), s1


def _mlstm_sample_step_kernel(c0_ref, n0_ref, cols_ref, q_ref, k_ref, v_ref, op_ref, sc_ref, bif_ref, nw_ref,
                              c_out_ref, n_out_ref, m_out_ref, o_ref, *, heads, dk):
    gi = sc_ref[0, :, 0:1] + bif_ref[:, 0:1]
    gf = sc_ref[0, :, 1:2] + bif_ref[:, 1:2]
    m0 = sc_ref[0, :, 2:3]
    gi = GATE_CAP * jnp.tanh(gi / GATE_CAP)
    logf = _log_sigmoid(GATE_CAP * jnp.tanh(gf / GATE_CAP))
    m_new = jnp.maximum(logf + m0, gi)
    f_s = jnp.exp(logf + m0 - m_new)
    i_s = jnp.exp(gi - m_new)
    m_out_ref[0] = m_new
    scale = dk ** -0.5
    n1 = f_s * n0_ref[0] + i_s * k_ref[0]
    n_out_ref[0] = n1
    den = jnp.sum(q_ref[0] * scale * n1, axis=-1, keepdims=True)
    floor = jnp.exp(-m_new)
    for h in range(heads):
        kc = cols_ref[0, :, h:h + 1]
        qc = cols_ref[0, :, heads + h:heads + h + 1] * scale
        c1 = f_s[h:h + 1, :] * c0_ref[0, h] + i_s[h:h + 1, :] * (kc * v_ref[0, h:h + 1, :])
        c_out_ref[0, h] = c1
        num = jnp.sum(qc * c1, axis=0, keepdims=True)
        h_t = num / jnp.maximum(jnp.abs(den[h:h + 1, :]), floor[h:h + 1, :])
        h_n = h_t * lax.rsqrt(jnp.mean(h_t * h_t, axis=-1, keepdims=True) + RMS_EPS) * nw_ref[h:h + 1, :]
        o_ref[0, h:h + 1, :] = _sigmoid(op_ref[0, h:h + 1, :]) * h_n


def _mlstm_sample_step(c0, n0, cols, q, k, v, o_pre, sc, bif2, nw2):
    n, heads, dk, dv = c0.shape

    def blk(a):
        nd = a.ndim
        return pl.BlockSpec((1,) + a.shape[1:], lambda b: (b,) + (0,) * (nd - 1))

    def full(a):
        nd = a.ndim
        return pl.BlockSpec(a.shape, lambda b: (0,) * nd)

    m_shape = (n, heads, 1)
    return pl.pallas_call(
        functools.partial(_mlstm_sample_step_kernel, heads=heads, dk=dk),
        grid=(n,),
        in_specs=[blk(c0), blk(n0), blk(cols), blk(q), blk(k), blk(v), blk(o_pre), blk(sc), full(bif2), full(nw2)],
        out_specs=[blk(c0), blk(n0), pl.BlockSpec((1, heads, 1), lambda b: (b, 0, 0)), blk(v)],
        out_shape=[jax.ShapeDtypeStruct(c0.shape, F32), jax.ShapeDtypeStruct(n0.shape, F32),
                   jax.ShapeDtypeStruct(m_shape, F32), jax.ShapeDtypeStruct(v.shape, F32)],
        compiler_params=_cparams(("parallel",)),
        name="mlstm_sample_step",
    )(c0, n0, cols, q, k, v, o_pre, sc, bif2, nw2)


def _mlstm_sample_layer(x, g, p, c0, n0, m0, *, heads, dk, dv):
    n = x.shape[0]
    qk_w, val = heads * dk, heads * dv
    pm = _nmm(x, g, p["w_main"], norm=True, tm=128, tn=1024)
    pif = _nmm(x, g, p["w_if"], norm=True, tm=128, tn=128)
    q = pm[:, :qk_w].reshape(n, heads, dk)
    k = pm[:, qk_w:2 * qk_w].reshape(n, heads, dk)
    v = pm[:, 2 * qk_w:2 * qk_w + val].reshape(n, heads, dv)
    o_pre = pm[:, 2 * qk_w + val:].reshape(n, heads, dv)
    cols = jnp.concatenate([jnp.transpose(k, (0, 2, 1)), jnp.transpose(q, (0, 2, 1))], axis=-1)
    sc = jnp.stack([pif[:, :heads], pif[:, heads:2 * heads], m0], axis=-1)
    bif2 = jnp.stack([p["bif"][0, :heads], p["bif"][0, heads:2 * heads]], axis=-1)
    c1, n1, m1, o = _mlstm_sample_step(c0, n0, cols, q, k, v, o_pre, sc, bif2, p["nw"].reshape(heads, dv))
    ones = jnp.ones((1, val), F32)
    y = _nmm(o.reshape(n, val), ones, p["w_out"], x, norm=False, tm=128, tn=1024)
    return y, c1, n1, m1[:, :, 0]


def _rwkv_sample_step_kernel(s0_ref, r_ref, k_ref, lw_ref, kk_ref, a_ref, vc_ref, gc_ref, hpc_ref, hpr_ref,
                             s_out_ref, o_ref, *, heads):
    r = r_ref[0]
    k = k_ref[0]
    kk = kk_ref[0]
    kk = kk * lax.rsqrt(jnp.maximum(jnp.sum(kk * kk, axis=-1, keepdims=True), 1e-24))
    bv = kk * a_ref[0]
    w = jnp.exp(lw_ref[0])
    bonus = jnp.sum(r * k * hpr_ref[...], axis=-1, keepdims=True)
    for h in range(heads):
        s0 = s0_ref[0, h]
        vc = vc_ref[0, :, h:h + 1]
        sa = jnp.sum(s0 * (-kk[h:h + 1, :]), axis=-1, keepdims=True)
        s1 = s0 * w[h:h + 1, :] + sa * bv[h:h + 1, :] + vc * k[h:h + 1, :]
        s_out_ref[0, h] = s1
        y = jnp.sum(s1 * r[h:h + 1, :], axis=-1, keepdims=True)
        mean = jnp.mean(y, axis=0, keepdims=True)
        yc = y - mean
        var = jnp.mean(yc * yc, axis=0, keepdims=True)
        y = yc * lax.rsqrt(var + RW_GN_EPS) * hpc_ref[0, :, h:h + 1] + hpc_ref[1, :, h:h + 1]
        y = y + bonus[h:h + 1, :] * vc
        o_ref[0, :, h:h + 1] = y * gc_ref[0, :, h:h + 1]


def _rwkv_sample_step(s0, r, k, lw, kk, a, v_c, g_c, hp_c, rk):
    n, heads, hd, _ = s0.shape

    def blk(a_):
        nd = a_.ndim
        return pl.BlockSpec((1,) + a_.shape[1:], lambda b: (b,) + (0,) * (nd - 1))

    def full(a_):
        nd = a_.ndim
        return pl.BlockSpec(a_.shape, lambda b: (0,) * nd)

    return pl.pallas_call(
        functools.partial(_rwkv_sample_step_kernel, heads=heads),
        grid=(n,),
        in_specs=[blk(s0), blk(r), blk(k), blk(lw), blk(kk), blk(a), blk(v_c), blk(g_c), full(hp_c), full(rk)],
        out_specs=[blk(s0), blk(v_c)],
        out_shape=[jax.ShapeDtypeStruct(s0.shape, F32), jax.ShapeDtypeStruct(v_c.shape, F32)],
        compiler_params=_cparams(("parallel",)),
        name="rwkv_sample_step",
    )(s0, r, k, lw, kk, a, v_c, g_c, hp_c, rk)


def _rwkv_sample_layer(x, g_norm, p, shift0, s0, *, heads, hd):
    n, d = x.shape
    hn = _rmsnorm(x, g_norm, tm=128)
    r, k, v, lw, kk, a, g = _rwkv_proj(hn, shift0, p, tm=128)

    def rows(z):
        return z.reshape(n, heads, hd)

    def cols(z):
        return jnp.transpose(z.reshape(n, heads, hd), (0, 2, 1))

    hp = p["hp"]
    hp_c = jnp.stack([hp[1].reshape(heads, hd).T, hp[2].reshape(heads, hd).T])
    s1, o_c = _rwkv_sample_step(s0, rows(r), rows(k), rows(lw), rows(kk), rows(a), cols(v), cols(g),
                                hp_c, hp[0].reshape(heads, hd))
    o = jnp.transpose(o_c, (0, 2, 1)).reshape(n, d)
    ones = jnp.ones((1, d), F32)
    y = _nmm(o, ones, p["w_o"], x, norm=False, tm=128, tn=1024)
    return y, hn, s1


def _pad_cols(a, n):
    return jnp.pad(a, ((0, 0), (0, n - a.shape[1])))


def _gdn_prep(w_in, conv_w, a_log, dt_bias, norm_w, w_out, *, heads, dk, dv):
    key, val = heads * dk, heads * dv
    ch = 2 * key + val
    main = ch + val
    gp = jnp.zeros((8, 128), F32)
    gp = gp.at[0, heads:2 * heads].set(a_log).at[1, heads:2 * heads].set(dt_bias)
    return dict(w_main=w_in[:, :main].astype(BF16),
                w_ba=_pad_cols(w_in[:, main:], 128).astype(BF16),
                cw_t=jnp.pad(conv_w.T, ((0, 8 - CONV_W), (0, 0))),
                gp=gp, nw=norm_w[None, :], w_out=w_out.astype(BF16))


def _gdn_prompt_layer(x, g, p, conv0, s0, *, batch, seq, heads, dk, dv, norm=True, residual=True):
    pm = _nmm(x, g, p["w_main"], norm=norm, tm=512, tn=512)
    pba = _nmm(x, g, p["w_ba"], norm=norm, tm=512, tn=128)
    conv0 = jnp.pad(conv0, ((0, 0), (8 - (CONV_W - 1), 0), (0, 0)))
    o, conv, s = _gdn_prompt(pm, pba, conv0, s0, p["cw_t"], p["gp"], p["nw"],
                             batch=batch, seq=seq, heads=heads, dk=dk, dv=dv, tb=256)
    ones = jnp.ones((1, o.shape[1]), F32)
    y = _nmm(o, ones, p["w_out"], x if residual else None, norm=False, tm=512, tn=512)
    return y, conv[:, 8 - (CONV_W - 1):, :], s


def _trunk(x, states, w, *, batch, seq):
    conv_in, gs_in, c_in, n_in, m_in, shift_in, rs_in = states
    depth = w["norm_mix"].shape[0]
    gh, gdk, gdv = gs_in.shape[2:]
    mh, mdk, mdv = c_in.shape[2:]
    rh, rhd = rs_in.shape[2:4]
    prompt = seq > 1
    tm_ffn = 1024 if prompt else 128
    outs = [[] for _ in range(7)]
    for i in range(depth):
        j = i // 3
        g = w["norm_mix"][i][None, :]
        if i % 3 == 0:
            p = w["gdn"][j]
            if prompt:
                x, cb, s = _gdn_prompt_layer(x, g, p, conv_in[j], gs_in[j], batch=batch, seq=seq,
                                             heads=gh, dk=gdk, dv=gdv)
            else:
                x, cb, s = _gdn_sample_layer(x, g, p, conv_in[j], gs_in[j], heads=gh, dk=gdk, dv=gdv)
            outs[0].append(cb)
            outs[1].append(s)
        elif i % 3 == 1:
            p = w["ml"][j]
            if prompt:
                x, c, n, m = _mlstm_prompt_layer(x, g, p, c_in[j], n_in[j], m_in[j], batch=batch, seq=seq,
                                                 heads=mh, dk=mdk, dv=mdv)
            else:
                x, c, n, m = _mlstm_sample_layer(x, g, p, c_in[j], n_in[j], m_in[j], heads=mh, dk=mdk, dv=mdv)
            outs[2].append(c)
            outs[3].append(n)
            outs[4].append(m)
        else:
            p = w["rw"][j]
            if prompt:
                x, sh, s = _rwkv_prompt_layer(x, g, p, shift_in[j], rs_in[j], batch=batch, seq=seq,
                                              heads=rh, hd=rhd)
            else:
                x, sh, s = _rwkv_sample_layer(x, g, p, shift_in[j], rs_in[j], heads=rh, hd=rhd)
            outs[5].append(sh)
            outs[6].append(s)
        x = _ffn(x, w["norm_ffn"][i][None, :], w["ffn_w1"][i], w["ffn_w2"][i], tm=tm_ffn, tf=512)
    y = _rmsnorm(x, w["norm_final"][None, :], tm=tm_ffn)
    return y, tuple(jnp.stack(z, axis=0) for z in outs)


def kernel(x_prompt, x_sample, state_gdn_conv, state_gdn_S, state_mlstm_C, state_mlstm_n, state_mlstm_m, state_rwkv_shift, state_rwkv_S, norm_mix, norm_ffn, norm_final, gdn_w_in, gdn_conv_w, gdn_a_log, gdn_dt_bias, gdn_norm_w, gdn_w_out, ml_w_in, ml_b_if, ml_norm_w, ml_w_out, rw_mu, rw_w_rkv, rw_w_o, rw_w0, rw_w1, rw_w2, rw_a0, rw_a1, rw_a2, rw_g1, rw_g2, rw_k_k, rw_k_a, rw_r_k, rw_lnx_w, rw_lnx_b, ffn_w1, ffn_w2):
    gh, gdk, gdv = state_gdn_S.shape[2:]
    mh, mdk, mdv = state_mlstm_C.shape[2:]
    w = dict(
        norm_mix=norm_mix, norm_ffn=norm_ffn, norm_final=norm_final,
        ffn_w1=ffn_w1.astype(BF16), ffn_w2=ffn_w2.astype(BF16),
        gdn=[_gdn_prep(gdn_w_in[j], gdn_conv_w[j], gdn_a_log[j], gdn_dt_bias[j], gdn_norm_w[j], gdn_w_out[j],
                       heads=gh, dk=gdk, dv=gdv) for j in range(gdn_w_in.shape[0])],
        ml=[_mlstm_prep(ml_w_in[j], ml_b_if[j], ml_norm_w[j], ml_w_out[j], heads=mh, dk=mdk, dv=mdv)
            for j in range(ml_w_in.shape[0])],
        rw=[_rwkv_prep(rw_mu[j], rw_w_rkv[j], rw_w_o[j], rw_w0[j], rw_w1[j], rw_w2[j], rw_a0[j], rw_a1[j],
                       rw_a2[j], rw_g1[j], rw_g2[j], rw_k_k[j], rw_k_a[j], rw_r_k[j], rw_lnx_w[j], rw_lnx_b[j])
            for j in range(rw_mu.shape[0])])
    sample_states = (state_gdn_conv, state_gdn_S, state_mlstm_C, state_mlstm_n, state_mlstm_m,
                     state_rwkv_shift, state_rwkv_S)
    bp, tp, d = x_prompt.shape
    bs, ts, _ = x_sample.shape
    assert ts == 1
    prompt_states = tuple(jnp.zeros((s.shape[0], bp) + s.shape[2:], s.dtype) for s in sample_states)
    y_p, new_p = _trunk(x_prompt.reshape(bp * tp, d), prompt_states, w, batch=bp, seq=tp)
    y_s, new_s = _trunk(x_sample.reshape(bs * ts, d), sample_states, w, batch=bs, seq=ts)
    out = [y_p.reshape(bp, tp, d), y_s.reshape(bs, ts, d)]
    for a, b in zip(new_p, new_s):
        out += [a, b]
    return tuple(out)
```

```python
import functools
import math

import jax
import jax.numpy as jnp
from jax import lax
from jax.experimental import pallas as pl
from jax.experimental.pallas import tpu as pltpu

F32 = jnp.float32
BF16 = jnp.bfloat16

RMS_EPS = 1e-6
NEG_BIG = -1e30
GATE_CAP = 15.0
RW_GN_EPS = 64e-5
CONV_W = 4
CHUNK = 64
RW_GROUP = 8
V7X_VMEM_LIMIT = 56 * 1024 * 1024
HI = lax.Precision.HIGHEST


def _cparams(sem):
    return pltpu.CompilerParams(dimension_semantics=sem, vmem_limit_bytes=V7X_VMEM_LIMIT)


def _dot(a, b):
    return jnp.dot(a.astype(BF16), b.astype(BF16), preferred_element_type=F32)


def _dot_nt(a, b):
    return lax.dot_general(a.astype(BF16), b.astype(BF16), (((1,), (1,)), ((), ())),
                           preferred_element_type=F32)


def _dot_tn(a, b):
    return lax.dot_general(a.astype(BF16), b.astype(BF16), (((0,), (0,)), ((), ())),
                           preferred_element_type=F32)


def _dot_hi(a, b):
    return jnp.dot(a, b, preferred_element_type=F32, precision=HI)


def _sigmoid(x):
    return 1.0 / (1.0 + jnp.exp(-x))


def _silu(x):
    return x * _sigmoid(x)


def _softplus(x):
    return jnp.maximum(x, 0.0) + jnp.log(1.0 + jnp.exp(-jnp.abs(x)))


def _log_sigmoid(x):
    return -_softplus(-x)


def _tri_masks(l):
    r = lax.broadcasted_iota(jnp.int32, (l, l), 0)
    c = lax.broadcasted_iota(jnp.int32, (l, l), 1)
    return r >= c, r > c


INV_BASE = 16


def _unit_lower_inverse(mats, l):
    r = lax.broadcasted_iota(jnp.int32, (l, l), 0)
    c = lax.broadcasted_iota(jnp.int32, (l, l), 1)
    eye = (r == c).astype(F32)
    size = min(INV_BASE, l)
    shift = size.bit_length() - 1
    diag = (r >> shift) == (c >> shift)
    a_d = [jnp.where(diag, a, 0.0) for a in mats]
    t = [eye - a for a in a_d]
    p = [_dot(a, a) for a in a_d]
    n = 2
    while n < size:
        tp = [_dot(ti, pi) for ti, pi in zip(t, p)]
        n *= 2
        if n < size:
            p = [_dot(pi, pi) for pi in p]
        t = [ti + d for ti, d in zip(t, tp)]
    while size < l:
        shift = size.bit_length() - 1
        off = ((r >> (shift + 1)) == (c >> (shift + 1))) & ((r >> shift) > (c >> shift))
        x = [_dot(jnp.where(off, a, 0.0), ti) for a, ti in zip(mats, t)]
        y = [_dot(ti, xi) for ti, xi in zip(t, x)]
        t = [ti - yi for ti, yi in zip(t, y)]
        size *= 2
    return t


def _nmm_kernel(*refs, norm, residual):
    x_ref, g_ref, w_ref = refs[:3]
    res_ref = refs[3] if residual else None
    o_ref, xn_ref = refs[-2:]

    @pl.when(pl.program_id(1) == 0)
    def _():
        x = x_ref[...]
        if norm:
            x = x * lax.rsqrt(jnp.mean(x * x, axis=-1, keepdims=True) + RMS_EPS) * g_ref[...]
        xn_ref[...] = x.astype(BF16)

    y = jnp.dot(xn_ref[...], w_ref[...], preferred_element_type=F32)
    if residual:
        y = res_ref[...] + y
    o_ref[...] = y


def _nmm(x, g, w, res=None, *, norm, tm, tn):
    m, k = x.shape
    n = w.shape[1]
    tm, tn = min(tm, m), min(tn, n)
    assert m % tm == 0 and n % tn == 0
    in_specs = [pl.BlockSpec((tm, k), lambda i, j: (i, 0)),
                pl.BlockSpec((1, k), lambda i, j: (0, 0)),
                pl.BlockSpec((k, tn), lambda i, j: (0, j))]
    args = [x, g, w]
    if res is not None:
        in_specs.append(pl.BlockSpec((tm, tn), lambda i, j: (i, j)))
        args.append(res)
    return pl.pallas_call(
        functools.partial(_nmm_kernel, norm=norm, residual=res is not None),
        grid=(m // tm, n // tn),
        in_specs=in_specs,
        out_specs=pl.BlockSpec((tm, tn), lambda i, j: (i, j)),
        out_shape=jax.ShapeDtypeStruct((m, n), F32),
        scratch_shapes=[pltpu.VMEM((tm, k), BF16)],
        compiler_params=_cparams(("parallel", "arbitrary")),
        name="nmm",
    )(*args)


def _ffn_kernel(x_ref, g_ref, w1_ref, w2_ref, o_ref, xn_ref, acc_ref):
    f = pl.program_id(1)

    @pl.when(f == 0)
    def _():
        x = x_ref[...]
        x = x * lax.rsqrt(jnp.mean(x * x, axis=-1, keepdims=True) + RMS_EPS) * g_ref[...]
        xn_ref[...] = x.astype(BF16)
        acc_ref[...] = jnp.zeros_like(acc_ref)

    h = jnp.dot(xn_ref[...], w1_ref[...], preferred_element_type=F32)
    a = jnp.square(jnp.maximum(h, 0.0)).astype(BF16)
    acc_ref[...] += jnp.dot(a, w2_ref[...], preferred_element_type=F32)

    @pl.when(f == pl.num_programs(1) - 1)
    def _():
        o_ref[...] = x_ref[...] + acc_ref[...]


def _ffn(x, g, w1, w2, *, tm, tf):
    m, d = x.shape
    dff = w1.shape[1]
    tm, tf = min(tm, m), min(tf, dff)
    assert m % tm == 0 and dff % tf == 0
    return pl.pallas_call(
        _ffn_kernel,
        grid=(m // tm, dff // tf),
        in_specs=[pl.BlockSpec((tm, d), lambda i, j: (i, 0)),
                  pl.BlockSpec((1, d), lambda i, j: (0, 0)),
                  pl.BlockSpec((d, tf), lambda i, j: (0, j)),
                  pl.BlockSpec((tf, d), lambda i, j: (j, 0))],
        out_specs=pl.BlockSpec((tm, d), lambda i, j: (i, 0)),
        out_shape=jax.ShapeDtypeStruct((m, d), F32),
        scratch_shapes=[pltpu.VMEM((tm, d), BF16), pltpu.VMEM((tm, d), F32)],
        compiler_params=_cparams(("parallel", "arbitrary")),
        name="ffn",
    )(x, g, w1, w2)


def _norm_kernel(x_ref, g_ref, o_ref):
    x = x_ref[...]
    o_ref[...] = x * lax.rsqrt(jnp.mean(x * x, axis=-1, keepdims=True) + RMS_EPS) * g_ref[...]


def _rmsnorm(x, g, *, tm):
    m, d = x.shape
    tm = min(tm, m)
    assert m % tm == 0
    return pl.pallas_call(
        _norm_kernel,
        grid=(m // tm,),
        in_specs=[pl.BlockSpec((tm, d), lambda i: (i, 0)), pl.BlockSpec((1, d), lambda i: (0, 0))],
        out_specs=pl.BlockSpec((tm, d), lambda i: (i, 0)),
        out_shape=jax.ShapeDtypeStruct((m, d), F32),
        compiler_params=_cparams(("parallel",)),
        name="rmsnorm",
    )(x, g)


def _gdn_prompt_kernel(pm_ref, pba_ref, conv0_ref, s0_ref, cw_ref, gp_ref, nw_ref,
                       o_ref, conv_out_ref, s_out_ref,
                       full_ref, qkv_ref, s_ref, *, tb, heads, dk, dv):
    t = pl.program_id(1)
    key = heads * dk
    ch = 2 * key + heads * dv
    l = CHUNK

    @pl.when(t == 0)
    def _():
        full_ref[0:8, :] = conv0_ref[0]
        s_ref[...] = s0_ref[0]

    full_ref[8:8 + tb, :] = pm_ref[:, 0:ch]
    for c in range(ch // 128):
        cs = slice(c * 128, (c + 1) * 128)
        y = full_ref[5:5 + tb, cs] * cw_ref[0:1, cs]
        for j in range(1, CONV_W):
            y = y + full_ref[5 + j:5 + j + tb, cs] * cw_ref[j:j + 1, cs]
        y = _silu(y)
        if c * 128 < key:
            y = y * lax.rsqrt(jnp.sum(y * y, axis=-1, keepdims=True) + 1e-6) * (dk ** -0.5)
        elif c * 128 < 2 * key:
            y = y * lax.rsqrt(jnp.sum(y * y, axis=-1, keepdims=True) + 1e-6)
        qkv_ref[:, cs] = y
    full_ref[0:8, :] = full_ref[tb:tb + 8, :]

    incl, strict = _tri_masks(l)
    tril = incl.astype(F32)
    a_log = gp_ref[0:1, :]
    dt_bias = gp_ref[1:2, :]

    def chunk(ci, carry):
        r0 = pl.multiple_of(ci * l, l)
        rows = pl.ds(r0, l)
        ba = pba_ref[rows, :]
        beta_all = _sigmoid(ba)
        g_all = -jnp.exp(a_log) * _softplus(ba + dt_bias)
        gc = _dot_hi(tril, g_all)
        gc_t = gc.T
        hh = range(heads)
        b_col = [beta_all[:, h:h + 1] for h in hh]
        gi = [gc[:, heads + h:heads + h + 1] for h in hh]
        g_last = [gc[l - 1:l, heads + h:heads + h + 1] for h in hh]
        q = [qkv_ref[rows, h * dk:(h + 1) * dk] for h in hh]
        k = [qkv_ref[rows, key + h * dk:key + (h + 1) * dk] for h in hh]
        v = [qkv_ref[rows, 2 * key + h * dv:2 * key + (h + 1) * dv] for h in hh]
        dmat = [jnp.where(incl, jnp.exp(jnp.where(incl, gi[h] - gc_t[heads + h:heads + h + 1, :], 0.0)), 0.0)
                for h in hh]
        kb = [k[h] * b_col[h] for h in hh]
        kk = [_dot_nt(kb[h], k[h]) for h in hh]
        qk = [_dot_nt(q[h], k[h]) for h in hh]
        t_inv = _unit_lower_inverse([jnp.where(strict, kk[h] * dmat[h], 0.0) for h in hh], l)
        egi = [jnp.exp(gi[h]) for h in hh]
        sol = [_dot(t_inv[h], jnp.concatenate([v[h] * b_col[h], kb[h] * egi[h]], axis=-1)) for h in hh]
        s = [s_ref[h] for h in hh]
        ws = [_dot(sol[h][:, dv:], s[h]) for h in hh]
        qs = [_dot(q[h] * egi[h], s[h]) for h in hh]
        v_new = [sol[h][:, :dv] - ws[h] for h in hh]
        o2 = [_dot(jnp.where(incl, qk[h] * dmat[h], 0.0), v_new[h]) for h in hh]
        ds = [_dot_tn(k[h] * jnp.exp(g_last[h] - gi[h]), v_new[h]) for h in hh]
        for h in hh:
            s_ref[h] = s[h] * jnp.exp(g_last[h]) + ds[h]
            o = qs[h] + o2[h]
            z = pm_ref[rows, ch + h * dv:ch + (h + 1) * dv]
            o = o * lax.rsqrt(jnp.mean(o * o, axis=-1, keepdims=True) + RMS_EPS) * nw_ref[...]
            o_ref[rows, h * dv:(h + 1) * dv] = o * _silu(z)
        return carry

    lax.fori_loop(0, tb // l, chunk, 0)

    @pl.when(t == pl.num_programs(1) - 1)
    def _():
        conv_out_ref[0] = full_ref[0:8, :]
        s_out_ref[0] = s_ref[...]


def _gdn_prompt(pm, pba, conv0, s0, cw_t, gp, nw, *, batch, seq, heads, dk, dv, tb):
    key, val = heads * dk, heads * dv
    ch = 2 * key + val
    tb = min(tb, seq)
    assert seq % tb == 0 and tb % CHUNK == 0
    nt = seq // tb
    return pl.pallas_call(
        functools.partial(_gdn_prompt_kernel, tb=tb, heads=heads, dk=dk, dv=dv),
        grid=(batch, nt),
        in_specs=[pl.BlockSpec((tb, ch + val), lambda b, t: (b * nt + t, 0)),
                  pl.BlockSpec((tb, 128), lambda b, t: (b * nt + t, 0)),
                  pl.BlockSpec((1, 8, ch), lambda b, t: (b, 0, 0)),
                  pl.BlockSpec((1, heads, dk, dv), lambda b, t: (b, 0, 0, 0)),
                  pl.BlockSpec((8, ch), lambda b, t: (0, 0)),
                  pl.BlockSpec((8, 128), lambda b, t: (0, 0)),
                  pl.BlockSpec((1, dv), lambda b, t: (0, 0))],
        out_specs=[pl.BlockSpec((tb, val), lambda b, t: (b * nt + t, 0)),
                   pl.BlockSpec((1, 8, ch), lambda b, t: (b, 0, 0)),
                   pl.BlockSpec((1, heads, dk, dv), lambda b, t: (b, 0, 0, 0))],
        out_shape=[jax.ShapeDtypeStruct((batch * seq, val), F32),
                   jax.ShapeDtypeStruct((batch, 8, ch), F32),
                   jax.ShapeDtypeStruct((batch, heads, dk, dv), F32)],
        scratch_shapes=[pltpu.VMEM((tb + 8, ch), F32), pltpu.VMEM((tb, ch), F32),
                        pltpu.VMEM((heads, dk, dv), F32)],
        compiler_params=_cparams(("parallel", "arbitrary")),
        name="gdn_prompt",
    )(pm, pba, conv0, s0, cw_t, gp, nw)


def _mlstm_prompt_kernel(pm_ref, pif_ref, bif_ref, c0_ref, n0_ref, m0_ref, nw_ref,
                         o_ref, c_out_ref, n_out_ref, m_out_ref,
                         c_ref, n_ref, m_ref, *, tb, heads, dk, dv):
    t = pl.program_id(1)
    l = CHUNK
    qk_w = heads * dk
    v_off = 2 * qk_w
    o_off = v_off + heads * dv

    @pl.when(t == 0)
    def _():
        c_ref[...] = c0_ref[0]
        n_ref[...] = n0_ref[0]
        m_ref[...] = m0_ref[0]

    incl, _ = _tri_masks(l)
    tril = incl.astype(F32)

    def chunk(ci, carry):
        r0 = pl.multiple_of(ci * l, l)
        rows = pl.ds(r0, l)
        gates = pif_ref[rows, :] + bif_ref[...]
        gates = GATE_CAP * jnp.tanh(gates / GATE_CAP)
        bcum = _dot_hi(tril, _log_sigmoid(gates))
        bcum_t = bcum.T
        gates_t = gates.T
        hh = range(heads)
        bi = [bcum[:, heads + h:heads + h + 1] for h in hh]
        ii = [gates[:, h:h + 1] for h in hh]
        b_last = [bcum[l - 1:l, heads + h:heads + h + 1] for h in hh]
        q = [pm_ref[rows, h * dk:(h + 1) * dk] * (dk ** -0.5) for h in hh]
        k = [pm_ref[rows, qk_w + h * dk:qk_w + (h + 1) * dk] for h in hh]
        v = [pm_ref[rows, v_off + h * dv:v_off + (h + 1) * dv] for h in hh]
        qk = [_dot_nt(q[h], k[h]) for h in hh]
        c_mat = [c_ref[h] for h in hh]
        qc = [_dot(q[h], c_mat[h]) for h in hh]
        dlog = [jnp.where(incl, bi[h] - bcum_t[heads + h:heads + h + 1, :] + gates_t[h:h + 1, :], NEG_BIG)
                for h in hh]
        m_intra = [jnp.max(dlog[h], axis=-1, keepdims=True) for h in hh]
        p = [jnp.where(incl, jnp.exp(dlog[h] - m_intra[h]), 0.0) * qk[h] for h in hh]
        num_intra = [_dot(p[h], v[h]) for h in hh]
        a_log = [b_last[h] - bi[h] + ii[h] for h in hh]
        m_chunk = [jnp.max(a_log[h], axis=0, keepdims=True) for h in hh]
        kw = [k[h] * jnp.exp(a_log[h] - m_chunk[h]) for h in hh]
        kv_chunk = [_dot_tn(kw[h], v[h]) for h in hh]
        for h in hh:
            den_intra = jnp.sum(p[h], axis=-1, keepdims=True)
            k_chunk = jnp.sum(kw[h], axis=0, keepdims=True)
            m_prev = m_ref[h:h + 1, 0:1]
            n_vec = n_ref[h:h + 1, :]
            m_t = jnp.maximum(bi[h] + m_prev, m_intra[h])
            s_inter = jnp.exp(bi[h] + m_prev - m_t)
            s_intra = jnp.exp(m_intra[h] - m_t)
            num = s_inter * qc[h] + s_intra * num_intra[h]
            den = s_inter * jnp.sum(q[h] * n_vec, axis=-1, keepdims=True) + s_intra * den_intra
            h_t = num / jnp.maximum(jnp.abs(den), jnp.exp(-m_t))
            m_new = jnp.maximum(b_last[h] + m_prev, m_chunk[h])
            f_s = jnp.exp(b_last[h] + m_prev - m_new)
            i_s = jnp.exp(m_chunk[h] - m_new)
            c_ref[h] = f_s * c_mat[h] + i_s * kv_chunk[h]
            n_ref[h:h + 1, :] = f_s * n_vec + i_s * k_chunk
            m_ref[h:h + 1, :] = jnp.broadcast_to(m_new, (1, m_ref.shape[1]))
            h_n = h_t * lax.rsqrt(jnp.mean(h_t * h_t, axis=-1, keepdims=True) + RMS_EPS)
            h_n = h_n * nw_ref[:, h * dv:(h + 1) * dv]
            o_pre = pm_ref[rows, o_off + h * dv:o_off + (h + 1) * dv]
            o_ref[rows, h * dv:(h + 1) * dv] = _sigmoid(o_pre) * h_n
        return carry

    lax.fori_loop(0, tb // l, chunk, 0)

    @pl.when(t == pl.num_programs(1) - 1)
    def _():
        c_out_ref[0] = c_ref[...]
        n_out_ref[0] = n_ref[...]
        m_out_ref[0] = m_ref[...]


def _mlstm_prompt(pm, pif, bif, c0, n0, m0, nw, *, batch, seq, heads, dk, dv, tb):
    width = pm.shape[1]
    val = heads * dv
    tb = min(tb, seq)
    assert seq % tb == 0 and tb % CHUNK == 0
    nt = seq // tb
    return pl.pallas_call(
        functools.partial(_mlstm_prompt_kernel, tb=tb, heads=heads, dk=dk, dv=dv),
        grid=(batch, nt),
        in_specs=[pl.BlockSpec((tb, width), lambda b, t: (b * nt + t, 0)),
                  pl.BlockSpec((tb, 128), lambda b, t: (b * nt + t, 0)),
                  pl.BlockSpec((1, 128), lambda b, t: (0, 0)),
                  pl.BlockSpec((1, heads, dk, dv), lambda b, t: (b, 0, 0, 0)),
                  pl.BlockSpec((1, 8, dk), lambda b, t: (b, 0, 0)),
                  pl.BlockSpec((1, 8, 128), lambda b, t: (b, 0, 0)),
                  pl.BlockSpec((1, val), lambda b, t: (0, 0))],
        out_specs=[pl.BlockSpec((tb, val), lambda b, t: (b * nt + t, 0)),
                   pl.BlockSpec((1, heads, dk, dv), lambda b, t: (b, 0, 0, 0)),
                   pl.BlockSpec((1, 8, dk), lambda b, t: (b, 0, 0)),
                   pl.BlockSpec((1, 8, 128), lambda b, t: (b, 0, 0))],
        out_shape=[jax.ShapeDtypeStruct((batch * seq, val), F32),
                   jax.ShapeDtypeStruct((batch, heads, dk, dv), F32),
                   jax.ShapeDtypeStruct((batch, 8, dk), F32),
                   jax.ShapeDtypeStruct((batch, 8, 128), F32)],
        scratch_shapes=[pltpu.VMEM((heads, dk, dv), F32), pltpu.VMEM((8, dk), F32),
                        pltpu.VMEM((8, 128), F32)],
        compiler_params=_cparams(("parallel", "arbitrary")),
        name="mlstm_prompt",
    )(pm, pif, bif, c0, n0, m0, nw)


def _mlstm_prep(w_in, b_if, norm_w, w_out, *, heads, dk, dv):
    main = 2 * heads * dk + 2 * heads * dv
    return dict(w_main=w_in[:, :main].astype(BF16),
                w_if=_pad_cols(w_in[:, main:], 128).astype(BF16),
                bif=_pad_cols(b_if[None, :], 128), nw=norm_w[None, :], w_out=w_out.astype(BF16))


def _mlstm_prompt_layer(x, g, p, c0, n0, m0, *, batch, seq, heads, dk, dv, norm=True, residual=True):
    pm = _nmm(x, g, p["w_main"], norm=norm, tm=512, tn=512)
    pif = _nmm(x, g, p["w_if"], norm=norm, tm=512, tn=128)
    n0p = jnp.pad(n0, ((0, 0), (0, 8 - heads), (0, 0)))
    m0p = jnp.broadcast_to(jnp.pad(m0, ((0, 0), (0, 8 - heads)))[:, :, None], (batch, 8, 128))
    o, c, n, m = _mlstm_prompt(pm, pif, p["bif"], c0, n0p, m0p, p["nw"],
                               batch=batch, seq=seq, heads=heads, dk=dk, dv=dv, tb=256)
    ones = jnp.ones((1, o.shape[1]), F32)
    y = _nmm(o, ones, p["w_out"], x if residual else None, norm=False, tm=512, tn=512)
    return y, c, n[:, :heads, :], m[:, :heads, 0]


def _rwkv_proj_kernel(h_ref, prev_ref, mu_ref, wrkv_ref, w1_ref, w2_ref, a1_ref, a2_ref, g1_ref, g2_ref,
                      vec_ref, r_ref, k_ref, v_ref, lw_ref, kk_ref, a_ref, g_ref):
    h = h_ref[...]
    xx = prev_ref[...] - h

    def mix(j):
        return (h + xx * mu_ref[j:j + 1, :]).astype(BF16)

    w0, a0, k_k, k_a = (vec_ref[j:j + 1, :] for j in range(4))
    r_ref[...] = jnp.dot(mix(0), wrkv_ref[0], preferred_element_type=F32)
    lora_w = _dot(jnp.tanh(_dot(mix(1), w1_ref[...])), w2_ref[...])
    w_log = -_softplus(-(w0 + lora_w)) - 0.5
    lw_ref[...] = -jnp.exp(w_log)
    k = jnp.dot(mix(2), wrkv_ref[1], preferred_element_type=F32)
    v_ref[...] = jnp.dot(mix(3), wrkv_ref[2], preferred_element_type=F32)
    a = _sigmoid(a0 + _dot(_dot(mix(4), a1_ref[...]), a2_ref[...]))
    g_ref[...] = _dot(_sigmoid(_dot(mix(5), g1_ref[...])), g2_ref[...])
    kk_ref[...] = k * k_k
    k_ref[...] = k * (1.0 + (a - 1.0) * k_a)
    a_ref[...] = a


def _rwkv_proj(hn, prev, p, *, tm):
    m, d = hn.shape
    tm = min(tm, m)
    assert m % tm == 0
    row = pl.BlockSpec((tm, d), lambda i: (i, 0))

    def full(a):
        nd = a.ndim
        return pl.BlockSpec(a.shape, lambda i: (0,) * nd)

    consts = [p["mu"], p["w_rkv"], p["w1"], p["w2"], p["a1"], p["a2"], p["g1"], p["g2"], p["vec"]]
    return pl.pallas_call(
        _rwkv_proj_kernel,
        grid=(m // tm,),
        in_specs=[row, row] + [full(a) for a in consts],
        out_specs=[row] * 7,
        out_shape=[jax.ShapeDtypeStruct((m, d), F32)] * 7,
        compiler_params=_cparams(("parallel",)),
        name="rwkv_proj",
    )(hn, prev, *consts)


def _rwkv_post(y, r, k, v, g, r_k, lnx_w, lnx_b):
    mean = jnp.mean(y, axis=-1, keepdims=True)
    yc = y - mean
    var = jnp.mean(yc * yc, axis=-1, keepdims=True)
    y = yc * lax.rsqrt(var + RW_GN_EPS) * lnx_w + lnx_b
    y = y + jnp.sum(r * k * r_k, axis=-1, keepdims=True) * v
    return y * g


def _rwkv_prompt_kernel(r_ref, k_ref, v_ref, lw_ref, kk_ref, a_ref, g_ref, s0_ref, hp_ref,
                        o_ref, s_out_ref, s_ref, *, tb, heads, hd):
    t = pl.program_id(1)
    l = CHUNK

    @pl.when(t == 0)
    def _():
        s_ref[...] = s0_ref[0]

    incl, strict = _tri_masks(l)
    tril = incl.astype(F32)

    def chunk(ci, carry):
        r0 = pl.multiple_of(ci * l, l)
        rows = pl.ds(r0, l)
        lw = lw_ref[rows, :]
        lwc = _dot_hi(tril, lw)
        lw_last = lwc[l - 1:l, :]
        e_in = jnp.exp(lwc)
        e_prev = jnp.exp(lwc - lw)
        e_neg = jnp.exp(-lwc)
        e_end = jnp.exp(lw_last - lwc)
        e_last = jnp.exp(lw_last)
        for h0 in range(0, heads, RW_GROUP):
            hh = range(h0, h0 + RW_GROUP)
            hs = {h: slice(h * hd, (h + 1) * hd) for h in hh}
            r = {h: r_ref[rows, hs[h]] for h in hh}
            k = {h: k_ref[rows, hs[h]] for h in hh}
            v = {h: v_ref[rows, hs[h]] for h in hh}
            kk = {h: kk_ref[rows, hs[h]] for h in hh}
            kk = {h: kk[h] * lax.rsqrt(jnp.maximum(jnp.sum(kk[h] * kk[h], axis=-1, keepdims=True), 1e-24))
                  for h in hh}
            bv = {h: kk[h] * a_ref[rows, hs[h]] for h in hh}
            a_t = {h: -kk[h] * e_prev[:, hs[h]] for h in hh}
            r_t = {h: r[h] * e_in[:, hs[h]] for h in hh}
            b_t = {h: bv[h] * e_neg[:, hs[h]] for h in hh}
            k_t = {h: k[h] * e_neg[:, hs[h]] for h in hh}
            ab = {h: _dot_nt(a_t[h], b_t[h]) for h in hh}
            ak = {h: _dot_nt(a_t[h], k_t[h]) for h in hh}
            rb = {h: _dot_nt(r_t[h], b_t[h]) for h in hh}
            rk = {h: _dot_nt(r_t[h], k_t[h]) for h in hh}
            t_inv = dict(zip(hh, _unit_lower_inverse([jnp.where(strict, -ab[h], 0.0) for h in hh], l)))
            s = {h: s_ref[h] for h in hh}
            a_s = {h: _dot_nt(a_t[h], s[h]) for h in hh}
            r_s = {h: _dot_nt(r_t[h], s[h]) for h in hh}
            akv = {h: _dot(jnp.where(strict, ak[h], 0.0), v[h]) for h in hh}
            rkv = {h: _dot(jnp.where(incl, rk[h], 0.0), v[h]) for h in hh}
            u = {h: _dot(t_inv[h], a_s[h] + akv[h]) for h in hh}
            rbu = {h: _dot(jnp.where(incl, rb[h], 0.0), u[h]) for h in hh}
            ub = {h: _dot_tn(u[h], bv[h] * e_end[:, hs[h]]) for h in hh}
            vk = {h: _dot_tn(v[h], k[h] * e_end[:, hs[h]]) for h in hh}
            for h in hh:
                s_ref[h] = s[h] * e_last[:, hs[h]] + ub[h] + vk[h]
                y = r_s[h] + rbu[h] + rkv[h]
                o_ref[rows, hs[h]] = _rwkv_post(y, r[h], k[h], v[h], g_ref[rows, hs[h]],
                                                hp_ref[0:1, hs[h]], hp_ref[1:2, hs[h]], hp_ref[2:3, hs[h]])
        return carry

    lax.fori_loop(0, tb // l, chunk, 0)

    @pl.when(t == pl.num_programs(1) - 1)
    def _():
        s_out_ref[0] = s_ref[...]


def _rwkv_prompt(r, k, v, lw, kk, a, g, s0, hp, *, batch, seq, heads, hd, tb):
    d = heads * hd
    tb = min(tb, seq)
    assert seq % tb == 0 and tb % CHUNK == 0
    nt = seq // tb
    row = pl.BlockSpec((tb, d), lambda b, t: (b * nt + t, 0))
    st = pl.BlockSpec((1, heads, hd, hd), lambda b, t: (b, 0, 0, 0))
    return pl.pallas_call(
        functools.partial(_rwkv_prompt_kernel, tb=tb, heads=heads, hd=hd),
        grid=(batch, nt),
        in_specs=[row] * 7 + [st, pl.BlockSpec((8, d), lambda b, t: (0, 0))],
        out_specs=[row, st],
        out_shape=[jax.ShapeDtypeStruct((batch * seq, d), F32),
                   jax.ShapeDtypeStruct((batch, heads, hd, hd), F32)],
        scratch_shapes=[pltpu.VMEM((heads, hd, hd), F32)],
        compiler_params=_cparams(("parallel", "arbitrary")),
        name="rwkv_prompt",
    )(r, k, v, lw, kk, a, g, s0, hp)


def _pad_rows(a, n):
    return jnp.pad(a, ((0, n - a.shape[0]), (0, 0)))


def _rwkv_prep(mu, w_rkv, w_o, w0, w1, w2, a0, a1, a2, g1, g2, k_k, k_a, r_k, lnx_w, lnx_b):
    d = w0.shape[0]
    lw = -(-w1.shape[1] // 128) * 128
    la = -(-a1.shape[1] // 128) * 128
    lg = -(-g1.shape[1] // 128) * 128
    return dict(mu=_pad_rows(mu, 8), w_rkv=w_rkv.astype(BF16), w_o=w_o.astype(BF16),
                w1=_pad_cols(w1, lw).astype(BF16), w2=_pad_rows(w2, lw).astype(BF16),
                a1=_pad_cols(a1, la).astype(BF16), a2=_pad_rows(a2, la).astype(BF16),
                g1=_pad_cols(g1, lg).astype(BF16), g2=_pad_rows(g2, lg).astype(BF16),
                vec=_pad_rows(jnp.stack([w0, a0, k_k, k_a]), 8),
                hp=_pad_rows(jnp.stack([r_k.reshape(d), lnx_w, lnx_b]), 8))


def _rwkv_prompt_layer(x, g_norm, p, shift0, s0, *, batch, seq, heads, hd, norm=True, residual=True):
    d = x.shape[1]
    hn = _rmsnorm(x, g_norm, tm=1024) if norm else x
    hn3 = hn.reshape(batch, seq, d)
    prev = jnp.concatenate([shift0[:, None, :], hn3[:, :-1]], axis=1).reshape(batch * seq, d)
    r, k, v, lw, kk, a, g = _rwkv_proj(hn, prev, p, tm=256)
    o, s = _rwkv_prompt(r, k, v, lw, kk, a, g, s0, p["hp"], batch=batch, seq=seq, heads=heads, hd=hd, tb=256)
    ones = jnp.ones((1, d), F32)
    y = _nmm(o, ones, p["w_o"], x if residual else None, norm=False, tm=512, tn=512)
    return y, hn3[:, -1], s


def _gdn_sample_pre_kernel(pm_ref, pba_ref, conv_ref, cw_ref, gp_ref, qkv_ref, conv_out_ref, sc_ref,
                           *, heads, dk, dv):
    key = heads * dk
    ch = 2 * key + heads * dv
    u = pm_ref[:, 0:ch]
    y = u * cw_ref[CONV_W - 1:CONV_W, :]
    for j in range(CONV_W - 1):
        y = y + conv_ref[j] * cw_ref[j:j + 1, :]
        conv_out_ref[j] = conv_ref[j + 1] if j + 1 < CONV_W - 1 else u
    y = _silu(y)
    for c in range(ch // 128):
        cs = slice(c * 128, (c + 1) * 128)
        yc = y[:, cs]
        if c * 128 < key:
            yc = yc * lax.rsqrt(jnp.sum(yc * yc, axis=-1, keepdims=True) + 1e-6) * (dk ** -0.5)
        elif c * 128 < 2 * key:
            yc = yc * lax.rsqrt(jnp.sum(yc * yc, axis=-1, keepdims=True) + 1e-6)
        qkv_ref[:, cs] = yc
    ba = pba_ref[...]
    lane = lax.broadcasted_iota(jnp.int32, ba.shape, 1)
    g = -jnp.exp(gp_ref[0:1, :]) * _softplus(ba + gp_ref[1:2, :])
    sc_ref[...] = jnp.where(lane < heads, _sigmoid(ba), jnp.exp(g))


def _gdn_sample_pre(pm, pba, conv_t, cw_t, gp, *, heads, dk, dv):
    n = pm.shape[0]
    ch = 2 * heads * dk + heads * dv
    return pl.pallas_call(
        functools.partial(_gdn_sample_pre_kernel, heads=heads, dk=dk, dv=dv),
        out_shape=[jax.ShapeDtypeStruct((n, ch), F32),
                   jax.ShapeDtypeStruct((CONV_W - 1, n, ch), F32),
                   jax.ShapeDtypeStruct((n, 128), F32)],
        compiler_params=pltpu.CompilerParams(vmem_limit_bytes=V7X_VMEM_LIMIT),
        name="gdn_sample_pre",
    )(pm, pba, conv_t, cw_t, gp)


def _gdn_sample_step_kernel(s0_ref, cols_ref, v_ref, z_ref, sc_ref, nw_ref, s_out_ref, o_ref, *, heads):
    for h in range(heads):
        kc = cols_ref[0, :, h:h + 1]
        qc = cols_ref[0, :, heads + h:heads + h + 1]
        s0 = s0_ref[0, h]
        beta = sc_ref[0, h:h + 1, 0:1]
        eg = sc_ref[0, h:h + 1, 1:2]
        ks = jnp.sum(kc * s0, axis=0, keepdims=True)
        s1 = eg * s0 + kc * (beta * (v_ref[0, h:h + 1, :] - eg * ks))
        s_out_ref[0, h] = s1
        o = jnp.sum(qc * s1, axis=0, keepdims=True)
        o = o * lax.rsqrt(jnp.mean(o * o, axis=-1, keepdims=True) + RMS_EPS) * nw_ref[...]
        o_ref[0, h:h + 1, :] = o * _silu(z_ref[0, h:h + 1, :])


def _gdn_sample_step(s0, cols, v, z, sc, nw):
    n, heads, dk, dv = s0.shape

    def blk(a):
        nd = a.ndim
        return pl.BlockSpec((1,) + a.shape[1:], lambda b: (b,) + (0,) * (nd - 1))

    return pl.pallas_call(
        functools.partial(_gdn_sample_step_kernel, heads=heads),
        grid=(n,),
        in_specs=[blk(s0), blk(cols), blk(v), blk(z), blk(sc), pl.BlockSpec((1, dv), lambda b: (0, 0))],
        out_specs=[blk(s0), blk(v)],
        out_shape=[jax.ShapeDtypeStruct(s0.shape, F32), jax.ShapeDtypeStruct(v.shape, F32)],
        compiler_params=_cparams(("parallel",)),
        name="gdn_sample_step",
    )(s0, cols, v, z, sc, nw)


def _gdn_sample_layer(x, g, p, conv0, s0, *, heads, dk, dv):
    n = x.shape[0]
    key, val = heads * dk, heads * dv
    ch = 2 * key + val
    pm = _nmm(x, g, p["w_main"], norm=True, tm=128, tn=1024)
    pba = _nmm(x, g, p["w_ba"], norm=True, tm=128, tn=128)
    qkv, conv_t, sc = _gdn_sample_pre(pm, pba, jnp.transpose(conv0, (1, 0, 2)), p["cw_t"], p["gp"],
                                      heads=heads, dk=dk, dv=dv)
    q_c = jnp.transpose(qkv[:, :key].reshape(n, heads, dk), (0, 2, 1))
    k_c = jnp.transpose(qkv[:, key:2 * key].reshape(n, heads, dk), (0, 2, 1))
    cols = jnp.concatenate([k_c, q_c], axis=-1)
    sc3 = jnp.stack([sc[:, :heads], sc[:, heads:2 * heads]], axis=-1)
    s1, o = _gdn_sample_step(s0, cols, qkv[:, 2 * key:].reshape(n, heads, dv),
                             pm[:, ch:].reshape(n, heads, dv), sc3, p["nw"])
    ones = jnp.ones((1, val), F32)
    y = _nmm(o.reshape(n, val), ones, p["w_out"], x, norm=False, tm=128, tn=1024)
    return y, jnp.transpose(conv_t, (1, 0, 2)), s1


def _mlstm_sample_step_kernel(c0_ref, n0_ref, cols_ref, q_ref, k_ref, v_ref, op_ref, sc_ref, bif_ref, nw_ref,
                              c_out_ref, n_out_ref, m_out_ref, o_ref, *, heads, dk):
    gi = sc_ref[0, :, 0:1] + bif_ref[:, 0:1]
    gf = sc_ref[0, :, 1:2] + bif_ref[:, 1:2]
    m0 = sc_ref[0, :, 2:3]
    gi = GATE_CAP * jnp.tanh(gi / GATE_CAP)
    logf = _log_sigmoid(GATE_CAP * jnp.tanh(gf / GATE_CAP))
    m_new = jnp.maximum(logf + m0, gi)
    f_s = jnp.exp(logf + m0 - m_new)
    i_s = jnp.exp(gi - m_new)
    m_out_ref[0] = m_new
    scale = dk ** -0.5
    n1 = f_s * n0_ref[0] + i_s * k_ref[0]
    n_out_ref[0] = n1
    den = jnp.sum(q_ref[0] * scale * n1, axis=-1, keepdims=True)
    floor = jnp.exp(-m_new)
    for h in range(heads):
        kc = cols_ref[0, :, h:h + 1]
        qc = cols_ref[0, :, heads + h:heads + h + 1] * scale
        c1 = f_s[h:h + 1, :] * c0_ref[0, h] + i_s[h:h + 1, :] * (kc * v_ref[0, h:h + 1, :])
        c_out_ref[0, h] = c1
        num = jnp.sum(qc * c1, axis=0, keepdims=True)
        h_t = num / jnp.maximum(jnp.abs(den[h:h + 1, :]), floor[h:h + 1, :])
        h_n = h_t * lax.rsqrt(jnp.mean(h_t * h_t, axis=-1, keepdims=True) + RMS_EPS) * nw_ref[h:h + 1, :]
        o_ref[0, h:h + 1, :] = _sigmoid(op_ref[0, h:h + 1, :]) * h_n


def _mlstm_sample_step(c0, n0, cols, q, k, v, o_pre, sc, bif2, nw2):
    n, heads, dk, dv = c0.shape

    def blk(a):
        nd = a.ndim
        return pl.BlockSpec((1,) + a.shape[1:], lambda b: (b,) + (0,) * (nd - 1))

    def full(a):
        nd = a.ndim
        return pl.BlockSpec(a.shape, lambda b: (0,) * nd)

    m_shape = (n, heads, 1)
    return pl.pallas_call(
        functools.partial(_mlstm_sample_step_kernel, heads=heads, dk=dk),
        grid=(n,),
        in_specs=[blk(c0), blk(n0), blk(cols), blk(q), blk(k), blk(v), blk(o_pre), blk(sc), full(bif2), full(nw2)],
        out_specs=[blk(c0), blk(n0), pl.BlockSpec((1, heads, 1), lambda b: (b, 0, 0)), blk(v)],
        out_shape=[jax.ShapeDtypeStruct(c0.shape, F32), jax.ShapeDtypeStruct(n0.shape, F32),
                   jax.ShapeDtypeStruct(m_shape, F32), jax.ShapeDtypeStruct(v.shape, F32)],
        compiler_params=_cparams(("parallel",)),
        name="mlstm_sample_step",
    )(c0, n0, cols, q, k, v, o_pre, sc, bif2, nw2)


def _mlstm_sample_layer(x, g, p, c0, n0, m0, *, heads, dk, dv):
    n = x.shape[0]
    qk_w, val = heads * dk, heads * dv
    pm = _nmm(x, g, p["w_main"], norm=True, tm=128, tn=1024)
    pif = _nmm(x, g, p["w_if"], norm=True, tm=128, tn=128)
    q = pm[:, :qk_w].reshape(n, heads, dk)
    k = pm[:, qk_w:2 * qk_w].reshape(n, heads, dk)
    v = pm[:, 2 * qk_w:2 * qk_w + val].reshape(n, heads, dv)
    o_pre = pm[:, 2 * qk_w + val:].reshape(n, heads, dv)
    cols = jnp.concatenate([jnp.transpose(k, (0, 2, 1)), jnp.transpose(q, (0, 2, 1))], axis=-1)
    sc = jnp.stack([pif[:, :heads], pif[:, heads:2 * heads], m0], axis=-1)
    bif2 = jnp.stack([p["bif"][0, :heads], p["bif"][0, heads:2 * heads]], axis=-1)
    c1, n1, m1, o = _mlstm_sample_step(c0, n0, cols, q, k, v, o_pre, sc, bif2, p["nw"].reshape(heads, dv))
    ones = jnp.ones((1, val), F32)
    y = _nmm(o.reshape(n, val), ones, p["w_out"], x, norm=False, tm=128, tn=1024)
    return y, c1, n1, m1[:, :, 0]


def _rwkv_sample_step_kernel(s0_ref, r_ref, k_ref, lw_ref, kk_ref, a_ref, vc_ref, gc_ref, hpc_ref, hpr_ref,
                             s_out_ref, o_ref, *, heads):
    r = r_ref[0]
    k = k_ref[0]
    kk = kk_ref[0]
    kk = kk * lax.rsqrt(jnp.maximum(jnp.sum(kk * kk, axis=-1, keepdims=True), 1e-24))
    bv = kk * a_ref[0]
    w = jnp.exp(lw_ref[0])
    bonus = jnp.sum(r * k * hpr_ref[...], axis=-1, keepdims=True)
    for h in range(heads):
        s0 = s0_ref[0, h]
        vc = vc_ref[0, :, h:h + 1]
        sa = jnp.sum(s0 * (-kk[h:h + 1, :]), axis=-1, keepdims=True)
        s1 = s0 * w[h:h + 1, :] + sa * bv[h:h + 1, :] + vc * k[h:h + 1, :]
        s_out_ref[0, h] = s1
        y = jnp.sum(s1 * r[h:h + 1, :], axis=-1, keepdims=True)
        mean = jnp.mean(y, axis=0, keepdims=True)
        yc = y - mean
        var = jnp.mean(yc * yc, axis=0, keepdims=True)
        y = yc * lax.rsqrt(var + RW_GN_EPS) * hpc_ref[0, :, h:h + 1] + hpc_ref[1, :, h:h + 1]
        y = y + bonus[h:h + 1, :] * vc
        o_ref[0, :, h:h + 1] = y * gc_ref[0, :, h:h + 1]


def _rwkv_sample_step(s0, r, k, lw, kk, a, v_c, g_c, hp_c, rk):
    n, heads, hd, _ = s0.shape

    def blk(a_):
        nd = a_.ndim
        return pl.BlockSpec((1,) + a_.shape[1:], lambda b: (b,) + (0,) * (nd - 1))

    def full(a_):
        nd = a_.ndim
        return pl.BlockSpec(a_.shape, lambda b: (0,) * nd)

    return pl.pallas_call(
        functools.partial(_rwkv_sample_step_kernel, heads=heads),
        grid=(n,),
        in_specs=[blk(s0), blk(r), blk(k), blk(lw), blk(kk), blk(a), blk(v_c), blk(g_c), full(hp_c), full(rk)],
        out_specs=[blk(s0), blk(v_c)],
        out_shape=[jax.ShapeDtypeStruct(s0.shape, F32), jax.ShapeDtypeStruct(v_c.shape, F32)],
        compiler_params=_cparams(("parallel",)),
        name="rwkv_sample_step",
    )(s0, r, k, lw, kk, a, v_c, g_c, hp_c, rk)


def _rwkv_sample_layer(x, g_norm, p, shift0, s0, *, heads, hd):
    n, d = x.shape
    hn = _rmsnorm(x, g_norm, tm=128)
    r, k, v, lw, kk, a, g = _rwkv_proj(hn, shift0, p, tm=128)

    def rows(z):
        return z.reshape(n, heads, hd)

    def cols(z):
        return jnp.transpose(z.reshape(n, heads, hd), (0, 2, 1))

    hp = p["hp"]
    hp_c = jnp.stack([hp[1].reshape(heads, hd).T, hp[2].reshape(heads, hd).T])
    s1, o_c = _rwkv_sample_step(s0, rows(r), rows(k), rows(lw), rows(kk), rows(a), cols(v), cols(g),
                                hp_c, hp[0].reshape(heads, hd))
    o = jnp.transpose(o_c, (0, 2, 1)).reshape(n, d)
    ones = jnp.ones((1, d), F32)
    y = _nmm(o, ones, p["w_o"], x, norm=False, tm=128, tn=1024)
    return y, hn, s1


def _pad_cols(a, n):
    return jnp.pad(a, ((0, 0), (0, n - a.shape[1])))


def _gdn_prep(w_in, conv_w, a_log, dt_bias, norm_w, w_out, *, heads, dk, dv):
    key, val = heads * dk, heads * dv
    ch = 2 * key + val
    main = ch + val
    gp = jnp.zeros((8, 128), F32)
    gp = gp.at[0, heads:2 * heads].set(a_log).at[1, heads:2 * heads].set(dt_bias)
    return dict(w_main=w_in[:, :main].astype(BF16),
                w_ba=_pad_cols(w_in[:, main:], 128).astype(BF16),
                cw_t=jnp.pad(conv_w.T, ((0, 8 - CONV_W), (0, 0))),
                gp=gp, nw=norm_w[None, :], w_out=w_out.astype(BF16))


def _gdn_prompt_layer(x, g, p, conv0, s0, *, batch, seq, heads, dk, dv, norm=True, residual=True):
    pm = _nmm(x, g, p["w_main"], norm=norm, tm=512, tn=512)
    pba = _nmm(x, g, p["w_ba"], norm=norm, tm=512, tn=128)
    conv0 = jnp.pad(conv0, ((0, 0), (8 - (CONV_W - 1), 0), (0, 0)))
    o, conv, s = _gdn_prompt(pm, pba, conv0, s0, p["cw_t"], p["gp"], p["nw"],
                             batch=batch, seq=seq, heads=heads, dk=dk, dv=dv, tb=256)
    ones = jnp.ones((1, o.shape[1]), F32)
    y = _nmm(o, ones, p["w_out"], x if residual else None, norm=False, tm=512, tn=512)
    return y, conv[:, 8 - (CONV_W - 1):, :], s


def _trunk(x, states, w, *, batch, seq):
    conv_in, gs_in, c_in, n_in, m_in, shift_in, rs_in = states
    depth = w["norm_mix"].shape[0]
    gh, gdk, gdv = gs_in.shape[2:]
    mh, mdk, mdv = c_in.shape[2:]
    rh, rhd = rs_in.shape[2:4]
    prompt = seq > 1
    tm_ffn = 1024 if prompt else 128
    outs = [[] for _ in range(7)]
    for i in range(depth):
        j = i // 3
        g = w["norm_mix"][i][None, :]
        if i % 3 == 0:
            p = w["gdn"][j]
            if prompt:
                x, cb, s = _gdn_prompt_layer(x, g, p, conv_in[j], gs_in[j], batch=batch, seq=seq,
                                             heads=gh, dk=gdk, dv=gdv)
            else:
                x, cb, s = _gdn_sample_layer(x, g, p, conv_in[j], gs_in[j], heads=gh, dk=gdk, dv=gdv)
            outs[0].append(cb)
            outs[1].append(s)
        elif i % 3 == 1:
            p = w["ml"][j]
            if prompt:
                x, c, n, m = _mlstm_prompt_layer(x, g, p, c_in[j], n_in[j], m_in[j], batch=batch, seq=seq,
                                                 heads=mh, dk=mdk, dv=mdv)
            else:
                x, c, n, m = _mlstm_sample_layer(x, g, p, c_in[j], n_in[j], m_in[j], heads=mh, dk=mdk, dv=mdv)
            outs[2].append(c)
            outs[3].append(n)
            outs[4].append(m)
        else:
            p = w["rw"][j]
            if prompt:
                x, sh, s = _rwkv_prompt_layer(x, g, p, shift_in[j], rs_in[j], batch=batch, seq=seq,
                                              heads=rh, hd=rhd)
            else:
                x, sh, s = _rwkv_sample_layer(x, g, p, shift_in[j], rs_in[j], heads=rh, hd=rhd)
            outs[5].append(sh)
            outs[6].append(s)
        x = _ffn(x, w["norm_ffn"][i][None, :], w["ffn_w1"][i], w["ffn_w2"][i], tm=tm_ffn, tf=512)
    y = _rmsnorm(x, w["norm_final"][None, :], tm=tm_ffn)
    return y, tuple(jnp.stack(z, axis=0) for z in outs)


def kernel(x_prompt, x_sample, state_gdn_conv, state_gdn_S, state_mlstm_C, state_mlstm_n, state_mlstm_m, state_rwkv_shift, state_rwkv_S, norm_mix, norm_ffn, norm_final, gdn_w_in, gdn_conv_w, gdn_a_log, gdn_dt_bias, gdn_norm_w, gdn_w_out, ml_w_in, ml_b_if, ml_norm_w, ml_w_out, rw_mu, rw_w_rkv, rw_w_o, rw_w0, rw_w1, rw_w2, rw_a0, rw_a1, rw_a2, rw_g1, rw_g2, rw_k_k, rw_k_a, rw_r_k, rw_lnx_w, rw_lnx_b, ffn_w1, ffn_w2):
    gh, gdk, gdv = state_gdn_S.shape[2:]
    mh, mdk, mdv = state_mlstm_C.shape[2:]
    w = dict(
        norm_mix=norm_mix, norm_ffn=norm_ffn, norm_final=norm_final,
        ffn_w1=ffn_w1.astype(BF16), ffn_w2=ffn_w2.astype(BF16),
        gdn=[_gdn_prep(gdn_w_in[j], gdn_conv_w[j], gdn_a_log[j], gdn_dt_bias[j], gdn_norm_w[j], gdn_w_out[j],
                       heads=gh, dk=gdk, dv=gdv) for j in range(gdn_w_in.shape[0])],
        ml=[_mlstm_prep(ml_w_in[j], ml_b_if[j], ml_norm_w[j], ml_w_out[j], heads=mh, dk=mdk, dv=mdv)
            for j in range(ml_w_in.shape[0])],
        rw=[_rwkv_prep(rw_mu[j], rw_w_rkv[j], rw_w_o[j], rw_w0[j], rw_w1[j], rw_w2[j], rw_a0[j], rw_a1[j],
                       rw_a2[j], rw_g1[j], rw_g2[j], rw_k_k[j], rw_k_a[j], rw_r_k[j], rw_lnx_w[j], rw_lnx_b[j])
            for j in range(rw_mu.shape[0])])
    sample_states = (state_gdn_conv, state_gdn_S, state_mlstm_C, state_mlstm_n, state_mlstm_m,
                     state_rwkv_shift, state_rwkv_S)
    bp, tp, d = x_prompt.shape
    bs, ts, _ = x_sample.shape
    assert ts == 1
    prompt_states = tuple(jnp.zeros((s.shape[0], bp) + s.shape[2:], s.dtype) for s in sample_states)
    y_p, new_p = _trunk(x_prompt.reshape(bp * tp, d), prompt_states, w, batch=bp, seq=tp)
    y_s, new_s = _trunk(x_sample.reshape(bs * ts, d), sample_states, w, batch=bs, seq=ts)
    out = [y_p.reshape(bp, tp, d), y_s.reshape(bs, ts, d)]
    for a, b in zip(new_p, new_s):
        out += [a, b]
    return tuple(out)
```

```python
import functools
import math

import jax
import jax.numpy as jnp
from jax import lax
from jax.experimental import pallas as pl
from jax.experimental.pallas import tpu as pltpu

F32 = jnp.float32
BF16 = jnp.bfloat16

RMS_EPS = 1e-6
NEG_BIG = -1e30
GATE_CAP = 15.0
RW_GN_EPS = 64e-5
CONV_W = 4
CHUNK = 64
RW_GROUP = 8
V7X_VMEM_LIMIT = 56 * 1024 * 1024
HI = lax.Precision.HIGHEST


def _cparams(sem):
    return pltpu.CompilerParams(dimension_semantics=sem, vmem_limit_bytes=V7X_VMEM_LIMIT)


def _dot(a, b):
    return jnp.dot(a.astype(BF16), b.astype(BF16), preferred_element_type=F32)


def _dot_nt(a, b):
    return lax.dot_general(a.astype(BF16), b.astype(BF16), (((1,), (1,)), ((), ())),
                           preferred_element_type=F32)


def _dot_tn(a, b):
    return lax.dot_general(a.astype(BF16), b.astype(BF16), (((0,), (0,)), ((), ())),
                           preferred_element_type=F32)


def _dot_hi(a, b):
    return jnp.dot(a, b, preferred_element_type=F32, precision=HI)


def _sigmoid(x):
    return 1.0 / (1.0 + jnp.exp(-x))


def _silu(x):
    return x * _sigmoid(x)


def _softplus(x):
    return jnp.maximum(x, 0.0) + jnp.log(1.0 + jnp.exp(-jnp.abs(x)))


def _log_sigmoid(x):
    return -_softplus(-x)


def _tri_masks(l):
    r = lax.broadcasted_iota(jnp.int32, (l, l), 0)
    c = lax.broadcasted_iota(jnp.int32, (l, l), 1)
    return r >= c, r > c


INV_BASE = 16


def _unit_lower_inverse(mats, l):
    n = mats[0].shape[0]
    r = lax.broadcasted_iota(jnp.int32, (n, n), 0)
    c = lax.broadcasted_iota(jnp.int32, (n, n), 1)
    eye = (r == c).astype(F32)
    size = min(INV_BASE, l)
    shift = size.bit_length() - 1
    diag = (r >> shift) == (c >> shift)
    a_d = [jnp.where(diag, a, 0.0) for a in mats]
    t = [eye - a for a in a_d]
    p = [_dot(a, a) for a in a_d]
    n = 2
    while n < size:
        tp = [_dot(ti, pi) for ti, pi in zip(t, p)]
        n *= 2
        if n < size:
            p = [_dot(pi, pi) for pi in p]
        t = [ti + d for ti, d in zip(t, tp)]
    while size < l:
        shift = size.bit_length() - 1
        off = ((r >> (shift + 1)) == (c >> (shift + 1))) & ((r >> shift) > (c >> shift))
        x = [_dot(jnp.where(off, a, 0.0), ti) for a, ti in zip(mats, t)]
        y = [_dot(ti, xi) for ti, xi in zip(t, x)]
        t = [ti - yi for ti, yi in zip(t, y)]
        size *= 2
    return t


def _nmm_kernel(*refs, norm, residual, aux):
    x_ref, g_ref, w_ref = refs[:3]
    pos = 3
    res_ref = aux_w_ref = aux_o_ref = None
    if residual:
        res_ref = refs[pos]
        pos += 1
    if aux:
        aux_w_ref = refs[pos]
        pos += 1
    o_ref = refs[pos]
    if aux:
        aux_o_ref = refs[pos + 1]
    xn_ref = refs[-1]

    @pl.when(pl.program_id(1) == 0)
    def _():
        x = x_ref[...]
        if norm:
            x = x * lax.rsqrt(jnp.mean(x * x, axis=-1, keepdims=True) + RMS_EPS) * g_ref[...]
        xn_ref[...] = x.astype(BF16)
        if aux:
            aux_o_ref[...] = jnp.dot(xn_ref[...], aux_w_ref[...], preferred_element_type=F32)

    y = jnp.dot(xn_ref[...], w_ref[...], preferred_element_type=F32)
    if residual:
        y = res_ref[...] + y
    o_ref[...] = y


def _nmm(x, g, w, res=None, w_aux=None, *, norm, tm, tn):
    m, k = x.shape
    n = w.shape[1]
    tm, tn = min(tm, m), min(tn, n)
    assert m % tm == 0 and n % tn == 0
    in_specs = [pl.BlockSpec((tm, k), lambda i, j: (i, 0)),
                pl.BlockSpec((1, k), lambda i, j: (0, 0)),
                pl.BlockSpec((k, tn), lambda i, j: (0, j))]
    args = [x, g, w]
    out_specs = [pl.BlockSpec((tm, tn), lambda i, j: (i, j))]
    out_shape = [jax.ShapeDtypeStruct((m, n), F32)]
    if res is not None:
        in_specs.append(pl.BlockSpec((tm, tn), lambda i, j: (i, j)))
        args.append(res)
    if w_aux is not None:
        na = w_aux.shape[1]
        in_specs.append(pl.BlockSpec((k, na), lambda i, j: (0, 0)))
        args.append(w_aux)
        out_specs.append(pl.BlockSpec((tm, na), lambda i, j: (i, 0)))
        out_shape.append(jax.ShapeDtypeStruct((m, na), F32))
    out = pl.pallas_call(
        functools.partial(_nmm_kernel, norm=norm, residual=res is not None, aux=w_aux is not None),
        grid=(m // tm, n // tn),
        in_specs=in_specs,
        out_specs=out_specs,
        out_shape=out_shape,
        scratch_shapes=[pltpu.VMEM((tm, k), BF16)],
        compiler_params=_cparams(("parallel", "arbitrary")),
        name="nmm",
    )(*args)
    return out if w_aux is not None else out[0]


TM_IN_PROJ = 512
TM_OUT_PROJ = 1024
TM_FFN, TF_FFN = 1024, 512
TB_MIX = 256


def _in_proj(x, g, w_main, w_aux, *, norm=True):
    return _nmm(x, g, w_main, None, w_aux, norm=norm, tm=TM_IN_PROJ, tn=w_main.shape[1])


def _out_proj(o, w_out, res):
    ones = jnp.ones((1, o.shape[1]), F32)
    return _nmm(o, ones, w_out, res, norm=False, tm=TM_OUT_PROJ, tn=w_out.shape[1])


def _ffn_kernel(x_ref, g_ref, w1_ref, w2_ref, o_ref, xn_ref, acc_ref):
    f = pl.program_id(1)

    @pl.when(f == 0)
    def _():
        x = x_ref[...]
        x = x * lax.rsqrt(jnp.mean(x * x, axis=-1, keepdims=True) + RMS_EPS) * g_ref[...]
        xn_ref[...] = x.astype(BF16)
        acc_ref[...] = jnp.zeros_like(acc_ref)

    h = jnp.dot(xn_ref[...], w1_ref[...], preferred_element_type=F32)
    a = jnp.square(jnp.maximum(h, 0.0)).astype(BF16)
    acc_ref[...] += jnp.dot(a, w2_ref[...], preferred_element_type=F32)

    @pl.when(f == pl.num_programs(1) - 1)
    def _():
        o_ref[...] = x_ref[...] + acc_ref[...]


def _ffn(x, g, w1, w2, *, tm, tf):
    m, d = x.shape
    dff = w1.shape[1]
    tm, tf = min(tm, m), min(tf, dff)
    assert m % tm == 0 and dff % tf == 0
    return pl.pallas_call(
        _ffn_kernel,
        grid=(m // tm, dff // tf),
        in_specs=[pl.BlockSpec((tm, d), lambda i, j: (i, 0)),
                  pl.BlockSpec((1, d), lambda i, j: (0, 0)),
                  pl.BlockSpec((d, tf), lambda i, j: (0, j)),
                  pl.BlockSpec((tf, d), lambda i, j: (j, 0))],
        out_specs=pl.BlockSpec((tm, d), lambda i, j: (i, 0)),
        out_shape=jax.ShapeDtypeStruct((m, d), F32),
        scratch_shapes=[pltpu.VMEM((tm, d), BF16), pltpu.VMEM((tm, d), F32)],
        compiler_params=_cparams(("parallel", "arbitrary")),
        name="ffn",
    )(x, g, w1, w2)


def _norm_kernel(x_ref, g_ref, o_ref):
    x = x_ref[...]
    o_ref[...] = x * lax.rsqrt(jnp.mean(x * x, axis=-1, keepdims=True) + RMS_EPS) * g_ref[...]


def _rmsnorm(x, g, *, tm):
    m, d = x.shape
    tm = min(tm, m)
    assert m % tm == 0
    return pl.pallas_call(
        _norm_kernel,
        grid=(m // tm,),
        in_specs=[pl.BlockSpec((tm, d), lambda i: (i, 0)), pl.BlockSpec((1, d), lambda i: (0, 0))],
        out_specs=pl.BlockSpec((tm, d), lambda i: (i, 0)),
        out_shape=jax.ShapeDtypeStruct((m, d), F32),
        compiler_params=_cparams(("parallel",)),
        name="rmsnorm",
    )(x, g)


def _gdn_prompt_kernel(pm_ref, pba_ref, conv0_ref, s0_ref, cw_ref, gp_ref, nw_ref,
                       o_ref, conv_out_ref, s_out_ref,
                       full_ref, qkv_ref, s_ref, u_ref, w_ref, qd_ref, kd_ref, qk_ref, gl_ref,
                       *, tb, heads, dk, dv):
    t = pl.program_id(1)
    key = heads * dk
    ch = 2 * key + heads * dv
    l = CHUNK

    @pl.when(t == 0)
    def _():
        full_ref[0:8, :] = conv0_ref[0]
        s_ref[...] = s0_ref[0]

    full_ref[8:8 + tb, :] = pm_ref[:, 0:ch]
    for c in range(ch // 128):
        cs = slice(c * 128, (c + 1) * 128)
        y = full_ref[5:5 + tb, cs] * cw_ref[0:1, cs]
        for j in range(1, CONV_W):
            y = y + full_ref[5 + j:5 + j + tb, cs] * cw_ref[j:j + 1, cs]
        y = _silu(y)
        if c * 128 < key:
            y = y * lax.rsqrt(jnp.sum(y * y, axis=-1, keepdims=True) + 1e-6) * (dk ** -0.5)
        elif c * 128 < 2 * key:
            y = y * lax.rsqrt(jnp.sum(y * y, axis=-1, keepdims=True) + 1e-6)
        qkv_ref[:, cs] = y
    full_ref[0:8, :] = full_ref[tb:tb + 8, :]

    a_log = gp_ref[0:1, :]
    dt_bias = gp_ref[1:2, :]
    hh = range(heads)
    l2 = 2 * l
    r2 = lax.broadcasted_iota(jnp.int32, (l2, l2), 0)
    c2 = lax.broadcasted_iota(jnp.int32, (l2, l2), 1)
    same = (r2 >= l) == (c2 >= l)
    incl = same & (r2 >= c2)
    strict = same & (r2 > c2)
    tril = incl.astype(F32)
    first = lax.broadcasted_iota(jnp.int32, (l2, 128), 0) < l

    pairs = range(tb // l2)
    rows = [slice(pi * l2, (pi + 1) * l2) for pi in pairs]
    beta_all, gc, gc_t, g_end = [], [], [], []
    for pi in pairs:
        ba = pba_ref[rows[pi], :]
        beta_all.append(_sigmoid(ba))
        g_all = -jnp.exp(a_log) * _softplus(ba + dt_bias)
        gc.append(_dot_hi(tril, g_all))
        gc_t.append(gc[pi].T)
        g_end.append(jnp.where(first, gc[pi][l - 1:l, :], gc[pi][l2 - 1:l2, :]))
        gl_ref[2 * pi:2 * pi + 1, :] = jnp.exp(gc[pi][l - 1:l, :])
        gl_ref[2 * pi + 1:2 * pi + 2, :] = jnp.exp(gc[pi][l2 - 1:l2, :])
    cc = [(pi, h) for pi in pairs for h in hh]
    b_col = [beta_all[pi][:, h:h + 1] for pi, h in cc]
    gi = [gc[pi][:, heads + h:heads + h + 1] for pi, h in cc]
    q = [qkv_ref[rows[pi], h * dk:(h + 1) * dk] for pi, h in cc]
    k = [qkv_ref[rows[pi], key + h * dk:key + (h + 1) * dk] for pi, h in cc]
    v = [qkv_ref[rows[pi], 2 * key + h * dv:2 * key + (h + 1) * dv] for pi, h in cc]
    nc = range(len(cc))
    dmat = [jnp.where(incl, jnp.exp(jnp.where(incl, gi[i] - gc_t[pi][heads + h:heads + h + 1, :], 0.0)), 0.0)
            for i, (pi, h) in enumerate(cc)]
    kb = [k[i] * b_col[i] for i in nc]
    kk = [_dot_nt(kb[i], k[i]) for i in nc]
    qk = [_dot_nt(q[i], k[i]) for i in nc]
    t_inv = _unit_lower_inverse([jnp.where(strict, kk[i] * dmat[i], 0.0) for i in nc], l)
    egi = [jnp.exp(gi[i]) for i in nc]
    sol = [_dot(t_inv[i], jnp.concatenate([v[i] * b_col[i], kb[i] * egi[i]], axis=-1)) for i in nc]
    for i, (pi, h) in enumerate(cc):
        hs = slice(h * dk, (h + 1) * dk)
        u_ref[rows[pi], h * dv:(h + 1) * dv] = sol[i][:, :dv]
        w_ref[rows[pi], hs] = sol[i][:, dv:].astype(BF16)
        qd_ref[rows[pi], hs] = (q[i] * egi[i]).astype(BF16)
        kd_ref[rows[pi], hs] = (k[i] * jnp.exp(g_end[pi][:, heads + h:heads + h + 1] - gi[i])).astype(BF16)
        qkm = jnp.where(incl, qk[i] * dmat[i], 0.0).astype(BF16)
        qk_ref[h, pi * l2:pi * l2 + l, :] = qkm[:l, :l]
        qk_ref[h, pi * l2 + l:(pi + 1) * l2, :] = qkm[l:, l:]

    for ci in range(tb // l):
        rows = slice(ci * l, (ci + 1) * l)
        s = [s_ref[h] for h in hh]
        wq = [_dot(jnp.concatenate([w_ref[rows, h * dk:(h + 1) * dk], qd_ref[rows, h * dk:(h + 1) * dk]], axis=0),
                   s[h]) for h in hh]
        v_new = [u_ref[rows, h * dv:(h + 1) * dv] - wq[h][:l] for h in hh]
        o2 = [_dot(qk_ref[h, rows, :], v_new[h]) for h in hh]
        ds = [_dot_tn(kd_ref[rows, h * dk:(h + 1) * dk], v_new[h]) for h in hh]
        for h in hh:
            s_ref[h] = s[h] * gl_ref[ci:ci + 1, heads + h:heads + h + 1] + ds[h]
            o = wq[h][l:] + o2[h]
            z = pm_ref[rows, ch + h * dv:ch + (h + 1) * dv]
            o = o * lax.rsqrt(jnp.mean(o * o, axis=-1, keepdims=True) + RMS_EPS) * nw_ref[...]
            o_ref[rows, h * dv:(h + 1) * dv] = o * _silu(z)

    @pl.when(t == pl.num_programs(1) - 1)
    def _():
        conv_out_ref[0] = full_ref[0:8, :]
        s_out_ref[0] = s_ref[...]


def _gdn_prompt(pm, pba, conv0, s0, cw_t, gp, nw, *, batch, seq, heads, dk, dv, tb):
    key, val = heads * dk, heads * dv
    ch = 2 * key + val
    tb = min(tb, seq)
    assert seq % tb == 0 and tb % CHUNK == 0
    nt = seq // tb
    return pl.pallas_call(
        functools.partial(_gdn_prompt_kernel, tb=tb, heads=heads, dk=dk, dv=dv),
        grid=(batch, nt),
        in_specs=[pl.BlockSpec((tb, ch + val), lambda b, t: (b * nt + t, 0)),
                  pl.BlockSpec((tb, 128), lambda b, t: (b * nt + t, 0)),
                  pl.BlockSpec((1, 8, ch), lambda b, t: (b, 0, 0)),
                  pl.BlockSpec((1, heads, dk, dv), lambda b, t: (b, 0, 0, 0)),
                  pl.BlockSpec((8, ch), lambda b, t: (0, 0)),
                  pl.BlockSpec((8, 128), lambda b, t: (0, 0)),
                  pl.BlockSpec((1, dv), lambda b, t: (0, 0))],
        out_specs=[pl.BlockSpec((tb, val), lambda b, t: (b * nt + t, 0)),
                   pl.BlockSpec((1, 8, ch), lambda b, t: (b, 0, 0)),
                   pl.BlockSpec((1, heads, dk, dv), lambda b, t: (b, 0, 0, 0))],
        out_shape=[jax.ShapeDtypeStruct((batch * seq, val), F32),
                   jax.ShapeDtypeStruct((batch, 8, ch), F32),
                   jax.ShapeDtypeStruct((batch, heads, dk, dv), F32)],
        scratch_shapes=[pltpu.VMEM((tb + 8, ch), F32), pltpu.VMEM((tb, ch), F32),
                        pltpu.VMEM((heads, dk, dv), F32),
                        pltpu.VMEM((tb, val), F32), pltpu.VMEM((tb, key), BF16),
                        pltpu.VMEM((tb, key), BF16), pltpu.VMEM((tb, key), BF16),
                        pltpu.VMEM((heads, tb, CHUNK), BF16), pltpu.VMEM((max(8, tb // CHUNK), 128), F32)],
        compiler_params=_cparams(("parallel", "arbitrary")),
        name="gdn_prompt",
    )(pm, pba, conv0, s0, cw_t, gp, nw)


def _mlstm_prompt_kernel(pm_ref, pif_ref, bif_ref, c0_ref, n0_ref, m0_ref, nw_ref,
                         o_ref, c_out_ref, n_out_ref, m_out_ref,
                         c_ref, n_ref, m_ref, *, tb, heads, dk, dv):
    t = pl.program_id(1)
    l = CHUNK
    qk_w = heads * dk
    v_off = 2 * qk_w
    o_off = v_off + heads * dv

    @pl.when(t == 0)
    def _():
        c_ref[...] = c0_ref[0]
        n_ref[...] = n0_ref[0]
        m_ref[...] = m0_ref[0]

    incl, _ = _tri_masks(l)
    tril = incl.astype(F32)

    def chunk(ci, carry):
        r0 = pl.multiple_of(ci * l, l)
        rows = pl.ds(r0, l)
        gates = pif_ref[rows, :] + bif_ref[...]
        gates = GATE_CAP * jnp.tanh(gates / GATE_CAP)
        bcum = _dot_hi(tril, _log_sigmoid(gates))
        bcum_t = bcum.T
        gates_t = gates.T
        hh = range(heads)
        bi = [bcum[:, heads + h:heads + h + 1] for h in hh]
        ii = [gates[:, h:h + 1] for h in hh]
        b_last = [bcum[l - 1:l, heads + h:heads + h + 1] for h in hh]
        q = [pm_ref[rows, h * dk:(h + 1) * dk] * (dk ** -0.5) for h in hh]
        k = [pm_ref[rows, qk_w + h * dk:qk_w + (h + 1) * dk] for h in hh]
        v = [pm_ref[rows, v_off + h * dv:v_off + (h + 1) * dv] for h in hh]
        qk = [_dot_nt(q[h], k[h]) for h in hh]
        c_mat = [c_ref[h] for h in hh]
        qc = [_dot(q[h], c_mat[h]) for h in hh]
        dlog = [jnp.where(incl, bi[h] - bcum_t[heads + h:heads + h + 1, :] + gates_t[h:h + 1, :], NEG_BIG)
                for h in hh]
        m_intra = [jnp.max(dlog[h], axis=-1, keepdims=True) for h in hh]
        p = [jnp.where(incl, jnp.exp(dlog[h] - m_intra[h]), 0.0) * qk[h] for h in hh]
        num_intra = [_dot(p[h], v[h]) for h in hh]
        a_log = [b_last[h] - bi[h] + ii[h] for h in hh]
        m_chunk = [jnp.max(a_log[h], axis=0, keepdims=True) for h in hh]
        kw = [k[h] * jnp.exp(a_log[h] - m_chunk[h]) for h in hh]
        kv_chunk = [_dot_tn(kw[h], v[h]) for h in hh]
        den_intra = [jnp.sum(p[h], axis=-1, keepdims=True) for h in hh]
        m_prev = [m_ref[h:h + 1, 0:1] for h in hh]
        n_vec = [n_ref[h:h + 1, :] for h in hh]
        qn = [jnp.sum(q[h] * n_vec[h], axis=-1, keepdims=True) for h in hh]
        m_t = [jnp.maximum(bi[h] + m_prev[h], m_intra[h]) for h in hh]
        s_inter = [jnp.exp(bi[h] + m_prev[h] - m_t[h]) for h in hh]
        s_intra = [jnp.exp(m_intra[h] - m_t[h]) for h in hh]
        den = [s_inter[h] * qn[h] + s_intra[h] * den_intra[h] for h in hh]
        h_t = [(s_inter[h] * qc[h] + s_intra[h] * num_intra[h])
               / jnp.maximum(jnp.abs(den[h]), jnp.exp(-m_t[h])) for h in hh]
        ms = [jnp.mean(h_t[h] * h_t[h], axis=-1, keepdims=True) for h in hh]
        for h in hh:
            k_chunk = jnp.sum(kw[h], axis=0, keepdims=True)
            m_new = jnp.maximum(b_last[h] + m_prev[h], m_chunk[h])
            f_s = jnp.exp(b_last[h] + m_prev[h] - m_new)
            i_s = jnp.exp(m_chunk[h] - m_new)
            c_ref[h] = f_s * c_mat[h] + i_s * kv_chunk[h]
            n_ref[h:h + 1, :] = f_s * n_vec[h] + i_s * k_chunk
            m_ref[h:h + 1, :] = jnp.broadcast_to(m_new, (1, m_ref.shape[1]))
            h_n = h_t[h] * lax.rsqrt(ms[h] + RMS_EPS) * nw_ref[:, h * dv:(h + 1) * dv]
            o_pre = pm_ref[rows, o_off + h * dv:o_off + (h + 1) * dv]
            o_ref[rows, h * dv:(h + 1) * dv] = _sigmoid(o_pre) * h_n
        return carry

    lax.fori_loop(0, tb // l, chunk, 0)

    @pl.when(t == pl.num_programs(1) - 1)
    def _():
        c_out_ref[0] = c_ref[...]
        n_out_ref[0] = n_ref[...]
        m_out_ref[0] = m_ref[...]


def _mlstm_prompt(pm, pif, bif, c0, n0, m0, nw, *, batch, seq, heads, dk, dv, tb):
    width = pm.shape[1]
    val = heads * dv
    tb = min(tb, seq)
    assert seq % tb == 0 and tb % CHUNK == 0
    nt = seq // tb
    return pl.pallas_call(
        functools.partial(_mlstm_prompt_kernel, tb=tb, heads=heads, dk=dk, dv=dv),
        grid=(batch, nt),
        in_specs=[pl.BlockSpec((tb, width), lambda b, t: (b * nt + t, 0)),
                  pl.BlockSpec((tb, 128), lambda b, t: (b * nt + t, 0)),
                  pl.BlockSpec((1, 128), lambda b, t: (0, 0)),
                  pl.BlockSpec((1, heads, dk, dv), lambda b, t: (b, 0, 0, 0)),
                  pl.BlockSpec((1, 8, dk), lambda b, t: (b, 0, 0)),
                  pl.BlockSpec((1, 8, 128), lambda b, t: (b, 0, 0)),
                  pl.BlockSpec((1, val), lambda b, t: (0, 0))],
        out_specs=[pl.BlockSpec((tb, val), lambda b, t: (b * nt + t, 0)),
                   pl.BlockSpec((1, heads, dk, dv), lambda b, t: (b, 0, 0, 0)),
                   pl.BlockSpec((1, 8, dk), lambda b, t: (b, 0, 0)),
                   pl.BlockSpec((1, 8, 128), lambda b, t: (b, 0, 0))],
        out_shape=[jax.ShapeDtypeStruct((batch * seq, val), F32),
                   jax.ShapeDtypeStruct((batch, heads, dk, dv), F32),
                   jax.ShapeDtypeStruct((batch, 8, dk), F32),
                   jax.ShapeDtypeStruct((batch, 8, 128), F32)],
        scratch_shapes=[pltpu.VMEM((heads, dk, dv), F32), pltpu.VMEM((8, dk), F32),
                        pltpu.VMEM((8, 128), F32)],
        compiler_params=_cparams(("parallel", "arbitrary")),
        name="mlstm_prompt",
    )(pm, pif, bif, c0, n0, m0, nw)


def _mlstm_prep(w_in, b_if, norm_w, w_out, *, heads, dk, dv):
    main = 2 * heads * dk + 2 * heads * dv
    return dict(w_main=w_in[:, :main].astype(BF16),
                w_if=_pad_cols(w_in[:, main:], 128).astype(BF16),
                bif=_pad_cols(b_if[None, :], 128), nw=norm_w[None, :], w_out=w_out.astype(BF16))


def _mlstm_prompt_layer(x, g, p, c0, n0, m0, *, batch, seq, heads, dk, dv, norm=True, residual=True):
    pm, pif = _in_proj(x, g, p["w_main"], p["w_if"], norm=norm)
    n0p = jnp.pad(n0, ((0, 0), (0, 8 - heads), (0, 0)))
    m0p = jnp.broadcast_to(jnp.pad(m0, ((0, 0), (0, 8 - heads)))[:, :, None], (batch, 8, 128))
    o, c, n, m = _mlstm_prompt(pm, pif, p["bif"], c0, n0p, m0p, p["nw"],
                               batch=batch, seq=seq, heads=heads, dk=dk, dv=dv, tb=TB_MIX)
    y = _out_proj(o, p["w_out"], x if residual else None)
    return y, c, n[:, :heads, :], m[:, :heads, 0]


def _rwkv_proj_kernel(h_ref, prev_ref, mu_ref, wrkv_ref, w1_ref, w2_ref, a1_ref, a2_ref, g1_ref, g2_ref,
                      vec_ref, r_ref, k_ref, v_ref, lw_ref, kk_ref, a_ref, g_ref):
    h = h_ref[...]
    xx = prev_ref[...] - h

    def mix(j):
        return (h + xx * mu_ref[j:j + 1, :]).astype(BF16)

    w0, a0, k_k, k_a = (vec_ref[j:j + 1, :] for j in range(4))
    r_ref[...] = jnp.dot(mix(0), wrkv_ref[0], preferred_element_type=F32)
    lora_w = _dot(jnp.tanh(_dot(mix(1), w1_ref[...])), w2_ref[...])
    w_log = -_softplus(-(w0 + lora_w)) - 0.5
    lw_ref[...] = -jnp.exp(w_log)
    k = jnp.dot(mix(2), wrkv_ref[1], preferred_element_type=F32)
    v_ref[...] = jnp.dot(mix(3), wrkv_ref[2], preferred_element_type=F32)
    a = _sigmoid(a0 + _dot(_dot(mix(4), a1_ref[...]), a2_ref[...]))
    g_ref[...] = _dot(_sigmoid(_dot(mix(5), g1_ref[...])), g2_ref[...])
    kk_ref[...] = k * k_k
    k_ref[...] = k * (1.0 + (a - 1.0) * k_a)
    a_ref[...] = a


def _rwkv_proj(hn, prev, p, *, tm):
    m, d = hn.shape
    tm = min(tm, m)
    assert m % tm == 0
    row = pl.BlockSpec((tm, d), lambda i: (i, 0))

    def full(a):
        nd = a.ndim
        return pl.BlockSpec(a.shape, lambda i: (0,) * nd)

    consts = [p["mu"], p["w_rkv"], p["w1"], p["w2"], p["a1"], p["a2"], p["g1"], p["g2"], p["vec"]]
    return pl.pallas_call(
        _rwkv_proj_kernel,
        grid=(m // tm,),
        in_specs=[row, row] + [full(a) for a in consts],
        out_specs=[row] * 7,
        out_shape=[jax.ShapeDtypeStruct((m, d), F32)] * 7,
        compiler_params=_cparams(("parallel",)),
        name="rwkv_proj",
    )(hn, prev, *consts)


def _rwkv_post(y, r, k, v, g, r_k, lnx_w, lnx_b):
    mean = jnp.mean(y, axis=-1, keepdims=True)
    yc = y - mean
    var = jnp.mean(yc * yc, axis=-1, keepdims=True)
    y = yc * lax.rsqrt(var + RW_GN_EPS) * lnx_w + lnx_b
    y = y + jnp.sum(r * k * r_k, axis=-1, keepdims=True) * v
    return y * g


def _rwkv_prompt_kernel(r_ref, k_ref, v_ref, lw_ref, kk_ref, a_ref, g_ref, s0_ref, hp_ref,
                        o_ref, s_out_ref, s_ref, rr_ref, yy_ref, mx_ref, n0_ref, gw_ref, gb_ref, el_ref,
                        *, tb, heads, hd):
    t = pl.program_id(1)
    l = CHUNK

    @pl.when(t == 0)
    def _():
        s_ref[...] = s0_ref[0]

    hh = range(heads)
    hs = [slice(h * hd, (h + 1) * hd) for h in hh]
    l2 = 2 * l
    r2 = lax.broadcasted_iota(jnp.int32, (l2, l2), 0)
    c2 = lax.broadcasted_iota(jnp.int32, (l2, l2), 1)
    same = (r2 >= l) == (c2 >= l)
    incl = same & (r2 >= c2)
    strict = same & (r2 > c2)
    tril = incl.astype(F32)
    first = lax.broadcasted_iota(jnp.int32, (l2, heads * hd), 0) < l
    zeros = jnp.zeros((l2, hd), F32)

    def pair(pi, carry):
        rows = pl.ds(pl.multiple_of(pi * l2, l2), l2)
        lw = lw_ref[rows, :]
        lwc = _dot_hi(tril, lw)
        lw_end = jnp.where(first, lwc[l - 1:l, :], lwc[l2 - 1:l2, :])
        e_in = jnp.exp(lwc)
        e_prev = jnp.exp(lwc - lw)
        e_neg = jnp.exp(-lwc)
        e_end = jnp.exp(lw_end - lwc)
        r = [r_ref[rows, hs[h]] for h in hh]
        v = [v_ref[rows, hs[h]] for h in hh]
        k = [k_ref[rows, hs[h]] for h in hh]
        kk = [kk_ref[rows, hs[h]] for h in hh]
        kk = [kk[h] * lax.rsqrt(jnp.maximum(jnp.sum(kk[h] * kk[h], axis=-1, keepdims=True), 1e-24)) for h in hh]
        bv = [kk[h] * a_ref[rows, hs[h]] for h in hh]
        a_t = [-kk[h] * e_prev[:, hs[h]] for h in hh]
        r_t = [r[h] * e_in[:, hs[h]] for h in hh]
        gm = [_dot_nt(jnp.concatenate([a_t[h], r_t[h]], axis=0),
                      jnp.concatenate([bv[h] * e_neg[:, hs[h]], k[h] * e_neg[:, hs[h]]], axis=0)) for h in hh]
        t_inv = _unit_lower_inverse([jnp.where(strict, -gm[h][:l2, :l2], 0.0) for h in hh], l)
        akv = [_dot(jnp.where(strict, gm[h][:l2, l2:], 0.0), v[h]) for h in hh]
        x1 = [_dot(t_inv[h], jnp.concatenate([a_t[h], akv[h]], axis=1)) for h in hh]
        low = [jnp.concatenate([x1[h], jnp.concatenate([zeros, v[h]], axis=1)], axis=0) for h in hh]
        x2 = [_dot(jnp.concatenate([jnp.where(incl, gm[h][l2:, :l2], 0.0),
                                    jnp.where(incl, gm[h][l2:, l2:], 0.0)], axis=1), low[h]) for h in hh]
        for c in range(2):
            cs = slice(c * l, (c + 1) * l)
            cs2 = slice(l2 + c * l, l2 + (c + 1) * l)
            mn = [_dot_tn(jnp.concatenate([low[h][cs], low[h][cs2]], axis=0),
                          jnp.concatenate([(bv[h] * e_end[:, hs[h]])[cs], (k[h] * e_end[:, hs[h]])[cs]], axis=0))
                  for h in hh]
            crow = pl.ds(pl.multiple_of(pi * l2 + c * l, l), l)
            for h in hh:
                mx_ref[h, crow, :] = mn[h][:hd].astype(BF16)
                n0_ref[h, crow, :] = mn[h][hd:]
        for h in hh:
            rr_ref[h, rows, :] = (r_t[h] + x2[h][:, :hd]).astype(BF16)
            yy_ref[h, rows, :] = x2[h][:, hd:]
            g = g_ref[rows, hs[h]]
            bonus = jnp.sum(r[h] * k[h] * hp_ref[0:1, hs[h]], axis=-1, keepdims=True) * v[h]
            gw_ref[h, rows, :] = hp_ref[1:2, hs[h]] * g
            gb_ref[h, rows, :] = (hp_ref[2:3, hs[h]] + bonus) * g
            el_ref[h, pl.ds(pi * 2, 1), :] = jnp.exp(lwc[l - 1:l, hs[h]])
            el_ref[h, pl.ds(pi * 2 + 1, 1), :] = jnp.exp(lwc[l2 - 1:l2, hs[h]])
        return carry

    lax.fori_loop(0, tb // l2, pair, 0)

    for ci in range(tb // l):
        rows = slice(ci * l, (ci + 1) * l)
        s = [s_ref[h] for h in hh]
        y = [_dot_nt(rr_ref[h, rows, :], s[h]) for h in hh]
        sm = [_dot(s[h], mx_ref[h, rows, :]) for h in hh]
        for h in hh:
            s_ref[h] = s[h] * el_ref[h, ci:ci + 1, :] + sm[h] + n0_ref[h, rows, :]
        y = [y[h] + yy_ref[h, rows, :] for h in hh]
        mean = [jnp.mean(y[h], axis=-1, keepdims=True) for h in hh]
        yc = [y[h] - mean[h] for h in hh]
        var = [jnp.mean(yc[h] * yc[h], axis=-1, keepdims=True) for h in hh]
        for h in hh:
            o_ref[rows, hs[h]] = yc[h] * lax.rsqrt(var[h] + RW_GN_EPS) * gw_ref[h, rows, :] + gb_ref[h, rows, :]

    @pl.when(t == pl.num_programs(1) - 1)
    def _():
        s_out_ref[0] = s_ref[...]


def _rwkv_prompt(r, k, v, lw, kk, a, g, s0, hp, *, batch, seq, heads, hd, tb):
    d = heads * hd
    tb = min(tb, seq)
    assert seq % tb == 0 and tb % CHUNK == 0
    nt = seq // tb
    row = pl.BlockSpec((tb, d), lambda b, t: (b * nt + t, 0))
    st = pl.BlockSpec((1, heads, hd, hd), lambda b, t: (b, 0, 0, 0))
    return pl.pallas_call(
        functools.partial(_rwkv_prompt_kernel, tb=tb, heads=heads, hd=hd),
        grid=(batch, nt),
        in_specs=[row] * 7 + [st, pl.BlockSpec((8, d), lambda b, t: (0, 0))],
        out_specs=[row, st],
        out_shape=[jax.ShapeDtypeStruct((batch * seq, d), F32),
                   jax.ShapeDtypeStruct((batch, heads, hd, hd), F32)],
        scratch_shapes=[pltpu.VMEM((heads, hd, hd), F32),
                        pltpu.VMEM((heads, tb, hd), BF16), pltpu.VMEM((heads, tb, hd), F32),
                        pltpu.VMEM((heads, tb, hd), BF16), pltpu.VMEM((heads, tb, hd), F32),
                        pltpu.VMEM((heads, tb, hd), F32), pltpu.VMEM((heads, tb, hd), F32),
                        pltpu.VMEM((heads, max(8, tb // CHUNK), hd), F32)],
        compiler_params=_cparams(("parallel", "arbitrary")),
        name="rwkv_prompt",
    )(r, k, v, lw, kk, a, g, s0, hp)


def _pad_rows(a, n):
    return jnp.pad(a, ((0, n - a.shape[0]), (0, 0)))


def _rwkv_prep(mu, w_rkv, w_o, w0, w1, w2, a0, a1, a2, g1, g2, k_k, k_a, r_k, lnx_w, lnx_b):
    d = w0.shape[0]
    lw = -(-w1.shape[1] // 128) * 128
    la = -(-a1.shape[1] // 128) * 128
    lg = -(-g1.shape[1] // 128) * 128
    return dict(mu=_pad_rows(mu, 8), w_rkv=w_rkv.astype(BF16), w_o=w_o.astype(BF16),
                w1=_pad_cols(w1, lw).astype(BF16), w2=_pad_rows(w2, lw).astype(BF16),
                a1=_pad_cols(a1, la).astype(BF16), a2=_pad_rows(a2, la).astype(BF16),
                g1=_pad_cols(g1, lg).astype(BF16), g2=_pad_rows(g2, lg).astype(BF16),
                vec=_pad_rows(jnp.stack([w0, a0, k_k, k_a]), 8),
                hp=_pad_rows(jnp.stack([r_k.reshape(d), lnx_w, lnx_b]), 8))


def _rwkv_prompt_layer(x, g_norm, p, shift0, s0, *, batch, seq, heads, hd, norm=True, residual=True):
    d = x.shape[1]
    hn = _rmsnorm(x, g_norm, tm=1024) if norm else x
    hn3 = hn.reshape(batch, seq, d)
    prev = jnp.concatenate([shift0[:, None, :], hn3[:, :-1]], axis=1).reshape(batch * seq, d)
    r, k, v, lw, kk, a, g = _rwkv_proj(hn, prev, p, tm=256)
    o, s = _rwkv_prompt(r, k, v, lw, kk, a, g, s0, p["hp"], batch=batch, seq=seq, heads=heads, hd=hd, tb=TB_MIX)
    y = _out_proj(o, p["w_o"], x if residual else None)
    return y, hn3[:, -1], s


def _gdn_sample_pre_kernel(pm_ref, pba_ref, conv_ref, cw_ref, gp_ref, qkv_ref, conv_out_ref, sc_ref,
                           *, heads, dk, dv):
    key = heads * dk
    ch = 2 * key + heads * dv
    u = pm_ref[:, 0:ch]
    y = u * cw_ref[CONV_W - 1:CONV_W, :]
    for j in range(CONV_W - 1):
        y = y + conv_ref[j] * cw_ref[j:j + 1, :]
        conv_out_ref[j] = conv_ref[j + 1] if j + 1 < CONV_W - 1 else u
    y = _silu(y)
    for c in range(ch // 128):
        cs = slice(c * 128, (c + 1) * 128)
        yc = y[:, cs]
        if c * 128 < key:
            yc = yc * lax.rsqrt(jnp.sum(yc * yc, axis=-1, keepdims=True) + 1e-6) * (dk ** -0.5)
        elif c * 128 < 2 * key:
            yc = yc * lax.rsqrt(jnp.sum(yc * yc, axis=-1, keepdims=True) + 1e-6)
        qkv_ref[:, cs] = yc
    ba = pba_ref[...]
    lane = lax.broadcasted_iota(jnp.int32, ba.shape, 1)
    g = -jnp.exp(gp_ref[0:1, :]) * _softplus(ba + gp_ref[1:2, :])
    sc_ref[...] = jnp.where(lane < heads, _sigmoid(ba), jnp.exp(g))


def _gdn_sample_pre(pm, pba, conv_t, cw_t, gp, *, heads, dk, dv):
    n = pm.shape[0]
    ch = 2 * heads * dk + heads * dv
    return pl.pallas_call(
        functools.partial(_gdn_sample_pre_kernel, heads=heads, dk=dk, dv=dv),
        out_shape=[jax.ShapeDtypeStruct((n, ch), F32),
                   jax.ShapeDtypeStruct((CONV_W - 1, n, ch), F32),
                   jax.ShapeDtypeStruct((n, 128), F32)],
        compiler_params=pltpu.CompilerParams(vmem_limit_bytes=V7X_VMEM_LIMIT),
        name="gdn_sample_pre",
    )(pm, pba, conv_t, cw_t, gp)


def _gdn_sample_step_kernel(s0_ref, cols_ref, v_ref, z_ref, sc_ref, nw_ref, s_out_ref, o_ref, *, heads):
    for h in range(heads):
        kc = cols_ref[0, :, h:h + 1]
        qc = cols_ref[0, :, heads + h:heads + h + 1]
        s0 = s0_ref[0, h]
        beta = sc_ref[0, h:h + 1, 0:1]
        eg = sc_ref[0, h:h + 1, 1:2]
        ks = jnp.sum(kc * s0, axis=0, keepdims=True)
        s1 = eg * s0 + kc * (beta * (v_ref[0, h:h + 1, :] - eg * ks))
        s_out_ref[0, h] = s1
        o = jnp.sum(qc * s1, axis=0, keepdims=True)
        o = o * lax.rsqrt(jnp.mean(o * o, axis=-1, keepdims=True) + RMS_EPS) * nw_ref[...]
        o_ref[0, h:h + 1, :] = o * _silu(z_ref[0, h:h + 1, :])


def _gdn_sample_step(s0, cols, v, z, sc, nw):
    n, heads, dk, dv = s0.shape

    def blk(a):
        nd = a.ndim
        return pl.BlockSpec((1,) + a.shape[1:], lambda b: (b,) + (0,) * (nd - 1))

    return pl.pallas_call(
        functools.partial(_gdn_sample_step_kernel, heads=heads),
        grid=(n,),
        in_specs=[blk(s0), blk(cols), blk(v), blk(z), blk(sc), pl.BlockSpec((1, dv), lambda b: (0, 0))],
        out_specs=[blk(s0), blk(v)],
        out_shape=[jax.ShapeDtypeStruct(s0.shape, F32), jax.ShapeDtypeStruct(v.shape, F32)],
        compiler_params=_cparams(("parallel",)),
        name="gdn_sample_step",
    )(s0, cols, v, z, sc, nw)


def _gdn_sample_layer(x, g, p, conv0, s0, *, heads, dk, dv):
    n = x.shape[0]
    key, val = heads * dk, heads * dv
    ch = 2 * key + val
    pm, pba = _in_proj(x, g, p["w_main"], p["w_ba"])
    qkv, conv_t, sc = _gdn_sample_pre(pm, pba, jnp.transpose(conv0, (1, 0, 2)), p["cw_t"], p["gp"],
                                      heads=heads, dk=dk, dv=dv)
    q_c = jnp.transpose(qkv[:, :key].reshape(n, heads, dk), (0, 2, 1))
    k_c = jnp.transpose(qkv[:, key:2 * key].reshape(n, heads, dk), (0, 2, 1))
    cols = jnp.concatenate([k_c, q_c], axis=-1)
    sc3 = jnp.stack([sc[:, :heads], sc[:, heads:2 * heads]], axis=-1)
    s1, o = _gdn_sample_step(s0, cols, qkv[:, 2 * key:].reshape(n, heads, dv),
                             pm[:, ch:].reshape(n, heads, dv), sc3, p["nw"])
    y = _out_proj(o.reshape(n, val), p["w_out"], x)
    return y, jnp.transpose(conv_t, (1, 0, 2)), s1


def _mlstm_sample_step_kernel(c0_ref, n0_ref, cols_ref, q_ref, k_ref, v_ref, op_ref, sc_ref, bif_ref, nw_ref,
                              c_out_ref, n_out_ref, m_out_ref, o_ref, *, heads, dk):
    gi = sc_ref[0, :, 0:1] + bif_ref[:, 0:1]
    gf = sc_ref[0, :, 1:2] + bif_ref[:, 1:2]
    m0 = sc_ref[0, :, 2:3]
    gi = GATE_CAP * jnp.tanh(gi / GATE_CAP)
    logf = _log_sigmoid(GATE_CAP * jnp.tanh(gf / GATE_CAP))
    m_new = jnp.maximum(logf + m0, gi)
    f_s = jnp.exp(logf + m0 - m_new)
    i_s = jnp.exp(gi - m_new)
    m_out_ref[0] = m_new
    scale = dk ** -0.5
    n1 = f_s * n0_ref[0] + i_s * k_ref[0]
    n_out_ref[0] = n1
    den = jnp.sum(q_ref[0] * scale * n1, axis=-1, keepdims=True)
    floor = jnp.exp(-m_new)
    for h in range(heads):
        kc = cols_ref[0, :, h:h + 1]
        qc = cols_ref[0, :, heads + h:heads + h + 1] * scale
        c1 = f_s[h:h + 1, :] * c0_ref[0, h] + i_s[h:h + 1, :] * (kc * v_ref[0, h:h + 1, :])
        c_out_ref[0, h] = c1
        num = jnp.sum(qc * c1, axis=0, keepdims=True)
        h_t = num / jnp.maximum(jnp.abs(den[h:h + 1, :]), floor[h:h + 1, :])
        h_n = h_t * lax.rsqrt(jnp.mean(h_t * h_t, axis=-1, keepdims=True) + RMS_EPS) * nw_ref[h:h + 1, :]
        o_ref[0, h:h + 1, :] = _sigmoid(op_ref[0, h:h + 1, :]) * h_n


def _mlstm_sample_step(c0, n0, cols, q, k, v, o_pre, sc, bif2, nw2):
    n, heads, dk, dv = c0.shape

    def blk(a):
        nd = a.ndim
        return pl.BlockSpec((1,) + a.shape[1:], lambda b: (b,) + (0,) * (nd - 1))

    def full(a):
        nd = a.ndim
        return pl.BlockSpec(a.shape, lambda b: (0,) * nd)

    m_shape = (n, heads, 1)
    return pl.pallas_call(
        functools.partial(_mlstm_sample_step_kernel, heads=heads, dk=dk),
        grid=(n,),
        in_specs=[blk(c0), blk(n0), blk(cols), blk(q), blk(k), blk(v), blk(o_pre), blk(sc), full(bif2), full(nw2)],
        out_specs=[blk(c0), blk(n0), pl.BlockSpec((1, heads, 1), lambda b: (b, 0, 0)), blk(v)],
        out_shape=[jax.ShapeDtypeStruct(c0.shape, F32), jax.ShapeDtypeStruct(n0.shape, F32),
                   jax.ShapeDtypeStruct(m_shape, F32), jax.ShapeDtypeStruct(v.shape, F32)],
        compiler_params=_cparams(("parallel",)),
        name="mlstm_sample_step",
    )(c0, n0, cols, q, k, v, o_pre, sc, bif2, nw2)


def _mlstm_sample_layer(x, g, p, c0, n0, m0, *, heads, dk, dv):
    n = x.shape[0]
    qk_w, val = heads * dk, heads * dv
    pm, pif = _in_proj(x, g, p["w_main"], p["w_if"])
    q = pm[:, :qk_w].reshape(n, heads, dk)
    k = pm[:, qk_w:2 * qk_w].reshape(n, heads, dk)
    v = pm[:, 2 * qk_w:2 * qk_w + val].reshape(n, heads, dv)
    o_pre = pm[:, 2 * qk_w + val:].reshape(n, heads, dv)
    cols = jnp.concatenate([jnp.transpose(k, (0, 2, 1)), jnp.transpose(q, (0, 2, 1))], axis=-1)
    sc = jnp.stack([pif[:, :heads], pif[:, heads:2 * heads], m0], axis=-1)
    bif2 = jnp.stack([p["bif"][0, :heads], p["bif"][0, heads:2 * heads]], axis=-1)
    c1, n1, m1, o = _mlstm_sample_step(c0, n0, cols, q, k, v, o_pre, sc, bif2, p["nw"].reshape(heads, dv))
    y = _out_proj(o.reshape(n, val), p["w_out"], x)
    return y, c1, n1, m1[:, :, 0]


def _rwkv_sample_step_kernel(s0_ref, r_ref, k_ref, lw_ref, kk_ref, a_ref, vc_ref, gc_ref, hpc_ref, hpr_ref,
                             s_out_ref, o_ref, *, heads):
    r = r_ref[0]
    k = k_ref[0]
    kk = kk_ref[0]
    kk = kk * lax.rsqrt(jnp.maximum(jnp.sum(kk * kk, axis=-1, keepdims=True), 1e-24))
    bv = kk * a_ref[0]
    w = jnp.exp(lw_ref[0])
    bonus = jnp.sum(r * k * hpr_ref[...], axis=-1, keepdims=True)
    for h in range(heads):
        s0 = s0_ref[0, h]
        vc = vc_ref[0, :, h:h + 1]
        sa = jnp.sum(s0 * (-kk[h:h + 1, :]), axis=-1, keepdims=True)
        s1 = s0 * w[h:h + 1, :] + sa * bv[h:h + 1, :] + vc * k[h:h + 1, :]
        s_out_ref[0, h] = s1
        y = jnp.sum(s1 * r[h:h + 1, :], axis=-1, keepdims=True)
        mean = jnp.mean(y, axis=0, keepdims=True)
        yc = y - mean
        var = jnp.mean(yc * yc, axis=0, keepdims=True)
        y = yc * lax.rsqrt(var + RW_GN_EPS) * hpc_ref[0, :, h:h + 1] + hpc_ref[1, :, h:h + 1]
        y = y + bonus[h:h + 1, :] * vc
        o_ref[0, :, h:h + 1] = y * gc_ref[0, :, h:h + 1]


def _rwkv_sample_step(s0, r, k, lw, kk, a, v_c, g_c, hp_c, rk):
    n, heads, hd, _ = s0.shape

    def blk(a_):
        nd = a_.ndim
        return pl.BlockSpec((1,) + a_.shape[1:], lambda b: (b,) + (0,) * (nd - 1))

    def full(a_):
        nd = a_.ndim
        return pl.BlockSpec(a_.shape, lambda b: (0,) * nd)

    return pl.pallas_call(
        functools.partial(_rwkv_sample_step_kernel, heads=heads),
        grid=(n,),
        in_specs=[blk(s0), blk(r), blk(k), blk(lw), blk(kk), blk(a), blk(v_c), blk(g_c), full(hp_c), full(rk)],
        out_specs=[blk(s0), blk(v_c)],
        out_shape=[jax.ShapeDtypeStruct(s0.shape, F32), jax.ShapeDtypeStruct(v_c.shape, F32)],
        compiler_params=_cparams(("parallel",)),
        name="rwkv_sample_step",
    )(s0, r, k, lw, kk, a, v_c, g_c, hp_c, rk)


def _rwkv_sample_layer(x, g_norm, p, shift0, s0, *, heads, hd):
    n, d = x.shape
    hn = _rmsnorm(x, g_norm, tm=128)
    r, k, v, lw, kk, a, g = _rwkv_proj(hn, shift0, p, tm=128)

    def rows(z):
        return z.reshape(n, heads, hd)

    def cols(z):
        return jnp.transpose(z.reshape(n, heads, hd), (0, 2, 1))

    hp = p["hp"]
    hp_c = jnp.stack([hp[1].reshape(heads, hd).T, hp[2].reshape(heads, hd).T])
    s1, o_c = _rwkv_sample_step(s0, rows(r), rows(k), rows(lw), rows(kk), rows(a), cols(v), cols(g),
                                hp_c, hp[0].reshape(heads, hd))
    o = jnp.transpose(o_c, (0, 2, 1)).reshape(n, d)
    y = _out_proj(o, p["w_o"], x)
    return y, hn, s1


def _pad_cols(a, n):
    return jnp.pad(a, ((0, 0), (0, n - a.shape[1])))


def _gdn_prep(w_in, conv_w, a_log, dt_bias, norm_w, w_out, *, heads, dk, dv):
    key, val = heads * dk, heads * dv
    ch = 2 * key + val
    main = ch + val
    gp = jnp.zeros((8, 128), F32)
    gp = gp.at[0, heads:2 * heads].set(a_log).at[1, heads:2 * heads].set(dt_bias)
    return dict(w_main=w_in[:, :main].astype(BF16),
                w_ba=_pad_cols(w_in[:, main:], 128).astype(BF16),
                cw_t=jnp.pad(conv_w.T, ((0, 8 - CONV_W), (0, 0))),
                gp=gp, nw=norm_w[None, :], w_out=w_out.astype(BF16))


def _gdn_prompt_layer(x, g, p, conv0, s0, *, batch, seq, heads, dk, dv, norm=True, residual=True):
    pm, pba = _in_proj(x, g, p["w_main"], p["w_ba"], norm=norm)
    conv0 = jnp.pad(conv0, ((0, 0), (8 - (CONV_W - 1), 0), (0, 0)))
    o, conv, s = _gdn_prompt(pm, pba, conv0, s0, p["cw_t"], p["gp"], p["nw"],
                             batch=batch, seq=seq, heads=heads, dk=dk, dv=dv, tb=TB_MIX)
    y = _out_proj(o, p["w_out"], x if residual else None)
    return y, conv[:, 8 - (CONV_W - 1):, :], s


def _trunk(x, states, w, *, batch, seq):
    conv_in, gs_in, c_in, n_in, m_in, shift_in, rs_in = states
    depth = w["norm_mix"].shape[0]
    gh, gdk, gdv = gs_in.shape[2:]
    mh, mdk, mdv = c_in.shape[2:]
    rh, rhd = rs_in.shape[2:4]
    prompt = seq > 1
    tm_ffn = 1024 if prompt else 128
    outs = [[] for _ in range(7)]
    for i in range(depth):
        j = i // 3
        g = w["norm_mix"][i][None, :]
        if i % 3 == 0:
            p = w["gdn"][j]
            if prompt:
                x, cb, s = _gdn_prompt_layer(x, g, p, conv_in[j], gs_in[j], batch=batch, seq=seq,
                                             heads=gh, dk=gdk, dv=gdv)
            else:
                x, cb, s = _gdn_sample_layer(x, g, p, conv_in[j], gs_in[j], heads=gh, dk=gdk, dv=gdv)
            outs[0].append(cb)
            outs[1].append(s)
        elif i % 3 == 1:
            p = w["ml"][j]
            if prompt:
                x, c, n, m = _mlstm_prompt_layer(x, g, p, c_in[j], n_in[j], m_in[j], batch=batch, seq=seq,
                                                 heads=mh, dk=mdk, dv=mdv)
            else:
                x, c, n, m = _mlstm_sample_layer(x, g, p, c_in[j], n_in[j], m_in[j], heads=mh, dk=mdk, dv=mdv)
            outs[2].append(c)
            outs[3].append(n)
            outs[4].append(m)
        else:
            p = w["rw"][j]
            if prompt:
                x, sh, s = _rwkv_prompt_layer(x, g, p, shift_in[j], rs_in[j], batch=batch, seq=seq,
                                              heads=rh, hd=rhd)
            else:
                x, sh, s = _rwkv_sample_layer(x, g, p, shift_in[j], rs_in[j], heads=rh, hd=rhd)
            outs[5].append(sh)
            outs[6].append(s)
        x = _ffn(x, w["norm_ffn"][i][None, :], w["ffn_w1"][i], w["ffn_w2"][i], tm=tm_ffn, tf=512)
    y = _rmsnorm(x, w["norm_final"][None, :], tm=tm_ffn)
    return y, tuple(jnp.stack(z, axis=0) for z in outs)


def kernel(x_prompt, x_sample, state_gdn_conv, state_gdn_S, state_mlstm_C, state_mlstm_n, state_mlstm_m, state_rwkv_shift, state_rwkv_S, norm_mix, norm_ffn, norm_final, gdn_w_in, gdn_conv_w, gdn_a_log, gdn_dt_bias, gdn_norm_w, gdn_w_out, ml_w_in, ml_b_if, ml_norm_w, ml_w_out, rw_mu, rw_w_rkv, rw_w_o, rw_w0, rw_w1, rw_w2, rw_a0, rw_a1, rw_a2, rw_g1, rw_g2, rw_k_k, rw_k_a, rw_r_k, rw_lnx_w, rw_lnx_b, ffn_w1, ffn_w2):
    gh, gdk, gdv = state_gdn_S.shape[2:]
    mh, mdk, mdv = state_mlstm_C.shape[2:]
    w = dict(
        norm_mix=norm_mix, norm_ffn=norm_ffn, norm_final=norm_final,
        ffn_w1=ffn_w1.astype(BF16), ffn_w2=ffn_w2.astype(BF16),
        gdn=[_gdn_prep(gdn_w_in[j], gdn_conv_w[j], gdn_a_log[j], gdn_dt_bias[j], gdn_norm_w[j], gdn_w_out[j],
                       heads=gh, dk=gdk, dv=gdv) for j in range(gdn_w_in.shape[0])],
        ml=[_mlstm_prep(ml_w_in[j], ml_b_if[j], ml_norm_w[j], ml_w_out[j], heads=mh, dk=mdk, dv=mdv)
            for j in range(ml_w_in.shape[0])],
        rw=[_rwkv_prep(rw_mu[j], rw_w_rkv[j], rw_w_o[j], rw_w0[j], rw_w1[j], rw_w2[j], rw_a0[j], rw_a1[j],
                       rw_a2[j], rw_g1[j], rw_g2[j], rw_k_k[j], rw_k_a[j], rw_r_k[j], rw_lnx_w[j], rw_lnx_b[j])
            for j in range(rw_mu.shape[0])])
    sample_states = (state_gdn_conv, state_gdn_S, state_mlstm_C, state_mlstm_n, state_mlstm_m,
                     state_rwkv_shift, state_rwkv_S)
    bp, tp, d = x_prompt.shape
    bs, ts, _ = x_sample.shape
    assert ts == 1
    prompt_states = tuple(jnp.zeros((s.shape[0], bp) + s.shape[2:], s.dtype) for s in sample_states)
    y_p, new_p = _trunk(x_prompt.reshape(bp * tp, d), prompt_states, w, batch=bp, seq=tp)
    y_s, new_s = _trunk(x_sample.reshape(bs * ts, d), sample_states, w, batch=bs, seq=ts)
    out = [y_p.reshape(bp, tp, d), y_s.reshape(bs, ts, d)]
    for a, b in zip(new_p, new_s):
        out += [a, b]
    return tuple(out)
```

```python
import functools
import math

import jax
import jax.numpy as jnp
from jax import lax
from jax.experimental import pallas as pl
from jax.experimental.pallas import tpu as pltpu

F32 = jnp.float32
BF16 = jnp.bfloat16

RMS_EPS = 1e-6
NEG_BIG = -1e30
GATE_CAP = 15.0
RW_GN_EPS = 64e-5
CONV_W = 4
CHUNK = 64
RW_GROUP = 8
V7X_VMEM_LIMIT = 56 * 1024 * 1024
HI = lax.Precision.HIGHEST


def _cparams(sem):
    return pltpu.CompilerParams(dimension_semantics=sem, vmem_limit_bytes=V7X_VMEM_LIMIT)


def _dot(a, b):
    return jnp.dot(a.astype(BF16), b.astype(BF16), preferred_element_type=F32)


def _dot_nt(a, b):
    return lax.dot_general(a.astype(BF16), b.astype(BF16), (((1,), (1,)), ((), ())),
                           preferred_element_type=F32)


def _dot_tn(a, b):
    return lax.dot_general(a.astype(BF16), b.astype(BF16), (((0,), (0,)), ((), ())),
                           preferred_element_type=F32)


def _dot_hi(a, b):
    return jnp.dot(a, b, preferred_element_type=F32, precision=HI)


def _sigmoid(x):
    return 1.0 / (1.0 + jnp.exp(-x))


def _silu(x):
    return x * _sigmoid(x)


def _softplus(x):
    return jnp.maximum(x, 0.0) + jnp.log(1.0 + jnp.exp(-jnp.abs(x)))


def _log_sigmoid(x):
    return -_softplus(-x)


def _tri_masks(l):
    r = lax.broadcasted_iota(jnp.int32, (l, l), 0)
    c = lax.broadcasted_iota(jnp.int32, (l, l), 1)
    return r >= c, r > c


INV_BASE = 16


def _unit_lower_inverse(mats, l):
    n = mats[0].shape[0]
    r = lax.broadcasted_iota(jnp.int32, (n, n), 0)
    c = lax.broadcasted_iota(jnp.int32, (n, n), 1)
    eye = (r == c).astype(F32)
    size = min(INV_BASE, l)
    shift = size.bit_length() - 1
    diag = (r >> shift) == (c >> shift)
    a_d = [jnp.where(diag, a, 0.0) for a in mats]
    t = [eye - a for a in a_d]
    p = [_dot(a, a) for a in a_d]
    n = 2
    while n < size:
        tp = [_dot(ti, pi) for ti, pi in zip(t, p)]
        n *= 2
        if n < size:
            p = [_dot(pi, pi) for pi in p]
        t = [ti + d for ti, d in zip(t, tp)]
    while size < l:
        shift = size.bit_length() - 1
        off = ((r >> (shift + 1)) == (c >> (shift + 1))) & ((r >> shift) > (c >> shift))
        x = [_dot(jnp.where(off, a, 0.0), ti) for a, ti in zip(mats, t)]
        y = [_dot(ti, xi) for ti, xi in zip(t, x)]
        t = [ti - yi for ti, yi in zip(t, y)]
        size *= 2
    return t


def _nmm_kernel(*refs, norm, residual, aux):
    x_ref, g_ref, w_ref = refs[:3]
    pos = 3
    res_ref = aux_w_ref = aux_o_ref = None
    if residual:
        res_ref = refs[pos]
        pos += 1
    if aux:
        aux_w_ref = refs[pos]
        pos += 1
    o_ref = refs[pos]
    if aux:
        aux_o_ref = refs[pos + 1]
    xn_ref = refs[-1]

    @pl.when(pl.program_id(1) == 0)
    def _():
        x = x_ref[...]
        if norm:
            x = x * lax.rsqrt(jnp.mean(x * x, axis=-1, keepdims=True) + RMS_EPS) * g_ref[...]
        xn_ref[...] = x.astype(BF16)
        if aux:
            aux_o_ref[...] = jnp.dot(xn_ref[...], aux_w_ref[...], preferred_element_type=F32)

    y = jnp.dot(xn_ref[...], w_ref[...], preferred_element_type=F32)
    if residual:
        y = res_ref[...] + y
    o_ref[...] = y


def _nmm(x, g, w, res=None, w_aux=None, *, norm, tm, tn):
    m, k = x.shape
    n = w.shape[1]
    tm, tn = min(tm, m), min(tn, n)
    assert m % tm == 0 and n % tn == 0
    in_specs = [pl.BlockSpec((tm, k), lambda i, j: (i, 0)),
                pl.BlockSpec((1, k), lambda i, j: (0, 0)),
                pl.BlockSpec((k, tn), lambda i, j: (0, j))]
    args = [x, g, w]
    out_specs = [pl.BlockSpec((tm, tn), lambda i, j: (i, j))]
    out_shape = [jax.ShapeDtypeStruct((m, n), F32)]
    if res is not None:
        in_specs.append(pl.BlockSpec((tm, tn), lambda i, j: (i, j)))
        args.append(res)
    if w_aux is not None:
        na = w_aux.shape[1]
        in_specs.append(pl.BlockSpec((k, na), lambda i, j: (0, 0)))
        args.append(w_aux)
        out_specs.append(pl.BlockSpec((tm, na), lambda i, j: (i, 0)))
        out_shape.append(jax.ShapeDtypeStruct((m, na), F32))
    out = pl.pallas_call(
        functools.partial(_nmm_kernel, norm=norm, residual=res is not None, aux=w_aux is not None),
        grid=(m // tm, n // tn),
        in_specs=in_specs,
        out_specs=out_specs,
        out_shape=out_shape,
        scratch_shapes=[pltpu.VMEM((tm, k), BF16)],
        compiler_params=_cparams(("parallel", "arbitrary")),
        name="nmm",
    )(*args)
    return out if w_aux is not None else out[0]


TM_IN_PROJ = 512
TM_OUT_PROJ = 1024
TM_FFN, TF_FFN = 1024, 512
TB_MIX = 256
TM_RWKV_PROJ = 256


def _in_proj(x, g, w_main, w_aux, *, norm=True):
    return _nmm(x, g, w_main, None, w_aux, norm=norm, tm=TM_IN_PROJ, tn=w_main.shape[1])


def _out_proj(o, w_out, res):
    ones = jnp.ones((1, o.shape[1]), F32)
    return _nmm(o, ones, w_out, res, norm=False, tm=TM_OUT_PROJ, tn=w_out.shape[1])


def _ffn_kernel(x_ref, g_ref, w1_ref, w2_ref, o_ref, xn_ref, acc_ref):
    f = pl.program_id(1)

    @pl.when(f == 0)
    def _():
        x = x_ref[...]
        x = x * lax.rsqrt(jnp.mean(x * x, axis=-1, keepdims=True) + RMS_EPS) * g_ref[...]
        xn_ref[...] = x.astype(BF16)
        acc_ref[...] = jnp.zeros_like(acc_ref)

    h = jnp.dot(xn_ref[...], w1_ref[...], preferred_element_type=F32)
    a = jnp.square(jnp.maximum(h, 0.0)).astype(BF16)
    acc_ref[...] += jnp.dot(a, w2_ref[...], preferred_element_type=F32)

    @pl.when(f == pl.num_programs(1) - 1)
    def _():
        o_ref[...] = x_ref[...] + acc_ref[...]


def _ffn(x, g, w1, w2, *, tm, tf):
    m, d = x.shape
    dff = w1.shape[1]
    tm, tf = min(tm, m), min(tf, dff)
    assert m % tm == 0 and dff % tf == 0
    return pl.pallas_call(
        _ffn_kernel,
        grid=(m // tm, dff // tf),
        in_specs=[pl.BlockSpec((tm, d), lambda i, j: (i, 0)),
                  pl.BlockSpec((1, d), lambda i, j: (0, 0)),
                  pl.BlockSpec((d, tf), lambda i, j: (0, j)),
                  pl.BlockSpec((tf, d), lambda i, j: (j, 0))],
        out_specs=pl.BlockSpec((tm, d), lambda i, j: (i, 0)),
        out_shape=jax.ShapeDtypeStruct((m, d), F32),
        scratch_shapes=[pltpu.VMEM((tm, d), BF16), pltpu.VMEM((tm, d), F32)],
        compiler_params=_cparams(("parallel", "arbitrary")),
        name="ffn",
    )(x, g, w1, w2)


def _norm_kernel(x_ref, g_ref, o_ref):
    x = x_ref[...]
    o_ref[...] = x * lax.rsqrt(jnp.mean(x * x, axis=-1, keepdims=True) + RMS_EPS) * g_ref[...]


def _rmsnorm(x, g, *, tm):
    m, d = x.shape
    tm = min(tm, m)
    assert m % tm == 0
    return pl.pallas_call(
        _norm_kernel,
        grid=(m // tm,),
        in_specs=[pl.BlockSpec((tm, d), lambda i: (i, 0)), pl.BlockSpec((1, d), lambda i: (0, 0))],
        out_specs=pl.BlockSpec((tm, d), lambda i: (i, 0)),
        out_shape=jax.ShapeDtypeStruct((m, d), F32),
        compiler_params=_cparams(("parallel",)),
        name="rmsnorm",
    )(x, g)


def _gdn_prompt_kernel(pm_ref, pba_ref, conv0_ref, s0_ref, cw_ref, gp_ref, nw_ref,
                       o_ref, conv_out_ref, s_out_ref,
                       full_ref, qkv_ref, s_ref, u_ref, w_ref, qd_ref, kd_ref, qk_ref, gl_ref,
                       *, tb, heads, dk, dv):
    t = pl.program_id(1)
    key = heads * dk
    ch = 2 * key + heads * dv
    l = CHUNK

    @pl.when(t == 0)
    def _():
        full_ref[0:8, :] = conv0_ref[0]
        s_ref[...] = s0_ref[0]

    full_ref[8:8 + tb, :] = pm_ref[:, 0:ch]
    for c in range(ch // 128):
        cs = slice(c * 128, (c + 1) * 128)
        y = full_ref[5:5 + tb, cs] * cw_ref[0:1, cs]
        for j in range(1, CONV_W):
            y = y + full_ref[5 + j:5 + j + tb, cs] * cw_ref[j:j + 1, cs]
        y = _silu(y)
        if c * 128 < key:
            y = y * lax.rsqrt(jnp.sum(y * y, axis=-1, keepdims=True) + 1e-6) * (dk ** -0.5)
        elif c * 128 < 2 * key:
            y = y * lax.rsqrt(jnp.sum(y * y, axis=-1, keepdims=True) + 1e-6)
        qkv_ref[:, cs] = y
    full_ref[0:8, :] = full_ref[tb:tb + 8, :]

    a_log = gp_ref[0:1, :]
    dt_bias = gp_ref[1:2, :]
    hh = range(heads)
    l2 = 2 * l
    r2 = lax.broadcasted_iota(jnp.int32, (l2, l2), 0)
    c2 = lax.broadcasted_iota(jnp.int32, (l2, l2), 1)
    same = (r2 >= l) == (c2 >= l)
    incl = same & (r2 >= c2)
    strict = same & (r2 > c2)
    tril = incl.astype(F32)
    first = lax.broadcasted_iota(jnp.int32, (l2, 128), 0) < l

    pairs = range(tb // l2)
    rows = [slice(pi * l2, (pi + 1) * l2) for pi in pairs]
    beta_all, gc, gc_t, g_end = [], [], [], []
    for pi in pairs:
        ba = pba_ref[rows[pi], :]
        beta_all.append(_sigmoid(ba))
        g_all = -jnp.exp(a_log) * _softplus(ba + dt_bias)
        gc.append(_dot_hi(tril, g_all))
        gc_t.append(gc[pi].T)
        g_end.append(jnp.where(first, gc[pi][l - 1:l, :], gc[pi][l2 - 1:l2, :]))
        gl_ref[2 * pi:2 * pi + 1, :] = jnp.exp(gc[pi][l - 1:l, :])
        gl_ref[2 * pi + 1:2 * pi + 2, :] = jnp.exp(gc[pi][l2 - 1:l2, :])
    cc = [(pi, h) for pi in pairs for h in hh]
    b_col = [beta_all[pi][:, h:h + 1] for pi, h in cc]
    gi = [gc[pi][:, heads + h:heads + h + 1] for pi, h in cc]
    q = [qkv_ref[rows[pi], h * dk:(h + 1) * dk] for pi, h in cc]
    k = [qkv_ref[rows[pi], key + h * dk:key + (h + 1) * dk] for pi, h in cc]
    v = [qkv_ref[rows[pi], 2 * key + h * dv:2 * key + (h + 1) * dv] for pi, h in cc]
    nc = range(len(cc))
    dmat = [jnp.where(incl, jnp.exp(jnp.where(incl, gi[i] - gc_t[pi][heads + h:heads + h + 1, :], 0.0)), 0.0)
            for i, (pi, h) in enumerate(cc)]
    kb = [k[i] * b_col[i] for i in nc]
    kk = [_dot_nt(kb[i], k[i]) for i in nc]
    qk = [_dot_nt(q[i], k[i]) for i in nc]
    t_inv = _unit_lower_inverse([jnp.where(strict, kk[i] * dmat[i], 0.0) for i in nc], l)
    egi = [jnp.exp(gi[i]) for i in nc]
    sol = [_dot(t_inv[i], jnp.concatenate([v[i] * b_col[i], kb[i] * egi[i]], axis=-1)) for i in nc]
    for i, (pi, h) in enumerate(cc):
        hs = slice(h * dk, (h + 1) * dk)
        u_ref[rows[pi], h * dv:(h + 1) * dv] = sol[i][:, :dv]
        w_ref[rows[pi], hs] = sol[i][:, dv:].astype(BF16)
        qd_ref[rows[pi], hs] = (q[i] * egi[i]).astype(BF16)
        kd_ref[rows[pi], hs] = (k[i] * jnp.exp(g_end[pi][:, heads + h:heads + h + 1] - gi[i])).astype(BF16)
        qkm = jnp.where(incl, qk[i] * dmat[i], 0.0).astype(BF16)
        qk_ref[h, pi * l2:pi * l2 + l, :] = qkm[:l, :l]
        qk_ref[h, pi * l2 + l:(pi + 1) * l2, :] = qkm[l:, l:]

    for ci in range(tb // l):
        rows = slice(ci * l, (ci + 1) * l)
        s = [s_ref[h] for h in hh]
        wq = [_dot(jnp.concatenate([w_ref[rows, h * dk:(h + 1) * dk], qd_ref[rows, h * dk:(h + 1) * dk]], axis=0),
                   s[h]) for h in hh]
        v_new = [u_ref[rows, h * dv:(h + 1) * dv] - wq[h][:l] for h in hh]
        o2 = [_dot(qk_ref[h, rows, :], v_new[h]) for h in hh]
        ds = [_dot_tn(kd_ref[rows, h * dk:(h + 1) * dk], v_new[h]) for h in hh]
        for h in hh:
            s_ref[h] = s[h] * gl_ref[ci:ci + 1, heads + h:heads + h + 1] + ds[h]
            o = wq[h][l:] + o2[h]
            z = pm_ref[rows, ch + h * dv:ch + (h + 1) * dv]
            o = o * lax.rsqrt(jnp.mean(o * o, axis=-1, keepdims=True) + RMS_EPS) * nw_ref[...]
            o_ref[rows, h * dv:(h + 1) * dv] = o * _silu(z)

    @pl.when(t == pl.num_programs(1) - 1)
    def _():
        conv_out_ref[0] = full_ref[0:8, :]
        s_out_ref[0] = s_ref[...]


def _gdn_prompt(pm, pba, conv0, s0, cw_t, gp, nw, *, batch, seq, heads, dk, dv, tb):
    key, val = heads * dk, heads * dv
    ch = 2 * key + val
    tb = min(tb, seq)
    assert seq % tb == 0 and tb % CHUNK == 0
    nt = seq // tb
    return pl.pallas_call(
        functools.partial(_gdn_prompt_kernel, tb=tb, heads=heads, dk=dk, dv=dv),
        grid=(batch, nt),
        in_specs=[pl.BlockSpec((tb, ch + val), lambda b, t: (b * nt + t, 0)),
                  pl.BlockSpec((tb, 128), lambda b, t: (b * nt + t, 0)),
                  pl.BlockSpec((1, 8, ch), lambda b, t: (b, 0, 0)),
                  pl.BlockSpec((1, heads, dk, dv), lambda b, t: (b, 0, 0, 0)),
                  pl.BlockSpec((8, ch), lambda b, t: (0, 0)),
                  pl.BlockSpec((8, 128), lambda b, t: (0, 0)),
                  pl.BlockSpec((1, dv), lambda b, t: (0, 0))],
        out_specs=[pl.BlockSpec((tb, val), lambda b, t: (b * nt + t, 0)),
                   pl.BlockSpec((1, 8, ch), lambda b, t: (b, 0, 0)),
                   pl.BlockSpec((1, heads, dk, dv), lambda b, t: (b, 0, 0, 0))],
        out_shape=[jax.ShapeDtypeStruct((batch * seq, val), F32),
                   jax.ShapeDtypeStruct((batch, 8, ch), F32),
                   jax.ShapeDtypeStruct((batch, heads, dk, dv), F32)],
        scratch_shapes=[pltpu.VMEM((tb + 8, ch), F32), pltpu.VMEM((tb, ch), F32),
                        pltpu.VMEM((heads, dk, dv), F32),
                        pltpu.VMEM((tb, val), F32), pltpu.VMEM((tb, key), BF16),
                        pltpu.VMEM((tb, key), BF16), pltpu.VMEM((tb, key), BF16),
                        pltpu.VMEM((heads, tb, CHUNK), BF16), pltpu.VMEM((max(8, tb // CHUNK), 128), F32)],
        compiler_params=_cparams(("parallel", "arbitrary")),
        name="gdn_prompt",
    )(pm, pba, conv0, s0, cw_t, gp, nw)


def _mlstm_prompt_kernel(pm_ref, pif_ref, bif_ref, c0_ref, n0_ref, m0_ref, nw_ref,
                         o_ref, c_out_ref, n_out_ref, m_out_ref,
                         c_ref, n_ref, m_ref, *, tb, heads, dk, dv):
    t = pl.program_id(1)
    l = CHUNK
    qk_w = heads * dk
    v_off = 2 * qk_w
    o_off = v_off + heads * dv

    @pl.when(t == 0)
    def _():
        c_ref[...] = c0_ref[0]
        n_ref[...] = n0_ref[0]
        m_ref[...] = m0_ref[0]

    incl, _ = _tri_masks(l)
    tril = incl.astype(F32)

    def chunk(ci, carry):
        r0 = pl.multiple_of(ci * l, l)
        rows = pl.ds(r0, l)
        gates = pif_ref[rows, :] + bif_ref[...]
        gates = GATE_CAP * jnp.tanh(gates / GATE_CAP)
        bcum = _dot_hi(tril, _log_sigmoid(gates))
        bcum_t = bcum.T
        gates_t = gates.T
        hh = range(heads)
        bi = [bcum[:, heads + h:heads + h + 1] for h in hh]
        ii = [gates[:, h:h + 1] for h in hh]
        b_last = [bcum[l - 1:l, heads + h:heads + h + 1] for h in hh]
        q = [pm_ref[rows, h * dk:(h + 1) * dk] * (dk ** -0.5) for h in hh]
        k = [pm_ref[rows, qk_w + h * dk:qk_w + (h + 1) * dk] for h in hh]
        v = [pm_ref[rows, v_off + h * dv:v_off + (h + 1) * dv] for h in hh]
        qk = [_dot_nt(q[h], k[h]) for h in hh]
        c_mat = [c_ref[h] for h in hh]
        qc = [_dot(q[h], c_mat[h]) for h in hh]
        dlog = [jnp.where(incl, bi[h] - bcum_t[heads + h:heads + h + 1, :] + gates_t[h:h + 1, :], NEG_BIG)
                for h in hh]
        m_intra = [jnp.max(dlog[h], axis=-1, keepdims=True) for h in hh]
        p = [jnp.where(incl, jnp.exp(dlog[h] - m_intra[h]), 0.0) * qk[h] for h in hh]
        num_intra = [_dot(p[h], v[h]) for h in hh]
        a_log = [b_last[h] - bi[h] + ii[h] for h in hh]
        m_chunk = [jnp.max(a_log[h], axis=0, keepdims=True) for h in hh]
        kw = [k[h] * jnp.exp(a_log[h] - m_chunk[h]) for h in hh]
        kv_chunk = [_dot_tn(kw[h], v[h]) for h in hh]
        den_intra = [jnp.sum(p[h], axis=-1, keepdims=True) for h in hh]
        m_prev = [m_ref[h:h + 1, 0:1] for h in hh]
        n_vec = [n_ref[h:h + 1, :] for h in hh]
        qn = [jnp.sum(q[h] * n_vec[h], axis=-1, keepdims=True) for h in hh]
        m_t = [jnp.maximum(bi[h] + m_prev[h], m_intra[h]) for h in hh]
        s_inter = [jnp.exp(bi[h] + m_prev[h] - m_t[h]) for h in hh]
        s_intra = [jnp.exp(m_intra[h] - m_t[h]) for h in hh]
        den = [s_inter[h] * qn[h] + s_intra[h] * den_intra[h] for h in hh]
        h_t = [(s_inter[h] * qc[h] + s_intra[h] * num_intra[h])
               / jnp.maximum(jnp.abs(den[h]), jnp.exp(-m_t[h])) for h in hh]
        ms = [jnp.mean(h_t[h] * h_t[h], axis=-1, keepdims=True) for h in hh]
        for h in hh:
            k_chunk = jnp.sum(kw[h], axis=0, keepdims=True)
            m_new = jnp.maximum(b_last[h] + m_prev[h], m_chunk[h])
            f_s = jnp.exp(b_last[h] + m_prev[h] - m_new)
            i_s = jnp.exp(m_chunk[h] - m_new)
            c_ref[h] = f_s * c_mat[h] + i_s * kv_chunk[h]
            n_ref[h:h + 1, :] = f_s * n_vec[h] + i_s * k_chunk
            m_ref[h:h + 1, :] = jnp.broadcast_to(m_new, (1, m_ref.shape[1]))
            h_n = h_t[h] * lax.rsqrt(ms[h] + RMS_EPS) * nw_ref[:, h * dv:(h + 1) * dv]
            o_pre = pm_ref[rows, o_off + h * dv:o_off + (h + 1) * dv]
            o_ref[rows, h * dv:(h + 1) * dv] = _sigmoid(o_pre) * h_n
        return carry

    lax.fori_loop(0, tb // l, chunk, 0)

    @pl.when(t == pl.num_programs(1) - 1)
    def _():
        c_out_ref[0] = c_ref[...]
        n_out_ref[0] = n_ref[...]
        m_out_ref[0] = m_ref[...]


def _mlstm_prompt(pm, pif, bif, c0, n0, m0, nw, *, batch, seq, heads, dk, dv, tb):
    width = pm.shape[1]
    val = heads * dv
    tb = min(tb, seq)
    assert seq % tb == 0 and tb % CHUNK == 0
    nt = seq // tb
    return pl.pallas_call(
        functools.partial(_mlstm_prompt_kernel, tb=tb, heads=heads, dk=dk, dv=dv),
        grid=(batch, nt),
        in_specs=[pl.BlockSpec((tb, width), lambda b, t: (b * nt + t, 0)),
                  pl.BlockSpec((tb, 128), lambda b, t: (b * nt + t, 0)),
                  pl.BlockSpec((1, 128), lambda b, t: (0, 0)),
                  pl.BlockSpec((1, heads, dk, dv), lambda b, t: (b, 0, 0, 0)),
                  pl.BlockSpec((1, 8, dk), lambda b, t: (b, 0, 0)),
                  pl.BlockSpec((1, 8, 128), lambda b, t: (b, 0, 0)),
                  pl.BlockSpec((1, val), lambda b, t: (0, 0))],
        out_specs=[pl.BlockSpec((tb, val), lambda b, t: (b * nt + t, 0)),
                   pl.BlockSpec((1, heads, dk, dv), lambda b, t: (b, 0, 0, 0)),
                   pl.BlockSpec((1, 8, dk), lambda b, t: (b, 0, 0)),
                   pl.BlockSpec((1, 8, 128), lambda b, t: (b, 0, 0))],
        out_shape=[jax.ShapeDtypeStruct((batch * seq, val), F32),
                   jax.ShapeDtypeStruct((batch, heads, dk, dv), F32),
                   jax.ShapeDtypeStruct((batch, 8, dk), F32),
                   jax.ShapeDtypeStruct((batch, 8, 128), F32)],
        scratch_shapes=[pltpu.VMEM((heads, dk, dv), F32), pltpu.VMEM((8, dk), F32),
                        pltpu.VMEM((8, 128), F32)],
        compiler_params=_cparams(("parallel", "arbitrary")),
        name="mlstm_prompt",
    )(pm, pif, bif, c0, n0, m0, nw)


def _mlstm_prep(w_in, b_if, norm_w, w_out, *, heads, dk, dv):
    main = 2 * heads * dk + 2 * heads * dv
    return dict(w_main=w_in[:, :main].astype(BF16),
                w_if=_pad_cols(w_in[:, main:], 128).astype(BF16),
                bif=_pad_cols(b_if[None, :], 128), nw=norm_w[None, :], w_out=w_out.astype(BF16))


def _mlstm_prompt_layer(x, g, p, c0, n0, m0, *, batch, seq, heads, dk, dv, norm=True, residual=True):
    pm, pif = _in_proj(x, g, p["w_main"], p["w_if"], norm=norm)
    n0p = jnp.pad(n0, ((0, 0), (0, 8 - heads), (0, 0)))
    m0p = jnp.broadcast_to(jnp.pad(m0, ((0, 0), (0, 8 - heads)))[:, :, None], (batch, 8, 128))
    o, c, n, m = _mlstm_prompt(pm, pif, p["bif"], c0, n0p, m0p, p["nw"],
                               batch=batch, seq=seq, heads=heads, dk=dk, dv=dv, tb=TB_MIX)
    y = _out_proj(o, p["w_out"], x if residual else None)
    return y, c, n[:, :heads, :], m[:, :heads, 0]


def _rwkv_proj_body(h, prev, mu_ref, wrkv_ref, w1_ref, w2_ref, a1_ref, a2_ref, g1_ref, g2_ref,
                    vec_ref, r_ref, k_ref, v_ref, lw_ref, kk_ref, a_ref, g_ref):
    xx = prev - h

    def mix(j):
        return (h + xx * mu_ref[j:j + 1, :]).astype(BF16)

    w0, a0, k_k, k_a = (vec_ref[j:j + 1, :] for j in range(4))
    r_ref[...] = jnp.dot(mix(0), wrkv_ref[0], preferred_element_type=F32)
    lora_w = _dot(jnp.tanh(_dot(mix(1), w1_ref[...])), w2_ref[...])
    w_log = -_softplus(-(w0 + lora_w)) - 0.5
    lw_ref[...] = -jnp.exp(w_log)
    k = jnp.dot(mix(2), wrkv_ref[1], preferred_element_type=F32)
    v_ref[...] = jnp.dot(mix(3), wrkv_ref[2], preferred_element_type=F32)
    a = _sigmoid(a0 + _dot(_dot(mix(4), a1_ref[...]), a2_ref[...]))
    g_ref[...] = _dot(_sigmoid(_dot(mix(5), g1_ref[...])), g2_ref[...])
    kk_ref[...] = k * k_k
    k_ref[...] = k * (1.0 + (a - 1.0) * k_a)
    a_ref[...] = a


def _rwkv_proj_sample_kernel(x_ref, gn_ref, prev_ref, *refs):
    h = _norm_rows(x_ref[...], gn_ref[...])
    hn_ref = refs[-1]
    hn_ref[...] = h
    _rwkv_proj_body(h, prev_ref[...], *refs[:-1])


def _rwkv_proj_prompt_kernel(x_ref, gn_ref, shift0_ref, *refs):
    carry_ref = refs[-1]
    shift_out_ref = refs[-2]
    t = pl.program_id(1)

    @pl.when(t == 0)
    def _():
        carry_ref[...] = shift0_ref[0]

    h = _norm_rows(x_ref[...], gn_ref[...])
    rows = h.shape[0]
    first = lax.broadcasted_iota(jnp.int32, h.shape, 0) == 0
    prev = jnp.where(first, carry_ref[0:1, :], pltpu.roll(h, 1, 0))
    carry_ref[0:1, :] = h[rows - 1:rows, :]
    _rwkv_proj_body(h, prev, *refs[:-2])

    @pl.when(t == pl.num_programs(1) - 1)
    def _():
        shift_out_ref[0] = carry_ref[...]


def _norm_rows(x, g):
    return x * lax.rsqrt(jnp.mean(x * x, axis=-1, keepdims=True) + RMS_EPS) * g


def _rwkv_proj(x, g_norm, shift0, p, *, batch, seq, tm):
    m, d = x.shape
    consts = [p["mu"], p["w_rkv"], p["w1"], p["w2"], p["a1"], p["a2"], p["g1"], p["g2"], p["vec"]]
    if seq == 1:
        row = pl.BlockSpec((m, d), lambda i: (0, 0))
        out = pl.pallas_call(
            _rwkv_proj_sample_kernel,
            grid=(1,),
            in_specs=[row, pl.BlockSpec((1, d), lambda i: (0, 0)), row] + [_const_block(a) for a in consts],
            out_specs=[row] * 8,
            out_shape=[jax.ShapeDtypeStruct((m, d), F32)] * 8,
            compiler_params=_cparams(("arbitrary",)),
            name="rwkv_proj_sample",
        )(x, g_norm, shift0, *consts)
        return out[:7], out[7]
    tm = min(tm, seq)
    assert seq % tm == 0
    nt = seq // tm
    row = pl.BlockSpec((tm, d), lambda b, t: (b * nt + t, 0))
    st = pl.BlockSpec((1, 8, d), lambda b, t: (b, 0, 0))

    def full(a):
        nd = a.ndim
        return pl.BlockSpec(a.shape, lambda b, t: (0,) * nd)

    shift0_p = jnp.pad(shift0[:, None, :], ((0, 0), (0, 7), (0, 0)))
    out = pl.pallas_call(
        _rwkv_proj_prompt_kernel,
        grid=(batch, nt),
        in_specs=[row, full(g_norm), st] + [full(a) for a in consts],
        out_specs=[row] * 7 + [st],
        out_shape=[jax.ShapeDtypeStruct((m, d), F32)] * 7 + [jax.ShapeDtypeStruct((batch, 8, d), F32)],
        scratch_shapes=[pltpu.VMEM((8, d), F32)],
        compiler_params=_cparams(("parallel", "arbitrary")),
        name="rwkv_proj_prompt",
    )(x, g_norm, shift0_p, *consts)
    return out[:7], out[7][:, 0, :]


def _rwkv_prompt_kernel(r_ref, k_ref, v_ref, lw_ref, kk_ref, a_ref, g_ref, s0_ref, hp_ref,
                        o_ref, s_out_ref, s_ref, rr_ref, yy_ref, mx_ref, n0_ref, gw_ref, gb_ref, el_ref,
                        *, tb, heads, hd):
    t = pl.program_id(1)
    l = CHUNK

    @pl.when(t == 0)
    def _():
        s_ref[...] = s0_ref[0]

    hh = range(heads)
    hs = [slice(h * hd, (h + 1) * hd) for h in hh]
    l2 = 2 * l
    r2 = lax.broadcasted_iota(jnp.int32, (l2, l2), 0)
    c2 = lax.broadcasted_iota(jnp.int32, (l2, l2), 1)
    same = (r2 >= l) == (c2 >= l)
    incl = same & (r2 >= c2)
    strict = same & (r2 > c2)
    tril = incl.astype(F32)
    first = lax.broadcasted_iota(jnp.int32, (l2, heads * hd), 0) < l
    zeros = jnp.zeros((l2, hd), F32)

    def pair(pi, carry):
        rows = pl.ds(pl.multiple_of(pi * l2, l2), l2)
        lw = lw_ref[rows, :]
        lwc = _dot_hi(tril, lw)
        lw_end = jnp.where(first, lwc[l - 1:l, :], lwc[l2 - 1:l2, :])
        e_in = jnp.exp(lwc)
        e_prev = jnp.exp(lwc - lw)
        e_neg = jnp.exp(-lwc)
        e_end = jnp.exp(lw_end - lwc)
        r = [r_ref[rows, hs[h]] for h in hh]
        v = [v_ref[rows, hs[h]] for h in hh]
        k = [k_ref[rows, hs[h]] for h in hh]
        kk = [kk_ref[rows, hs[h]] for h in hh]
        kk = [kk[h] * lax.rsqrt(jnp.maximum(jnp.sum(kk[h] * kk[h], axis=-1, keepdims=True), 1e-24)) for h in hh]
        bv = [kk[h] * a_ref[rows, hs[h]] for h in hh]
        a_t = [-kk[h] * e_prev[:, hs[h]] for h in hh]
        r_t = [r[h] * e_in[:, hs[h]] for h in hh]
        gm = [_dot_nt(jnp.concatenate([a_t[h], r_t[h]], axis=0),
                      jnp.concatenate([bv[h] * e_neg[:, hs[h]], k[h] * e_neg[:, hs[h]]], axis=0)) for h in hh]
        t_inv = _unit_lower_inverse([jnp.where(strict, -gm[h][:l2, :l2], 0.0) for h in hh], l)
        akv = [_dot(jnp.where(strict, gm[h][:l2, l2:], 0.0), v[h]) for h in hh]
        x1 = [_dot(t_inv[h], jnp.concatenate([a_t[h], akv[h]], axis=1)) for h in hh]
        low = [jnp.concatenate([x1[h], jnp.concatenate([zeros, v[h]], axis=1)], axis=0) for h in hh]
        x2 = [_dot(jnp.concatenate([jnp.where(incl, gm[h][l2:, :l2], 0.0),
                                    jnp.where(incl, gm[h][l2:, l2:], 0.0)], axis=1), low[h]) for h in hh]
        for c in range(2):
            cs = slice(c * l, (c + 1) * l)
            cs2 = slice(l2 + c * l, l2 + (c + 1) * l)
            mn = [_dot_tn(jnp.concatenate([low[h][cs], low[h][cs2]], axis=0),
                          jnp.concatenate([(bv[h] * e_end[:, hs[h]])[cs], (k[h] * e_end[:, hs[h]])[cs]], axis=0))
                  for h in hh]
            crow = pl.ds(pl.multiple_of(pi * l2 + c * l, l), l)
            for h in hh:
                mx_ref[h, crow, :] = mn[h][:hd].astype(BF16)
                n0_ref[h, crow, :] = mn[h][hd:]
        for h in hh:
            rr_ref[h, rows, :] = (r_t[h] + x2[h][:, :hd]).astype(BF16)
            yy_ref[h, rows, :] = x2[h][:, hd:]
            g = g_ref[rows, hs[h]]
            bonus = jnp.sum(r[h] * k[h] * hp_ref[0:1, hs[h]], axis=-1, keepdims=True) * v[h]
            gw_ref[h, rows, :] = hp_ref[1:2, hs[h]] * g
            gb_ref[h, rows, :] = (hp_ref[2:3, hs[h]] + bonus) * g
            el_ref[h, pl.ds(pi * 2, 1), :] = jnp.exp(lwc[l - 1:l, hs[h]])
            el_ref[h, pl.ds(pi * 2 + 1, 1), :] = jnp.exp(lwc[l2 - 1:l2, hs[h]])
        return carry

    lax.fori_loop(0, tb // l2, pair, 0)

    for ci in range(tb // l):
        rows = slice(ci * l, (ci + 1) * l)
        s = [s_ref[h] for h in hh]
        y = [_dot_nt(rr_ref[h, rows, :], s[h]) for h in hh]
        sm = [_dot(s[h], mx_ref[h, rows, :]) for h in hh]
        for h in hh:
            s_ref[h] = s[h] * el_ref[h, ci:ci + 1, :] + sm[h] + n0_ref[h, rows, :]
        y = [y[h] + yy_ref[h, rows, :] for h in hh]
        mean = [jnp.mean(y[h], axis=-1, keepdims=True) for h in hh]
        yc = [y[h] - mean[h] for h in hh]
        var = [jnp.mean(yc[h] * yc[h], axis=-1, keepdims=True) for h in hh]
        for h in hh:
            o_ref[rows, hs[h]] = yc[h] * lax.rsqrt(var[h] + RW_GN_EPS) * gw_ref[h, rows, :] + gb_ref[h, rows, :]

    @pl.when(t == pl.num_programs(1) - 1)
    def _():
        s_out_ref[0] = s_ref[...]


def _rwkv_prompt(r, k, v, lw, kk, a, g, s0, hp, *, batch, seq, heads, hd, tb):
    d = heads * hd
    tb = min(tb, seq)
    assert seq % tb == 0 and tb % CHUNK == 0
    nt = seq // tb
    row = pl.BlockSpec((tb, d), lambda b, t: (b * nt + t, 0))
    st = pl.BlockSpec((1, heads, hd, hd), lambda b, t: (b, 0, 0, 0))
    return pl.pallas_call(
        functools.partial(_rwkv_prompt_kernel, tb=tb, heads=heads, hd=hd),
        grid=(batch, nt),
        in_specs=[row] * 7 + [st, pl.BlockSpec((8, d), lambda b, t: (0, 0))],
        out_specs=[row, st],
        out_shape=[jax.ShapeDtypeStruct((batch * seq, d), F32),
                   jax.ShapeDtypeStruct((batch, heads, hd, hd), F32)],
        scratch_shapes=[pltpu.VMEM((heads, hd, hd), F32),
                        pltpu.VMEM((heads, tb, hd), BF16), pltpu.VMEM((heads, tb, hd), F32),
                        pltpu.VMEM((heads, tb, hd), BF16), pltpu.VMEM((heads, tb, hd), F32),
                        pltpu.VMEM((heads, tb, hd), F32), pltpu.VMEM((heads, tb, hd), F32),
                        pltpu.VMEM((heads, max(8, tb // CHUNK), hd), F32)],
        compiler_params=_cparams(("parallel", "arbitrary")),
        name="rwkv_prompt",
    )(r, k, v, lw, kk, a, g, s0, hp)


def _pad_rows(a, n):
    return jnp.pad(a, ((0, n - a.shape[0]), (0, 0)))


def _rwkv_prep(mu, w_rkv, w_o, w0, w1, w2, a0, a1, a2, g1, g2, k_k, k_a, r_k, lnx_w, lnx_b):
    d = w0.shape[0]
    lw = -(-w1.shape[1] // 128) * 128
    la = -(-a1.shape[1] // 128) * 128
    lg = -(-g1.shape[1] // 128) * 128
    return dict(mu=_pad_rows(mu, 8), w_rkv=w_rkv.astype(BF16), w_o=w_o.astype(BF16),
                w1=_pad_cols(w1, lw).astype(BF16), w2=_pad_rows(w2, lw).astype(BF16),
                a1=_pad_cols(a1, la).astype(BF16), a2=_pad_rows(a2, la).astype(BF16),
                g1=_pad_cols(g1, lg).astype(BF16), g2=_pad_rows(g2, lg).astype(BF16),
                vec=_pad_rows(jnp.stack([w0, a0, k_k, k_a]), 8),
                hp=_pad_rows(jnp.stack([r_k.reshape(d), lnx_w, lnx_b]), 8))


def _rwkv_prompt_layer(x, g_norm, p, shift0, s0, *, batch, seq, heads, hd, residual=True):
    (r, k, v, lw, kk, a, g), shift = _rwkv_proj(x, g_norm, shift0, p, batch=batch, seq=seq, tm=TM_RWKV_PROJ)
    o, s = _rwkv_prompt(r, k, v, lw, kk, a, g, s0, p["hp"], batch=batch, seq=seq, heads=heads, hd=hd, tb=TB_MIX)
    y = _out_proj(o, p["w_o"], x if residual else None)
    return y, shift, s


def _gdn_sample_pre_kernel(pm_ref, pba_ref, conv_ref, cw_ref, gp_ref, qkv_ref, conv_out_ref, sc_ref,
                           *, heads, dk, dv):
    key = heads * dk
    ch = 2 * key + heads * dv
    u = pm_ref[:, 0:ch]
    y = u * cw_ref[CONV_W - 1:CONV_W, :]
    for j in range(CONV_W - 1):
        y = y + conv_ref[j] * cw_ref[j:j + 1, :]
        conv_out_ref[j] = conv_ref[j + 1] if j + 1 < CONV_W - 1 else u
    y = _silu(y)
    for c in range(ch // 128):
        cs = slice(c * 128, (c + 1) * 128)
        yc = y[:, cs]
        if c * 128 < key:
            yc = yc * lax.rsqrt(jnp.sum(yc * yc, axis=-1, keepdims=True) + 1e-6) * (dk ** -0.5)
        elif c * 128 < 2 * key:
            yc = yc * lax.rsqrt(jnp.sum(yc * yc, axis=-1, keepdims=True) + 1e-6)
        qkv_ref[:, cs] = yc
    ba = pba_ref[...]
    lane = lax.broadcasted_iota(jnp.int32, ba.shape, 1)
    g = -jnp.exp(gp_ref[0:1, :]) * _softplus(ba + gp_ref[1:2, :])
    sc_ref[...] = jnp.where(lane < heads, _sigmoid(ba), jnp.exp(g))


def _gdn_sample_pre(pm, pba, conv_t, cw_t, gp, *, heads, dk, dv):
    n = pm.shape[0]
    ch = 2 * heads * dk + heads * dv
    return pl.pallas_call(
        functools.partial(_gdn_sample_pre_kernel, heads=heads, dk=dk, dv=dv),
        out_shape=[jax.ShapeDtypeStruct((n, ch), F32),
                   jax.ShapeDtypeStruct((CONV_W - 1, n, ch), F32),
                   jax.ShapeDtypeStruct((n, 128), F32)],
        compiler_params=pltpu.CompilerParams(vmem_limit_bytes=V7X_VMEM_LIMIT),
        name="gdn_sample_pre",
    )(pm, pba, conv_t, cw_t, gp)


SEQ_PER_STEP = 4


def _seq_block(a, nb):
    nd = a.ndim
    return pl.BlockSpec((nb,) + a.shape[1:], lambda b: (b,) + (0,) * (nd - 1))


def _const_block(a):
    nd = a.ndim
    return pl.BlockSpec(a.shape, lambda b: (0,) * nd)


def _gdn_sample_step_kernel(s0_ref, cols_ref, v_ref, z_ref, sc_ref, nw_ref, acc_ref, s_out_ref, o_ref,
                            *, heads, nb):
    del acc_ref
    hh = range(heads)
    for i in range(nb):
        kc = [cols_ref[i, :, h:h + 1] for h in hh]
        qc = [cols_ref[i, :, heads + h:heads + h + 1] for h in hh]
        s0 = [s0_ref[i, h] for h in hh]
        eg = [sc_ref[i, h:h + 1, 1:2] for h in hh]
        ks = [jnp.sum(kc[h] * s0[h], axis=0, keepdims=True) for h in hh]
        s1 = [eg[h] * s0[h] + kc[h] * (sc_ref[i, h:h + 1, 0:1] * (v_ref[i, h:h + 1, :] - eg[h] * ks[h]))
              for h in hh]
        o = [jnp.sum(qc[h] * s1[h], axis=0, keepdims=True) for h in hh]
        ms = [jnp.mean(o[h] * o[h], axis=-1, keepdims=True) for h in hh]
        for h in hh:
            s_out_ref[i, h] = s1[h]
            o_ref[i, h:h + 1, :] = o[h] * lax.rsqrt(ms[h] + RMS_EPS) * nw_ref[...] * _silu(z_ref[i, h:h + 1, :])


def _gdn_sample_step(s_all, layer, s_acc, cols, v, z, sc, nw):
    _, n, heads, dk, dv = s_all.shape
    nb = SEQ_PER_STEP
    assert n % nb == 0
    state = pl.BlockSpec((None, nb, heads, dk, dv), lambda b: (layer, b, 0, 0, 0))
    return pl.pallas_call(
        functools.partial(_gdn_sample_step_kernel, heads=heads, nb=nb),
        grid=(n // nb,),
        in_specs=[state, _seq_block(cols, nb), _seq_block(v, nb), _seq_block(z, nb), _seq_block(sc, nb),
                  pl.BlockSpec((1, dv), lambda b: (0, 0)), pl.BlockSpec(memory_space=pl.ANY)],
        out_specs=[state, _seq_block(v, nb)],
        out_shape=[jax.ShapeDtypeStruct(s_all.shape, F32), jax.ShapeDtypeStruct(v.shape, F32)],
        input_output_aliases={6: 0},
        compiler_params=_cparams(("parallel",)),
        name="gdn_sample_step",
    )(s_all, cols, v, z, sc, nw, s_acc)


def _gdn_sample_layer(x, g, p, conv0, s_all, layer, s_acc, *, heads, dk, dv):
    n = x.shape[0]
    key, val = heads * dk, heads * dv
    ch = 2 * key + val
    pm, pba = _in_proj(x, g, p["w_main"], p["w_ba"])
    qkv, conv_t, sc = _gdn_sample_pre(pm, pba, jnp.transpose(conv0, (1, 0, 2)), p["cw_t"], p["gp"],
                                      heads=heads, dk=dk, dv=dv)
    q_c = jnp.transpose(qkv[:, :key].reshape(n, heads, dk), (0, 2, 1))
    k_c = jnp.transpose(qkv[:, key:2 * key].reshape(n, heads, dk), (0, 2, 1))
    cols = jnp.concatenate([k_c, q_c], axis=-1)
    sc3 = jnp.stack([sc[:, :heads], sc[:, heads:2 * heads]], axis=-1)
    s_acc, o = _gdn_sample_step(s_all, layer, s_acc, cols, qkv[:, 2 * key:].reshape(n, heads, dv),
                                pm[:, ch:].reshape(n, heads, dv), sc3, p["nw"])
    y = _out_proj(o.reshape(n, val), p["w_out"], x)
    return y, jnp.transpose(conv_t, (1, 0, 2)), s_acc


def _mlstm_sample_step_kernel(c0_ref, n0_ref, cols_ref, q_ref, k_ref, v_ref, op_ref, sc_ref, bif_ref, nw_ref,
                              c_out_ref, n_out_ref, m_out_ref, o_ref, *, heads, dk, nb):
    scale = dk ** -0.5
    for i in range(nb):
        gi = sc_ref[i, :, 0:1] + bif_ref[:, 0:1]
        gf = sc_ref[i, :, 1:2] + bif_ref[:, 1:2]
        m0 = sc_ref[i, :, 2:3]
        gi = GATE_CAP * jnp.tanh(gi / GATE_CAP)
        logf = _log_sigmoid(GATE_CAP * jnp.tanh(gf / GATE_CAP))
        m_new = jnp.maximum(logf + m0, gi)
        f_s = jnp.exp(logf + m0 - m_new)
        i_s = jnp.exp(gi - m_new)
        m_out_ref[i] = m_new
        n1 = f_s * n0_ref[i] + i_s * k_ref[i]
        n_out_ref[i] = n1
        den = jnp.sum(q_ref[i] * scale * n1, axis=-1, keepdims=True)
        floor = jnp.exp(-m_new)
        hh = range(heads)
        kc = [cols_ref[i, :, h:h + 1] for h in hh]
        qc = [cols_ref[i, :, heads + h:heads + h + 1] * scale for h in hh]
        c1 = [f_s[h:h + 1, :] * c0_ref[i, h] + i_s[h:h + 1, :] * (kc[h] * v_ref[i, h:h + 1, :]) for h in hh]
        num = [jnp.sum(qc[h] * c1[h], axis=0, keepdims=True) for h in hh]
        h_t = [num[h] / jnp.maximum(jnp.abs(den[h:h + 1, :]), floor[h:h + 1, :]) for h in hh]
        ms = [jnp.mean(h_t[h] * h_t[h], axis=-1, keepdims=True) for h in hh]
        for h in hh:
            c_out_ref[i, h] = c1[h]
            h_n = h_t[h] * lax.rsqrt(ms[h] + RMS_EPS) * nw_ref[h:h + 1, :]
            o_ref[i, h:h + 1, :] = _sigmoid(op_ref[i, h:h + 1, :]) * h_n


def _mlstm_sample_step(c0, n0, cols, q, k, v, o_pre, sc, bif2, nw2):
    n, heads, dk, dv = c0.shape
    nb = SEQ_PER_STEP
    assert n % nb == 0
    full = _const_block

    def blk(a):
        return _seq_block(a, nb)

    m_shape = (n, heads, 1)
    return pl.pallas_call(
        functools.partial(_mlstm_sample_step_kernel, heads=heads, dk=dk, nb=nb),
        grid=(n // nb,),
        in_specs=[blk(c0), blk(n0), blk(cols), blk(q), blk(k), blk(v), blk(o_pre), blk(sc), full(bif2), full(nw2)],
        out_specs=[blk(c0), blk(n0), pl.BlockSpec((nb, heads, 1), lambda b: (b, 0, 0)), blk(v)],
        out_shape=[jax.ShapeDtypeStruct(c0.shape, F32), jax.ShapeDtypeStruct(n0.shape, F32),
                   jax.ShapeDtypeStruct(m_shape, F32), jax.ShapeDtypeStruct(v.shape, F32)],
        compiler_params=_cparams(("parallel",)),
        name="mlstm_sample_step",
    )(c0, n0, cols, q, k, v, o_pre, sc, bif2, nw2)


def _mlstm_sample_layer(x, g, p, c0, n0, m0, *, heads, dk, dv):
    n = x.shape[0]
    qk_w, val = heads * dk, heads * dv
    pm, pif = _in_proj(x, g, p["w_main"], p["w_if"])
    q = pm[:, :qk_w].reshape(n, heads, dk)
    k = pm[:, qk_w:2 * qk_w].reshape(n, heads, dk)
    v = pm[:, 2 * qk_w:2 * qk_w + val].reshape(n, heads, dv)
    o_pre = pm[:, 2 * qk_w + val:].reshape(n, heads, dv)
    cols = jnp.concatenate([jnp.transpose(k, (0, 2, 1)), jnp.transpose(q, (0, 2, 1))], axis=-1)
    sc = jnp.stack([pif[:, :heads], pif[:, heads:2 * heads], m0], axis=-1)
    bif2 = jnp.stack([p["bif"][0, :heads], p["bif"][0, heads:2 * heads]], axis=-1)
    c1, n1, m1, o = _mlstm_sample_step(c0, n0, cols, q, k, v, o_pre, sc, bif2, p["nw"].reshape(heads, dv))
    y = _out_proj(o.reshape(n, val), p["w_out"], x)
    return y, c1, n1, m1[:, :, 0]


def _rwkv_sample_step_kernel(s0_ref, r_ref, k_ref, lw_ref, kk_ref, a_ref, vc_ref, gc_ref, hpc_ref, hpr_ref,
                             s_out_ref, o_ref, *, heads, nb):
    hh = range(heads)
    for i in range(nb):
        r = r_ref[i]
        k = k_ref[i]
        kk = kk_ref[i]
        kk = kk * lax.rsqrt(jnp.maximum(jnp.sum(kk * kk, axis=-1, keepdims=True), 1e-24))
        bv = kk * a_ref[i]
        w = jnp.exp(lw_ref[i])
        bonus = jnp.sum(r * k * hpr_ref[...], axis=-1, keepdims=True)
        s0 = [s0_ref[i, h] for h in hh]
        vc = [vc_ref[i, :, h:h + 1] for h in hh]
        sa = [jnp.sum(s0[h] * (-kk[h:h + 1, :]), axis=-1, keepdims=True) for h in hh]
        s1 = [s0[h] * w[h:h + 1, :] + sa[h] * bv[h:h + 1, :] + vc[h] * k[h:h + 1, :] for h in hh]
        y = [jnp.sum(s1[h] * r[h:h + 1, :], axis=-1, keepdims=True) for h in hh]
        for h in hh:
            s_out_ref[i, h] = s1[h]
            yc = y[h] - jnp.mean(y[h], axis=0, keepdims=True)
            var = jnp.mean(yc * yc, axis=0, keepdims=True)
            yn = yc * lax.rsqrt(var + RW_GN_EPS) * hpc_ref[0, :, h:h + 1] + hpc_ref[1, :, h:h + 1]
            o_ref[i, :, h:h + 1] = (yn + bonus[h:h + 1, :] * vc[h]) * gc_ref[i, :, h:h + 1]


def _rwkv_sample_step(s0, r, k, lw, kk, a, v_c, g_c, hp_c, rk):
    n, heads, hd, _ = s0.shape
    nb = SEQ_PER_STEP
    assert n % nb == 0
    full = _const_block

    def blk(a_):
        return _seq_block(a_, nb)

    return pl.pallas_call(
        functools.partial(_rwkv_sample_step_kernel, heads=heads, nb=nb),
        grid=(n // nb,),
        in_specs=[blk(s0), blk(r), blk(k), blk(lw), blk(kk), blk(a), blk(v_c), blk(g_c), full(hp_c), full(rk)],
        out_specs=[blk(s0), blk(v_c)],
        out_shape=[jax.ShapeDtypeStruct(s0.shape, F32), jax.ShapeDtypeStruct(v_c.shape, F32)],
        compiler_params=_cparams(("parallel",)),
        name="rwkv_sample_step",
    )(s0, r, k, lw, kk, a, v_c, g_c, hp_c, rk)


def _rwkv_sample_layer(x, g_norm, p, shift0, s0, *, heads, hd):
    n, d = x.shape
    (r, k, v, lw, kk, a, g), hn = _rwkv_proj(x, g_norm, shift0, p, batch=n, seq=1, tm=n)

    def rows(z):
        return z.reshape(n, heads, hd)

    def cols(z):
        return jnp.transpose(z.reshape(n, heads, hd), (0, 2, 1))

    hp = p["hp"]
    hp_c = jnp.stack([hp[1].reshape(heads, hd).T, hp[2].reshape(heads, hd).T])
    s1, o_c = _rwkv_sample_step(s0, rows(r), rows(k), rows(lw), rows(kk), rows(a), cols(v), cols(g),
                                hp_c, hp[0].reshape(heads, hd))
    o = jnp.transpose(o_c, (0, 2, 1)).reshape(n, d)
    y = _out_proj(o, p["w_o"], x)
    return y, hn, s1


def _pad_cols(a, n):
    return jnp.pad(a, ((0, 0), (0, n - a.shape[1])))


def _gdn_prep(w_in, conv_w, a_log, dt_bias, norm_w, w_out, *, heads, dk, dv):
    key, val = heads * dk, heads * dv
    ch = 2 * key + val
    main = ch + val
    gp = jnp.zeros((8, 128), F32)
    gp = gp.at[0, heads:2 * heads].set(a_log).at[1, heads:2 * heads].set(dt_bias)
    return dict(w_main=w_in[:, :main].astype(BF16),
                w_ba=_pad_cols(w_in[:, main:], 128).astype(BF16),
                cw_t=jnp.pad(conv_w.T, ((0, 8 - CONV_W), (0, 0))),
                gp=gp, nw=norm_w[None, :], w_out=w_out.astype(BF16))


def _gdn_prompt_layer(x, g, p, conv0, s0, *, batch, seq, heads, dk, dv, norm=True, residual=True):
    pm, pba = _in_proj(x, g, p["w_main"], p["w_ba"], norm=norm)
    conv0 = jnp.pad(conv0, ((0, 0), (8 - (CONV_W - 1), 0), (0, 0)))
    o, conv, s = _gdn_prompt(pm, pba, conv0, s0, p["cw_t"], p["gp"], p["nw"],
                             batch=batch, seq=seq, heads=heads, dk=dk, dv=dv, tb=TB_MIX)
    y = _out_proj(o, p["w_out"], x if residual else None)
    return y, conv[:, 8 - (CONV_W - 1):, :], s


def _trunk(x, states, w, *, batch, seq):
    conv_in, gs_in, c_in, n_in, m_in, shift_in, rs_in = states
    depth = w["norm_mix"].shape[0]
    gh, gdk, gdv = gs_in.shape[2:]
    mh, mdk, mdv = c_in.shape[2:]
    rh, rhd = rs_in.shape[2:4]
    prompt = seq > 1
    outs = [[] for _ in range(7)]
    gs_acc = None if prompt else jnp.zeros_like(gs_in)
    for i in range(depth):
        j = i // 3
        g = w["norm_mix"][i][None, :]
        if i % 3 == 0:
            p = w["gdn"][j]
            if prompt:
                x, cb, s = _gdn_prompt_layer(x, g, p, conv_in[j], gs_in[j], batch=batch, seq=seq,
                                             heads=gh, dk=gdk, dv=gdv)
                outs[1].append(s)
            else:
                x, cb, gs_acc = _gdn_sample_layer(x, g, p, conv_in[j], gs_in, j, gs_acc, heads=gh, dk=gdk, dv=gdv)
            outs[0].append(cb)
        elif i % 3 == 1:
            p = w["ml"][j]
            if prompt:
                x, c, n, m = _mlstm_prompt_layer(x, g, p, c_in[j], n_in[j], m_in[j], batch=batch, seq=seq,
                                                 heads=mh, dk=mdk, dv=mdv)
            else:
                x, c, n, m = _mlstm_sample_layer(x, g, p, c_in[j], n_in[j], m_in[j], heads=mh, dk=mdk, dv=mdv)
            outs[2].append(c)
            outs[3].append(n)
            outs[4].append(m)
        else:
            p = w["rw"][j]
            if prompt:
                x, sh, s = _rwkv_prompt_layer(x, g, p, shift_in[j], rs_in[j], batch=batch, seq=seq,
                                              heads=rh, hd=rhd)
            else:
                x, sh, s = _rwkv_sample_layer(x, g, p, shift_in[j], rs_in[j], heads=rh, hd=rhd)
            outs[5].append(sh)
            outs[6].append(s)
        x = _ffn(x, w["norm_ffn"][i][None, :], w["ffn_w1"][i], w["ffn_w2"][i], tm=TM_FFN, tf=TF_FFN)
    y = _rmsnorm(x, w["norm_final"][None, :], tm=TM_FFN)
    new = [jnp.stack(z, axis=0) if z else None for z in outs]
    if not prompt:
        new[1] = gs_acc
    return y, tuple(new)


def kernel(x_prompt, x_sample, state_gdn_conv, state_gdn_S, state_mlstm_C, state_mlstm_n, state_mlstm_m, state_rwkv_shift, state_rwkv_S, norm_mix, norm_ffn, norm_final, gdn_w_in, gdn_conv_w, gdn_a_log, gdn_dt_bias, gdn_norm_w, gdn_w_out, ml_w_in, ml_b_if, ml_norm_w, ml_w_out, rw_mu, rw_w_rkv, rw_w_o, rw_w0, rw_w1, rw_w2, rw_a0, rw_a1, rw_a2, rw_g1, rw_g2, rw_k_k, rw_k_a, rw_r_k, rw_lnx_w, rw_lnx_b, ffn_w1, ffn_w2):
    gh, gdk, gdv = state_gdn_S.shape[2:]
    mh, mdk, mdv = state_mlstm_C.shape[2:]
    w = dict(
        norm_mix=norm_mix, norm_ffn=norm_ffn, norm_final=norm_final,
        ffn_w1=[ffn_w1[i].astype(BF16) for i in range(ffn_w1.shape[0])],
        ffn_w2=[ffn_w2[i].astype(BF16) for i in range(ffn_w2.shape[0])],
        gdn=[_gdn_prep(gdn_w_in[j], gdn_conv_w[j], gdn_a_log[j], gdn_dt_bias[j], gdn_norm_w[j], gdn_w_out[j],
                       heads=gh, dk=gdk, dv=gdv) for j in range(gdn_w_in.shape[0])],
        ml=[_mlstm_prep(ml_w_in[j], ml_b_if[j], ml_norm_w[j], ml_w_out[j], heads=mh, dk=mdk, dv=mdv)
            for j in range(ml_w_in.shape[0])],
        rw=[_rwkv_prep(rw_mu[j], rw_w_rkv[j], rw_w_o[j], rw_w0[j], rw_w1[j], rw_w2[j], rw_a0[j], rw_a1[j],
                       rw_a2[j], rw_g1[j], rw_g2[j], rw_k_k[j], rw_k_a[j], rw_r_k[j], rw_lnx_w[j], rw_lnx_b[j])
            for j in range(rw_mu.shape[0])])
    sample_states = (state_gdn_conv, state_gdn_S, state_mlstm_C, state_mlstm_n, state_mlstm_m,
                     state_rwkv_shift, state_rwkv_S)
    bp, tp, d = x_prompt.shape
    bs, ts, _ = x_sample.shape
    assert ts == 1
    prompt_states = tuple(jnp.zeros((s.shape[0], bp) + s.shape[2:], s.dtype) for s in sample_states)
    y_p, new_p = _trunk(x_prompt.reshape(bp * tp, d), prompt_states, w, batch=bp, seq=tp)
    y_s, new_s = _trunk(x_sample.reshape(bs * ts, d), sample_states, w, batch=bs, seq=ts)
    out = [y_p.reshape(bp, tp, d), y_s.reshape(bs, ts, d)]
    for a, b in zip(new_p, new_s):
        out += [a, b]
    return tuple(out)
```

```python
import functools
import math

import jax
import jax.numpy as jnp
from jax import lax
from jax.experimental import pallas as pl
from jax.experimental.pallas import tpu as pltpu

F32 = jnp.float32
BF16 = jnp.bfloat16

RMS_EPS = 1e-6
NEG_BIG = -1e30
GATE_CAP = 15.0
RW_GN_EPS = 64e-5
CONV_W = 4
CHUNK = 64
RW_GROUP = 8
V7X_VMEM_LIMIT = 56 * 1024 * 1024
HI = lax.Precision.HIGHEST


def _cparams(sem):
    return pltpu.CompilerParams(dimension_semantics=sem, vmem_limit_bytes=V7X_VMEM_LIMIT)


def _dot(a, b):
    return jnp.dot(a.astype(BF16), b.astype(BF16), preferred_element_type=F32)


def _dot_nt(a, b):
    return lax.dot_general(a.astype(BF16), b.astype(BF16), (((1,), (1,)), ((), ())),
                           preferred_element_type=F32)


def _dot_tn(a, b):
    return lax.dot_general(a.astype(BF16), b.astype(BF16), (((0,), (0,)), ((), ())),
                           preferred_element_type=F32)


def _dot_hi(a, b):
    return jnp.dot(a, b, preferred_element_type=F32, precision=HI)


def _sigmoid(x):
    return 1.0 / (1.0 + jnp.exp(-x))


def _silu(x):
    return x * _sigmoid(x)


def _softplus(x):
    return jnp.maximum(x, 0.0) + jnp.log(1.0 + jnp.exp(-jnp.abs(x)))


def _log_sigmoid(x):
    return -_softplus(-x)


def _tri_masks(l):
    r = lax.broadcasted_iota(jnp.int32, (l, l), 0)
    c = lax.broadcasted_iota(jnp.int32, (l, l), 1)
    return r >= c, r > c


INV_BASE = 16


def _unit_lower_inverse(mats, l):
    n = mats[0].shape[0]
    r = lax.broadcasted_iota(jnp.int32, (n, n), 0)
    c = lax.broadcasted_iota(jnp.int32, (n, n), 1)
    eye = (r == c).astype(F32)
    size = min(INV_BASE, l)
    shift = size.bit_length() - 1
    diag = (r >> shift) == (c >> shift)
    merges = []
    s = size
    while s < l:
        sh = s.bit_length() - 1
        off = ((r >> (sh + 1)) == (c >> (sh + 1))) & ((r >> sh) > (c >> sh))
        merges.append([jnp.where(off, a, 0.0).astype(BF16) for a in mats])
        s *= 2
    t = [eye - jnp.where(diag, a, 0.0) for a in mats]
    tb = [ti.astype(BF16) for ti in t]
    ab = [jnp.where(diag, a, 0.0).astype(BF16) for a in mats]
    p = [_dot(a, a).astype(BF16) for a in ab]
    k = 2
    while k < size:
        if 2 * k < size:
            both = [_dot(pi, jnp.concatenate([ti, pi], axis=1)) for ti, pi in zip(tb, p)]
            t = [ti + bi[:, :n] for ti, bi in zip(t, both)]
            p = [bi[:, n:].astype(BF16) for bi in both]
        else:
            t = [ti + _dot(pi, ti_b) for ti, ti_b, pi in zip(t, tb, p)]
        tb = [ti.astype(BF16) for ti in t]
        k *= 2
    for a_off in merges:
        x = [_dot(a, ti) for a, ti in zip(a_off, tb)]
        y = [_dot(ti, xi) for ti, xi in zip(tb, x)]
        t = [ti - yi for ti, yi in zip(t, y)]
        tb = [ti.astype(BF16) for ti in t]
    return tb


def _nmm_kernel(*refs, norm, residual, aux):
    x_ref, g_ref, w_ref = refs[:3]
    pos = 3
    res_ref = aux_w_ref = aux_o_ref = None
    if residual:
        res_ref = refs[pos]
        pos += 1
    if aux:
        aux_w_ref = refs[pos]
        pos += 1
    o_ref = refs[pos]
    if aux:
        aux_o_ref = refs[pos + 1]
    xn_ref = refs[-1]

    @pl.when(pl.program_id(1) == 0)
    def _():
        x = x_ref[...]
        if norm:
            x = x * lax.rsqrt(jnp.mean(x * x, axis=-1, keepdims=True) + RMS_EPS) * g_ref[...]
        xn_ref[...] = x.astype(BF16)
        if aux:
            aux_o_ref[...] = jnp.dot(xn_ref[...], aux_w_ref[...], preferred_element_type=F32)

    y = jnp.dot(xn_ref[...], w_ref[...], preferred_element_type=F32)
    if residual:
        y = res_ref[...] + y
    o_ref[...] = y


def _nmm(x, g, w, res=None, w_aux=None, *, norm, tm, tn):
    m, k = x.shape
    n = w.shape[1]
    tm, tn = min(tm, m), min(tn, n)
    assert m % tm == 0 and n % tn == 0
    in_specs = [pl.BlockSpec((tm, k), lambda i, j: (i, 0)),
                pl.BlockSpec((1, k), lambda i, j: (0, 0)),
                pl.BlockSpec((k, tn), lambda i, j: (0, j))]
    args = [x, g, w]
    out_specs = [pl.BlockSpec((tm, tn), lambda i, j: (i, j))]
    out_shape = [jax.ShapeDtypeStruct((m, n), F32)]
    if res is not None:
        in_specs.append(pl.BlockSpec((tm, tn), lambda i, j: (i, j)))
        args.append(res)
    if w_aux is not None:
        na = w_aux.shape[1]
        in_specs.append(pl.BlockSpec((k, na), lambda i, j: (0, 0)))
        args.append(w_aux)
        out_specs.append(pl.BlockSpec((tm, na), lambda i, j: (i, 0)))
        out_shape.append(jax.ShapeDtypeStruct((m, na), F32))
    out = pl.pallas_call(
        functools.partial(_nmm_kernel, norm=norm, residual=res is not None, aux=w_aux is not None),
        grid=(m // tm, n // tn),
        in_specs=in_specs,
        out_specs=out_specs,
        out_shape=out_shape,
        scratch_shapes=[pltpu.VMEM((tm, k), BF16)],
        compiler_params=_cparams(("parallel", "arbitrary")),
        name="nmm",
    )(*args)
    return out if w_aux is not None else out[0]


TM_IN_PROJ = 512
TM_OUT_PROJ = 1024
TM_FFN, TF_FFN = 1024, 512
TB_MIX = 256
TM_RWKV_PROJ = 256


def _in_proj(x, g, w_main, w_aux, *, norm=True):
    return _nmm(x, g, w_main, None, w_aux, norm=norm, tm=TM_IN_PROJ, tn=w_main.shape[1])


def _out_proj(o, w_out, res):
    ones = jnp.ones((1, o.shape[1]), F32)
    return _nmm(o, ones, w_out, res, norm=False, tm=TM_OUT_PROJ, tn=w_out.shape[1])


def _ffn_kernel(x_ref, g_ref, w1_ref, w2_ref, o_ref, xn_ref, acc_ref):
    f = pl.program_id(1)

    @pl.when(f == 0)
    def _():
        x = x_ref[...]
        x = x * lax.rsqrt(jnp.mean(x * x, axis=-1, keepdims=True) + RMS_EPS) * g_ref[...]
        xn_ref[...] = x.astype(BF16)
        acc_ref[...] = jnp.zeros_like(acc_ref)

    h = jnp.dot(xn_ref[...], w1_ref[...], preferred_element_type=F32)
    a = jnp.square(jnp.maximum(h, 0.0)).astype(BF16)
    acc_ref[...] += jnp.dot(a, w2_ref[...], preferred_element_type=F32)

    @pl.when(f == pl.num_programs(1) - 1)
    def _():
        o_ref[...] = x_ref[...] + acc_ref[...]


def _ffn(x, g, w1, w2, *, tm, tf):
    m, d = x.shape
    dff = w1.shape[1]
    tm, tf = min(tm, m), min(tf, dff)
    assert m % tm == 0 and dff % tf == 0
    return pl.pallas_call(
        _ffn_kernel,
        grid=(m // tm, dff // tf),
        in_specs=[pl.BlockSpec((tm, d), lambda i, j: (i, 0)),
                  pl.BlockSpec((1, d), lambda i, j: (0, 0)),
                  pl.BlockSpec((d, tf), lambda i, j: (0, j)),
                  pl.BlockSpec((tf, d), lambda i, j: (j, 0))],
        out_specs=pl.BlockSpec((tm, d), lambda i, j: (i, 0)),
        out_shape=jax.ShapeDtypeStruct((m, d), F32),
        scratch_shapes=[pltpu.VMEM((tm, d), BF16), pltpu.VMEM((tm, d), F32)],
        compiler_params=_cparams(("parallel", "arbitrary")),
        name="ffn",
    )(x, g, w1, w2)


def _norm_kernel(x_ref, g_ref, o_ref):
    x = x_ref[...]
    o_ref[...] = x * lax.rsqrt(jnp.mean(x * x, axis=-1, keepdims=True) + RMS_EPS) * g_ref[...]


def _rmsnorm(x, g, *, tm):
    m, d = x.shape
    tm = min(tm, m)
    assert m % tm == 0
    return pl.pallas_call(
        _norm_kernel,
        grid=(m // tm,),
        in_specs=[pl.BlockSpec((tm, d), lambda i: (i, 0)), pl.BlockSpec((1, d), lambda i: (0, 0))],
        out_specs=pl.BlockSpec((tm, d), lambda i: (i, 0)),
        out_shape=jax.ShapeDtypeStruct((m, d), F32),
        compiler_params=_cparams(("parallel",)),
        name="rmsnorm",
    )(x, g)


def _gdn_prompt_kernel(pm_ref, pba_ref, conv0_ref, s0_ref, cw_ref, gp_ref, nw_ref,
                       o_ref, conv_out_ref, s_out_ref,
                       full_ref, qkv_ref, s_ref, u_ref, w_ref, qd_ref, kd_ref, qk_ref, gl_ref,
                       *, tb, heads, dk, dv):
    t = pl.program_id(1)
    key = heads * dk
    ch = 2 * key + heads * dv
    l = CHUNK

    @pl.when(t == 0)
    def _():
        full_ref[0:8, :] = conv0_ref[0]
        s_ref[...] = s0_ref[0]

    full_ref[8:8 + tb, :] = pm_ref[:, 0:ch]
    for c in range(ch // 128):
        cs = slice(c * 128, (c + 1) * 128)
        ext = full_ref[:, cs]
        y = ext[8:] * cw_ref[CONV_W - 1:CONV_W, cs]
        for s in range(1, CONV_W):
            y = y + pltpu.roll(ext, s, 0)[8:] * cw_ref[CONV_W - 1 - s:CONV_W - s, cs]
        y = _silu(y)
        if c * 128 < key:
            y = y * lax.rsqrt(jnp.sum(y * y, axis=-1, keepdims=True) + 1e-6) * (dk ** -0.5)
        elif c * 128 < 2 * key:
            y = y * lax.rsqrt(jnp.sum(y * y, axis=-1, keepdims=True) + 1e-6)
        qkv_ref[:, cs] = y
    full_ref[0:8, :] = full_ref[tb:tb + 8, :]

    a_log = gp_ref[0:1, :]
    dt_bias = gp_ref[1:2, :]
    hh = range(heads)
    l2 = 2 * l
    r2 = lax.broadcasted_iota(jnp.int32, (l2, l2), 0)
    c2 = lax.broadcasted_iota(jnp.int32, (l2, l2), 1)
    same = (r2 >= l) == (c2 >= l)
    incl = same & (r2 >= c2)
    strict = same & (r2 > c2)
    tril = incl.astype(F32)
    first = lax.broadcasted_iota(jnp.int32, (l2, 128), 0) < l

    pairs = range(tb // l2)
    rows = [slice(pi * l2, (pi + 1) * l2) for pi in pairs]
    beta_all, gc, gc_t, g_end = [], [], [], []
    for pi in pairs:
        ba = pba_ref[rows[pi], :]
        beta_all.append(_sigmoid(ba))
        g_all = -jnp.exp(a_log) * _softplus(ba + dt_bias)
        gc.append(_dot_hi(tril, g_all))
        gc_t.append(gc[pi].T)
        g_end.append(jnp.where(first, gc[pi][l - 1:l, :], gc[pi][l2 - 1:l2, :]))
        gl_ref[2 * pi:2 * pi + 1, :] = jnp.exp(gc[pi][l - 1:l, :])
        gl_ref[2 * pi + 1:2 * pi + 2, :] = jnp.exp(gc[pi][l2 - 1:l2, :])
    cc = [(pi, h) for pi in pairs for h in hh]
    b_col = [beta_all[pi][:, h:h + 1] for pi, h in cc]
    gi = [gc[pi][:, heads + h:heads + h + 1] for pi, h in cc]
    q = [qkv_ref[rows[pi], h * dk:(h + 1) * dk] for pi, h in cc]
    k = [qkv_ref[rows[pi], key + h * dk:key + (h + 1) * dk] for pi, h in cc]
    v = [qkv_ref[rows[pi], 2 * key + h * dv:2 * key + (h + 1) * dv] for pi, h in cc]
    nc = range(len(cc))
    dmat = [jnp.where(incl, jnp.exp(jnp.where(incl, gi[i] - gc_t[pi][heads + h:heads + h + 1, :], 0.0)), 0.0)
            for i, (pi, h) in enumerate(cc)]
    kb = [k[i] * b_col[i] for i in nc]
    kk = [_dot_nt(kb[i], k[i]) for i in nc]
    qk = [_dot_nt(q[i], k[i]) for i in nc]
    t_inv = _unit_lower_inverse([jnp.where(strict, kk[i] * dmat[i], 0.0) for i in nc], l)
    egi = [jnp.exp(gi[i]) for i in nc]
    sol = [_dot(t_inv[i], jnp.concatenate([v[i] * b_col[i], kb[i] * egi[i]], axis=-1)) for i in nc]
    for i, (pi, h) in enumerate(cc):
        hs = slice(h * dk, (h + 1) * dk)
        u_ref[rows[pi], h * dv:(h + 1) * dv] = sol[i][:, :dv]
        w_ref[rows[pi], hs] = sol[i][:, dv:].astype(BF16)
        qd_ref[rows[pi], hs] = (q[i] * egi[i]).astype(BF16)
        kd_ref[rows[pi], hs] = (k[i] * jnp.exp(g_end[pi][:, heads + h:heads + h + 1] - gi[i])).astype(BF16)
        qkm = jnp.where(incl, qk[i] * dmat[i], 0.0).astype(BF16)
        qk_ref[h, pi * l2:pi * l2 + l, :] = qkm[:l, :l]
        qk_ref[h, pi * l2 + l:(pi + 1) * l2, :] = qkm[l:, l:]

    for ci in range(tb // l):
        rows = slice(ci * l, (ci + 1) * l)
        s = [s_ref[h] for h in hh]
        wq = [_dot(jnp.concatenate([w_ref[rows, h * dk:(h + 1) * dk], qd_ref[rows, h * dk:(h + 1) * dk]], axis=0),
                   s[h]) for h in hh]
        v_new = [u_ref[rows, h * dv:(h + 1) * dv] - wq[h][:l] for h in hh]
        o2 = [_dot(qk_ref[h, rows, :], v_new[h]) for h in hh]
        ds = [_dot_tn(kd_ref[rows, h * dk:(h + 1) * dk], v_new[h]) for h in hh]
        for h in hh:
            s_ref[h] = s[h] * gl_ref[ci:ci + 1, heads + h:heads + h + 1] + ds[h]
            o = wq[h][l:] + o2[h]
            z = pm_ref[rows, ch + h * dv:ch + (h + 1) * dv]
            o = o * lax.rsqrt(jnp.mean(o * o, axis=-1, keepdims=True) + RMS_EPS) * nw_ref[...]
            o_ref[rows, h * dv:(h + 1) * dv] = o * _silu(z)

    @pl.when(t == pl.num_programs(1) - 1)
    def _():
        conv_out_ref[0] = full_ref[0:8, :]
        s_out_ref[0] = s_ref[...]


def _gdn_prompt(pm, pba, conv0, s0, cw_t, gp, nw, *, batch, seq, heads, dk, dv, tb):
    key, val = heads * dk, heads * dv
    ch = 2 * key + val
    tb = min(tb, seq)
    assert seq % tb == 0 and tb % CHUNK == 0
    nt = seq // tb
    return pl.pallas_call(
        functools.partial(_gdn_prompt_kernel, tb=tb, heads=heads, dk=dk, dv=dv),
        grid=(batch, nt),
        in_specs=[pl.BlockSpec((tb, ch + val), lambda b, t: (b * nt + t, 0)),
                  pl.BlockSpec((tb, 128), lambda b, t: (b * nt + t, 0)),
                  pl.BlockSpec((1, 8, ch), lambda b, t: (b, 0, 0)),
                  pl.BlockSpec((1, heads, dk, dv), lambda b, t: (b, 0, 0, 0)),
                  pl.BlockSpec((8, ch), lambda b, t: (0, 0)),
                  pl.BlockSpec((8, 128), lambda b, t: (0, 0)),
                  pl.BlockSpec((1, dv), lambda b, t: (0, 0))],
        out_specs=[pl.BlockSpec((tb, val), lambda b, t: (b * nt + t, 0)),
                   pl.BlockSpec((1, 8, ch), lambda b, t: (b, 0, 0)),
                   pl.BlockSpec((1, heads, dk, dv), lambda b, t: (b, 0, 0, 0))],
        out_shape=[jax.ShapeDtypeStruct((batch * seq, val), F32),
                   jax.ShapeDtypeStruct((batch, 8, ch), F32),
                   jax.ShapeDtypeStruct((batch, heads, dk, dv), F32)],
        scratch_shapes=[pltpu.VMEM((tb + 8, ch), F32), pltpu.VMEM((tb, ch), F32),
                        pltpu.VMEM((heads, dk, dv), F32),
                        pltpu.VMEM((tb, val), F32), pltpu.VMEM((tb, key), BF16),
                        pltpu.VMEM((tb, key), BF16), pltpu.VMEM((tb, key), BF16),
                        pltpu.VMEM((heads, tb, CHUNK), BF16), pltpu.VMEM((max(8, tb // CHUNK), 128), F32)],
        compiler_params=_cparams(("parallel", "arbitrary")),
        name="gdn_prompt",
    )(pm, pba, conv0, s0, cw_t, gp, nw)


def _mlstm_prompt_kernel(pm_ref, pif_ref, bif_ref, c0_ref, n0_ref, m0_ref, nw_ref,
                         o_ref, c_out_ref, n_out_ref, m_out_ref,
                         c_ref, n_ref, m_ref, *, tb, heads, dk, dv):
    t = pl.program_id(1)
    l = CHUNK
    qk_w = heads * dk
    v_off = 2 * qk_w
    o_off = v_off + heads * dv

    @pl.when(t == 0)
    def _():
        c_ref[...] = c0_ref[0]
        n_ref[...] = n0_ref[0]
        m_ref[...] = m0_ref[0]

    incl, _ = _tri_masks(l)
    tril = incl.astype(F32)

    def chunk(ci, carry):
        r0 = pl.multiple_of(ci * l, l)
        rows = pl.ds(r0, l)
        gates = pif_ref[rows, :] + bif_ref[...]
        gates = GATE_CAP * jnp.tanh(gates / GATE_CAP)
        bcum = _dot_hi(tril, _log_sigmoid(gates))
        bcum_t = bcum.T
        gates_t = gates.T
        hh = range(heads)
        bi = [bcum[:, heads + h:heads + h + 1] for h in hh]
        ii = [gates[:, h:h + 1] for h in hh]
        b_last = [bcum[l - 1:l, heads + h:heads + h + 1] for h in hh]
        q = [pm_ref[rows, h * dk:(h + 1) * dk] * (dk ** -0.5) for h in hh]
        k = [pm_ref[rows, qk_w + h * dk:qk_w + (h + 1) * dk] for h in hh]
        v = [pm_ref[rows, v_off + h * dv:v_off + (h + 1) * dv] for h in hh]
        qk = [_dot_nt(q[h], k[h]) for h in hh]
        c_mat = [c_ref[h] for h in hh]
        qc = [_dot(q[h], c_mat[h]) for h in hh]
        dlog = [jnp.where(incl, bi[h] - bcum_t[heads + h:heads + h + 1, :] + gates_t[h:h + 1, :], NEG_BIG)
                for h in hh]
        m_intra = [jnp.max(dlog[h], axis=-1, keepdims=True) for h in hh]
        p = [jnp.where(incl, jnp.exp(dlog[h] - m_intra[h]), 0.0) * qk[h] for h in hh]
        num_intra = [_dot(p[h], v[h]) for h in hh]
        a_log = [b_last[h] - bi[h] + ii[h] for h in hh]
        m_chunk = [jnp.max(a_log[h], axis=0, keepdims=True) for h in hh]
        kw = [k[h] * jnp.exp(a_log[h] - m_chunk[h]) for h in hh]
        kv_chunk = [_dot_tn(kw[h], v[h]) for h in hh]
        den_intra = [jnp.sum(p[h], axis=-1, keepdims=True) for h in hh]
        m_prev = [m_ref[h:h + 1, 0:1] for h in hh]
        n_vec = [n_ref[h:h + 1, :] for h in hh]
        qn = [jnp.sum(q[h] * n_vec[h], axis=-1, keepdims=True) for h in hh]
        m_t = [jnp.maximum(bi[h] + m_prev[h], m_intra[h]) for h in hh]
        s_inter = [jnp.exp(bi[h] + m_prev[h] - m_t[h]) for h in hh]
        s_intra = [jnp.exp(m_intra[h] - m_t[h]) for h in hh]
        den = [s_inter[h] * qn[h] + s_intra[h] * den_intra[h] for h in hh]
        h_t = [(s_inter[h] * qc[h] + s_intra[h] * num_intra[h])
               / jnp.maximum(jnp.abs(den[h]), jnp.exp(-m_t[h])) for h in hh]
        ms = [jnp.mean(h_t[h] * h_t[h], axis=-1, keepdims=True) for h in hh]
        for h in hh:
            k_chunk = jnp.sum(kw[h], axis=0, keepdims=True)
            m_new = jnp.maximum(b_last[h] + m_prev[h], m_chunk[h])
            f_s = jnp.exp(b_last[h] + m_prev[h] - m_new)
            i_s = jnp.exp(m_chunk[h] - m_new)
            c_ref[h] = f_s * c_mat[h] + i_s * kv_chunk[h]
            n_ref[h:h + 1, :] = f_s * n_vec[h] + i_s * k_chunk
            m_ref[h:h + 1, :] = jnp.broadcast_to(m_new, (1, m_ref.shape[1]))
            h_n = h_t[h] * lax.rsqrt(ms[h] + RMS_EPS) * nw_ref[:, h * dv:(h + 1) * dv]
            o_pre = pm_ref[rows, o_off + h * dv:o_off + (h + 1) * dv]
            o_ref[rows, h * dv:(h + 1) * dv] = _sigmoid(o_pre) * h_n
        return carry

    lax.fori_loop(0, tb // l, chunk, 0)

    @pl.when(t == pl.num_programs(1) - 1)
    def _():
        c_out_ref[0] = c_ref[...]
        n_out_ref[0] = n_ref[...]
        m_out_ref[0] = m_ref[...]


def _mlstm_prompt(pm, pif, bif, c0, n0, m0, nw, *, batch, seq, heads, dk, dv, tb):
    width = pm.shape[1]
    val = heads * dv
    tb = min(tb, seq)
    assert seq % tb == 0 and tb % CHUNK == 0
    nt = seq // tb
    return pl.pallas_call(
        functools.partial(_mlstm_prompt_kernel, tb=tb, heads=heads, dk=dk, dv=dv),
        grid=(batch, nt),
        in_specs=[pl.BlockSpec((tb, width), lambda b, t: (b * nt + t, 0)),
                  pl.BlockSpec((tb, 128), lambda b, t: (b * nt + t, 0)),
                  pl.BlockSpec((1, 128), lambda b, t: (0, 0)),
                  pl.BlockSpec((1, heads, dk, dv), lambda b, t: (b, 0, 0, 0)),
                  pl.BlockSpec((1, 8, dk), lambda b, t: (b, 0, 0)),
                  pl.BlockSpec((1, 8, 128), lambda b, t: (b, 0, 0)),
                  pl.BlockSpec((1, val), lambda b, t: (0, 0))],
        out_specs=[pl.BlockSpec((tb, val), lambda b, t: (b * nt + t, 0)),
                   pl.BlockSpec((1, heads, dk, dv), lambda b, t: (b, 0, 0, 0)),
                   pl.BlockSpec((1, 8, dk), lambda b, t: (b, 0, 0)),
                   pl.BlockSpec((1, 8, 128), lambda b, t: (b, 0, 0))],
        out_shape=[jax.ShapeDtypeStruct((batch * seq, val), F32),
                   jax.ShapeDtypeStruct((batch, heads, dk, dv), F32),
                   jax.ShapeDtypeStruct((batch, 8, dk), F32),
                   jax.ShapeDtypeStruct((batch, 8, 128), F32)],
        scratch_shapes=[pltpu.VMEM((heads, dk, dv), F32), pltpu.VMEM((8, dk), F32),
                        pltpu.VMEM((8, 128), F32)],
        compiler_params=_cparams(("parallel", "arbitrary")),
        name="mlstm_prompt",
    )(pm, pif, bif, c0, n0, m0, nw)


def _mlstm_prep(w_in, b_if, norm_w, w_out, *, heads, dk, dv):
    main = 2 * heads * dk + 2 * heads * dv
    return dict(w_main=w_in[:, :main].astype(BF16),
                w_if=_pad_cols(w_in[:, main:], 128).astype(BF16),
                bif=_pad_cols(b_if[None, :], 128), nw=norm_w[None, :], w_out=w_out.astype(BF16))


def _mlstm_prompt_layer(x, g, p, c0, n0, m0, *, batch, seq, heads, dk, dv, norm=True, residual=True):
    pm, pif = _in_proj(x, g, p["w_main"], p["w_if"], norm=norm)
    n0p = jnp.pad(n0, ((0, 0), (0, 8 - heads), (0, 0)))
    m0p = jnp.broadcast_to(jnp.pad(m0, ((0, 0), (0, 8 - heads)))[:, :, None], (batch, 8, 128))
    o, c, n, m = _mlstm_prompt(pm, pif, p["bif"], c0, n0p, m0p, p["nw"],
                               batch=batch, seq=seq, heads=heads, dk=dk, dv=dv, tb=TB_MIX)
    y = _out_proj(o, p["w_out"], x if residual else None)
    return y, c, n[:, :heads, :], m[:, :heads, 0]


def _rwkv_proj_body(h, prev, mu_ref, wrkv_ref, w1_ref, w2_ref, a1_ref, a2_ref, g1_ref, g2_ref,
                    vec_ref, r_ref, k_ref, v_ref, lw_ref, kk_ref, a_ref, g_ref):
    xx = prev - h

    def mix(j):
        return (h + xx * mu_ref[j:j + 1, :]).astype(BF16)

    w0, a0, k_k, k_a = (vec_ref[j:j + 1, :] for j in range(4))
    r_ref[...] = jnp.dot(mix(0), wrkv_ref[0], preferred_element_type=F32)
    lora_w = _dot(jnp.tanh(_dot(mix(1), w1_ref[...])), w2_ref[...])
    w_log = -_softplus(-(w0 + lora_w)) - 0.5
    lw_ref[...] = -jnp.exp(w_log)
    k = jnp.dot(mix(2), wrkv_ref[1], preferred_element_type=F32)
    v_ref[...] = jnp.dot(mix(3), wrkv_ref[2], preferred_element_type=F32)
    a = _sigmoid(a0 + _dot(_dot(mix(4), a1_ref[...]), a2_ref[...]))
    g_ref[...] = _dot(_sigmoid(_dot(mix(5), g1_ref[...])), g2_ref[...])
    kk_ref[...] = k * k_k
    k_ref[...] = k * (1.0 + (a - 1.0) * k_a)
    a_ref[...] = a


def _rwkv_proj_sample_kernel(x_ref, gn_ref, prev_ref, *refs):
    h = _norm_rows(x_ref[...], gn_ref[...])
    hn_ref = refs[-1]
    hn_ref[...] = h
    _rwkv_proj_body(h, prev_ref[...], *refs[:-1])


def _rwkv_proj_prompt_kernel(x_ref, gn_ref, shift0_ref, *refs):
    carry_ref = refs[-1]
    shift_out_ref = refs[-2]
    t = pl.program_id(1)

    @pl.when(t == 0)
    def _():
        carry_ref[...] = shift0_ref[0]

    h = _norm_rows(x_ref[...], gn_ref[...])
    rows = h.shape[0]
    first = lax.broadcasted_iota(jnp.int32, h.shape, 0) == 0
    prev = jnp.where(first, carry_ref[0:1, :], pltpu.roll(h, 1, 0))
    carry_ref[0:1, :] = h[rows - 1:rows, :]
    _rwkv_proj_body(h, prev, *refs[:-2])

    @pl.when(t == pl.num_programs(1) - 1)
    def _():
        shift_out_ref[0] = carry_ref[...]


def _norm_rows(x, g):
    return x * lax.rsqrt(jnp.mean(x * x, axis=-1, keepdims=True) + RMS_EPS) * g


def _rwkv_proj(x, g_norm, shift0, p, *, batch, seq, tm):
    m, d = x.shape
    consts = [p["mu"], p["w_rkv"], p["w1"], p["w2"], p["a1"], p["a2"], p["g1"], p["g2"], p["vec"]]
    if seq == 1:
        row = pl.BlockSpec((m, d), lambda i: (0, 0))
        out = pl.pallas_call(
            _rwkv_proj_sample_kernel,
            grid=(1,),
            in_specs=[row, pl.BlockSpec((1, d), lambda i: (0, 0)), row] + [_const_block(a) for a in consts],
            out_specs=[row] * 8,
            out_shape=[jax.ShapeDtypeStruct((m, d), F32)] * 8,
            compiler_params=_cparams(("arbitrary",)),
            name="rwkv_proj_sample",
        )(x, g_norm, shift0, *consts)
        return out[:7], out[7]
    tm = min(tm, seq)
    assert seq % tm == 0
    nt = seq // tm
    row = pl.BlockSpec((tm, d), lambda b, t: (b * nt + t, 0))
    st = pl.BlockSpec((1, 8, d), lambda b, t: (b, 0, 0))

    def full(a):
        nd = a.ndim
        return pl.BlockSpec(a.shape, lambda b, t: (0,) * nd)

    shift0_p = jnp.pad(shift0[:, None, :], ((0, 0), (0, 7), (0, 0)))
    out = pl.pallas_call(
        _rwkv_proj_prompt_kernel,
        grid=(batch, nt),
        in_specs=[row, full(g_norm), st] + [full(a) for a in consts],
        out_specs=[row] * 7 + [st],
        out_shape=[jax.ShapeDtypeStruct((m, d), F32)] * 7 + [jax.ShapeDtypeStruct((batch, 8, d), F32)],
        scratch_shapes=[pltpu.VMEM((8, d), F32)],
        compiler_params=_cparams(("parallel", "arbitrary")),
        name="rwkv_proj_prompt",
    )(x, g_norm, shift0_p, *consts)
    return out[:7], out[7][:, 0, :]


def _rwkv_prompt_kernel(r_ref, k_ref, v_ref, lw_ref, kk_ref, a_ref, g_ref, s0_ref, hp_ref,
                        o_ref, s_out_ref, s_ref, rr_ref, yy_ref, mx_ref, n0_ref, gw_ref, gb_ref, el_ref,
                        *, tb, heads, hd):
    t = pl.program_id(1)
    l = CHUNK

    @pl.when(t == 0)
    def _():
        s_ref[...] = s0_ref[0]

    hh = range(heads)
    hs = [slice(h * hd, (h + 1) * hd) for h in hh]
    l2 = 2 * l
    r2 = lax.broadcasted_iota(jnp.int32, (l2, l2), 0)
    c2 = lax.broadcasted_iota(jnp.int32, (l2, l2), 1)
    same = (r2 >= l) == (c2 >= l)
    incl = same & (r2 >= c2)
    strict = same & (r2 > c2)
    tril = incl.astype(F32)
    first = lax.broadcasted_iota(jnp.int32, (l2, heads * hd), 0) < l
    first2 = (lax.broadcasted_iota(jnp.int32, (2 * l2, hd), 0) & l) == 0
    zeros = jnp.zeros((l2, hd), F32)

    def pair(pi, carry):
        rows = pl.ds(pl.multiple_of(pi * l2, l2), l2)
        lw = lw_ref[rows, :]
        lwc = _dot_hi(tril, lw)
        lw_end = jnp.where(first, lwc[l - 1:l, :], lwc[l2 - 1:l2, :])
        e_in = jnp.exp(lwc)
        e_prev = jnp.exp(lwc - lw)
        e_neg = jnp.exp(-lwc)
        e_end = jnp.exp(lw_end - lwc)
        r = [r_ref[rows, hs[h]] for h in hh]
        v = [v_ref[rows, hs[h]] for h in hh]
        k = [k_ref[rows, hs[h]] for h in hh]
        kk = [kk_ref[rows, hs[h]] for h in hh]
        kk = [kk[h] * lax.rsqrt(jnp.maximum(jnp.sum(kk[h] * kk[h], axis=-1, keepdims=True), 1e-24)) for h in hh]
        bv = [kk[h] * a_ref[rows, hs[h]] for h in hh]
        a_t = [-kk[h] * e_prev[:, hs[h]] for h in hh]
        r_t = [r[h] * e_in[:, hs[h]] for h in hh]
        gm = [_dot_nt(jnp.concatenate([a_t[h], r_t[h]], axis=0),
                      jnp.concatenate([bv[h] * e_neg[:, hs[h]], k[h] * e_neg[:, hs[h]]], axis=0)) for h in hh]
        ak_m = [jnp.where(strict, gm[h][:l2, l2:], 0.0).astype(BF16) for h in hh]
        rbk_m = [jnp.concatenate([jnp.where(incl, gm[h][l2:, :l2], 0.0),
                                  jnp.where(incl, gm[h][l2:, l2:], 0.0)], axis=1).astype(BF16) for h in hh]
        t_inv = _unit_lower_inverse([jnp.where(strict, -gm[h][:l2, :l2], 0.0) for h in hh], l)
        vb = [v[h].astype(BF16) for h in hh]
        akv = [_dot(ak_m[h], vb[h]) for h in hh]
        x1 = [_dot(t_inv[h], jnp.concatenate([a_t[h], akv[h]], axis=1)).astype(BF16) for h in hh]
        low = [jnp.concatenate([x1[h], jnp.concatenate([zeros.astype(BF16), vb[h]], axis=1)], axis=0) for h in hh]
        x2 = [_dot(rbk_m[h], low[h]) for h in hh]
        bk = [jnp.concatenate([bv[h] * e_end[:, hs[h]], k[h] * e_end[:, hs[h]]], axis=0) for h in hh]
        bk2 = [jnp.concatenate([jnp.where(first2, bk[h], 0.0), jnp.where(first2, 0.0, bk[h])], axis=1) for h in hh]
        mn = [_dot_tn(low[h], bk2[h]) for h in hh]
        for c in range(2):
            crow = pl.ds(pl.multiple_of(pi * l2 + c * l, l), l)
            for h in hh:
                mx_ref[h, crow, :] = mn[h][:hd, c * hd:(c + 1) * hd].astype(BF16)
                n0_ref[h, crow, :] = mn[h][hd:, c * hd:(c + 1) * hd]
        for h in hh:
            rr_ref[h, rows, :] = (r_t[h] + x2[h][:, :hd]).astype(BF16)
            yy_ref[h, rows, :] = x2[h][:, hd:]
            g = g_ref[rows, hs[h]]
            bonus = jnp.sum(r[h] * k[h] * hp_ref[0:1, hs[h]], axis=-1, keepdims=True) * v[h]
            gw_ref[h, rows, :] = hp_ref[1:2, hs[h]] * g
            gb_ref[h, rows, :] = (hp_ref[2:3, hs[h]] + bonus) * g
            el_ref[h, pl.ds(pi * 2, 1), :] = jnp.exp(lwc[l - 1:l, hs[h]])
            el_ref[h, pl.ds(pi * 2 + 1, 1), :] = jnp.exp(lwc[l2 - 1:l2, hs[h]])
        return carry

    lax.fori_loop(0, tb // l2, pair, 0)

    for ci in range(tb // l):
        rows = slice(ci * l, (ci + 1) * l)
        s = [s_ref[h] for h in hh]
        y = [_dot_nt(rr_ref[h, rows, :], s[h]) for h in hh]
        sm = [_dot(s[h], mx_ref[h, rows, :]) for h in hh]
        for h in hh:
            s_ref[h] = s[h] * el_ref[h, ci:ci + 1, :] + sm[h] + n0_ref[h, rows, :]
        y = [y[h] + yy_ref[h, rows, :] for h in hh]
        mean = [jnp.mean(y[h], axis=-1, keepdims=True) for h in hh]
        yc = [y[h] - mean[h] for h in hh]
        var = [jnp.mean(yc[h] * yc[h], axis=-1, keepdims=True) for h in hh]
        for h in hh:
            o_ref[rows, hs[h]] = yc[h] * lax.rsqrt(var[h] + RW_GN_EPS) * gw_ref[h, rows, :] + gb_ref[h, rows, :]

    @pl.when(t == pl.num_programs(1) - 1)
    def _():
        s_out_ref[0] = s_ref[...]


def _rwkv_prompt(r, k, v, lw, kk, a, g, s0, hp, *, batch, seq, heads, hd, tb):
    d = heads * hd
    tb = min(tb, seq)
    assert seq % tb == 0 and tb % CHUNK == 0
    nt = seq // tb
    row = pl.BlockSpec((tb, d), lambda b, t: (b * nt + t, 0))
    st = pl.BlockSpec((1, heads, hd, hd), lambda b, t: (b, 0, 0, 0))
    return pl.pallas_call(
        functools.partial(_rwkv_prompt_kernel, tb=tb, heads=heads, hd=hd),
        grid=(batch, nt),
        in_specs=[row] * 7 + [st, pl.BlockSpec((8, d), lambda b, t: (0, 0))],
        out_specs=[row, st],
        out_shape=[jax.ShapeDtypeStruct((batch * seq, d), F32),
                   jax.ShapeDtypeStruct((batch, heads, hd, hd), F32)],
        scratch_shapes=[pltpu.VMEM((heads, hd, hd), F32),
                        pltpu.VMEM((heads, tb, hd), BF16), pltpu.VMEM((heads, tb, hd), F32),
                        pltpu.VMEM((heads, tb, hd), BF16), pltpu.VMEM((heads, tb, hd), F32),
                        pltpu.VMEM((heads, tb, hd), F32), pltpu.VMEM((heads, tb, hd), F32),
                        pltpu.VMEM((heads, max(8, tb // CHUNK), hd), F32)],
        compiler_params=_cparams(("parallel", "arbitrary")),
        name="rwkv_prompt",
    )(r, k, v, lw, kk, a, g, s0, hp)


def _pad_rows(a, n):
    return jnp.pad(a, ((0, n - a.shape[0]), (0, 0)))


def _rwkv_prep(mu, w_rkv, w_o, w0, w1, w2, a0, a1, a2, g1, g2, k_k, k_a, r_k, lnx_w, lnx_b):
    d = w0.shape[0]
    lw = -(-w1.shape[1] // 128) * 128
    la = -(-a1.shape[1] // 128) * 128
    lg = -(-g1.shape[1] // 128) * 128
    return dict(mu=_pad_rows(mu, 8), w_rkv=w_rkv.astype(BF16), w_o=w_o.astype(BF16),
                w1=_pad_cols(w1, lw).astype(BF16), w2=_pad_rows(w2, lw).astype(BF16),
                a1=_pad_cols(a1, la).astype(BF16), a2=_pad_rows(a2, la).astype(BF16),
                g1=_pad_cols(g1, lg).astype(BF16), g2=_pad_rows(g2, lg).astype(BF16),
                vec=_pad_rows(jnp.stack([w0, a0, k_k, k_a]), 8),
                hp=_pad_rows(jnp.stack([r_k.reshape(d), lnx_w, lnx_b]), 8))


def _rwkv_prompt_layer(x, g_norm, p, shift0, s0, *, batch, seq, heads, hd, residual=True):
    (r, k, v, lw, kk, a, g), shift = _rwkv_proj(x, g_norm, shift0, p, batch=batch, seq=seq, tm=TM_RWKV_PROJ)
    o, s = _rwkv_prompt(r, k, v, lw, kk, a, g, s0, p["hp"], batch=batch, seq=seq, heads=heads, hd=hd, tb=TB_MIX)
    y = _out_proj(o, p["w_o"], x if residual else None)
    return y, shift, s


def _gdn_sample_pre_kernel(pm_ref, pba_ref, conv_ref, cw_ref, gp_ref, qkv_ref, conv_out_ref, sc_ref,
                           *, heads, dk, dv):
    key = heads * dk
    ch = 2 * key + heads * dv
    u = pm_ref[:, 0:ch]
    y = u * cw_ref[CONV_W - 1:CONV_W, :]
    for j in range(CONV_W - 1):
        y = y + conv_ref[j] * cw_ref[j:j + 1, :]
        conv_out_ref[j] = conv_ref[j + 1] if j + 1 < CONV_W - 1 else u
    y = _silu(y)
    for c in range(ch // 128):
        cs = slice(c * 128, (c + 1) * 128)
        yc = y[:, cs]
        if c * 128 < key:
            yc = yc * lax.rsqrt(jnp.sum(yc * yc, axis=-1, keepdims=True) + 1e-6) * (dk ** -0.5)
        elif c * 128 < 2 * key:
            yc = yc * lax.rsqrt(jnp.sum(yc * yc, axis=-1, keepdims=True) + 1e-6)
        qkv_ref[:, cs] = yc
    ba = pba_ref[...]
    lane = lax.broadcasted_iota(jnp.int32, ba.shape, 1)
    g = -jnp.exp(gp_ref[0:1, :]) * _softplus(ba + gp_ref[1:2, :])
    sc_ref[...] = jnp.where(lane < heads, _sigmoid(ba), jnp.exp(g))


def _gdn_sample_pre(pm, pba, conv_t, cw_t, gp, *, heads, dk, dv):
    n = pm.shape[0]
    ch = 2 * heads * dk + heads * dv
    return pl.pallas_call(
        functools.partial(_gdn_sample_pre_kernel, heads=heads, dk=dk, dv=dv),
        out_shape=[jax.ShapeDtypeStruct((n, ch), F32),
                   jax.ShapeDtypeStruct((CONV_W - 1, n, ch), F32),
                   jax.ShapeDtypeStruct((n, 128), F32)],
        compiler_params=pltpu.CompilerParams(vmem_limit_bytes=V7X_VMEM_LIMIT),
        name="gdn_sample_pre",
    )(pm, pba, conv_t, cw_t, gp)


SEQ_PER_STEP = 4


def _seq_block(a, nb):
    nd = a.ndim
    return pl.BlockSpec((nb,) + a.shape[1:], lambda b: (b,) + (0,) * (nd - 1))


def _const_block(a):
    nd = a.ndim
    return pl.BlockSpec(a.shape, lambda b: (0,) * nd)


def _gdn_sample_step_kernel(s0_ref, cols_ref, v_ref, z_ref, sc_ref, nw_ref, acc_ref, s_out_ref, o_ref,
                            *, heads, nb):
    del acc_ref
    hh = range(heads)
    for i in range(nb):
        kc = [cols_ref[i, :, h:h + 1] for h in hh]
        qc = [cols_ref[i, :, heads + h:heads + h + 1] for h in hh]
        s0 = [s0_ref[i, h] for h in hh]
        eg = [sc_ref[i, h:h + 1, 1:2] for h in hh]
        ks = [jnp.sum(kc[h] * s0[h], axis=0, keepdims=True) for h in hh]
        s1 = [eg[h] * s0[h] + kc[h] * (sc_ref[i, h:h + 1, 0:1] * (v_ref[i, h:h + 1, :] - eg[h] * ks[h]))
              for h in hh]
        o = [jnp.sum(qc[h] * s1[h], axis=0, keepdims=True) for h in hh]
        ms = [jnp.mean(o[h] * o[h], axis=-1, keepdims=True) for h in hh]
        for h in hh:
            s_out_ref[i, h] = s1[h]
            o_ref[i, h:h + 1, :] = o[h] * lax.rsqrt(ms[h] + RMS_EPS) * nw_ref[...] * _silu(z_ref[i, h:h + 1, :])


def _gdn_sample_step(s_all, layer, s_acc, cols, v, z, sc, nw):
    _, n, heads, dk, dv = s_all.shape
    nb = SEQ_PER_STEP
    assert n % nb == 0
    state = pl.BlockSpec((None, nb, heads, dk, dv), lambda b: (layer, b, 0, 0, 0))
    return pl.pallas_call(
        functools.partial(_gdn_sample_step_kernel, heads=heads, nb=nb),
        grid=(n // nb,),
        in_specs=[state, _seq_block(cols, nb), _seq_block(v, nb), _seq_block(z, nb), _seq_block(sc, nb),
                  pl.BlockSpec((1, dv), lambda b: (0, 0)), pl.BlockSpec(memory_space=pl.ANY)],
        out_specs=[state, _seq_block(v, nb)],
        out_shape=[jax.ShapeDtypeStruct(s_all.shape, F32), jax.ShapeDtypeStruct(v.shape, F32)],
        input_output_aliases={6: 0},
        compiler_params=_cparams(("parallel",)),
        name="gdn_sample_step",
    )(s_all, cols, v, z, sc, nw, s_acc)


def _gdn_sample_layer(x, g, p, conv0, s_all, layer, s_acc, *, heads, dk, dv):
    n = x.shape[0]
    key, val = heads * dk, heads * dv
    ch = 2 * key + val
    pm, pba = _in_proj(x, g, p["w_main"], p["w_ba"])
    qkv, conv_t, sc = _gdn_sample_pre(pm, pba, jnp.transpose(conv0, (1, 0, 2)), p["cw_t"], p["gp"],
                                      heads=heads, dk=dk, dv=dv)
    q_c = jnp.transpose(qkv[:, :key].reshape(n, heads, dk), (0, 2, 1))
    k_c = jnp.transpose(qkv[:, key:2 * key].reshape(n, heads, dk), (0, 2, 1))
    cols = jnp.concatenate([k_c, q_c], axis=-1)
    sc3 = jnp.stack([sc[:, :heads], sc[:, heads:2 * heads]], axis=-1)
    s_acc, o = _gdn_sample_step(s_all, layer, s_acc, cols, qkv[:, 2 * key:].reshape(n, heads, dv),
                                pm[:, ch:].reshape(n, heads, dv), sc3, p["nw"])
    y = _out_proj(o.reshape(n, val), p["w_out"], x)
    return y, jnp.transpose(conv_t, (1, 0, 2)), s_acc


def _mlstm_sample_step_kernel(c0_ref, n0_ref, cols_ref, q_ref, k_ref, v_ref, op_ref, sc_ref, bif_ref, nw_ref,
                              c_out_ref, n_out_ref, m_out_ref, o_ref, *, heads, dk, nb):
    scale = dk ** -0.5
    for i in range(nb):
        gi = sc_ref[i, :, 0:1] + bif_ref[:, 0:1]
        gf = sc_ref[i, :, 1:2] + bif_ref[:, 1:2]
        m0 = sc_ref[i, :, 2:3]
        gi = GATE_CAP * jnp.tanh(gi / GATE_CAP)
        logf = _log_sigmoid(GATE_CAP * jnp.tanh(gf / GATE_CAP))
        m_new = jnp.maximum(logf + m0, gi)
        f_s = jnp.exp(logf + m0 - m_new)
        i_s = jnp.exp(gi - m_new)
        m_out_ref[i] = m_new
        n1 = f_s * n0_ref[i] + i_s * k_ref[i]
        n_out_ref[i] = n1
        den = jnp.sum(q_ref[i] * scale * n1, axis=-1, keepdims=True)
        floor = jnp.exp(-m_new)
        hh = range(heads)
        kc = [cols_ref[i, :, h:h + 1] for h in hh]
        qc = [cols_ref[i, :, heads + h:heads + h + 1] * scale for h in hh]
        c1 = [f_s[h:h + 1, :] * c0_ref[i, h] + i_s[h:h + 1, :] * (kc[h] * v_ref[i, h:h + 1, :]) for h in hh]
        num = [jnp.sum(qc[h] * c1[h], axis=0, keepdims=True) for h in hh]
        h_t = [num[h] / jnp.maximum(jnp.abs(den[h:h + 1, :]), floor[h:h + 1, :]) for h in hh]
        ms = [jnp.mean(h_t[h] * h_t[h], axis=-1, keepdims=True) for h in hh]
        for h in hh:
            c_out_ref[i, h] = c1[h]
            h_n = h_t[h] * lax.rsqrt(ms[h] + RMS_EPS) * nw_ref[h:h + 1, :]
            o_ref[i, h:h + 1, :] = _sigmoid(op_ref[i, h:h + 1, :]) * h_n


def _mlstm_sample_step(c0, n0, cols, q, k, v, o_pre, sc, bif2, nw2):
    n, heads, dk, dv = c0.shape
    nb = SEQ_PER_STEP
    assert n % nb == 0
    full = _const_block

    def blk(a):
        return _seq_block(a, nb)

    m_shape = (n, heads, 1)
    return pl.pallas_call(
        functools.partial(_mlstm_sample_step_kernel, heads=heads, dk=dk, nb=nb),
        grid=(n // nb,),
        in_specs=[blk(c0), blk(n0), blk(cols), blk(q), blk(k), blk(v), blk(o_pre), blk(sc), full(bif2), full(nw2)],
        out_specs=[blk(c0), blk(n0), pl.BlockSpec((nb, heads, 1), lambda b: (b, 0, 0)), blk(v)],
        out_shape=[jax.ShapeDtypeStruct(c0.shape, F32), jax.ShapeDtypeStruct(n0.shape, F32),
                   jax.ShapeDtypeStruct(m_shape, F32), jax.ShapeDtypeStruct(v.shape, F32)],
        compiler_params=_cparams(("parallel",)),
        name="mlstm_sample_step",
    )(c0, n0, cols, q, k, v, o_pre, sc, bif2, nw2)


def _mlstm_sample_layer(x, g, p, c0, n0, m0, *, heads, dk, dv):
    n = x.shape[0]
    qk_w, val = heads * dk, heads * dv
    pm, pif = _in_proj(x, g, p["w_main"], p["w_if"])
    q = pm[:, :qk_w].reshape(n, heads, dk)
    k = pm[:, qk_w:2 * qk_w].reshape(n, heads, dk)
    v = pm[:, 2 * qk_w:2 * qk_w + val].reshape(n, heads, dv)
    o_pre = pm[:, 2 * qk_w + val:].reshape(n, heads, dv)
    cols = jnp.concatenate([jnp.transpose(k, (0, 2, 1)), jnp.transpose(q, (0, 2, 1))], axis=-1)
    sc = jnp.stack([pif[:, :heads], pif[:, heads:2 * heads], m0], axis=-1)
    bif2 = jnp.stack([p["bif"][0, :heads], p["bif"][0, heads:2 * heads]], axis=-1)
    c1, n1, m1, o = _mlstm_sample_step(c0, n0, cols, q, k, v, o_pre, sc, bif2, p["nw"].reshape(heads, dv))
    y = _out_proj(o.reshape(n, val), p["w_out"], x)
    return y, c1, n1, m1[:, :, 0]


def _rwkv_sample_step_kernel(s0_ref, r_ref, k_ref, lw_ref, kk_ref, a_ref, vc_ref, gc_ref, hpc_ref, hpr_ref,
                             s_out_ref, o_ref, *, heads, nb):
    hh = range(heads)
    for i in range(nb):
        r = r_ref[i]
        k = k_ref[i]
        kk = kk_ref[i]
        kk = kk * lax.rsqrt(jnp.maximum(jnp.sum(kk * kk, axis=-1, keepdims=True), 1e-24))
        bv = kk * a_ref[i]
        w = jnp.exp(lw_ref[i])
        bonus = jnp.sum(r * k * hpr_ref[...], axis=-1, keepdims=True)
        s0 = [s0_ref[i, h] for h in hh]
        vc = [vc_ref[i, :, h:h + 1] for h in hh]
        sa = [jnp.sum(s0[h] * (-kk[h:h + 1, :]), axis=-1, keepdims=True) for h in hh]
        s1 = [s0[h] * w[h:h + 1, :] + sa[h] * bv[h:h + 1, :] + vc[h] * k[h:h + 1, :] for h in hh]
        y = [jnp.sum(s1[h] * r[h:h + 1, :], axis=-1, keepdims=True) for h in hh]
        for h in hh:
            s_out_ref[i, h] = s1[h]
            yc = y[h] - jnp.mean(y[h], axis=0, keepdims=True)
            var = jnp.mean(yc * yc, axis=0, keepdims=True)
            yn = yc * lax.rsqrt(var + RW_GN_EPS) * hpc_ref[0, :, h:h + 1] + hpc_ref[1, :, h:h + 1]
            o_ref[i, :, h:h + 1] = (yn + bonus[h:h + 1, :] * vc[h]) * gc_ref[i, :, h:h + 1]


def _rwkv_sample_step(s0, r, k, lw, kk, a, v_c, g_c, hp_c, rk):
    n, heads, hd, _ = s0.shape
    nb = SEQ_PER_STEP
    assert n % nb == 0
    full = _const_block

    def blk(a_):
        return _seq_block(a_, nb)

    return pl.pallas_call(
        functools.partial(_rwkv_sample_step_kernel, heads=heads, nb=nb),
        grid=(n // nb,),
        in_specs=[blk(s0), blk(r), blk(k), blk(lw), blk(kk), blk(a), blk(v_c), blk(g_c), full(hp_c), full(rk)],
        out_specs=[blk(s0), blk(v_c)],
        out_shape=[jax.ShapeDtypeStruct(s0.shape, F32), jax.ShapeDtypeStruct(v_c.shape, F32)],
        compiler_params=_cparams(("parallel",)),
        name="rwkv_sample_step",
    )(s0, r, k, lw, kk, a, v_c, g_c, hp_c, rk)


def _rwkv_sample_layer(x, g_norm, p, shift0, s0, *, heads, hd):
    n, d = x.shape
    (r, k, v, lw, kk, a, g), hn = _rwkv_proj(x, g_norm, shift0, p, batch=n, seq=1, tm=n)

    def rows(z):
        return z.reshape(n, heads, hd)

    def cols(z):
        return jnp.transpose(z.reshape(n, heads, hd), (0, 2, 1))

    hp = p["hp"]
    hp_c = jnp.stack([hp[1].reshape(heads, hd).T, hp[2].reshape(heads, hd).T])
    s1, o_c = _rwkv_sample_step(s0, rows(r), rows(k), rows(lw), rows(kk), rows(a), cols(v), cols(g),
                                hp_c, hp[0].reshape(heads, hd))
    o = jnp.transpose(o_c, (0, 2, 1)).reshape(n, d)
    y = _out_proj(o, p["w_o"], x)
    return y, hn, s1


def _pad_cols(a, n):
    return jnp.pad(a, ((0, 0), (0, n - a.shape[1])))


def _gdn_prep(w_in, conv_w, a_log, dt_bias, norm_w, w_out, *, heads, dk, dv):
    key, val = heads * dk, heads * dv
    ch = 2 * key + val
    main = ch + val
    gp = jnp.zeros((8, 128), F32)
    gp = gp.at[0, heads:2 * heads].set(a_log).at[1, heads:2 * heads].set(dt_bias)
    return dict(w_main=w_in[:, :main].astype(BF16),
                w_ba=_pad_cols(w_in[:, main:], 128).astype(BF16),
                cw_t=jnp.pad(conv_w.T, ((0, 8 - CONV_W), (0, 0))),
                gp=gp, nw=norm_w[None, :], w_out=w_out.astype(BF16))


def _gdn_prompt_layer(x, g, p, conv0, s0, *, batch, seq, heads, dk, dv, norm=True, residual=True):
    pm, pba = _in_proj(x, g, p["w_main"], p["w_ba"], norm=norm)
    conv0 = jnp.pad(conv0, ((0, 0), (8 - (CONV_W - 1), 0), (0, 0)))
    o, conv, s = _gdn_prompt(pm, pba, conv0, s0, p["cw_t"], p["gp"], p["nw"],
                             batch=batch, seq=seq, heads=heads, dk=dk, dv=dv, tb=TB_MIX)
    y = _out_proj(o, p["w_out"], x if residual else None)
    return y, conv[:, 8 - (CONV_W - 1):, :], s


def _trunk(x, states, w, *, batch, seq):
    conv_in, gs_in, c_in, n_in, m_in, shift_in, rs_in = states
    depth = w["norm_mix"].shape[0]
    gh, gdk, gdv = gs_in.shape[2:]
    mh, mdk, mdv = c_in.shape[2:]
    rh, rhd = rs_in.shape[2:4]
    prompt = seq > 1
    outs = [[] for _ in range(7)]
    gs_acc = None if prompt else jnp.zeros_like(gs_in)
    for i in range(depth):
        j = i // 3
        g = w["norm_mix"][i][None, :]
        if i % 3 == 0:
            p = w["gdn"][j]
            if prompt:
                x, cb, s = _gdn_prompt_layer(x, g, p, conv_in[j], gs_in[j], batch=batch, seq=seq,
                                             heads=gh, dk=gdk, dv=gdv)
                outs[1].append(s)
            else:
                x, cb, gs_acc = _gdn_sample_layer(x, g, p, conv_in[j], gs_in, j, gs_acc, heads=gh, dk=gdk, dv=gdv)
            outs[0].append(cb)
        elif i % 3 == 1:
            p = w["ml"][j]
            if prompt:
                x, c, n, m = _mlstm_prompt_layer(x, g, p, c_in[j], n_in[j], m_in[j], batch=batch, seq=seq,
                                                 heads=mh, dk=mdk, dv=mdv)
            else:
                x, c, n, m = _mlstm_sample_layer(x, g, p, c_in[j], n_in[j], m_in[j], heads=mh, dk=mdk, dv=mdv)
            outs[2].append(c)
            outs[3].append(n)
            outs[4].append(m)
        else:
            p = w["rw"][j]
            if prompt:
                x, sh, s = _rwkv_prompt_layer(x, g, p, shift_in[j], rs_in[j], batch=batch, seq=seq,
                                              heads=rh, hd=rhd)
            else:
                x, sh, s = _rwkv_sample_layer(x, g, p, shift_in[j], rs_in[j], heads=rh, hd=rhd)
            outs[5].append(sh)
            outs[6].append(s)
        x = _ffn(x, w["norm_ffn"][i][None, :], w["ffn_w1"][i], w["ffn_w2"][i], tm=TM_FFN, tf=TF_FFN)
    y = _rmsnorm(x, w["norm_final"][None, :], tm=TM_FFN)
    new = [jnp.stack(z, axis=0) if z else None for z in outs]
    if not prompt:
        new[1] = gs_acc
    return y, tuple(new)


def kernel(x_prompt, x_sample, state_gdn_conv, state_gdn_S, state_mlstm_C, state_mlstm_n, state_mlstm_m, state_rwkv_shift, state_rwkv_S, norm_mix, norm_ffn, norm_final, gdn_w_in, gdn_conv_w, gdn_a_log, gdn_dt_bias, gdn_norm_w, gdn_w_out, ml_w_in, ml_b_if, ml_norm_w, ml_w_out, rw_mu, rw_w_rkv, rw_w_o, rw_w0, rw_w1, rw_w2, rw_a0, rw_a1, rw_a2, rw_g1, rw_g2, rw_k_k, rw_k_a, rw_r_k, rw_lnx_w, rw_lnx_b, ffn_w1, ffn_w2):
    gh, gdk, gdv = state_gdn_S.shape[2:]
    mh, mdk, mdv = state_mlstm_C.shape[2:]
    w = dict(
        norm_mix=norm_mix, norm_ffn=norm_ffn, norm_final=norm_final,
        ffn_w1=[ffn_w1[i].astype(BF16) for i in range(ffn_w1.shape[0])],
        ffn_w2=[ffn_w2[i].astype(BF16) for i in range(ffn_w2.shape[0])],
        gdn=[_gdn_prep(gdn_w_in[j], gdn_conv_w[j], gdn_a_log[j], gdn_dt_bias[j], gdn_norm_w[j], gdn_w_out[j],
                       heads=gh, dk=gdk, dv=gdv) for j in range(gdn_w_in.shape[0])],
        ml=[_mlstm_prep(ml_w_in[j], ml_b_if[j], ml_norm_w[j], ml_w_out[j], heads=mh, dk=mdk, dv=mdv)
            for j in range(ml_w_in.shape[0])],
        rw=[_rwkv_prep(rw_mu[j], rw_w_rkv[j], rw_w_o[j], rw_w0[j], rw_w1[j], rw_w2[j], rw_a0[j], rw_a1[j],
                       rw_a2[j], rw_g1[j], rw_g2[j], rw_k_k[j], rw_k_a[j], rw_r_k[j], rw_lnx_w[j], rw_lnx_b[j])
            for j in range(rw_mu.shape[0])])
    sample_states = (state_gdn_conv, state_gdn_S, state_mlstm_C, state_mlstm_n, state_mlstm_m,
                     state_rwkv_shift, state_rwkv_S)
    bp, tp, d = x_prompt.shape
    bs, ts, _ = x_sample.shape
    assert ts == 1
    prompt_states = tuple(jnp.zeros((s.shape[0], bp) + s.shape[2:], s.dtype) for s in sample_states)
    y_p, new_p = _trunk(x_prompt.reshape(bp * tp, d), prompt_states, w, batch=bp, seq=tp)
    y_s, new_s = _trunk(x_sample.reshape(bs * ts, d), sample_states, w, batch=bs, seq=ts)
    out = [y_p.reshape(bp, tp, d), y_s.reshape(bs, ts, d)]
    for a, b in zip(new_p, new_s):
        out += [a, b]
    return tuple(out)
```

```python
import functools
import math

import jax
import jax.numpy as jnp
from jax import lax
from jax.experimental import pallas as pl
from jax.experimental.pallas import tpu as pltpu

F32 = jnp.float32
BF16 = jnp.bfloat16

RMS_EPS = 1e-6
NEG_BIG = -1e30
GATE_CAP = 15.0
RW_GN_EPS = 64e-5
CONV_W = 4
CHUNK = 64
RW_GROUP = 8
V7X_VMEM_LIMIT = 56 * 1024 * 1024
HI = lax.Precision.HIGHEST


def _cparams(sem):
    return pltpu.CompilerParams(dimension_semantics=sem, vmem_limit_bytes=V7X_VMEM_LIMIT)


def _dot(a, b):
    return jnp.dot(a.astype(BF16), b.astype(BF16), preferred_element_type=F32)


def _dot_nt(a, b):
    return lax.dot_general(a.astype(BF16), b.astype(BF16), (((1,), (1,)), ((), ())),
                           preferred_element_type=F32)


def _dot_tn(a, b):
    return lax.dot_general(a.astype(BF16), b.astype(BF16), (((0,), (0,)), ((), ())),
                           preferred_element_type=F32)


def _dot_hi(a, b):
    return jnp.dot(a, b, preferred_element_type=F32, precision=HI)


def _sigmoid(x):
    return 1.0 / (1.0 + jnp.exp(-x))


def _silu(x):
    return x * _sigmoid(x)


def _softplus(x):
    return jnp.maximum(x, 0.0) + jnp.log(1.0 + jnp.exp(-jnp.abs(x)))


def _log_sigmoid(x):
    return -_softplus(-x)


def _tri_masks(l):
    r = lax.broadcasted_iota(jnp.int32, (l, l), 0)
    c = lax.broadcasted_iota(jnp.int32, (l, l), 1)
    return r >= c, r > c


INV_BASE = 16


def _unit_lower_inverse(mats, l):
    n = mats[0].shape[0]
    r = lax.broadcasted_iota(jnp.int32, (n, n), 0)
    c = lax.broadcasted_iota(jnp.int32, (n, n), 1)
    eye = (r == c).astype(F32)
    size = min(INV_BASE, l)
    shift = size.bit_length() - 1
    diag = (r >> shift) == (c >> shift)
    merges = []
    s = size
    while s < l:
        sh = s.bit_length() - 1
        off = ((r >> (sh + 1)) == (c >> (sh + 1))) & ((r >> sh) > (c >> sh))
        merges.append([jnp.where(off, a, 0.0).astype(BF16) for a in mats])
        s *= 2
    t = [eye - jnp.where(diag, a, 0.0) for a in mats]
    tb = [ti.astype(BF16) for ti in t]
    ab = [jnp.where(diag, a, 0.0).astype(BF16) for a in mats]
    p = [_dot(a, a).astype(BF16) for a in ab]
    k = 2
    while k < size:
        if 2 * k < size:
            both = [_dot(pi, jnp.concatenate([ti, pi], axis=1)) for ti, pi in zip(tb, p)]
            t = [ti + bi[:, :n] for ti, bi in zip(t, both)]
            p = [bi[:, n:].astype(BF16) for bi in both]
        else:
            t = [ti + _dot(pi, ti_b) for ti, ti_b, pi in zip(t, tb, p)]
        tb = [ti.astype(BF16) for ti in t]
        k *= 2
    for a_off in merges:
        x = [_dot(a, ti) for a, ti in zip(a_off, tb)]
        y = [_dot(ti, xi) for ti, xi in zip(tb, x)]
        t = [ti - yi for ti, yi in zip(t, y)]
        tb = [ti.astype(BF16) for ti in t]
    return tb


def _nmm_kernel(*refs, norm, residual, aux):
    x_ref, g_ref, w_ref = refs[:3]
    pos = 3
    res_ref = aux_w_ref = aux_o_ref = None
    if residual:
        res_ref = refs[pos]
        pos += 1
    if aux:
        aux_w_ref = refs[pos]
        pos += 1
    o_ref = refs[pos]
    if aux:
        aux_o_ref = refs[pos + 1]
    xn_ref = refs[-1]

    @pl.when(pl.program_id(1) == 0)
    def _():
        x = x_ref[...]
        if norm:
            x = x * lax.rsqrt(jnp.mean(x * x, axis=-1, keepdims=True) + RMS_EPS) * g_ref[...]
        xn_ref[...] = x.astype(BF16)
        if aux:
            aux_o_ref[...] = jnp.dot(xn_ref[...], aux_w_ref[...], preferred_element_type=F32)

    y = jnp.dot(xn_ref[...], w_ref[...], preferred_element_type=F32)
    if residual:
        y = res_ref[...] + y
    o_ref[...] = y


def _nmm(x, g, w, res=None, w_aux=None, *, norm, tm, tn):
    m, k = x.shape
    n = w.shape[1]
    tm, tn = min(tm, m), min(tn, n)
    assert m % tm == 0 and n % tn == 0
    in_specs = [pl.BlockSpec((tm, k), lambda i, j: (i, 0)),
                pl.BlockSpec((1, k), lambda i, j: (0, 0)),
                pl.BlockSpec((k, tn), lambda i, j: (0, j))]
    args = [x, g, w]
    out_specs = [pl.BlockSpec((tm, tn), lambda i, j: (i, j))]
    out_shape = [jax.ShapeDtypeStruct((m, n), F32)]
    if res is not None:
        in_specs.append(pl.BlockSpec((tm, tn), lambda i, j: (i, j)))
        args.append(res)
    if w_aux is not None:
        na = w_aux.shape[1]
        in_specs.append(pl.BlockSpec((k, na), lambda i, j: (0, 0)))
        args.append(w_aux)
        out_specs.append(pl.BlockSpec((tm, na), lambda i, j: (i, 0)))
        out_shape.append(jax.ShapeDtypeStruct((m, na), F32))
    out = pl.pallas_call(
        functools.partial(_nmm_kernel, norm=norm, residual=res is not None, aux=w_aux is not None),
        grid=(m // tm, n // tn),
        in_specs=in_specs,
        out_specs=out_specs,
        out_shape=out_shape,
        scratch_shapes=[pltpu.VMEM((tm, k), BF16)],
        compiler_params=_cparams(("parallel", "arbitrary")),
        name="nmm",
    )(*args)
    return out if w_aux is not None else out[0]


TM_IN_PROJ = 512
TM_OUT_PROJ = 1024
TM_FFN, TF_FFN = 1024, 1024
TB_MIX = 256
TM_RWKV_PROJ = 256


def _in_proj(x, g, w_main, w_aux, *, norm=True):
    return _nmm(x, g, w_main, None, w_aux, norm=norm, tm=TM_IN_PROJ, tn=w_main.shape[1])


def _out_proj(o, w_out, res):
    ones = jnp.ones((1, o.shape[1]), F32)
    return _nmm(o, ones, w_out, res, norm=False, tm=TM_OUT_PROJ, tn=w_out.shape[1])


def _ffn_kernel(x_ref, g_ref, w1_ref, w2_ref, go_ref, o_ref, xn_ref, acc_ref, *, out_norm):
    f = pl.program_id(1)

    @pl.when(f == 0)
    def _():
        xn_ref[...] = _norm_rows(x_ref[...], g_ref[...]).astype(BF16)
        acc_ref[...] = jnp.zeros_like(acc_ref)

    h = jnp.dot(xn_ref[...], w1_ref[...], preferred_element_type=F32)
    a = jnp.square(jnp.maximum(h, 0.0)).astype(BF16)
    acc_ref[...] += jnp.dot(a, w2_ref[...], preferred_element_type=F32)

    @pl.when(f == pl.num_programs(1) - 1)
    def _():
        y = x_ref[...] + acc_ref[...]
        o_ref[...] = _norm_rows(y, go_ref[...]) if out_norm else y


def _ffn(x, g, w1, w2, g_out=None, *, tm, tf):
    m, d = x.shape
    dff = w1.shape[1]
    tm, tf = min(tm, m), min(tf, dff)
    assert m % tm == 0 and dff % tf == 0
    vec = pl.BlockSpec((1, d), lambda i, j: (0, 0))
    return pl.pallas_call(
        functools.partial(_ffn_kernel, out_norm=g_out is not None),
        grid=(m // tm, dff // tf),
        in_specs=[pl.BlockSpec((tm, d), lambda i, j: (i, 0)), vec,
                  pl.BlockSpec((d, tf), lambda i, j: (0, j)),
                  pl.BlockSpec((tf, d), lambda i, j: (j, 0)), vec],
        out_specs=pl.BlockSpec((tm, d), lambda i, j: (i, 0)),
        out_shape=jax.ShapeDtypeStruct((m, d), F32),
        scratch_shapes=[pltpu.VMEM((tm, d), BF16), pltpu.VMEM((tm, d), F32)],
        compiler_params=_cparams(("parallel", "arbitrary")),
        name="ffn",
    )(x, g, w1, w2, g if g_out is None else g_out)


def _gdn_prompt_kernel(pm_ref, pba_ref, conv0_ref, s0_ref, cw_ref, gp_ref, nw_ref,
                       o_ref, conv_out_ref, s_out_ref,
                       full_ref, qkv_ref, s_ref, u_ref, w_ref, qd_ref, kd_ref, qk_ref, gl_ref,
                       *, tb, heads, dk, dv):
    t = pl.program_id(1)
    key = heads * dk
    ch = 2 * key + heads * dv
    l = CHUNK

    @pl.when(t == 0)
    def _():
        full_ref[0:8, :] = conv0_ref[0]
        s_ref[...] = s0_ref[0]

    full_ref[8:8 + tb, :] = pm_ref[:, 0:ch]
    for c in range(ch // 128):
        cs = slice(c * 128, (c + 1) * 128)
        ext = full_ref[:, cs]
        y = ext[8:] * cw_ref[CONV_W - 1:CONV_W, cs]
        for s in range(1, CONV_W):
            y = y + pltpu.roll(ext, s, 0)[8:] * cw_ref[CONV_W - 1 - s:CONV_W - s, cs]
        y = _silu(y)
        if c * 128 < key:
            y = y * lax.rsqrt(jnp.sum(y * y, axis=-1, keepdims=True) + 1e-6) * (dk ** -0.5)
        elif c * 128 < 2 * key:
            y = y * lax.rsqrt(jnp.sum(y * y, axis=-1, keepdims=True) + 1e-6)
        qkv_ref[:, cs] = y
    full_ref[0:8, :] = full_ref[tb:tb + 8, :]

    a_log = gp_ref[0:1, :]
    dt_bias = gp_ref[1:2, :]
    hh = range(heads)
    l2 = 2 * l
    r2 = lax.broadcasted_iota(jnp.int32, (l2, l2), 0)
    c2 = lax.broadcasted_iota(jnp.int32, (l2, l2), 1)
    same = (r2 >= l) == (c2 >= l)
    incl = same & (r2 >= c2)
    strict = same & (r2 > c2)
    tril = incl.astype(F32)
    first = lax.broadcasted_iota(jnp.int32, (l2, 128), 0) < l

    pairs = range(tb // l2)
    rows = [slice(pi * l2, (pi + 1) * l2) for pi in pairs]
    beta_all, gc, gc_t, g_end = [], [], [], []
    for pi in pairs:
        ba = pba_ref[rows[pi], :]
        beta_all.append(_sigmoid(ba))
        g_all = -jnp.exp(a_log) * _softplus(ba + dt_bias)
        gc.append(_dot_hi(tril, g_all))
        gc_t.append(gc[pi].T)
        g_end.append(jnp.where(first, gc[pi][l - 1:l, :], gc[pi][l2 - 1:l2, :]))
        gl_ref[2 * pi:2 * pi + 1, :] = jnp.exp(gc[pi][l - 1:l, :])
        gl_ref[2 * pi + 1:2 * pi + 2, :] = jnp.exp(gc[pi][l2 - 1:l2, :])
    cc = [(pi, h) for pi in pairs for h in hh]
    b_col = [beta_all[pi][:, h:h + 1] for pi, h in cc]
    gi = [gc[pi][:, heads + h:heads + h + 1] for pi, h in cc]
    q = [qkv_ref[rows[pi], h * dk:(h + 1) * dk] for pi, h in cc]
    k = [qkv_ref[rows[pi], key + h * dk:key + (h + 1) * dk] for pi, h in cc]
    v = [qkv_ref[rows[pi], 2 * key + h * dv:2 * key + (h + 1) * dv] for pi, h in cc]
    nc = range(len(cc))
    dmat = [jnp.where(incl, jnp.exp(jnp.where(incl, gi[i] - gc_t[pi][heads + h:heads + h + 1, :], 0.0)), 0.0)
            for i, (pi, h) in enumerate(cc)]
    kb = [k[i] * b_col[i] for i in nc]
    kk = [_dot_nt(kb[i], k[i]) for i in nc]
    qk = [_dot_nt(q[i], k[i]) for i in nc]
    t_inv = _unit_lower_inverse([jnp.where(strict, kk[i] * dmat[i], 0.0) for i in nc], l)
    egi = [jnp.exp(gi[i]) for i in nc]
    sol = [_dot(t_inv[i], jnp.concatenate([v[i] * b_col[i], kb[i] * egi[i]], axis=-1)) for i in nc]
    for i, (pi, h) in enumerate(cc):
        hs = slice(h * dk, (h + 1) * dk)
        u_ref[rows[pi], h * dv:(h + 1) * dv] = sol[i][:, :dv]
        w_ref[rows[pi], hs] = sol[i][:, dv:].astype(BF16)
        qd_ref[rows[pi], hs] = (q[i] * egi[i]).astype(BF16)
        kd_ref[rows[pi], hs] = (k[i] * jnp.exp(g_end[pi][:, heads + h:heads + h + 1] - gi[i])).astype(BF16)
        qkm = jnp.where(incl, qk[i] * dmat[i], 0.0).astype(BF16)
        qk_ref[h, pi * l2:pi * l2 + l, :] = qkm[:l, :l]
        qk_ref[h, pi * l2 + l:(pi + 1) * l2, :] = qkm[l:, l:]

    for ci in range(tb // l):
        rows = slice(ci * l, (ci + 1) * l)
        s = [s_ref[h] for h in hh]
        wq = [_dot(jnp.concatenate([w_ref[rows, h * dk:(h + 1) * dk], qd_ref[rows, h * dk:(h + 1) * dk]], axis=0),
                   s[h]) for h in hh]
        v_new = [u_ref[rows, h * dv:(h + 1) * dv] - wq[h][:l] for h in hh]
        o2 = [_dot(qk_ref[h, rows, :], v_new[h]) for h in hh]
        ds = [_dot_tn(kd_ref[rows, h * dk:(h + 1) * dk], v_new[h]) for h in hh]
        for h in hh:
            s_ref[h] = s[h] * gl_ref[ci:ci + 1, heads + h:heads + h + 1] + ds[h]
            o = wq[h][l:] + o2[h]
            z = pm_ref[rows, ch + h * dv:ch + (h + 1) * dv]
            o = o * lax.rsqrt(jnp.mean(o * o, axis=-1, keepdims=True) + RMS_EPS) * nw_ref[...]
            o_ref[rows, h * dv:(h + 1) * dv] = o * _silu(z)

    @pl.when(t == pl.num_programs(1) - 1)
    def _():
        conv_out_ref[0] = full_ref[0:8, :]
        s_out_ref[0] = s_ref[...]


def _gdn_prompt(pm, pba, conv0, s0, cw_t, gp, nw, *, batch, seq, heads, dk, dv, tb):
    key, val = heads * dk, heads * dv
    ch = 2 * key + val
    tb = min(tb, seq)
    assert seq % tb == 0 and tb % CHUNK == 0
    nt = seq // tb
    return pl.pallas_call(
        functools.partial(_gdn_prompt_kernel, tb=tb, heads=heads, dk=dk, dv=dv),
        grid=(batch, nt),
        in_specs=[pl.BlockSpec((tb, ch + val), lambda b, t: (b * nt + t, 0)),
                  pl.BlockSpec((tb, 128), lambda b, t: (b * nt + t, 0)),
                  pl.BlockSpec((1, 8, ch), lambda b, t: (b, 0, 0)),
                  pl.BlockSpec((1, heads, dk, dv), lambda b, t: (b, 0, 0, 0)),
                  pl.BlockSpec((8, ch), lambda b, t: (0, 0)),
                  pl.BlockSpec((8, 128), lambda b, t: (0, 0)),
                  pl.BlockSpec((1, dv), lambda b, t: (0, 0))],
        out_specs=[pl.BlockSpec((tb, val), lambda b, t: (b * nt + t, 0)),
                   pl.BlockSpec((1, 8, ch), lambda b, t: (b, 0, 0)),
                   pl.BlockSpec((1, heads, dk, dv), lambda b, t: (b, 0, 0, 0))],
        out_shape=[jax.ShapeDtypeStruct((batch * seq, val), F32),
                   jax.ShapeDtypeStruct((batch, 8, ch), F32),
                   jax.ShapeDtypeStruct((batch, heads, dk, dv), F32)],
        scratch_shapes=[pltpu.VMEM((tb + 8, ch), F32), pltpu.VMEM((tb, ch), F32),
                        pltpu.VMEM((heads, dk, dv), F32),
                        pltpu.VMEM((tb, val), F32), pltpu.VMEM((tb, key), BF16),
                        pltpu.VMEM((tb, key), BF16), pltpu.VMEM((tb, key), BF16),
                        pltpu.VMEM((heads, tb, CHUNK), BF16), pltpu.VMEM((max(8, tb // CHUNK), 128), F32)],
        compiler_params=_cparams(("parallel", "arbitrary")),
        name="gdn_prompt",
    )(pm, pba, conv0, s0, cw_t, gp, nw)


def _mlstm_prompt_kernel(pm_ref, pif_ref, bif_ref, c0_ref, n0_ref, m0_ref, nw_ref,
                         o_ref, c_out_ref, n_out_ref, m_out_ref,
                         c_ref, n_ref, m_ref, bc_ref, ni_ref, col_ref, kv_ref, kc_ref, sc_ref,
                         cin_ref, nin_ref, min_ref, *, tb, heads, dk, dv):
    t = pl.program_id(1)
    l = CHUNK
    qk_w = heads * dk
    v_off = 2 * qk_w
    o_off = v_off + heads * dv

    @pl.when(t == 0)
    def _():
        c_ref[...] = c0_ref[0]
        n_ref[...] = n0_ref[0]
        m_ref[...] = m0_ref[0]

    incl, _ = _tri_masks(l)
    tril = incl.astype(F32)
    hh = range(heads)
    nchunk = tb // l
    scale = dk ** -0.5

    crow = [slice(ci * l, (ci + 1) * l) for ci in range(nchunk)]
    gates, bcum, bcum_t, gates_t = [], [], [], []
    for ci in range(nchunk):
        g = pif_ref[crow[ci], :] + bif_ref[...]
        g = GATE_CAP * jnp.tanh(g / GATE_CAP)
        gates.append(g)
        bcum.append(_dot_hi(tril, _log_sigmoid(g)))
        bcum_t.append(bcum[ci].T)
        gates_t.append(g.T)
        bc_ref[crow[ci], :] = bcum[ci]
    cc = [(ci, h) for ci in range(nchunk) for h in hh]
    nc = range(len(cc))
    bi = [bcum[ci][:, heads + h:heads + h + 1] for ci, h in cc]
    b_last = [bcum[ci][l - 1:l, heads + h:heads + h + 1] for ci, h in cc]
    k = [pm_ref[crow[ci], qk_w + h * dk:qk_w + (h + 1) * dk] for ci, h in cc]
    v = [pm_ref[crow[ci], v_off + h * dv:v_off + (h + 1) * dv].astype(BF16) for ci, h in cc]
    qk = [_dot_nt(pm_ref[crow[ci], h * dk:(h + 1) * dk] * scale, k[i]) for i, (ci, h) in enumerate(cc)]
    dlog = [jnp.where(incl, bi[i] - bcum_t[ci][heads + h:heads + h + 1, :] + gates_t[ci][h:h + 1, :], NEG_BIG)
            for i, (ci, h) in enumerate(cc)]
    m_intra = [jnp.max(dlog[i], axis=-1, keepdims=True) for i in nc]
    p = [jnp.where(incl, jnp.exp(dlog[i] - m_intra[i]), 0.0) * qk[i] for i in nc]
    den_intra = [jnp.sum(p[i], axis=-1, keepdims=True) for i in nc]
    num_intra = [_dot(p[i], v[i]) for i in nc]
    a_log = [b_last[i] - bi[i] + gates[ci][:, h:h + 1] for i, (ci, h) in enumerate(cc)]
    m_chunk = [jnp.max(a_log[i], axis=0, keepdims=True) for i in nc]
    kw = [k[i] * jnp.exp(a_log[i] - m_chunk[i]) for i in nc]
    kv_chunk = [_dot_tn(kw[i], v[i]) for i in nc]
    for i, (ci, h) in enumerate(cc):
        ni_ref[crow[ci], h * dv:(h + 1) * dv] = num_intra[i]
        col_ref[crow[ci], h:h + 1] = m_intra[i]
        col_ref[crow[ci], heads + h:heads + h + 1] = den_intra[i]
        kv_ref[i] = kv_chunk[i]
        kc_ref[i:i + 1, :] = jnp.sum(kw[i], axis=0, keepdims=True)
        sc_ref[i:i + 1, 0:1] = m_chunk[i]
        sc_ref[i:i + 1, 1:2] = b_last[i]

    for ci in range(nchunk):
        for h in hh:
            i = ci * heads + h
            c_mat = c_ref[h]
            n_vec = n_ref[h:h + 1, :]
            m_prev = m_ref[h:h + 1, :]
            cin_ref[i] = c_mat
            nin_ref[i:i + 1, :] = n_vec
            min_ref[i:i + 1, :] = m_prev
            m_chunk = sc_ref[i:i + 1, 0:1]
            b_last = sc_ref[i:i + 1, 1:2]
            m_new = jnp.maximum(b_last + m_prev, m_chunk)
            f_s = jnp.exp(b_last + m_prev - m_new)
            i_s = jnp.exp(m_chunk - m_new)
            c_ref[h] = f_s[:, 0:1] * c_mat + i_s[:, 0:1] * kv_ref[i]
            n_ref[h:h + 1, :] = f_s * n_vec + i_s * kc_ref[i:i + 1, :]
            m_ref[h:h + 1, :] = m_new

    q = [pm_ref[crow[ci], h * dk:(h + 1) * dk] * scale for ci, h in cc]
    qc = [_dot(q[i], cin_ref[i]) for i in nc]
    qn = [jnp.sum(q[i] * nin_ref[i:i + 1, :], axis=-1, keepdims=True) for i in nc]
    m_prev = [min_ref[i:i + 1, 0:1] for i in nc]
    bi = [bc_ref[crow[ci], heads + h:heads + h + 1] for ci, h in cc]
    m_in = [col_ref[crow[ci], h:h + 1] for ci, h in cc]
    m_t = [jnp.maximum(bi[i] + m_prev[i], m_in[i]) for i in nc]
    s_inter = [jnp.exp(bi[i] + m_prev[i] - m_t[i]) for i in nc]
    s_intra = [jnp.exp(m_in[i] - m_t[i]) for i in nc]
    den = [s_inter[i] * qn[i] + s_intra[i] * col_ref[crow[ci], heads + h:heads + h + 1]
           for i, (ci, h) in enumerate(cc)]
    h_t = [(s_inter[i] * qc[i] + s_intra[i] * ni_ref[crow[ci], h * dv:(h + 1) * dv])
           / jnp.maximum(jnp.abs(den[i]), jnp.exp(-m_t[i])) for i, (ci, h) in enumerate(cc)]
    ms = [jnp.mean(h_t[i] * h_t[i], axis=-1, keepdims=True) for i in nc]
    for i, (ci, h) in enumerate(cc):
        h_n = h_t[i] * lax.rsqrt(ms[i] + RMS_EPS) * nw_ref[:, h * dv:(h + 1) * dv]
        o_pre = pm_ref[crow[ci], o_off + h * dv:o_off + (h + 1) * dv]
        o_ref[crow[ci], h * dv:(h + 1) * dv] = _sigmoid(o_pre) * h_n

    @pl.when(t == pl.num_programs(1) - 1)
    def _():
        c_out_ref[0] = c_ref[...]
        n_out_ref[0] = n_ref[...]
        m_out_ref[0] = m_ref[...]


def _mlstm_prompt(pm, pif, bif, c0, n0, m0, nw, *, batch, seq, heads, dk, dv, tb):
    width = pm.shape[1]
    val = heads * dv
    tb = min(tb, seq)
    assert seq % tb == 0 and tb % CHUNK == 0
    nt = seq // tb
    nck = (tb // CHUNK) * heads
    return pl.pallas_call(
        functools.partial(_mlstm_prompt_kernel, tb=tb, heads=heads, dk=dk, dv=dv),
        grid=(batch, nt),
        in_specs=[pl.BlockSpec((tb, width), lambda b, t: (b * nt + t, 0)),
                  pl.BlockSpec((tb, 128), lambda b, t: (b * nt + t, 0)),
                  pl.BlockSpec((1, 128), lambda b, t: (0, 0)),
                  pl.BlockSpec((1, heads, dk, dv), lambda b, t: (b, 0, 0, 0)),
                  pl.BlockSpec((1, 8, dk), lambda b, t: (b, 0, 0)),
                  pl.BlockSpec((1, 8, 128), lambda b, t: (b, 0, 0)),
                  pl.BlockSpec((1, val), lambda b, t: (0, 0))],
        out_specs=[pl.BlockSpec((tb, val), lambda b, t: (b * nt + t, 0)),
                   pl.BlockSpec((1, heads, dk, dv), lambda b, t: (b, 0, 0, 0)),
                   pl.BlockSpec((1, 8, dk), lambda b, t: (b, 0, 0)),
                   pl.BlockSpec((1, 8, 128), lambda b, t: (b, 0, 0))],
        out_shape=[jax.ShapeDtypeStruct((batch * seq, val), F32),
                   jax.ShapeDtypeStruct((batch, heads, dk, dv), F32),
                   jax.ShapeDtypeStruct((batch, 8, dk), F32),
                   jax.ShapeDtypeStruct((batch, 8, 128), F32)],
        scratch_shapes=[pltpu.VMEM((heads, dk, dv), F32), pltpu.VMEM((8, dk), F32),
                        pltpu.VMEM((8, 128), F32),
                        pltpu.VMEM((tb, 128), F32), pltpu.VMEM((tb, val), F32), pltpu.VMEM((tb, 128), F32),
                        pltpu.VMEM((nck, dk, dv), F32), pltpu.VMEM((max(8, nck), dk), F32),
                        pltpu.VMEM((max(8, nck), 128), F32),
                        pltpu.VMEM((nck, dk, dv), F32), pltpu.VMEM((max(8, nck), dk), F32),
                        pltpu.VMEM((max(8, nck), 128), F32)],
        compiler_params=_cparams(("parallel", "arbitrary")),
        name="mlstm_prompt",
    )(pm, pif, bif, c0, n0, m0, nw)


def _mlstm_prep(w_in, b_if, norm_w, w_out, *, heads, dk, dv):
    main = 2 * heads * dk + 2 * heads * dv
    return dict(w_main=w_in[:, :main].astype(BF16),
                w_if=_pad_cols(w_in[:, main:], 128).astype(BF16),
                bif=_pad_cols(b_if[None, :], 128), nw=norm_w[None, :], w_out=w_out.astype(BF16))


def _mlstm_prompt_layer(x, g, p, c0, n0, m0, *, batch, seq, heads, dk, dv, norm=True, residual=True):
    pm, pif = _in_proj(x, g, p["w_main"], p["w_if"], norm=norm)
    n0p = jnp.pad(n0, ((0, 0), (0, 8 - heads), (0, 0)))
    m0p = jnp.broadcast_to(jnp.pad(m0, ((0, 0), (0, 8 - heads)))[:, :, None], (batch, 8, 128))
    o, c, n, m = _mlstm_prompt(pm, pif, p["bif"], c0, n0p, m0p, p["nw"],
                               batch=batch, seq=seq, heads=heads, dk=dk, dv=dv, tb=TB_MIX)
    y = _out_proj(o, p["w_out"], x if residual else None)
    return y, c, n[:, :heads, :], m[:, :heads, 0]


def _rwkv_proj_body(h, prev, mu_ref, wrkv_ref, w1_ref, w2_ref, a1_ref, a2_ref, g1_ref, g2_ref,
                    vec_ref, r_ref, k_ref, v_ref, lw_ref, kk_ref, a_ref, g_ref):
    xx = prev - h

    def mix(j):
        return (h + xx * mu_ref[j:j + 1, :]).astype(BF16)

    w0, a0, k_k, k_a = (vec_ref[j:j + 1, :] for j in range(4))
    r_ref[...] = jnp.dot(mix(0), wrkv_ref[0], preferred_element_type=F32)
    lora_w = _dot(jnp.tanh(_dot(mix(1), w1_ref[...])), w2_ref[...])
    w_log = -_softplus(-(w0 + lora_w)) - 0.5
    lw_ref[...] = -jnp.exp(w_log)
    k = jnp.dot(mix(2), wrkv_ref[1], preferred_element_type=F32)
    v_ref[...] = jnp.dot(mix(3), wrkv_ref[2], preferred_element_type=F32)
    a = _sigmoid(a0 + _dot(_dot(mix(4), a1_ref[...]), a2_ref[...]))
    g_ref[...] = _dot(_sigmoid(_dot(mix(5), g1_ref[...])), g2_ref[...])
    kk_ref[...] = k * k_k
    k_ref[...] = k * (1.0 + (a - 1.0) * k_a)
    a_ref[...] = a


def _rwkv_proj_sample_kernel(x_ref, gn_ref, prev_ref, *refs):
    h = _norm_rows(x_ref[...], gn_ref[...])
    hn_ref = refs[-1]
    hn_ref[...] = h
    _rwkv_proj_body(h, prev_ref[...], *refs[:-1])


def _rwkv_proj_prompt_kernel(x_ref, gn_ref, shift0_ref, *refs):
    carry_ref = refs[-1]
    shift_out_ref = refs[-2]
    t = pl.program_id(1)

    @pl.when(t == 0)
    def _():
        carry_ref[...] = shift0_ref[0]

    h = _norm_rows(x_ref[...], gn_ref[...])
    rows = h.shape[0]
    first = lax.broadcasted_iota(jnp.int32, h.shape, 0) == 0
    prev = jnp.where(first, carry_ref[0:1, :], pltpu.roll(h, 1, 0))
    carry_ref[0:1, :] = h[rows - 1:rows, :]
    _rwkv_proj_body(h, prev, *refs[:-2])

    @pl.when(t == pl.num_programs(1) - 1)
    def _():
        shift_out_ref[0] = carry_ref[...]


def _norm_rows(x, g):
    return x * lax.rsqrt(jnp.mean(x * x, axis=-1, keepdims=True) + RMS_EPS) * g


def _rwkv_proj(x, g_norm, shift0, p, *, batch, seq, tm):
    m, d = x.shape
    consts = [p["mu"], p["w_rkv"], p["w1"], p["w2"], p["a1"], p["a2"], p["g1"], p["g2"], p["vec"]]
    if seq == 1:
        row = pl.BlockSpec((m, d), lambda i: (0, 0))
        out = pl.pallas_call(
            _rwkv_proj_sample_kernel,
            grid=(1,),
            in_specs=[row, pl.BlockSpec((1, d), lambda i: (0, 0)), row] + [_const_block(a) for a in consts],
            out_specs=[row] * 8,
            out_shape=[jax.ShapeDtypeStruct((m, d), F32)] * 8,
            compiler_params=_cparams(("arbitrary",)),
            name="rwkv_proj_sample",
        )(x, g_norm, shift0, *consts)
        return out[:7], out[7]
    tm = min(tm, seq)
    assert seq % tm == 0
    nt = seq // tm
    row = pl.BlockSpec((tm, d), lambda b, t: (b * nt + t, 0))
    st = pl.BlockSpec((1, 8, d), lambda b, t: (b, 0, 0))

    def full(a):
        nd = a.ndim
        return pl.BlockSpec(a.shape, lambda b, t: (0,) * nd)

    shift0_p = jnp.pad(shift0[:, None, :], ((0, 0), (0, 7), (0, 0)))
    out = pl.pallas_call(
        _rwkv_proj_prompt_kernel,
        grid=(batch, nt),
        in_specs=[row, full(g_norm), st] + [full(a) for a in consts],
        out_specs=[row] * 7 + [st],
        out_shape=[jax.ShapeDtypeStruct((m, d), F32)] * 7 + [jax.ShapeDtypeStruct((batch, 8, d), F32)],
        scratch_shapes=[pltpu.VMEM((8, d), F32)],
        compiler_params=_cparams(("parallel", "arbitrary")),
        name="rwkv_proj_prompt",
    )(x, g_norm, shift0_p, *consts)
    return out[:7], out[7][:, 0, :]


def _rwkv_prompt_kernel(r_ref, k_ref, v_ref, lw_ref, kk_ref, a_ref, g_ref, s0_ref, hp_ref,
                        o_ref, s_out_ref, s_ref, rr_ref, yy_ref, mx_ref, n0_ref, gw_ref, gb_ref, el_ref,
                        *, tb, heads, hd):
    t = pl.program_id(1)
    l = CHUNK

    @pl.when(t == 0)
    def _():
        s_ref[...] = s0_ref[0]

    hh = range(heads)
    hs = [slice(h * hd, (h + 1) * hd) for h in hh]
    l2 = 2 * l
    r2 = lax.broadcasted_iota(jnp.int32, (l2, l2), 0)
    c2 = lax.broadcasted_iota(jnp.int32, (l2, l2), 1)
    same = (r2 >= l) == (c2 >= l)
    incl = same & (r2 >= c2)
    strict = same & (r2 > c2)
    tril = incl.astype(F32)
    first = lax.broadcasted_iota(jnp.int32, (l2, heads * hd), 0) < l
    first2 = (lax.broadcasted_iota(jnp.int32, (2 * l2, hd), 0) & l) == 0
    zeros = jnp.zeros((l2, hd), F32)

    def pair(pi, carry):
        rows = pl.ds(pl.multiple_of(pi * l2, l2), l2)
        lw = lw_ref[rows, :]
        lwc = _dot_hi(tril, lw)
        lw_end = jnp.where(first, lwc[l - 1:l, :], lwc[l2 - 1:l2, :])
        e_in = jnp.exp(lwc)
        e_prev = jnp.exp(lwc - lw)
        e_neg = jnp.exp(-lwc)
        e_end = jnp.exp(lw_end - lwc)
        r = [r_ref[rows, hs[h]] for h in hh]
        v = [v_ref[rows, hs[h]] for h in hh]
        k = [k_ref[rows, hs[h]] for h in hh]
        kk = [kk_ref[rows, hs[h]] for h in hh]
        kk = [kk[h] * lax.rsqrt(jnp.maximum(jnp.sum(kk[h] * kk[h], axis=-1, keepdims=True), 1e-24)) for h in hh]
        bv = [kk[h] * a_ref[rows, hs[h]] for h in hh]
        a_t = [-kk[h] * e_prev[:, hs[h]] for h in hh]
        r_t = [r[h] * e_in[:, hs[h]] for h in hh]
        gm = [_dot_nt(jnp.concatenate([a_t[h], r_t[h]], axis=0),
                      jnp.concatenate([bv[h] * e_neg[:, hs[h]], k[h] * e_neg[:, hs[h]]], axis=0)) for h in hh]
        ak_m = [jnp.where(strict, gm[h][:l2, l2:], 0.0).astype(BF16) for h in hh]
        rbk_m = [jnp.concatenate([jnp.where(incl, gm[h][l2:, :l2], 0.0),
                                  jnp.where(incl, gm[h][l2:, l2:], 0.0)], axis=1).astype(BF16) for h in hh]
        t_inv = _unit_lower_inverse([jnp.where(strict, -gm[h][:l2, :l2], 0.0) for h in hh], l)
        vb = [v[h].astype(BF16) for h in hh]
        akv = [_dot(ak_m[h], vb[h]) for h in hh]
        x1 = [_dot(t_inv[h], jnp.concatenate([a_t[h], akv[h]], axis=1)).astype(BF16) for h in hh]
        low = [jnp.concatenate([x1[h], jnp.concatenate([zeros.astype(BF16), vb[h]], axis=1)], axis=0) for h in hh]
        x2 = [_dot(rbk_m[h], low[h]) for h in hh]
        bk = [jnp.concatenate([bv[h] * e_end[:, hs[h]], k[h] * e_end[:, hs[h]]], axis=0) for h in hh]
        bk2 = [jnp.concatenate([jnp.where(first2, bk[h], 0.0), jnp.where(first2, 0.0, bk[h])], axis=1) for h in hh]
        mn = [_dot_tn(low[h], bk2[h]) for h in hh]
        for c in range(2):
            crow = pl.ds(pl.multiple_of(pi * l2 + c * l, l), l)
            for h in hh:
                mx_ref[h, crow, :] = mn[h][:hd, c * hd:(c + 1) * hd].astype(BF16)
                n0_ref[h, crow, :] = mn[h][hd:, c * hd:(c + 1) * hd]
        for h in hh:
            rr_ref[h, rows, :] = (r_t[h] + x2[h][:, :hd]).astype(BF16)
            yy_ref[h, rows, :] = x2[h][:, hd:]
            g = g_ref[rows, hs[h]]
            bonus = jnp.sum(r[h] * k[h] * hp_ref[0:1, hs[h]], axis=-1, keepdims=True) * v[h]
            gw_ref[h, rows, :] = hp_ref[1:2, hs[h]] * g
            gb_ref[h, rows, :] = (hp_ref[2:3, hs[h]] + bonus) * g
            el_ref[h, pl.ds(pi * 2, 1), :] = jnp.exp(lwc[l - 1:l, hs[h]])
            el_ref[h, pl.ds(pi * 2 + 1, 1), :] = jnp.exp(lwc[l2 - 1:l2, hs[h]])
        return carry

    lax.fori_loop(0, tb // l2, pair, 0)

    for ci in range(tb // l):
        rows = slice(ci * l, (ci + 1) * l)
        s = [s_ref[h] for h in hh]
        y = [_dot_nt(rr_ref[h, rows, :], s[h]) for h in hh]
        sm = [_dot(s[h], mx_ref[h, rows, :]) for h in hh]
        for h in hh:
            s_ref[h] = s[h] * el_ref[h, ci:ci + 1, :] + sm[h] + n0_ref[h, rows, :]
        y = [y[h] + yy_ref[h, rows, :] for h in hh]
        mean = [jnp.mean(y[h], axis=-1, keepdims=True) for h in hh]
        yc = [y[h] - mean[h] for h in hh]
        var = [jnp.mean(yc[h] * yc[h], axis=-1, keepdims=True) for h in hh]
        for h in hh:
            o_ref[rows, hs[h]] = yc[h] * lax.rsqrt(var[h] + RW_GN_EPS) * gw_ref[h, rows, :] + gb_ref[h, rows, :]

    @pl.when(t == pl.num_programs(1) - 1)
    def _():
        s_out_ref[0] = s_ref[...]


def _rwkv_prompt(r, k, v, lw, kk, a, g, s0, hp, *, batch, seq, heads, hd, tb):
    d = heads * hd
    tb = min(tb, seq)
    assert seq % tb == 0 and tb % CHUNK == 0
    nt = seq // tb
    row = pl.BlockSpec((tb, d), lambda b, t: (b * nt + t, 0))
    st = pl.BlockSpec((1, heads, hd, hd), lambda b, t: (b, 0, 0, 0))
    return pl.pallas_call(
        functools.partial(_rwkv_prompt_kernel, tb=tb, heads=heads, hd=hd),
        grid=(batch, nt),
        in_specs=[row] * 7 + [st, pl.BlockSpec((8, d), lambda b, t: (0, 0))],
        out_specs=[row, st],
        out_shape=[jax.ShapeDtypeStruct((batch * seq, d), F32),
                   jax.ShapeDtypeStruct((batch, heads, hd, hd), F32)],
        scratch_shapes=[pltpu.VMEM((heads, hd, hd), F32),
                        pltpu.VMEM((heads, tb, hd), BF16), pltpu.VMEM((heads, tb, hd), F32),
                        pltpu.VMEM((heads, tb, hd), BF16), pltpu.VMEM((heads, tb, hd), F32),
                        pltpu.VMEM((heads, tb, hd), F32), pltpu.VMEM((heads, tb, hd), F32),
                        pltpu.VMEM((heads, max(8, tb // CHUNK), hd), F32)],
        compiler_params=_cparams(("parallel", "arbitrary")),
        name="rwkv_prompt",
    )(r, k, v, lw, kk, a, g, s0, hp)


def _pad_rows(a, n):
    return jnp.pad(a, ((0, n - a.shape[0]), (0, 0)))


def _rwkv_prep(mu, w_rkv, w_o, w0, w1, w2, a0, a1, a2, g1, g2, k_k, k_a, r_k, lnx_w, lnx_b):
    d = w0.shape[0]
    lw = -(-w1.shape[1] // 128) * 128
    la = -(-a1.shape[1] // 128) * 128
    lg = -(-g1.shape[1] // 128) * 128
    return dict(mu=_pad_rows(mu, 8), w_rkv=w_rkv.astype(BF16), w_o=w_o.astype(BF16),
                w1=_pad_cols(w1, lw).astype(BF16), w2=_pad_rows(w2, lw).astype(BF16),
                a1=_pad_cols(a1, la).astype(BF16), a2=_pad_rows(a2, la).astype(BF16),
                g1=_pad_cols(g1, lg).astype(BF16), g2=_pad_rows(g2, lg).astype(BF16),
                vec=_pad_rows(jnp.stack([w0, a0, k_k, k_a]), 8),
                hp=_pad_rows(jnp.stack([r_k.reshape(d), lnx_w, lnx_b]), 8))


def _rwkv_prompt_layer(x, g_norm, p, shift0, s0, *, batch, seq, heads, hd, residual=True):
    (r, k, v, lw, kk, a, g), shift = _rwkv_proj(x, g_norm, shift0, p, batch=batch, seq=seq, tm=TM_RWKV_PROJ)
    o, s = _rwkv_prompt(r, k, v, lw, kk, a, g, s0, p["hp"], batch=batch, seq=seq, heads=heads, hd=hd, tb=TB_MIX)
    y = _out_proj(o, p["w_o"], x if residual else None)
    return y, shift, s


def _gdn_sample_pre_kernel(pm_ref, pba_ref, conv_ref, cw_ref, gp_ref, qkv_ref, conv_out_ref, sc_ref,
                           *, heads, dk, dv):
    key = heads * dk
    ch = 2 * key + heads * dv
    u = pm_ref[:, 0:ch]
    y = u * cw_ref[CONV_W - 1:CONV_W, :]
    for j in range(CONV_W - 1):
        y = y + conv_ref[j] * cw_ref[j:j + 1, :]
        conv_out_ref[j] = conv_ref[j + 1] if j + 1 < CONV_W - 1 else u
    y = _silu(y)
    for c in range(ch // 128):
        cs = slice(c * 128, (c + 1) * 128)
        yc = y[:, cs]
        if c * 128 < key:
            yc = yc * lax.rsqrt(jnp.sum(yc * yc, axis=-1, keepdims=True) + 1e-6) * (dk ** -0.5)
        elif c * 128 < 2 * key:
            yc = yc * lax.rsqrt(jnp.sum(yc * yc, axis=-1, keepdims=True) + 1e-6)
        qkv_ref[:, cs] = yc
    ba = pba_ref[...]
    lane = lax.broadcasted_iota(jnp.int32, ba.shape, 1)
    g = -jnp.exp(gp_ref[0:1, :]) * _softplus(ba + gp_ref[1:2, :])
    sc_ref[...] = jnp.where(lane < heads, _sigmoid(ba), jnp.exp(g))


def _gdn_sample_pre(pm, pba, conv_t, cw_t, gp, *, heads, dk, dv):
    n = pm.shape[0]
    ch = 2 * heads * dk + heads * dv
    return pl.pallas_call(
        functools.partial(_gdn_sample_pre_kernel, heads=heads, dk=dk, dv=dv),
        out_shape=[jax.ShapeDtypeStruct((n, ch), F32),
                   jax.ShapeDtypeStruct((CONV_W - 1, n, ch), F32),
                   jax.ShapeDtypeStruct((n, 128), F32)],
        compiler_params=pltpu.CompilerParams(vmem_limit_bytes=V7X_VMEM_LIMIT),
        name="gdn_sample_pre",
    )(pm, pba, conv_t, cw_t, gp)


SEQ_PER_STEP = 4


def _seq_block(a, nb):
    nd = a.ndim
    return pl.BlockSpec((nb,) + a.shape[1:], lambda b: (b,) + (0,) * (nd - 1))


def _const_block(a):
    nd = a.ndim
    return pl.BlockSpec(a.shape, lambda b: (0,) * nd)


def _gdn_sample_step_kernel(s0_ref, cols_ref, v_ref, z_ref, sc_ref, nw_ref, acc_ref, s_out_ref, o_ref,
                            *, heads, nb):
    del acc_ref
    hh = range(heads)
    for i in range(nb):
        kc = [cols_ref[i, :, h:h + 1] for h in hh]
        qc = [cols_ref[i, :, heads + h:heads + h + 1] for h in hh]
        s0 = [s0_ref[i, h] for h in hh]
        eg = [sc_ref[i, h:h + 1, 1:2] for h in hh]
        ks = [jnp.sum(kc[h] * s0[h], axis=0, keepdims=True) for h in hh]
        s1 = [eg[h] * s0[h] + kc[h] * (sc_ref[i, h:h + 1, 0:1] * (v_ref[i, h:h + 1, :] - eg[h] * ks[h]))
              for h in hh]
        o = [jnp.sum(qc[h] * s1[h], axis=0, keepdims=True) for h in hh]
        ms = [jnp.mean(o[h] * o[h], axis=-1, keepdims=True) for h in hh]
        for h in hh:
            s_out_ref[i, h] = s1[h]
            o_ref[i, h:h + 1, :] = o[h] * lax.rsqrt(ms[h] + RMS_EPS) * nw_ref[...] * _silu(z_ref[i, h:h + 1, :])


def _gdn_sample_step(s_all, layer, s_acc, cols, v, z, sc, nw):
    _, n, heads, dk, dv = s_all.shape
    nb = SEQ_PER_STEP
    assert n % nb == 0
    state = pl.BlockSpec((None, nb, heads, dk, dv), lambda b: (layer, b, 0, 0, 0))
    return pl.pallas_call(
        functools.partial(_gdn_sample_step_kernel, heads=heads, nb=nb),
        grid=(n // nb,),
        in_specs=[state, _seq_block(cols, nb), _seq_block(v, nb), _seq_block(z, nb), _seq_block(sc, nb),
                  pl.BlockSpec((1, dv), lambda b: (0, 0)), pl.BlockSpec(memory_space=pl.ANY)],
        out_specs=[state, _seq_block(v, nb)],
        out_shape=[jax.ShapeDtypeStruct(s_all.shape, F32), jax.ShapeDtypeStruct(v.shape, F32)],
        input_output_aliases={6: 0},
        compiler_params=_cparams(("parallel",)),
        name="gdn_sample_step",
    )(s_all, cols, v, z, sc, nw, s_acc)


def _gdn_sample_layer(x, g, p, conv0, s_all, layer, s_acc, *, heads, dk, dv):
    n = x.shape[0]
    key, val = heads * dk, heads * dv
    ch = 2 * key + val
    pm, pba = _in_proj(x, g, p["w_main"], p["w_ba"])
    qkv, conv_t, sc = _gdn_sample_pre(pm, pba, jnp.transpose(conv0, (1, 0, 2)), p["cw_t"], p["gp"],
                                      heads=heads, dk=dk, dv=dv)
    q_c = jnp.transpose(qkv[:, :key].reshape(n, heads, dk), (0, 2, 1))
    k_c = jnp.transpose(qkv[:, key:2 * key].reshape(n, heads, dk), (0, 2, 1))
    cols = jnp.concatenate([k_c, q_c], axis=-1)
    sc3 = jnp.stack([sc[:, :heads], sc[:, heads:2 * heads]], axis=-1)
    s_acc, o = _gdn_sample_step(s_all, layer, s_acc, cols, qkv[:, 2 * key:].reshape(n, heads, dv),
                                pm[:, ch:].reshape(n, heads, dv), sc3, p["nw"])
    y = _out_proj(o.reshape(n, val), p["w_out"], x)
    return y, jnp.transpose(conv_t, (1, 0, 2)), s_acc


def _mlstm_sample_step_kernel(c0_ref, n0_ref, cols_ref, q_ref, k_ref, v_ref, op_ref, sc_ref, bif_ref, nw_ref,
                              c_out_ref, n_out_ref, m_out_ref, o_ref, *, heads, dk, nb):
    scale = dk ** -0.5
    for i in range(nb):
        gi = sc_ref[i, :, 0:1] + bif_ref[:, 0:1]
        gf = sc_ref[i, :, 1:2] + bif_ref[:, 1:2]
        m0 = sc_ref[i, :, 2:3]
        gi = GATE_CAP * jnp.tanh(gi / GATE_CAP)
        logf = _log_sigmoid(GATE_CAP * jnp.tanh(gf / GATE_CAP))
        m_new = jnp.maximum(logf + m0, gi)
        f_s = jnp.exp(logf + m0 - m_new)
        i_s = jnp.exp(gi - m_new)
        m_out_ref[i] = m_new
        n1 = f_s * n0_ref[i] + i_s * k_ref[i]
        n_out_ref[i] = n1
        den = jnp.sum(q_ref[i] * scale * n1, axis=-1, keepdims=True)
        floor = jnp.exp(-m_new)
        hh = range(heads)
        kc = [cols_ref[i, :, h:h + 1] for h in hh]
        qc = [cols_ref[i, :, heads + h:heads + h + 1] * scale for h in hh]
        c1 = [f_s[h:h + 1, :] * c0_ref[i, h] + i_s[h:h + 1, :] * (kc[h] * v_ref[i, h:h + 1, :]) for h in hh]
        num = [jnp.sum(qc[h] * c1[h], axis=0, keepdims=True) for h in hh]
        h_t = [num[h] / jnp.maximum(jnp.abs(den[h:h + 1, :]), floor[h:h + 1, :]) for h in hh]
        ms = [jnp.mean(h_t[h] * h_t[h], axis=-1, keepdims=True) for h in hh]
        for h in hh:
            c_out_ref[i, h] = c1[h]
            h_n = h_t[h] * lax.rsqrt(ms[h] + RMS_EPS) * nw_ref[h:h + 1, :]
            o_ref[i, h:h + 1, :] = _sigmoid(op_ref[i, h:h + 1, :]) * h_n


def _mlstm_sample_step(c0, n0, cols, q, k, v, o_pre, sc, bif2, nw2):
    n, heads, dk, dv = c0.shape
    nb = SEQ_PER_STEP
    assert n % nb == 0
    full = _const_block

    def blk(a):
        return _seq_block(a, nb)

    m_shape = (n, heads, 1)
    return pl.pallas_call(
        functools.partial(_mlstm_sample_step_kernel, heads=heads, dk=dk, nb=nb),
        grid=(n // nb,),
        in_specs=[blk(c0), blk(n0), blk(cols), blk(q), blk(k), blk(v), blk(o_pre), blk(sc), full(bif2), full(nw2)],
        out_specs=[blk(c0), blk(n0), pl.BlockSpec((nb, heads, 1), lambda b: (b, 0, 0)), blk(v)],
        out_shape=[jax.ShapeDtypeStruct(c0.shape, F32), jax.ShapeDtypeStruct(n0.shape, F32),
                   jax.ShapeDtypeStruct(m_shape, F32), jax.ShapeDtypeStruct(v.shape, F32)],
        compiler_params=_cparams(("parallel",)),
        name="mlstm_sample_step",
    )(c0, n0, cols, q, k, v, o_pre, sc, bif2, nw2)


def _mlstm_sample_layer(x, g, p, c0, n0, m0, *, heads, dk, dv):
    n = x.shape[0]
    qk_w, val = heads * dk, heads * dv
    pm, pif = _in_proj(x, g, p["w_main"], p["w_if"])
    q = pm[:, :qk_w].reshape(n, heads, dk)
    k = pm[:, qk_w:2 * qk_w].reshape(n, heads, dk)
    v = pm[:, 2 * qk_w:2 * qk_w + val].reshape(n, heads, dv)
    o_pre = pm[:, 2 * qk_w + val:].reshape(n, heads, dv)
    cols = jnp.concatenate([jnp.transpose(k, (0, 2, 1)), jnp.transpose(q, (0, 2, 1))], axis=-1)
    sc = jnp.stack([pif[:, :heads], pif[:, heads:2 * heads], m0], axis=-1)
    bif2 = jnp.stack([p["bif"][0, :heads], p["bif"][0, heads:2 * heads]], axis=-1)
    c1, n1, m1, o = _mlstm_sample_step(c0, n0, cols, q, k, v, o_pre, sc, bif2, p["nw"].reshape(heads, dv))
    y = _out_proj(o.reshape(n, val), p["w_out"], x)
    return y, c1, n1, m1[:, :, 0]


RW_ROW_GROUP = 8


def _rwkv_sample_step_kernel(s_ref, r_ref, k_ref, lw_ref, kk_ref, a_ref, v_ref, g_ref, hp_ref,
                             s_out_ref, o_ref, y_ref, *, hd):
    kk = kk_ref[0]
    kk = kk * lax.rsqrt(jnp.maximum(jnp.sum(kk * kk, axis=0, keepdims=True), 1e-24))
    av = -kk
    bv = kk * a_ref[0]
    w = jnp.exp(lw_ref[0])
    r = r_ref[0]
    k = k_ref[0]
    v = v_ref[0]
    for v0 in range(0, hd, RW_ROW_GROUP):
        vv = range(v0, v0 + RW_ROW_GROUP)
        s0 = [s_ref[0, i] for i in vv]
        sa = [jnp.sum(s * av, axis=0, keepdims=True) for s in s0]
        s1 = [s * w + sa_i * bv + v[i:i + 1, :] * k for s, sa_i, i in zip(s0, sa, vv)]
        y = [jnp.sum(s * r, axis=0, keepdims=True) for s in s1]
        for i, s, y_i in zip(vv, s1, y):
            s_out_ref[0, i] = s
            y_ref[i:i + 1, :] = y_i
    y = y_ref[...]
    yc = y - jnp.mean(y, axis=0, keepdims=True)
    var = jnp.mean(yc * yc, axis=0, keepdims=True)
    hp = hp_ref[0]
    yn = yc * lax.rsqrt(var + RW_GN_EPS) * hp[:, 1:2] + hp[:, 2:3]
    bonus = jnp.sum(r * k * hp[:, 0:1], axis=0, keepdims=True)
    o_ref[0] = (yn + bonus * v) * g_ref[0]


def _rwkv_sample_step(s_t, r, k, lw, kk, a, v, g, hp3):
    heads, hd, _, n = s_t.shape

    def blk(z):
        nd = z.ndim
        return pl.BlockSpec((1,) + z.shape[1:], lambda h: (h,) + (0,) * (nd - 1))

    return pl.pallas_call(
        functools.partial(_rwkv_sample_step_kernel, hd=hd),
        grid=(heads,),
        in_specs=[blk(s_t)] + [blk(z) for z in (r, k, lw, kk, a, v, g, hp3)],
        out_specs=[blk(s_t), blk(v)],
        out_shape=[jax.ShapeDtypeStruct(s_t.shape, F32), jax.ShapeDtypeStruct(v.shape, F32)],
        scratch_shapes=[pltpu.VMEM((hd, n), F32)],
        compiler_params=_cparams(("parallel",)),
        name="rwkv_sample_step",
    )(s_t, r, k, lw, kk, a, v, g, hp3)


def _rwkv_sample_layer(x, g_norm, p, shift0, s0, *, heads, hd):
    n, d = x.shape
    (r, k, v, lw, kk, a, g), hn = _rwkv_proj(x, g_norm, shift0, p, batch=n, seq=1, tm=n)

    def lanes(z):
        return z.T.reshape(heads, hd, n)

    hp = p["hp"]
    hp3 = jnp.stack([hp[j].reshape(heads, hd) for j in range(3)], axis=-1)
    s1_t, o_t = _rwkv_sample_step(jnp.transpose(s0, (1, 2, 3, 0)), lanes(r), lanes(k), lanes(lw), lanes(kk),
                                  lanes(a), lanes(v), lanes(g), hp3)
    y = _out_proj(o_t.reshape(d, n).T, p["w_o"], x)
    return y, hn, jnp.transpose(s1_t, (3, 0, 1, 2))


def _pad_cols(a, n):
    return jnp.pad(a, ((0, 0), (0, n - a.shape[1])))


def _gdn_prep(w_in, conv_w, a_log, dt_bias, norm_w, w_out, *, heads, dk, dv):
    key, val = heads * dk, heads * dv
    ch = 2 * key + val
    main = ch + val
    gp = jnp.zeros((8, 128), F32)
    gp = gp.at[0, heads:2 * heads].set(a_log).at[1, heads:2 * heads].set(dt_bias)
    return dict(w_main=w_in[:, :main].astype(BF16),
                w_ba=_pad_cols(w_in[:, main:], 128).astype(BF16),
                cw_t=jnp.pad(conv_w.T, ((0, 8 - CONV_W), (0, 0))),
                gp=gp, nw=norm_w[None, :], w_out=w_out.astype(BF16))


def _gdn_prompt_layer(x, g, p, conv0, s0, *, batch, seq, heads, dk, dv, norm=True, residual=True):
    pm, pba = _in_proj(x, g, p["w_main"], p["w_ba"], norm=norm)
    conv0 = jnp.pad(conv0, ((0, 0), (8 - (CONV_W - 1), 0), (0, 0)))
    o, conv, s = _gdn_prompt(pm, pba, conv0, s0, p["cw_t"], p["gp"], p["nw"],
                             batch=batch, seq=seq, heads=heads, dk=dk, dv=dv, tb=TB_MIX)
    y = _out_proj(o, p["w_out"], x if residual else None)
    return y, conv[:, 8 - (CONV_W - 1):, :], s


def _trunk(x, states, w, *, batch, seq):
    conv_in, gs_in, c_in, n_in, m_in, shift_in, rs_in = states
    depth = w["norm_mix"].shape[0]
    gh, gdk, gdv = gs_in.shape[2:]
    mh, mdk, mdv = c_in.shape[2:]
    rh, rhd = rs_in.shape[2:4]
    prompt = seq > 1
    outs = [[] for _ in range(7)]
    gs_acc = None if prompt else jnp.zeros_like(gs_in)
    for i in range(depth):
        j = i // 3
        g = w["norm_mix"][i][None, :]
        if i % 3 == 0:
            p = w["gdn"][j]
            if prompt:
                x, cb, s = _gdn_prompt_layer(x, g, p, conv_in[j], gs_in[j], batch=batch, seq=seq,
                                             heads=gh, dk=gdk, dv=gdv)
                outs[1].append(s)
            else:
                x, cb, gs_acc = _gdn_sample_layer(x, g, p, conv_in[j], gs_in, j, gs_acc, heads=gh, dk=gdk, dv=gdv)
            outs[0].append(cb)
        elif i % 3 == 1:
            p = w["ml"][j]
            if prompt:
                x, c, n, m = _mlstm_prompt_layer(x, g, p, c_in[j], n_in[j], m_in[j], batch=batch, seq=seq,
                                                 heads=mh, dk=mdk, dv=mdv)
            else:
                x, c, n, m = _mlstm_sample_layer(x, g, p, c_in[j], n_in[j], m_in[j], heads=mh, dk=mdk, dv=mdv)
            outs[2].append(c)
            outs[3].append(n)
            outs[4].append(m)
        else:
            p = w["rw"][j]
            if prompt:
                x, sh, s = _rwkv_prompt_layer(x, g, p, shift_in[j], rs_in[j], batch=batch, seq=seq,
                                              heads=rh, hd=rhd)
            else:
                x, sh, s = _rwkv_sample_layer(x, g, p, shift_in[j], rs_in[j], heads=rh, hd=rhd)
            outs[5].append(sh)
            outs[6].append(s)
        g_out = w["norm_final"][None, :] if i == depth - 1 else None
        x = _ffn(x, w["norm_ffn"][i][None, :], w["ffn_w1"][i], w["ffn_w2"][i], g_out, tm=TM_FFN, tf=TF_FFN)
    y = x
    new =[jnp.stack(z, axis=0) if z else None for z in outs]
    if not prompt:
        new[1] = gs_acc
    return y, tuple(new)


def kernel(x_prompt, x_sample, state_gdn_conv, state_gdn_S, state_mlstm_C, state_mlstm_n, state_mlstm_m, state_rwkv_shift, state_rwkv_S, norm_mix, norm_ffn, norm_final, gdn_w_in, gdn_conv_w, gdn_a_log, gdn_dt_bias, gdn_norm_w, gdn_w_out, ml_w_in, ml_b_if, ml_norm_w, ml_w_out, rw_mu, rw_w_rkv, rw_w_o, rw_w0, rw_w1, rw_w2, rw_a0, rw_a1, rw_a2, rw_g1, rw_g2, rw_k_k, rw_k_a, rw_r_k, rw_lnx_w, rw_lnx_b, ffn_w1, ffn_w2):
    gh, gdk, gdv = state_gdn_S.shape[2:]
    mh, mdk, mdv = state_mlstm_C.shape[2:]
    w = dict(
        norm_mix=norm_mix, norm_ffn=norm_ffn, norm_final=norm_final,
        ffn_w1=[ffn_w1[i].astype(BF16) for i in range(ffn_w1.shape[0])],
        ffn_w2=[ffn_w2[i].astype(BF16) for i in range(ffn_w2.shape[0])],
        gdn=[_gdn_prep(gdn_w_in[j], gdn_conv_w[j], gdn_a_log[j], gdn_dt_bias[j], gdn_norm_w[j], gdn_w_out[j],
                       heads=gh, dk=gdk, dv=gdv) for j in range(gdn_w_in.shape[0])],
        ml=[_mlstm_prep(ml_w_in[j], ml_b_if[j], ml_norm_w[j], ml_w_out[j], heads=mh, dk=mdk, dv=mdv)
            for j in range(ml_w_in.shape[0])],
        rw=[_rwkv_prep(rw_mu[j], rw_w_rkv[j], rw_w_o[j], rw_w0[j], rw_w1[j], rw_w2[j], rw_a0[j], rw_a1[j],
                       rw_a2[j], rw_g1[j], rw_g2[j], rw_k_k[j], rw_k_a[j], rw_r_k[j], rw_lnx_w[j], rw_lnx_b[j])
            for j in range(rw_mu.shape[0])])
    sample_states = (state_gdn_conv, state_gdn_S, state_mlstm_C, state_mlstm_n, state_mlstm_m,
                     state_rwkv_shift, state_rwkv_S)
    bp, tp, d = x_prompt.shape
    bs, ts, _ = x_sample.shape
    assert ts == 1
    prompt_states = tuple(jnp.zeros((s.shape[0], bp) + s.shape[2:], s.dtype) for s in sample_states)
    y_p, new_p = _trunk(x_prompt.reshape(bp * tp, d), prompt_states, w, batch=bp, seq=tp)
    y_s, new_s = _trunk(x_sample.reshape(bs * ts, d), sample_states, w, batch=bs, seq=ts)
    out = [y_p.reshape(bp, tp, d), y_s.reshape(bs, ts, d)]
    for a, b in zip(new_p, new_s):
        out += [a, b]
    return tuple(out)
```

```python
import functools
import math

import jax
import jax.numpy as jnp
from jax import lax
from jax.experimental import pallas as pl
from jax.experimental.pallas import tpu as pltpu

F32 = jnp.float32
BF16 = jnp.bfloat16

RMS_EPS = 1e-6
NEG_BIG = -1e30
GATE_CAP = 15.0
RW_GN_EPS = 64e-5
CONV_W = 4
CHUNK = 64
RW_GROUP = 8
V7X_VMEM_LIMIT = 56 * 1024 * 1024
HI = lax.Precision.HIGHEST


def _cparams(sem):
    return pltpu.CompilerParams(dimension_semantics=sem, vmem_limit_bytes=V7X_VMEM_LIMIT)


def _dot(a, b):
    return jnp.dot(a.astype(BF16), b.astype(BF16), preferred_element_type=F32)


def _dot_nt(a, b):
    return lax.dot_general(a.astype(BF16), b.astype(BF16), (((1,), (1,)), ((), ())),
                           preferred_element_type=F32)


def _dot_tn(a, b):
    return lax.dot_general(a.astype(BF16), b.astype(BF16), (((0,), (0,)), ((), ())),
                           preferred_element_type=F32)


def _dot_hi(a, b):
    return jnp.dot(a, b, preferred_element_type=F32, precision=HI)


def _sigmoid(x):
    return 1.0 / (1.0 + jnp.exp(-x))


def _silu(x):
    return x * _sigmoid(x)


def _softplus(x):
    return jnp.maximum(x, 0.0) + jnp.log(1.0 + jnp.exp(-jnp.abs(x)))


def _log_sigmoid(x):
    return -_softplus(-x)


def _tri_masks(l):
    r = lax.broadcasted_iota(jnp.int32, (l, l), 0)
    c = lax.broadcasted_iota(jnp.int32, (l, l), 1)
    return r >= c, r > c


INV_BASE = 16


def _unit_lower_inverse(mats, l):
    n = mats[0].shape[0]
    r = lax.broadcasted_iota(jnp.int32, (n, n), 0)
    c = lax.broadcasted_iota(jnp.int32, (n, n), 1)
    eye = (r == c).astype(F32)
    size = min(INV_BASE, l)
    shift = size.bit_length() - 1
    diag = (r >> shift) == (c >> shift)
    merges = []
    s = size
    while s < l:
        sh = s.bit_length() - 1
        off = ((r >> (sh + 1)) == (c >> (sh + 1))) & ((r >> sh) > (c >> sh))
        merges.append([jnp.where(off, a, 0.0).astype(BF16) for a in mats])
        s *= 2
    t = [eye - jnp.where(diag, a, 0.0) for a in mats]
    tb = [ti.astype(BF16) for ti in t]
    ab = [jnp.where(diag, a, 0.0).astype(BF16) for a in mats]
    p = [_dot(a, a).astype(BF16) for a in ab]
    k = 2
    while k < size:
        if 2 * k < size:
            both = [_dot(pi, jnp.concatenate([ti, pi], axis=1)) for ti, pi in zip(tb, p)]
            t = [ti + bi[:, :n] for ti, bi in zip(t, both)]
            p = [bi[:, n:].astype(BF16) for bi in both]
        else:
            t = [ti + _dot(pi, ti_b) for ti, ti_b, pi in zip(t, tb, p)]
        tb = [ti.astype(BF16) for ti in t]
        k *= 2
    for a_off in merges:
        x = [_dot(a, ti) for a, ti in zip(a_off, tb)]
        y = [_dot(ti, xi) for ti, xi in zip(tb, x)]
        t = [ti - yi for ti, yi in zip(t, y)]
        tb = [ti.astype(BF16) for ti in t]
    return tb


def _nmm_kernel(*refs, norm, residual, aux):
    x_ref, g_ref, w_ref = refs[:3]
    pos = 3
    res_ref = aux_w_ref = aux_o_ref = None
    if residual:
        res_ref = refs[pos]
        pos += 1
    if aux:
        aux_w_ref = refs[pos]
        pos += 1
    o_ref = refs[pos]
    if aux:
        aux_o_ref = refs[pos + 1]
    xn_ref = refs[-1]

    @pl.when(pl.program_id(1) == 0)
    def _():
        x = x_ref[...]
        if norm:
            x = x * lax.rsqrt(jnp.mean(x * x, axis=-1, keepdims=True) + RMS_EPS) * g_ref[...]
        xn_ref[...] = x.astype(BF16)
        if aux:
            aux_o_ref[...] = jnp.dot(xn_ref[...], aux_w_ref[...], preferred_element_type=F32)

    y = jnp.dot(xn_ref[...], w_ref[...], preferred_element_type=F32)
    if residual:
        y = res_ref[...] + y
    o_ref[...] = y


def _nmm(x, g, w, res=None, w_aux=None, *, norm, tm, tn):
    m, k = x.shape
    n = w.shape[1]
    tm, tn = min(tm, m), min(tn, n)
    assert m % tm == 0 and n % tn == 0
    in_specs = [pl.BlockSpec((tm, k), lambda i, j: (i, 0)),
                pl.BlockSpec((1, k), lambda i, j: (0, 0)),
                pl.BlockSpec((k, tn), lambda i, j: (0, j))]
    args = [x, g, w]
    out_specs = [pl.BlockSpec((tm, tn), lambda i, j: (i, j))]
    out_shape = [jax.ShapeDtypeStruct((m, n), F32)]
    if res is not None:
        in_specs.append(pl.BlockSpec((tm, tn), lambda i, j: (i, j)))
        args.append(res)
    if w_aux is not None:
        na = w_aux.shape[1]
        in_specs.append(pl.BlockSpec((k, na), lambda i, j: (0, 0)))
        args.append(w_aux)
        out_specs.append(pl.BlockSpec((tm, na), lambda i, j: (i, 0)))
        out_shape.append(jax.ShapeDtypeStruct((m, na), F32))
    out = pl.pallas_call(
        functools.partial(_nmm_kernel, norm=norm, residual=res is not None, aux=w_aux is not None),
        grid=(m // tm, n // tn),
        in_specs=in_specs,
        out_specs=out_specs,
        out_shape=out_shape,
        scratch_shapes=[pltpu.VMEM((tm, k), BF16)],
        compiler_params=_cparams(("parallel", "arbitrary")),
        name="nmm",
    )(*args)
    return out if w_aux is not None else out[0]


TM_IN_PROJ = 512
TM_GDN_PROJ = 256
TM_OUT_PROJ = 1024
TM_FFN, TF_FFN = 1024, 1024
TB_MIX = 256
TM_RWKV_PROJ = 256


def _in_proj(x, g, w_main, w_aux, *, norm=True):
    return _nmm(x, g, w_main, None, w_aux, norm=norm, tm=TM_IN_PROJ, tn=w_main.shape[1])


def _out_proj(o, w_out, res):
    ones = jnp.ones((1, o.shape[1]), F32)
    return _nmm(o, ones, w_out, res, norm=False, tm=TM_OUT_PROJ, tn=w_out.shape[1])


def _ffn_kernel(x_ref, g_ref, w1_ref, w2_ref, go_ref, o_ref, xn_ref, acc_ref, *, out_norm):
    f = pl.program_id(1)

    @pl.when(f == 0)
    def _():
        xn_ref[...] = _norm_rows(x_ref[...], g_ref[...]).astype(BF16)
        acc_ref[...] = jnp.zeros_like(acc_ref)

    h = jnp.dot(xn_ref[...], w1_ref[...], preferred_element_type=F32)
    a = jnp.square(jnp.maximum(h, 0.0)).astype(BF16)
    acc_ref[...] += jnp.dot(a, w2_ref[...], preferred_element_type=F32)

    @pl.when(f == pl.num_programs(1) - 1)
    def _():
        y = x_ref[...] + acc_ref[...]
        o_ref[...] = _norm_rows(y, go_ref[...]) if out_norm else y


def _ffn(x, g, w1, w2, g_out=None, *, tm, tf):
    m, d = x.shape
    dff = w1.shape[1]
    tm, tf = min(tm, m), min(tf, dff)
    assert m % tm == 0 and dff % tf == 0
    vec = pl.BlockSpec((1, d), lambda i, j: (0, 0))
    return pl.pallas_call(
        functools.partial(_ffn_kernel, out_norm=g_out is not None),
        grid=(m // tm, dff // tf),
        in_specs=[pl.BlockSpec((tm, d), lambda i, j: (i, 0)), vec,
                  pl.BlockSpec((d, tf), lambda i, j: (0, j)),
                  pl.BlockSpec((tf, d), lambda i, j: (j, 0)), vec],
        out_specs=pl.BlockSpec((tm, d), lambda i, j: (i, 0)),
        out_shape=jax.ShapeDtypeStruct((m, d), F32),
        scratch_shapes=[pltpu.VMEM((tm, d), BF16), pltpu.VMEM((tm, d), F32)],
        compiler_params=_cparams(("parallel", "arbitrary")),
        name="ffn",
    )(x, g, w1, w2, g if g_out is None else g_out)


def _gdn_prompt_kernel(qkv_ref, pba_ref, s0_ref, gp_ref, nw_ref,
                       o_ref, s_out_ref,
                       s_ref, u_ref, w_ref, qd_ref, kd_ref, qk_ref, gl_ref,
                       *, tb, heads, dk, dv):
    t = pl.program_id(1)
    key = heads * dk
    ch = 2 * key + heads * dv
    l = CHUNK

    @pl.when(t == 0)
    def _():
        s_ref[...] = s0_ref[0]

    a_log = gp_ref[0:1, :]
    dt_bias = gp_ref[1:2, :]
    hh = range(heads)
    l2 = 2 * l
    r2 = lax.broadcasted_iota(jnp.int32, (l2, l2), 0)
    c2 = lax.broadcasted_iota(jnp.int32, (l2, l2), 1)
    same = (r2 >= l) == (c2 >= l)
    incl = same & (r2 >= c2)
    strict = same & (r2 > c2)
    tril = incl.astype(F32)
    first = lax.broadcasted_iota(jnp.int32, (l2, 128), 0) < l

    pairs = range(tb // l2)
    rows = [slice(pi * l2, (pi + 1) * l2) for pi in pairs]
    beta_all, gc, gc_t, g_end = [], [], [], []
    for pi in pairs:
        ba = pba_ref[rows[pi], :]
        beta_all.append(_sigmoid(ba))
        g_all = -jnp.exp(a_log) * _softplus(ba + dt_bias)
        gc.append(_dot_hi(tril, g_all))
        gc_t.append(gc[pi].T)
        g_end.append(jnp.where(first, gc[pi][l - 1:l, :], gc[pi][l2 - 1:l2, :]))
        gl_ref[2 * pi:2 * pi + 1, :] = jnp.exp(gc[pi][l - 1:l, :])
        gl_ref[2 * pi + 1:2 * pi + 2, :] = jnp.exp(gc[pi][l2 - 1:l2, :])
    cc = [(pi, h) for pi in pairs for h in hh]
    b_col = [beta_all[pi][:, h:h + 1] for pi, h in cc]
    gi = [gc[pi][:, heads + h:heads + h + 1] for pi, h in cc]
    q = [qkv_ref[rows[pi], h * dk:(h + 1) * dk] for pi, h in cc]
    k = [qkv_ref[rows[pi], key + h * dk:key + (h + 1) * dk] for pi, h in cc]
    v = [qkv_ref[rows[pi], 2 * key + h * dv:2 * key + (h + 1) * dv] for pi, h in cc]
    nc = range(len(cc))
    dmat = [jnp.where(incl, jnp.exp(jnp.where(incl, gi[i] - gc_t[pi][heads + h:heads + h + 1, :], 0.0)), 0.0)
            for i, (pi, h) in enumerate(cc)]
    kb = [k[i] * b_col[i] for i in nc]
    kk = [_dot_nt(kb[i], k[i]) for i in nc]
    qk = [_dot_nt(q[i], k[i]) for i in nc]
    t_inv = _unit_lower_inverse([jnp.where(strict, kk[i] * dmat[i], 0.0) for i in nc], l)
    egi = [jnp.exp(gi[i]) for i in nc]
    sol = [_dot(t_inv[i], jnp.concatenate([v[i] * b_col[i], kb[i] * egi[i]], axis=-1)) for i in nc]
    for i, (pi, h) in enumerate(cc):
        hs = slice(h * dk, (h + 1) * dk)
        u_ref[rows[pi], h * dv:(h + 1) * dv] = sol[i][:, :dv]
        w_ref[rows[pi], hs] = sol[i][:, dv:].astype(BF16)
        qd_ref[rows[pi], hs] = (q[i] * egi[i]).astype(BF16)
        kd_ref[rows[pi], hs] = (k[i] * jnp.exp(g_end[pi][:, heads + h:heads + h + 1] - gi[i])).astype(BF16)
        qkm = jnp.where(incl, qk[i] * dmat[i], 0.0).astype(BF16)
        qk_ref[h, pi * l2:pi * l2 + l, :] = qkm[:l, :l]
        qk_ref[h, pi * l2 + l:(pi + 1) * l2, :] = qkm[l:, l:]

    for ci in range(tb // l):
        rows = slice(ci * l, (ci + 1) * l)
        s = [s_ref[h] for h in hh]
        wq = [_dot(jnp.concatenate([w_ref[rows, h * dk:(h + 1) * dk], qd_ref[rows, h * dk:(h + 1) * dk]], axis=0),
                   s[h]) for h in hh]
        v_new = [u_ref[rows, h * dv:(h + 1) * dv] - wq[h][:l] for h in hh]
        o2 = [_dot(qk_ref[h, rows, :], v_new[h]) for h in hh]
        ds = [_dot_tn(kd_ref[rows, h * dk:(h + 1) * dk], v_new[h]) for h in hh]
        for h in hh:
            s_ref[h] = s[h] * gl_ref[ci:ci + 1, heads + h:heads + h + 1] + ds[h]
            o = wq[h][l:] + o2[h]
            z = qkv_ref[rows, ch + h * dv:ch + (h + 1) * dv]
            o = o * lax.rsqrt(jnp.mean(o * o, axis=-1, keepdims=True) + RMS_EPS) * nw_ref[...]
            o_ref[rows, h * dv:(h + 1) * dv] = o * _silu(z)

    @pl.when(t == pl.num_programs(1) - 1)
    def _():
        s_out_ref[0] = s_ref[...]


def _gdn_prompt(qkvz, pba, s0, gp, nw, *, batch, seq, heads, dk, dv, tb):
    key, val = heads * dk, heads * dv
    ch = 2 * key + val
    tb = min(tb, seq)
    assert seq % tb == 0 and tb % (2 * CHUNK) == 0
    nt = seq // tb
    return pl.pallas_call(
        functools.partial(_gdn_prompt_kernel, tb=tb, heads=heads, dk=dk, dv=dv),
        grid=(batch, nt),
        in_specs=[pl.BlockSpec((tb, ch + val), lambda b, t: (b * nt + t, 0)),
                  pl.BlockSpec((tb, 128), lambda b, t: (b * nt + t, 0)),
                  pl.BlockSpec((1, heads, dk, dv), lambda b, t: (b, 0, 0, 0)),
                  pl.BlockSpec((8, 128), lambda b, t: (0, 0)),
                  pl.BlockSpec((1, dv), lambda b, t: (0, 0))],
        out_specs=[pl.BlockSpec((tb, val), lambda b, t: (b * nt + t, 0)),
                   pl.BlockSpec((1, heads, dk, dv), lambda b, t: (b, 0, 0, 0))],
        out_shape=[jax.ShapeDtypeStruct((batch * seq, val), F32),
                   jax.ShapeDtypeStruct((batch, heads, dk, dv), F32)],
        scratch_shapes=[pltpu.VMEM((heads, dk, dv), F32),
                        pltpu.VMEM((tb, val), F32), pltpu.VMEM((tb, key), BF16),
                        pltpu.VMEM((tb, key), BF16), pltpu.VMEM((tb, key), BF16),
                        pltpu.VMEM((heads, tb, CHUNK), BF16), pltpu.VMEM((max(8, tb // CHUNK), 128), F32)],
        compiler_params=_cparams(("parallel", "arbitrary")),
        name="gdn_prompt",
    )(qkvz, pba, s0, gp, nw)


GDN_PROJ_COLS = 256
GDN_CONV_ROWS = 64


def _gdn_in_proj_kernel(x_ref, g_ref, w_ref, wa_ref, conv0_ref, cw_ref, o_ref, aux_ref, conv_out_ref,
                        xn_ref, carry_ref, *, heads, dk, dv):
    t = pl.program_id(1)
    key = heads * dk
    ch = 2 * key + heads * dv
    tm = x_ref.shape[0]
    n = w_ref.shape[1]

    @pl.when(t == 0)
    def _():
        carry_ref[...] = conv0_ref[0]

    xn_ref[...] = _norm_rows(x_ref[...], g_ref[...]).astype(BF16)
    aux_ref[...] = jnp.dot(xn_ref[...], wa_ref[...], preferred_element_type=F32)
    for c0 in range(0, n, GDN_PROJ_COLS):
        cs = slice(c0, c0 + GDN_PROJ_COLS)
        y = jnp.dot(xn_ref[...], w_ref[:, cs], preferred_element_type=F32)
        if c0 >= ch:
            o_ref[:, cs] = y
            continue
        ext = jnp.concatenate([carry_ref[:, cs], y], axis=0)
        carry_ref[:, cs] = y[tm - 8:, :]
        for r0 in range(0, tm, GDN_CONV_ROWS):
            for j in range(0, GDN_PROJ_COLS, dk):
                cj = slice(c0 + j, c0 + j + dk)
                blk = ext[r0:r0 + GDN_CONV_ROWS + 8, j:j + dk]
                z = blk[8:] * cw_ref[CONV_W - 1:CONV_W, cj]
                for s in range(1, CONV_W):
                    z = z + pltpu.roll(blk, s, 0)[8:] * cw_ref[CONV_W - 1 - s:CONV_W - s, cj]
                z = _silu(z)
                if c0 + j < key:
                    z = z * lax.rsqrt(jnp.sum(z * z, axis=-1, keepdims=True) + 1e-6) * (dk ** -0.5)
                elif c0 + j < 2 * key:
                    z = z * lax.rsqrt(jnp.sum(z * z, axis=-1, keepdims=True) + 1e-6)
                o_ref[r0:r0 + GDN_CONV_ROWS, cj] = z

    @pl.when(t == pl.num_programs(1) - 1)
    def _():
        conv_out_ref[0] = carry_ref[...]


def _gdn_in_proj(x, g, w_main, w_ba, conv0, cw_t, *, batch, seq, heads, dk, dv, tm):
    m, d = x.shape
    key, val = heads * dk, heads * dv
    ch = 2 * key + val
    n = w_main.shape[1]
    assert dk == dv and n % GDN_PROJ_COLS == 0 and ch % GDN_PROJ_COLS == 0 and GDN_PROJ_COLS % dk == 0
    tm = min(tm, seq)
    assert seq % tm == 0 and tm >= 8
    nt = seq // tm
    row = lambda width: pl.BlockSpec((tm, width), lambda b, t: (b * nt + t, 0))
    full = lambda a: pl.BlockSpec(a.shape, lambda b, t: (0,) * a.ndim)
    st = pl.BlockSpec((1, 8, ch), lambda b, t: (b, 0, 0))
    return pl.pallas_call(
        functools.partial(_gdn_in_proj_kernel, heads=heads, dk=dk, dv=dv),
        grid=(batch, nt),
        in_specs=[row(d), full(g), full(w_main), full(w_ba), st, full(cw_t)],
        out_specs=[row(n), row(w_ba.shape[1]), st],
        out_shape=[jax.ShapeDtypeStruct((m, n), F32), jax.ShapeDtypeStruct((m, w_ba.shape[1]), F32),
                   jax.ShapeDtypeStruct((batch, 8, ch), F32)],
        scratch_shapes=[pltpu.VMEM((tm, d), BF16), pltpu.VMEM((8, ch), F32)],
        compiler_params=_cparams(("parallel", "arbitrary")),
        name="gdn_in_proj",
    )(x, g, w_main, w_ba, conv0, cw_t)


def _mlstm_prompt_kernel(pm_ref, pif_ref, bif_ref, c0_ref, n0_ref, m0_ref, nw_ref,
                         o_ref, c_out_ref, n_out_ref, m_out_ref,
                         c_ref, n_ref, m_ref, bc_ref, ni_ref, col_ref, kv_ref, kc_ref, sc_ref,
                         cin_ref, nin_ref, min_ref, *, tb, heads, dk, dv):
    t = pl.program_id(1)
    l = CHUNK
    qk_w = heads * dk
    v_off = 2 * qk_w
    o_off = v_off + heads * dv

    @pl.when(t == 0)
    def _():
        c_ref[...] = c0_ref[0]
        n_ref[...] = n0_ref[0]
        m_ref[...] = m0_ref[0]

    incl, _ = _tri_masks(l)
    tril = incl.astype(F32)
    hh = range(heads)
    nchunk = tb // l
    scale = dk ** -0.5

    crow = [slice(ci * l, (ci + 1) * l) for ci in range(nchunk)]
    gates, bcum, bcum_t, gates_t = [], [], [], []
    for ci in range(nchunk):
        g = pif_ref[crow[ci], :] + bif_ref[...]
        g = GATE_CAP * jnp.tanh(g / GATE_CAP)
        gates.append(g)
        bcum.append(_dot_hi(tril, _log_sigmoid(g)))
        bcum_t.append(bcum[ci].T)
        gates_t.append(g.T)
        bc_ref[crow[ci], :] = bcum[ci]
    cc = [(ci, h) for ci in range(nchunk) for h in hh]
    nc = range(len(cc))
    bi = [bcum[ci][:, heads + h:heads + h + 1] for ci, h in cc]
    b_last = [bcum[ci][l - 1:l, heads + h:heads + h + 1] for ci, h in cc]
    k = [pm_ref[crow[ci], qk_w + h * dk:qk_w + (h + 1) * dk] for ci, h in cc]
    v = [pm_ref[crow[ci], v_off + h * dv:v_off + (h + 1) * dv].astype(BF16) for ci, h in cc]
    qk = [_dot_nt(pm_ref[crow[ci], h * dk:(h + 1) * dk] * scale, k[i]) for i, (ci, h) in enumerate(cc)]
    dlog = [jnp.where(incl, bi[i] - bcum_t[ci][heads + h:heads + h + 1, :] + gates_t[ci][h:h + 1, :], NEG_BIG)
            for i, (ci, h) in enumerate(cc)]
    m_intra = [jnp.max(dlog[i], axis=-1, keepdims=True) for i in nc]
    p = [jnp.where(incl, jnp.exp(dlog[i] - m_intra[i]), 0.0) * qk[i] for i in nc]
    den_intra = [jnp.sum(p[i], axis=-1, keepdims=True) for i in nc]
    num_intra = [_dot(p[i], v[i]) for i in nc]
    a_log = [b_last[i] - bi[i] + gates[ci][:, h:h + 1] for i, (ci, h) in enumerate(cc)]
    m_chunk = [jnp.max(a_log[i], axis=0, keepdims=True) for i in nc]
    kw = [k[i] * jnp.exp(a_log[i] - m_chunk[i]) for i in nc]
    kv_chunk = [_dot_tn(kw[i], v[i]) for i in nc]
    for i, (ci, h) in enumerate(cc):
        ni_ref[crow[ci], h * dv:(h + 1) * dv] = num_intra[i]
        col_ref[crow[ci], h:h + 1] = m_intra[i]
        col_ref[crow[ci], heads + h:heads + h + 1] = den_intra[i]
        kv_ref[i] = kv_chunk[i]
        kc_ref[i:i + 1, :] = jnp.sum(kw[i], axis=0, keepdims=True)
        sc_ref[i:i + 1, 0:1] = m_chunk[i]
        sc_ref[i:i + 1, 1:2] = b_last[i]

    for ci in range(nchunk):
        for h in hh:
            i = ci * heads + h
            c_mat = c_ref[h]
            n_vec = n_ref[h:h + 1, :]
            m_prev = m_ref[h:h + 1, :]
            cin_ref[i] = c_mat
            nin_ref[i:i + 1, :] = n_vec
            min_ref[i:i + 1, :] = m_prev
            m_chunk = sc_ref[i:i + 1, 0:1]
            b_last = sc_ref[i:i + 1, 1:2]
            m_new = jnp.maximum(b_last + m_prev, m_chunk)
            f_s = jnp.exp(b_last + m_prev - m_new)
            i_s = jnp.exp(m_chunk - m_new)
            c_ref[h] = f_s[:, 0:1] * c_mat + i_s[:, 0:1] * kv_ref[i]
            n_ref[h:h + 1, :] = f_s * n_vec + i_s * kc_ref[i:i + 1, :]
            m_ref[h:h + 1, :] = m_new

    q = [pm_ref[crow[ci], h * dk:(h + 1) * dk] * scale for ci, h in cc]
    qc = [_dot(q[i], cin_ref[i]) for i in nc]
    qn = [jnp.sum(q[i] * nin_ref[i:i + 1, :], axis=-1, keepdims=True) for i in nc]
    m_prev = [min_ref[i:i + 1, 0:1] for i in nc]
    bi = [bc_ref[crow[ci], heads + h:heads + h + 1] for ci, h in cc]
    m_in = [col_ref[crow[ci], h:h + 1] for ci, h in cc]
    m_t = [jnp.maximum(bi[i] + m_prev[i], m_in[i]) for i in nc]
    s_inter = [jnp.exp(bi[i] + m_prev[i] - m_t[i]) for i in nc]
    s_intra = [jnp.exp(m_in[i] - m_t[i]) for i in nc]
    den = [s_inter[i] * qn[i] + s_intra[i] * col_ref[crow[ci], heads + h:heads + h + 1]
           for i, (ci, h) in enumerate(cc)]
    h_t = [(s_inter[i] * qc[i] + s_intra[i] * ni_ref[crow[ci], h * dv:(h + 1) * dv])
           / jnp.maximum(jnp.abs(den[i]), jnp.exp(-m_t[i])) for i, (ci, h) in enumerate(cc)]
    ms = [jnp.mean(h_t[i] * h_t[i], axis=-1, keepdims=True) for i in nc]
    for i, (ci, h) in enumerate(cc):
        h_n = h_t[i] * lax.rsqrt(ms[i] + RMS_EPS) * nw_ref[:, h * dv:(h + 1) * dv]
        o_pre = pm_ref[crow[ci], o_off + h * dv:o_off + (h + 1) * dv]
        o_ref[crow[ci], h * dv:(h + 1) * dv] = _sigmoid(o_pre) * h_n

    @pl.when(t == pl.num_programs(1) - 1)
    def _():
        c_out_ref[0] = c_ref[...]
        n_out_ref[0] = n_ref[...]
        m_out_ref[0] = m_ref[...]


def _mlstm_prompt(pm, pif, bif, c0, n0, m0, nw, *, batch, seq, heads, dk, dv, tb):
    width = pm.shape[1]
    val = heads * dv
    tb = min(tb, seq)
    assert seq % tb == 0 and tb % CHUNK == 0
    nt = seq // tb
    nck = (tb // CHUNK) * heads
    return pl.pallas_call(
        functools.partial(_mlstm_prompt_kernel, tb=tb, heads=heads, dk=dk, dv=dv),
        grid=(batch, nt),
        in_specs=[pl.BlockSpec((tb, width), lambda b, t: (b * nt + t, 0)),
                  pl.BlockSpec((tb, 128), lambda b, t: (b * nt + t, 0)),
                  pl.BlockSpec((1, 128), lambda b, t: (0, 0)),
                  pl.BlockSpec((1, heads, dk, dv), lambda b, t: (b, 0, 0, 0)),
                  pl.BlockSpec((1, 8, dk), lambda b, t: (b, 0, 0)),
                  pl.BlockSpec((1, 8, 128), lambda b, t: (b, 0, 0)),
                  pl.BlockSpec((1, val), lambda b, t: (0, 0))],
        out_specs=[pl.BlockSpec((tb, val), lambda b, t: (b * nt + t, 0)),
                   pl.BlockSpec((1, heads, dk, dv), lambda b, t: (b, 0, 0, 0)),
                   pl.BlockSpec((1, 8, dk), lambda b, t: (b, 0, 0)),
                   pl.BlockSpec((1, 8, 128), lambda b, t: (b, 0, 0))],
        out_shape=[jax.ShapeDtypeStruct((batch * seq, val), F32),
                   jax.ShapeDtypeStruct((batch, heads, dk, dv), F32),
                   jax.ShapeDtypeStruct((batch, 8, dk), F32),
                   jax.ShapeDtypeStruct((batch, 8, 128), F32)],
        scratch_shapes=[pltpu.VMEM((heads, dk, dv), F32), pltpu.VMEM((8, dk), F32),
                        pltpu.VMEM((8, 128), F32),
                        pltpu.VMEM((tb, 128), F32), pltpu.VMEM((tb, val), F32), pltpu.VMEM((tb, 128), F32),
                        pltpu.VMEM((nck, dk, dv), F32), pltpu.VMEM((max(8, nck), dk), F32),
                        pltpu.VMEM((max(8, nck), 128), F32),
                        pltpu.VMEM((nck, dk, dv), F32), pltpu.VMEM((max(8, nck), dk), F32),
                        pltpu.VMEM((max(8, nck), 128), F32)],
        compiler_params=_cparams(("parallel", "arbitrary")),
        name="mlstm_prompt",
    )(pm, pif, bif, c0, n0, m0, nw)


def _mlstm_prep(w_in, b_if, norm_w, w_out, *, heads, dk, dv):
    main = 2 * heads * dk + 2 * heads * dv
    return dict(w_main=w_in[:, :main].astype(BF16),
                w_if=_pad_cols(w_in[:, main:], 128).astype(BF16),
                bif=_pad_cols(b_if[None, :], 128), nw=norm_w[None, :], w_out=w_out.astype(BF16))


def _mlstm_prompt_layer(x, g, p, c0, n0, m0, *, batch, seq, heads, dk, dv, norm=True, residual=True):
    pm, pif = _in_proj(x, g, p["w_main"], p["w_if"], norm=norm)
    n0p = jnp.pad(n0, ((0, 0), (0, 8 - heads), (0, 0)))
    m0p = jnp.broadcast_to(jnp.pad(m0, ((0, 0), (0, 8 - heads)))[:, :, None], (batch, 8, 128))
    o, c, n, m = _mlstm_prompt(pm, pif, p["bif"], c0, n0p, m0p, p["nw"],
                               batch=batch, seq=seq, heads=heads, dk=dk, dv=dv, tb=TB_MIX)
    y = _out_proj(o, p["w_out"], x if residual else None)
    return y, c, n[:, :heads, :], m[:, :heads, 0]


def _rwkv_proj_body(h, prev, mu_ref, wrkv_ref, w1_ref, w2_ref, a1_ref, a2_ref, g1_ref, g2_ref,
                    vec_ref, r_ref, k_ref, v_ref, lw_ref, kk_ref, a_ref, g_ref):
    xx = prev - h

    def mix(j):
        return (h + xx * mu_ref[j:j + 1, :]).astype(BF16)

    w0, a0, k_k, k_a = (vec_ref[j:j + 1, :] for j in range(4))
    r_ref[...] = jnp.dot(mix(0), wrkv_ref[0], preferred_element_type=F32)
    lora_w = _dot(jnp.tanh(_dot(mix(1), w1_ref[...])), w2_ref[...])
    w_log = -_softplus(-(w0 + lora_w)) - 0.5
    lw_ref[...] = -jnp.exp(w_log)
    k = jnp.dot(mix(2), wrkv_ref[1], preferred_element_type=F32)
    v_ref[...] = jnp.dot(mix(3), wrkv_ref[2], preferred_element_type=F32)
    a = _sigmoid(a0 + _dot(_dot(mix(4), a1_ref[...]), a2_ref[...]))
    g_ref[...] = _dot(_sigmoid(_dot(mix(5), g1_ref[...])), g2_ref[...])
    kk_ref[...] = k * k_k
    k_ref[...] = k * (1.0 + (a - 1.0) * k_a)
    a_ref[...] = a


def _rwkv_proj_sample_kernel(x_ref, gn_ref, prev_ref, *refs):
    h = _norm_rows(x_ref[...], gn_ref[...])
    hn_ref = refs[-1]
    hn_ref[...] = h
    _rwkv_proj_body(h, prev_ref[...], *refs[:-1])


def _rwkv_proj_prompt_kernel(x_ref, gn_ref, shift0_ref, *refs):
    carry_ref = refs[-1]
    shift_out_ref = refs[-2]
    t = pl.program_id(1)

    @pl.when(t == 0)
    def _():
        carry_ref[...] = shift0_ref[0]

    h = _norm_rows(x_ref[...], gn_ref[...])
    rows = h.shape[0]
    first = lax.broadcasted_iota(jnp.int32, h.shape, 0) == 0
    prev = jnp.where(first, carry_ref[0:1, :], pltpu.roll(h, 1, 0))
    carry_ref[0:1, :] = h[rows - 1:rows, :]
    _rwkv_proj_body(h, prev, *refs[:-2])

    @pl.when(t == pl.num_programs(1) - 1)
    def _():
        shift_out_ref[0] = carry_ref[...]


def _norm_rows(x, g):
    return x * lax.rsqrt(jnp.mean(x * x, axis=-1, keepdims=True) + RMS_EPS) * g


def _rwkv_proj(x, g_norm, shift0, p, *, batch, seq, tm):
    m, d = x.shape
    consts = [p["mu"], p["w_rkv"], p["w1"], p["w2"], p["a1"], p["a2"], p["g1"], p["g2"], p["vec"]]
    if seq == 1:
        row = pl.BlockSpec((m, d), lambda i: (0, 0))
        out = pl.pallas_call(
            _rwkv_proj_sample_kernel,
            grid=(1,),
            in_specs=[row, pl.BlockSpec((1, d), lambda i: (0, 0)), row] + [_const_block(a) for a in consts],
            out_specs=[row] * 8,
            out_shape=[jax.ShapeDtypeStruct((m, d), F32)] * 8,
            compiler_params=_cparams(("arbitrary",)),
            name="rwkv_proj_sample",
        )(x, g_norm, shift0, *consts)
        return out[:7], out[7]
    tm = min(tm, seq)
    assert seq % tm == 0
    nt = seq // tm
    row = pl.BlockSpec((tm, d), lambda b, t: (b * nt + t, 0))
    st = pl.BlockSpec((1, 8, d), lambda b, t: (b, 0, 0))

    def full(a):
        nd = a.ndim
        return pl.BlockSpec(a.shape, lambda b, t: (0,) * nd)

    shift0_p = jnp.pad(shift0[:, None, :], ((0, 0), (0, 7), (0, 0)))
    out = pl.pallas_call(
        _rwkv_proj_prompt_kernel,
        grid=(batch, nt),
        in_specs=[row, full(g_norm), st] + [full(a) for a in consts],
        out_specs=[row] * 7 + [st],
        out_shape=[jax.ShapeDtypeStruct((m, d), F32)] * 7 + [jax.ShapeDtypeStruct((batch, 8, d), F32)],
        scratch_shapes=[pltpu.VMEM((8, d), F32)],
        compiler_params=_cparams(("parallel", "arbitrary")),
        name="rwkv_proj_prompt",
    )(x, g_norm, shift0_p, *consts)
    return out[:7], out[7][:, 0, :]


def _rwkv_prompt_kernel(r_ref, k_ref, v_ref, lw_ref, kk_ref, a_ref, g_ref, s0_ref, hp_ref,
                        o_ref, s_out_ref, s_ref, rr_ref, yy_ref, mx_ref, n0_ref, gw_ref, gb_ref, el_ref,
                        *, tb, heads, hd):
    t = pl.program_id(1)
    l = CHUNK

    @pl.when(t == 0)
    def _():
        s_ref[...] = s0_ref[0]

    hh = range(heads)
    hs = [slice(h * hd, (h + 1) * hd) for h in hh]
    l2 = 2 * l
    r2 = lax.broadcasted_iota(jnp.int32, (l2, l2), 0)
    c2 = lax.broadcasted_iota(jnp.int32, (l2, l2), 1)
    same = (r2 >= l) == (c2 >= l)
    incl = same & (r2 >= c2)
    strict = same & (r2 > c2)
    tril = incl.astype(F32)
    first = lax.broadcasted_iota(jnp.int32, (l2, heads * hd), 0) < l
    first2 = (lax.broadcasted_iota(jnp.int32, (2 * l2, hd), 0) & l) == 0
    zeros = jnp.zeros((l2, hd), F32)

    def pair(pi, carry):
        rows = pl.ds(pl.multiple_of(pi * l2, l2), l2)
        lw = lw_ref[rows, :]
        lwc = _dot_hi(tril, lw)
        lw_end = jnp.where(first, lwc[l - 1:l, :], lwc[l2 - 1:l2, :])
        e_in = jnp.exp(lwc)
        e_prev = jnp.exp(lwc - lw)
        e_neg = jnp.exp(-lwc)
        e_end = jnp.exp(lw_end - lwc)
        r = [r_ref[rows, hs[h]] for h in hh]
        v = [v_ref[rows, hs[h]] for h in hh]
        k = [k_ref[rows, hs[h]] for h in hh]
        kk = [kk_ref[rows, hs[h]] for h in hh]
        kk = [kk[h] * lax.rsqrt(jnp.maximum(jnp.sum(kk[h] * kk[h], axis=-1, keepdims=True), 1e-24)) for h in hh]
        bv = [kk[h] * a_ref[rows, hs[h]] for h in hh]
        a_t = [-kk[h] * e_prev[:, hs[h]] for h in hh]
        r_t = [r[h] * e_in[:, hs[h]] for h in hh]
        gm = [_dot_nt(jnp.concatenate([a_t[h], r_t[h]], axis=0),
                      jnp.concatenate([bv[h] * e_neg[:, hs[h]], k[h] * e_neg[:, hs[h]]], axis=0)) for h in hh]
        ak_m = [jnp.where(strict, gm[h][:l2, l2:], 0.0).astype(BF16) for h in hh]
        rbk_m = [jnp.concatenate([jnp.where(incl, gm[h][l2:, :l2], 0.0),
                                  jnp.where(incl, gm[h][l2:, l2:], 0.0)], axis=1).astype(BF16) for h in hh]
        t_inv = _unit_lower_inverse([jnp.where(strict, -gm[h][:l2, :l2], 0.0) for h in hh], l)
        vb = [v[h].astype(BF16) for h in hh]
        akv = [_dot(ak_m[h], vb[h]) for h in hh]
        x1 = [_dot(t_inv[h], jnp.concatenate([a_t[h], akv[h]], axis=1)).astype(BF16) for h in hh]
        low = [jnp.concatenate([x1[h], jnp.concatenate([zeros.astype(BF16), vb[h]], axis=1)], axis=0) for h in hh]
        x2 = [_dot(rbk_m[h], low[h]) for h in hh]
        bk = [jnp.concatenate([bv[h] * e_end[:, hs[h]], k[h] * e_end[:, hs[h]]], axis=0) for h in hh]
        bk2 = [jnp.concatenate([jnp.where(first2, bk[h], 0.0), jnp.where(first2, 0.0, bk[h])], axis=1) for h in hh]
        mn = [_dot_tn(low[h], bk2[h]) for h in hh]
        for c in range(2):
            crow = pl.ds(pl.multiple_of(pi * l2 + c * l, l), l)
            for h in hh:
                mx_ref[h, crow, :] = mn[h][:hd, c * hd:(c + 1) * hd].astype(BF16)
                n0_ref[h, crow, :] = mn[h][hd:, c * hd:(c + 1) * hd]
        for h in hh:
            rr_ref[h, rows, :] = (r_t[h] + x2[h][:, :hd]).astype(BF16)
            yy_ref[h, rows, :] = x2[h][:, hd:]
            g = g_ref[rows, hs[h]]
            bonus = jnp.sum(r[h] * k[h] * hp_ref[0:1, hs[h]], axis=-1, keepdims=True) * v[h]
            gw_ref[h, rows, :] = hp_ref[1:2, hs[h]] * g
            gb_ref[h, rows, :] = (hp_ref[2:3, hs[h]] + bonus) * g
            el_ref[h, pl.ds(pi * 2, 1), :] = jnp.exp(lwc[l - 1:l, hs[h]])
            el_ref[h, pl.ds(pi * 2 + 1, 1), :] = jnp.exp(lwc[l2 - 1:l2, hs[h]])
        return carry

    lax.fori_loop(0, tb // l2, pair, 0)

    for ci in range(tb // l):
        rows = slice(ci * l, (ci + 1) * l)
        s = [s_ref[h] for h in hh]
        y = [_dot_nt(rr_ref[h, rows, :], s[h]) for h in hh]
        sm = [_dot(s[h], mx_ref[h, rows, :]) for h in hh]
        for h in hh:
            s_ref[h] = s[h] * el_ref[h, ci:ci + 1, :] + sm[h] + n0_ref[h, rows, :]
        y = [y[h] + yy_ref[h, rows, :] for h in hh]
        mean = [jnp.mean(y[h], axis=-1, keepdims=True) for h in hh]
        yc = [y[h] - mean[h] for h in hh]
        var = [jnp.mean(yc[h] * yc[h], axis=-1, keepdims=True) for h in hh]
        for h in hh:
            o_ref[rows, hs[h]] = yc[h] * lax.rsqrt(var[h] + RW_GN_EPS) * gw_ref[h, rows, :] + gb_ref[h, rows, :]

    @pl.when(t == pl.num_programs(1) - 1)
    def _():
        s_out_ref[0] = s_ref[...]


def _rwkv_prompt(r, k, v, lw, kk, a, g, s0, hp, *, batch, seq, heads, hd, tb):
    d = heads * hd
    tb = min(tb, seq)
    assert seq % tb == 0 and tb % CHUNK == 0
    nt = seq // tb
    row = pl.BlockSpec((tb, d), lambda b, t: (b * nt + t, 0))
    st = pl.BlockSpec((1, heads, hd, hd), lambda b, t: (b, 0, 0, 0))
    return pl.pallas_call(
        functools.partial(_rwkv_prompt_kernel, tb=tb, heads=heads, hd=hd),
        grid=(batch, nt),
        in_specs=[row] * 7 + [st, pl.BlockSpec((8, d), lambda b, t: (0, 0))],
        out_specs=[row, st],
        out_shape=[jax.ShapeDtypeStruct((batch * seq, d), F32),
                   jax.ShapeDtypeStruct((batch, heads, hd, hd), F32)],
        scratch_shapes=[pltpu.VMEM((heads, hd, hd), F32),
                        pltpu.VMEM((heads, tb, hd), BF16), pltpu.VMEM((heads, tb, hd), F32),
                        pltpu.VMEM((heads, tb, hd), BF16), pltpu.VMEM((heads, tb, hd), F32),
                        pltpu.VMEM((heads, tb, hd), F32), pltpu.VMEM((heads, tb, hd), F32),
                        pltpu.VMEM((heads, max(8, tb // CHUNK), hd), F32)],
        compiler_params=_cparams(("parallel", "arbitrary")),
        name="rwkv_prompt",
    )(r, k, v, lw, kk, a, g, s0, hp)


def _pad_rows(a, n):
    return jnp.pad(a, ((0, n - a.shape[0]), (0, 0)))


def _rwkv_prep(mu, w_rkv, w_o, w0, w1, w2, a0, a1, a2, g1, g2, k_k, k_a, r_k, lnx_w, lnx_b):
    d = w0.shape[0]
    lw = -(-w1.shape[1] // 128) * 128
    la = -(-a1.shape[1] // 128) * 128
    lg = -(-g1.shape[1] // 128) * 128
    return dict(mu=_pad_rows(mu, 8), w_rkv=w_rkv.astype(BF16), w_o=w_o.astype(BF16),
                w1=_pad_cols(w1, lw).astype(BF16), w2=_pad_rows(w2, lw).astype(BF16),
                a1=_pad_cols(a1, la).astype(BF16), a2=_pad_rows(a2, la).astype(BF16),
                g1=_pad_cols(g1, lg).astype(BF16), g2=_pad_rows(g2, lg).astype(BF16),
                vec=_pad_rows(jnp.stack([w0, a0, k_k, k_a]), 8),
                hp=_pad_rows(jnp.stack([r_k.reshape(d), lnx_w, lnx_b]), 8))


def _rwkv_prompt_layer(x, g_norm, p, shift0, s0, *, batch, seq, heads, hd, residual=True):
    (r, k, v, lw, kk, a, g), shift = _rwkv_proj(x, g_norm, shift0, p, batch=batch, seq=seq, tm=TM_RWKV_PROJ)
    o, s = _rwkv_prompt(r, k, v, lw, kk, a, g, s0, p["hp"], batch=batch, seq=seq, heads=heads, hd=hd, tb=TB_MIX)
    y = _out_proj(o, p["w_o"], x if residual else None)
    return y, shift, s


def _gdn_sample_pre_kernel(pm_ref, pba_ref, conv_ref, cw_ref, gp_ref, qkv_ref, conv_out_ref, sc_ref,
                           *, heads, dk, dv):
    key = heads * dk
    ch = 2 * key + heads * dv
    u = pm_ref[:, 0:ch]
    y = u * cw_ref[CONV_W - 1:CONV_W, :]
    for j in range(CONV_W - 1):
        y = y + conv_ref[j] * cw_ref[j:j + 1, :]
        conv_out_ref[j] = conv_ref[j + 1] if j + 1 < CONV_W - 1 else u
    y = _silu(y)
    for c in range(ch // 128):
        cs = slice(c * 128, (c + 1) * 128)
        yc = y[:, cs]
        if c * 128 < key:
            yc = yc * lax.rsqrt(jnp.sum(yc * yc, axis=-1, keepdims=True) + 1e-6) * (dk ** -0.5)
        elif c * 128 < 2 * key:
            yc = yc * lax.rsqrt(jnp.sum(yc * yc, axis=-1, keepdims=True) + 1e-6)
        qkv_ref[:, cs] = yc
    ba = pba_ref[...]
    lane = lax.broadcasted_iota(jnp.int32, ba.shape, 1)
    g = -jnp.exp(gp_ref[0:1, :]) * _softplus(ba + gp_ref[1:2, :])
    sc_ref[...] = jnp.where(lane < heads, _sigmoid(ba), jnp.exp(g))


def _gdn_sample_pre(pm, pba, conv_t, cw_t, gp, *, heads, dk, dv):
    n = pm.shape[0]
    ch = 2 * heads * dk + heads * dv
    return pl.pallas_call(
        functools.partial(_gdn_sample_pre_kernel, heads=heads, dk=dk, dv=dv),
        out_shape=[jax.ShapeDtypeStruct((n, ch), F32),
                   jax.ShapeDtypeStruct((CONV_W - 1, n, ch), F32),
                   jax.ShapeDtypeStruct((n, 128), F32)],
        compiler_params=pltpu.CompilerParams(vmem_limit_bytes=V7X_VMEM_LIMIT),
        name="gdn_sample_pre",
    )(pm, pba, conv_t, cw_t, gp)


SEQ_PER_STEP = 4


def _seq_block(a, nb):
    nd = a.ndim
    return pl.BlockSpec((nb,) + a.shape[1:], lambda b: (b,) + (0,) * (nd - 1))


def _const_block(a):
    nd = a.ndim
    return pl.BlockSpec(a.shape, lambda b: (0,) * nd)


def _gdn_sample_step_kernel(s0_ref, cols_ref, v_ref, z_ref, sc_ref, nw_ref, acc_ref, s_out_ref, o_ref,
                            *, heads, nb):
    del acc_ref
    hh = range(heads)
    for i in range(nb):
        kc = [cols_ref[i, :, h:h + 1] for h in hh]
        qc = [cols_ref[i, :, heads + h:heads + h + 1] for h in hh]
        s0 = [s0_ref[i, h] for h in hh]
        eg = [sc_ref[i, h:h + 1, 1:2] for h in hh]
        ks = [jnp.sum(kc[h] * s0[h], axis=0, keepdims=True) for h in hh]
        s1 = [eg[h] * s0[h] + kc[h] * (sc_ref[i, h:h + 1, 0:1] * (v_ref[i, h:h + 1, :] - eg[h] * ks[h]))
              for h in hh]
        o = [jnp.sum(qc[h] * s1[h], axis=0, keepdims=True) for h in hh]
        ms = [jnp.mean(o[h] * o[h], axis=-1, keepdims=True) for h in hh]
        for h in hh:
            s_out_ref[i, h] = s1[h]
            o_ref[i, h:h + 1, :] = o[h] * lax.rsqrt(ms[h] + RMS_EPS) * nw_ref[...] * _silu(z_ref[i, h:h + 1, :])


def _gdn_sample_step(s_all, layer, s_acc, cols, v, z, sc, nw):
    _, n, heads, dk, dv = s_all.shape
    nb = SEQ_PER_STEP
    assert n % nb == 0
    state = pl.BlockSpec((None, nb, heads, dk, dv), lambda b: (layer, b, 0, 0, 0))
    return pl.pallas_call(
        functools.partial(_gdn_sample_step_kernel, heads=heads, nb=nb),
        grid=(n // nb,),
        in_specs=[state, _seq_block(cols, nb), _seq_block(v, nb), _seq_block(z, nb), _seq_block(sc, nb),
                  pl.BlockSpec((1, dv), lambda b: (0, 0)), pl.BlockSpec(memory_space=pl.ANY)],
        out_specs=[state, _seq_block(v, nb)],
        out_shape=[jax.ShapeDtypeStruct(s_all.shape, F32), jax.ShapeDtypeStruct(v.shape, F32)],
        input_output_aliases={6: 0},
        compiler_params=_cparams(("parallel",)),
        name="gdn_sample_step",
    )(s_all, cols, v, z, sc, nw, s_acc)


def _gdn_sample_layer(x, g, p, conv0, s_all, layer, s_acc, *, heads, dk, dv):
    n = x.shape[0]
    key, val = heads * dk, heads * dv
    ch = 2 * key + val
    pm, pba = _in_proj(x, g, p["w_main"], p["w_ba"])
    qkv, conv_t, sc = _gdn_sample_pre(pm, pba, jnp.transpose(conv0, (1, 0, 2)), p["cw_t"], p["gp"],
                                      heads=heads, dk=dk, dv=dv)
    q_c = jnp.transpose(qkv[:, :key].reshape(n, heads, dk), (0, 2, 1))
    k_c = jnp.transpose(qkv[:, key:2 * key].reshape(n, heads, dk), (0, 2, 1))
    cols = jnp.concatenate([k_c, q_c], axis=-1)
    sc3 = jnp.stack([sc[:, :heads], sc[:, heads:2 * heads]], axis=-1)
    s_acc, o = _gdn_sample_step(s_all, layer, s_acc, cols, qkv[:, 2 * key:].reshape(n, heads, dv),
                                pm[:, ch:].reshape(n, heads, dv), sc3, p["nw"])
    y = _out_proj(o.reshape(n, val), p["w_out"], x)
    return y, jnp.transpose(conv_t, (1, 0, 2)), s_acc


def _mlstm_sample_step_kernel(c0_ref, n0_ref, cols_ref, q_ref, k_ref, v_ref, op_ref, sc_ref, bif_ref, nw_ref,
                              c_out_ref, n_out_ref, m_out_ref, o_ref, *, heads, dk, nb):
    scale = dk ** -0.5
    for i in range(nb):
        gi = sc_ref[i, :, 0:1] + bif_ref[:, 0:1]
        gf = sc_ref[i, :, 1:2] + bif_ref[:, 1:2]
        m0 = sc_ref[i, :, 2:3]
        gi = GATE_CAP * jnp.tanh(gi / GATE_CAP)
        logf = _log_sigmoid(GATE_CAP * jnp.tanh(gf / GATE_CAP))
        m_new = jnp.maximum(logf + m0, gi)
        f_s = jnp.exp(logf + m0 - m_new)
        i_s = jnp.exp(gi - m_new)
        m_out_ref[i] = m_new
        n1 = f_s * n0_ref[i] + i_s * k_ref[i]
        n_out_ref[i] = n1
        den = jnp.sum(q_ref[i] * scale * n1, axis=-1, keepdims=True)
        floor = jnp.exp(-m_new)
        hh = range(heads)
        kc = [cols_ref[i, :, h:h + 1] for h in hh]
        qc = [cols_ref[i, :, heads + h:heads + h + 1] * scale for h in hh]
        c1 = [f_s[h:h + 1, :] * c0_ref[i, h] + i_s[h:h + 1, :] * (kc[h] * v_ref[i, h:h + 1, :]) for h in hh]
        num = [jnp.sum(qc[h] * c1[h], axis=0, keepdims=True) for h in hh]
        h_t = [num[h] / jnp.maximum(jnp.abs(den[h:h + 1, :]), floor[h:h + 1, :]) for h in hh]
        ms = [jnp.mean(h_t[h] * h_t[h], axis=-1, keepdims=True) for h in hh]
        for h in hh:
            c_out_ref[i, h] = c1[h]
            h_n = h_t[h] * lax.rsqrt(ms[h] + RMS_EPS) * nw_ref[h:h + 1, :]
            o_ref[i, h:h + 1, :] = _sigmoid(op_ref[i, h:h + 1, :]) * h_n


def _mlstm_sample_step(c0, n0, cols, q, k, v, o_pre, sc, bif2, nw2):
    n, heads, dk, dv = c0.shape
    nb = SEQ_PER_STEP
    assert n % nb == 0
    full = _const_block

    def blk(a):
        return _seq_block(a, nb)

    m_shape = (n, heads, 1)
    return pl.pallas_call(
        functools.partial(_mlstm_sample_step_kernel, heads=heads, dk=dk, nb=nb),
        grid=(n // nb,),
        in_specs=[blk(c0), blk(n0), blk(cols), blk(q), blk(k), blk(v), blk(o_pre), blk(sc), full(bif2), full(nw2)],
        out_specs=[blk(c0), blk(n0), pl.BlockSpec((nb, heads, 1), lambda b: (b, 0, 0)), blk(v)],
        out_shape=[jax.ShapeDtypeStruct(c0.shape, F32), jax.ShapeDtypeStruct(n0.shape, F32),
                   jax.ShapeDtypeStruct(m_shape, F32), jax.ShapeDtypeStruct(v.shape, F32)],
        compiler_params=_cparams(("parallel",)),
        name="mlstm_sample_step",
    )(c0, n0, cols, q, k, v, o_pre, sc, bif2, nw2)


def _mlstm_sample_layer(x, g, p, c0, n0, m0, *, heads, dk, dv):
    n = x.shape[0]
    qk_w, val = heads * dk, heads * dv
    pm, pif = _in_proj(x, g, p["w_main"], p["w_if"])
    q = pm[:, :qk_w].reshape(n, heads, dk)
    k = pm[:, qk_w:2 * qk_w].reshape(n, heads, dk)
    v = pm[:, 2 * qk_w:2 * qk_w + val].reshape(n, heads, dv)
    o_pre = pm[:, 2 * qk_w + val:].reshape(n, heads, dv)
    cols = jnp.concatenate([jnp.transpose(k, (0, 2, 1)), jnp.transpose(q, (0, 2, 1))], axis=-1)
    sc = jnp.stack([pif[:, :heads], pif[:, heads:2 * heads], m0], axis=-1)
    bif2 = jnp.stack([p["bif"][0, :heads], p["bif"][0, heads:2 * heads]], axis=-1)
    c1, n1, m1, o = _mlstm_sample_step(c0, n0, cols, q, k, v, o_pre, sc, bif2, p["nw"].reshape(heads, dv))
    y = _out_proj(o.reshape(n, val), p["w_out"], x)
    return y, c1, n1, m1[:, :, 0]


RW_ROW_GROUP = 8


def _rwkv_sample_step_kernel(s_ref, r_ref, k_ref, lw_ref, kk_ref, a_ref, v_ref, g_ref, hp_ref,
                             s_out_ref, o_ref, y_ref, *, hd):
    kk = kk_ref[0]
    kk = kk * lax.rsqrt(jnp.maximum(jnp.sum(kk * kk, axis=0, keepdims=True), 1e-24))
    av = -kk
    bv = kk * a_ref[0]
    w = jnp.exp(lw_ref[0])
    r = r_ref[0]
    k = k_ref[0]
    v = v_ref[0]
    for v0 in range(0, hd, RW_ROW_GROUP):
        vv = range(v0, v0 + RW_ROW_GROUP)
        s0 = [s_ref[0, i] for i in vv]
        sa = [jnp.sum(s * av, axis=0, keepdims=True) for s in s0]
        s1 = [s * w + sa_i * bv + v[i:i + 1, :] * k for s, sa_i, i in zip(s0, sa, vv)]
        y = [jnp.sum(s * r, axis=0, keepdims=True) for s in s1]
        for i, s, y_i in zip(vv, s1, y):
            s_out_ref[0, i] = s
            y_ref[i:i + 1, :] = y_i
    y = y_ref[...]
    yc = y - jnp.mean(y, axis=0, keepdims=True)
    var = jnp.mean(yc * yc, axis=0, keepdims=True)
    hp = hp_ref[0]
    yn = yc * lax.rsqrt(var + RW_GN_EPS) * hp[:, 1:2] + hp[:, 2:3]
    bonus = jnp.sum(r * k * hp[:, 0:1], axis=0, keepdims=True)
    o_ref[0] = (yn + bonus * v) * g_ref[0]


def _rwkv_sample_step(s_t, r, k, lw, kk, a, v, g, hp3):
    heads, hd, _, n = s_t.shape

    def blk(z):
        nd = z.ndim
        return pl.BlockSpec((1,) + z.shape[1:], lambda h: (h,) + (0,) * (nd - 1))

    return pl.pallas_call(
        functools.partial(_rwkv_sample_step_kernel, hd=hd),
        grid=(heads,),
        in_specs=[blk(s_t)] + [blk(z) for z in (r, k, lw, kk, a, v, g, hp3)],
        out_specs=[blk(s_t), blk(v)],
        out_shape=[jax.ShapeDtypeStruct(s_t.shape, F32), jax.ShapeDtypeStruct(v.shape, F32)],
        scratch_shapes=[pltpu.VMEM((hd, n), F32)],
        compiler_params=_cparams(("parallel",)),
        name="rwkv_sample_step",
    )(s_t, r, k, lw, kk, a, v, g, hp3)


def _rwkv_sample_layer(x, g_norm, p, shift0, s0, *, heads, hd):
    n, d = x.shape
    (r, k, v, lw, kk, a, g), hn = _rwkv_proj(x, g_norm, shift0, p, batch=n, seq=1, tm=n)

    def lanes(z):
        return z.T.reshape(heads, hd, n)

    hp = p["hp"]
    hp3 = jnp.stack([hp[j].reshape(heads, hd) for j in range(3)], axis=-1)
    s1_t, o_t = _rwkv_sample_step(jnp.transpose(s0, (1, 2, 3, 0)), lanes(r), lanes(k), lanes(lw), lanes(kk),
                                  lanes(a), lanes(v), lanes(g), hp3)
    y = _out_proj(o_t.reshape(d, n).T, p["w_o"], x)
    return y, hn, jnp.transpose(s1_t, (3, 0, 1, 2))


def _pad_cols(a, n):
    return jnp.pad(a, ((0, 0), (0, n - a.shape[1])))


def _gdn_prep(w_in, conv_w, a_log, dt_bias, norm_w, w_out, *, heads, dk, dv):
    key, val = heads * dk, heads * dv
    ch = 2 * key + val
    main = ch + val
    gp = jnp.zeros((8, 128), F32)
    gp = gp.at[0, heads:2 * heads].set(a_log).at[1, heads:2 * heads].set(dt_bias)
    return dict(w_main=w_in[:, :main].astype(BF16),
                w_ba=_pad_cols(w_in[:, main:], 128).astype(BF16),
                cw_t=jnp.pad(conv_w.T, ((0, 8 - CONV_W), (0, 0))),
                gp=gp, nw=norm_w[None, :], w_out=w_out.astype(BF16))


def _gdn_prompt_layer(x, g, p, conv0, s0, *, batch, seq, heads, dk, dv, norm=True, residual=True):
    conv0 = jnp.pad(conv0, ((0, 0), (8 - (CONV_W - 1), 0), (0, 0)))
    qkvz, pba, conv = _gdn_in_proj(x, g, p["w_main"], p["w_ba"], conv0, p["cw_t"], batch=batch, seq=seq,
                                   heads=heads, dk=dk, dv=dv, tm=TM_GDN_PROJ)
    o, s = _gdn_prompt(qkvz, pba, s0, p["gp"], p["nw"], batch=batch, seq=seq, heads=heads, dk=dk, dv=dv, tb=TB_MIX)
    y = _out_proj(o, p["w_out"], x if residual else None)
    return y, conv[:, 8 - (CONV_W - 1):, :], s


def _trunk(x, states, w, *, batch, seq):
    conv_in, gs_in, c_in, n_in, m_in, shift_in, rs_in = states
    depth = w["norm_mix"].shape[0]
    gh, gdk, gdv = gs_in.shape[2:]
    mh, mdk, mdv = c_in.shape[2:]
    rh, rhd = rs_in.shape[2:4]
    prompt = seq > 1
    outs = [[] for _ in range(7)]
    gs_acc = None if prompt else jnp.zeros_like(gs_in)
    for i in range(depth):
        j = i // 3
        g = w["norm_mix"][i][None, :]
        if i % 3 == 0:
            p = w["gdn"][j]
            if prompt:
                x, cb, s = _gdn_prompt_layer(x, g, p, conv_in[j], gs_in[j], batch=batch, seq=seq,
                                             heads=gh, dk=gdk, dv=gdv)
                outs[1].append(s)
            else:
                x, cb, gs_acc = _gdn_sample_layer(x, g, p, conv_in[j], gs_in, j, gs_acc, heads=gh, dk=gdk, dv=gdv)
            outs[0].append(cb)
        elif i % 3 == 1:
            p = w["ml"][j]
            if prompt:
                x, c, n, m = _mlstm_prompt_layer(x, g, p, c_in[j], n_in[j], m_in[j], batch=batch, seq=seq,
                                                 heads=mh, dk=mdk, dv=mdv)
            else:
                x, c, n, m = _mlstm_sample_layer(x, g, p, c_in[j], n_in[j], m_in[j], heads=mh, dk=mdk, dv=mdv)
            outs[2].append(c)
            outs[3].append(n)
            outs[4].append(m)
        else:
            p = w["rw"][j]
            if prompt:
                x, sh, s = _rwkv_prompt_layer(x, g, p, shift_in[j], rs_in[j], batch=batch, seq=seq,
                                              heads=rh, hd=rhd)
            else:
                x, sh, s = _rwkv_sample_layer(x, g, p, shift_in[j], rs_in[j], heads=rh, hd=rhd)
            outs[5].append(sh)
            outs[6].append(s)
        g_out = w["norm_final"][None, :] if i == depth - 1 else None
        x = _ffn(x, w["norm_ffn"][i][None, :], w["ffn_w1"][i], w["ffn_w2"][i], g_out, tm=TM_FFN, tf=TF_FFN)
    y = x
    new =[jnp.stack(z, axis=0) if z else None for z in outs]
    if not prompt:
        new[1] = gs_acc
    return y, tuple(new)


def kernel(x_prompt, x_sample, state_gdn_conv, state_gdn_S, state_mlstm_C, state_mlstm_n, state_mlstm_m, state_rwkv_shift, state_rwkv_S, norm_mix, norm_ffn, norm_final, gdn_w_in, gdn_conv_w, gdn_a_log, gdn_dt_bias, gdn_norm_w, gdn_w_out, ml_w_in, ml_b_if, ml_norm_w, ml_w_out, rw_mu, rw_w_rkv, rw_w_o, rw_w0, rw_w1, rw_w2, rw_a0, rw_a1, rw_a2, rw_g1, rw_g2, rw_k_k, rw_k_a, rw_r_k, rw_lnx_w, rw_lnx_b, ffn_w1, ffn_w2):
    gh, gdk, gdv = state_gdn_S.shape[2:]
    mh, mdk, mdv = state_mlstm_C.shape[2:]
    w = dict(
        norm_mix=norm_mix, norm_ffn=norm_ffn, norm_final=norm_final,
        ffn_w1=[ffn_w1[i].astype(BF16) for i in range(ffn_w1.shape[0])],
        ffn_w2=[ffn_w2[i].astype(BF16) for i in range(ffn_w2.shape[0])],
        gdn=[_gdn_prep(gdn_w_in[j], gdn_conv_w[j], gdn_a_log[j], gdn_dt_bias[j], gdn_norm_w[j], gdn_w_out[j],
                       heads=gh, dk=gdk, dv=gdv) for j in range(gdn_w_in.shape[0])],
        ml=[_mlstm_prep(ml_w_in[j], ml_b_if[j], ml_norm_w[j], ml_w_out[j], heads=mh, dk=mdk, dv=mdv)
            for j in range(ml_w_in.shape[0])],
        rw=[_rwkv_prep(rw_mu[j], rw_w_rkv[j], rw_w_o[j], rw_w0[j], rw_w1[j], rw_w2[j], rw_a0[j], rw_a1[j],
                       rw_a2[j], rw_g1[j], rw_g2[j], rw_k_k[j], rw_k_a[j], rw_r_k[j], rw_lnx_w[j], rw_lnx_b[j])
            for j in range(rw_mu.shape[0])])
    sample_states = (state_gdn_conv, state_gdn_S, state_mlstm_C, state_mlstm_n, state_mlstm_m,
                     state_rwkv_shift, state_rwkv_S)
    bp, tp, d = x_prompt.shape
    bs, ts, _ = x_sample.shape
    assert ts == 1
    prompt_states = tuple(jnp.zeros((s.shape[0], bp) + s.shape[2:], s.dtype) for s in sample_states)
    y_p, new_p = _trunk(x_prompt.reshape(bp * tp, d), prompt_states, w, batch=bp, seq=tp)
    y_s, new_s = _trunk(x_sample.reshape(bs * ts, d), sample_states, w, batch=bs, seq=ts)
    out = [y_p.reshape(bp, tp, d), y_s.reshape(bs, ts, d)]
    for a, b in zip(new_p, new_s):
        out += [a, b]
    return tuple(out)
```

```python
import functools
import math

import jax
import jax.numpy as jnp
from jax import lax
from jax.experimental import pallas as pl
from jax.experimental.pallas import tpu as pltpu

F32 = jnp.float32
BF16 = jnp.bfloat16

RMS_EPS = 1e-6
NEG_BIG = -1e30
GATE_CAP = 15.0
RW_GN_EPS = 64e-5
CONV_W = 4
CHUNK = 64
RW_GROUP = 8
V7X_VMEM_LIMIT = 56 * 1024 * 1024
HI = lax.Precision.HIGHEST


def _cparams(sem):
    return pltpu.CompilerParams(dimension_semantics=sem, vmem_limit_bytes=V7X_VMEM_LIMIT)


def _dot(a, b):
    return jnp.dot(a.astype(BF16), b.astype(BF16), preferred_element_type=F32)


def _dot_nt(a, b):
    return lax.dot_general(a.astype(BF16), b.astype(BF16), (((1,), (1,)), ((), ())),
                           preferred_element_type=F32)


def _dot_tn(a, b):
    return lax.dot_general(a.astype(BF16), b.astype(BF16), (((0,), (0,)), ((), ())),
                           preferred_element_type=F32)


def _dot_hi(a, b):
    return jnp.dot(a, b, preferred_element_type=F32, precision=HI)


def _sigmoid(x):
    return 1.0 / (1.0 + jnp.exp(-x))


def _silu(x):
    return x * _sigmoid(x)


def _softplus(x):
    return jnp.maximum(x, 0.0) + jnp.log(1.0 + jnp.exp(-jnp.abs(x)))


def _log_sigmoid(x):
    return -_softplus(-x)


def _tri_masks(l):
    r = lax.broadcasted_iota(jnp.int32, (l, l), 0)
    c = lax.broadcasted_iota(jnp.int32, (l, l), 1)
    return r >= c, r > c


INV_BASE = 16


def _unit_lower_inverse(mats, l):
    n = mats[0].shape[0]
    r = lax.broadcasted_iota(jnp.int32, (n, n), 0)
    c = lax.broadcasted_iota(jnp.int32, (n, n), 1)
    eye = (r == c).astype(F32)
    size = min(INV_BASE, l)
    shift = size.bit_length() - 1
    diag = (r >> shift) == (c >> shift)
    merges = []
    s = size
    while s < l:
        sh = s.bit_length() - 1
        off = ((r >> (sh + 1)) == (c >> (sh + 1))) & ((r >> sh) > (c >> sh))
        merges.append([jnp.where(off, a, 0.0).astype(BF16) for a in mats])
        s *= 2
    t = [eye - jnp.where(diag, a, 0.0) for a in mats]
    tb = [ti.astype(BF16) for ti in t]
    ab = [jnp.where(diag, a, 0.0).astype(BF16) for a in mats]
    p = [_dot(a, a).astype(BF16) for a in ab]
    k = 2
    while k < size:
        if 2 * k < size:
            both = [_dot(pi, jnp.concatenate([ti, pi], axis=1)) for ti, pi in zip(tb, p)]
            t = [ti + bi[:, :n] for ti, bi in zip(t, both)]
            p = [bi[:, n:].astype(BF16) for bi in both]
        else:
            t = [ti + _dot(pi, ti_b) for ti, ti_b, pi in zip(t, tb, p)]
        tb = [ti.astype(BF16) for ti in t]
        k *= 2
    for a_off in merges:
        x = [_dot(a, ti) for a, ti in zip(a_off, tb)]
        y = [_dot(ti, xi) for ti, xi in zip(tb, x)]
        t = [ti - yi for ti, yi in zip(t, y)]
        tb = [ti.astype(BF16) for ti in t]
    return tb


def _nmm_kernel(*refs, norm, residual, aux):
    x_ref, g_ref, w_ref = refs[:3]
    pos = 3
    res_ref = aux_w_ref = aux_o_ref = None
    if residual:
        res_ref = refs[pos]
        pos += 1
    if aux:
        aux_w_ref = refs[pos]
        pos += 1
    o_ref = refs[pos]
    if aux:
        aux_o_ref = refs[pos + 1]
    xn_ref = refs[-1]

    @pl.when(pl.program_id(1) == 0)
    def _():
        x = x_ref[...]
        if norm:
            x = x * lax.rsqrt(jnp.mean(x * x, axis=-1, keepdims=True) + RMS_EPS) * g_ref[...]
        xn_ref[...] = x.astype(BF16)
        if aux:
            aux_o_ref[...] = jnp.dot(xn_ref[...], aux_w_ref[...], preferred_element_type=F32)

    y = jnp.dot(xn_ref[...], w_ref[...], preferred_element_type=F32)
    if residual:
        y = res_ref[...] + y
    o_ref[...] = y


def _nmm(x, g, w, res=None, w_aux=None, *, norm, tm, tn):
    m, k = x.shape
    n = w.shape[1]
    tm, tn = min(tm, m), min(tn, n)
    assert m % tm == 0 and n % tn == 0
    in_specs = [pl.BlockSpec((tm, k), lambda i, j: (i, 0)),
                pl.BlockSpec((1, k), lambda i, j: (0, 0)),
                pl.BlockSpec((k, tn), lambda i, j: (0, j))]
    args = [x, g, w]
    out_specs = [pl.BlockSpec((tm, tn), lambda i, j: (i, j))]
    out_shape = [jax.ShapeDtypeStruct((m, n), F32)]
    if res is not None:
        in_specs.append(pl.BlockSpec((tm, tn), lambda i, j: (i, j)))
        args.append(res)
    if w_aux is not None:
        na = w_aux.shape[1]
        in_specs.append(pl.BlockSpec((k, na), lambda i, j: (0, 0)))
        args.append(w_aux)
        out_specs.append(pl.BlockSpec((tm, na), lambda i, j: (i, 0)))
        out_shape.append(jax.ShapeDtypeStruct((m, na), F32))
    out = pl.pallas_call(
        functools.partial(_nmm_kernel, norm=norm, residual=res is not None, aux=w_aux is not None),
        grid=(m // tm, n // tn),
        in_specs=in_specs,
        out_specs=out_specs,
        out_shape=out_shape,
        scratch_shapes=[pltpu.VMEM((tm, k), BF16)],
        compiler_params=_cparams(("parallel", "arbitrary")),
        name="nmm",
    )(*args)
    return out if w_aux is not None else out[0]


TM_IN_PROJ = 512
TM_GDN_PROJ = 512
TM_OUT_PROJ = 1024
TM_FFN, TF_FFN = 1024, 1024
TB_MIX = 256
TM_RWKV_PROJ = 256


def _in_proj(x, g, w_main, w_aux, *, norm=True):
    return _nmm(x, g, w_main, None, w_aux, norm=norm, tm=TM_IN_PROJ, tn=w_main.shape[1])


def _out_proj(o, w_out, res):
    ones = jnp.ones((1, o.shape[1]), F32)
    return _nmm(o, ones, w_out, res, norm=False, tm=TM_OUT_PROJ, tn=w_out.shape[1])


def _ffn_kernel(x_ref, g_ref, w1_ref, w2_ref, go_ref, o_ref, xn_ref, acc_ref, *, out_norm):
    f = pl.program_id(1)

    @pl.when(f == 0)
    def _():
        xn_ref[...] = _norm_rows(x_ref[...], g_ref[...]).astype(BF16)
        acc_ref[...] = jnp.zeros_like(acc_ref)

    h = jnp.dot(xn_ref[...], w1_ref[...], preferred_element_type=F32)
    a = jnp.square(jnp.maximum(h, 0.0)).astype(BF16)
    acc_ref[...] += jnp.dot(a, w2_ref[...], preferred_element_type=F32)

    @pl.when(f == pl.num_programs(1) - 1)
    def _():
        y = x_ref[...] + acc_ref[...]
        o_ref[...] = _norm_rows(y, go_ref[...]) if out_norm else y


def _ffn(x, g, w1, w2, layer, g_out=None, *, tm, tf):
    m, d = x.shape
    dff = w1.shape[2]
    tm, tf = min(tm, m), min(tf, dff)
    assert m % tm == 0 and dff % tf == 0
    vec = pl.BlockSpec((1, d), lambda i, j: (0, 0))
    return pl.pallas_call(
        functools.partial(_ffn_kernel, out_norm=g_out is not None),
        grid=(m // tm, dff // tf),
        in_specs=[pl.BlockSpec((tm, d), lambda i, j: (i, 0)), vec,
                  pl.BlockSpec((None, d, tf), lambda i, j: (layer, 0, j)),
                  pl.BlockSpec((None, tf, d), lambda i, j: (layer, j, 0)), vec],
        out_specs=pl.BlockSpec((tm, d), lambda i, j: (i, 0)),
        out_shape=jax.ShapeDtypeStruct((m, d), F32),
        scratch_shapes=[pltpu.VMEM((tm, d), BF16), pltpu.VMEM((tm, d), F32)],
        compiler_params=_cparams(("parallel", "arbitrary")),
        name="ffn",
    )(x, g, w1, w2, g if g_out is None else g_out)


def _gdn_prompt_kernel(qkv_ref, pba_ref, s0_ref, gp_ref, nw_ref,
                       o_ref, s_out_ref,
                       s_ref, u_ref, w_ref, qd_ref, kd_ref, qk_ref, gl_ref,
                       *, tb, heads, dk, dv):
    t = pl.program_id(1)
    key = heads * dk
    ch = 2 * key + heads * dv
    l = CHUNK

    @pl.when(t == 0)
    def _():
        s_ref[...] = s0_ref[0]

    a_log = gp_ref[0:1, :]
    dt_bias = gp_ref[1:2, :]
    hh = range(heads)
    l2 = 2 * l
    r2 = lax.broadcasted_iota(jnp.int32, (l2, l2), 0)
    c2 = lax.broadcasted_iota(jnp.int32, (l2, l2), 1)
    same = (r2 >= l) == (c2 >= l)
    incl = same & (r2 >= c2)
    strict = same & (r2 > c2)
    tril = incl.astype(F32)
    first = lax.broadcasted_iota(jnp.int32, (l2, 128), 0) < l

    pairs = range(tb // l2)
    rows = [slice(pi * l2, (pi + 1) * l2) for pi in pairs]
    beta_all, gc, gc_t, g_end = [], [], [], []
    for pi in pairs:
        ba = pba_ref[rows[pi], :]
        beta_all.append(_sigmoid(ba))
        g_all = -jnp.exp(a_log) * _softplus(ba + dt_bias)
        gc.append(_dot_hi(tril, g_all))
        gc_t.append(gc[pi].T)
        g_end.append(jnp.where(first, gc[pi][l - 1:l, :], gc[pi][l2 - 1:l2, :]))
        gl_ref[2 * pi:2 * pi + 1, :] = jnp.exp(gc[pi][l - 1:l, :])
        gl_ref[2 * pi + 1:2 * pi + 2, :] = jnp.exp(gc[pi][l2 - 1:l2, :])
    cc = [(pi, h) for pi in pairs for h in hh]
    b_col = [beta_all[pi][:, h:h + 1] for pi, h in cc]
    gi = [gc[pi][:, heads + h:heads + h + 1] for pi, h in cc]
    q = [qkv_ref[rows[pi], h * dk:(h + 1) * dk] for pi, h in cc]
    k = [qkv_ref[rows[pi], key + h * dk:key + (h + 1) * dk] for pi, h in cc]
    v = [qkv_ref[rows[pi], 2 * key + h * dv:2 * key + (h + 1) * dv] for pi, h in cc]
    nc = range(len(cc))
    dmat = [jnp.where(incl, jnp.exp(jnp.where(incl, gi[i] - gc_t[pi][heads + h:heads + h + 1, :], 0.0)), 0.0)
            for i, (pi, h) in enumerate(cc)]
    kb = [k[i] * b_col[i] for i in nc]
    kk = [_dot_nt(kb[i], k[i]) for i in nc]
    qk = [_dot_nt(q[i], k[i]) for i in nc]
    t_inv = _unit_lower_inverse([jnp.where(strict, kk[i] * dmat[i], 0.0) for i in nc], l)
    egi = [jnp.exp(gi[i]) for i in nc]
    sol = [_dot(t_inv[i], jnp.concatenate([v[i] * b_col[i], kb[i] * egi[i]], axis=-1)) for i in nc]
    for i, (pi, h) in enumerate(cc):
        hs = slice(h * dk, (h + 1) * dk)
        u_ref[rows[pi], h * dv:(h + 1) * dv] = sol[i][:, :dv]
        w_ref[rows[pi], hs] = sol[i][:, dv:].astype(BF16)
        qd_ref[rows[pi], hs] = (q[i] * egi[i]).astype(BF16)
        kd_ref[rows[pi], hs] = (k[i] * jnp.exp(g_end[pi][:, heads + h:heads + h + 1] - gi[i])).astype(BF16)
        qkm = jnp.where(incl, qk[i] * dmat[i], 0.0).astype(BF16)
        qk_ref[h, pi * l2:pi * l2 + l, :] = qkm[:l, :l]
        qk_ref[h, pi * l2 + l:(pi + 1) * l2, :] = qkm[l:, l:]

    for ci in range(tb // l):
        rows = slice(ci * l, (ci + 1) * l)
        s = [s_ref[h] for h in hh]
        wq = [_dot(jnp.concatenate([w_ref[rows, h * dk:(h + 1) * dk], qd_ref[rows, h * dk:(h + 1) * dk]], axis=0),
                   s[h]) for h in hh]
        v_new = [u_ref[rows, h * dv:(h + 1) * dv] - wq[h][:l] for h in hh]
        o2 = [_dot(qk_ref[h, rows, :], v_new[h]) for h in hh]
        ds = [_dot_tn(kd_ref[rows, h * dk:(h + 1) * dk], v_new[h]) for h in hh]
        for h in hh:
            s_ref[h] = s[h] * gl_ref[ci:ci + 1, heads + h:heads + h + 1] + ds[h]
            o = wq[h][l:] + o2[h]
            z = qkv_ref[rows, ch + h * dv:ch + (h + 1) * dv]
            o = o * lax.rsqrt(jnp.mean(o * o, axis=-1, keepdims=True) + RMS_EPS) * nw_ref[...]
            o_ref[rows, h * dv:(h + 1) * dv] = o * _silu(z)

    @pl.when(t == pl.num_programs(1) - 1)
    def _():
        s_out_ref[0] = s_ref[...]


def _gdn_prompt(qkvz, pba, s0, gp, nw, *, batch, seq, heads, dk, dv, tb):
    key, val = heads * dk, heads * dv
    ch = 2 * key + val
    tb = min(tb, seq)
    assert seq % tb == 0 and tb % (2 * CHUNK) == 0
    nt = seq // tb
    return pl.pallas_call(
        functools.partial(_gdn_prompt_kernel, tb=tb, heads=heads, dk=dk, dv=dv),
        grid=(batch, nt),
        in_specs=[pl.BlockSpec((tb, ch + val), lambda b, t: (b * nt + t, 0)),
                  pl.BlockSpec((tb, 128), lambda b, t: (b * nt + t, 0)),
                  pl.BlockSpec((1, heads, dk, dv), lambda b, t: (b, 0, 0, 0)),
                  pl.BlockSpec((8, 128), lambda b, t: (0, 0)),
                  pl.BlockSpec((1, dv), lambda b, t: (0, 0))],
        out_specs=[pl.BlockSpec((tb, val), lambda b, t: (b * nt + t, 0)),
                   pl.BlockSpec((1, heads, dk, dv), lambda b, t: (b, 0, 0, 0))],
        out_shape=[jax.ShapeDtypeStruct((batch * seq, val), F32),
                   jax.ShapeDtypeStruct((batch, heads, dk, dv), F32)],
        scratch_shapes=[pltpu.VMEM((heads, dk, dv), F32),
                        pltpu.VMEM((tb, val), F32), pltpu.VMEM((tb, key), BF16),
                        pltpu.VMEM((tb, key), BF16), pltpu.VMEM((tb, key), BF16),
                        pltpu.VMEM((heads, tb, CHUNK), BF16), pltpu.VMEM((max(8, tb // CHUNK), 128), F32)],
        compiler_params=_cparams(("parallel", "arbitrary")),
        name="gdn_prompt",
    )(qkvz, pba, s0, gp, nw)


GDN_PROJ_COLS = 256
GDN_CONV_ROWS = 64


def _gdn_in_proj_kernel(x_ref, g_ref, w_ref, wa_ref, conv0_ref, cw_ref, o_ref, aux_ref, conv_out_ref,
                        xn_ref, carry_ref, *, heads, dk, dv):
    t = pl.program_id(1)
    key = heads * dk
    ch = 2 * key + heads * dv
    tm = x_ref.shape[0]
    n = w_ref.shape[1]

    @pl.when(t == 0)
    def _():
        carry_ref[...] = conv0_ref[0]

    xn_ref[...] = _norm_rows(x_ref[...], g_ref[...]).astype(BF16)
    aux_ref[...] = jnp.dot(xn_ref[...], wa_ref[...], preferred_element_type=F32)
    for c0 in range(0, n, GDN_PROJ_COLS):
        cs = slice(c0, c0 + GDN_PROJ_COLS)
        y = jnp.dot(xn_ref[...], w_ref[:, cs], preferred_element_type=F32)
        if c0 >= ch:
            o_ref[:, cs] = y
            continue
        ext = jnp.concatenate([carry_ref[:, cs], y], axis=0)
        carry_ref[:, cs] = y[tm - 8:, :]
        for r0 in range(0, tm, GDN_CONV_ROWS):
            for j in range(0, GDN_PROJ_COLS, dk):
                cj = slice(c0 + j, c0 + j + dk)
                blk = ext[r0:r0 + GDN_CONV_ROWS + 8, j:j + dk]
                z = blk[8:] * cw_ref[CONV_W - 1:CONV_W, cj]
                for s in range(1, CONV_W):
                    z = z + pltpu.roll(blk, s, 0)[8:] * cw_ref[CONV_W - 1 - s:CONV_W - s, cj]
                z = _silu(z)
                if c0 + j < key:
                    z = z * lax.rsqrt(jnp.sum(z * z, axis=-1, keepdims=True) + 1e-6) * (dk ** -0.5)
                elif c0 + j < 2 * key:
                    z = z * lax.rsqrt(jnp.sum(z * z, axis=-1, keepdims=True) + 1e-6)
                o_ref[r0:r0 + GDN_CONV_ROWS, cj] = z

    @pl.when(t == pl.num_programs(1) - 1)
    def _():
        conv_out_ref[0] = carry_ref[...]


def _gdn_in_proj(x, g, w_main, w_ba, conv0, cw_t, *, batch, seq, heads, dk, dv, tm):
    m, d = x.shape
    key, val = heads * dk, heads * dv
    ch = 2 * key + val
    n = w_main.shape[1]
    assert dk == dv and n % GDN_PROJ_COLS == 0 and ch % GDN_PROJ_COLS == 0 and GDN_PROJ_COLS % dk == 0
    tm = min(tm, seq)
    assert seq % tm == 0 and tm >= 8
    nt = seq // tm
    row = lambda width: pl.BlockSpec((tm, width), lambda b, t: (b * nt + t, 0))
    full = lambda a: pl.BlockSpec(a.shape, lambda b, t: (0,) * a.ndim)
    st = pl.BlockSpec((1, 8, ch), lambda b, t: (b, 0, 0))
    return pl.pallas_call(
        functools.partial(_gdn_in_proj_kernel, heads=heads, dk=dk, dv=dv),
        grid=(batch, nt),
        in_specs=[row(d), full(g), full(w_main), full(w_ba), st, full(cw_t)],
        out_specs=[row(n), row(w_ba.shape[1]), st],
        out_shape=[jax.ShapeDtypeStruct((m, n), F32), jax.ShapeDtypeStruct((m, w_ba.shape[1]), F32),
                   jax.ShapeDtypeStruct((batch, 8, ch), F32)],
        scratch_shapes=[pltpu.VMEM((tm, d), BF16), pltpu.VMEM((8, ch), F32)],
        compiler_params=_cparams(("parallel", "arbitrary")),
        name="gdn_in_proj",
    )(x, g, w_main, w_ba, conv0, cw_t)


def _mlstm_prompt_kernel(pm_ref, pif_ref, bif_ref, c0_ref, n0_ref, m0_ref, nw_ref,
                         o_ref, c_out_ref, n_out_ref, m_out_ref,
                         c_ref, n_ref, m_ref, bc_ref, ni_ref, col_ref, kv_ref, kc_ref, sc_ref,
                         cin_ref, nin_ref, min_ref, *, tb, heads, dk, dv):
    t = pl.program_id(1)
    l = CHUNK
    qk_w = heads * dk
    v_off = 2 * qk_w
    o_off = v_off + heads * dv

    @pl.when(t == 0)
    def _():
        c_ref[...] = c0_ref[0]
        n_ref[...] = n0_ref[0]
        m_ref[...] = m0_ref[0]

    incl, _ = _tri_masks(l)
    tril = incl.astype(F32)
    hh = range(heads)
    nchunk = tb // l
    scale = dk ** -0.5

    crow = [slice(ci * l, (ci + 1) * l) for ci in range(nchunk)]
    gates, bcum, bcum_t, gates_t = [], [], [], []
    for ci in range(nchunk):
        g = pif_ref[crow[ci], :] + bif_ref[...]
        g = GATE_CAP * jnp.tanh(g / GATE_CAP)
        gates.append(g)
        bcum.append(_dot_hi(tril, _log_sigmoid(g)))
        bcum_t.append(bcum[ci].T)
        gates_t.append(g.T)
        bc_ref[crow[ci], :] = bcum[ci]
    cc = [(ci, h) for ci in range(nchunk) for h in hh]
    nc = range(len(cc))
    bi = [bcum[ci][:, heads + h:heads + h + 1] for ci, h in cc]
    b_last = [bcum[ci][l - 1:l, heads + h:heads + h + 1] for ci, h in cc]
    k = [pm_ref[crow[ci], qk_w + h * dk:qk_w + (h + 1) * dk] for ci, h in cc]
    v = [pm_ref[crow[ci], v_off + h * dv:v_off + (h + 1) * dv].astype(BF16) for ci, h in cc]
    qk = [_dot_nt(pm_ref[crow[ci], h * dk:(h + 1) * dk] * scale, k[i]) for i, (ci, h) in enumerate(cc)]
    dlog = [jnp.where(incl, bi[i] - bcum_t[ci][heads + h:heads + h + 1, :] + gates_t[ci][h:h + 1, :], NEG_BIG)
            for i, (ci, h) in enumerate(cc)]
    m_intra = [jnp.max(dlog[i], axis=-1, keepdims=True) for i in nc]
    p = [jnp.where(incl, jnp.exp(dlog[i] - m_intra[i]), 0.0) * qk[i] for i in nc]
    den_intra = [jnp.sum(p[i], axis=-1, keepdims=True) for i in nc]
    num_intra = [_dot(p[i], v[i]) for i in nc]
    a_log = [b_last[i] - bi[i] + gates[ci][:, h:h + 1] for i, (ci, h) in enumerate(cc)]
    m_chunk = [jnp.max(a_log[i], axis=0, keepdims=True) for i in nc]
    kw = [k[i] * jnp.exp(a_log[i] - m_chunk[i]) for i in nc]
    kv_chunk = [_dot_tn(kw[i], v[i]) for i in nc]
    for i, (ci, h) in enumerate(cc):
        ni_ref[crow[ci], h * dv:(h + 1) * dv] = num_intra[i]
        col_ref[crow[ci], h:h + 1] = m_intra[i]
        col_ref[crow[ci], heads + h:heads + h + 1] = den_intra[i]
        kv_ref[i] = kv_chunk[i]
        kc_ref[i:i + 1, :] = jnp.sum(kw[i], axis=0, keepdims=True)
        sc_ref[i:i + 1, 0:1] = m_chunk[i]
        sc_ref[i:i + 1, 1:2] = b_last[i]

    for ci in range(nchunk):
        for h in hh:
            i = ci * heads + h
            c_mat = c_ref[h]
            n_vec = n_ref[h:h + 1, :]
            m_prev = m_ref[h:h + 1, :]
            cin_ref[i] = c_mat
            nin_ref[i:i + 1, :] = n_vec
            min_ref[i:i + 1, :] = m_prev
            m_chunk = sc_ref[i:i + 1, 0:1]
            b_last = sc_ref[i:i + 1, 1:2]
            m_new = jnp.maximum(b_last + m_prev, m_chunk)
            f_s = jnp.exp(b_last + m_prev - m_new)
            i_s = jnp.exp(m_chunk - m_new)
            c_ref[h] = f_s[:, 0:1] * c_mat + i_s[:, 0:1] * kv_ref[i]
            n_ref[h:h + 1, :] = f_s * n_vec + i_s * kc_ref[i:i + 1, :]
            m_ref[h:h + 1, :] = m_new

    q = [pm_ref[crow[ci], h * dk:(h + 1) * dk] * scale for ci, h in cc]
    qc = [_dot(q[i], cin_ref[i]) for i in nc]
    qn = [jnp.sum(q[i] * nin_ref[i:i + 1, :], axis=-1, keepdims=True) for i in nc]
    m_prev = [min_ref[i:i + 1, 0:1] for i in nc]
    bi = [bc_ref[crow[ci], heads + h:heads + h + 1] for ci, h in cc]
    m_in = [col_ref[crow[ci], h:h + 1] for ci, h in cc]
    m_t = [jnp.maximum(bi[i] + m_prev[i], m_in[i]) for i in nc]
    s_inter = [jnp.exp(bi[i] + m_prev[i] - m_t[i]) for i in nc]
    s_intra = [jnp.exp(m_in[i] - m_t[i]) for i in nc]
    den = [s_inter[i] * qn[i] + s_intra[i] * col_ref[crow[ci], heads + h:heads + h + 1]
           for i, (ci, h) in enumerate(cc)]
    h_t = [(s_inter[i] * qc[i] + s_intra[i] * ni_ref[crow[ci], h * dv:(h + 1) * dv])
           / jnp.maximum(jnp.abs(den[i]), jnp.exp(-m_t[i])) for i, (ci, h) in enumerate(cc)]
    ms = [jnp.mean(h_t[i] * h_t[i], axis=-1, keepdims=True) for i in nc]
    for i, (ci, h) in enumerate(cc):
        h_n = h_t[i] * lax.rsqrt(ms[i] + RMS_EPS) * nw_ref[:, h * dv:(h + 1) * dv]
        o_pre = pm_ref[crow[ci], o_off + h * dv:o_off + (h + 1) * dv]
        o_ref[crow[ci], h * dv:(h + 1) * dv] = _sigmoid(o_pre) * h_n

    @pl.when(t == pl.num_programs(1) - 1)
    def _():
        c_out_ref[0] = c_ref[...]
        n_out_ref[0] = n_ref[...]
        m_out_ref[0] = m_ref[...]


def _mlstm_prompt(pm, pif, bif, c0, n0, m0, nw, *, batch, seq, heads, dk, dv, tb):
    width = pm.shape[1]
    val = heads * dv
    tb = min(tb, seq)
    assert seq % tb == 0 and tb % CHUNK == 0
    nt = seq // tb
    nck = (tb // CHUNK) * heads
    return pl.pallas_call(
        functools.partial(_mlstm_prompt_kernel, tb=tb, heads=heads, dk=dk, dv=dv),
        grid=(batch, nt),
        in_specs=[pl.BlockSpec((tb, width), lambda b, t: (b * nt + t, 0)),
                  pl.BlockSpec((tb, 128), lambda b, t: (b * nt + t, 0)),
                  pl.BlockSpec((1, 128), lambda b, t: (0, 0)),
                  pl.BlockSpec((1, heads, dk, dv), lambda b, t: (b, 0, 0, 0)),
                  pl.BlockSpec((1, 8, dk), lambda b, t: (b, 0, 0)),
                  pl.BlockSpec((1, 8, 128), lambda b, t: (b, 0, 0)),
                  pl.BlockSpec((1, val), lambda b, t: (0, 0))],
        out_specs=[pl.BlockSpec((tb, val), lambda b, t: (b * nt + t, 0)),
                   pl.BlockSpec((1, heads, dk, dv), lambda b, t: (b, 0, 0, 0)),
                   pl.BlockSpec((1, 8, dk), lambda b, t: (b, 0, 0)),
                   pl.BlockSpec((1, 8, 128), lambda b, t: (b, 0, 0))],
        out_shape=[jax.ShapeDtypeStruct((batch * seq, val), F32),
                   jax.ShapeDtypeStruct((batch, heads, dk, dv), F32),
                   jax.ShapeDtypeStruct((batch, 8, dk), F32),
                   jax.ShapeDtypeStruct((batch, 8, 128), F32)],
        scratch_shapes=[pltpu.VMEM((heads, dk, dv), F32), pltpu.VMEM((8, dk), F32),
                        pltpu.VMEM((8, 128), F32),
                        pltpu.VMEM((tb, 128), F32), pltpu.VMEM((tb, val), F32), pltpu.VMEM((tb, 128), F32),
                        pltpu.VMEM((nck, dk, dv), F32), pltpu.VMEM((max(8, nck), dk), F32),
                        pltpu.VMEM((max(8, nck), 128), F32),
                        pltpu.VMEM((nck, dk, dv), F32), pltpu.VMEM((max(8, nck), dk), F32),
                        pltpu.VMEM((max(8, nck), 128), F32)],
        compiler_params=_cparams(("parallel", "arbitrary")),
        name="mlstm_prompt",
    )(pm, pif, bif, c0, n0, m0, nw)


def _mlstm_prep(w_in, b_if, norm_w, w_out, *, heads, dk, dv):
    main = 2 * heads * dk + 2 * heads * dv
    return dict(w_main=w_in[:, :main].astype(BF16),
                w_if=_pad_cols(w_in[:, main:], 128).astype(BF16),
                bif=_pad_cols(b_if[None, :], 128), nw=norm_w[None, :], w_out=w_out.astype(BF16))


def _mlstm_prompt_layer(x, g, p, c0, n0, m0, *, batch, seq, heads, dk, dv, norm=True, residual=True):
    pm, pif = _in_proj(x, g, p["w_main"], p["w_if"], norm=norm)
    n0p = jnp.pad(n0, ((0, 0), (0, 8 - heads), (0, 0)))
    m0p = jnp.broadcast_to(jnp.pad(m0, ((0, 0), (0, 8 - heads)))[:, :, None], (batch, 8, 128))
    o, c, n, m = _mlstm_prompt(pm, pif, p["bif"], c0, n0p, m0p, p["nw"],
                               batch=batch, seq=seq, heads=heads, dk=dk, dv=dv, tb=TB_MIX)
    y = _out_proj(o, p["w_out"], x if residual else None)
    return y, c, n[:, :heads, :], m[:, :heads, 0]


def _rwkv_proj_body(h, prev, mu_ref, wrkv_ref, w1_ref, w2_ref, a1_ref, a2_ref, g1_ref, g2_ref,
                    vec_ref, r_ref, k_ref, v_ref, lw_ref, kk_ref, a_ref, g_ref):
    xx = prev - h

    def mix(j):
        return (h + xx * mu_ref[j:j + 1, :]).astype(BF16)

    w0, a0, k_k, k_a = (vec_ref[j:j + 1, :] for j in range(4))
    r_ref[...] = jnp.dot(mix(0), wrkv_ref[0], preferred_element_type=F32)
    lora_w = _dot(jnp.tanh(_dot(mix(1), w1_ref[...])), w2_ref[...])
    w_log = -_softplus(-(w0 + lora_w)) - 0.5
    lw_ref[...] = -jnp.exp(w_log)
    k = jnp.dot(mix(2), wrkv_ref[1], preferred_element_type=F32)
    v_ref[...] = jnp.dot(mix(3), wrkv_ref[2], preferred_element_type=F32)
    a = _sigmoid(a0 + _dot(_dot(mix(4), a1_ref[...]), a2_ref[...]))
    g_ref[...] = _dot(_sigmoid(_dot(mix(5), g1_ref[...])), g2_ref[...])
    kk_ref[...] = k * k_k
    k_ref[...] = k * (1.0 + (a - 1.0) * k_a)
    a_ref[...] = a


def _rwkv_proj_sample_kernel(x_ref, gn_ref, prev_ref, *refs):
    h = _norm_rows(x_ref[...], gn_ref[...])
    hn_ref = refs[-1]
    hn_ref[...] = h
    _rwkv_proj_body(h, prev_ref[...], *refs[:-1])


def _rwkv_proj_prompt_kernel(x_ref, gn_ref, shift0_ref, *refs):
    carry_ref = refs[-1]
    shift_out_ref = refs[-2]
    t = pl.program_id(1)

    @pl.when(t == 0)
    def _():
        carry_ref[...] = shift0_ref[0]

    h = _norm_rows(x_ref[...], gn_ref[...])
    rows = h.shape[0]
    first = lax.broadcasted_iota(jnp.int32, h.shape, 0) == 0
    prev = jnp.where(first, carry_ref[0:1, :], pltpu.roll(h, 1, 0))
    carry_ref[0:1, :] = h[rows - 1:rows, :]
    _rwkv_proj_body(h, prev, *refs[:-2])

    @pl.when(t == pl.num_programs(1) - 1)
    def _():
        shift_out_ref[0] = carry_ref[...]


def _norm_rows(x, g):
    return x * lax.rsqrt(jnp.mean(x * x, axis=-1, keepdims=True) + RMS_EPS) * g


def _rwkv_proj(x, g_norm, shift0, p, *, batch, seq, tm):
    m, d = x.shape
    consts = [p["mu"], p["w_rkv"], p["w1"], p["w2"], p["a1"], p["a2"], p["g1"], p["g2"], p["vec"]]
    if seq == 1:
        row = pl.BlockSpec((m, d), lambda i: (0, 0))
        out = pl.pallas_call(
            _rwkv_proj_sample_kernel,
            grid=(1,),
            in_specs=[row, pl.BlockSpec((1, d), lambda i: (0, 0)), row] + [_const_block(a) for a in consts],
            out_specs=[row] * 8,
            out_shape=[jax.ShapeDtypeStruct((m, d), F32)] * 8,
            compiler_params=_cparams(("arbitrary",)),
            name="rwkv_proj_sample",
        )(x, g_norm, shift0, *consts)
        return out[:7], out[7]
    tm = min(tm, seq)
    assert seq % tm == 0
    nt = seq // tm
    row = pl.BlockSpec((tm, d), lambda b, t: (b * nt + t, 0))
    st = pl.BlockSpec((1, 8, d), lambda b, t: (b, 0, 0))

    def full(a):
        nd = a.ndim
        return pl.BlockSpec(a.shape, lambda b, t: (0,) * nd)

    shift0_p = jnp.pad(shift0[:, None, :], ((0, 0), (0, 7), (0, 0)))
    out = pl.pallas_call(
        _rwkv_proj_prompt_kernel,
        grid=(batch, nt),
        in_specs=[row, full(g_norm), st] + [full(a) for a in consts],
        out_specs=[row] * 7 + [st],
        out_shape=[jax.ShapeDtypeStruct((m, d), F32)] * 7 + [jax.ShapeDtypeStruct((batch, 8, d), F32)],
        scratch_shapes=[pltpu.VMEM((8, d), F32)],
        compiler_params=_cparams(("parallel", "arbitrary")),
        name="rwkv_proj_prompt",
    )(x, g_norm, shift0_p, *consts)
    return out[:7], out[7][:, 0, :]


def _rwkv_prompt_kernel(r_ref, k_ref, v_ref, lw_ref, kk_ref, a_ref, g_ref, s0_ref, hp_ref,
                        o_ref, s_out_ref, s_ref, rr_ref, yy_ref, mx_ref, n0_ref, gw_ref, gb_ref, el_ref,
                        *, tb, heads, hd):
    t = pl.program_id(1)
    l = CHUNK

    @pl.when(t == 0)
    def _():
        s_ref[...] = s0_ref[0]

    hh = range(heads)
    hs = [slice(h * hd, (h + 1) * hd) for h in hh]
    l2 = 2 * l
    r2 = lax.broadcasted_iota(jnp.int32, (l2, l2), 0)
    c2 = lax.broadcasted_iota(jnp.int32, (l2, l2), 1)
    same = (r2 >= l) == (c2 >= l)
    incl = same & (r2 >= c2)
    strict = same & (r2 > c2)
    tril = incl.astype(F32)
    first = lax.broadcasted_iota(jnp.int32, (l2, heads * hd), 0) < l
    first2 = (lax.broadcasted_iota(jnp.int32, (2 * l2, hd), 0) & l) == 0
    zeros = jnp.zeros((l2, hd), F32)

    def pair(pi, carry):
        rows = pl.ds(pl.multiple_of(pi * l2, l2), l2)
        lw = lw_ref[rows, :]
        lwc = _dot_hi(tril, lw)
        lw_end = jnp.where(first, lwc[l - 1:l, :], lwc[l2 - 1:l2, :])
        e_in = jnp.exp(lwc)
        e_prev = jnp.exp(lwc - lw)
        e_neg = jnp.exp(-lwc)
        e_end = jnp.exp(lw_end - lwc)
        r = [r_ref[rows, hs[h]] for h in hh]
        v = [v_ref[rows, hs[h]] for h in hh]
        k = [k_ref[rows, hs[h]] for h in hh]
        kk = [kk_ref[rows, hs[h]] for h in hh]
        kk = [kk[h] * lax.rsqrt(jnp.maximum(jnp.sum(kk[h] * kk[h], axis=-1, keepdims=True), 1e-24)) for h in hh]
        bv = [kk[h] * a_ref[rows, hs[h]] for h in hh]
        a_t = [-kk[h] * e_prev[:, hs[h]] for h in hh]
        r_t = [r[h] * e_in[:, hs[h]] for h in hh]
        gm = [_dot_nt(jnp.concatenate([a_t[h], r_t[h]], axis=0),
                      jnp.concatenate([bv[h] * e_neg[:, hs[h]], k[h] * e_neg[:, hs[h]]], axis=0)) for h in hh]
        ak_m = [jnp.where(strict, gm[h][:l2, l2:], 0.0).astype(BF16) for h in hh]
        rbk_m = [jnp.concatenate([jnp.where(incl, gm[h][l2:, :l2], 0.0),
                                  jnp.where(incl, gm[h][l2:, l2:], 0.0)], axis=1).astype(BF16) for h in hh]
        t_inv = _unit_lower_inverse([jnp.where(strict, -gm[h][:l2, :l2], 0.0) for h in hh], l)
        vb = [v[h].astype(BF16) for h in hh]
        akv = [_dot(ak_m[h], vb[h]) for h in hh]
        x1 = [_dot(t_inv[h], jnp.concatenate([a_t[h], akv[h]], axis=1)).astype(BF16) for h in hh]
        low = [jnp.concatenate([x1[h], jnp.concatenate([zeros.astype(BF16), vb[h]], axis=1)], axis=0) for h in hh]
        x2 = [_dot(rbk_m[h], low[h]) for h in hh]
        bk = [jnp.concatenate([bv[h] * e_end[:, hs[h]], k[h] * e_end[:, hs[h]]], axis=0) for h in hh]
        bk2 = [jnp.concatenate([jnp.where(first2, bk[h], 0.0), jnp.where(first2, 0.0, bk[h])], axis=1) for h in hh]
        mn = [_dot_tn(low[h], bk2[h]) for h in hh]
        for c in range(2):
            crow = pl.ds(pl.multiple_of(pi * l2 + c * l, l), l)
            for h in hh:
                mx_ref[h, crow, :] = mn[h][:hd, c * hd:(c + 1) * hd].astype(BF16)
                n0_ref[h, crow, :] = mn[h][hd:, c * hd:(c + 1) * hd]
        for h in hh:
            rr_ref[h, rows, :] = (r_t[h] + x2[h][:, :hd]).astype(BF16)
            yy_ref[h, rows, :] = x2[h][:, hd:]
            g = g_ref[rows, hs[h]]
            bonus = jnp.sum(r[h] * k[h] * hp_ref[0:1, hs[h]], axis=-1, keepdims=True) * v[h]
            gw_ref[h, rows, :] = hp_ref[1:2, hs[h]] * g
            gb_ref[h, rows, :] = (hp_ref[2:3, hs[h]] + bonus) * g
            el_ref[h, pl.ds(pi * 2, 1), :] = jnp.exp(lwc[l - 1:l, hs[h]])
            el_ref[h, pl.ds(pi * 2 + 1, 1), :] = jnp.exp(lwc[l2 - 1:l2, hs[h]])
        return carry

    lax.fori_loop(0, tb // l2, pair, 0)

    for ci in range(tb // l):
        rows = slice(ci * l, (ci + 1) * l)
        s = [s_ref[h] for h in hh]
        y = [_dot_nt(rr_ref[h, rows, :], s[h]) for h in hh]
        sm = [_dot(s[h], mx_ref[h, rows, :]) for h in hh]
        for h in hh:
            s_ref[h] = s[h] * el_ref[h, ci:ci + 1, :] + sm[h] + n0_ref[h, rows, :]
        y = [y[h] + yy_ref[h, rows, :] for h in hh]
        mean = [jnp.mean(y[h], axis=-1, keepdims=True) for h in hh]
        yc = [y[h] - mean[h] for h in hh]
        var = [jnp.mean(yc[h] * yc[h], axis=-1, keepdims=True) for h in hh]
        for h in hh:
            o_ref[rows, hs[h]] = yc[h] * lax.rsqrt(var[h] + RW_GN_EPS) * gw_ref[h, rows, :] + gb_ref[h, rows, :]

    @pl.when(t == pl.num_programs(1) - 1)
    def _():
        s_out_ref[0] = s_ref[...]


def _rwkv_prompt(r, k, v, lw, kk, a, g, s0, hp, *, batch, seq, heads, hd, tb):
    d = heads * hd
    tb = min(tb, seq)
    assert seq % tb == 0 and tb % CHUNK == 0
    nt = seq // tb
    row = pl.BlockSpec((tb, d), lambda b, t: (b * nt + t, 0))
    st = pl.BlockSpec((1, heads, hd, hd), lambda b, t: (b, 0, 0, 0))
    return pl.pallas_call(
        functools.partial(_rwkv_prompt_kernel, tb=tb, heads=heads, hd=hd),
        grid=(batch, nt),
        in_specs=[row] * 7 + [st, pl.BlockSpec((8, d), lambda b, t: (0, 0))],
        out_specs=[row, st],
        out_shape=[jax.ShapeDtypeStruct((batch * seq, d), F32),
                   jax.ShapeDtypeStruct((batch, heads, hd, hd), F32)],
        scratch_shapes=[pltpu.VMEM((heads, hd, hd), F32),
                        pltpu.VMEM((heads, tb, hd), BF16), pltpu.VMEM((heads, tb, hd), F32),
                        pltpu.VMEM((heads, tb, hd), BF16), pltpu.VMEM((heads, tb, hd), F32),
                        pltpu.VMEM((heads, tb, hd), F32), pltpu.VMEM((heads, tb, hd), F32),
                        pltpu.VMEM((heads, max(8, tb // CHUNK), hd), F32)],
        compiler_params=_cparams(("parallel", "arbitrary")),
        name="rwkv_prompt",
    )(r, k, v, lw, kk, a, g, s0, hp)


def _pad_rows(a, n):
    return jnp.pad(a, ((0, n - a.shape[0]), (0, 0)))


def _rwkv_prep(mu, w_rkv, w_o, w0, w1, w2, a0, a1, a2, g1, g2, k_k, k_a, r_k, lnx_w, lnx_b):
    d = w0.shape[0]
    lw = -(-w1.shape[1] // 128) * 128
    la = -(-a1.shape[1] // 128) * 128
    lg = -(-g1.shape[1] // 128) * 128
    return dict(mu=_pad_rows(mu, 8), w_rkv=w_rkv.astype(BF16), w_o=w_o.astype(BF16),
                w1=_pad_cols(w1, lw).astype(BF16), w2=_pad_rows(w2, lw).astype(BF16),
                a1=_pad_cols(a1, la).astype(BF16), a2=_pad_rows(a2, la).astype(BF16),
                g1=_pad_cols(g1, lg).astype(BF16), g2=_pad_rows(g2, lg).astype(BF16),
                vec=_pad_rows(jnp.stack([w0, a0, k_k, k_a]), 8),
                hp=_pad_rows(jnp.stack([r_k.reshape(d), lnx_w, lnx_b]), 8))


def _rwkv_prompt_layer(x, g_norm, p, shift0, s0, *, batch, seq, heads, hd, residual=True):
    (r, k, v, lw, kk, a, g), shift = _rwkv_proj(x, g_norm, shift0, p, batch=batch, seq=seq, tm=TM_RWKV_PROJ)
    o, s = _rwkv_prompt(r, k, v, lw, kk, a, g, s0, p["hp"], batch=batch, seq=seq, heads=heads, hd=hd, tb=TB_MIX)
    y = _out_proj(o, p["w_o"], x if residual else None)
    return y, shift, s


def _gdn_sample_pre_kernel(pm_ref, pba_ref, conv_ref, cw_ref, gp_ref, qkv_ref, conv_out_ref, sc_ref,
                           *, heads, dk, dv):
    key = heads * dk
    ch = 2 * key + heads * dv
    u = pm_ref[:, 0:ch]
    y = u * cw_ref[CONV_W - 1:CONV_W, :]
    for j in range(CONV_W - 1):
        y = y + conv_ref[j] * cw_ref[j:j + 1, :]
        conv_out_ref[j] = conv_ref[j + 1] if j + 1 < CONV_W - 1 else u
    y = _silu(y)
    for c in range(ch // 128):
        cs = slice(c * 128, (c + 1) * 128)
        yc = y[:, cs]
        if c * 128 < key:
            yc = yc * lax.rsqrt(jnp.sum(yc * yc, axis=-1, keepdims=True) + 1e-6) * (dk ** -0.5)
        elif c * 128 < 2 * key:
            yc = yc * lax.rsqrt(jnp.sum(yc * yc, axis=-1, keepdims=True) + 1e-6)
        qkv_ref[:, cs] = yc
    ba = pba_ref[...]
    lane = lax.broadcasted_iota(jnp.int32, ba.shape, 1)
    g = -jnp.exp(gp_ref[0:1, :]) * _softplus(ba + gp_ref[1:2, :])
    sc_ref[...] = jnp.where(lane < heads, _sigmoid(ba), jnp.exp(g))


def _gdn_sample_pre(pm, pba, conv_t, cw_t, gp, *, heads, dk, dv):
    n = pm.shape[0]
    ch = 2 * heads * dk + heads * dv
    return pl.pallas_call(
        functools.partial(_gdn_sample_pre_kernel, heads=heads, dk=dk, dv=dv),
        out_shape=[jax.ShapeDtypeStruct((n, ch), F32),
                   jax.ShapeDtypeStruct((CONV_W - 1, n, ch), F32),
                   jax.ShapeDtypeStruct((n, 128), F32)],
        compiler_params=pltpu.CompilerParams(vmem_limit_bytes=V7X_VMEM_LIMIT),
        name="gdn_sample_pre",
    )(pm, pba, conv_t, cw_t, gp)


SEQ_PER_STEP = 4


def _seq_block(a, nb):
    nd = a.ndim
    return pl.BlockSpec((nb,) + a.shape[1:], lambda b: (b,) + (0,) * (nd - 1))


def _const_block(a):
    nd = a.ndim
    return pl.BlockSpec(a.shape, lambda b: (0,) * nd)


def _gdn_sample_step_kernel(s0_ref, cols_ref, v_ref, z_ref, sc_ref, nw_ref, *refs, heads, nb, n_prev):
    prev_refs = refs[:n_prev]
    s_out_all, o_ref = refs[n_prev:]
    for p, prev_ref in enumerate(prev_refs):
        s_out_all[p] = prev_ref[...]
    s_out_ref = s_out_all.at[n_prev] if n_prev else s_out_all
    hh = range(heads)
    for i in range(nb):
        kc = [cols_ref[i, :, h:h + 1] for h in hh]
        qc = [cols_ref[i, :, heads + h:heads + h + 1] for h in hh]
        s0 = [s0_ref[i, h] for h in hh]
        eg = [sc_ref[i, h:h + 1, 1:2] for h in hh]
        ks = [jnp.sum(kc[h] * s0[h], axis=0, keepdims=True) for h in hh]
        s1 = [eg[h] * s0[h] + kc[h] * (sc_ref[i, h:h + 1, 0:1] * (v_ref[i, h:h + 1, :] - eg[h] * ks[h]))
              for h in hh]
        o = [jnp.sum(qc[h] * s1[h], axis=0, keepdims=True) for h in hh]
        ms = [jnp.mean(o[h] * o[h], axis=-1, keepdims=True) for h in hh]
        for h in hh:
            s_out_ref[i, h] = s1[h]
            o_ref[i, h:h + 1, :] = o[h] * lax.rsqrt(ms[h] + RMS_EPS) * nw_ref[...] * _silu(z_ref[i, h:h + 1, :])


def _gdn_sample_step(s_all, layer, prev_new, cols, v, z, sc, nw):
    n_layers, n, heads, dk, dv = s_all.shape
    nb = SEQ_PER_STEP
    assert n % nb == 0
    last = layer == n_layers - 1
    n_prev = len(prev_new) if last else 0
    one = pl.BlockSpec((nb, heads, dk, dv), lambda b: (b, 0, 0, 0))
    if n_prev:
        out_state = pl.BlockSpec((n_layers, nb, heads, dk, dv), lambda b: (0, b, 0, 0, 0))
        out_shape = jax.ShapeDtypeStruct(s_all.shape, F32)
    else:
        out_state, out_shape = one, jax.ShapeDtypeStruct(s_all.shape[1:], F32)
    return pl.pallas_call(
        functools.partial(_gdn_sample_step_kernel, heads=heads, nb=nb, n_prev=n_prev),
        grid=(n // nb,),
        in_specs=[pl.BlockSpec((None, nb, heads, dk, dv), lambda b: (layer, b, 0, 0, 0)),
                  _seq_block(cols, nb), _seq_block(v, nb), _seq_block(z, nb), _seq_block(sc, nb),
                  pl.BlockSpec((1, dv), lambda b: (0, 0))] + [one] * n_prev,
        out_specs=[out_state, _seq_block(v, nb)],
        out_shape=[out_shape, jax.ShapeDtypeStruct(v.shape, F32)],
        compiler_params=_cparams(("parallel",)),
        name="gdn_sample_step",
    )(s_all, cols, v, z, sc, nw, *(prev_new if n_prev else []))


def _gdn_sample_layer(x, g, p, conv0, s_all, layer, prev_new, *, heads, dk, dv):
    n = x.shape[0]
    key, val = heads * dk, heads * dv
    ch = 2 * key + val
    pm, pba = _in_proj(x, g, p["w_main"], p["w_ba"])
    qkv, conv_t, sc = _gdn_sample_pre(pm, pba, jnp.transpose(conv0, (1, 0, 2)), p["cw_t"], p["gp"],
                                      heads=heads, dk=dk, dv=dv)
    q_c = jnp.transpose(qkv[:, :key].reshape(n, heads, dk), (0, 2, 1))
    k_c = jnp.transpose(qkv[:, key:2 * key].reshape(n, heads, dk), (0, 2, 1))
    cols = jnp.concatenate([k_c, q_c], axis=-1)
    sc3 = jnp.stack([sc[:, :heads], sc[:, heads:2 * heads]], axis=-1)
    s_new, o = _gdn_sample_step(s_all, layer, prev_new, cols, qkv[:, 2 * key:].reshape(n, heads, dv),
                                pm[:, ch:].reshape(n, heads, dv), sc3, p["nw"])
    y = _out_proj(o.reshape(n, val), p["w_out"], x)
    return y, jnp.transpose(conv_t, (1, 0, 2)), s_new


def _mlstm_sample_step_kernel(c0_ref, n0_ref, cols_ref, q_ref, k_ref, v_ref, op_ref, sc_ref, bif_ref, nw_ref,
                              c_out_ref, n_out_ref, m_out_ref, o_ref, *, heads, dk, nb):
    scale = dk ** -0.5
    for i in range(nb):
        gi = sc_ref[i, :, 0:1] + bif_ref[:, 0:1]
        gf = sc_ref[i, :, 1:2] + bif_ref[:, 1:2]
        m0 = sc_ref[i, :, 2:3]
        gi = GATE_CAP * jnp.tanh(gi / GATE_CAP)
        logf = _log_sigmoid(GATE_CAP * jnp.tanh(gf / GATE_CAP))
        m_new = jnp.maximum(logf + m0, gi)
        f_s = jnp.exp(logf + m0 - m_new)
        i_s = jnp.exp(gi - m_new)
        m_out_ref[i] = m_new
        n1 = f_s * n0_ref[i] + i_s * k_ref[i]
        n_out_ref[i] = n1
        den = jnp.sum(q_ref[i] * scale * n1, axis=-1, keepdims=True)
        floor = jnp.exp(-m_new)
        hh = range(heads)
        kc = [cols_ref[i, :, h:h + 1] for h in hh]
        qc = [cols_ref[i, :, heads + h:heads + h + 1] * scale for h in hh]
        c1 = [f_s[h:h + 1, :] * c0_ref[i, h] + i_s[h:h + 1, :] * (kc[h] * v_ref[i, h:h + 1, :]) for h in hh]
        num = [jnp.sum(qc[h] * c1[h], axis=0, keepdims=True) for h in hh]
        h_t = [num[h] / jnp.maximum(jnp.abs(den[h:h + 1, :]), floor[h:h + 1, :]) for h in hh]
        ms = [jnp.mean(h_t[h] * h_t[h], axis=-1, keepdims=True) for h in hh]
        for h in hh:
            c_out_ref[i, h] = c1[h]
            h_n = h_t[h] * lax.rsqrt(ms[h] + RMS_EPS) * nw_ref[h:h + 1, :]
            o_ref[i, h:h + 1, :] = _sigmoid(op_ref[i, h:h + 1, :]) * h_n


def _mlstm_sample_step(c0, n0, cols, q, k, v, o_pre, sc, bif2, nw2):
    n, heads, dk, dv = c0.shape
    nb = SEQ_PER_STEP
    assert n % nb == 0
    full = _const_block

    def blk(a):
        return _seq_block(a, nb)

    m_shape = (n, heads, 1)
    return pl.pallas_call(
        functools.partial(_mlstm_sample_step_kernel, heads=heads, dk=dk, nb=nb),
        grid=(n // nb,),
        in_specs=[blk(c0), blk(n0), blk(cols), blk(q), blk(k), blk(v), blk(o_pre), blk(sc), full(bif2), full(nw2)],
        out_specs=[blk(c0), blk(n0), pl.BlockSpec((nb, heads, 1), lambda b: (b, 0, 0)), blk(v)],
        out_shape=[jax.ShapeDtypeStruct(c0.shape, F32), jax.ShapeDtypeStruct(n0.shape, F32),
                   jax.ShapeDtypeStruct(m_shape, F32), jax.ShapeDtypeStruct(v.shape, F32)],
        compiler_params=_cparams(("parallel",)),
        name="mlstm_sample_step",
    )(c0, n0, cols, q, k, v, o_pre, sc, bif2, nw2)


def _mlstm_sample_layer(x, g, p, c0, n0, m0, *, heads, dk, dv):
    n = x.shape[0]
    qk_w, val = heads * dk, heads * dv
    pm, pif = _in_proj(x, g, p["w_main"], p["w_if"])
    q = pm[:, :qk_w].reshape(n, heads, dk)
    k = pm[:, qk_w:2 * qk_w].reshape(n, heads, dk)
    v = pm[:, 2 * qk_w:2 * qk_w + val].reshape(n, heads, dv)
    o_pre = pm[:, 2 * qk_w + val:].reshape(n, heads, dv)
    cols = jnp.concatenate([jnp.transpose(k, (0, 2, 1)), jnp.transpose(q, (0, 2, 1))], axis=-1)
    sc = jnp.stack([pif[:, :heads], pif[:, heads:2 * heads], m0], axis=-1)
    bif2 = jnp.stack([p["bif"][0, :heads], p["bif"][0, heads:2 * heads]], axis=-1)
    c1, n1, m1, o = _mlstm_sample_step(c0, n0, cols, q, k, v, o_pre, sc, bif2, p["nw"].reshape(heads, dv))
    y = _out_proj(o.reshape(n, val), p["w_out"], x)
    return y, c1, n1, m1[:, :, 0]


RW_ROW_GROUP = 8


def _rwkv_sample_step_kernel(s_ref, r_ref, k_ref, lw_ref, kk_ref, a_ref, v_ref, g_ref, hp_ref,
                             s_out_ref, o_ref, y_ref, *, hd):
    kk = kk_ref[0]
    kk = kk * lax.rsqrt(jnp.maximum(jnp.sum(kk * kk, axis=0, keepdims=True), 1e-24))
    av = -kk
    bv = kk * a_ref[0]
    w = jnp.exp(lw_ref[0])
    r = r_ref[0]
    k = k_ref[0]
    v = v_ref[0]
    for v0 in range(0, hd, RW_ROW_GROUP):
        vv = range(v0, v0 + RW_ROW_GROUP)
        s0 = [s_ref[0, i] for i in vv]
        sa = [jnp.sum(s * av, axis=0, keepdims=True) for s in s0]
        s1 = [s * w + sa_i * bv + v[i:i + 1, :] * k for s, sa_i, i in zip(s0, sa, vv)]
        y = [jnp.sum(s * r, axis=0, keepdims=True) for s in s1]
        for i, s, y_i in zip(vv, s1, y):
            s_out_ref[0, i] = s
            y_ref[i:i + 1, :] = y_i
    y = y_ref[...]
    yc = y - jnp.mean(y, axis=0, keepdims=True)
    var = jnp.mean(yc * yc, axis=0, keepdims=True)
    hp = hp_ref[0]
    yn = yc * lax.rsqrt(var + RW_GN_EPS) * hp[:, 1:2] + hp[:, 2:3]
    bonus = jnp.sum(r * k * hp[:, 0:1], axis=0, keepdims=True)
    o_ref[0] = (yn + bonus * v) * g_ref[0]


def _rwkv_sample_step(s_t, r, k, lw, kk, a, v, g, hp3):
    heads, hd, _, n = s_t.shape

    def blk(z):
        nd = z.ndim
        return pl.BlockSpec((1,) + z.shape[1:], lambda h: (h,) + (0,) * (nd - 1))

    return pl.pallas_call(
        functools.partial(_rwkv_sample_step_kernel, hd=hd),
        grid=(heads,),
        in_specs=[blk(s_t)] + [blk(z) for z in (r, k, lw, kk, a, v, g, hp3)],
        out_specs=[blk(s_t), blk(v)],
        out_shape=[jax.ShapeDtypeStruct(s_t.shape, F32), jax.ShapeDtypeStruct(v.shape, F32)],
        scratch_shapes=[pltpu.VMEM((hd, n), F32)],
        compiler_params=_cparams(("parallel",)),
        name="rwkv_sample_step",
    )(s_t, r, k, lw, kk, a, v, g, hp3)


def _rwkv_sample_layer(x, g_norm, p, shift0, s0, *, heads, hd):
    n, d = x.shape
    (r, k, v, lw, kk, a, g), hn = _rwkv_proj(x, g_norm, shift0, p, batch=n, seq=1, tm=n)

    def lanes(z):
        return z.T.reshape(heads, hd, n)

    hp = p["hp"]
    hp3 = jnp.stack([hp[j].reshape(heads, hd) for j in range(3)], axis=-1)
    s1_t, o_t = _rwkv_sample_step(jnp.transpose(s0, (1, 2, 3, 0)), lanes(r), lanes(k), lanes(lw), lanes(kk),
                                  lanes(a), lanes(v), lanes(g), hp3)
    y = _out_proj(o_t.reshape(d, n).T, p["w_o"], x)
    return y, hn, jnp.transpose(s1_t, (3, 0, 1, 2))


def _pad_cols(a, n):
    return jnp.pad(a, ((0, 0), (0, n - a.shape[1])))


def _gdn_prep(w_in, conv_w, a_log, dt_bias, norm_w, w_out, *, heads, dk, dv):
    key, val = heads * dk, heads * dv
    ch = 2 * key + val
    main = ch + val
    gp = jnp.zeros((8, 128), F32)
    gp = gp.at[0, heads:2 * heads].set(a_log).at[1, heads:2 * heads].set(dt_bias)
    return dict(w_main=w_in[:, :main].astype(BF16),
                w_ba=_pad_cols(w_in[:, main:], 128).astype(BF16),
                cw_t=jnp.pad(conv_w.T, ((0, 8 - CONV_W), (0, 0))),
                gp=gp, nw=norm_w[None, :], w_out=w_out.astype(BF16))


def _gdn_prompt_layer(x, g, p, conv0, s0, *, batch, seq, heads, dk, dv, norm=True, residual=True):
    conv0 = jnp.pad(conv0, ((0, 0), (8 - (CONV_W - 1), 0), (0, 0)))
    qkvz, pba, conv = _gdn_in_proj(x, g, p["w_main"], p["w_ba"], conv0, p["cw_t"], batch=batch, seq=seq,
                                   heads=heads, dk=dk, dv=dv, tm=TM_GDN_PROJ)
    o, s = _gdn_prompt(qkvz, pba, s0, p["gp"], p["nw"], batch=batch, seq=seq, heads=heads, dk=dk, dv=dv, tb=TB_MIX)
    y = _out_proj(o, p["w_out"], x if residual else None)
    return y, conv[:, 8 - (CONV_W - 1):, :], s


def _trunk(x, states, w, *, batch, seq):
    conv_in, gs_in, c_in, n_in, m_in, shift_in, rs_in = states
    depth = w["norm_mix"].shape[0]
    gh, gdk, gdv = gs_in.shape[2:]
    mh, mdk, mdv = c_in.shape[2:]
    rh, rhd = rs_in.shape[2:4]
    prompt = seq > 1
    outs = [[] for _ in range(7)]
    gs_new = []
    for i in range(depth):
        j = i // 3
        g = w["norm_mix"][i][None, :]
        if i % 3 == 0:
            p = w["gdn"][j]
            if prompt:
                x, cb, s = _gdn_prompt_layer(x, g, p, conv_in[j], gs_in[j], batch=batch, seq=seq,
                                             heads=gh, dk=gdk, dv=gdv)
                outs[1].append(s)
            else:
                x, cb, s = _gdn_sample_layer(x, g, p, conv_in[j], gs_in, j, gs_new, heads=gh, dk=gdk, dv=gdv)
                gs_new.append(s)
            outs[0].append(cb)
        elif i % 3 == 1:
            p = w["ml"][j]
            if prompt:
                x, c, n, m = _mlstm_prompt_layer(x, g, p, c_in[j], n_in[j], m_in[j], batch=batch, seq=seq,
                                                 heads=mh, dk=mdk, dv=mdv)
            else:
                x, c, n, m = _mlstm_sample_layer(x, g, p, c_in[j], n_in[j], m_in[j], heads=mh, dk=mdk, dv=mdv)
            outs[2].append(c)
            outs[3].append(n)
            outs[4].append(m)
        else:
            p = w["rw"][j]
            if prompt:
                x, sh, s = _rwkv_prompt_layer(x, g, p, shift_in[j], rs_in[j], batch=batch, seq=seq,
                                              heads=rh, hd=rhd)
            else:
                x, sh, s = _rwkv_sample_layer(x, g, p, shift_in[j], rs_in[j], heads=rh, hd=rhd)
            outs[5].append(sh)
            outs[6].append(s)
        g_out = w["norm_final"][None, :] if i == depth - 1 else None
        x = _ffn(x, w["norm_ffn"][i][None, :], w["ffn_w1"], w["ffn_w2"], i, g_out, tm=TM_FFN, tf=TF_FFN)
    y = x
    new = [jnp.stack(z, axis=0) if z else None for z in outs]
    if not prompt:
        new[1] = gs_new[-1] if len(gs_new) > 1 else gs_new[0][None]
    return y, tuple(new)


def kernel(x_prompt, x_sample, state_gdn_conv, state_gdn_S, state_mlstm_C, state_mlstm_n, state_mlstm_m, state_rwkv_shift, state_rwkv_S, norm_mix, norm_ffn, norm_final, gdn_w_in, gdn_conv_w, gdn_a_log, gdn_dt_bias, gdn_norm_w, gdn_w_out, ml_w_in, ml_b_if, ml_norm_w, ml_w_out, rw_mu, rw_w_rkv, rw_w_o, rw_w0, rw_w1, rw_w2, rw_a0, rw_a1, rw_a2, rw_g1, rw_g2, rw_k_k, rw_k_a, rw_r_k, rw_lnx_w, rw_lnx_b, ffn_w1, ffn_w2):
    gh, gdk, gdv = state_gdn_S.shape[2:]
    mh, mdk, mdv = state_mlstm_C.shape[2:]
    w = dict(
        norm_mix=norm_mix, norm_ffn=norm_ffn, norm_final=norm_final,
        ffn_w1=ffn_w1.astype(BF16), ffn_w2=ffn_w2.astype(BF16),
        gdn=[_gdn_prep(gdn_w_in[j], gdn_conv_w[j], gdn_a_log[j], gdn_dt_bias[j], gdn_norm_w[j], gdn_w_out[j],
                       heads=gh, dk=gdk, dv=gdv) for j in range(gdn_w_in.shape[0])],
        ml=[_mlstm_prep(ml_w_in[j], ml_b_if[j], ml_norm_w[j], ml_w_out[j], heads=mh, dk=mdk, dv=mdv)
            for j in range(ml_w_in.shape[0])],
        rw=[_rwkv_prep(rw_mu[j], rw_w_rkv[j], rw_w_o[j], rw_w0[j], rw_w1[j], rw_w2[j], rw_a0[j], rw_a1[j],
                       rw_a2[j], rw_g1[j], rw_g2[j], rw_k_k[j], rw_k_a[j], rw_r_k[j], rw_lnx_w[j], rw_lnx_b[j])
            for j in range(rw_mu.shape[0])])
    sample_states = (state_gdn_conv, state_gdn_S, state_mlstm_C, state_mlstm_n, state_mlstm_m,
                     state_rwkv_shift, state_rwkv_S)
    bp, tp, d = x_prompt.shape
    bs, ts, _ = x_sample.shape
    assert ts == 1
    prompt_states = tuple(jnp.zeros((s.shape[0], bp) + s.shape[2:], s.dtype) for s in sample_states)
    y_p, new_p = _trunk(x_prompt.reshape(bp * tp, d), prompt_states, w, batch=bp, seq=tp)
    y_s, new_s = _trunk(x_sample.reshape(bs * ts, d), sample_states, w, batch=bs, seq=ts)
    out = [y_p.reshape(bp, tp, d), y_s.reshape(bs, ts, d)]
    for a, b in zip(new_p, new_s):
        out += [a, b]
    return tuple(out)
```

```python
import functools

import jax
import jax.numpy as jnp
from jax import lax
from jax.experimental import pallas as pl
from jax.experimental.pallas import tpu as pltpu

F32 = jnp.float32
BF16 = jnp.bfloat16

RMS_EPS = 1e-6
NEG_BIG = -1e30
GATE_CAP = 15.0
RW_GN_EPS = 64e-5
CONV_W = 4
CHUNK = 64
RW_GROUP = 8
V7X_VMEM_LIMIT = 56 * 1024 * 1024
HI = lax.Precision.HIGHEST


def _cparams(sem):
    return pltpu.CompilerParams(dimension_semantics=sem, vmem_limit_bytes=V7X_VMEM_LIMIT)


def _dot(a, b):
    return jnp.dot(a.astype(BF16), b.astype(BF16), preferred_element_type=F32)


def _dot_nt(a, b):
    return lax.dot_general(a.astype(BF16), b.astype(BF16), (((1,), (1,)), ((), ())),
                           preferred_element_type=F32)


def _dot_tn(a, b):
    return lax.dot_general(a.astype(BF16), b.astype(BF16), (((0,), (0,)), ((), ())),
                           preferred_element_type=F32)


def _dot_hi(a, b):
    return jnp.dot(a, b, preferred_element_type=F32, precision=HI)


def _sigmoid(x):
    return 1.0 / (1.0 + jnp.exp(-x))


def _silu(x):
    return x * _sigmoid(x)


def _softplus(x):
    return jnp.maximum(x, 0.0) + jnp.log(1.0 + jnp.exp(-jnp.abs(x)))


def _log_sigmoid(x):
    return -_softplus(-x)


def _tri_masks(l):
    r = lax.broadcasted_iota(jnp.int32, (l, l), 0)
    c = lax.broadcasted_iota(jnp.int32, (l, l), 1)
    return r >= c, r > c


INV_BASE = 16


def _unit_lower_inverse(mats, l):
    n = mats[0].shape[0]
    r = lax.broadcasted_iota(jnp.int32, (n, n), 0)
    c = lax.broadcasted_iota(jnp.int32, (n, n), 1)
    eye = (r == c).astype(F32)
    size = min(INV_BASE, l)
    shift = size.bit_length() - 1
    diag = (r >> shift) == (c >> shift)
    merges = []
    s = size
    while s < l:
        sh = s.bit_length() - 1
        off = ((r >> (sh + 1)) == (c >> (sh + 1))) & ((r >> sh) > (c >> sh))
        merges.append([jnp.where(off, a, 0.0).astype(BF16) for a in mats])
        s *= 2
    t = [eye - jnp.where(diag, a, 0.0) for a in mats]
    tb = [ti.astype(BF16) for ti in t]
    ab = [jnp.where(diag, a, 0.0).astype(BF16) for a in mats]
    p = [_dot(a, a).astype(BF16) for a in ab]
    k = 2
    while k < size:
        if 2 * k < size:
            both = [_dot(pi, jnp.concatenate([ti, pi], axis=1)) for ti, pi in zip(tb, p)]
            t = [ti + bi[:, :n] for ti, bi in zip(t, both)]
            p = [bi[:, n:].astype(BF16) for bi in both]
        else:
            t = [ti + _dot(pi, ti_b) for ti, ti_b, pi in zip(t, tb, p)]
        tb = [ti.astype(BF16) for ti in t]
        k *= 2
    for a_off in merges:
        x = [_dot(a, ti) for a, ti in zip(a_off, tb)]
        y = [_dot(ti, xi) for ti, xi in zip(tb, x)]
        t = [ti - yi for ti, yi in zip(t, y)]
        tb = [ti.astype(BF16) for ti in t]
    return tb


TM_IN_PROJ = 512
TM_GDN_PROJ = 512
TM_FFN, TF_FFN = 1024, 1024
TB_MIX = 256
TM_RWKV_PROJ = 256


def _norm_rows(x, g):
    return x * lax.rsqrt(jnp.mean(x * x, axis=-1, keepdims=True) + RMS_EPS) * g


def _in_proj_kernel(x_ref, g_ref, w_ref, wa_ref, o_ref, oa_ref):
    xn = _norm_rows(x_ref[...], g_ref[...]).astype(BF16)
    oa_ref[...] = jnp.dot(xn, wa_ref[...], preferred_element_type=F32)
    o_ref[...] = jnp.dot(xn, w_ref[...], preferred_element_type=F32)


def _in_proj(x, g, w_main, w_aux):
    m, k = x.shape
    n, na = w_main.shape[1], w_aux.shape[1]
    tm = min(TM_IN_PROJ, m)
    assert m % tm == 0
    return pl.pallas_call(
        _in_proj_kernel,
        grid=(m // tm,),
        in_specs=[pl.BlockSpec((tm, k), lambda i: (i, 0)), pl.BlockSpec((1, k), lambda i: (0, 0)),
                  pl.BlockSpec((k, n), lambda i: (0, 0)), pl.BlockSpec((k, na), lambda i: (0, 0))],
        out_specs=[pl.BlockSpec((tm, n), lambda i: (i, 0)), pl.BlockSpec((tm, na), lambda i: (i, 0))],
        out_shape=[jax.ShapeDtypeStruct((m, n), F32), jax.ShapeDtypeStruct((m, na), F32)],
        compiler_params=_cparams(("parallel",)),
        name="in_proj",
    )(x, g, w_main, w_aux)


def _ffn_kernel(res_ref, mix_ref, wo_ref, g_ref, w1_ref, w2_ref, go_ref, o_ref, xn_ref, acc_ref, *, out_norm):
    f = pl.program_id(1)

    @pl.when(f == 0)
    def _():
        x = res_ref[...] + jnp.dot(mix_ref[...].astype(BF16), wo_ref[...], preferred_element_type=F32)
        o_ref[...] = x
        xn_ref[...] = _norm_rows(x, g_ref[...]).astype(BF16)
        acc_ref[...] = jnp.zeros_like(acc_ref)

    h = jnp.dot(xn_ref[...], w1_ref[...], preferred_element_type=F32)
    a = jnp.square(jnp.maximum(h, 0.0)).astype(BF16)
    acc_ref[...] += jnp.dot(a, w2_ref[...], preferred_element_type=F32)

    @pl.when(f == pl.num_programs(1) - 1)
    def _():
        y = o_ref[...] + acc_ref[...]
        o_ref[...] = _norm_rows(y, go_ref[...]) if out_norm else y


def _ffn(res, mix, w_out, g, w1, w2, layer, g_out=None, *, tm, tf):
    m, d = res.shape
    dff = w1.shape[2]
    tm, tf = min(tm, m), min(tf, dff)
    assert m % tm == 0 and dff % tf == 0
    vec = pl.BlockSpec((1, d), lambda i, j: (0, 0))
    return pl.pallas_call(
        functools.partial(_ffn_kernel, out_norm=g_out is not None),
        grid=(m // tm, dff // tf),
        in_specs=[pl.BlockSpec((tm, d), lambda i, j: (i, 0)),
                  pl.BlockSpec((tm, mix.shape[1]), lambda i, j: (i, 0)),
                  pl.BlockSpec(w_out.shape, lambda i, j: (0, 0)), vec,
                  pl.BlockSpec((None, d, tf), lambda i, j: (layer, 0, j)),
                  pl.BlockSpec((None, tf, d), lambda i, j: (layer, j, 0)), vec],
        out_specs=pl.BlockSpec((tm, d), lambda i, j: (i, 0)),
        out_shape=jax.ShapeDtypeStruct((m, d), F32),
        scratch_shapes=[pltpu.VMEM((tm, d), BF16), pltpu.VMEM((tm, d), F32)],
        compiler_params=_cparams(("parallel", "arbitrary")),
        name="ffn",
    )(res, mix, w_out, g, w1, w2, g if g_out is None else g_out)


def _gdn_prompt_kernel(qkv_ref, pba_ref, s0_ref, gp_ref, nw_ref,
                       o_ref, s_out_ref,
                       s_ref, u_ref, w_ref, qd_ref, kd_ref, qk_ref, gl_ref,
                       *, tb, heads, dk, dv):
    t = pl.program_id(1)
    key = heads * dk
    ch = 2 * key + heads * dv
    l = CHUNK

    @pl.when(t == 0)
    def _():
        s_ref[...] = s0_ref[0]

    a_log = gp_ref[0:1, :]
    dt_bias = gp_ref[1:2, :]
    hh = range(heads)
    l2 = 2 * l
    r2 = lax.broadcasted_iota(jnp.int32, (l2, l2), 0)
    c2 = lax.broadcasted_iota(jnp.int32, (l2, l2), 1)
    same = (r2 >= l) == (c2 >= l)
    incl = same & (r2 >= c2)
    strict = same & (r2 > c2)
    tril = incl.astype(F32)
    first = lax.broadcasted_iota(jnp.int32, (l2, 128), 0) < l

    pairs = range(tb // l2)
    rows = [slice(pi * l2, (pi + 1) * l2) for pi in pairs]
    beta_all, gc, gc_t, g_end = [], [], [], []
    for pi in pairs:
        ba = pba_ref[rows[pi], :]
        beta_all.append(_sigmoid(ba))
        g_all = -jnp.exp(a_log) * _softplus(ba + dt_bias)
        gc.append(_dot_hi(tril, g_all))
        gc_t.append(gc[pi].T)
        g_end.append(jnp.where(first, gc[pi][l - 1:l, :], gc[pi][l2 - 1:l2, :]))
        gl_ref[2 * pi:2 * pi + 1, :] = jnp.exp(gc[pi][l - 1:l, :])
        gl_ref[2 * pi + 1:2 * pi + 2, :] = jnp.exp(gc[pi][l2 - 1:l2, :])
    cc = [(pi, h) for pi in pairs for h in hh]
    b_col = [beta_all[pi][:, h:h + 1] for pi, h in cc]
    gi = [gc[pi][:, heads + h:heads + h + 1] for pi, h in cc]
    q = [qkv_ref[rows[pi], h * dk:(h + 1) * dk] for pi, h in cc]
    k = [qkv_ref[rows[pi], key + h * dk:key + (h + 1) * dk] for pi, h in cc]
    v = [qkv_ref[rows[pi], 2 * key + h * dv:2 * key + (h + 1) * dv] for pi, h in cc]
    nc = range(len(cc))
    dmat = [jnp.where(incl, jnp.exp(jnp.where(incl, gi[i] - gc_t[pi][heads + h:heads + h + 1, :], 0.0)), 0.0)
            for i, (pi, h) in enumerate(cc)]
    kb = [k[i] * b_col[i] for i in nc]
    kk = [_dot_nt(kb[i], k[i]) for i in nc]
    qk = [_dot_nt(q[i], k[i]) for i in nc]
    t_inv = _unit_lower_inverse([jnp.where(strict, kk[i] * dmat[i], 0.0) for i in nc], l)
    egi = [jnp.exp(gi[i]) for i in nc]
    sol = [_dot(t_inv[i], jnp.concatenate([v[i] * b_col[i], kb[i] * egi[i]], axis=-1)) for i in nc]
    for i, (pi, h) in enumerate(cc):
        hs = slice(h * dk, (h + 1) * dk)
        u_ref[rows[pi], h * dv:(h + 1) * dv] = sol[i][:, :dv]
        w_ref[rows[pi], hs] = sol[i][:, dv:].astype(BF16)
        qd_ref[rows[pi], hs] = (q[i] * egi[i]).astype(BF16)
        kd_ref[rows[pi], hs] = (k[i] * jnp.exp(g_end[pi][:, heads + h:heads + h + 1] - gi[i])).astype(BF16)
        qkm = jnp.where(incl, qk[i] * dmat[i], 0.0).astype(BF16)
        qk_ref[h, pi * l2:pi * l2 + l, :] = qkm[:l, :l]
        qk_ref[h, pi * l2 + l:(pi + 1) * l2, :] = qkm[l:, l:]

    for ci in range(tb // l):
        rows = slice(ci * l, (ci + 1) * l)
        s = [s_ref[h] for h in hh]
        wq = [_dot(jnp.concatenate([w_ref[rows, h * dk:(h + 1) * dk], qd_ref[rows, h * dk:(h + 1) * dk]], axis=0),
                   s[h]) for h in hh]
        v_new = [u_ref[rows, h * dv:(h + 1) * dv] - wq[h][:l] for h in hh]
        o2 = [_dot(qk_ref[h, rows, :], v_new[h]) for h in hh]
        ds = [_dot_tn(kd_ref[rows, h * dk:(h + 1) * dk], v_new[h]) for h in hh]
        for h in hh:
            s_ref[h] = s[h] * gl_ref[ci:ci + 1, heads + h:heads + h + 1] + ds[h]
            o = wq[h][l:] + o2[h]
            z = qkv_ref[rows, ch + h * dv:ch + (h + 1) * dv]
            o = o * lax.rsqrt(jnp.mean(o * o, axis=-1, keepdims=True) + RMS_EPS) * nw_ref[...]
            o_ref[rows, h * dv:(h + 1) * dv] = o * _silu(z)

    @pl.when(t == pl.num_programs(1) - 1)
    def _():
        s_out_ref[0] = s_ref[...]


def _gdn_prompt(qkvz, pba, s0, gp, nw, *, batch, seq, heads, dk, dv, tb):
    key, val = heads * dk, heads * dv
    ch = 2 * key + val
    tb = min(tb, seq)
    assert seq % tb == 0 and tb % (2 * CHUNK) == 0
    nt = seq // tb
    return pl.pallas_call(
        functools.partial(_gdn_prompt_kernel, tb=tb, heads=heads, dk=dk, dv=dv),
        grid=(batch, nt),
        in_specs=[pl.BlockSpec((tb, ch + val), lambda b, t: (b * nt + t, 0)),
                  pl.BlockSpec((tb, 128), lambda b, t: (b * nt + t, 0)),
                  pl.BlockSpec((1, heads, dk, dv), lambda b, t: (b, 0, 0, 0)),
                  pl.BlockSpec((8, 128), lambda b, t: (0, 0)),
                  pl.BlockSpec((1, dv), lambda b, t: (0, 0))],
        out_specs=[pl.BlockSpec((tb, val), lambda b, t: (b * nt + t, 0)),
                   pl.BlockSpec((1, heads, dk, dv), lambda b, t: (b, 0, 0, 0))],
        out_shape=[jax.ShapeDtypeStruct((batch * seq, val), F32),
                   jax.ShapeDtypeStruct((batch, heads, dk, dv), F32)],
        scratch_shapes=[pltpu.VMEM((heads, dk, dv), F32),
                        pltpu.VMEM((tb, val), F32), pltpu.VMEM((tb, key), BF16),
                        pltpu.VMEM((tb, key), BF16), pltpu.VMEM((tb, key), BF16),
                        pltpu.VMEM((heads, tb, CHUNK), BF16), pltpu.VMEM((max(8, tb // CHUNK), 128), F32)],
        compiler_params=_cparams(("parallel", "arbitrary")),
        name="gdn_prompt",
    )(qkvz, pba, s0, gp, nw)


GDN_PROJ_COLS = 256
GDN_CONV_ROWS = 64


def _gdn_in_proj_kernel(x_ref, g_ref, w_ref, wa_ref, conv0_ref, cw_ref, o_ref, aux_ref, conv_out_ref,
                        xn_ref, carry_ref, *, heads, dk, dv):
    t = pl.program_id(1)
    key = heads * dk
    ch = 2 * key + heads * dv
    tm = x_ref.shape[0]
    n = w_ref.shape[1]

    @pl.when(t == 0)
    def _():
        carry_ref[...] = conv0_ref[0]

    xn_ref[...] = _norm_rows(x_ref[...], g_ref[...]).astype(BF16)
    aux_ref[...] = jnp.dot(xn_ref[...], wa_ref[...], preferred_element_type=F32)
    for c0 in range(0, n, GDN_PROJ_COLS):
        cs = slice(c0, c0 + GDN_PROJ_COLS)
        y = jnp.dot(xn_ref[...], w_ref[:, cs], preferred_element_type=F32)
        if c0 >= ch:
            o_ref[:, cs] = y
            continue
        ext = jnp.concatenate([carry_ref[:, cs], y], axis=0)
        carry_ref[:, cs] = y[tm - 8:, :]
        for r0 in range(0, tm, GDN_CONV_ROWS):
            for j in range(0, GDN_PROJ_COLS, dk):
                cj = slice(c0 + j, c0 + j + dk)
                blk = ext[r0:r0 + GDN_CONV_ROWS + 8, j:j + dk]
                z = blk[8:] * cw_ref[CONV_W - 1:CONV_W, cj]
                for s in range(1, CONV_W):
                    z = z + pltpu.roll(blk, s, 0)[8:] * cw_ref[CONV_W - 1 - s:CONV_W - s, cj]
                z = _silu(z)
                if c0 + j < key:
                    z = z * lax.rsqrt(jnp.sum(z * z, axis=-1, keepdims=True) + 1e-6) * (dk ** -0.5)
                elif c0 + j < 2 * key:
                    z = z * lax.rsqrt(jnp.sum(z * z, axis=-1, keepdims=True) + 1e-6)
                o_ref[r0:r0 + GDN_CONV_ROWS, cj] = z

    @pl.when(t == pl.num_programs(1) - 1)
    def _():
        conv_out_ref[0] = carry_ref[...]


def _gdn_in_proj(x, g, w_main, w_ba, conv0, cw_t, *, batch, seq, heads, dk, dv, tm):
    m, d = x.shape
    key, val = heads * dk, heads * dv
    ch = 2 * key + val
    n = w_main.shape[1]
    assert dk == dv and n % GDN_PROJ_COLS == 0 and ch % GDN_PROJ_COLS == 0 and GDN_PROJ_COLS % dk == 0
    tm = min(tm, seq)
    assert seq % tm == 0 and tm >= 8
    nt = seq // tm
    row = lambda width: pl.BlockSpec((tm, width), lambda b, t: (b * nt + t, 0))
    full = lambda a: pl.BlockSpec(a.shape, lambda b, t: (0,) * a.ndim)
    st = pl.BlockSpec((1, 8, ch), lambda b, t: (b, 0, 0))
    return pl.pallas_call(
        functools.partial(_gdn_in_proj_kernel, heads=heads, dk=dk, dv=dv),
        grid=(batch, nt),
        in_specs=[row(d), full(g), full(w_main), full(w_ba), st, full(cw_t)],
        out_specs=[row(n), row(w_ba.shape[1]), st],
        out_shape=[jax.ShapeDtypeStruct((m, n), F32), jax.ShapeDtypeStruct((m, w_ba.shape[1]), F32),
                   jax.ShapeDtypeStruct((batch, 8, ch), F32)],
        scratch_shapes=[pltpu.VMEM((tm, d), BF16), pltpu.VMEM((8, ch), F32)],
        compiler_params=_cparams(("parallel", "arbitrary")),
        name="gdn_in_proj",
    )(x, g, w_main, w_ba, conv0, cw_t)


def _mlstm_prompt_kernel(pm_ref, pif_ref, bif_ref, c0_ref, n0_ref, m0_ref, nw_ref,
                         o_ref, c_out_ref, n_out_ref, m_out_ref,
                         c_ref, n_ref, m_ref, bc_ref, ni_ref, col_ref, kv_ref, kc_ref, sc_ref,
                         cin_ref, nin_ref, min_ref, *, tb, heads, dk, dv):
    t = pl.program_id(1)
    l = CHUNK
    qk_w = heads * dk
    v_off = 2 * qk_w
    o_off = v_off + heads * dv

    @pl.when(t == 0)
    def _():
        c_ref[...] = c0_ref[0]
        n_ref[...] = n0_ref[0]
        m_ref[...] = m0_ref[0]

    incl, _ = _tri_masks(l)
    tril = incl.astype(F32)
    hh = range(heads)
    nchunk = tb // l
    scale = dk ** -0.5

    crow = [slice(ci * l, (ci + 1) * l) for ci in range(nchunk)]
    gates, bcum, bcum_t, gates_t = [], [], [], []
    for ci in range(nchunk):
        g = pif_ref[crow[ci], :] + bif_ref[...]
        g = GATE_CAP * jnp.tanh(g / GATE_CAP)
        gates.append(g)
        bcum.append(_dot_hi(tril, _log_sigmoid(g)))
        bcum_t.append(bcum[ci].T)
        gates_t.append(g.T)
        bc_ref[crow[ci], :] = bcum[ci]
    cc = [(ci, h) for ci in range(nchunk) for h in hh]
    nc = range(len(cc))
    bi = [bcum[ci][:, heads + h:heads + h + 1] for ci, h in cc]
    b_last = [bcum[ci][l - 1:l, heads + h:heads + h + 1] for ci, h in cc]
    k = [pm_ref[crow[ci], qk_w + h * dk:qk_w + (h + 1) * dk] for ci, h in cc]
    v = [pm_ref[crow[ci], v_off + h * dv:v_off + (h + 1) * dv].astype(BF16) for ci, h in cc]
    qk = [_dot_nt(pm_ref[crow[ci], h * dk:(h + 1) * dk] * scale, k[i]) for i, (ci, h) in enumerate(cc)]
    dlog = [jnp.where(incl, bi[i] - bcum_t[ci][heads + h:heads + h + 1, :] + gates_t[ci][h:h + 1, :], NEG_BIG)
            for i, (ci, h) in enumerate(cc)]
    m_intra = [jnp.max(dlog[i], axis=-1, keepdims=True) for i in nc]
    p = [jnp.where(incl, jnp.exp(dlog[i] - m_intra[i]), 0.0) * qk[i] for i in nc]
    den_intra = [jnp.sum(p[i], axis=-1, keepdims=True) for i in nc]
    num_intra = [_dot(p[i], v[i]) for i in nc]
    a_log = [b_last[i] - bi[i] + gates[ci][:, h:h + 1] for i, (ci, h) in enumerate(cc)]
    m_chunk = [jnp.max(a_log[i], axis=0, keepdims=True) for i in nc]
    kw = [k[i] * jnp.exp(a_log[i] - m_chunk[i]) for i in nc]
    kv_chunk = [_dot_tn(kw[i], v[i]) for i in nc]
    for i, (ci, h) in enumerate(cc):
        ni_ref[crow[ci], h * dv:(h + 1) * dv] = num_intra[i]
        col_ref[crow[ci], h:h + 1] = m_intra[i]
        col_ref[crow[ci], heads + h:heads + h + 1] = den_intra[i]
        kv_ref[i] = kv_chunk[i]
        kc_ref[i:i + 1, :] = jnp.sum(kw[i], axis=0, keepdims=True)
        sc_ref[i:i + 1, 0:1] = m_chunk[i]
        sc_ref[i:i + 1, 1:2] = b_last[i]

    for ci in range(nchunk):
        for h in hh:
            i = ci * heads + h
            c_mat = c_ref[h]
            n_vec = n_ref[h:h + 1, :]
            m_prev = m_ref[h:h + 1, :]
            cin_ref[i] = c_mat
            nin_ref[i:i + 1, :] = n_vec
            min_ref[i:i + 1, :] = m_prev
            m_chunk = sc_ref[i:i + 1, 0:1]
            b_last = sc_ref[i:i + 1, 1:2]
            m_new = jnp.maximum(b_last + m_prev, m_chunk)
            f_s = jnp.exp(b_last + m_prev - m_new)
            i_s = jnp.exp(m_chunk - m_new)
            c_ref[h] = f_s[:, 0:1] * c_mat + i_s[:, 0:1] * kv_ref[i]
            n_ref[h:h + 1, :] = f_s * n_vec + i_s * kc_ref[i:i + 1, :]
            m_ref[h:h + 1, :] = m_new

    q = [pm_ref[crow[ci], h * dk:(h + 1) * dk] * scale for ci, h in cc]
    qc = [_dot(q[i], cin_ref[i]) for i in nc]
    qn = [jnp.sum(q[i] * nin_ref[i:i + 1, :], axis=-1, keepdims=True) for i in nc]
    m_prev = [min_ref[i:i + 1, 0:1] for i in nc]
    bi = [bc_ref[crow[ci], heads + h:heads + h + 1] for ci, h in cc]
    m_in = [col_ref[crow[ci], h:h + 1] for ci, h in cc]
    m_t = [jnp.maximum(bi[i] + m_prev[i], m_in[i]) for i in nc]
    s_inter = [jnp.exp(bi[i] + m_prev[i] - m_t[i]) for i in nc]
    s_intra = [jnp.exp(m_in[i] - m_t[i]) for i in nc]
    den = [s_inter[i] * qn[i] + s_intra[i] * col_ref[crow[ci], heads + h:heads + h + 1]
           for i, (ci, h) in enumerate(cc)]
    h_t = [(s_inter[i] * qc[i] + s_intra[i] * ni_ref[crow[ci], h * dv:(h + 1) * dv])
           / jnp.maximum(jnp.abs(den[i]), jnp.exp(-m_t[i])) for i, (ci, h) in enumerate(cc)]
    ms = [jnp.mean(h_t[i] * h_t[i], axis=-1, keepdims=True) for i in nc]
    for i, (ci, h) in enumerate(cc):
        h_n = h_t[i] * lax.rsqrt(ms[i] + RMS_EPS) * nw_ref[:, h * dv:(h + 1) * dv]
        o_pre = pm_ref[crow[ci], o_off + h * dv:o_off + (h + 1) * dv]
        o_ref[crow[ci], h * dv:(h + 1) * dv] = _sigmoid(o_pre) * h_n

    @pl.when(t == pl.num_programs(1) - 1)
    def _():
        c_out_ref[0] = c_ref[...]
        n_out_ref[0] = n_ref[...]
        m_out_ref[0] = m_ref[...]


def _mlstm_prompt(pm, pif, bif, c0, n0, m0, nw, *, batch, seq, heads, dk, dv, tb):
    width = pm.shape[1]
    val = heads * dv
    tb = min(tb, seq)
    assert seq % tb == 0 and tb % CHUNK == 0
    nt = seq // tb
    nck = (tb // CHUNK) * heads
    return pl.pallas_call(
        functools.partial(_mlstm_prompt_kernel, tb=tb, heads=heads, dk=dk, dv=dv),
        grid=(batch, nt),
        in_specs=[pl.BlockSpec((tb, width), lambda b, t: (b * nt + t, 0)),
                  pl.BlockSpec((tb, 128), lambda b, t: (b * nt + t, 0)),
                  pl.BlockSpec((1, 128), lambda b, t: (0, 0)),
                  pl.BlockSpec((1, heads, dk, dv), lambda b, t: (b, 0, 0, 0)),
                  pl.BlockSpec((1, 8, dk), lambda b, t: (b, 0, 0)),
                  pl.BlockSpec((1, 8, 128), lambda b, t: (b, 0, 0)),
                  pl.BlockSpec((1, val), lambda b, t: (0, 0))],
        out_specs=[pl.BlockSpec((tb, val), lambda b, t: (b * nt + t, 0)),
                   pl.BlockSpec((1, heads, dk, dv), lambda b, t: (b, 0, 0, 0)),
                   pl.BlockSpec((1, 8, dk), lambda b, t: (b, 0, 0)),
                   pl.BlockSpec((1, 8, 128), lambda b, t: (b, 0, 0))],
        out_shape=[jax.ShapeDtypeStruct((batch * seq, val), F32),
                   jax.ShapeDtypeStruct((batch, heads, dk, dv), F32),
                   jax.ShapeDtypeStruct((batch, 8, dk), F32),
                   jax.ShapeDtypeStruct((batch, 8, 128), F32)],
        scratch_shapes=[pltpu.VMEM((heads, dk, dv), F32), pltpu.VMEM((8, dk), F32),
                        pltpu.VMEM((8, 128), F32),
                        pltpu.VMEM((tb, 128), F32), pltpu.VMEM((tb, val), F32), pltpu.VMEM((tb, 128), F32),
                        pltpu.VMEM((nck, dk, dv), F32), pltpu.VMEM((max(8, nck), dk), F32),
                        pltpu.VMEM((max(8, nck), 128), F32),
                        pltpu.VMEM((nck, dk, dv), F32), pltpu.VMEM((max(8, nck), dk), F32),
                        pltpu.VMEM((max(8, nck), 128), F32)],
        compiler_params=_cparams(("parallel", "arbitrary")),
        name="mlstm_prompt",
    )(pm, pif, bif, c0, n0, m0, nw)


def _mlstm_prep(w_in, b_if, norm_w, w_out, *, heads, dk, dv):
    main = 2 * heads * dk + 2 * heads * dv
    return dict(w_main=w_in[:, :main].astype(BF16),
                w_if=_pad_cols(w_in[:, main:], 128).astype(BF16),
                bif=_pad_cols(b_if[None, :], 128), nw=norm_w[None, :], w_out=w_out.astype(BF16))


def _mlstm_prompt_layer(x, g, p, c0, n0, m0, *, batch, seq, heads, dk, dv):
    pm, pif = _in_proj(x, g, p["w_main"], p["w_if"])
    n0p = jnp.pad(n0, ((0, 0), (0, 8 - heads), (0, 0)))
    m0p = jnp.broadcast_to(jnp.pad(m0, ((0, 0), (0, 8 - heads)))[:, :, None], (batch, 8, 128))
    o, c, n, m = _mlstm_prompt(pm, pif, p["bif"], c0, n0p, m0p, p["nw"],
                               batch=batch, seq=seq, heads=heads, dk=dk, dv=dv, tb=TB_MIX)
    return o, c, n[:, :heads, :], m[:, :heads, 0]


def _rwkv_proj_body(h, prev, mu_ref, wrkv_ref, w1_ref, w2_ref, a1_ref, a2_ref, g1_ref, g2_ref,
                    vec_ref, r_ref, k_ref, v_ref, lw_ref, kk_ref, a_ref, g_ref):
    xx = prev - h

    def mix(j):
        return (h + xx * mu_ref[j:j + 1, :]).astype(BF16)

    w0, a0, k_k, k_a = (vec_ref[j:j + 1, :] for j in range(4))
    r_ref[...] = jnp.dot(mix(0), wrkv_ref[0], preferred_element_type=F32)
    lora_w = _dot(jnp.tanh(_dot(mix(1), w1_ref[...])), w2_ref[...])
    w_log = -_softplus(-(w0 + lora_w)) - 0.5
    lw_ref[...] = -jnp.exp(w_log)
    k = jnp.dot(mix(2), wrkv_ref[1], preferred_element_type=F32)
    v_ref[...] = jnp.dot(mix(3), wrkv_ref[2], preferred_element_type=F32)
    a = _sigmoid(a0 + _dot(_dot(mix(4), a1_ref[...]), a2_ref[...]))
    g_ref[...] = _dot(_sigmoid(_dot(mix(5), g1_ref[...])), g2_ref[...])
    kk_ref[...] = k * k_k
    k_ref[...] = k * (1.0 + (a - 1.0) * k_a)
    a_ref[...] = a


def _rwkv_proj_sample_kernel(x_ref, gn_ref, prev_ref, *refs):
    h = _norm_rows(x_ref[...], gn_ref[...])
    hn_ref = refs[-1]
    hn_ref[...] = h
    _rwkv_proj_body(h, prev_ref[...], *refs[:-1])


def _rwkv_proj_prompt_kernel(x_ref, gn_ref, shift0_ref, *refs):
    carry_ref = refs[-1]
    shift_out_ref = refs[-2]
    t = pl.program_id(1)

    @pl.when(t == 0)
    def _():
        carry_ref[...] = shift0_ref[0]

    h = _norm_rows(x_ref[...], gn_ref[...])
    rows = h.shape[0]
    first = lax.broadcasted_iota(jnp.int32, h.shape, 0) == 0
    prev = jnp.where(first, carry_ref[0:1, :], pltpu.roll(h, 1, 0))
    carry_ref[0:1, :] = h[rows - 1:rows, :]
    _rwkv_proj_body(h, prev, *refs[:-2])

    @pl.when(t == pl.num_programs(1) - 1)
    def _():
        shift_out_ref[0] = carry_ref[...]


def _rwkv_proj(x, g_norm, shift0, p, *, batch, seq, tm):
    m, d = x.shape
    consts = [p["mu"], p["w_rkv"], p["w1"], p["w2"], p["a1"], p["a2"], p["g1"], p["g2"], p["vec"]]
    if seq == 1:
        row = pl.BlockSpec((m, d), lambda i: (0, 0))
        out = pl.pallas_call(
            _rwkv_proj_sample_kernel,
            grid=(1,),
            in_specs=[row, pl.BlockSpec((1, d), lambda i: (0, 0)), row] + [_const_block(a) for a in consts],
            out_specs=[row] * 8,
            out_shape=[jax.ShapeDtypeStruct((m, d), F32)] * 8,
            compiler_params=_cparams(("arbitrary",)),
            name="rwkv_proj_sample",
        )(x, g_norm, shift0, *consts)
        return out[:7], out[7]
    tm = min(tm, seq)
    assert seq % tm == 0
    nt = seq // tm
    row = pl.BlockSpec((tm, d), lambda b, t: (b * nt + t, 0))
    st = pl.BlockSpec((1, 8, d), lambda b, t: (b, 0, 0))

    def full(a):
        nd = a.ndim
        return pl.BlockSpec(a.shape, lambda b, t: (0,) * nd)

    shift0_p = jnp.pad(shift0[:, None, :], ((0, 0), (0, 7), (0, 0)))
    out = pl.pallas_call(
        _rwkv_proj_prompt_kernel,
        grid=(batch, nt),
        in_specs=[row, full(g_norm), st] + [full(a) for a in consts],
        out_specs=[row] * 7 + [st],
        out_shape=[jax.ShapeDtypeStruct((m, d), F32)] * 7 + [jax.ShapeDtypeStruct((batch, 8, d), F32)],
        scratch_shapes=[pltpu.VMEM((8, d), F32)],
        compiler_params=_cparams(("parallel", "arbitrary")),
        name="rwkv_proj_prompt",
    )(x, g_norm, shift0_p, *consts)
    return out[:7], out[7][:, 0, :]


def _rwkv_prompt_kernel(r_ref, k_ref, v_ref, lw_ref, kk_ref, a_ref, g_ref, s0_ref, hp_ref,
                        o_ref, s_out_ref, s_ref, rr_ref, yy_ref, mx_ref, n0_ref, gw_ref, gb_ref, el_ref,
                        *, tb, heads, hd):
    t = pl.program_id(1)
    l = CHUNK

    @pl.when(t == 0)
    def _():
        s_ref[...] = s0_ref[0]

    hh = range(heads)
    hs = [slice(h * hd, (h + 1) * hd) for h in hh]
    l2 = 2 * l
    r2 = lax.broadcasted_iota(jnp.int32, (l2, l2), 0)
    c2 = lax.broadcasted_iota(jnp.int32, (l2, l2), 1)
    same = (r2 >= l) == (c2 >= l)
    incl = same & (r2 >= c2)
    strict = same & (r2 > c2)
    tril = incl.astype(F32)
    first = lax.broadcasted_iota(jnp.int32, (l2, heads * hd), 0) < l
    first2 = (lax.broadcasted_iota(jnp.int32, (2 * l2, hd), 0) & l) == 0
    zeros = jnp.zeros((l2, hd), F32)

    def pair(pi, carry):
        rows = pl.ds(pl.multiple_of(pi * l2, l2), l2)
        lw = lw_ref[rows, :]
        lwc = _dot_hi(tril, lw)
        lw_end = jnp.where(first, lwc[l - 1:l, :], lwc[l2 - 1:l2, :])
        e_in = jnp.exp(lwc)
        e_prev = jnp.exp(lwc - lw)
        e_neg = jnp.exp(-lwc)
        e_end = jnp.exp(lw_end - lwc)
        r = [r_ref[rows, hs[h]] for h in hh]
        v = [v_ref[rows, hs[h]] for h in hh]
        k = [k_ref[rows, hs[h]] for h in hh]
        kk = [kk_ref[rows, hs[h]] for h in hh]
        kk = [kk[h] * lax.rsqrt(jnp.maximum(jnp.sum(kk[h] * kk[h], axis=-1, keepdims=True), 1e-24)) for h in hh]
        bv = [kk[h] * a_ref[rows, hs[h]] for h in hh]
        a_t = [-kk[h] * e_prev[:, hs[h]] for h in hh]
        r_t = [r[h] * e_in[:, hs[h]] for h in hh]
        gm = [_dot_nt(jnp.concatenate([a_t[h], r_t[h]], axis=0),
                      jnp.concatenate([bv[h] * e_neg[:, hs[h]], k[h] * e_neg[:, hs[h]]], axis=0)) for h in hh]
        ak_m = [jnp.where(strict, gm[h][:l2, l2:], 0.0).astype(BF16) for h in hh]
        rbk_m = [jnp.concatenate([jnp.where(incl, gm[h][l2:, :l2], 0.0),
                                  jnp.where(incl, gm[h][l2:, l2:], 0.0)], axis=1).astype(BF16) for h in hh]
        t_inv = _unit_lower_inverse([jnp.where(strict, -gm[h][:l2, :l2], 0.0) for h in hh], l)
        vb = [v[h].astype(BF16) for h in hh]
        akv = [_dot(ak_m[h], vb[h]) for h in hh]
        x1 = [_dot(t_inv[h], jnp.concatenate([a_t[h], akv[h]], axis=1)).astype(BF16) for h in hh]
        low = [jnp.concatenate([x1[h], jnp.concatenate([zeros.astype(BF16), vb[h]], axis=1)], axis=0) for h in hh]
        x2 = [_dot(rbk_m[h], low[h]) for h in hh]
        bk = [jnp.concatenate([bv[h] * e_end[:, hs[h]], k[h] * e_end[:, hs[h]]], axis=0) for h in hh]
        bk2 = [jnp.concatenate([jnp.where(first2, bk[h], 0.0), jnp.where(first2, 0.0, bk[h])], axis=1) for h in hh]
        mn = [_dot_tn(low[h], bk2[h]) for h in hh]
        for c in range(2):
            crow = pl.ds(pl.multiple_of(pi * l2 + c * l, l), l)
            for h in hh:
                mx_ref[h, crow, :] = mn[h][:hd, c * hd:(c + 1) * hd].astype(BF16)
                n0_ref[h, crow, :] = mn[h][hd:, c * hd:(c + 1) * hd]
        for h in hh:
            rr_ref[h, rows, :] = (r_t[h] + x2[h][:, :hd]).astype(BF16)
            yy_ref[h, rows, :] = x2[h][:, hd:]
            g = g_ref[rows, hs[h]]
            bonus = jnp.sum(r[h] * k[h] * hp_ref[0:1, hs[h]], axis=-1, keepdims=True) * v[h]
            gw_ref[h, rows, :] = hp_ref[1:2, hs[h]] * g
            gb_ref[h, rows, :] = (hp_ref[2:3, hs[h]] + bonus) * g
            el_ref[h, pl.ds(pi * 2, 1), :] = jnp.exp(lwc[l - 1:l, hs[h]])
            el_ref[h, pl.ds(pi * 2 + 1, 1), :] = jnp.exp(lwc[l2 - 1:l2, hs[h]])
        return carry

    lax.fori_loop(0, tb // l2, pair, 0)

    for ci in range(tb // l):
        rows = slice(ci * l, (ci + 1) * l)
        s = [s_ref[h] for h in hh]
        y = [_dot_nt(rr_ref[h, rows, :], s[h]) for h in hh]
        sm = [_dot(s[h], mx_ref[h, rows, :]) for h in hh]
        for h in hh:
            s_ref[h] = s[h] * el_ref[h, ci:ci + 1, :] + sm[h] + n0_ref[h, rows, :]
        y = [y[h] + yy_ref[h, rows, :] for h in hh]
        mean = [jnp.mean(y[h], axis=-1, keepdims=True) for h in hh]
        yc = [y[h] - mean[h] for h in hh]
        var = [jnp.mean(yc[h] * yc[h], axis=-1, keepdims=True) for h in hh]
        for h in hh:
            o_ref[rows, hs[h]] = yc[h] * lax.rsqrt(var[h] + RW_GN_EPS) * gw_ref[h, rows, :] + gb_ref[h, rows, :]

    @pl.when(t == pl.num_programs(1) - 1)
    def _():
        s_out_ref[0] = s_ref[...]


def _rwkv_prompt(r, k, v, lw, kk, a, g, s0, hp, *, batch, seq, heads, hd, tb):
    d = heads * hd
    tb = min(tb, seq)
    assert seq % tb == 0 and tb % CHUNK == 0
    nt = seq // tb
    row = pl.BlockSpec((tb, d), lambda b, t: (b * nt + t, 0))
    st = pl.BlockSpec((1, heads, hd, hd), lambda b, t: (b, 0, 0, 0))
    return pl.pallas_call(
        functools.partial(_rwkv_prompt_kernel, tb=tb, heads=heads, hd=hd),
        grid=(batch, nt),
        in_specs=[row] * 7 + [st, pl.BlockSpec((8, d), lambda b, t: (0, 0))],
        out_specs=[row, st],
        out_shape=[jax.ShapeDtypeStruct((batch * seq, d), F32),
                   jax.ShapeDtypeStruct((batch, heads, hd, hd), F32)],
        scratch_shapes=[pltpu.VMEM((heads, hd, hd), F32),
                        pltpu.VMEM((heads, tb, hd), BF16), pltpu.VMEM((heads, tb, hd), F32),
                        pltpu.VMEM((heads, tb, hd), BF16), pltpu.VMEM((heads, tb, hd), F32),
                        pltpu.VMEM((heads, tb, hd), F32), pltpu.VMEM((heads, tb, hd), F32),
                        pltpu.VMEM((heads, max(8, tb // CHUNK), hd), F32)],
        compiler_params=_cparams(("parallel", "arbitrary")),
        name="rwkv_prompt",
    )(r, k, v, lw, kk, a, g, s0, hp)


def _pad_rows(a, n):
    return jnp.pad(a, ((0, n - a.shape[0]), (0, 0)))


def _rwkv_prep(mu, w_rkv, w_o, w0, w1, w2, a0, a1, a2, g1, g2, k_k, k_a, r_k, lnx_w, lnx_b):
    d = w0.shape[0]
    lw = -(-w1.shape[1] // 128) * 128
    la = -(-a1.shape[1] // 128) * 128
    lg = -(-g1.shape[1] // 128) * 128
    return dict(mu=_pad_rows(mu, 8), w_rkv=w_rkv.astype(BF16), w_o=w_o.astype(BF16),
                w1=_pad_cols(w1, lw).astype(BF16), w2=_pad_rows(w2, lw).astype(BF16),
                a1=_pad_cols(a1, la).astype(BF16), a2=_pad_rows(a2, la).astype(BF16),
                g1=_pad_cols(g1, lg).astype(BF16), g2=_pad_rows(g2, lg).astype(BF16),
                vec=_pad_rows(jnp.stack([w0, a0, k_k, k_a]), 8),
                hp=_pad_rows(jnp.stack([r_k.reshape(d), lnx_w, lnx_b]), 8))


def _rwkv_prompt_layer(x, g_norm, p, shift0, s0, *, batch, seq, heads, hd):
    (r, k, v, lw, kk, a, g), shift = _rwkv_proj(x, g_norm, shift0, p, batch=batch, seq=seq, tm=TM_RWKV_PROJ)
    o, s = _rwkv_prompt(r, k, v, lw, kk, a, g, s0, p["hp"], batch=batch, seq=seq, heads=heads, hd=hd, tb=TB_MIX)
    return o, shift, s


def _gdn_sample_pre_kernel(pm_ref, pba_ref, conv_ref, cw_ref, gp_ref, qkv_ref, conv_out_ref, sc_ref,
                           *, heads, dk, dv):
    key = heads * dk
    ch = 2 * key + heads * dv
    u = pm_ref[:, 0:ch]
    y = u * cw_ref[CONV_W - 1:CONV_W, :]
    for j in range(CONV_W - 1):
        y = y + conv_ref[j] * cw_ref[j:j + 1, :]
        conv_out_ref[j] = conv_ref[j + 1] if j + 1 < CONV_W - 1 else u
    y = _silu(y)
    for c in range(ch // 128):
        cs = slice(c * 128, (c + 1) * 128)
        yc = y[:, cs]
        if c * 128 < key:
            yc = yc * lax.rsqrt(jnp.sum(yc * yc, axis=-1, keepdims=True) + 1e-6) * (dk ** -0.5)
        elif c * 128 < 2 * key:
            yc = yc * lax.rsqrt(jnp.sum(yc * yc, axis=-1, keepdims=True) + 1e-6)
        qkv_ref[:, cs] = yc
    ba = pba_ref[...]
    lane = lax.broadcasted_iota(jnp.int32, ba.shape, 1)
    g = -jnp.exp(gp_ref[0:1, :]) * _softplus(ba + gp_ref[1:2, :])
    sc_ref[...] = jnp.where(lane < heads, _sigmoid(ba), jnp.exp(g))


def _gdn_sample_pre(pm, pba, conv_t, cw_t, gp, *, heads, dk, dv):
    n = pm.shape[0]
    ch = 2 * heads * dk + heads * dv
    return pl.pallas_call(
        functools.partial(_gdn_sample_pre_kernel, heads=heads, dk=dk, dv=dv),
        out_shape=[jax.ShapeDtypeStruct((n, ch), F32),
                   jax.ShapeDtypeStruct((CONV_W - 1, n, ch), F32),
                   jax.ShapeDtypeStruct((n, 128), F32)],
        compiler_params=pltpu.CompilerParams(vmem_limit_bytes=V7X_VMEM_LIMIT),
        name="gdn_sample_pre",
    )(pm, pba, conv_t, cw_t, gp)


SEQ_PER_STEP = 4


def _seq_block(a, nb):
    nd = a.ndim
    return pl.BlockSpec((nb,) + a.shape[1:], lambda b: (b,) + (0,) * (nd - 1))


def _const_block(a):
    nd = a.ndim
    return pl.BlockSpec(a.shape, lambda b: (0,) * nd)


def _gdn_sample_step_kernel(s0_ref, cols_ref, v_ref, z_ref, sc_ref, nw_ref, *refs, heads, nb, n_prev):
    prev_refs = refs[:n_prev]
    s_out_all, o_ref = refs[n_prev:]
    for p, prev_ref in enumerate(prev_refs):
        s_out_all[p] = prev_ref[...]
    s_out_ref = s_out_all.at[n_prev] if n_prev else s_out_all
    hh = range(heads)
    for i in range(nb):
        kc = [cols_ref[i, :, h:h + 1] for h in hh]
        qc = [cols_ref[i, :, heads + h:heads + h + 1] for h in hh]
        s0 = [s0_ref[i, h] for h in hh]
        eg = [sc_ref[i, h:h + 1, 1:2] for h in hh]
        ks = [jnp.sum(kc[h] * s0[h], axis=0, keepdims=True) for h in hh]
        s1 = [eg[h] * s0[h] + kc[h] * (sc_ref[i, h:h + 1, 0:1] * (v_ref[i, h:h + 1, :] - eg[h] * ks[h]))
              for h in hh]
        o = [jnp.sum(qc[h] * s1[h], axis=0, keepdims=True) for h in hh]
        ms = [jnp.mean(o[h] * o[h], axis=-1, keepdims=True) for h in hh]
        for h in hh:
            s_out_ref[i, h] = s1[h]
            o_ref[i, h:h + 1, :] = o[h] * lax.rsqrt(ms[h] + RMS_EPS) * nw_ref[...] * _silu(z_ref[i, h:h + 1, :])


def _gdn_sample_step(s_all, layer, prev_new, cols, v, z, sc, nw):
    n_layers, n, heads, dk, dv = s_all.shape
    nb = SEQ_PER_STEP
    assert n % nb == 0
    last = layer == n_layers - 1
    n_prev = len(prev_new) if last else 0
    one = pl.BlockSpec((nb, heads, dk, dv), lambda b: (b, 0, 0, 0))
    if n_prev:
        out_state = pl.BlockSpec((n_layers, nb, heads, dk, dv), lambda b: (0, b, 0, 0, 0))
        out_shape = jax.ShapeDtypeStruct(s_all.shape, F32)
    else:
        out_state, out_shape = one, jax.ShapeDtypeStruct(s_all.shape[1:], F32)
    return pl.pallas_call(
        functools.partial(_gdn_sample_step_kernel, heads=heads, nb=nb, n_prev=n_prev),
        grid=(n // nb,),
        in_specs=[pl.BlockSpec((None, nb, heads, dk, dv), lambda b: (layer, b, 0, 0, 0)),
                  _seq_block(cols, nb), _seq_block(v, nb), _seq_block(z, nb), _seq_block(sc, nb),
                  pl.BlockSpec((1, dv), lambda b: (0, 0))] + [one] * n_prev,
        out_specs=[out_state, _seq_block(v, nb)],
        out_shape=[out_shape, jax.ShapeDtypeStruct(v.shape, F32)],
        compiler_params=_cparams(("parallel",)),
        name="gdn_sample_step",
    )(s_all, cols, v, z, sc, nw, *(prev_new if n_prev else []))


def _gdn_sample_layer(x, g, p, conv0, s_all, layer, prev_new, *, heads, dk, dv):
    n = x.shape[0]
    key, val = heads * dk, heads * dv
    ch = 2 * key + val
    pm, pba = _in_proj(x, g, p["w_main"], p["w_ba"])
    qkv, conv_t, sc = _gdn_sample_pre(pm, pba, jnp.transpose(conv0, (1, 0, 2)), p["cw_t"], p["gp"],
                                      heads=heads, dk=dk, dv=dv)
    q_c = jnp.transpose(qkv[:, :key].reshape(n, heads, dk), (0, 2, 1))
    k_c = jnp.transpose(qkv[:, key:2 * key].reshape(n, heads, dk), (0, 2, 1))
    cols = jnp.concatenate([k_c, q_c], axis=-1)
    sc3 = jnp.stack([sc[:, :heads], sc[:, heads:2 * heads]], axis=-1)
    s_new, o = _gdn_sample_step(s_all, layer, prev_new, cols, qkv[:, 2 * key:].reshape(n, heads, dv),
                                pm[:, ch:].reshape(n, heads, dv), sc3, p["nw"])
    return o.reshape(n, val), jnp.transpose(conv_t, (1, 0, 2)), s_new


def _mlstm_sample_step_kernel(c0_ref, n0_ref, cols_ref, q_ref, k_ref, v_ref, op_ref, sc_ref, bif_ref, nw_ref,
                              c_out_ref, n_out_ref, m_out_ref, o_ref, *, heads, dk, nb):
    scale = dk ** -0.5
    for i in range(nb):
        gi = sc_ref[i, :, 0:1] + bif_ref[:, 0:1]
        gf = sc_ref[i, :, 1:2] + bif_ref[:, 1:2]
        m0 = sc_ref[i, :, 2:3]
        gi = GATE_CAP * jnp.tanh(gi / GATE_CAP)
        logf = _log_sigmoid(GATE_CAP * jnp.tanh(gf / GATE_CAP))
        m_new = jnp.maximum(logf + m0, gi)
        f_s = jnp.exp(logf + m0 - m_new)
        i_s = jnp.exp(gi - m_new)
        m_out_ref[i] = m_new
        n1 = f_s * n0_ref[i] + i_s * k_ref[i]
        n_out_ref[i] = n1
        den = jnp.sum(q_ref[i] * scale * n1, axis=-1, keepdims=True)
        floor = jnp.exp(-m_new)
        hh = range(heads)
        kc = [cols_ref[i, :, h:h + 1] for h in hh]
        qc = [cols_ref[i, :, heads + h:heads + h + 1] * scale for h in hh]
        c1 = [f_s[h:h + 1, :] * c0_ref[i, h] + i_s[h:h + 1, :] * (kc[h] * v_ref[i, h:h + 1, :]) for h in hh]
        num = [jnp.sum(qc[h] * c1[h], axis=0, keepdims=True) for h in hh]
        h_t = [num[h] / jnp.maximum(jnp.abs(den[h:h + 1, :]), floor[h:h + 1, :]) for h in hh]
        ms = [jnp.mean(h_t[h] * h_t[h], axis=-1, keepdims=True) for h in hh]
        for h in hh:
            c_out_ref[i, h] = c1[h]
            h_n = h_t[h] * lax.rsqrt(ms[h] + RMS_EPS) * nw_ref[h:h + 1, :]
            o_ref[i, h:h + 1, :] = _sigmoid(op_ref[i, h:h + 1, :]) * h_n


def _mlstm_sample_step(c0, n0, cols, q, k, v, o_pre, sc, bif2, nw2):
    n, heads, dk, dv = c0.shape
    nb = SEQ_PER_STEP
    assert n % nb == 0
    full = _const_block

    def blk(a):
        return _seq_block(a, nb)

    m_shape = (n, heads, 1)
    return pl.pallas_call(
        functools.partial(_mlstm_sample_step_kernel, heads=heads, dk=dk, nb=nb),
        grid=(n // nb,),
        in_specs=[blk(c0), blk(n0), blk(cols), blk(q), blk(k), blk(v), blk(o_pre), blk(sc), full(bif2), full(nw2)],
        out_specs=[blk(c0), blk(n0), pl.BlockSpec((nb, heads, 1), lambda b: (b, 0, 0)), blk(v)],
        out_shape=[jax.ShapeDtypeStruct(c0.shape, F32), jax.ShapeDtypeStruct(n0.shape, F32),
                   jax.ShapeDtypeStruct(m_shape, F32), jax.ShapeDtypeStruct(v.shape, F32)],
        compiler_params=_cparams(("parallel",)),
        name="mlstm_sample_step",
    )(c0, n0, cols, q, k, v, o_pre, sc, bif2, nw2)


def _mlstm_sample_layer(x, g, p, c0, n0, m0, *, heads, dk, dv):
    n = x.shape[0]
    qk_w, val = heads * dk, heads * dv
    pm, pif = _in_proj(x, g, p["w_main"], p["w_if"])
    q = pm[:, :qk_w].reshape(n, heads, dk)
    k = pm[:, qk_w:2 * qk_w].reshape(n, heads, dk)
    v = pm[:, 2 * qk_w:2 * qk_w + val].reshape(n, heads, dv)
    o_pre = pm[:, 2 * qk_w + val:].reshape(n, heads, dv)
    cols = jnp.concatenate([jnp.transpose(k, (0, 2, 1)), jnp.transpose(q, (0, 2, 1))], axis=-1)
    sc = jnp.stack([pif[:, :heads], pif[:, heads:2 * heads], m0], axis=-1)
    bif2 = jnp.stack([p["bif"][0, :heads], p["bif"][0, heads:2 * heads]], axis=-1)
    c1, n1, m1, o = _mlstm_sample_step(c0, n0, cols, q, k, v, o_pre, sc, bif2, p["nw"].reshape(heads, dv))
    return o.reshape(n, val), c1, n1, m1[:, :, 0]


RW_ROW_GROUP = 8


def _rwkv_sample_step_kernel(s_ref, r_ref, k_ref, lw_ref, kk_ref, a_ref, v_ref, g_ref, hp_ref,
                             s_out_ref, o_ref, y_ref, *, hd):
    kk = kk_ref[0]
    kk = kk * lax.rsqrt(jnp.maximum(jnp.sum(kk * kk, axis=0, keepdims=True), 1e-24))
    av = -kk
    bv = kk * a_ref[0]
    w = jnp.exp(lw_ref[0])
    r = r_ref[0]
    k = k_ref[0]
    v = v_ref[0]
    for v0 in range(0, hd, RW_ROW_GROUP):
        vv = range(v0, v0 + RW_ROW_GROUP)
        s0 = [s_ref[0, i] for i in vv]
        sa = [jnp.sum(s * av, axis=0, keepdims=True) for s in s0]
        s1 = [s * w + sa_i * bv + v[i:i + 1, :] * k for s, sa_i, i in zip(s0, sa, vv)]
        y = [jnp.sum(s * r, axis=0, keepdims=True) for s in s1]
        for i, s, y_i in zip(vv, s1, y):
            s_out_ref[0, i] = s
            y_ref[i:i + 1, :] = y_i
    y = y_ref[...]
    yc = y - jnp.mean(y, axis=0, keepdims=True)
    var = jnp.mean(yc * yc, axis=0, keepdims=True)
    hp = hp_ref[0]
    yn = yc * lax.rsqrt(var + RW_GN_EPS) * hp[:, 1:2] + hp[:, 2:3]
    bonus = jnp.sum(r * k * hp[:, 0:1], axis=0, keepdims=True)
    o_ref[0] = (yn + bonus * v) * g_ref[0]


def _rwkv_sample_step(s_t, r, k, lw, kk, a, v, g, hp3):
    heads, hd, _, n = s_t.shape

    def blk(z):
        nd = z.ndim
        return pl.BlockSpec((1,) + z.shape[1:], lambda h: (h,) + (0,) * (nd - 1))

    return pl.pallas_call(
        functools.partial(_rwkv_sample_step_kernel, hd=hd),
        grid=(heads,),
        in_specs=[blk(s_t)] + [blk(z) for z in (r, k, lw, kk, a, v, g, hp3)],
        out_specs=[blk(s_t), blk(v)],
        out_shape=[jax.ShapeDtypeStruct(s_t.shape, F32), jax.ShapeDtypeStruct(v.shape, F32)],
        scratch_shapes=[pltpu.VMEM((hd, n), F32)],
        compiler_params=_cparams(("parallel",)),
        name="rwkv_sample_step",
    )(s_t, r, k, lw, kk, a, v, g, hp3)


def _rwkv_sample_layer(x, g_norm, p, shift0, s0, *, heads, hd):
    n, d = x.shape
    (r, k, v, lw, kk, a, g), hn = _rwkv_proj(x, g_norm, shift0, p, batch=n, seq=1, tm=n)

    def lanes(z):
        return z.T.reshape(heads, hd, n)

    hp = p["hp"]
    hp3 = jnp.stack([hp[j].reshape(heads, hd) for j in range(3)], axis=-1)
    s1_t, o_t = _rwkv_sample_step(jnp.transpose(s0, (1, 2, 3, 0)), lanes(r), lanes(k), lanes(lw), lanes(kk),
                                  lanes(a), lanes(v), lanes(g), hp3)
    return o_t.reshape(d, n).T, hn, jnp.transpose(s1_t, (3, 0, 1, 2))


def _pad_cols(a, n):
    return jnp.pad(a, ((0, 0), (0, n - a.shape[1])))


def _gdn_prep(w_in, conv_w, a_log, dt_bias, norm_w, w_out, *, heads, dk, dv):
    key, val = heads * dk, heads * dv
    ch = 2 * key + val
    main = ch + val
    gp = jnp.zeros((8, 128), F32)
    gp = gp.at[0, heads:2 * heads].set(a_log).at[1, heads:2 * heads].set(dt_bias)
    return dict(w_main=w_in[:, :main].astype(BF16),
                w_ba=_pad_cols(w_in[:, main:], 128).astype(BF16),
                cw_t=jnp.pad(conv_w.T, ((0, 8 - CONV_W), (0, 0))),
                gp=gp, nw=norm_w[None, :], w_out=w_out.astype(BF16))


def _gdn_prompt_layer(x, g, p, conv0, s0, *, batch, seq, heads, dk, dv):
    conv0 = jnp.pad(conv0, ((0, 0), (8 - (CONV_W - 1), 0), (0, 0)))
    qkvz, pba, conv = _gdn_in_proj(x, g, p["w_main"], p["w_ba"], conv0, p["cw_t"], batch=batch, seq=seq,
                                   heads=heads, dk=dk, dv=dv, tm=TM_GDN_PROJ)
    o, s = _gdn_prompt(qkvz, pba, s0, p["gp"], p["nw"], batch=batch, seq=seq, heads=heads, dk=dk, dv=dv, tb=TB_MIX)
    return o, conv[:, 8 - (CONV_W - 1):, :], s


def _trunk(x, states, w, *, batch, seq):
    conv_in, gs_in, c_in, n_in, m_in, shift_in, rs_in = states
    depth = w["norm_mix"].shape[0]
    gh, gdk, gdv = gs_in.shape[2:]
    mh, mdk, mdv = c_in.shape[2:]
    rh, rhd = rs_in.shape[2:4]
    prompt = seq > 1
    outs = [[] for _ in range(7)]
    gs_new = []
    for i in range(depth):
        j = i // 3
        g = w["norm_mix"][i][None, :]
        if i % 3 == 0:
            p = w["gdn"][j]
            if prompt:
                mix, cb, s = _gdn_prompt_layer(x, g, p, conv_in[j], gs_in[j], batch=batch, seq=seq,
                                               heads=gh, dk=gdk, dv=gdv)
                outs[1].append(s)
            else:
                mix, cb, s = _gdn_sample_layer(x, g, p, conv_in[j], gs_in, j, gs_new, heads=gh, dk=gdk, dv=gdv)
                gs_new.append(s)
            outs[0].append(cb)
            w_out = p["w_out"]
        elif i % 3 == 1:
            p = w["ml"][j]
            if prompt:
                mix, c, n, m = _mlstm_prompt_layer(x, g, p, c_in[j], n_in[j], m_in[j], batch=batch, seq=seq,
                                                   heads=mh, dk=mdk, dv=mdv)
            else:
                mix, c, n, m = _mlstm_sample_layer(x, g, p, c_in[j], n_in[j], m_in[j], heads=mh, dk=mdk, dv=mdv)
            w_out = p["w_out"]
            outs[2].append(c)
            outs[3].append(n)
            outs[4].append(m)
        else:
            p = w["rw"][j]
            if prompt:
                mix, sh, s = _rwkv_prompt_layer(x, g, p, shift_in[j], rs_in[j], batch=batch, seq=seq,
                                                heads=rh, hd=rhd)
            else:
                mix, sh, s = _rwkv_sample_layer(x, g, p, shift_in[j], rs_in[j], heads=rh, hd=rhd)
            w_out = p["w_o"]
            outs[5].append(sh)
            outs[6].append(s)
        g_out = w["norm_final"][None, :] if i == depth - 1 else None
        x = _ffn(x, mix, w_out, w["norm_ffn"][i][None, :], w["ffn_w1"], w["ffn_w2"], i, g_out,
                 tm=TM_FFN, tf=TF_FFN)
    y = x
    new = [jnp.stack(z, axis=0) if z else None for z in outs]
    if not prompt:
        new[1] = gs_new[-1] if len(gs_new) > 1 else gs_new[0][None]
    return y, tuple(new)


def kernel(x_prompt, x_sample, state_gdn_conv, state_gdn_S, state_mlstm_C, state_mlstm_n, state_mlstm_m, state_rwkv_shift, state_rwkv_S, norm_mix, norm_ffn, norm_final, gdn_w_in, gdn_conv_w, gdn_a_log, gdn_dt_bias, gdn_norm_w, gdn_w_out, ml_w_in, ml_b_if, ml_norm_w, ml_w_out, rw_mu, rw_w_rkv, rw_w_o, rw_w0, rw_w1, rw_w2, rw_a0, rw_a1, rw_a2, rw_g1, rw_g2, rw_k_k, rw_k_a, rw_r_k, rw_lnx_w, rw_lnx_b, ffn_w1, ffn_w2):
    gh, gdk, gdv = state_gdn_S.shape[2:]
    mh, mdk, mdv = state_mlstm_C.shape[2:]
    w = dict(
        norm_mix=norm_mix, norm_ffn=norm_ffn, norm_final=norm_final,
        ffn_w1=ffn_w1.astype(BF16), ffn_w2=ffn_w2.astype(BF16),
        gdn=[_gdn_prep(gdn_w_in[j], gdn_conv_w[j], gdn_a_log[j], gdn_dt_bias[j], gdn_norm_w[j], gdn_w_out[j],
                       heads=gh, dk=gdk, dv=gdv) for j in range(gdn_w_in.shape[0])],
        ml=[_mlstm_prep(ml_w_in[j], ml_b_if[j], ml_norm_w[j], ml_w_out[j], heads=mh, dk=mdk, dv=mdv)
            for j in range(ml_w_in.shape[0])],
        rw=[_rwkv_prep(rw_mu[j], rw_w_rkv[j], rw_w_o[j], rw_w0[j], rw_w1[j], rw_w2[j], rw_a0[j], rw_a1[j],
                       rw_a2[j], rw_g1[j], rw_g2[j], rw_k_k[j], rw_k_a[j], rw_r_k[j], rw_lnx_w[j], rw_lnx_b[j])
            for j in range(rw_mu.shape[0])])
    sample_states = (state_gdn_conv, state_gdn_S, state_mlstm_C, state_mlstm_n, state_mlstm_m,
                     state_rwkv_shift, state_rwkv_S)
    bp, tp, d = x_prompt.shape
    bs, ts, _ = x_sample.shape
    assert ts == 1
    prompt_states = tuple(jnp.zeros((s.shape[0], bp) + s.shape[2:], s.dtype) for s in sample_states)
    y_p, new_p = _trunk(x_prompt.reshape(bp * tp, d), prompt_states, w, batch=bp, seq=tp)
    y_s, new_s = _trunk(x_sample.reshape(bs * ts, d), sample_states, w, batch=bs, seq=ts)
    out = [y_p.reshape(bp, tp, d), y_s.reshape(bs, ts, d)]
    for a, b in zip(new_p, new_s):
        out += [a, b]
    return tuple(out)
```

```python
import functools

import jax
import jax.numpy as jnp
from jax import lax
from jax.experimental import pallas as pl
from jax.experimental.pallas import tpu as pltpu

F32 = jnp.float32
BF16 = jnp.bfloat16

RMS_EPS = 1e-6
NEG_BIG = -1e30
GATE_CAP = 15.0
RW_GN_EPS = 64e-5
CONV_W = 4
CHUNK = 64
V7X_VMEM_LIMIT = 56 * 1024 * 1024
HI = lax.Precision.HIGHEST


def _cparams(sem):
    return pltpu.CompilerParams(dimension_semantics=sem, vmem_limit_bytes=V7X_VMEM_LIMIT)


def _dot(a, b):
    return jnp.dot(a.astype(BF16), b.astype(BF16), preferred_element_type=F32)


def _dot_nt(a, b):
    return lax.dot_general(a.astype(BF16), b.astype(BF16), (((1,), (1,)), ((), ())),
                           preferred_element_type=F32)


def _dot_tn(a, b):
    return lax.dot_general(a.astype(BF16), b.astype(BF16), (((0,), (0,)), ((), ())),
                           preferred_element_type=F32)


def _dot_hi(a, b):
    return jnp.dot(a, b, preferred_element_type=F32, precision=HI)


def _sigmoid(x):
    return 1.0 / (1.0 + jnp.exp(-x))


def _silu(x):
    return x * _sigmoid(x)


def _softplus(x):
    return jnp.maximum(x, 0.0) + jnp.log(1.0 + jnp.exp(-jnp.abs(x)))


def _log_sigmoid(x):
    return -_softplus(-x)


def _tri_masks(l):
    r = lax.broadcasted_iota(jnp.int32, (l, l), 0)
    c = lax.broadcasted_iota(jnp.int32, (l, l), 1)
    return r >= c, r > c


INV_BASE = 16


def _unit_lower_inverse(mats, l):
    n = mats[0].shape[0]
    r = lax.broadcasted_iota(jnp.int32, (n, n), 0)
    c = lax.broadcasted_iota(jnp.int32, (n, n), 1)
    eye = (r == c).astype(F32)
    size = min(INV_BASE, l)
    shift = size.bit_length() - 1
    diag = (r >> shift) == (c >> shift)
    merges = []
    s = size
    while s < l:
        sh = s.bit_length() - 1
        off = ((r >> (sh + 1)) == (c >> (sh + 1))) & ((r >> sh) > (c >> sh))
        merges.append([jnp.where(off, a, 0.0).astype(BF16) for a in mats])
        s *= 2
    t = [eye - jnp.where(diag, a, 0.0) for a in mats]
    tb = [ti.astype(BF16) for ti in t]
    ab = [jnp.where(diag, a, 0.0).astype(BF16) for a in mats]
    p = [_dot(a, a).astype(BF16) for a in ab]
    k = 2
    while k < size:
        if 2 * k < size:
            both = [_dot(pi, jnp.concatenate([ti, pi], axis=1)) for ti, pi in zip(tb, p)]
            t = [ti + bi[:, :n] for ti, bi in zip(t, both)]
            p = [bi[:, n:].astype(BF16) for bi in both]
        else:
            t = [ti + _dot(pi, ti_b) for ti, ti_b, pi in zip(t, tb, p)]
        tb = [ti.astype(BF16) for ti in t]
        k *= 2
    for a_off in merges:
        x = [_dot(a, ti) for a, ti in zip(a_off, tb)]
        y = [_dot(ti, xi) for ti, xi in zip(tb, x)]
        t = [ti - yi for ti, yi in zip(t, y)]
        tb = [ti.astype(BF16) for ti in t]
    return tb


TM_IN_PROJ = 512
TM_GDN_PROJ = 512
TM_FFN, TF_FFN = 1024, 1024
TB_MIX = 256
TM_RWKV_PROJ = 256


def _norm_rows(x, g):
    return x * lax.rsqrt(jnp.mean(x * x, axis=-1, keepdims=True) + RMS_EPS) * g


def _in_proj_kernel(x_ref, g_ref, w_ref, wa_ref, o_ref, oa_ref):
    xn = _norm_rows(x_ref[...], g_ref[...]).astype(BF16)
    oa_ref[...] = jnp.dot(xn, wa_ref[...], preferred_element_type=F32)
    o_ref[...] = jnp.dot(xn, w_ref[...], preferred_element_type=F32)


def _in_proj(x, g, w_main, w_aux):
    m, k = x.shape
    n, na = w_main.shape[1], w_aux.shape[1]
    tm = min(TM_IN_PROJ, m)
    assert m % tm == 0
    return pl.pallas_call(
        _in_proj_kernel,
        grid=(m // tm,),
        in_specs=[pl.BlockSpec((tm, k), lambda i: (i, 0)), pl.BlockSpec((1, k), lambda i: (0, 0)),
                  pl.BlockSpec((k, n), lambda i: (0, 0)), pl.BlockSpec((k, na), lambda i: (0, 0))],
        out_specs=[pl.BlockSpec((tm, n), lambda i: (i, 0)), pl.BlockSpec((tm, na), lambda i: (i, 0))],
        out_shape=[jax.ShapeDtypeStruct((m, n), F32), jax.ShapeDtypeStruct((m, na), F32)],
        compiler_params=_cparams(("parallel",)),
        name="in_proj",
    )(x, g, w_main, w_aux)


def _ffn_kernel(res_ref, mix_ref, wo_ref, g_ref, w1_ref, w2_ref, go_ref, o_ref, xn_ref, acc_ref, *, out_norm):
    f = pl.program_id(1)

    @pl.when(f == 0)
    def _():
        x = res_ref[...] + jnp.dot(mix_ref[...].astype(BF16), wo_ref[...], preferred_element_type=F32)
        o_ref[...] = x
        xn_ref[...] = _norm_rows(x, g_ref[...]).astype(BF16)
        acc_ref[...] = jnp.zeros_like(acc_ref)

    h = jnp.dot(xn_ref[...], w1_ref[...], preferred_element_type=F32)
    a = jnp.square(jnp.maximum(h, 0.0)).astype(BF16)
    acc_ref[...] += jnp.dot(a, w2_ref[...], preferred_element_type=F32)

    @pl.when(f == pl.num_programs(1) - 1)
    def _():
        y = o_ref[...] + acc_ref[...]
        o_ref[...] = _norm_rows(y, go_ref[...]) if out_norm else y


def _ffn(res, mix, w_out, g, w1, w2, layer, g_out=None, *, tm, tf):
    m, d = res.shape
    dff = w1.shape[2]
    tm, tf = min(tm, m), min(tf, dff)
    assert m % tm == 0 and dff % tf == 0
    vec = pl.BlockSpec((1, d), lambda i, j: (0, 0))
    return pl.pallas_call(
        functools.partial(_ffn_kernel, out_norm=g_out is not None),
        grid=(m // tm, dff // tf),
        in_specs=[pl.BlockSpec((tm, d), lambda i, j: (i, 0)),
                  pl.BlockSpec((tm, mix.shape[1]), lambda i, j: (i, 0)),
                  pl.BlockSpec(w_out.shape, lambda i, j: (0, 0)), vec,
                  pl.BlockSpec((None, d, tf), lambda i, j: (layer, 0, j)),
                  pl.BlockSpec((None, tf, d), lambda i, j: (layer, j, 0)), vec],
        out_specs=pl.BlockSpec((tm, d), lambda i, j: (i, 0)),
        out_shape=jax.ShapeDtypeStruct((m, d), F32),
        scratch_shapes=[pltpu.VMEM((tm, d), BF16), pltpu.VMEM((tm, d), F32)],
        compiler_params=_cparams(("parallel", "arbitrary")),
        name="ffn",
    )(res, mix, w_out, g, w1, w2, g if g_out is None else g_out)


def _gdn_prompt_kernel(qkv_ref, pba_ref, s0_ref, gp_ref, nw_ref,
                       o_ref, s_out_ref,
                       s_ref, u_ref, w_ref, qd_ref, kd_ref, qk_ref, gl_ref,
                       *, tb, heads, dk, dv):
    t = pl.program_id(1)
    key = heads * dk
    ch = 2 * key + heads * dv
    l = CHUNK

    @pl.when(t == 0)
    def _():
        s_ref[...] = s0_ref[0]

    a_log = gp_ref[0:1, :]
    dt_bias = gp_ref[1:2, :]
    hh = range(heads)
    l2 = 2 * l
    r2 = lax.broadcasted_iota(jnp.int32, (l2, l2), 0)
    c2 = lax.broadcasted_iota(jnp.int32, (l2, l2), 1)
    same = (r2 >= l) == (c2 >= l)
    incl = same & (r2 >= c2)
    strict = same & (r2 > c2)
    tril = incl.astype(F32)
    first = lax.broadcasted_iota(jnp.int32, (l2, 128), 0) < l

    pairs = range(tb // l2)
    rows = [slice(pi * l2, (pi + 1) * l2) for pi in pairs]
    beta_all, gc, gc_t, g_end = [], [], [], []
    for pi in pairs:
        ba = pba_ref[rows[pi], :]
        beta_all.append(_sigmoid(ba))
        g_all = -jnp.exp(a_log) * _softplus(ba + dt_bias)
        gc.append(_dot_hi(tril, g_all))
        gc_t.append(gc[pi].T)
        g_end.append(jnp.where(first, gc[pi][l - 1:l, :], gc[pi][l2 - 1:l2, :]))
        gl_ref[2 * pi:2 * pi + 1, :] = jnp.exp(gc[pi][l - 1:l, :])
        gl_ref[2 * pi + 1:2 * pi + 2, :] = jnp.exp(gc[pi][l2 - 1:l2, :])
    cc = [(pi, h) for pi in pairs for h in hh]
    b_col = [beta_all[pi][:, h:h + 1] for pi, h in cc]
    gi = [gc[pi][:, heads + h:heads + h + 1] for pi, h in cc]
    q = [qkv_ref[rows[pi], h * dk:(h + 1) * dk] for pi, h in cc]
    k = [qkv_ref[rows[pi], key + h * dk:key + (h + 1) * dk] for pi, h in cc]
    v = [qkv_ref[rows[pi], 2 * key + h * dv:2 * key + (h + 1) * dv] for pi, h in cc]
    nc = range(len(cc))
    dmat = [jnp.where(incl, jnp.exp(jnp.where(incl, gi[i] - gc_t[pi][heads + h:heads + h + 1, :], 0.0)), 0.0)
            for i, (pi, h) in enumerate(cc)]
    kb = [k[i] * b_col[i] for i in nc]
    kk = [_dot_nt(kb[i], k[i]) for i in nc]
    qk = [_dot_nt(q[i], k[i]) for i in nc]
    t_inv = _unit_lower_inverse([jnp.where(strict, kk[i] * dmat[i], 0.0) for i in nc], l)
    egi = [jnp.exp(gi[i]) for i in nc]
    sol = [_dot(t_inv[i], jnp.concatenate([v[i] * b_col[i], kb[i] * egi[i]], axis=-1)) for i in nc]
    for i, (pi, h) in enumerate(cc):
        hs = slice(h * dk, (h + 1) * dk)
        u_ref[rows[pi], h * dv:(h + 1) * dv] = sol[i][:, :dv]
        w_ref[rows[pi], hs] = sol[i][:, dv:].astype(BF16)
        qd_ref[rows[pi], hs] = (q[i] * egi[i]).astype(BF16)
        kd_ref[rows[pi], hs] = (k[i] * jnp.exp(g_end[pi][:, heads + h:heads + h + 1] - gi[i])).astype(BF16)
        qkm = jnp.where(incl, qk[i] * dmat[i], 0.0).astype(BF16)
        qk_ref[h, pi * l2:pi * l2 + l, :] = qkm[:l, :l]
        qk_ref[h, pi * l2 + l:(pi + 1) * l2, :] = qkm[l:, l:]

    for ci in range(tb // l):
        rows = slice(ci * l, (ci + 1) * l)
        s = [s_ref[h] for h in hh]
        wq = [_dot(jnp.concatenate([w_ref[rows, h * dk:(h + 1) * dk], qd_ref[rows, h * dk:(h + 1) * dk]], axis=0),
                   s[h]) for h in hh]
        v_new = [u_ref[rows, h * dv:(h + 1) * dv] - wq[h][:l] for h in hh]
        o2 = [_dot(qk_ref[h, rows, :], v_new[h]) for h in hh]
        ds = [_dot_tn(kd_ref[rows, h * dk:(h + 1) * dk], v_new[h]) for h in hh]
        for h in hh:
            s_ref[h] = s[h] * gl_ref[ci:ci + 1, heads + h:heads + h + 1] + ds[h]
            o = wq[h][l:] + o2[h]
            z = qkv_ref[rows, ch + h * dv:ch + (h + 1) * dv]
            o = o * lax.rsqrt(jnp.mean(o * o, axis=-1, keepdims=True) + RMS_EPS) * nw_ref[...]
            o_ref[rows, h * dv:(h + 1) * dv] = o * _silu(z)

    @pl.when(t == pl.num_programs(1) - 1)
    def _():
        s_out_ref[0] = s_ref[...]


def _gdn_prompt(qkvz, pba, s0, gp, nw, *, batch, seq, heads, dk, dv, tb):
    key, val = heads * dk, heads * dv
    ch = 2 * key + val
    tb = min(tb, seq)
    assert seq % tb == 0 and tb % (2 * CHUNK) == 0
    nt = seq // tb
    return pl.pallas_call(
        functools.partial(_gdn_prompt_kernel, tb=tb, heads=heads, dk=dk, dv=dv),
        grid=(batch, nt),
        in_specs=[pl.BlockSpec((tb, ch + val), lambda b, t: (b * nt + t, 0)),
                  pl.BlockSpec((tb, 128), lambda b, t: (b * nt + t, 0)),
                  pl.BlockSpec((1, heads, dk, dv), lambda b, t: (b, 0, 0, 0)),
                  pl.BlockSpec((8, 128), lambda b, t: (0, 0)),
                  pl.BlockSpec((1, dv), lambda b, t: (0, 0))],
        out_specs=[pl.BlockSpec((tb, val), lambda b, t: (b * nt + t, 0)),
                   pl.BlockSpec((1, heads, dk, dv), lambda b, t: (b, 0, 0, 0))],
        out_shape=[jax.ShapeDtypeStruct((batch * seq, val), F32),
                   jax.ShapeDtypeStruct((batch, heads, dk, dv), F32)],
        scratch_shapes=[pltpu.VMEM((heads, dk, dv), F32),
                        pltpu.VMEM((tb, val), F32), pltpu.VMEM((tb, key), BF16),
                        pltpu.VMEM((tb, key), BF16), pltpu.VMEM((tb, key), BF16),
                        pltpu.VMEM((heads, tb, CHUNK), BF16), pltpu.VMEM((max(8, tb // CHUNK), 128), F32)],
        compiler_params=_cparams(("parallel", "arbitrary")),
        name="gdn_prompt",
    )(qkvz, pba, s0, gp, nw)


GDN_PROJ_COLS = 256
GDN_CONV_ROWS = 64


def _gdn_in_proj_kernel(x_ref, g_ref, w_ref, wa_ref, conv0_ref, cw_ref, o_ref, aux_ref, conv_out_ref,
                        xn_ref, carry_ref, *, heads, dk, dv):
    t = pl.program_id(1)
    key = heads * dk
    ch = 2 * key + heads * dv
    tm = x_ref.shape[0]
    n = w_ref.shape[1]

    @pl.when(t == 0)
    def _():
        carry_ref[...] = conv0_ref[0]

    xn_ref[...] = _norm_rows(x_ref[...], g_ref[...]).astype(BF16)
    aux_ref[...] = jnp.dot(xn_ref[...], wa_ref[...], preferred_element_type=F32)
    for c0 in range(0, n, GDN_PROJ_COLS):
        cs = slice(c0, c0 + GDN_PROJ_COLS)
        y = jnp.dot(xn_ref[...], w_ref[:, cs], preferred_element_type=F32)
        if c0 >= ch:
            o_ref[:, cs] = y
            continue
        ext = jnp.concatenate([carry_ref[:, cs], y], axis=0)
        carry_ref[:, cs] = y[tm - 8:, :]
        for r0 in range(0, tm, GDN_CONV_ROWS):
            for j in range(0, GDN_PROJ_COLS, dk):
                cj = slice(c0 + j, c0 + j + dk)
                blk = ext[r0:r0 + GDN_CONV_ROWS + 8, j:j + dk]
                z = blk[8:] * cw_ref[CONV_W - 1:CONV_W, cj]
                for s in range(1, CONV_W):
                    z = z + pltpu.roll(blk, s, 0)[8:] * cw_ref[CONV_W - 1 - s:CONV_W - s, cj]
                z = _silu(z)
                if c0 + j < key:
                    z = z * lax.rsqrt(jnp.sum(z * z, axis=-1, keepdims=True) + 1e-6) * (dk ** -0.5)
                elif c0 + j < 2 * key:
                    z = z * lax.rsqrt(jnp.sum(z * z, axis=-1, keepdims=True) + 1e-6)
                o_ref[r0:r0 + GDN_CONV_ROWS, cj] = z

    @pl.when(t == pl.num_programs(1) - 1)
    def _():
        conv_out_ref[0] = carry_ref[...]


def _gdn_in_proj(x, g, w_main, w_ba, conv0, cw_t, *, batch, seq, heads, dk, dv, tm):
    m, d = x.shape
    key, val = heads * dk, heads * dv
    ch = 2 * key + val
    n = w_main.shape[1]
    assert dk == dv and n % GDN_PROJ_COLS == 0 and ch % GDN_PROJ_COLS == 0 and GDN_PROJ_COLS % dk == 0
    tm = min(tm, seq)
    assert seq % tm == 0 and tm >= 8
    nt = seq // tm
    row = lambda width: pl.BlockSpec((tm, width), lambda b, t: (b * nt + t, 0))
    full = lambda a: pl.BlockSpec(a.shape, lambda b, t: (0,) * a.ndim)
    st = pl.BlockSpec((1, 8, ch), lambda b, t: (b, 0, 0))
    return pl.pallas_call(
        functools.partial(_gdn_in_proj_kernel, heads=heads, dk=dk, dv=dv),
        grid=(batch, nt),
        in_specs=[row(d), full(g), full(w_main), full(w_ba), st, full(cw_t)],
        out_specs=[row(n), row(w_ba.shape[1]), st],
        out_shape=[jax.ShapeDtypeStruct((m, n), F32), jax.ShapeDtypeStruct((m, w_ba.shape[1]), F32),
                   jax.ShapeDtypeStruct((batch, 8, ch), F32)],
        scratch_shapes=[pltpu.VMEM((tm, d), BF16), pltpu.VMEM((8, ch), F32)],
        compiler_params=_cparams(("parallel", "arbitrary")),
        name="gdn_in_proj",
    )(x, g, w_main, w_ba, conv0, cw_t)


def _mlstm_prompt_kernel(pm_ref, pif_ref, bif_ref, c0_ref, n0_ref, m0_ref, nw_ref,
                         o_ref, c_out_ref, n_out_ref, m_out_ref,
                         c_ref, n_ref, m_ref, bc_ref, ni_ref, col_ref, kv_ref, kc_ref, sc_ref,
                         cin_ref, nin_ref, min_ref, *, tb, heads, dk, dv):
    t = pl.program_id(1)
    l = CHUNK
    qk_w = heads * dk
    v_off = 2 * qk_w
    o_off = v_off + heads * dv

    @pl.when(t == 0)
    def _():
        c_ref[...] = c0_ref[0]
        n_ref[...] = n0_ref[0]
        m_ref[...] = m0_ref[0]

    incl, _ = _tri_masks(l)
    tril = incl.astype(F32)
    hh = range(heads)
    nchunk = tb // l
    scale = dk ** -0.5

    crow = [slice(ci * l, (ci + 1) * l) for ci in range(nchunk)]
    gates, bcum, bcum_t, gates_t = [], [], [], []
    for ci in range(nchunk):
        g = pif_ref[crow[ci], :] + bif_ref[...]
        g = GATE_CAP * jnp.tanh(g / GATE_CAP)
        gates.append(g)
        bcum.append(_dot_hi(tril, _log_sigmoid(g)))
        bcum_t.append(bcum[ci].T)
        gates_t.append(g.T)
        bc_ref[crow[ci], :] = bcum[ci]
    cc = [(ci, h) for ci in range(nchunk) for h in hh]
    nc = range(len(cc))
    bi = [bcum[ci][:, heads + h:heads + h + 1] for ci, h in cc]
    b_last = [bcum[ci][l - 1:l, heads + h:heads + h + 1] for ci, h in cc]
    k = [pm_ref[crow[ci], qk_w + h * dk:qk_w + (h + 1) * dk] for ci, h in cc]
    v = [pm_ref[crow[ci], v_off + h * dv:v_off + (h + 1) * dv].astype(BF16) for ci, h in cc]
    qk = [_dot_nt(pm_ref[crow[ci], h * dk:(h + 1) * dk] * scale, k[i]) for i, (ci, h) in enumerate(cc)]
    dlog = [jnp.where(incl, bi[i] - bcum_t[ci][heads + h:heads + h + 1, :] + gates_t[ci][h:h + 1, :], NEG_BIG)
            for i, (ci, h) in enumerate(cc)]
    m_intra = [jnp.max(dlog[i], axis=-1, keepdims=True) for i in nc]
    p = [jnp.where(incl, jnp.exp(dlog[i] - m_intra[i]), 0.0) * qk[i] for i in nc]
    den_intra = [jnp.sum(p[i], axis=-1, keepdims=True) for i in nc]
    num_intra = [_dot(p[i], v[i]) for i in nc]
    a_log = [b_last[i] - bi[i] + gates[ci][:, h:h + 1] for i, (ci, h) in enumerate(cc)]
    m_chunk = [jnp.max(a_log[i], axis=0, keepdims=True) for i in nc]
    kw = [k[i] * jnp.exp(a_log[i] - m_chunk[i]) for i in nc]
    kv_chunk = [_dot_tn(kw[i], v[i]) for i in nc]
    for i, (ci, h) in enumerate(cc):
        ni_ref[crow[ci], h * dv:(h + 1) * dv] = num_intra[i]
        col_ref[crow[ci], h:h + 1] = m_intra[i]
        col_ref[crow[ci], heads + h:heads + h + 1] = den_intra[i]
        kv_ref[i] = kv_chunk[i]
        kc_ref[i:i + 1, :] = jnp.sum(kw[i], axis=0, keepdims=True)
        sc_ref[i:i + 1, 0:1] = m_chunk[i]
        sc_ref[i:i + 1, 1:2] = b_last[i]

    for ci in range(nchunk):
        for h in hh:
            i = ci * heads + h
            c_mat = c_ref[h]
            n_vec = n_ref[h:h + 1, :]
            m_prev = m_ref[h:h + 1, :]
            cin_ref[i] = c_mat
            nin_ref[i:i + 1, :] = n_vec
            min_ref[i:i + 1, :] = m_prev
            m_chunk = sc_ref[i:i + 1, 0:1]
            b_last = sc_ref[i:i + 1, 1:2]
            m_new = jnp.maximum(b_last + m_prev, m_chunk)
            f_s = jnp.exp(b_last + m_prev - m_new)
            i_s = jnp.exp(m_chunk - m_new)
            c_ref[h] = f_s[:, 0:1] * c_mat + i_s[:, 0:1] * kv_ref[i]
            n_ref[h:h + 1, :] = f_s * n_vec + i_s * kc_ref[i:i + 1, :]
            m_ref[h:h + 1, :] = m_new

    q = [pm_ref[crow[ci], h * dk:(h + 1) * dk] * scale for ci, h in cc]
    qc = [_dot(q[i], cin_ref[i]) for i in nc]
    qn = [jnp.sum(q[i] * nin_ref[i:i + 1, :], axis=-1, keepdims=True) for i in nc]
    m_prev = [min_ref[i:i + 1, 0:1] for i in nc]
    bi = [bc_ref[crow[ci], heads + h:heads + h + 1] for ci, h in cc]
    m_in = [col_ref[crow[ci], h:h + 1] for ci, h in cc]
    m_t = [jnp.maximum(bi[i] + m_prev[i], m_in[i]) for i in nc]
    s_inter = [jnp.exp(bi[i] + m_prev[i] - m_t[i]) for i in nc]
    s_intra = [jnp.exp(m_in[i] - m_t[i]) for i in nc]
    den = [s_inter[i] * qn[i] + s_intra[i] * col_ref[crow[ci], heads + h:heads + h + 1]
           for i, (ci, h) in enumerate(cc)]
    h_t = [(s_inter[i] * qc[i] + s_intra[i] * ni_ref[crow[ci], h * dv:(h + 1) * dv])
           / jnp.maximum(jnp.abs(den[i]), jnp.exp(-m_t[i])) for i, (ci, h) in enumerate(cc)]
    ms = [jnp.mean(h_t[i] * h_t[i], axis=-1, keepdims=True) for i in nc]
    for i, (ci, h) in enumerate(cc):
        h_n = h_t[i] * lax.rsqrt(ms[i] + RMS_EPS) * nw_ref[:, h * dv:(h + 1) * dv]
        o_pre = pm_ref[crow[ci], o_off + h * dv:o_off + (h + 1) * dv]
        o_ref[crow[ci], h * dv:(h + 1) * dv] = _sigmoid(o_pre) * h_n

    @pl.when(t == pl.num_programs(1) - 1)
    def _():
        c_out_ref[0] = c_ref[...]
        n_out_ref[0] = n_ref[...]
        m_out_ref[0] = m_ref[...]


def _mlstm_prompt(pm, pif, bif, c0, n0, m0, nw, *, batch, seq, heads, dk, dv, tb):
    width = pm.shape[1]
    val = heads * dv
    tb = min(tb, seq)
    assert seq % tb == 0 and tb % CHUNK == 0
    nt = seq // tb
    nck = (tb // CHUNK) * heads
    return pl.pallas_call(
        functools.partial(_mlstm_prompt_kernel, tb=tb, heads=heads, dk=dk, dv=dv),
        grid=(batch, nt),
        in_specs=[pl.BlockSpec((tb, width), lambda b, t: (b * nt + t, 0)),
                  pl.BlockSpec((tb, 128), lambda b, t: (b * nt + t, 0)),
                  pl.BlockSpec((1, 128), lambda b, t: (0, 0)),
                  pl.BlockSpec((1, heads, dk, dv), lambda b, t: (b, 0, 0, 0)),
                  pl.BlockSpec((1, 8, dk), lambda b, t: (b, 0, 0)),
                  pl.BlockSpec((1, 8, 128), lambda b, t: (b, 0, 0)),
                  pl.BlockSpec((1, val), lambda b, t: (0, 0))],
        out_specs=[pl.BlockSpec((tb, val), lambda b, t: (b * nt + t, 0)),
                   pl.BlockSpec((1, heads, dk, dv), lambda b, t: (b, 0, 0, 0)),
                   pl.BlockSpec((1, 8, dk), lambda b, t: (b, 0, 0)),
                   pl.BlockSpec((1, 8, 128), lambda b, t: (b, 0, 0))],
        out_shape=[jax.ShapeDtypeStruct((batch * seq, val), F32),
                   jax.ShapeDtypeStruct((batch, heads, dk, dv), F32),
                   jax.ShapeDtypeStruct((batch, 8, dk), F32),
                   jax.ShapeDtypeStruct((batch, 8, 128), F32)],
        scratch_shapes=[pltpu.VMEM((heads, dk, dv), F32), pltpu.VMEM((8, dk), F32),
                        pltpu.VMEM((8, 128), F32),
                        pltpu.VMEM((tb, 128), F32), pltpu.VMEM((tb, val), F32), pltpu.VMEM((tb, 128), F32),
                        pltpu.VMEM((nck, dk, dv), F32), pltpu.VMEM((max(8, nck), dk), F32),
                        pltpu.VMEM((max(8, nck), 128), F32),
                        pltpu.VMEM((nck, dk, dv), F32), pltpu.VMEM((max(8, nck), dk), F32),
                        pltpu.VMEM((max(8, nck), 128), F32)],
        compiler_params=_cparams(("parallel", "arbitrary")),
        name="mlstm_prompt",
    )(pm, pif, bif, c0, n0, m0, nw)


def _mlstm_prep(w_in, b_if, norm_w, w_out, *, heads, dk, dv):
    main = 2 * heads * dk + 2 * heads * dv
    return dict(w_main=w_in[:, :main].astype(BF16),
                w_if=_pad_cols(w_in[:, main:], 128).astype(BF16),
                bif=_pad_cols(b_if[None, :], 128), nw=norm_w[None, :], w_out=w_out.astype(BF16))


def _mlstm_prompt_layer(x, g, p, c0, n0, m0, *, batch, seq, heads, dk, dv):
    pm, pif = _in_proj(x, g, p["w_main"], p["w_if"])
    n0p = jnp.pad(n0, ((0, 0), (0, 8 - heads), (0, 0)))
    m0p = jnp.broadcast_to(jnp.pad(m0, ((0, 0), (0, 8 - heads)))[:, :, None], (batch, 8, 128))
    o, c, n, m = _mlstm_prompt(pm, pif, p["bif"], c0, n0p, m0p, p["nw"],
                               batch=batch, seq=seq, heads=heads, dk=dk, dv=dv, tb=TB_MIX)
    return o, c, n[:, :heads, :], m[:, :heads, 0]


def _rwkv_proj_body(h, prev, mu_ref, wrkv_ref, w1_ref, w2_ref, a1_ref, a2_ref, g1_ref, g2_ref,
                    vec_ref, r_ref, k_ref, v_ref, lw_ref, kk_ref, a_ref, g_ref):
    xx = prev - h

    def mix(j):
        return (h + xx * mu_ref[j:j + 1, :]).astype(BF16)

    w0, a0, k_k, k_a = (vec_ref[j:j + 1, :] for j in range(4))
    r_ref[...] = jnp.dot(mix(0), wrkv_ref[0], preferred_element_type=F32)
    lora_w = _dot(jnp.tanh(_dot(mix(1), w1_ref[...])), w2_ref[...])
    w_log = -_softplus(-(w0 + lora_w)) - 0.5
    lw_ref[...] = -jnp.exp(w_log)
    k = jnp.dot(mix(2), wrkv_ref[1], preferred_element_type=F32)
    v_ref[...] = jnp.dot(mix(3), wrkv_ref[2], preferred_element_type=F32)
    a = _sigmoid(a0 + _dot(_dot(mix(4), a1_ref[...]), a2_ref[...]))
    g_ref[...] = _dot(_sigmoid(_dot(mix(5), g1_ref[...])), g2_ref[...])
    kk_ref[...] = k * k_k
    k_ref[...] = k * (1.0 + (a - 1.0) * k_a)
    a_ref[...] = a


def _rwkv_proj_sample_kernel(x_ref, gn_ref, prev_ref, *refs):
    h = _norm_rows(x_ref[...], gn_ref[...])
    hn_ref = refs[-1]
    hn_ref[...] = h
    _rwkv_proj_body(h, prev_ref[...], *refs[:-1])


def _rwkv_proj_prompt_kernel(x_ref, gn_ref, shift0_ref, *refs):
    carry_ref = refs[-1]
    shift_out_ref = refs[-2]
    t = pl.program_id(1)

    @pl.when(t == 0)
    def _():
        carry_ref[...] = shift0_ref[0]

    h = _norm_rows(x_ref[...], gn_ref[...])
    rows = h.shape[0]
    first = lax.broadcasted_iota(jnp.int32, h.shape, 0) == 0
    prev = jnp.where(first, carry_ref[0:1, :], pltpu.roll(h, 1, 0))
    carry_ref[0:1, :] = h[rows - 1:rows, :]
    _rwkv_proj_body(h, prev, *refs[:-2])

    @pl.when(t == pl.num_programs(1) - 1)
    def _():
        shift_out_ref[0] = carry_ref[...]


def _rwkv_proj(x, g_norm, shift0, p, *, batch, seq, tm):
    m, d = x.shape
    consts = [p["mu"], p["w_rkv"], p["w1"], p["w2"], p["a1"], p["a2"], p["g1"], p["g2"], p["vec"]]
    if seq == 1:
        row = pl.BlockSpec((m, d), lambda i: (0, 0))
        out = pl.pallas_call(
            _rwkv_proj_sample_kernel,
            grid=(1,),
            in_specs=[row, pl.BlockSpec((1, d), lambda i: (0, 0)), row] + [_const_block(a) for a in consts],
            out_specs=[row] * 8,
            out_shape=[jax.ShapeDtypeStruct((m, d), F32)] * 8,
            compiler_params=_cparams(("arbitrary",)),
            name="rwkv_proj_sample",
        )(x, g_norm, shift0, *consts)
        return out[:7], out[7]
    tm = min(tm, seq)
    assert seq % tm == 0
    nt = seq // tm
    row = pl.BlockSpec((tm, d), lambda b, t: (b * nt + t, 0))
    st = pl.BlockSpec((1, 8, d), lambda b, t: (b, 0, 0))

    def full(a):
        nd = a.ndim
        return pl.BlockSpec(a.shape, lambda b, t: (0,) * nd)

    shift0_p = jnp.pad(shift0[:, None, :], ((0, 0), (0, 7), (0, 0)))
    out = pl.pallas_call(
        _rwkv_proj_prompt_kernel,
        grid=(batch, nt),
        in_specs=[row, full(g_norm), st] + [full(a) for a in consts],
        out_specs=[row] * 7 + [st],
        out_shape=[jax.ShapeDtypeStruct((m, d), F32)] * 7 + [jax.ShapeDtypeStruct((batch, 8, d), F32)],
        scratch_shapes=[pltpu.VMEM((8, d), F32)],
        compiler_params=_cparams(("parallel", "arbitrary")),
        name="rwkv_proj_prompt",
    )(x, g_norm, shift0_p, *consts)
    return out[:7], out[7][:, 0, :]


def _rwkv_prompt_kernel(r_ref, k_ref, v_ref, lw_ref, kk_ref, a_ref, g_ref, s0_ref, hp_ref,
                        o_ref, s_out_ref, s_ref, rr_ref, yy_ref, mx_ref, n0_ref, gw_ref, gb_ref, el_ref,
                        *, tb, heads, hd):
    t = pl.program_id(1)
    l = CHUNK

    @pl.when(t == 0)
    def _():
        s_ref[...] = s0_ref[0]

    hh = range(heads)
    hs = [slice(h * hd, (h + 1) * hd) for h in hh]
    l2 = 2 * l
    r2 = lax.broadcasted_iota(jnp.int32, (l2, l2), 0)
    c2 = lax.broadcasted_iota(jnp.int32, (l2, l2), 1)
    same = (r2 >= l) == (c2 >= l)
    incl = same & (r2 >= c2)
    strict = same & (r2 > c2)
    tril = incl.astype(F32)
    first = lax.broadcasted_iota(jnp.int32, (l2, heads * hd), 0) < l
    first2 = (lax.broadcasted_iota(jnp.int32, (2 * l2, hd), 0) & l) == 0
    zeros = jnp.zeros((l2, hd), F32)

    pairs = range(tb // l2)
    prow = [slice(pi * l2, (pi + 1) * l2) for pi in pairs]
    lwc, e_in, e_prev, e_neg, e_end = [], [], [], [], []
    for pi in pairs:
        lw = lw_ref[prow[pi], :]
        c = _dot_hi(tril, lw)
        lwc.append(c)
        e_in.append(jnp.exp(c))
        e_prev.append(jnp.exp(c - lw))
        e_neg.append(jnp.exp(-c))
        e_end.append(jnp.exp(jnp.where(first, c[l - 1:l, :], c[l2 - 1:l2, :]) - c))
    cc = [(pi, h) for pi in pairs for h in hh]
    nc = range(len(cc))
    r = [r_ref[prow[pi], hs[h]] for pi, h in cc]
    v = [v_ref[prow[pi], hs[h]] for pi, h in cc]
    k = [k_ref[prow[pi], hs[h]] for pi, h in cc]
    kk = [kk_ref[prow[pi], hs[h]] for pi, h in cc]
    kk = [kk[i] * lax.rsqrt(jnp.maximum(jnp.sum(kk[i] * kk[i], axis=-1, keepdims=True), 1e-24)) for i in nc]
    bv = [kk[i] * a_ref[prow[pi], hs[h]] for i, (pi, h) in enumerate(cc)]
    a_t = [-kk[i] * e_prev[pi][:, hs[h]] for i, (pi, h) in enumerate(cc)]
    r_t = [r[i] * e_in[pi][:, hs[h]] for i, (pi, h) in enumerate(cc)]
    gm = [_dot_nt(jnp.concatenate([a_t[i], r_t[i]], axis=0),
                  jnp.concatenate([bv[i] * e_neg[pi][:, hs[h]], k[i] * e_neg[pi][:, hs[h]]], axis=0))
          for i, (pi, h) in enumerate(cc)]
    ak_m = [jnp.where(strict, gm[i][:l2, l2:], 0.0).astype(BF16) for i in nc]
    rbk_m = [jnp.concatenate([jnp.where(incl, gm[i][l2:, :l2], 0.0),
                              jnp.where(incl, gm[i][l2:, l2:], 0.0)], axis=1).astype(BF16) for i in nc]
    t_inv = _unit_lower_inverse([jnp.where(strict, -gm[i][:l2, :l2], 0.0) for i in nc], l)
    vb = [v[i].astype(BF16) for i in nc]
    akv = [_dot(ak_m[i], vb[i]) for i in nc]
    x1 = [_dot(t_inv[i], jnp.concatenate([a_t[i], akv[i]], axis=1)).astype(BF16) for i in nc]
    low = [jnp.concatenate([x1[i], jnp.concatenate([zeros.astype(BF16), vb[i]], axis=1)], axis=0) for i in nc]
    x2 = [_dot(rbk_m[i], low[i]) for i in nc]
    bk = [jnp.concatenate([bv[i] * e_end[pi][:, hs[h]], k[i] * e_end[pi][:, hs[h]]], axis=0)
          for i, (pi, h) in enumerate(cc)]
    bk2 = [jnp.concatenate([jnp.where(first2, bk[i], 0.0), jnp.where(first2, 0.0, bk[i])], axis=1) for i in nc]
    mn = [_dot_tn(low[i], bk2[i]) for i in nc]
    for i, (pi, h) in enumerate(cc):
        for c in range(2):
            crow = slice(pi * l2 + c * l, pi * l2 + (c + 1) * l)
            mx_ref[h, crow, :] = mn[i][:hd, c * hd:(c + 1) * hd].astype(BF16)
            n0_ref[h, crow, :] = mn[i][hd:, c * hd:(c + 1) * hd]
        rr_ref[h, prow[pi], :] = (r_t[i] + x2[i][:, :hd]).astype(BF16)
        yy_ref[h, prow[pi], :] = x2[i][:, hd:]
        g = g_ref[prow[pi], hs[h]]
        bonus = jnp.sum(r[i] * k[i] * hp_ref[0:1, hs[h]], axis=-1, keepdims=True) * v[i]
        gw_ref[h, prow[pi], :] = hp_ref[1:2, hs[h]] * g
        gb_ref[h, prow[pi], :] = (hp_ref[2:3, hs[h]] + bonus) * g
        el_ref[h, 2 * pi:2 * pi + 1, :] = jnp.exp(lwc[pi][l - 1:l, hs[h]])
        el_ref[h, 2 * pi + 1:2 * pi + 2, :] = jnp.exp(lwc[pi][l2 - 1:l2, hs[h]])

    for ci in range(tb // l):
        rows = slice(ci * l, (ci + 1) * l)
        s = [s_ref[h] for h in hh]
        y = [_dot_nt(rr_ref[h, rows, :], s[h]) for h in hh]
        sm = [_dot(s[h], mx_ref[h, rows, :]) for h in hh]
        for h in hh:
            s_ref[h] = s[h] * el_ref[h, ci:ci + 1, :] + sm[h] + n0_ref[h, rows, :]
        y = [y[h] + yy_ref[h, rows, :] for h in hh]
        mean = [jnp.mean(y[h], axis=-1, keepdims=True) for h in hh]
        yc = [y[h] - mean[h] for h in hh]
        var = [jnp.mean(yc[h] * yc[h], axis=-1, keepdims=True) for h in hh]
        for h in hh:
            o_ref[rows, hs[h]] = yc[h] * lax.rsqrt(var[h] + RW_GN_EPS) * gw_ref[h, rows, :] + gb_ref[h, rows, :]

    @pl.when(t == pl.num_programs(1) - 1)
    def _():
        s_out_ref[0] = s_ref[...]


def _rwkv_prompt(r, k, v, lw, kk, a, g, s0, hp, *, batch, seq, heads, hd, tb):
    d = heads * hd
    tb = min(tb, seq)
    assert seq % tb == 0 and tb % CHUNK == 0
    nt = seq // tb
    row = pl.BlockSpec((tb, d), lambda b, t: (b * nt + t, 0))
    st = pl.BlockSpec((1, heads, hd, hd), lambda b, t: (b, 0, 0, 0))
    return pl.pallas_call(
        functools.partial(_rwkv_prompt_kernel, tb=tb, heads=heads, hd=hd),
        grid=(batch, nt),
        in_specs=[row] * 7 + [st, pl.BlockSpec((8, d), lambda b, t: (0, 0))],
        out_specs=[row, st],
        out_shape=[jax.ShapeDtypeStruct((batch * seq, d), F32),
                   jax.ShapeDtypeStruct((batch, heads, hd, hd), F32)],
        scratch_shapes=[pltpu.VMEM((heads, hd, hd), F32),
                        pltpu.VMEM((heads, tb, hd), BF16), pltpu.VMEM((heads, tb, hd), F32),
                        pltpu.VMEM((heads, tb, hd), BF16), pltpu.VMEM((heads, tb, hd), F32),
                        pltpu.VMEM((heads, tb, hd), F32), pltpu.VMEM((heads, tb, hd), F32),
                        pltpu.VMEM((heads, max(8, tb // CHUNK), hd), F32)],
        compiler_params=_cparams(("parallel", "arbitrary")),
        name="rwkv_prompt",
    )(r, k, v, lw, kk, a, g, s0, hp)


def _pad_rows(a, n):
    return jnp.pad(a, ((0, n - a.shape[0]), (0, 0)))


def _rwkv_prep(mu, w_rkv, w_o, w0, w1, w2, a0, a1, a2, g1, g2, k_k, k_a, r_k, lnx_w, lnx_b):
    d = w0.shape[0]
    lw = -(-w1.shape[1] // 128) * 128
    la = -(-a1.shape[1] // 128) * 128
    lg = -(-g1.shape[1] // 128) * 128
    return dict(mu=_pad_rows(mu, 8), w_rkv=w_rkv.astype(BF16), w_o=w_o.astype(BF16),
                w1=_pad_cols(w1, lw).astype(BF16), w2=_pad_rows(w2, lw).astype(BF16),
                a1=_pad_cols(a1, la).astype(BF16), a2=_pad_rows(a2, la).astype(BF16),
                g1=_pad_cols(g1, lg).astype(BF16), g2=_pad_rows(g2, lg).astype(BF16),
                vec=_pad_rows(jnp.stack([w0, a0, k_k, k_a]), 8),
                hp=_pad_rows(jnp.stack([r_k.reshape(d), lnx_w, lnx_b]), 8))


def _rwkv_prompt_layer(x, g_norm, p, shift0, s0, *, batch, seq, heads, hd):
    (r, k, v, lw, kk, a, g), shift = _rwkv_proj(x, g_norm, shift0, p, batch=batch, seq=seq, tm=TM_RWKV_PROJ)
    o, s = _rwkv_prompt(r, k, v, lw, kk, a, g, s0, p["hp"], batch=batch, seq=seq, heads=heads, hd=hd, tb=TB_MIX)
    return o, shift, s


def _gdn_sample_pre_kernel(pm_ref, pba_ref, conv_ref, cw_ref, gp_ref, qkv_ref, conv_out_ref, sc_ref,
                           *, heads, dk, dv):
    key = heads * dk
    ch = 2 * key + heads * dv
    u = pm_ref[:, 0:ch]
    y = u * cw_ref[CONV_W - 1:CONV_W, :]
    for j in range(CONV_W - 1):
        y = y + conv_ref[j] * cw_ref[j:j + 1, :]
        conv_out_ref[j] = conv_ref[j + 1] if j + 1 < CONV_W - 1 else u
    y = _silu(y)
    for c in range(ch // 128):
        cs = slice(c * 128, (c + 1) * 128)
        yc = y[:, cs]
        if c * 128 < key:
            yc = yc * lax.rsqrt(jnp.sum(yc * yc, axis=-1, keepdims=True) + 1e-6) * (dk ** -0.5)
        elif c * 128 < 2 * key:
            yc = yc * lax.rsqrt(jnp.sum(yc * yc, axis=-1, keepdims=True) + 1e-6)
        qkv_ref[:, cs] = yc
    ba = pba_ref[...]
    lane = lax.broadcasted_iota(jnp.int32, ba.shape, 1)
    g = -jnp.exp(gp_ref[0:1, :]) * _softplus(ba + gp_ref[1:2, :])
    sc_ref[...] = jnp.where(lane < heads, _sigmoid(ba), jnp.exp(g))


def _gdn_sample_pre(pm, pba, conv_t, cw_t, gp, *, heads, dk, dv):
    n = pm.shape[0]
    ch = 2 * heads * dk + heads * dv
    return pl.pallas_call(
        functools.partial(_gdn_sample_pre_kernel, heads=heads, dk=dk, dv=dv),
        out_shape=[jax.ShapeDtypeStruct((n, ch), F32),
                   jax.ShapeDtypeStruct((CONV_W - 1, n, ch), F32),
                   jax.ShapeDtypeStruct((n, 128), F32)],
        compiler_params=pltpu.CompilerParams(vmem_limit_bytes=V7X_VMEM_LIMIT),
        name="gdn_sample_pre",
    )(pm, pba, conv_t, cw_t, gp)


SEQ_PER_STEP = 4


def _seq_block(a, nb):
    nd = a.ndim
    return pl.BlockSpec((nb,) + a.shape[1:], lambda b: (b,) + (0,) * (nd - 1))


def _const_block(a):
    nd = a.ndim
    return pl.BlockSpec(a.shape, lambda b: (0,) * nd)


def _gdn_sample_step_kernel(s0_ref, cols_ref, v_ref, z_ref, sc_ref, nw_ref, *refs, heads, nb, n_prev):
    prev_refs = refs[:n_prev]
    s_out_all, o_ref = refs[n_prev:]
    for p, prev_ref in enumerate(prev_refs):
        s_out_all[p] = prev_ref[...]
    s_out_ref = s_out_all.at[n_prev] if n_prev else s_out_all
    hh = range(heads)
    for i in range(nb):
        kc = [cols_ref[i, :, h:h + 1] for h in hh]
        qc = [cols_ref[i, :, heads + h:heads + h + 1] for h in hh]
        s0 = [s0_ref[i, h] for h in hh]
        eg = [sc_ref[i, h:h + 1, 1:2] for h in hh]
        ks = [jnp.sum(kc[h] * s0[h], axis=0, keepdims=True) for h in hh]
        s1 = [eg[h] * s0[h] + kc[h] * (sc_ref[i, h:h + 1, 0:1] * (v_ref[i, h:h + 1, :] - eg[h] * ks[h]))
              for h in hh]
        o = [jnp.sum(qc[h] * s1[h], axis=0, keepdims=True) for h in hh]
        ms = [jnp.mean(o[h] * o[h], axis=-1, keepdims=True) for h in hh]
        for h in hh:
            s_out_ref[i, h] = s1[h]
            o_ref[i, h:h + 1, :] = o[h] * lax.rsqrt(ms[h] + RMS_EPS) * nw_ref[...] * _silu(z_ref[i, h:h + 1, :])


def _gdn_sample_step(s_all, layer, prev_new, cols, v, z, sc, nw):
    n_layers, n, heads, dk, dv = s_all.shape
    nb = SEQ_PER_STEP
    assert n % nb == 0
    last = layer == n_layers - 1
    n_prev = len(prev_new) if last else 0
    one = pl.BlockSpec((nb, heads, dk, dv), lambda b: (b, 0, 0, 0))
    if n_prev:
        out_state = pl.BlockSpec((n_layers, nb, heads, dk, dv), lambda b: (0, b, 0, 0, 0))
        out_shape = jax.ShapeDtypeStruct(s_all.shape, F32)
    else:
        out_state, out_shape = one, jax.ShapeDtypeStruct(s_all.shape[1:], F32)
    return pl.pallas_call(
        functools.partial(_gdn_sample_step_kernel, heads=heads, nb=nb, n_prev=n_prev),
        grid=(n // nb,),
        in_specs=[pl.BlockSpec((None, nb, heads, dk, dv), lambda b: (layer, b, 0, 0, 0)),
                  _seq_block(cols, nb), _seq_block(v, nb), _seq_block(z, nb), _seq_block(sc, nb),
                  pl.BlockSpec((1, dv), lambda b: (0, 0))] + [one] * n_prev,
        out_specs=[out_state, _seq_block(v, nb)],
        out_shape=[out_shape, jax.ShapeDtypeStruct(v.shape, F32)],
        compiler_params=_cparams(("parallel",)),
        name="gdn_sample_step",
    )(s_all, cols, v, z, sc, nw, *(prev_new if n_prev else []))


def _gdn_sample_layer(x, g, p, conv0, s_all, layer, prev_new, *, heads, dk, dv):
    n = x.shape[0]
    key, val = heads * dk, heads * dv
    ch = 2 * key + val
    pm, pba = _in_proj(x, g, p["w_main"], p["w_ba"])
    qkv, conv_t, sc = _gdn_sample_pre(pm, pba, jnp.transpose(conv0, (1, 0, 2)), p["cw_t"], p["gp"],
                                      heads=heads, dk=dk, dv=dv)
    q_c = jnp.transpose(qkv[:, :key].reshape(n, heads, dk), (0, 2, 1))
    k_c = jnp.transpose(qkv[:, key:2 * key].reshape(n, heads, dk), (0, 2, 1))
    cols = jnp.concatenate([k_c, q_c], axis=-1)
    sc3 = jnp.stack([sc[:, :heads], sc[:, heads:2 * heads]], axis=-1)
    s_new, o = _gdn_sample_step(s_all, layer, prev_new, cols, qkv[:, 2 * key:].reshape(n, heads, dv),
                                pm[:, ch:].reshape(n, heads, dv), sc3, p["nw"])
    return o.reshape(n, val), jnp.transpose(conv_t, (1, 0, 2)), s_new


def _mlstm_sample_step_kernel(c0_ref, n0_ref, cols_ref, q_ref, k_ref, v_ref, op_ref, sc_ref, bif_ref, nw_ref,
                              c_out_ref, n_out_ref, m_out_ref, o_ref, *, heads, dk, nb):
    scale = dk ** -0.5
    for i in range(nb):
        gi = sc_ref[i, :, 0:1] + bif_ref[:, 0:1]
        gf = sc_ref[i, :, 1:2] + bif_ref[:, 1:2]
        m0 = sc_ref[i, :, 2:3]
        gi = GATE_CAP * jnp.tanh(gi / GATE_CAP)
        logf = _log_sigmoid(GATE_CAP * jnp.tanh(gf / GATE_CAP))
        m_new = jnp.maximum(logf + m0, gi)
        f_s = jnp.exp(logf + m0 - m_new)
        i_s = jnp.exp(gi - m_new)
        m_out_ref[i] = m_new
        n1 = f_s * n0_ref[i] + i_s * k_ref[i]
        n_out_ref[i] = n1
        den = jnp.sum(q_ref[i] * scale * n1, axis=-1, keepdims=True)
        floor = jnp.exp(-m_new)
        hh = range(heads)
        kc = [cols_ref[i, :, h:h + 1] for h in hh]
        qc = [cols_ref[i, :, heads + h:heads + h + 1] * scale for h in hh]
        c1 = [f_s[h:h + 1, :] * c0_ref[i, h] + i_s[h:h + 1, :] * (kc[h] * v_ref[i, h:h + 1, :]) for h in hh]
        num = [jnp.sum(qc[h] * c1[h], axis=0, keepdims=True) for h in hh]
        h_t = [num[h] / jnp.maximum(jnp.abs(den[h:h + 1, :]), floor[h:h + 1, :]) for h in hh]
        ms = [jnp.mean(h_t[h] * h_t[h], axis=-1, keepdims=True) for h in hh]
        for h in hh:
            c_out_ref[i, h] = c1[h]
            h_n = h_t[h] * lax.rsqrt(ms[h] + RMS_EPS) * nw_ref[h:h + 1, :]
            o_ref[i, h:h + 1, :] = _sigmoid(op_ref[i, h:h + 1, :]) * h_n


def _mlstm_sample_step(c0, n0, cols, q, k, v, o_pre, sc, bif2, nw2):
    n, heads, dk, dv = c0.shape
    nb = SEQ_PER_STEP
    assert n % nb == 0
    full = _const_block

    def blk(a):
        return _seq_block(a, nb)

    m_shape = (n, heads, 1)
    return pl.pallas_call(
        functools.partial(_mlstm_sample_step_kernel, heads=heads, dk=dk, nb=nb),
        grid=(n // nb,),
        in_specs=[blk(c0), blk(n0), blk(cols), blk(q), blk(k), blk(v), blk(o_pre), blk(sc), full(bif2), full(nw2)],
        out_specs=[blk(c0), blk(n0), pl.BlockSpec((nb, heads, 1), lambda b: (b, 0, 0)), blk(v)],
        out_shape=[jax.ShapeDtypeStruct(c0.shape, F32), jax.ShapeDtypeStruct(n0.shape, F32),
                   jax.ShapeDtypeStruct(m_shape, F32), jax.ShapeDtypeStruct(v.shape, F32)],
        compiler_params=_cparams(("parallel",)),
        name="mlstm_sample_step",
    )(c0, n0, cols, q, k, v, o_pre, sc, bif2, nw2)


def _mlstm_sample_layer(x, g, p, c0, n0, m0, *, heads, dk, dv):
    n = x.shape[0]
    qk_w, val = heads * dk, heads * dv
    pm, pif = _in_proj(x, g, p["w_main"], p["w_if"])
    q = pm[:, :qk_w].reshape(n, heads, dk)
    k = pm[:, qk_w:2 * qk_w].reshape(n, heads, dk)
    v = pm[:, 2 * qk_w:2 * qk_w + val].reshape(n, heads, dv)
    o_pre = pm[:, 2 * qk_w + val:].reshape(n, heads, dv)
    cols = jnp.concatenate([jnp.transpose(k, (0, 2, 1)), jnp.transpose(q, (0, 2, 1))], axis=-1)
    sc = jnp.stack([pif[:, :heads], pif[:, heads:2 * heads], m0], axis=-1)
    bif2 = jnp.stack([p["bif"][0, :heads], p["bif"][0, heads:2 * heads]], axis=-1)
    c1, n1, m1, o = _mlstm_sample_step(c0, n0, cols, q, k, v, o_pre, sc, bif2, p["nw"].reshape(heads, dv))
    return o.reshape(n, val), c1, n1, m1[:, :, 0]


RW_ROW_GROUP = 8


def _rwkv_sample_step_kernel(s_ref, r_ref, k_ref, lw_ref, kk_ref, a_ref, v_ref, g_ref, hp_ref,
                             s_out_ref, o_ref, y_ref, *, hd):
    kk = kk_ref[0]
    kk = kk * lax.rsqrt(jnp.maximum(jnp.sum(kk * kk, axis=0, keepdims=True), 1e-24))
    av = -kk
    bv = kk * a_ref[0]
    w = jnp.exp(lw_ref[0])
    r = r_ref[0]
    k = k_ref[0]
    v = v_ref[0]
    for v0 in range(0, hd, RW_ROW_GROUP):
        vv = range(v0, v0 + RW_ROW_GROUP)
        s0 = [s_ref[0, i] for i in vv]
        sa = [jnp.sum(s * av, axis=0, keepdims=True) for s in s0]
        s1 = [s * w + sa_i * bv + v[i:i + 1, :] * k for s, sa_i, i in zip(s0, sa, vv)]
        y = [jnp.sum(s * r, axis=0, keepdims=True) for s in s1]
        for i, s, y_i in zip(vv, s1, y):
            s_out_ref[0, i] = s
            y_ref[i:i + 1, :] = y_i
    y = y_ref[...]
    yc = y - jnp.mean(y, axis=0, keepdims=True)
    var = jnp.mean(yc * yc, axis=0, keepdims=True)
    hp = hp_ref[0]
    yn = yc * lax.rsqrt(var + RW_GN_EPS) * hp[:, 1:2] + hp[:, 2:3]
    bonus = jnp.sum(r * k * hp[:, 0:1], axis=0, keepdims=True)
    o_ref[0] = (yn + bonus * v) * g_ref[0]


def _rwkv_sample_step(s_t, r, k, lw, kk, a, v, g, hp3):
    heads, hd, _, n = s_t.shape

    def blk(z):
        nd = z.ndim
        return pl.BlockSpec((1,) + z.shape[1:], lambda h: (h,) + (0,) * (nd - 1))

    return pl.pallas_call(
        functools.partial(_rwkv_sample_step_kernel, hd=hd),
        grid=(heads,),
        in_specs=[blk(s_t)] + [blk(z) for z in (r, k, lw, kk, a, v, g, hp3)],
        out_specs=[blk(s_t), blk(v)],
        out_shape=[jax.ShapeDtypeStruct(s_t.shape, F32), jax.ShapeDtypeStruct(v.shape, F32)],
        scratch_shapes=[pltpu.VMEM((hd, n), F32)],
        compiler_params=_cparams(("parallel",)),
        name="rwkv_sample_step",
    )(s_t, r, k, lw, kk, a, v, g, hp3)


def _rwkv_sample_layer(x, g_norm, p, shift0, s0, *, heads, hd):
    n, d = x.shape
    (r, k, v, lw, kk, a, g), hn = _rwkv_proj(x, g_norm, shift0, p, batch=n, seq=1, tm=n)

    def lanes(z):
        return z.T.reshape(heads, hd, n)

    hp = p["hp"]
    hp3 = jnp.stack([hp[j].reshape(heads, hd) for j in range(3)], axis=-1)
    s1_t, o_t = _rwkv_sample_step(jnp.transpose(s0, (1, 2, 3, 0)), lanes(r), lanes(k), lanes(lw), lanes(kk),
                                  lanes(a), lanes(v), lanes(g), hp3)
    return o_t.reshape(d, n).T, hn, jnp.transpose(s1_t, (3, 0, 1, 2))


def _pad_cols(a, n):
    return jnp.pad(a, ((0, 0), (0, n - a.shape[1])))


def _gdn_prep(w_in, conv_w, a_log, dt_bias, norm_w, w_out, *, heads, dk, dv):
    key, val = heads * dk, heads * dv
    ch = 2 * key + val
    main = ch + val
    gp = jnp.zeros((8, 128), F32)
    gp = gp.at[0, heads:2 * heads].set(a_log).at[1, heads:2 * heads].set(dt_bias)
    return dict(w_main=w_in[:, :main].astype(BF16),
                w_ba=_pad_cols(w_in[:, main:], 128).astype(BF16),
                cw_t=jnp.pad(conv_w.T, ((0, 8 - CONV_W), (0, 0))),
                gp=gp, nw=norm_w[None, :], w_out=w_out.astype(BF16))


def _gdn_prompt_layer(x, g, p, conv0, s0, *, batch, seq, heads, dk, dv):
    conv0 = jnp.pad(conv0, ((0, 0), (8 - (CONV_W - 1), 0), (0, 0)))
    qkvz, pba, conv = _gdn_in_proj(x, g, p["w_main"], p["w_ba"], conv0, p["cw_t"], batch=batch, seq=seq,
                                   heads=heads, dk=dk, dv=dv, tm=TM_GDN_PROJ)
    o, s = _gdn_prompt(qkvz, pba, s0, p["gp"], p["nw"], batch=batch, seq=seq, heads=heads, dk=dk, dv=dv, tb=TB_MIX)
    return o, conv[:, 8 - (CONV_W - 1):, :], s


def _trunk(x, states, w, *, batch, seq):
    conv_in, gs_in, c_in, n_in, m_in, shift_in, rs_in = states
    depth = w["norm_mix"].shape[0]
    gh, gdk, gdv = gs_in.shape[2:]
    mh, mdk, mdv = c_in.shape[2:]
    rh, rhd = rs_in.shape[2:4]
    prompt = seq > 1
    outs = [[] for _ in range(7)]
    gs_new = []
    for i in range(depth):
        j = i // 3
        g = w["norm_mix"][i][None, :]
        if i % 3 == 0:
            p = w["gdn"][j]
            if prompt:
                mix, cb, s = _gdn_prompt_layer(x, g, p, conv_in[j], gs_in[j], batch=batch, seq=seq,
                                               heads=gh, dk=gdk, dv=gdv)
                outs[1].append(s)
            else:
                mix, cb, s = _gdn_sample_layer(x, g, p, conv_in[j], gs_in, j, gs_new, heads=gh, dk=gdk, dv=gdv)
                gs_new.append(s)
            outs[0].append(cb)
            w_out = p["w_out"]
        elif i % 3 == 1:
            p = w["ml"][j]
            if prompt:
                mix, c, n, m = _mlstm_prompt_layer(x, g, p, c_in[j], n_in[j], m_in[j], batch=batch, seq=seq,
                                                   heads=mh, dk=mdk, dv=mdv)
            else:
                mix, c, n, m = _mlstm_sample_layer(x, g, p, c_in[j], n_in[j], m_in[j], heads=mh, dk=mdk, dv=mdv)
            w_out = p["w_out"]
            outs[2].append(c)
            outs[3].append(n)
            outs[4].append(m)
        else:
            p = w["rw"][j]
            if prompt:
                mix, sh, s = _rwkv_prompt_layer(x, g, p, shift_in[j], rs_in[j], batch=batch, seq=seq,
                                                heads=rh, hd=rhd)
            else:
                mix, sh, s = _rwkv_sample_layer(x, g, p, shift_in[j], rs_in[j], heads=rh, hd=rhd)
            w_out = p["w_o"]
            outs[5].append(sh)
            outs[6].append(s)
        g_out = w["norm_final"][None, :] if i == depth - 1 else None
        x = _ffn(x, mix, w_out, w["norm_ffn"][i][None, :], w["ffn_w1"], w["ffn_w2"], i, g_out,
                 tm=TM_FFN, tf=TF_FFN)
    y = x
    new = [jnp.stack(z, axis=0) if z else None for z in outs]
    if not prompt:
        new[1] = gs_new[-1] if len(gs_new) > 1 else gs_new[0][None]
    return y, tuple(new)


def kernel(x_prompt, x_sample, state_gdn_conv, state_gdn_S, state_mlstm_C, state_mlstm_n, state_mlstm_m, state_rwkv_shift, state_rwkv_S, norm_mix, norm_ffn, norm_final, gdn_w_in, gdn_conv_w, gdn_a_log, gdn_dt_bias, gdn_norm_w, gdn_w_out, ml_w_in, ml_b_if, ml_norm_w, ml_w_out, rw_mu, rw_w_rkv, rw_w_o, rw_w0, rw_w1, rw_w2, rw_a0, rw_a1, rw_a2, rw_g1, rw_g2, rw_k_k, rw_k_a, rw_r_k, rw_lnx_w, rw_lnx_b, ffn_w1, ffn_w2):
    gh, gdk, gdv = state_gdn_S.shape[2:]
    mh, mdk, mdv = state_mlstm_C.shape[2:]
    w = dict(
        norm_mix=norm_mix, norm_ffn=norm_ffn, norm_final=norm_final,
        ffn_w1=ffn_w1.astype(BF16), ffn_w2=ffn_w2.astype(BF16),
        gdn=[_gdn_prep(gdn_w_in[j], gdn_conv_w[j], gdn_a_log[j], gdn_dt_bias[j], gdn_norm_w[j], gdn_w_out[j],
                       heads=gh, dk=gdk, dv=gdv) for j in range(gdn_w_in.shape[0])],
        ml=[_mlstm_prep(ml_w_in[j], ml_b_if[j], ml_norm_w[j], ml_w_out[j], heads=mh, dk=mdk, dv=mdv)
            for j in range(ml_w_in.shape[0])],
        rw=[_rwkv_prep(rw_mu[j], rw_w_rkv[j], rw_w_o[j], rw_w0[j], rw_w1[j], rw_w2[j], rw_a0[j], rw_a1[j],
                       rw_a2[j], rw_g1[j], rw_g2[j], rw_k_k[j], rw_k_a[j], rw_r_k[j], rw_lnx_w[j], rw_lnx_b[j])
            for j in range(rw_mu.shape[0])])
    sample_states = (state_gdn_conv, state_gdn_S, state_mlstm_C, state_mlstm_n, state_mlstm_m,
                     state_rwkv_shift, state_rwkv_S)
    bp, tp, d = x_prompt.shape
    bs, ts, _ = x_sample.shape
    assert ts == 1
    prompt_states = tuple(jnp.zeros((s.shape[0], bp) + s.shape[2:], s.dtype) for s in sample_states)
    y_p, new_p = _trunk(x_prompt.reshape(bp * tp, d), prompt_states, w, batch=bp, seq=tp)
    y_s, new_s = _trunk(x_sample.reshape(bs * ts, d), sample_states, w, batch=bs, seq=ts)
    out = [y_p.reshape(bp, tp, d), y_s.reshape(bs, ts, d)]
    for a, b in zip(new_p, new_s):
        out += [a, b]
    return tuple(out)
```

```python
import functools

import jax
import jax.numpy as jnp
from jax import lax
from jax.experimental import pallas as pl
from jax.experimental.pallas import tpu as pltpu

F32 = jnp.float32
BF16 = jnp.bfloat16

RMS_EPS = 1e-6
NEG_BIG = -1e30
GATE_CAP = 15.0
RW_GN_EPS = 64e-5
CONV_W = 4
CHUNK = 64
V7X_VMEM_LIMIT = 56 * 1024 * 1024
HI = lax.Precision.HIGHEST


def _cparams(sem):
    return pltpu.CompilerParams(dimension_semantics=sem, vmem_limit_bytes=V7X_VMEM_LIMIT)


def _dot(a, b):
    return jnp.dot(a.astype(BF16), b.astype(BF16), preferred_element_type=F32)


def _dot_nt(a, b):
    return lax.dot_general(a.astype(BF16), b.astype(BF16), (((1,), (1,)), ((), ())),
                           preferred_element_type=F32)


def _dot_tn(a, b):
    return lax.dot_general(a.astype(BF16), b.astype(BF16), (((0,), (0,)), ((), ())),
                           preferred_element_type=F32)


def _dot_hi(a, b):
    return jnp.dot(a, b, preferred_element_type=F32, precision=HI)


def _sigmoid(x):
    return 1.0 / (1.0 + jnp.exp(-x))


def _silu(x):
    return x * _sigmoid(x)


def _softplus(x):
    return jnp.maximum(x, 0.0) + jnp.log(1.0 + jnp.exp(-jnp.abs(x)))


def _log_sigmoid(x):
    return -_softplus(-x)


def _tri_masks(l):
    r = lax.broadcasted_iota(jnp.int32, (l, l), 0)
    c = lax.broadcasted_iota(jnp.int32, (l, l), 1)
    return r >= c, r > c


INV_BASE = 16


def _unit_lower_inverse(mats, l):
    n = mats[0].shape[0]
    r = lax.broadcasted_iota(jnp.int32, (n, n), 0)
    c = lax.broadcasted_iota(jnp.int32, (n, n), 1)
    eye = (r == c).astype(F32)
    size = min(INV_BASE, l)
    shift = size.bit_length() - 1
    diag = (r >> shift) == (c >> shift)
    merges = []
    s = size
    while s < l:
        sh = s.bit_length() - 1
        off = ((r >> (sh + 1)) == (c >> (sh + 1))) & ((r >> sh) > (c >> sh))
        merges.append([jnp.where(off, a, 0.0).astype(BF16) for a in mats])
        s *= 2
    t = [eye - jnp.where(diag, a, 0.0) for a in mats]
    tb = [ti.astype(BF16) for ti in t]
    ab = [jnp.where(diag, a, 0.0).astype(BF16) for a in mats]
    p = [_dot(a, a).astype(BF16) for a in ab]
    k = 2
    while k < size:
        if 2 * k < size:
            both = [_dot(pi, jnp.concatenate([ti, pi], axis=1)) for ti, pi in zip(tb, p)]
            t = [ti + bi[:, :n] for ti, bi in zip(t, both)]
            p = [bi[:, n:].astype(BF16) for bi in both]
        else:
            t = [ti + _dot(pi, ti_b) for ti, ti_b, pi in zip(t, tb, p)]
        tb = [ti.astype(BF16) for ti in t]
        k *= 2
    for a_off in merges:
        x = [_dot(a, ti) for a, ti in zip(a_off, tb)]
        y = [_dot(ti, xi) for ti, xi in zip(tb, x)]
        t = [ti - yi for ti, yi in zip(t, y)]
        tb = [ti.astype(BF16) for ti in t]
    return tb


TM_IN_PROJ = 512
TM_GDN_PROJ = 512
TM_FFN, TF_FFN = 1024, 1024
TB_MIX = 256
TM_RWKV_PROJ = 512


def _norm_rows(x, g):
    return x * lax.rsqrt(jnp.mean(x * x, axis=-1, keepdims=True) + RMS_EPS) * g


def _in_proj_kernel(x_ref, g_ref, w_ref, wa_ref, o_ref, oa_ref):
    xn = _norm_rows(x_ref[...], g_ref[...]).astype(BF16)
    oa_ref[...] = jnp.dot(xn, wa_ref[...], preferred_element_type=F32)
    o_ref[...] = jnp.dot(xn, w_ref[...], preferred_element_type=F32)


def _in_proj(x, g, w_main, w_aux):
    m, k = x.shape
    n, na = w_main.shape[1], w_aux.shape[1]
    tm = min(TM_IN_PROJ, m)
    assert m % tm == 0
    return pl.pallas_call(
        _in_proj_kernel,
        grid=(m // tm,),
        in_specs=[pl.BlockSpec((tm, k), lambda i: (i, 0)), pl.BlockSpec((1, k), lambda i: (0, 0)),
                  pl.BlockSpec((k, n), lambda i: (0, 0)), pl.BlockSpec((k, na), lambda i: (0, 0))],
        out_specs=[pl.BlockSpec((tm, n), lambda i: (i, 0)), pl.BlockSpec((tm, na), lambda i: (i, 0))],
        out_shape=[jax.ShapeDtypeStruct((m, n), F32), jax.ShapeDtypeStruct((m, na), F32)],
        compiler_params=_cparams(("parallel",)),
        name="in_proj",
    )(x, g, w_main, w_aux)


def _ffn_kernel(res_ref, mix_ref, wo_ref, g_ref, w1_ref, w2_ref, go_ref, o_ref, xn_ref, acc_ref, *, out_norm):
    f = pl.program_id(1)

    @pl.when(f == 0)
    def _():
        x = res_ref[...] + jnp.dot(mix_ref[...].astype(BF16), wo_ref[...], preferred_element_type=F32)
        o_ref[...] = x
        xn_ref[...] = _norm_rows(x, g_ref[...]).astype(BF16)
        acc_ref[...] = jnp.zeros_like(acc_ref)

    h = jnp.dot(xn_ref[...], w1_ref[...], preferred_element_type=F32)
    a = jnp.square(jnp.maximum(h, 0.0)).astype(BF16)
    acc_ref[...] += jnp.dot(a, w2_ref[...], preferred_element_type=F32)

    @pl.when(f == pl.num_programs(1) - 1)
    def _():
        y = o_ref[...] + acc_ref[...]
        o_ref[...] = _norm_rows(y, go_ref[...]) if out_norm else y


def _ffn(res, mix, w_out, g, w1, w2, layer, g_out=None, *, tm, tf):
    m, d = res.shape
    dff = w1.shape[2]
    tm, tf = min(tm, m), min(tf, dff)
    assert m % tm == 0 and dff % tf == 0
    vec = pl.BlockSpec((1, d), lambda i, j: (0, 0))
    return pl.pallas_call(
        functools.partial(_ffn_kernel, out_norm=g_out is not None),
        grid=(m // tm, dff // tf),
        in_specs=[pl.BlockSpec((tm, d), lambda i, j: (i, 0)),
                  pl.BlockSpec((tm, mix.shape[1]), lambda i, j: (i, 0)),
                  pl.BlockSpec(w_out.shape, lambda i, j: (0, 0)), vec,
                  pl.BlockSpec((None, d, tf), lambda i, j: (layer, 0, j)),
                  pl.BlockSpec((None, tf, d), lambda i, j: (layer, j, 0)), vec],
        out_specs=pl.BlockSpec((tm, d), lambda i, j: (i, 0)),
        out_shape=jax.ShapeDtypeStruct((m, d), F32),
        scratch_shapes=[pltpu.VMEM((tm, d), BF16), pltpu.VMEM((tm, d), F32)],
        compiler_params=_cparams(("parallel", "arbitrary")),
        name="ffn",
    )(res, mix, w_out, g, w1, w2, g if g_out is None else g_out)


def _gdn_prompt_kernel(qkv_ref, pba_ref, s0_ref, gp_ref, nw_ref,
                       o_ref, s_out_ref,
                       s_ref, u_ref, w_ref, qd_ref, kd_ref, qk_ref, gl_ref,
                       *, tb, heads, dk, dv):
    t = pl.program_id(1)
    key = heads * dk
    ch = 2 * key + heads * dv
    l = CHUNK

    @pl.when(t == 0)
    def _():
        s_ref[...] = s0_ref[0]

    a_log = gp_ref[0:1, :]
    dt_bias = gp_ref[1:2, :]
    hh = range(heads)
    l2 = 2 * l
    r2 = lax.broadcasted_iota(jnp.int32, (l2, l2), 0)
    c2 = lax.broadcasted_iota(jnp.int32, (l2, l2), 1)
    same = (r2 >= l) == (c2 >= l)
    incl = same & (r2 >= c2)
    strict = same & (r2 > c2)
    tril = incl.astype(F32)
    first = lax.broadcasted_iota(jnp.int32, (l2, 128), 0) < l

    pairs = range(tb // l2)
    rows = [slice(pi * l2, (pi + 1) * l2) for pi in pairs]
    beta_all, gc, gc_t, g_end = [], [], [], []
    for pi in pairs:
        ba = pba_ref[rows[pi], :]
        beta_all.append(_sigmoid(ba))
        g_all = -jnp.exp(a_log) * _softplus(ba + dt_bias)
        gc.append(_dot_hi(tril, g_all))
        gc_t.append(gc[pi].T)
        g_end.append(jnp.where(first, gc[pi][l - 1:l, :], gc[pi][l2 - 1:l2, :]))
        gl_ref[2 * pi:2 * pi + 1, :] = jnp.exp(gc[pi][l - 1:l, :])
        gl_ref[2 * pi + 1:2 * pi + 2, :] = jnp.exp(gc[pi][l2 - 1:l2, :])
    cc = [(pi, h) for pi in pairs for h in hh]
    b_col = [beta_all[pi][:, h:h + 1] for pi, h in cc]
    gi = [gc[pi][:, heads + h:heads + h + 1] for pi, h in cc]
    q = [qkv_ref[rows[pi], h * dk:(h + 1) * dk] for pi, h in cc]
    k = [qkv_ref[rows[pi], key + h * dk:key + (h + 1) * dk] for pi, h in cc]
    v = [qkv_ref[rows[pi], 2 * key + h * dv:2 * key + (h + 1) * dv] for pi, h in cc]
    nc = range(len(cc))
    dmat = [jnp.where(incl, jnp.exp(jnp.where(incl, gi[i] - gc_t[pi][heads + h:heads + h + 1, :], 0.0)), 0.0)
            for i, (pi, h) in enumerate(cc)]
    kb = [k[i] * b_col[i] for i in nc]
    kk = [_dot_nt(kb[i], k[i]) for i in nc]
    qk = [_dot_nt(q[i], k[i]) for i in nc]
    t_inv = _unit_lower_inverse([jnp.where(strict, kk[i] * dmat[i], 0.0) for i in nc], l)
    egi = [jnp.exp(gi[i]) for i in nc]
    sol = [_dot(t_inv[i], jnp.concatenate([v[i] * b_col[i], kb[i] * egi[i]], axis=-1)) for i in nc]
    for i, (pi, h) in enumerate(cc):
        hs = slice(h * dk, (h + 1) * dk)
        u_ref[rows[pi], h * dv:(h + 1) * dv] = sol[i][:, :dv]
        w_ref[rows[pi], hs] = sol[i][:, dv:].astype(BF16)
        qd_ref[rows[pi], hs] = (q[i] * egi[i]).astype(BF16)
        kd_ref[rows[pi], hs] = (k[i] * jnp.exp(g_end[pi][:, heads + h:heads + h + 1] - gi[i])).astype(BF16)
        qkm = jnp.where(incl, qk[i] * dmat[i], 0.0).astype(BF16)
        qk_ref[h, pi * l2:pi * l2 + l, :] = qkm[:l, :l]
        qk_ref[h, pi * l2 + l:(pi + 1) * l2, :] = qkm[l:, l:]

    for ci in range(tb // l):
        rows = slice(ci * l, (ci + 1) * l)
        s = [s_ref[h] for h in hh]
        wq = [_dot(jnp.concatenate([w_ref[rows, h * dk:(h + 1) * dk], qd_ref[rows, h * dk:(h + 1) * dk]], axis=0),
                   s[h]) for h in hh]
        v_new = [u_ref[rows, h * dv:(h + 1) * dv] - wq[h][:l] for h in hh]
        o2 = [_dot(qk_ref[h, rows, :], v_new[h]) for h in hh]
        ds = [_dot_tn(kd_ref[rows, h * dk:(h + 1) * dk], v_new[h]) for h in hh]
        for h in hh:
            s_ref[h] = s[h] * gl_ref[ci:ci + 1, heads + h:heads + h + 1] + ds[h]
            o = wq[h][l:] + o2[h]
            z = qkv_ref[rows, ch + h * dv:ch + (h + 1) * dv]
            o = o * lax.rsqrt(jnp.mean(o * o, axis=-1, keepdims=True) + RMS_EPS) * nw_ref[...]
            o_ref[rows, h * dv:(h + 1) * dv] = o * _silu(z)

    @pl.when(t == pl.num_programs(1) - 1)
    def _():
        s_out_ref[0] = s_ref[...]


def _gdn_prompt(qkvz, pba, s0, gp, nw, *, batch, seq, heads, dk, dv, tb):
    key, val = heads * dk, heads * dv
    ch = 2 * key + val
    tb = min(tb, seq)
    assert seq % tb == 0 and tb % (2 * CHUNK) == 0
    nt = seq // tb
    return pl.pallas_call(
        functools.partial(_gdn_prompt_kernel, tb=tb, heads=heads, dk=dk, dv=dv),
        grid=(batch, nt),
        in_specs=[pl.BlockSpec((tb, ch + val), lambda b, t: (b * nt + t, 0)),
                  pl.BlockSpec((tb, 128), lambda b, t: (b * nt + t, 0)),
                  pl.BlockSpec((1, heads, dk, dv), lambda b, t: (b, 0, 0, 0)),
                  pl.BlockSpec((8, 128), lambda b, t: (0, 0)),
                  pl.BlockSpec((1, dv), lambda b, t: (0, 0))],
        out_specs=[pl.BlockSpec((tb, val), lambda b, t: (b * nt + t, 0)),
                   pl.BlockSpec((1, heads, dk, dv), lambda b, t: (b, 0, 0, 0))],
        out_shape=[jax.ShapeDtypeStruct((batch * seq, val), F32),
                   jax.ShapeDtypeStruct((batch, heads, dk, dv), F32)],
        scratch_shapes=[pltpu.VMEM((heads, dk, dv), F32),
                        pltpu.VMEM((tb, val), F32), pltpu.VMEM((tb, key), BF16),
                        pltpu.VMEM((tb, key), BF16), pltpu.VMEM((tb, key), BF16),
                        pltpu.VMEM((heads, tb, CHUNK), BF16), pltpu.VMEM((max(8, tb // CHUNK), 128), F32)],
        compiler_params=_cparams(("parallel", "arbitrary")),
        name="gdn_prompt",
    )(qkvz, pba, s0, gp, nw)


GDN_PROJ_COLS = 256
GDN_CONV_ROWS = 64


def _gdn_in_proj_kernel(x_ref, g_ref, w_ref, wa_ref, conv0_ref, cw_ref, o_ref, aux_ref, conv_out_ref,
                        xn_ref, carry_ref, *, heads, dk, dv):
    t = pl.program_id(1)
    key = heads * dk
    ch = 2 * key + heads * dv
    tm = x_ref.shape[0]
    n = w_ref.shape[1]

    @pl.when(t == 0)
    def _():
        carry_ref[...] = conv0_ref[0]

    xn_ref[...] = _norm_rows(x_ref[...], g_ref[...]).astype(BF16)
    aux_ref[...] = jnp.dot(xn_ref[...], wa_ref[...], preferred_element_type=F32)
    for c0 in range(0, n, GDN_PROJ_COLS):
        cs = slice(c0, c0 + GDN_PROJ_COLS)
        y = jnp.dot(xn_ref[...], w_ref[:, cs], preferred_element_type=F32)
        if c0 >= ch:
            o_ref[:, cs] = y
            continue
        ext = jnp.concatenate([carry_ref[:, cs], y], axis=0)
        carry_ref[:, cs] = y[tm - 8:, :]
        for r0 in range(0, tm, GDN_CONV_ROWS):
            for j in range(0, GDN_PROJ_COLS, dk):
                cj = slice(c0 + j, c0 + j + dk)
                blk = ext[r0:r0 + GDN_CONV_ROWS + 8, j:j + dk]
                z = blk[8:] * cw_ref[CONV_W - 1:CONV_W, cj]
                for s in range(1, CONV_W):
                    z = z + pltpu.roll(blk, s, 0)[8:] * cw_ref[CONV_W - 1 - s:CONV_W - s, cj]
                z = _silu(z)
                if c0 + j < key:
                    z = z * lax.rsqrt(jnp.sum(z * z, axis=-1, keepdims=True) + 1e-6) * (dk ** -0.5)
                elif c0 + j < 2 * key:
                    z = z * lax.rsqrt(jnp.sum(z * z, axis=-1, keepdims=True) + 1e-6)
                o_ref[r0:r0 + GDN_CONV_ROWS, cj] = z

    @pl.when(t == pl.num_programs(1) - 1)
    def _():
        conv_out_ref[0] = carry_ref[...]


def _gdn_in_proj(x, g, w_main, w_ba, conv0, cw_t, *, batch, seq, heads, dk, dv, tm):
    m, d = x.shape
    key, val = heads * dk, heads * dv
    ch = 2 * key + val
    n = w_main.shape[1]
    assert dk == dv and n % GDN_PROJ_COLS == 0 and ch % GDN_PROJ_COLS == 0 and GDN_PROJ_COLS % dk == 0
    tm = min(tm, seq)
    assert seq % tm == 0 and tm >= 8
    nt = seq // tm
    row = lambda width: pl.BlockSpec((tm, width), lambda b, t: (b * nt + t, 0))
    full = lambda a: pl.BlockSpec(a.shape, lambda b, t: (0,) * a.ndim)
    st = pl.BlockSpec((1, 8, ch), lambda b, t: (b, 0, 0))
    return pl.pallas_call(
        functools.partial(_gdn_in_proj_kernel, heads=heads, dk=dk, dv=dv),
        grid=(batch, nt),
        in_specs=[row(d), full(g), full(w_main), full(w_ba), st, full(cw_t)],
        out_specs=[row(n), row(w_ba.shape[1]), st],
        out_shape=[jax.ShapeDtypeStruct((m, n), F32), jax.ShapeDtypeStruct((m, w_ba.shape[1]), F32),
                   jax.ShapeDtypeStruct((batch, 8, ch), F32)],
        scratch_shapes=[pltpu.VMEM((tm, d), BF16), pltpu.VMEM((8, ch), F32)],
        compiler_params=_cparams(("parallel", "arbitrary")),
        name="gdn_in_proj",
    )(x, g, w_main, w_ba, conv0, cw_t)


def _mlstm_prompt_kernel(pm_ref, pif_ref, bif_ref, c0_ref, n0_ref, m0_ref, nw_ref,
                         o_ref, c_out_ref, n_out_ref, m_out_ref,
                         c_ref, n_ref, m_ref, bc_ref, ni_ref, col_ref, kv_ref, kc_ref, sc_ref,
                         cin_ref, nin_ref, min_ref, *, tb, heads, dk, dv):
    t = pl.program_id(1)
    l = CHUNK
    qk_w = heads * dk
    v_off = 2 * qk_w
    o_off = v_off + heads * dv

    @pl.when(t == 0)
    def _():
        c_ref[...] = c0_ref[0]
        n_ref[...] = n0_ref[0]
        m_ref[...] = m0_ref[0]

    incl, _ = _tri_masks(l)
    tril = incl.astype(F32)
    hh = range(heads)
    nchunk = tb // l
    scale = dk ** -0.5

    crow = [slice(ci * l, (ci + 1) * l) for ci in range(nchunk)]
    gates, bcum, bcum_t, gates_t = [], [], [], []
    for ci in range(nchunk):
        g = pif_ref[crow[ci], :] + bif_ref[...]
        g = GATE_CAP * jnp.tanh(g / GATE_CAP)
        gates.append(g)
        bcum.append(_dot_hi(tril, _log_sigmoid(g)))
        bcum_t.append(bcum[ci].T)
        gates_t.append(g.T)
        bc_ref[crow[ci], :] = bcum[ci]
    cc = [(ci, h) for ci in range(nchunk) for h in hh]
    nc = range(len(cc))
    bi = [bcum[ci][:, heads + h:heads + h + 1] for ci, h in cc]
    b_last = [bcum[ci][l - 1:l, heads + h:heads + h + 1] for ci, h in cc]
    k = [pm_ref[crow[ci], qk_w + h * dk:qk_w + (h + 1) * dk] for ci, h in cc]
    v = [pm_ref[crow[ci], v_off + h * dv:v_off + (h + 1) * dv].astype(BF16) for ci, h in cc]
    qk = [_dot_nt(pm_ref[crow[ci], h * dk:(h + 1) * dk] * scale, k[i]) for i, (ci, h) in enumerate(cc)]
    dlog = [jnp.where(incl, bi[i] - bcum_t[ci][heads + h:heads + h + 1, :] + gates_t[ci][h:h + 1, :], NEG_BIG)
            for i, (ci, h) in enumerate(cc)]
    m_intra = [jnp.max(dlog[i], axis=-1, keepdims=True) for i in nc]
    p = [jnp.where(incl, jnp.exp(dlog[i] - m_intra[i]), 0.0) * qk[i] for i in nc]
    den_intra = [jnp.sum(p[i], axis=-1, keepdims=True) for i in nc]
    num_intra = [_dot(p[i], v[i]) for i in nc]
    a_log = [b_last[i] - bi[i] + gates[ci][:, h:h + 1] for i, (ci, h) in enumerate(cc)]
    m_chunk = [jnp.max(a_log[i], axis=0, keepdims=True) for i in nc]
    kw = [k[i] * jnp.exp(a_log[i] - m_chunk[i]) for i in nc]
    kv_chunk = [_dot_tn(kw[i], v[i]) for i in nc]
    for i, (ci, h) in enumerate(cc):
        ni_ref[crow[ci], h * dv:(h + 1) * dv] = num_intra[i]
        col_ref[crow[ci], h:h + 1] = m_intra[i]
        col_ref[crow[ci], heads + h:heads + h + 1] = den_intra[i]
        kv_ref[i] = kv_chunk[i]
        kc_ref[i:i + 1, :] = jnp.sum(kw[i], axis=0, keepdims=True)
        sc_ref[i:i + 1, 0:1] = m_chunk[i]
        sc_ref[i:i + 1, 1:2] = b_last[i]

    for ci in range(nchunk):
        for h in hh:
            i = ci * heads + h
            c_mat = c_ref[h]
            n_vec = n_ref[h:h + 1, :]
            m_prev = m_ref[h:h + 1, :]
            cin_ref[i] = c_mat
            nin_ref[i:i + 1, :] = n_vec
            min_ref[i:i + 1, :] = m_prev
            m_chunk = sc_ref[i:i + 1, 0:1]
            b_last = sc_ref[i:i + 1, 1:2]
            m_new = jnp.maximum(b_last + m_prev, m_chunk)
            f_s = jnp.exp(b_last + m_prev - m_new)
            i_s = jnp.exp(m_chunk - m_new)
            c_ref[h] = f_s[:, 0:1] * c_mat + i_s[:, 0:1] * kv_ref[i]
            n_ref[h:h + 1, :] = f_s * n_vec + i_s * kc_ref[i:i + 1, :]
            m_ref[h:h + 1, :] = m_new

    q = [pm_ref[crow[ci], h * dk:(h + 1) * dk] * scale for ci, h in cc]
    qc = [_dot(q[i], cin_ref[i]) for i in nc]
    qn = [jnp.sum(q[i] * nin_ref[i:i + 1, :], axis=-1, keepdims=True) for i in nc]
    m_prev = [min_ref[i:i + 1, 0:1] for i in nc]
    bi = [bc_ref[crow[ci], heads + h:heads + h + 1] for ci, h in cc]
    m_in = [col_ref[crow[ci], h:h + 1] for ci, h in cc]
    m_t = [jnp.maximum(bi[i] + m_prev[i], m_in[i]) for i in nc]
    s_inter = [jnp.exp(bi[i] + m_prev[i] - m_t[i]) for i in nc]
    s_intra = [jnp.exp(m_in[i] - m_t[i]) for i in nc]
    den = [s_inter[i] * qn[i] + s_intra[i] * col_ref[crow[ci], heads + h:heads + h + 1]
           for i, (ci, h) in enumerate(cc)]
    h_t = [(s_inter[i] * qc[i] + s_intra[i] * ni_ref[crow[ci], h * dv:(h + 1) * dv])
           / jnp.maximum(jnp.abs(den[i]), jnp.exp(-m_t[i])) for i, (ci, h) in enumerate(cc)]
    ms = [jnp.mean(h_t[i] * h_t[i], axis=-1, keepdims=True) for i in nc]
    for i, (ci, h) in enumerate(cc):
        h_n = h_t[i] * lax.rsqrt(ms[i] + RMS_EPS) * nw_ref[:, h * dv:(h + 1) * dv]
        o_pre = pm_ref[crow[ci], o_off + h * dv:o_off + (h + 1) * dv]
        o_ref[crow[ci], h * dv:(h + 1) * dv] = _sigmoid(o_pre) * h_n

    @pl.when(t == pl.num_programs(1) - 1)
    def _():
        c_out_ref[0] = c_ref[...]
        n_out_ref[0] = n_ref[...]
        m_out_ref[0] = m_ref[...]


def _mlstm_prompt(pm, pif, bif, c0, n0, m0, nw, *, batch, seq, heads, dk, dv, tb):
    width = pm.shape[1]
    val = heads * dv
    tb = min(tb, seq)
    assert seq % tb == 0 and tb % CHUNK == 0
    nt = seq // tb
    nck = (tb // CHUNK) * heads
    return pl.pallas_call(
        functools.partial(_mlstm_prompt_kernel, tb=tb, heads=heads, dk=dk, dv=dv),
        grid=(batch, nt),
        in_specs=[pl.BlockSpec((tb, width), lambda b, t: (b * nt + t, 0)),
                  pl.BlockSpec((tb, 128), lambda b, t: (b * nt + t, 0)),
                  pl.BlockSpec((1, 128), lambda b, t: (0, 0)),
                  pl.BlockSpec((1, heads, dk, dv), lambda b, t: (b, 0, 0, 0)),
                  pl.BlockSpec((1, 8, dk), lambda b, t: (b, 0, 0)),
                  pl.BlockSpec((1, 8, 128), lambda b, t: (b, 0, 0)),
                  pl.BlockSpec((1, val), lambda b, t: (0, 0))],
        out_specs=[pl.BlockSpec((tb, val), lambda b, t: (b * nt + t, 0)),
                   pl.BlockSpec((1, heads, dk, dv), lambda b, t: (b, 0, 0, 0)),
                   pl.BlockSpec((1, 8, dk), lambda b, t: (b, 0, 0)),
                   pl.BlockSpec((1, 8, 128), lambda b, t: (b, 0, 0))],
        out_shape=[jax.ShapeDtypeStruct((batch * seq, val), F32),
                   jax.ShapeDtypeStruct((batch, heads, dk, dv), F32),
                   jax.ShapeDtypeStruct((batch, 8, dk), F32),
                   jax.ShapeDtypeStruct((batch, 8, 128), F32)],
        scratch_shapes=[pltpu.VMEM((heads, dk, dv), F32), pltpu.VMEM((8, dk), F32),
                        pltpu.VMEM((8, 128), F32),
                        pltpu.VMEM((tb, 128), F32), pltpu.VMEM((tb, val), F32), pltpu.VMEM((tb, 128), F32),
                        pltpu.VMEM((nck, dk, dv), F32), pltpu.VMEM((max(8, nck), dk), F32),
                        pltpu.VMEM((max(8, nck), 128), F32),
                        pltpu.VMEM((nck, dk, dv), F32), pltpu.VMEM((max(8, nck), dk), F32),
                        pltpu.VMEM((max(8, nck), 128), F32)],
        compiler_params=_cparams(("parallel", "arbitrary")),
        name="mlstm_prompt",
    )(pm, pif, bif, c0, n0, m0, nw)


def _mlstm_prep(w_in, b_if, norm_w, w_out, *, heads, dk, dv):
    main = 2 * heads * dk + 2 * heads * dv
    return dict(w_main=w_in[:, :main].astype(BF16),
                w_if=_pad_cols(w_in[:, main:], 128).astype(BF16),
                bif=_pad_cols(b_if[None, :], 128), nw=norm_w[None, :], w_out=w_out.astype(BF16))


def _mlstm_prompt_layer(x, g, p, c0, n0, m0, *, batch, seq, heads, dk, dv):
    pm, pif = _in_proj(x, g, p["w_main"], p["w_if"])
    n0p = jnp.pad(n0, ((0, 0), (0, 8 - heads), (0, 0)))
    m0p = jnp.broadcast_to(jnp.pad(m0, ((0, 0), (0, 8 - heads)))[:, :, None], (batch, 8, 128))
    o, c, n, m = _mlstm_prompt(pm, pif, p["bif"], c0, n0p, m0p, p["nw"],
                               batch=batch, seq=seq, heads=heads, dk=dk, dv=dv, tb=TB_MIX)
    return o, c, n[:, :heads, :], m[:, :heads, 0]


def _rwkv_proj_body(h, prev, mu_ref, wrkv_ref, w1_ref, w2_ref, a1_ref, a2_ref, g1_ref, g2_ref,
                    vec_ref, r_ref, k_ref, v_ref, lw_ref, kk_ref, a_ref, g_ref):
    xx = prev - h

    def mix(j):
        return (h + xx * mu_ref[j:j + 1, :]).astype(BF16)

    w0, a0, k_k, k_a = (vec_ref[j:j + 1, :] for j in range(4))
    r_ref[...] = jnp.dot(mix(0), wrkv_ref[0], preferred_element_type=F32)
    lora_w = _dot(jnp.tanh(_dot(mix(1), w1_ref[...])), w2_ref[...])
    w_log = -_softplus(-(w0 + lora_w)) - 0.5
    lw_ref[...] = -jnp.exp(w_log)
    k = jnp.dot(mix(2), wrkv_ref[1], preferred_element_type=F32)
    v_ref[...] = jnp.dot(mix(3), wrkv_ref[2], preferred_element_type=F32)
    a = _sigmoid(a0 + _dot(_dot(mix(4), a1_ref[...]), a2_ref[...]))
    g_ref[...] = _dot(_sigmoid(_dot(mix(5), g1_ref[...])), g2_ref[...])
    kk_ref[...] = k * k_k
    k_ref[...] = k * (1.0 + (a - 1.0) * k_a)
    a_ref[...] = a


def _rwkv_proj_sample_kernel(x_ref, gn_ref, prev_ref, *refs):
    h = _norm_rows(x_ref[...], gn_ref[...])
    hn_ref = refs[-1]
    hn_ref[...] = h
    _rwkv_proj_body(h, prev_ref[...], *refs[:-1])


def _rwkv_proj_prompt_kernel(x_ref, gn_ref, shift0_ref, *refs):
    carry_ref = refs[-1]
    shift_out_ref = refs[-2]
    t = pl.program_id(1)

    @pl.when(t == 0)
    def _():
        carry_ref[...] = shift0_ref[0]

    h = _norm_rows(x_ref[...], gn_ref[...])
    rows = h.shape[0]
    first = lax.broadcasted_iota(jnp.int32, h.shape, 0) == 0
    prev = jnp.where(first, carry_ref[0:1, :], pltpu.roll(h, 1, 0))
    carry_ref[0:1, :] = h[rows - 1:rows, :]
    _rwkv_proj_body(h, prev, *refs[:-2])

    @pl.when(t == pl.num_programs(1) - 1)
    def _():
        shift_out_ref[0] = carry_ref[...]


def _rwkv_proj(x, g_norm, shift0, p, *, batch, seq, tm):
    m, d = x.shape
    consts = [p["mu"], p["w_rkv"], p["w1"], p["w2"], p["a1"], p["a2"], p["g1"], p["g2"], p["vec"]]
    if seq == 1:
        row = pl.BlockSpec((m, d), lambda i: (0, 0))
        out = pl.pallas_call(
            _rwkv_proj_sample_kernel,
            grid=(1,),
            in_specs=[row, pl.BlockSpec((1, d), lambda i: (0, 0)), row] + [_const_block(a) for a in consts],
            out_specs=[row] * 8,
            out_shape=[jax.ShapeDtypeStruct((m, d), F32)] * 8,
            compiler_params=_cparams(("arbitrary",)),
            name="rwkv_proj_sample",
        )(x, g_norm, shift0, *consts)
        return out[:7], out[7]
    tm = min(tm, seq)
    assert seq % tm == 0
    nt = seq // tm
    row = pl.BlockSpec((tm, d), lambda b, t: (b * nt + t, 0))
    st = pl.BlockSpec((1, 8, d), lambda b, t: (b, 0, 0))

    def full(a):
        nd = a.ndim
        return pl.BlockSpec(a.shape, lambda b, t: (0,) * nd)

    shift0_p = jnp.pad(shift0[:, None, :], ((0, 0), (0, 7), (0, 0)))
    out = pl.pallas_call(
        _rwkv_proj_prompt_kernel,
        grid=(batch, nt),
        in_specs=[row, full(g_norm), st] + [full(a) for a in consts],
        out_specs=[row] * 7 + [st],
        out_shape=[jax.ShapeDtypeStruct((m, d), F32)] * 7 + [jax.ShapeDtypeStruct((batch, 8, d), F32)],
        scratch_shapes=[pltpu.VMEM((8, d), F32)],
        compiler_params=_cparams(("parallel", "arbitrary")),
        name="rwkv_proj_prompt",
    )(x, g_norm, shift0_p, *consts)
    return out[:7], out[7][:, 0, :]


def _rwkv_prompt_kernel(r_ref, k_ref, v_ref, lw_ref, kk_ref, a_ref, g_ref, s0_ref, hp_ref,
                        o_ref, s_out_ref, s_ref, rr_ref, yy_ref, mx_ref, n0_ref, gw_ref, gb_ref, el_ref,
                        *, tb, heads, hd):
    t = pl.program_id(1)
    l = CHUNK

    @pl.when(t == 0)
    def _():
        s_ref[...] = s0_ref[0]

    hh = range(heads)
    hs = [slice(h * hd, (h + 1) * hd) for h in hh]
    l2 = 2 * l
    r2 = lax.broadcasted_iota(jnp.int32, (l2, l2), 0)
    c2 = lax.broadcasted_iota(jnp.int32, (l2, l2), 1)
    same = (r2 >= l) == (c2 >= l)
    incl = same & (r2 >= c2)
    strict = same & (r2 > c2)
    tril = incl.astype(F32)
    first = lax.broadcasted_iota(jnp.int32, (l2, heads * hd), 0) < l
    first2 = (lax.broadcasted_iota(jnp.int32, (2 * l2, hd), 0) & l) == 0
    zeros = jnp.zeros((l2, hd), F32)

    pairs = range(tb // l2)
    prow = [slice(pi * l2, (pi + 1) * l2) for pi in pairs]
    lwc, e_in, e_prev, e_neg, e_end = [], [], [], [], []
    for pi in pairs:
        lw = lw_ref[prow[pi], :]
        c = _dot_hi(tril, lw)
        lwc.append(c)
        e_in.append(jnp.exp(c))
        e_prev.append(jnp.exp(c - lw))
        e_neg.append(jnp.exp(-c))
        e_end.append(jnp.exp(jnp.where(first, c[l - 1:l, :], c[l2 - 1:l2, :]) - c))
    cc = [(pi, h) for pi in pairs for h in hh]
    nc = range(len(cc))
    r = [r_ref[prow[pi], hs[h]] for pi, h in cc]
    v = [v_ref[prow[pi], hs[h]] for pi, h in cc]
    k = [k_ref[prow[pi], hs[h]] for pi, h in cc]
    kk = [kk_ref[prow[pi], hs[h]] for pi, h in cc]
    kk = [kk[i] * lax.rsqrt(jnp.maximum(jnp.sum(kk[i] * kk[i], axis=-1, keepdims=True), 1e-24)) for i in nc]
    bv = [kk[i] * a_ref[prow[pi], hs[h]] for i, (pi, h) in enumerate(cc)]
    a_t = [-kk[i] * e_prev[pi][:, hs[h]] for i, (pi, h) in enumerate(cc)]
    r_t = [r[i] * e_in[pi][:, hs[h]] for i, (pi, h) in enumerate(cc)]
    gm = [_dot_nt(jnp.concatenate([a_t[i], r_t[i]], axis=0),
                  jnp.concatenate([bv[i] * e_neg[pi][:, hs[h]], k[i] * e_neg[pi][:, hs[h]]], axis=0))
          for i, (pi, h) in enumerate(cc)]
    ak_m = [jnp.where(strict, gm[i][:l2, l2:], 0.0).astype(BF16) for i in nc]
    rbk_m = [jnp.concatenate([jnp.where(incl, gm[i][l2:, :l2], 0.0),
                              jnp.where(incl, gm[i][l2:, l2:], 0.0)], axis=1).astype(BF16) for i in nc]
    t_inv = _unit_lower_inverse([jnp.where(strict, -gm[i][:l2, :l2], 0.0) for i in nc], l)
    vb = [v[i].astype(BF16) for i in nc]
    akv = [_dot(ak_m[i], vb[i]) for i in nc]
    x1 = [_dot(t_inv[i], jnp.concatenate([a_t[i], akv[i]], axis=1)).astype(BF16) for i in nc]
    low = [jnp.concatenate([x1[i], jnp.concatenate([zeros.astype(BF16), vb[i]], axis=1)], axis=0) for i in nc]
    x2 = [_dot(rbk_m[i], low[i]) for i in nc]
    bk = [jnp.concatenate([bv[i] * e_end[pi][:, hs[h]], k[i] * e_end[pi][:, hs[h]]], axis=0)
          for i, (pi, h) in enumerate(cc)]
    bk2 = [jnp.concatenate([jnp.where(first2, bk[i], 0.0), jnp.where(first2, 0.0, bk[i])], axis=1) for i in nc]
    mn = [_dot_tn(low[i], bk2[i]) for i in nc]
    for i, (pi, h) in enumerate(cc):
        for c in range(2):
            crow = slice(pi * l2 + c * l, pi * l2 + (c + 1) * l)
            mx_ref[h, crow, :] = mn[i][:hd, c * hd:(c + 1) * hd].astype(BF16)
            n0_ref[h, crow, :] = mn[i][hd:, c * hd:(c + 1) * hd]
        rr_ref[h, prow[pi], :] = (r_t[i] + x2[i][:, :hd]).astype(BF16)
        yy_ref[h, prow[pi], :] = x2[i][:, hd:]
        g = g_ref[prow[pi], hs[h]]
        bonus = jnp.sum(r[i] * k[i] * hp_ref[0:1, hs[h]], axis=-1, keepdims=True) * v[i]
        gw_ref[h, prow[pi], :] = hp_ref[1:2, hs[h]] * g
        gb_ref[h, prow[pi], :] = (hp_ref[2:3, hs[h]] + bonus) * g
        el_ref[h, 2 * pi:2 * pi + 1, :] = jnp.exp(lwc[pi][l - 1:l, hs[h]])
        el_ref[h, 2 * pi + 1:2 * pi + 2, :] = jnp.exp(lwc[pi][l2 - 1:l2, hs[h]])

    for ci in range(tb // l):
        rows = slice(ci * l, (ci + 1) * l)
        s = [s_ref[h] for h in hh]
        y = [_dot_nt(rr_ref[h, rows, :], s[h]) for h in hh]
        sm = [_dot(s[h], mx_ref[h, rows, :]) for h in hh]
        for h in hh:
            s_ref[h] = s[h] * el_ref[h, ci:ci + 1, :] + sm[h] + n0_ref[h, rows, :]
        y = [y[h] + yy_ref[h, rows, :] for h in hh]
        mean = [jnp.mean(y[h], axis=-1, keepdims=True) for h in hh]
        yc = [y[h] - mean[h] for h in hh]
        var = [jnp.mean(yc[h] * yc[h], axis=-1, keepdims=True) for h in hh]
        for h in hh:
            o_ref[rows, hs[h]] = yc[h] * lax.rsqrt(var[h] + RW_GN_EPS) * gw_ref[h, rows, :] + gb_ref[h, rows, :]

    @pl.when(t == pl.num_programs(1) - 1)
    def _():
        s_out_ref[0] = s_ref[...]


def _rwkv_prompt(r, k, v, lw, kk, a, g, s0, hp, *, batch, seq, heads, hd, tb):
    d = heads * hd
    tb = min(tb, seq)
    assert seq % tb == 0 and tb % CHUNK == 0
    nt = seq // tb
    row = pl.BlockSpec((tb, d), lambda b, t: (b * nt + t, 0))
    st = pl.BlockSpec((1, heads, hd, hd), lambda b, t: (b, 0, 0, 0))
    return pl.pallas_call(
        functools.partial(_rwkv_prompt_kernel, tb=tb, heads=heads, hd=hd),
        grid=(batch, nt),
        in_specs=[row] * 7 + [st, pl.BlockSpec((8, d), lambda b, t: (0, 0))],
        out_specs=[row, st],
        out_shape=[jax.ShapeDtypeStruct((batch * seq, d), F32),
                   jax.ShapeDtypeStruct((batch, heads, hd, hd), F32)],
        scratch_shapes=[pltpu.VMEM((heads, hd, hd), F32),
                        pltpu.VMEM((heads, tb, hd), BF16), pltpu.VMEM((heads, tb, hd), F32),
                        pltpu.VMEM((heads, tb, hd), BF16), pltpu.VMEM((heads, tb, hd), F32),
                        pltpu.VMEM((heads, tb, hd), F32), pltpu.VMEM((heads, tb, hd), F32),
                        pltpu.VMEM((heads, max(8, tb // CHUNK), hd), F32)],
        compiler_params=_cparams(("parallel", "arbitrary")),
        name="rwkv_prompt",
    )(r, k, v, lw, kk, a, g, s0, hp)


def _pad_rows(a, n):
    return jnp.pad(a, ((0, n - a.shape[0]), (0, 0)))


def _rwkv_prep(mu, w_rkv, w_o, w0, w1, w2, a0, a1, a2, g1, g2, k_k, k_a, r_k, lnx_w, lnx_b):
    d = w0.shape[0]
    lw = -(-w1.shape[1] // 128) * 128
    la = -(-a1.shape[1] // 128) * 128
    lg = -(-g1.shape[1] // 128) * 128
    return dict(mu=_pad_rows(mu, 8), w_rkv=w_rkv.astype(BF16), w_o=w_o.astype(BF16),
                w1=_pad_cols(w1, lw).astype(BF16), w2=_pad_rows(w2, lw).astype(BF16),
                a1=_pad_cols(a1, la).astype(BF16), a2=_pad_rows(a2, la).astype(BF16),
                g1=_pad_cols(g1, lg).astype(BF16), g2=_pad_rows(g2, lg).astype(BF16),
                vec=_pad_rows(jnp.stack([w0, a0, k_k, k_a]), 8),
                hp=_pad_rows(jnp.stack([r_k.reshape(d), lnx_w, lnx_b]), 8))


def _rwkv_prompt_layer(x, g_norm, p, shift0, s0, *, batch, seq, heads, hd):
    (r, k, v, lw, kk, a, g), shift = _rwkv_proj(x, g_norm, shift0, p, batch=batch, seq=seq, tm=TM_RWKV_PROJ)
    o, s = _rwkv_prompt(r, k, v, lw, kk, a, g, s0, p["hp"], batch=batch, seq=seq, heads=heads, hd=hd, tb=TB_MIX)
    return o, shift, s


def _gdn_sample_pre_kernel(pm_ref, pba_ref, conv_ref, cw_ref, gp_ref, qkv_ref, conv_out_ref, sc_ref,
                           *, heads, dk, dv):
    key = heads * dk
    ch = 2 * key + heads * dv
    u = pm_ref[:, 0:ch]
    y = u * cw_ref[CONV_W - 1:CONV_W, :]
    for j in range(CONV_W - 1):
        y = y + conv_ref[j] * cw_ref[j:j + 1, :]
        conv_out_ref[j] = conv_ref[j + 1] if j + 1 < CONV_W - 1 else u
    y = _silu(y)
    for c in range(ch // 128):
        cs = slice(c * 128, (c + 1) * 128)
        yc = y[:, cs]
        if c * 128 < key:
            yc = yc * lax.rsqrt(jnp.sum(yc * yc, axis=-1, keepdims=True) + 1e-6) * (dk ** -0.5)
        elif c * 128 < 2 * key:
            yc = yc * lax.rsqrt(jnp.sum(yc * yc, axis=-1, keepdims=True) + 1e-6)
        qkv_ref[:, cs] = yc
    ba = pba_ref[...]
    lane = lax.broadcasted_iota(jnp.int32, ba.shape, 1)
    g = -jnp.exp(gp_ref[0:1, :]) * _softplus(ba + gp_ref[1:2, :])
    sc_ref[...] = jnp.where(lane < heads, _sigmoid(ba), jnp.exp(g))


def _gdn_sample_pre(pm, pba, conv_t, cw_t, gp, *, heads, dk, dv):
    n = pm.shape[0]
    ch = 2 * heads * dk + heads * dv
    return pl.pallas_call(
        functools.partial(_gdn_sample_pre_kernel, heads=heads, dk=dk, dv=dv),
        out_shape=[jax.ShapeDtypeStruct((n, ch), F32),
                   jax.ShapeDtypeStruct((CONV_W - 1, n, ch), F32),
                   jax.ShapeDtypeStruct((n, 128), F32)],
        compiler_params=pltpu.CompilerParams(vmem_limit_bytes=V7X_VMEM_LIMIT),
        name="gdn_sample_pre",
    )(pm, pba, conv_t, cw_t, gp)


SEQ_PER_STEP = 4


def _seq_block(a, nb):
    nd = a.ndim
    return pl.BlockSpec((nb,) + a.shape[1:], lambda b: (b,) + (0,) * (nd - 1))


def _const_block(a):
    nd = a.ndim
    return pl.BlockSpec(a.shape, lambda b: (0,) * nd)


def _gdn_sample_step_kernel(s0_ref, cols_ref, v_ref, z_ref, sc_ref, nw_ref, *refs, heads, nb, n_prev):
    prev_refs = refs[:n_prev]
    s_out_all, o_ref = refs[n_prev:]
    for p, prev_ref in enumerate(prev_refs):
        s_out_all[p] = prev_ref[...]
    s_out_ref = s_out_all.at[n_prev] if n_prev else s_out_all
    hh = range(heads)
    for i in range(nb):
        kc = [cols_ref[i, :, h:h + 1] for h in hh]
        qc = [cols_ref[i, :, heads + h:heads + h + 1] for h in hh]
        s0 = [s0_ref[i, h] for h in hh]
        eg = [sc_ref[i, h:h + 1, 1:2] for h in hh]
        ks = [jnp.sum(kc[h] * s0[h], axis=0, keepdims=True) for h in hh]
        s1 = [eg[h] * s0[h] + kc[h] * (sc_ref[i, h:h + 1, 0:1] * (v_ref[i, h:h + 1, :] - eg[h] * ks[h]))
              for h in hh]
        o = [jnp.sum(qc[h] * s1[h], axis=0, keepdims=True) for h in hh]
        ms = [jnp.mean(o[h] * o[h], axis=-1, keepdims=True) for h in hh]
        for h in hh:
            s_out_ref[i, h] = s1[h]
            o_ref[i, h:h + 1, :] = o[h] * lax.rsqrt(ms[h] + RMS_EPS) * nw_ref[...] * _silu(z_ref[i, h:h + 1, :])


def _gdn_sample_step(s_all, layer, prev_new, cols, v, z, sc, nw):
    n_layers, n, heads, dk, dv = s_all.shape
    nb = SEQ_PER_STEP
    assert n % nb == 0
    last = layer == n_layers - 1
    n_prev = len(prev_new) if last else 0
    one = pl.BlockSpec((nb, heads, dk, dv), lambda b: (b, 0, 0, 0))
    if n_prev:
        out_state = pl.BlockSpec((n_layers, nb, heads, dk, dv), lambda b: (0, b, 0, 0, 0))
        out_shape = jax.ShapeDtypeStruct(s_all.shape, F32)
    else:
        out_state, out_shape = one, jax.ShapeDtypeStruct(s_all.shape[1:], F32)
    return pl.pallas_call(
        functools.partial(_gdn_sample_step_kernel, heads=heads, nb=nb, n_prev=n_prev),
        grid=(n // nb,),
        in_specs=[pl.BlockSpec((None, nb, heads, dk, dv), lambda b: (layer, b, 0, 0, 0)),
                  _seq_block(cols, nb), _seq_block(v, nb), _seq_block(z, nb), _seq_block(sc, nb),
                  pl.BlockSpec((1, dv), lambda b: (0, 0))] + [one] * n_prev,
        out_specs=[out_state, _seq_block(v, nb)],
        out_shape=[out_shape, jax.ShapeDtypeStruct(v.shape, F32)],
        compiler_params=_cparams(("parallel",)),
        name="gdn_sample_step",
    )(s_all, cols, v, z, sc, nw, *(prev_new if n_prev else []))


def _gdn_sample_layer(x, g, p, conv0, s_all, layer, prev_new, *, heads, dk, dv):
    n = x.shape[0]
    key, val = heads * dk, heads * dv
    ch = 2 * key + val
    pm, pba = _in_proj(x, g, p["w_main"], p["w_ba"])
    qkv, conv_t, sc = _gdn_sample_pre(pm, pba, jnp.transpose(conv0, (1, 0, 2)), p["cw_t"], p["gp"],
                                      heads=heads, dk=dk, dv=dv)
    q_c = jnp.transpose(qkv[:, :key].reshape(n, heads, dk), (0, 2, 1))
    k_c = jnp.transpose(qkv[:, key:2 * key].reshape(n, heads, dk), (0, 2, 1))
    cols = jnp.concatenate([k_c, q_c], axis=-1)
    sc3 = jnp.stack([sc[:, :heads], sc[:, heads:2 * heads]], axis=-1)
    s_new, o = _gdn_sample_step(s_all, layer, prev_new, cols, qkv[:, 2 * key:].reshape(n, heads, dv),
                                pm[:, ch:].reshape(n, heads, dv), sc3, p["nw"])
    return o.reshape(n, val), jnp.transpose(conv_t, (1, 0, 2)), s_new


def _mlstm_sample_step_kernel(c0_ref, n0_ref, cols_ref, q_ref, k_ref, v_ref, op_ref, sc_ref, bif_ref, nw_ref,
                              c_out_ref, n_out_ref, m_out_ref, o_ref, *, heads, dk, nb):
    scale = dk ** -0.5
    for i in range(nb):
        gi = sc_ref[i, :, 0:1] + bif_ref[:, 0:1]
        gf = sc_ref[i, :, 1:2] + bif_ref[:, 1:2]
        m0 = sc_ref[i, :, 2:3]
        gi = GATE_CAP * jnp.tanh(gi / GATE_CAP)
        logf = _log_sigmoid(GATE_CAP * jnp.tanh(gf / GATE_CAP))
        m_new = jnp.maximum(logf + m0, gi)
        f_s = jnp.exp(logf + m0 - m_new)
        i_s = jnp.exp(gi - m_new)
        m_out_ref[i] = m_new
        n1 = f_s * n0_ref[i] + i_s * k_ref[i]
        n_out_ref[i] = n1
        den = jnp.sum(q_ref[i] * scale * n1, axis=-1, keepdims=True)
        floor = jnp.exp(-m_new)
        hh = range(heads)
        kc = [cols_ref[i, :, h:h + 1] for h in hh]
        qc = [cols_ref[i, :, heads + h:heads + h + 1] * scale for h in hh]
        c1 = [f_s[h:h + 1, :] * c0_ref[i, h] + i_s[h:h + 1, :] * (kc[h] * v_ref[i, h:h + 1, :]) for h in hh]
        num = [jnp.sum(qc[h] * c1[h], axis=0, keepdims=True) for h in hh]
        h_t = [num[h] / jnp.maximum(jnp.abs(den[h:h + 1, :]), floor[h:h + 1, :]) for h in hh]
        ms = [jnp.mean(h_t[h] * h_t[h], axis=-1, keepdims=True) for h in hh]
        for h in hh:
            c_out_ref[i, h] = c1[h]
            h_n = h_t[h] * lax.rsqrt(ms[h] + RMS_EPS) * nw_ref[h:h + 1, :]
            o_ref[i, h:h + 1, :] = _sigmoid(op_ref[i, h:h + 1, :]) * h_n


def _mlstm_sample_step(c0, n0, cols, q, k, v, o_pre, sc, bif2, nw2):
    n, heads, dk, dv = c0.shape
    nb = SEQ_PER_STEP
    assert n % nb == 0
    full = _const_block

    def blk(a):
        return _seq_block(a, nb)

    m_shape = (n, heads, 1)
    return pl.pallas_call(
        functools.partial(_mlstm_sample_step_kernel, heads=heads, dk=dk, nb=nb),
        grid=(n // nb,),
        in_specs=[blk(c0), blk(n0), blk(cols), blk(q), blk(k), blk(v), blk(o_pre), blk(sc), full(bif2), full(nw2)],
        out_specs=[blk(c0), blk(n0), pl.BlockSpec((nb, heads, 1), lambda b: (b, 0, 0)), blk(v)],
        out_shape=[jax.ShapeDtypeStruct(c0.shape, F32), jax.ShapeDtypeStruct(n0.shape, F32),
                   jax.ShapeDtypeStruct(m_shape, F32), jax.ShapeDtypeStruct(v.shape, F32)],
        compiler_params=_cparams(("parallel",)),
        name="mlstm_sample_step",
    )(c0, n0, cols, q, k, v, o_pre, sc, bif2, nw2)


def _mlstm_sample_layer(x, g, p, c0, n0, m0, *, heads, dk, dv):
    n = x.shape[0]
    qk_w, val = heads * dk, heads * dv
    pm, pif = _in_proj(x, g, p["w_main"], p["w_if"])
    q = pm[:, :qk_w].reshape(n, heads, dk)
    k = pm[:, qk_w:2 * qk_w].reshape(n, heads, dk)
    v = pm[:, 2 * qk_w:2 * qk_w + val].reshape(n, heads, dv)
    o_pre = pm[:, 2 * qk_w + val:].reshape(n, heads, dv)
    cols = jnp.concatenate([jnp.transpose(k, (0, 2, 1)), jnp.transpose(q, (0, 2, 1))], axis=-1)
    sc = jnp.stack([pif[:, :heads], pif[:, heads:2 * heads], m0], axis=-1)
    bif2 = jnp.stack([p["bif"][0, :heads], p["bif"][0, heads:2 * heads]], axis=-1)
    c1, n1, m1, o = _mlstm_sample_step(c0, n0, cols, q, k, v, o_pre, sc, bif2, p["nw"].reshape(heads, dv))
    return o.reshape(n, val), c1, n1, m1[:, :, 0]


RW_ROW_GROUP = 8


def _rwkv_sample_step_kernel(s_ref, r_ref, k_ref, lw_ref, kk_ref, a_ref, v_ref, g_ref, hp_ref,
                             s_out_ref, o_ref, y_ref, *, hd):
    kk = kk_ref[0]
    kk = kk * lax.rsqrt(jnp.maximum(jnp.sum(kk * kk, axis=0, keepdims=True), 1e-24))
    av = -kk
    bv = kk * a_ref[0]
    w = jnp.exp(lw_ref[0])
    r = r_ref[0]
    k = k_ref[0]
    v = v_ref[0]
    for v0 in range(0, hd, RW_ROW_GROUP):
        vv = range(v0, v0 + RW_ROW_GROUP)
        s0 = [s_ref[0, i] for i in vv]
        sa = [jnp.sum(s * av, axis=0, keepdims=True) for s in s0]
        s1 = [s * w + sa_i * bv + v[i:i + 1, :] * k for s, sa_i, i in zip(s0, sa, vv)]
        y = [jnp.sum(s * r, axis=0, keepdims=True) for s in s1]
        for i, s, y_i in zip(vv, s1, y):
            s_out_ref[0, i] = s
            y_ref[i:i + 1, :] = y_i
    y = y_ref[...]
    yc = y - jnp.mean(y, axis=0, keepdims=True)
    var = jnp.mean(yc * yc, axis=0, keepdims=True)
    hp = hp_ref[0]
    yn = yc * lax.rsqrt(var + RW_GN_EPS) * hp[:, 1:2] + hp[:, 2:3]
    bonus = jnp.sum(r * k * hp[:, 0:1], axis=0, keepdims=True)
    o_ref[0] = (yn + bonus * v) * g_ref[0]


def _rwkv_sample_step(s_t, r, k, lw, kk, a, v, g, hp3):
    heads, hd, _, n = s_t.shape

    def blk(z):
        nd = z.ndim
        return pl.BlockSpec((1,) + z.shape[1:], lambda h: (h,) + (0,) * (nd - 1))

    return pl.pallas_call(
        functools.partial(_rwkv_sample_step_kernel, hd=hd),
        grid=(heads,),
        in_specs=[blk(s_t)] + [blk(z) for z in (r, k, lw, kk, a, v, g, hp3)],
        out_specs=[blk(s_t), blk(v)],
        out_shape=[jax.ShapeDtypeStruct(s_t.shape, F32), jax.ShapeDtypeStruct(v.shape, F32)],
        scratch_shapes=[pltpu.VMEM((hd, n), F32)],
        compiler_params=_cparams(("parallel",)),
        name="rwkv_sample_step",
    )(s_t, r, k, lw, kk, a, v, g, hp3)


def _rwkv_sample_layer(x, g_norm, p, shift0, s0, *, heads, hd):
    n, d = x.shape
    (r, k, v, lw, kk, a, g), hn = _rwkv_proj(x, g_norm, shift0, p, batch=n, seq=1, tm=n)

    def lanes(z):
        return z.T.reshape(heads, hd, n)

    hp = p["hp"]
    hp3 = jnp.stack([hp[j].reshape(heads, hd) for j in range(3)], axis=-1)
    s1_t, o_t = _rwkv_sample_step(jnp.transpose(s0, (1, 2, 3, 0)), lanes(r), lanes(k), lanes(lw), lanes(kk),
                                  lanes(a), lanes(v), lanes(g), hp3)
    return o_t.reshape(d, n).T, hn, jnp.transpose(s1_t, (3, 0, 1, 2))


def _pad_cols(a, n):
    return jnp.pad(a, ((0, 0), (0, n - a.shape[1])))


def _gdn_prep(w_in, conv_w, a_log, dt_bias, norm_w, w_out, *, heads, dk, dv):
    key, val = heads * dk, heads * dv
    ch = 2 * key + val
    main = ch + val
    gp = jnp.zeros((8, 128), F32)
    gp = gp.at[0, heads:2 * heads].set(a_log).at[1, heads:2 * heads].set(dt_bias)
    return dict(w_main=w_in[:, :main].astype(BF16),
                w_ba=_pad_cols(w_in[:, main:], 128).astype(BF16),
                cw_t=jnp.pad(conv_w.T, ((0, 8 - CONV_W), (0, 0))),
                gp=gp, nw=norm_w[None, :], w_out=w_out.astype(BF16))


def _gdn_prompt_layer(x, g, p, conv0, s0, *, batch, seq, heads, dk, dv):
    conv0 = jnp.pad(conv0, ((0, 0), (8 - (CONV_W - 1), 0), (0, 0)))
    qkvz, pba, conv = _gdn_in_proj(x, g, p["w_main"], p["w_ba"], conv0, p["cw_t"], batch=batch, seq=seq,
                                   heads=heads, dk=dk, dv=dv, tm=TM_GDN_PROJ)
    o, s = _gdn_prompt(qkvz, pba, s0, p["gp"], p["nw"], batch=batch, seq=seq, heads=heads, dk=dk, dv=dv, tb=TB_MIX)
    return o, conv[:, 8 - (CONV_W - 1):, :], s


def _trunk(x, states, w, *, batch, seq):
    conv_in, gs_in, c_in, n_in, m_in, shift_in, rs_in = states
    depth = w["norm_mix"].shape[0]
    gh, gdk, gdv = gs_in.shape[2:]
    mh, mdk, mdv = c_in.shape[2:]
    rh, rhd = rs_in.shape[2:4]
    prompt = seq > 1
    outs = [[] for _ in range(7)]
    gs_new = []
    for i in range(depth):
        j = i // 3
        g = w["norm_mix"][i][None, :]
        if i % 3 == 0:
            p = w["gdn"][j]
            if prompt:
                mix, cb, s = _gdn_prompt_layer(x, g, p, conv_in[j], gs_in[j], batch=batch, seq=seq,
                                               heads=gh, dk=gdk, dv=gdv)
                outs[1].append(s)
            else:
                mix, cb, s = _gdn_sample_layer(x, g, p, conv_in[j], gs_in, j, gs_new, heads=gh, dk=gdk, dv=gdv)
                gs_new.append(s)
            outs[0].append(cb)
            w_out = p["w_out"]
        elif i % 3 == 1:
            p = w["ml"][j]
            if prompt:
                mix, c, n, m = _mlstm_prompt_layer(x, g, p, c_in[j], n_in[j], m_in[j], batch=batch, seq=seq,
                                                   heads=mh, dk=mdk, dv=mdv)
            else:
                mix, c, n, m = _mlstm_sample_layer(x, g, p, c_in[j], n_in[j], m_in[j], heads=mh, dk=mdk, dv=mdv)
            w_out = p["w_out"]
            outs[2].append(c)
            outs[3].append(n)
            outs[4].append(m)
        else:
            p = w["rw"][j]
            if prompt:
                mix, sh, s = _rwkv_prompt_layer(x, g, p, shift_in[j], rs_in[j], batch=batch, seq=seq,
                                                heads=rh, hd=rhd)
            else:
                mix, sh, s = _rwkv_sample_layer(x, g, p, shift_in[j], rs_in[j], heads=rh, hd=rhd)
            w_out = p["w_o"]
            outs[5].append(sh)
            outs[6].append(s)
        g_out = w["norm_final"][None, :] if i == depth - 1 else None
        x = _ffn(x, mix, w_out, w["norm_ffn"][i][None, :], w["ffn_w1"], w["ffn_w2"], i, g_out,
                 tm=TM_FFN, tf=TF_FFN)
    y = x
    new = [jnp.stack(z, axis=0) if z else None for z in outs]
    if not prompt:
        new[1] = gs_new[-1] if len(gs_new) > 1 else gs_new[0][None]
    return y, tuple(new)


def kernel(x_prompt, x_sample, state_gdn_conv, state_gdn_S, state_mlstm_C, state_mlstm_n, state_mlstm_m, state_rwkv_shift, state_rwkv_S, norm_mix, norm_ffn, norm_final, gdn_w_in, gdn_conv_w, gdn_a_log, gdn_dt_bias, gdn_norm_w, gdn_w_out, ml_w_in, ml_b_if, ml_norm_w, ml_w_out, rw_mu, rw_w_rkv, rw_w_o, rw_w0, rw_w1, rw_w2, rw_a0, rw_a1, rw_a2, rw_g1, rw_g2, rw_k_k, rw_k_a, rw_r_k, rw_lnx_w, rw_lnx_b, ffn_w1, ffn_w2):
    gh, gdk, gdv = state_gdn_S.shape[2:]
    mh, mdk, mdv = state_mlstm_C.shape[2:]
    w = dict(
        norm_mix=norm_mix, norm_ffn=norm_ffn, norm_final=norm_final,
        ffn_w1=ffn_w1.astype(BF16), ffn_w2=ffn_w2.astype(BF16),
        gdn=[_gdn_prep(gdn_w_in[j], gdn_conv_w[j], gdn_a_log[j], gdn_dt_bias[j], gdn_norm_w[j], gdn_w_out[j],
                       heads=gh, dk=gdk, dv=gdv) for j in range(gdn_w_in.shape[0])],
        ml=[_mlstm_prep(ml_w_in[j], ml_b_if[j], ml_norm_w[j], ml_w_out[j], heads=mh, dk=mdk, dv=mdv)
            for j in range(ml_w_in.shape[0])],
        rw=[_rwkv_prep(rw_mu[j], rw_w_rkv[j], rw_w_o[j], rw_w0[j], rw_w1[j], rw_w2[j], rw_a0[j], rw_a1[j],
                       rw_a2[j], rw_g1[j], rw_g2[j], rw_k_k[j], rw_k_a[j], rw_r_k[j], rw_lnx_w[j], rw_lnx_b[j])
            for j in range(rw_mu.shape[0])])
    sample_states = (state_gdn_conv, state_gdn_S, state_mlstm_C, state_mlstm_n, state_mlstm_m,
                     state_rwkv_shift, state_rwkv_S)
    bp, tp, d = x_prompt.shape
    bs, ts, _ = x_sample.shape
    assert ts == 1
    prompt_states = tuple(jnp.zeros((s.shape[0], bp) + s.shape[2:], s.dtype) for s in sample_states)
    y_p, new_p = _trunk(x_prompt.reshape(bp * tp, d), prompt_states, w, batch=bp, seq=tp)
    y_s, new_s = _trunk(x_sample.reshape(bs * ts, d), sample_states, w, batch=bs, seq=ts)
    out = [y_p.reshape(bp, tp, d), y_s.reshape(bs, ts, d)]
    for a, b in zip(new_p, new_s):
        out += [a, b]
    return tuple(out)
```

```python
import functools

import jax
import jax.numpy as jnp
from jax import lax
from jax.experimental import pallas as pl
from jax.experimental.pallas import tpu as pltpu

F32 = jnp.float32
BF16 = jnp.bfloat16

RMS_EPS = 1e-6
NEG_BIG = -1e30
GATE_CAP = 15.0
RW_GN_EPS = 64e-5
CONV_W = 4
CHUNK = 64
V7X_VMEM_LIMIT = 56 * 1024 * 1024
HI = lax.Precision.HIGHEST


def _cparams(sem):
    return pltpu.CompilerParams(dimension_semantics=sem, vmem_limit_bytes=V7X_VMEM_LIMIT)


def _dot(a, b):
    return jnp.dot(a.astype(BF16), b.astype(BF16), preferred_element_type=F32)


def _dot_nt(a, b):
    return lax.dot_general(a.astype(BF16), b.astype(BF16), (((1,), (1,)), ((), ())),
                           preferred_element_type=F32)


def _dot_tn(a, b):
    return lax.dot_general(a.astype(BF16), b.astype(BF16), (((0,), (0,)), ((), ())),
                           preferred_element_type=F32)


def _dot_hi(a, b):
    return jnp.dot(a, b, preferred_element_type=F32, precision=HI)


def _sigmoid(x):
    return 1.0 / (1.0 + jnp.exp(-x))


def _silu(x):
    return x * _sigmoid(x)


def _softplus(x):
    return jnp.maximum(x, 0.0) + jnp.log(1.0 + jnp.exp(-jnp.abs(x)))


def _log_sigmoid(x):
    return -_softplus(-x)


def _tri_masks(l):
    r = lax.broadcasted_iota(jnp.int32, (l, l), 0)
    c = lax.broadcasted_iota(jnp.int32, (l, l), 1)
    return r >= c, r > c


INV_BASE = 16


def _unit_lower_inverse(mats, l):
    n = mats[0].shape[0]
    r = lax.broadcasted_iota(jnp.int32, (n, n), 0)
    c = lax.broadcasted_iota(jnp.int32, (n, n), 1)
    eye = (r == c).astype(F32)
    size = min(INV_BASE, l)
    shift = size.bit_length() - 1
    diag = (r >> shift) == (c >> shift)
    merges = []
    s = size
    while s < l:
        sh = s.bit_length() - 1
        off = ((r >> (sh + 1)) == (c >> (sh + 1))) & ((r >> sh) > (c >> sh))
        merges.append([jnp.where(off, a, 0.0).astype(BF16) for a in mats])
        s *= 2
    t = [eye - jnp.where(diag, a, 0.0) for a in mats]
    tb = [ti.astype(BF16) for ti in t]
    ab = [jnp.where(diag, a, 0.0).astype(BF16) for a in mats]
    p = [_dot(a, a).astype(BF16) for a in ab]
    k = 2
    while k < size:
        if 2 * k < size:
            both = [_dot(pi, jnp.concatenate([ti, pi], axis=1)) for ti, pi in zip(tb, p)]
            t = [ti + bi[:, :n] for ti, bi in zip(t, both)]
            p = [bi[:, n:].astype(BF16) for bi in both]
        else:
            t = [ti + _dot(pi, ti_b) for ti, ti_b, pi in zip(t, tb, p)]
        tb = [ti.astype(BF16) for ti in t]
        k *= 2
    for a_off in merges:
        x = [_dot(a, ti) for a, ti in zip(a_off, tb)]
        y = [_dot(ti, xi) for ti, xi in zip(tb, x)]
        t = [ti - yi for ti, yi in zip(t, y)]
        tb = [ti.astype(BF16) for ti in t]
    return tb


TM_IN_PROJ = 512
TM_GDN_PROJ = 512
TM_FFN, TF_FFN = 1024, 1024
TB_MIX = 256
TB_MLSTM = 512
TM_RWKV_PROJ = 512


def _norm_rows(x, g):
    return x * lax.rsqrt(jnp.mean(x * x, axis=-1, keepdims=True) + RMS_EPS) * g


def _in_proj_kernel(x_ref, g_ref, w_ref, wa_ref, o_ref, oa_ref):
    xn = _norm_rows(x_ref[...], g_ref[...]).astype(BF16)
    oa_ref[...] = jnp.dot(xn, wa_ref[...], preferred_element_type=F32)
    o_ref[...] = jnp.dot(xn, w_ref[...], preferred_element_type=F32)


def _in_proj(x, g, w_main, w_aux):
    m, k = x.shape
    n, na = w_main.shape[1], w_aux.shape[1]
    tm = min(TM_IN_PROJ, m)
    assert m % tm == 0
    return pl.pallas_call(
        _in_proj_kernel,
        grid=(m // tm,),
        in_specs=[pl.BlockSpec((tm, k), lambda i: (i, 0)), pl.BlockSpec((1, k), lambda i: (0, 0)),
                  pl.BlockSpec((k, n), lambda i: (0, 0)), pl.BlockSpec((k, na), lambda i: (0, 0))],
        out_specs=[pl.BlockSpec((tm, n), lambda i: (i, 0)), pl.BlockSpec((tm, na), lambda i: (i, 0))],
        out_shape=[jax.ShapeDtypeStruct((m, n), F32), jax.ShapeDtypeStruct((m, na), F32)],
        compiler_params=_cparams(("parallel",)),
        name="in_proj",
    )(x, g, w_main, w_aux)


def _ffn_kernel(res_ref, mix_ref, wo_ref, g_ref, w1_ref, w2_ref, go_ref, o_ref, xn_ref, acc_ref, *, out_norm):
    f = pl.program_id(1)

    @pl.when(f == 0)
    def _():
        x = res_ref[...] + jnp.dot(mix_ref[...].astype(BF16), wo_ref[...], preferred_element_type=F32)
        o_ref[...] = x
        xn_ref[...] = _norm_rows(x, g_ref[...]).astype(BF16)
        acc_ref[...] = jnp.zeros_like(acc_ref)

    h = jnp.dot(xn_ref[...], w1_ref[...], preferred_element_type=F32)
    a = jnp.square(jnp.maximum(h, 0.0)).astype(BF16)
    acc_ref[...] += jnp.dot(a, w2_ref[...], preferred_element_type=F32)

    @pl.when(f == pl.num_programs(1) - 1)
    def _():
        y = o_ref[...] + acc_ref[...]
        o_ref[...] = _norm_rows(y, go_ref[...]) if out_norm else y


def _ffn(res, mix, w_out, g, w1, w2, layer, g_out=None, *, tm, tf):
    m, d = res.shape
    dff = w1.shape[2]
    tm, tf = min(tm, m), min(tf, dff)
    assert m % tm == 0 and dff % tf == 0
    vec = pl.BlockSpec((1, d), lambda i, j: (0, 0))
    return pl.pallas_call(
        functools.partial(_ffn_kernel, out_norm=g_out is not None),
        grid=(m // tm, dff // tf),
        in_specs=[pl.BlockSpec((tm, d), lambda i, j: (i, 0)),
                  pl.BlockSpec((tm, mix.shape[1]), lambda i, j: (i, 0)),
                  pl.BlockSpec(w_out.shape, lambda i, j: (0, 0)), vec,
                  pl.BlockSpec((None, d, tf), lambda i, j: (layer, 0, j)),
                  pl.BlockSpec((None, tf, d), lambda i, j: (layer, j, 0)), vec],
        out_specs=pl.BlockSpec((tm, d), lambda i, j: (i, 0)),
        out_shape=jax.ShapeDtypeStruct((m, d), F32),
        scratch_shapes=[pltpu.VMEM((tm, d), BF16), pltpu.VMEM((tm, d), F32)],
        compiler_params=_cparams(("parallel", "arbitrary")),
        name="ffn",
    )(res, mix, w_out, g, w1, w2, g if g_out is None else g_out)


def _gdn_prompt_kernel(qkv_ref, pba_ref, s0_ref, gp_ref, nw_ref,
                       o_ref, s_out_ref,
                       s_ref, u_ref, w_ref, qd_ref, kd_ref, qk_ref, gl_ref,
                       *, tb, heads, dk, dv):
    t = pl.program_id(1)
    key = heads * dk
    ch = 2 * key + heads * dv
    l = CHUNK

    @pl.when(t == 0)
    def _():
        s_ref[...] = s0_ref[0]

    a_log = gp_ref[0:1, :]
    dt_bias = gp_ref[1:2, :]
    hh = range(heads)
    l2 = 2 * l
    r2 = lax.broadcasted_iota(jnp.int32, (l2, l2), 0)
    c2 = lax.broadcasted_iota(jnp.int32, (l2, l2), 1)
    same = (r2 >= l) == (c2 >= l)
    incl = same & (r2 >= c2)
    strict = same & (r2 > c2)
    tril = incl.astype(F32)
    first = lax.broadcasted_iota(jnp.int32, (l2, 128), 0) < l

    pairs = range(tb // l2)
    rows = [slice(pi * l2, (pi + 1) * l2) for pi in pairs]
    beta_all, gc, gc_t, g_end = [], [], [], []
    for pi in pairs:
        ba = pba_ref[rows[pi], :]
        beta_all.append(_sigmoid(ba))
        g_all = -jnp.exp(a_log) * _softplus(ba + dt_bias)
        gc.append(_dot_hi(tril, g_all))
        gc_t.append(gc[pi].T)
        g_end.append(jnp.where(first, gc[pi][l - 1:l, :], gc[pi][l2 - 1:l2, :]))
        gl_ref[2 * pi:2 * pi + 1, :] = jnp.exp(gc[pi][l - 1:l, :])
        gl_ref[2 * pi + 1:2 * pi + 2, :] = jnp.exp(gc[pi][l2 - 1:l2, :])
    cc = [(pi, h) for pi in pairs for h in hh]
    b_col = [beta_all[pi][:, h:h + 1] for pi, h in cc]
    gi = [gc[pi][:, heads + h:heads + h + 1] for pi, h in cc]
    q = [qkv_ref[rows[pi], h * dk:(h + 1) * dk] for pi, h in cc]
    k = [qkv_ref[rows[pi], key + h * dk:key + (h + 1) * dk] for pi, h in cc]
    v = [qkv_ref[rows[pi], 2 * key + h * dv:2 * key + (h + 1) * dv] for pi, h in cc]
    nc = range(len(cc))
    dmat = [jnp.where(incl, jnp.exp(jnp.where(incl, gi[i] - gc_t[pi][heads + h:heads + h + 1, :], 0.0)), 0.0)
            for i, (pi, h) in enumerate(cc)]
    kb = [k[i] * b_col[i] for i in nc]
    kk = [_dot_nt(kb[i], k[i]) for i in nc]
    qk = [_dot_nt(q[i], k[i]) for i in nc]
    t_inv = _unit_lower_inverse([jnp.where(strict, kk[i] * dmat[i], 0.0) for i in nc], l)
    egi = [jnp.exp(gi[i]) for i in nc]
    sol = [_dot(t_inv[i], jnp.concatenate([v[i] * b_col[i], kb[i] * egi[i]], axis=-1)) for i in nc]
    for i, (pi, h) in enumerate(cc):
        hs = slice(h * dk, (h + 1) * dk)
        u_ref[rows[pi], h * dv:(h + 1) * dv] = sol[i][:, :dv]
        w_ref[rows[pi], hs] = sol[i][:, dv:].astype(BF16)
        qd_ref[rows[pi], hs] = (q[i] * egi[i]).astype(BF16)
        kd_ref[rows[pi], hs] = (k[i] * jnp.exp(g_end[pi][:, heads + h:heads + h + 1] - gi[i])).astype(BF16)
        qkm = jnp.where(incl, qk[i] * dmat[i], 0.0).astype(BF16)
        qk_ref[h, pi * l2:pi * l2 + l, :] = qkm[:l, :l]
        qk_ref[h, pi * l2 + l:(pi + 1) * l2, :] = qkm[l:, l:]

    for ci in range(tb // l):
        rows = slice(ci * l, (ci + 1) * l)
        s = [s_ref[h] for h in hh]
        wq = [_dot(jnp.concatenate([w_ref[rows, h * dk:(h + 1) * dk], qd_ref[rows, h * dk:(h + 1) * dk]], axis=0),
                   s[h]) for h in hh]
        v_new = [u_ref[rows, h * dv:(h + 1) * dv] - wq[h][:l] for h in hh]
        o2 = [_dot(qk_ref[h, rows, :], v_new[h]) for h in hh]
        ds = [_dot_tn(kd_ref[rows, h * dk:(h + 1) * dk], v_new[h]) for h in hh]
        for h in hh:
            s_ref[h] = s[h] * gl_ref[ci:ci + 1, heads + h:heads + h + 1] + ds[h]
            o = wq[h][l:] + o2[h]
            z = qkv_ref[rows, ch + h * dv:ch + (h + 1) * dv]
            o = o * lax.rsqrt(jnp.mean(o * o, axis=-1, keepdims=True) + RMS_EPS) * nw_ref[...]
            o_ref[rows, h * dv:(h + 1) * dv] = o * _silu(z)

    @pl.when(t == pl.num_programs(1) - 1)
    def _():
        s_out_ref[0] = s_ref[...]


def _gdn_prompt(qkvz, pba, s0, gp, nw, *, batch, seq, heads, dk, dv, tb):
    key, val = heads * dk, heads * dv
    ch = 2 * key + val
    tb = min(tb, seq)
    assert seq % tb == 0 and tb % (2 * CHUNK) == 0
    nt = seq // tb
    return pl.pallas_call(
        functools.partial(_gdn_prompt_kernel, tb=tb, heads=heads, dk=dk, dv=dv),
        grid=(batch, nt),
        in_specs=[pl.BlockSpec((tb, ch + val), lambda b, t: (b * nt + t, 0)),
                  pl.BlockSpec((tb, 128), lambda b, t: (b * nt + t, 0)),
                  pl.BlockSpec((1, heads, dk, dv), lambda b, t: (b, 0, 0, 0)),
                  pl.BlockSpec((8, 128), lambda b, t: (0, 0)),
                  pl.BlockSpec((1, dv), lambda b, t: (0, 0))],
        out_specs=[pl.BlockSpec((tb, val), lambda b, t: (b * nt + t, 0)),
                   pl.BlockSpec((1, heads, dk, dv), lambda b, t: (b, 0, 0, 0))],
        out_shape=[jax.ShapeDtypeStruct((batch * seq, val), F32),
                   jax.ShapeDtypeStruct((batch, heads, dk, dv), F32)],
        scratch_shapes=[pltpu.VMEM((heads, dk, dv), F32),
                        pltpu.VMEM((tb, val), F32), pltpu.VMEM((tb, key), BF16),
                        pltpu.VMEM((tb, key), BF16), pltpu.VMEM((tb, key), BF16),
                        pltpu.VMEM((heads, tb, CHUNK), BF16), pltpu.VMEM((max(8, tb // CHUNK), 128), F32)],
        compiler_params=_cparams(("parallel", "arbitrary")),
        name="gdn_prompt",
    )(qkvz, pba, s0, gp, nw)


GDN_PROJ_COLS = 256
GDN_CONV_ROWS = 64


def _gdn_in_proj_kernel(x_ref, g_ref, w_ref, wa_ref, conv0_ref, cw_ref, o_ref, aux_ref, conv_out_ref,
                        xn_ref, carry_ref, *, heads, dk, dv):
    t = pl.program_id(1)
    key = heads * dk
    ch = 2 * key + heads * dv
    tm = x_ref.shape[0]
    n = w_ref.shape[1]

    @pl.when(t == 0)
    def _():
        carry_ref[...] = conv0_ref[0]

    xn_ref[...] = _norm_rows(x_ref[...], g_ref[...]).astype(BF16)
    aux_ref[...] = jnp.dot(xn_ref[...], wa_ref[...], preferred_element_type=F32)
    for c0 in range(0, n, GDN_PROJ_COLS):
        cs = slice(c0, c0 + GDN_PROJ_COLS)
        y = jnp.dot(xn_ref[...], w_ref[:, cs], preferred_element_type=F32)
        if c0 >= ch:
            o_ref[:, cs] = y
            continue
        ext = jnp.concatenate([carry_ref[:, cs], y], axis=0)
        carry_ref[:, cs] = y[tm - 8:, :]
        for r0 in range(0, tm, GDN_CONV_ROWS):
            for j in range(0, GDN_PROJ_COLS, dk):
                cj = slice(c0 + j, c0 + j + dk)
                blk = ext[r0:r0 + GDN_CONV_ROWS + 8, j:j + dk]
                z = blk[8:] * cw_ref[CONV_W - 1:CONV_W, cj]
                for s in range(1, CONV_W):
                    z = z + pltpu.roll(blk, s, 0)[8:] * cw_ref[CONV_W - 1 - s:CONV_W - s, cj]
                z = _silu(z)
                if c0 + j < key:
                    z = z * lax.rsqrt(jnp.sum(z * z, axis=-1, keepdims=True) + 1e-6) * (dk ** -0.5)
                elif c0 + j < 2 * key:
                    z = z * lax.rsqrt(jnp.sum(z * z, axis=-1, keepdims=True) + 1e-6)
                o_ref[r0:r0 + GDN_CONV_ROWS, cj] = z

    @pl.when(t == pl.num_programs(1) - 1)
    def _():
        conv_out_ref[0] = carry_ref[...]


def _gdn_in_proj(x, g, w_main, w_ba, conv0, cw_t, *, batch, seq, heads, dk, dv, tm):
    m, d = x.shape
    key, val = heads * dk, heads * dv
    ch = 2 * key + val
    n = w_main.shape[1]
    assert dk == dv and n % GDN_PROJ_COLS == 0 and ch % GDN_PROJ_COLS == 0 and GDN_PROJ_COLS % dk == 0
    tm = min(tm, seq)
    assert seq % tm == 0 and tm >= 8
    nt = seq // tm
    row = lambda width: pl.BlockSpec((tm, width), lambda b, t: (b * nt + t, 0))
    full = lambda a: pl.BlockSpec(a.shape, lambda b, t: (0,) * a.ndim)
    st = pl.BlockSpec((1, 8, ch), lambda b, t: (b, 0, 0))
    return pl.pallas_call(
        functools.partial(_gdn_in_proj_kernel, heads=heads, dk=dk, dv=dv),
        grid=(batch, nt),
        in_specs=[row(d), full(g), full(w_main), full(w_ba), st, full(cw_t)],
        out_specs=[row(n), row(w_ba.shape[1]), st],
        out_shape=[jax.ShapeDtypeStruct((m, n), F32), jax.ShapeDtypeStruct((m, w_ba.shape[1]), F32),
                   jax.ShapeDtypeStruct((batch, 8, ch), F32)],
        scratch_shapes=[pltpu.VMEM((tm, d), BF16), pltpu.VMEM((8, ch), F32)],
        compiler_params=_cparams(("parallel", "arbitrary")),
        name="gdn_in_proj",
    )(x, g, w_main, w_ba, conv0, cw_t)


def _mlstm_prompt_kernel(pm_ref, pif_ref, bif_ref, c0_ref, n0_ref, m0_ref, nw_ref,
                         o_ref, c_out_ref, n_out_ref, m_out_ref,
                         c_ref, n_ref, m_ref, bc_ref, ni_ref, col_ref, kv_ref, kc_ref, sc_ref,
                         cin_ref, nin_ref, min_ref, *, tb, heads, dk, dv):
    t = pl.program_id(1)
    l = CHUNK
    qk_w = heads * dk
    v_off = 2 * qk_w
    o_off = v_off + heads * dv

    @pl.when(t == 0)
    def _():
        c_ref[...] = c0_ref[0]
        n_ref[...] = n0_ref[0]
        m_ref[...] = m0_ref[0]

    incl, _ = _tri_masks(l)
    tril = incl.astype(F32)
    hh = range(heads)
    nchunk = tb // l
    scale = dk ** -0.5

    crow = [slice(ci * l, (ci + 1) * l) for ci in range(nchunk)]
    gates, bcum, bcum_t, gates_t = [], [], [], []
    for ci in range(nchunk):
        g = pif_ref[crow[ci], :] + bif_ref[...]
        g = GATE_CAP * jnp.tanh(g / GATE_CAP)
        gates.append(g)
        bcum.append(_dot_hi(tril, _log_sigmoid(g)))
        bcum_t.append(bcum[ci].T)
        gates_t.append(g.T)
        bc_ref[crow[ci], :] = bcum[ci]
    cc = [(ci, h) for ci in range(nchunk) for h in hh]
    nc = range(len(cc))
    bi = [bcum[ci][:, heads + h:heads + h + 1] for ci, h in cc]
    b_last = [bcum[ci][l - 1:l, heads + h:heads + h + 1] for ci, h in cc]
    k = [pm_ref[crow[ci], qk_w + h * dk:qk_w + (h + 1) * dk] for ci, h in cc]
    v = [pm_ref[crow[ci], v_off + h * dv:v_off + (h + 1) * dv].astype(BF16) for ci, h in cc]
    qk = [_dot_nt(pm_ref[crow[ci], h * dk:(h + 1) * dk] * scale, k[i]) for i, (ci, h) in enumerate(cc)]
    dlog = [jnp.where(incl, bi[i] - bcum_t[ci][heads + h:heads + h + 1, :] + gates_t[ci][h:h + 1, :], NEG_BIG)
            for i, (ci, h) in enumerate(cc)]
    m_intra = [jnp.max(dlog[i], axis=-1, keepdims=True) for i in nc]
    p = [jnp.where(incl, jnp.exp(dlog[i] - m_intra[i]), 0.0) * qk[i] for i in nc]
    den_intra = [jnp.sum(p[i], axis=-1, keepdims=True) for i in nc]
    num_intra = [_dot(p[i], v[i]) for i in nc]
    a_log = [b_last[i] - bi[i] + gates[ci][:, h:h + 1] for i, (ci, h) in enumerate(cc)]
    m_chunk = [jnp.max(a_log[i], axis=0, keepdims=True) for i in nc]
    kw = [k[i] * jnp.exp(a_log[i] - m_chunk[i]) for i in nc]
    kv_chunk = [_dot_tn(kw[i], v[i]) for i in nc]
    for i, (ci, h) in enumerate(cc):
        ni_ref[crow[ci], h * dv:(h + 1) * dv] = num_intra[i]
        col_ref[crow[ci], h:h + 1] = m_intra[i]
        col_ref[crow[ci], heads + h:heads + h + 1] = den_intra[i]
        kv_ref[i] = kv_chunk[i]
        kc_ref[i:i + 1, :] = jnp.sum(kw[i], axis=0, keepdims=True)
        sc_ref[i:i + 1, 0:1] = m_chunk[i]
        sc_ref[i:i + 1, 1:2] = b_last[i]

    for ci in range(nchunk):
        for h in hh:
            i = ci * heads + h
            c_mat = c_ref[h]
            n_vec = n_ref[h:h + 1, :]
            m_prev = m_ref[h:h + 1, :]
            cin_ref[i] = c_mat
            nin_ref[i:i + 1, :] = n_vec
            min_ref[i:i + 1, :] = m_prev
            m_chunk = sc_ref[i:i + 1, 0:1]
            b_last = sc_ref[i:i + 1, 1:2]
            m_new = jnp.maximum(b_last + m_prev, m_chunk)
            f_s = jnp.exp(b_last + m_prev - m_new)
            i_s = jnp.exp(m_chunk - m_new)
            c_ref[h] = f_s[:, 0:1] * c_mat + i_s[:, 0:1] * kv_ref[i]
            n_ref[h:h + 1, :] = f_s * n_vec + i_s * kc_ref[i:i + 1, :]
            m_ref[h:h + 1, :] = m_new

    q = [pm_ref[crow[ci], h * dk:(h + 1) * dk] * scale for ci, h in cc]
    qc = [_dot(q[i], cin_ref[i]) for i in nc]
    qn = [jnp.sum(q[i] * nin_ref[i:i + 1, :], axis=-1, keepdims=True) for i in nc]
    m_prev = [min_ref[i:i + 1, 0:1] for i in nc]
    bi = [bc_ref[crow[ci], heads + h:heads + h + 1] for ci, h in cc]
    m_in = [col_ref[crow[ci], h:h + 1] for ci, h in cc]
    m_t = [jnp.maximum(bi[i] + m_prev[i], m_in[i]) for i in nc]
    s_inter = [jnp.exp(bi[i] + m_prev[i] - m_t[i]) for i in nc]
    s_intra = [jnp.exp(m_in[i] - m_t[i]) for i in nc]
    den = [s_inter[i] * qn[i] + s_intra[i] * col_ref[crow[ci], heads + h:heads + h + 1]
           for i, (ci, h) in enumerate(cc)]
    h_t = [(s_inter[i] * qc[i] + s_intra[i] * ni_ref[crow[ci], h * dv:(h + 1) * dv])
           / jnp.maximum(jnp.abs(den[i]), jnp.exp(-m_t[i])) for i, (ci, h) in enumerate(cc)]
    ms = [jnp.mean(h_t[i] * h_t[i], axis=-1, keepdims=True) for i in nc]
    for i, (ci, h) in enumerate(cc):
        h_n = h_t[i] * lax.rsqrt(ms[i] + RMS_EPS) * nw_ref[:, h * dv:(h + 1) * dv]
        o_pre = pm_ref[crow[ci], o_off + h * dv:o_off + (h + 1) * dv]
        o_ref[crow[ci], h * dv:(h + 1) * dv] = _sigmoid(o_pre) * h_n

    @pl.when(t == pl.num_programs(1) - 1)
    def _():
        c_out_ref[0] = c_ref[...]
        n_out_ref[0] = n_ref[...]
        m_out_ref[0] = m_ref[...]


def _mlstm_prompt(pm, pif, bif, c0, n0, m0, nw, *, batch, seq, heads, dk, dv, tb):
    width = pm.shape[1]
    val = heads * dv
    tb = min(tb, seq)
    assert seq % tb == 0 and tb % CHUNK == 0
    nt = seq // tb
    nck = (tb // CHUNK) * heads
    return pl.pallas_call(
        functools.partial(_mlstm_prompt_kernel, tb=tb, heads=heads, dk=dk, dv=dv),
        grid=(batch, nt),
        in_specs=[pl.BlockSpec((tb, width), lambda b, t: (b * nt + t, 0)),
                  pl.BlockSpec((tb, 128), lambda b, t: (b * nt + t, 0)),
                  pl.BlockSpec((1, 128), lambda b, t: (0, 0)),
                  pl.BlockSpec((1, heads, dk, dv), lambda b, t: (b, 0, 0, 0)),
                  pl.BlockSpec((1, 8, dk), lambda b, t: (b, 0, 0)),
                  pl.BlockSpec((1, 8, 128), lambda b, t: (b, 0, 0)),
                  pl.BlockSpec((1, val), lambda b, t: (0, 0))],
        out_specs=[pl.BlockSpec((tb, val), lambda b, t: (b * nt + t, 0)),
                   pl.BlockSpec((1, heads, dk, dv), lambda b, t: (b, 0, 0, 0)),
                   pl.BlockSpec((1, 8, dk), lambda b, t: (b, 0, 0)),
                   pl.BlockSpec((1, 8, 128), lambda b, t: (b, 0, 0))],
        out_shape=[jax.ShapeDtypeStruct((batch * seq, val), F32),
                   jax.ShapeDtypeStruct((batch, heads, dk, dv), F32),
                   jax.ShapeDtypeStruct((batch, 8, dk), F32),
                   jax.ShapeDtypeStruct((batch, 8, 128), F32)],
        scratch_shapes=[pltpu.VMEM((heads, dk, dv), F32), pltpu.VMEM((8, dk), F32),
                        pltpu.VMEM((8, 128), F32),
                        pltpu.VMEM((tb, 128), F32), pltpu.VMEM((tb, val), F32), pltpu.VMEM((tb, 128), F32),
                        pltpu.VMEM((nck, dk, dv), F32), pltpu.VMEM((max(8, nck), dk), F32),
                        pltpu.VMEM((max(8, nck), 128), F32),
                        pltpu.VMEM((nck, dk, dv), F32), pltpu.VMEM((max(8, nck), dk), F32),
                        pltpu.VMEM((max(8, nck), 128), F32)],
        compiler_params=_cparams(("parallel", "arbitrary")),
        name="mlstm_prompt",
    )(pm, pif, bif, c0, n0, m0, nw)


def _mlstm_prep(w_in, b_if, norm_w, w_out, *, heads, dk, dv):
    main = 2 * heads * dk + 2 * heads * dv
    return dict(w_main=w_in[:, :main].astype(BF16),
                w_if=_pad_cols(w_in[:, main:], 128).astype(BF16),
                bif=_pad_cols(b_if[None, :], 128), nw=norm_w[None, :], w_out=w_out.astype(BF16))


def _mlstm_prompt_layer(x, g, p, c0, n0, m0, *, batch, seq, heads, dk, dv):
    pm, pif = _in_proj(x, g, p["w_main"], p["w_if"])
    n0p = jnp.pad(n0, ((0, 0), (0, 8 - heads), (0, 0)))
    m0p = jnp.broadcast_to(jnp.pad(m0, ((0, 0), (0, 8 - heads)))[:, :, None], (batch, 8, 128))
    o, c, n, m = _mlstm_prompt(pm, pif, p["bif"], c0, n0p, m0p, p["nw"],
                               batch=batch, seq=seq, heads=heads, dk=dk, dv=dv, tb=TB_MLSTM)
    return o, c, n[:, :heads, :], m[:, :heads, 0]


def _rwkv_proj_body(h, prev, mu_ref, wrkv_ref, w1_ref, w2_ref, a1_ref, a2_ref, g1_ref, g2_ref,
                    vec_ref, r_ref, k_ref, v_ref, lw_ref, kk_ref, a_ref, g_ref):
    xx = prev - h

    def mix(j):
        return (h + xx * mu_ref[j:j + 1, :]).astype(BF16)

    w0, a0, k_k, k_a = (vec_ref[j:j + 1, :] for j in range(4))
    r_ref[...] = jnp.dot(mix(0), wrkv_ref[0], preferred_element_type=F32)
    lora_w = _dot(jnp.tanh(_dot(mix(1), w1_ref[...])), w2_ref[...])
    w_log = -_softplus(-(w0 + lora_w)) - 0.5
    lw_ref[...] = -jnp.exp(w_log)
    k = jnp.dot(mix(2), wrkv_ref[1], preferred_element_type=F32)
    v_ref[...] = jnp.dot(mix(3), wrkv_ref[2], preferred_element_type=F32)
    a = _sigmoid(a0 + _dot(_dot(mix(4), a1_ref[...]), a2_ref[...]))
    g_ref[...] = _dot(_sigmoid(_dot(mix(5), g1_ref[...])), g2_ref[...])
    kk_ref[...] = k * k_k
    k_ref[...] = k * (1.0 + (a - 1.0) * k_a)
    a_ref[...] = a


def _rwkv_proj_sample_kernel(x_ref, gn_ref, prev_ref, *refs):
    h = _norm_rows(x_ref[...], gn_ref[...])
    hn_ref = refs[-1]
    hn_ref[...] = h
    _rwkv_proj_body(h, prev_ref[...], *refs[:-1])


def _rwkv_proj_prompt_kernel(x_ref, gn_ref, shift0_ref, *refs):
    carry_ref = refs[-1]
    shift_out_ref = refs[-2]
    t = pl.program_id(1)

    @pl.when(t == 0)
    def _():
        carry_ref[...] = shift0_ref[0]

    h = _norm_rows(x_ref[...], gn_ref[...])
    rows = h.shape[0]
    first = lax.broadcasted_iota(jnp.int32, h.shape, 0) == 0
    prev = jnp.where(first, carry_ref[0:1, :], pltpu.roll(h, 1, 0))
    carry_ref[0:1, :] = h[rows - 1:rows, :]
    _rwkv_proj_body(h, prev, *refs[:-2])

    @pl.when(t == pl.num_programs(1) - 1)
    def _():
        shift_out_ref[0] = carry_ref[...]


def _rwkv_proj(x, g_norm, shift0, p, *, batch, seq, tm):
    m, d = x.shape
    consts = [p["mu"], p["w_rkv"], p["w1"], p["w2"], p["a1"], p["a2"], p["g1"], p["g2"], p["vec"]]
    if seq == 1:
        row = pl.BlockSpec((m, d), lambda i: (0, 0))
        out = pl.pallas_call(
            _rwkv_proj_sample_kernel,
            grid=(1,),
            in_specs=[row, pl.BlockSpec((1, d), lambda i: (0, 0)), row] + [_const_block(a) for a in consts],
            out_specs=[row] * 8,
            out_shape=[jax.ShapeDtypeStruct((m, d), F32)] * 8,
            compiler_params=_cparams(("arbitrary",)),
            name="rwkv_proj_sample",
        )(x, g_norm, shift0, *consts)
        return out[:7], out[7]
    tm = min(tm, seq)
    assert seq % tm == 0
    nt = seq // tm
    row = pl.BlockSpec((tm, d), lambda b, t: (b * nt + t, 0))
    st = pl.BlockSpec((1, 8, d), lambda b, t: (b, 0, 0))

    def full(a):
        nd = a.ndim
        return pl.BlockSpec(a.shape, lambda b, t: (0,) * nd)

    shift0_p = jnp.pad(shift0[:, None, :], ((0, 0), (0, 7), (0, 0)))
    out = pl.pallas_call(
        _rwkv_proj_prompt_kernel,
        grid=(batch, nt),
        in_specs=[row, full(g_norm), st] + [full(a) for a in consts],
        out_specs=[row] * 7 + [st],
        out_shape=[jax.ShapeDtypeStruct((m, d), F32)] * 7 + [jax.ShapeDtypeStruct((batch, 8, d), F32)],
        scratch_shapes=[pltpu.VMEM((8, d), F32)],
        compiler_params=_cparams(("parallel", "arbitrary")),
        name="rwkv_proj_prompt",
    )(x, g_norm, shift0_p, *consts)
    return out[:7], out[7][:, 0, :]


def _rwkv_prompt_kernel(r_ref, k_ref, v_ref, lw_ref, kk_ref, a_ref, g_ref, s0_ref, hp_ref,
                        o_ref, s_out_ref, s_ref, rr_ref, yy_ref, mx_ref, n0_ref, gw_ref, gb_ref, el_ref,
                        *, tb, heads, hd):
    t = pl.program_id(1)
    l = CHUNK

    @pl.when(t == 0)
    def _():
        s_ref[...] = s0_ref[0]

    hh = range(heads)
    hs = [slice(h * hd, (h + 1) * hd) for h in hh]
    l2 = 2 * l
    r2 = lax.broadcasted_iota(jnp.int32, (l2, l2), 0)
    c2 = lax.broadcasted_iota(jnp.int32, (l2, l2), 1)
    same = (r2 >= l) == (c2 >= l)
    incl = same & (r2 >= c2)
    strict = same & (r2 > c2)
    tril = incl.astype(F32)
    first = lax.broadcasted_iota(jnp.int32, (l2, heads * hd), 0) < l
    first2 = (lax.broadcasted_iota(jnp.int32, (2 * l2, hd), 0) & l) == 0
    zeros = jnp.zeros((l2, hd), F32)

    pairs = range(tb // l2)
    prow = [slice(pi * l2, (pi + 1) * l2) for pi in pairs]
    lwc, e_in, e_prev, e_neg, e_end = [], [], [], [], []
    for pi in pairs:
        lw = lw_ref[prow[pi], :]
        c = _dot_hi(tril, lw)
        lwc.append(c)
        e_in.append(jnp.exp(c))
        e_prev.append(jnp.exp(c - lw))
        e_neg.append(jnp.exp(-c))
        e_end.append(jnp.exp(jnp.where(first, c[l - 1:l, :], c[l2 - 1:l2, :]) - c))
    cc = [(pi, h) for pi in pairs for h in hh]
    nc = range(len(cc))
    r = [r_ref[prow[pi], hs[h]] for pi, h in cc]
    v = [v_ref[prow[pi], hs[h]] for pi, h in cc]
    k = [k_ref[prow[pi], hs[h]] for pi, h in cc]
    kk = [kk_ref[prow[pi], hs[h]] for pi, h in cc]
    kk = [kk[i] * lax.rsqrt(jnp.maximum(jnp.sum(kk[i] * kk[i], axis=-1, keepdims=True), 1e-24)) for i in nc]
    bv = [kk[i] * a_ref[prow[pi], hs[h]] for i, (pi, h) in enumerate(cc)]
    a_t = [-kk[i] * e_prev[pi][:, hs[h]] for i, (pi, h) in enumerate(cc)]
    r_t = [r[i] * e_in[pi][:, hs[h]] for i, (pi, h) in enumerate(cc)]
    gm = [_dot_nt(jnp.concatenate([a_t[i], r_t[i]], axis=0),
                  jnp.concatenate([bv[i] * e_neg[pi][:, hs[h]], k[i] * e_neg[pi][:, hs[h]]], axis=0))
          for i, (pi, h) in enumerate(cc)]
    ak_m = [jnp.where(strict, gm[i][:l2, l2:], 0.0).astype(BF16) for i in nc]
    rbk_m = [jnp.concatenate([jnp.where(incl, gm[i][l2:, :l2], 0.0),
                              jnp.where(incl, gm[i][l2:, l2:], 0.0)], axis=1).astype(BF16) for i in nc]
    t_inv = _unit_lower_inverse([jnp.where(strict, -gm[i][:l2, :l2], 0.0) for i in nc], l)
    vb = [v[i].astype(BF16) for i in nc]
    akv = [_dot(ak_m[i], vb[i]) for i in nc]
    x1 = [_dot(t_inv[i], jnp.concatenate([a_t[i], akv[i]], axis=1)).astype(BF16) for i in nc]
    low = [jnp.concatenate([x1[i], jnp.concatenate([zeros.astype(BF16), vb[i]], axis=1)], axis=0) for i in nc]
    x2 = [_dot(rbk_m[i], low[i]) for i in nc]
    bk = [jnp.concatenate([bv[i] * e_end[pi][:, hs[h]], k[i] * e_end[pi][:, hs[h]]], axis=0)
          for i, (pi, h) in enumerate(cc)]
    bk2 = [jnp.concatenate([jnp.where(first2, bk[i], 0.0), jnp.where(first2, 0.0, bk[i])], axis=1) for i in nc]
    mn = [_dot_tn(low[i], bk2[i]) for i in nc]
    for i, (pi, h) in enumerate(cc):
        for c in range(2):
            crow = slice(pi * l2 + c * l, pi * l2 + (c + 1) * l)
            mx_ref[h, crow, :] = mn[i][:hd, c * hd:(c + 1) * hd].astype(BF16)
            n0_ref[h, crow, :] = mn[i][hd:, c * hd:(c + 1) * hd]
        rr_ref[h, prow[pi], :] = (r_t[i] + x2[i][:, :hd]).astype(BF16)
        yy_ref[h, prow[pi], :] = x2[i][:, hd:]
        g = g_ref[prow[pi], hs[h]]
        bonus = jnp.sum(r[i] * k[i] * hp_ref[0:1, hs[h]], axis=-1, keepdims=True) * v[i]
        gw_ref[h, prow[pi], :] = hp_ref[1:2, hs[h]] * g
        gb_ref[h, prow[pi], :] = (hp_ref[2:3, hs[h]] + bonus) * g
        el_ref[h, 2 * pi:2 * pi + 1, :] = jnp.exp(lwc[pi][l - 1:l, hs[h]])
        el_ref[h, 2 * pi + 1:2 * pi + 2, :] = jnp.exp(lwc[pi][l2 - 1:l2, hs[h]])

    for ci in range(tb // l):
        rows = slice(ci * l, (ci + 1) * l)
        s = [s_ref[h] for h in hh]
        y = [_dot_nt(rr_ref[h, rows, :], s[h]) for h in hh]
        sm = [_dot(s[h], mx_ref[h, rows, :]) for h in hh]
        for h in hh:
            s_ref[h] = s[h] * el_ref[h, ci:ci + 1, :] + sm[h] + n0_ref[h, rows, :]
        y = [y[h] + yy_ref[h, rows, :] for h in hh]
        mean = [jnp.mean(y[h], axis=-1, keepdims=True) for h in hh]
        yc = [y[h] - mean[h] for h in hh]
        var = [jnp.mean(yc[h] * yc[h], axis=-1, keepdims=True) for h in hh]
        for h in hh:
            o_ref[rows, hs[h]] = yc[h] * lax.rsqrt(var[h] + RW_GN_EPS) * gw_ref[h, rows, :] + gb_ref[h, rows, :]

    @pl.when(t == pl.num_programs(1) - 1)
    def _():
        s_out_ref[0] = s_ref[...]


def _rwkv_prompt(r, k, v, lw, kk, a, g, s0, hp, *, batch, seq, heads, hd, tb):
    d = heads * hd
    tb = min(tb, seq)
    assert seq % tb == 0 and tb % CHUNK == 0
    nt = seq // tb
    row = pl.BlockSpec((tb, d), lambda b, t: (b * nt + t, 0))
    st = pl.BlockSpec((1, heads, hd, hd), lambda b, t: (b, 0, 0, 0))
    return pl.pallas_call(
        functools.partial(_rwkv_prompt_kernel, tb=tb, heads=heads, hd=hd),
        grid=(batch, nt),
        in_specs=[row] * 7 + [st, pl.BlockSpec((8, d), lambda b, t: (0, 0))],
        out_specs=[row, st],
        out_shape=[jax.ShapeDtypeStruct((batch * seq, d), F32),
                   jax.ShapeDtypeStruct((batch, heads, hd, hd), F32)],
        scratch_shapes=[pltpu.VMEM((heads, hd, hd), F32),
                        pltpu.VMEM((heads, tb, hd), BF16), pltpu.VMEM((heads, tb, hd), F32),
                        pltpu.VMEM((heads, tb, hd), BF16), pltpu.VMEM((heads, tb, hd), F32),
                        pltpu.VMEM((heads, tb, hd), F32), pltpu.VMEM((heads, tb, hd), F32),
                        pltpu.VMEM((heads, max(8, tb // CHUNK), hd), F32)],
        compiler_params=_cparams(("parallel", "arbitrary")),
        name="rwkv_prompt",
    )(r, k, v, lw, kk, a, g, s0, hp)


def _pad_rows(a, n):
    return jnp.pad(a, ((0, n - a.shape[0]), (0, 0)))


def _rwkv_prep(mu, w_rkv, w_o, w0, w1, w2, a0, a1, a2, g1, g2, k_k, k_a, r_k, lnx_w, lnx_b):
    d = w0.shape[0]
    lw = -(-w1.shape[1] // 128) * 128
    la = -(-a1.shape[1] // 128) * 128
    lg = -(-g1.shape[1] // 128) * 128
    return dict(mu=_pad_rows(mu, 8), w_rkv=w_rkv.astype(BF16), w_o=w_o.astype(BF16),
                w1=_pad_cols(w1, lw).astype(BF16), w2=_pad_rows(w2, lw).astype(BF16),
                a1=_pad_cols(a1, la).astype(BF16), a2=_pad_rows(a2, la).astype(BF16),
                g1=_pad_cols(g1, lg).astype(BF16), g2=_pad_rows(g2, lg).astype(BF16),
                vec=_pad_rows(jnp.stack([w0, a0, k_k, k_a]), 8),
                hp=_pad_rows(jnp.stack([r_k.reshape(d), lnx_w, lnx_b]), 8))


def _rwkv_prompt_layer(x, g_norm, p, shift0, s0, *, batch, seq, heads, hd):
    (r, k, v, lw, kk, a, g), shift = _rwkv_proj(x, g_norm, shift0, p, batch=batch, seq=seq, tm=TM_RWKV_PROJ)
    o, s = _rwkv_prompt(r, k, v, lw, kk, a, g, s0, p["hp"], batch=batch, seq=seq, heads=heads, hd=hd, tb=TB_MIX)
    return o, shift, s


def _gdn_sample_pre_kernel(pm_ref, pba_ref, conv_ref, cw_ref, gp_ref, qkv_ref, conv_out_ref, sc_ref,
                           *, heads, dk, dv):
    key = heads * dk
    ch = 2 * key + heads * dv
    u = pm_ref[:, 0:ch]
    y = u * cw_ref[CONV_W - 1:CONV_W, :]
    for j in range(CONV_W - 1):
        y = y + conv_ref[j] * cw_ref[j:j + 1, :]
        conv_out_ref[j] = conv_ref[j + 1] if j + 1 < CONV_W - 1 else u
    y = _silu(y)
    for c in range(ch // 128):
        cs = slice(c * 128, (c + 1) * 128)
        yc = y[:, cs]
        if c * 128 < key:
            yc = yc * lax.rsqrt(jnp.sum(yc * yc, axis=-1, keepdims=True) + 1e-6) * (dk ** -0.5)
        elif c * 128 < 2 * key:
            yc = yc * lax.rsqrt(jnp.sum(yc * yc, axis=-1, keepdims=True) + 1e-6)
        qkv_ref[:, cs] = yc
    ba = pba_ref[...]
    lane = lax.broadcasted_iota(jnp.int32, ba.shape, 1)
    g = -jnp.exp(gp_ref[0:1, :]) * _softplus(ba + gp_ref[1:2, :])
    sc_ref[...] = jnp.where(lane < heads, _sigmoid(ba), jnp.exp(g))


def _gdn_sample_pre(pm, pba, conv_t, cw_t, gp, *, heads, dk, dv):
    n = pm.shape[0]
    ch = 2 * heads * dk + heads * dv
    return pl.pallas_call(
        functools.partial(_gdn_sample_pre_kernel, heads=heads, dk=dk, dv=dv),
        out_shape=[jax.ShapeDtypeStruct((n, ch), F32),
                   jax.ShapeDtypeStruct((CONV_W - 1, n, ch), F32),
                   jax.ShapeDtypeStruct((n, 128), F32)],
        compiler_params=pltpu.CompilerParams(vmem_limit_bytes=V7X_VMEM_LIMIT),
        name="gdn_sample_pre",
    )(pm, pba, conv_t, cw_t, gp)


SEQ_PER_STEP = 4


def _seq_block(a, nb):
    nd = a.ndim
    return pl.BlockSpec((nb,) + a.shape[1:], lambda b: (b,) + (0,) * (nd - 1))


def _const_block(a):
    nd = a.ndim
    return pl.BlockSpec(a.shape, lambda b: (0,) * nd)


def _gdn_sample_step_kernel(s0_ref, cols_ref, v_ref, z_ref, sc_ref, nw_ref, *refs, heads, nb, n_prev):
    prev_refs = refs[:n_prev]
    s_out_all, o_ref = refs[n_prev:]
    for p, prev_ref in enumerate(prev_refs):
        s_out_all[p] = prev_ref[...]
    s_out_ref = s_out_all.at[n_prev] if n_prev else s_out_all
    hh = range(heads)
    for i in range(nb):
        kc = [cols_ref[i, :, h:h + 1] for h in hh]
        qc = [cols_ref[i, :, heads + h:heads + h + 1] for h in hh]
        s0 = [s0_ref[i, h] for h in hh]
        eg = [sc_ref[i, h:h + 1, 1:2] for h in hh]
        ks = [jnp.sum(kc[h] * s0[h], axis=0, keepdims=True) for h in hh]
        s1 = [eg[h] * s0[h] + kc[h] * (sc_ref[i, h:h + 1, 0:1] * (v_ref[i, h:h + 1, :] - eg[h] * ks[h]))
              for h in hh]
        o = [jnp.sum(qc[h] * s1[h], axis=0, keepdims=True) for h in hh]
        ms = [jnp.mean(o[h] * o[h], axis=-1, keepdims=True) for h in hh]
        for h in hh:
            s_out_ref[i, h] = s1[h]
            o_ref[i, h:h + 1, :] = o[h] * lax.rsqrt(ms[h] + RMS_EPS) * nw_ref[...] * _silu(z_ref[i, h:h + 1, :])


def _gdn_sample_step(s_all, layer, prev_new, cols, v, z, sc, nw):
    n_layers, n, heads, dk, dv = s_all.shape
    nb = SEQ_PER_STEP
    assert n % nb == 0
    last = layer == n_layers - 1
    n_prev = len(prev_new) if last else 0
    one = pl.BlockSpec((nb, heads, dk, dv), lambda b: (b, 0, 0, 0))
    if n_prev:
        out_state = pl.BlockSpec((n_layers, nb, heads, dk, dv), lambda b: (0, b, 0, 0, 0))
        out_shape = jax.ShapeDtypeStruct(s_all.shape, F32)
    else:
        out_state, out_shape = one, jax.ShapeDtypeStruct(s_all.shape[1:], F32)
    return pl.pallas_call(
        functools.partial(_gdn_sample_step_kernel, heads=heads, nb=nb, n_prev=n_prev),
        grid=(n // nb,),
        in_specs=[pl.BlockSpec((None, nb, heads, dk, dv), lambda b: (layer, b, 0, 0, 0)),
                  _seq_block(cols, nb), _seq_block(v, nb), _seq_block(z, nb), _seq_block(sc, nb),
                  pl.BlockSpec((1, dv), lambda b: (0, 0))] + [one] * n_prev,
        out_specs=[out_state, _seq_block(v, nb)],
        out_shape=[out_shape, jax.ShapeDtypeStruct(v.shape, F32)],
        compiler_params=_cparams(("parallel",)),
        name="gdn_sample_step",
    )(s_all, cols, v, z, sc, nw, *(prev_new if n_prev else []))


def _gdn_sample_layer(x, g, p, conv0, s_all, layer, prev_new, *, heads, dk, dv):
    n = x.shape[0]
    key, val = heads * dk, heads * dv
    ch = 2 * key + val
    pm, pba = _in_proj(x, g, p["w_main"], p["w_ba"])
    qkv, conv_t, sc = _gdn_sample_pre(pm, pba, jnp.transpose(conv0, (1, 0, 2)), p["cw_t"], p["gp"],
                                      heads=heads, dk=dk, dv=dv)
    q_c = jnp.transpose(qkv[:, :key].reshape(n, heads, dk), (0, 2, 1))
    k_c = jnp.transpose(qkv[:, key:2 * key].reshape(n, heads, dk), (0, 2, 1))
    cols = jnp.concatenate([k_c, q_c], axis=-1)
    sc3 = jnp.stack([sc[:, :heads], sc[:, heads:2 * heads]], axis=-1)
    s_new, o = _gdn_sample_step(s_all, layer, prev_new, cols, qkv[:, 2 * key:].reshape(n, heads, dv),
                                pm[:, ch:].reshape(n, heads, dv), sc3, p["nw"])
    return o.reshape(n, val), jnp.transpose(conv_t, (1, 0, 2)), s_new


def _mlstm_sample_step_kernel(c0_ref, n0_ref, cols_ref, q_ref, k_ref, v_ref, op_ref, sc_ref, bif_ref, nw_ref,
                              c_out_ref, n_out_ref, m_out_ref, o_ref, *, heads, dk, nb):
    scale = dk ** -0.5
    for i in range(nb):
        gi = sc_ref[i, :, 0:1] + bif_ref[:, 0:1]
        gf = sc_ref[i, :, 1:2] + bif_ref[:, 1:2]
        m0 = sc_ref[i, :, 2:3]
        gi = GATE_CAP * jnp.tanh(gi / GATE_CAP)
        logf = _log_sigmoid(GATE_CAP * jnp.tanh(gf / GATE_CAP))
        m_new = jnp.maximum(logf + m0, gi)
        f_s = jnp.exp(logf + m0 - m_new)
        i_s = jnp.exp(gi - m_new)
        m_out_ref[i] = m_new
        n1 = f_s * n0_ref[i] + i_s * k_ref[i]
        n_out_ref[i] = n1
        den = jnp.sum(q_ref[i] * scale * n1, axis=-1, keepdims=True)
        floor = jnp.exp(-m_new)
        hh = range(heads)
        kc = [cols_ref[i, :, h:h + 1] for h in hh]
        qc = [cols_ref[i, :, heads + h:heads + h + 1] * scale for h in hh]
        c1 = [f_s[h:h + 1, :] * c0_ref[i, h] + i_s[h:h + 1, :] * (kc[h] * v_ref[i, h:h + 1, :]) for h in hh]
        num = [jnp.sum(qc[h] * c1[h], axis=0, keepdims=True) for h in hh]
        h_t = [num[h] / jnp.maximum(jnp.abs(den[h:h + 1, :]), floor[h:h + 1, :]) for h in hh]
        ms = [jnp.mean(h_t[h] * h_t[h], axis=-1, keepdims=True) for h in hh]
        for h in hh:
            c_out_ref[i, h] = c1[h]
            h_n = h_t[h] * lax.rsqrt(ms[h] + RMS_EPS) * nw_ref[h:h + 1, :]
            o_ref[i, h:h + 1, :] = _sigmoid(op_ref[i, h:h + 1, :]) * h_n


def _mlstm_sample_step(c0, n0, cols, q, k, v, o_pre, sc, bif2, nw2):
    n, heads, dk, dv = c0.shape
    nb = SEQ_PER_STEP
    assert n % nb == 0
    full = _const_block

    def blk(a):
        return _seq_block(a, nb)

    m_shape = (n, heads, 1)
    return pl.pallas_call(
        functools.partial(_mlstm_sample_step_kernel, heads=heads, dk=dk, nb=nb),
        grid=(n // nb,),
        in_specs=[blk(c0), blk(n0), blk(cols), blk(q), blk(k), blk(v), blk(o_pre), blk(sc), full(bif2), full(nw2)],
        out_specs=[blk(c0), blk(n0), pl.BlockSpec((nb, heads, 1), lambda b: (b, 0, 0)), blk(v)],
        out_shape=[jax.ShapeDtypeStruct(c0.shape, F32), jax.ShapeDtypeStruct(n0.shape, F32),
                   jax.ShapeDtypeStruct(m_shape, F32), jax.ShapeDtypeStruct(v.shape, F32)],
        compiler_params=_cparams(("parallel",)),
        name="mlstm_sample_step",
    )(c0, n0, cols, q, k, v, o_pre, sc, bif2, nw2)


def _mlstm_sample_layer(x, g, p, c0, n0, m0, *, heads, dk, dv):
    n = x.shape[0]
    qk_w, val = heads * dk, heads * dv
    pm, pif = _in_proj(x, g, p["w_main"], p["w_if"])
    q = pm[:, :qk_w].reshape(n, heads, dk)
    k = pm[:, qk_w:2 * qk_w].reshape(n, heads, dk)
    v = pm[:, 2 * qk_w:2 * qk_w + val].reshape(n, heads, dv)
    o_pre = pm[:, 2 * qk_w + val:].reshape(n, heads, dv)
    cols = jnp.concatenate([jnp.transpose(k, (0, 2, 1)), jnp.transpose(q, (0, 2, 1))], axis=-1)
    sc = jnp.stack([pif[:, :heads], pif[:, heads:2 * heads], m0], axis=-1)
    bif2 = jnp.stack([p["bif"][0, :heads], p["bif"][0, heads:2 * heads]], axis=-1)
    c1, n1, m1, o = _mlstm_sample_step(c0, n0, cols, q, k, v, o_pre, sc, bif2, p["nw"].reshape(heads, dv))
    return o.reshape(n, val), c1, n1, m1[:, :, 0]


RW_ROW_GROUP = 8


def _rwkv_sample_step_kernel(s_ref, r_ref, k_ref, lw_ref, kk_ref, a_ref, v_ref, g_ref, hp_ref,
                             s_out_ref, o_ref, y_ref, *, hd):
    kk = kk_ref[0]
    kk = kk * lax.rsqrt(jnp.maximum(jnp.sum(kk * kk, axis=0, keepdims=True), 1e-24))
    av = -kk
    bv = kk * a_ref[0]
    w = jnp.exp(lw_ref[0])
    r = r_ref[0]
    k = k_ref[0]
    v = v_ref[0]
    for v0 in range(0, hd, RW_ROW_GROUP):
        vv = range(v0, v0 + RW_ROW_GROUP)
        s0 = [s_ref[0, i] for i in vv]
        sa = [jnp.sum(s * av, axis=0, keepdims=True) for s in s0]
        s1 = [s * w + sa_i * bv + v[i:i + 1, :] * k for s, sa_i, i in zip(s0, sa, vv)]
        y = [jnp.sum(s * r, axis=0, keepdims=True) for s in s1]
        for i, s, y_i in zip(vv, s1, y):
            s_out_ref[0, i] = s
            y_ref[i:i + 1, :] = y_i
    y = y_ref[...]
    yc = y - jnp.mean(y, axis=0, keepdims=True)
    var = jnp.mean(yc * yc, axis=0, keepdims=True)
    hp = hp_ref[0]
    yn = yc * lax.rsqrt(var + RW_GN_EPS) * hp[:, 1:2] + hp[:, 2:3]
    bonus = jnp.sum(r * k * hp[:, 0:1], axis=0, keepdims=True)
    o_ref[0] = (yn + bonus * v) * g_ref[0]


def _rwkv_sample_step(s_t, r, k, lw, kk, a, v, g, hp3):
    heads, hd, _, n = s_t.shape

    def blk(z):
        nd = z.ndim
        return pl.BlockSpec((1,) + z.shape[1:], lambda h: (h,) + (0,) * (nd - 1))

    return pl.pallas_call(
        functools.partial(_rwkv_sample_step_kernel, hd=hd),
        grid=(heads,),
        in_specs=[blk(s_t)] + [blk(z) for z in (r, k, lw, kk, a, v, g, hp3)],
        out_specs=[blk(s_t), blk(v)],
        out_shape=[jax.ShapeDtypeStruct(s_t.shape, F32), jax.ShapeDtypeStruct(v.shape, F32)],
        scratch_shapes=[pltpu.VMEM((hd, n), F32)],
        compiler_params=_cparams(("parallel",)),
        name="rwkv_sample_step",
    )(s_t, r, k, lw, kk, a, v, g, hp3)


def _rwkv_sample_layer(x, g_norm, p, shift0, s0, *, heads, hd):
    n, d = x.shape
    (r, k, v, lw, kk, a, g), hn = _rwkv_proj(x, g_norm, shift0, p, batch=n, seq=1, tm=n)

    def lanes(z):
        return z.T.reshape(heads, hd, n)

    hp = p["hp"]
    hp3 = jnp.stack([hp[j].reshape(heads, hd) for j in range(3)], axis=-1)
    s1_t, o_t = _rwkv_sample_step(jnp.transpose(s0, (1, 2, 3, 0)), lanes(r), lanes(k), lanes(lw), lanes(kk),
                                  lanes(a), lanes(v), lanes(g), hp3)
    return o_t.reshape(d, n).T, hn, jnp.transpose(s1_t, (3, 0, 1, 2))


def _pad_cols(a, n):
    return jnp.pad(a, ((0, 0), (0, n - a.shape[1])))


def _gdn_prep(w_in, conv_w, a_log, dt_bias, norm_w, w_out, *, heads, dk, dv):
    key, val = heads * dk, heads * dv
    ch = 2 * key + val
    main = ch + val
    gp = jnp.zeros((8, 128), F32)
    gp = gp.at[0, heads:2 * heads].set(a_log).at[1, heads:2 * heads].set(dt_bias)
    return dict(w_main=w_in[:, :main].astype(BF16),
                w_ba=_pad_cols(w_in[:, main:], 128).astype(BF16),
                cw_t=jnp.pad(conv_w.T, ((0, 8 - CONV_W), (0, 0))),
                gp=gp, nw=norm_w[None, :], w_out=w_out.astype(BF16))


def _gdn_prompt_layer(x, g, p, conv0, s0, *, batch, seq, heads, dk, dv):
    conv0 = jnp.pad(conv0, ((0, 0), (8 - (CONV_W - 1), 0), (0, 0)))
    qkvz, pba, conv = _gdn_in_proj(x, g, p["w_main"], p["w_ba"], conv0, p["cw_t"], batch=batch, seq=seq,
                                   heads=heads, dk=dk, dv=dv, tm=TM_GDN_PROJ)
    o, s = _gdn_prompt(qkvz, pba, s0, p["gp"], p["nw"], batch=batch, seq=seq, heads=heads, dk=dk, dv=dv, tb=TB_MIX)
    return o, conv[:, 8 - (CONV_W - 1):, :], s


def _trunk(x, states, w, *, batch, seq):
    conv_in, gs_in, c_in, n_in, m_in, shift_in, rs_in = states
    depth = w["norm_mix"].shape[0]
    gh, gdk, gdv = gs_in.shape[2:]
    mh, mdk, mdv = c_in.shape[2:]
    rh, rhd = rs_in.shape[2:4]
    prompt = seq > 1
    outs = [[] for _ in range(7)]
    gs_new = []
    for i in range(depth):
        j = i // 3
        g = w["norm_mix"][i][None, :]
        if i % 3 == 0:
            p = w["gdn"][j]
            if prompt:
                mix, cb, s = _gdn_prompt_layer(x, g, p, conv_in[j], gs_in[j], batch=batch, seq=seq,
                                               heads=gh, dk=gdk, dv=gdv)
                outs[1].append(s)
            else:
                mix, cb, s = _gdn_sample_layer(x, g, p, conv_in[j], gs_in, j, gs_new, heads=gh, dk=gdk, dv=gdv)
                gs_new.append(s)
            outs[0].append(cb)
            w_out = p["w_out"]
        elif i % 3 == 1:
            p = w["ml"][j]
            if prompt:
                mix, c, n, m = _mlstm_prompt_layer(x, g, p, c_in[j], n_in[j], m_in[j], batch=batch, seq=seq,
                                                   heads=mh, dk=mdk, dv=mdv)
            else:
                mix, c, n, m = _mlstm_sample_layer(x, g, p, c_in[j], n_in[j], m_in[j], heads=mh, dk=mdk, dv=mdv)
            w_out = p["w_out"]
            outs[2].append(c)
            outs[3].append(n)
            outs[4].append(m)
        else:
            p = w["rw"][j]
            if prompt:
                mix, sh, s = _rwkv_prompt_layer(x, g, p, shift_in[j], rs_in[j], batch=batch, seq=seq,
                                                heads=rh, hd=rhd)
            else:
                mix, sh, s = _rwkv_sample_layer(x, g, p, shift_in[j], rs_in[j], heads=rh, hd=rhd)
            w_out = p["w_o"]
            outs[5].append(sh)
            outs[6].append(s)
        g_out = w["norm_final"][None, :] if i == depth - 1 else None
        x = _ffn(x, mix, w_out, w["norm_ffn"][i][None, :], w["ffn_w1"], w["ffn_w2"], i, g_out,
                 tm=TM_FFN, tf=TF_FFN)
    y = x
    new = [jnp.stack(z, axis=0) if z else None for z in outs]
    if not prompt:
        new[1] = gs_new[-1] if len(gs_new) > 1 else gs_new[0][None]
    return y, tuple(new)


def kernel(x_prompt, x_sample, state_gdn_conv, state_gdn_S, state_mlstm_C, state_mlstm_n, state_mlstm_m, state_rwkv_shift, state_rwkv_S, norm_mix, norm_ffn, norm_final, gdn_w_in, gdn_conv_w, gdn_a_log, gdn_dt_bias, gdn_norm_w, gdn_w_out, ml_w_in, ml_b_if, ml_norm_w, ml_w_out, rw_mu, rw_w_rkv, rw_w_o, rw_w0, rw_w1, rw_w2, rw_a0, rw_a1, rw_a2, rw_g1, rw_g2, rw_k_k, rw_k_a, rw_r_k, rw_lnx_w, rw_lnx_b, ffn_w1, ffn_w2):
    gh, gdk, gdv = state_gdn_S.shape[2:]
    mh, mdk, mdv = state_mlstm_C.shape[2:]
    w = dict(
        norm_mix=norm_mix, norm_ffn=norm_ffn, norm_final=norm_final,
        ffn_w1=ffn_w1.astype(BF16), ffn_w2=ffn_w2.astype(BF16),
        gdn=[_gdn_prep(gdn_w_in[j], gdn_conv_w[j], gdn_a_log[j], gdn_dt_bias[j], gdn_norm_w[j], gdn_w_out[j],
                       heads=gh, dk=gdk, dv=gdv) for j in range(gdn_w_in.shape[0])],
        ml=[_mlstm_prep(ml_w_in[j], ml_b_if[j], ml_norm_w[j], ml_w_out[j], heads=mh, dk=mdk, dv=mdv)
            for j in range(ml_w_in.shape[0])],
        rw=[_rwkv_prep(rw_mu[j], rw_w_rkv[j], rw_w_o[j], rw_w0[j], rw_w1[j], rw_w2[j], rw_a0[j], rw_a1[j],
                       rw_a2[j], rw_g1[j], rw_g2[j], rw_k_k[j], rw_k_a[j], rw_r_k[j], rw_lnx_w[j], rw_lnx_b[j])
            for j in range(rw_mu.shape[0])])
    sample_states = (state_gdn_conv, state_gdn_S, state_mlstm_C, state_mlstm_n, state_mlstm_m,
                     state_rwkv_shift, state_rwkv_S)
    bp, tp, d = x_prompt.shape
    bs, ts, _ = x_sample.shape
    assert ts == 1
    prompt_states = tuple(jnp.zeros((s.shape[0], bp) + s.shape[2:], s.dtype) for s in sample_states)
    y_p, new_p = _trunk(x_prompt.reshape(bp * tp, d), prompt_states, w, batch=bp, seq=tp)
    y_s, new_s = _trunk(x_sample.reshape(bs * ts, d), sample_states, w, batch=bs, seq=ts)
    out = [y_p.reshape(bp, tp, d), y_s.reshape(bs, ts, d)]
    for a, b in zip(new_p, new_s):
        out += [a, b]
    return tuple(out)
```

```python
import functools

import jax
import jax.numpy as jnp
from jax import lax
from jax.experimental import pallas as pl
from jax.experimental.pallas import tpu as pltpu

F32 = jnp.float32
BF16 = jnp.bfloat16

RMS_EPS = 1e-6
NEG_BIG = -1e30
GATE_CAP = 15.0
RW_GN_EPS = 64e-5
CONV_W = 4
CHUNK = 64
V7X_VMEM_LIMIT = 56 * 1024 * 1024
HI = lax.Precision.HIGHEST


def _cparams(sem):
    return pltpu.CompilerParams(dimension_semantics=sem, vmem_limit_bytes=V7X_VMEM_LIMIT)


def _dot(a, b):
    return jnp.dot(a.astype(BF16), b.astype(BF16), preferred_element_type=F32)


def _dot_nt(a, b):
    return lax.dot_general(a.astype(BF16), b.astype(BF16), (((1,), (1,)), ((), ())),
                           preferred_element_type=F32)


def _dot_tn(a, b):
    return lax.dot_general(a.astype(BF16), b.astype(BF16), (((0,), (0,)), ((), ())),
                           preferred_element_type=F32)


def _dot_hi(a, b):
    return jnp.dot(a, b, preferred_element_type=F32, precision=HI)


def _sigmoid(x):
    return 1.0 / (1.0 + jnp.exp(-x))


def _silu(x):
    return x * _sigmoid(x)


def _softplus(x):
    return jnp.maximum(x, 0.0) + jnp.log(1.0 + jnp.exp(-jnp.abs(x)))


def _log_sigmoid(x):
    return -_softplus(-x)


def _tri_masks(l):
    r = lax.broadcasted_iota(jnp.int32, (l, l), 0)
    c = lax.broadcasted_iota(jnp.int32, (l, l), 1)
    return r >= c, r > c


INV_BASE = 16


def _unit_lower_inverse(mats, l):
    n = mats[0].shape[0]
    r = lax.broadcasted_iota(jnp.int32, (n, n), 0)
    c = lax.broadcasted_iota(jnp.int32, (n, n), 1)
    eye = (r == c).astype(F32)
    size = min(INV_BASE, l)
    shift = size.bit_length() - 1
    diag = (r >> shift) == (c >> shift)
    merges = []
    s = size
    while s < l:
        sh = s.bit_length() - 1
        off = ((r >> (sh + 1)) == (c >> (sh + 1))) & ((r >> sh) > (c >> sh))
        merges.append([jnp.where(off, a, 0.0).astype(BF16) for a in mats])
        s *= 2
    t = [eye - jnp.where(diag, a, 0.0) for a in mats]
    tb = [ti.astype(BF16) for ti in t]
    ab = [jnp.where(diag, a, 0.0).astype(BF16) for a in mats]
    p = [_dot(a, a).astype(BF16) for a in ab]
    k = 2
    while k < size:
        if 2 * k < size:
            both = [_dot(pi, jnp.concatenate([ti, pi], axis=1)) for ti, pi in zip(tb, p)]
            t = [ti + bi[:, :n] for ti, bi in zip(t, both)]
            p = [bi[:, n:].astype(BF16) for bi in both]
        else:
            t = [ti + _dot(pi, ti_b) for ti, ti_b, pi in zip(t, tb, p)]
        tb = [ti.astype(BF16) for ti in t]
        k *= 2
    for a_off in merges:
        x = [_dot(a, ti) for a, ti in zip(a_off, tb)]
        y = [_dot(ti, xi) for ti, xi in zip(tb, x)]
        t = [ti - yi for ti, yi in zip(t, y)]
        tb = [ti.astype(BF16) for ti in t]
    return tb


TM_IN_PROJ = 512
TM_GDN_PROJ = 512
TM_FFN, TF_FFN = 1024, 1024
TB_MIX = 256
TB_GDN = 512
TB_MLSTM = 512
TM_RWKV_PROJ = 512


def _norm_rows(x, g):
    return x * lax.rsqrt(jnp.mean(x * x, axis=-1, keepdims=True) + RMS_EPS) * g


def _in_proj_kernel(x_ref, g_ref, w_ref, wa_ref, o_ref, oa_ref):
    xn = _norm_rows(x_ref[...], g_ref[...]).astype(BF16)
    oa_ref[...] = jnp.dot(xn, wa_ref[...], preferred_element_type=F32)
    o_ref[...] = jnp.dot(xn, w_ref[...], preferred_element_type=F32)


def _in_proj(x, g, w_main, w_aux):
    m, k = x.shape
    n, na = w_main.shape[1], w_aux.shape[1]
    tm = min(TM_IN_PROJ, m)
    assert m % tm == 0
    return pl.pallas_call(
        _in_proj_kernel,
        grid=(m // tm,),
        in_specs=[pl.BlockSpec((tm, k), lambda i: (i, 0)), pl.BlockSpec((1, k), lambda i: (0, 0)),
                  pl.BlockSpec((k, n), lambda i: (0, 0)), pl.BlockSpec((k, na), lambda i: (0, 0))],
        out_specs=[pl.BlockSpec((tm, n), lambda i: (i, 0)), pl.BlockSpec((tm, na), lambda i: (i, 0))],
        out_shape=[jax.ShapeDtypeStruct((m, n), F32), jax.ShapeDtypeStruct((m, na), F32)],
        compiler_params=_cparams(("parallel",)),
        name="in_proj",
    )(x, g, w_main, w_aux)


def _ffn_kernel(res_ref, mix_ref, wo_ref, g_ref, w1_ref, w2_ref, go_ref, o_ref, xn_ref, acc_ref, *, out_norm):
    f = pl.program_id(1)

    @pl.when(f == 0)
    def _():
        x = res_ref[...] + jnp.dot(mix_ref[...].astype(BF16), wo_ref[...], preferred_element_type=F32)
        o_ref[...] = x
        xn_ref[...] = _norm_rows(x, g_ref[...]).astype(BF16)
        acc_ref[...] = jnp.zeros_like(acc_ref)

    h = jnp.dot(xn_ref[...], w1_ref[...], preferred_element_type=F32)
    a = jnp.square(jnp.maximum(h, 0.0)).astype(BF16)
    acc_ref[...] += jnp.dot(a, w2_ref[...], preferred_element_type=F32)

    @pl.when(f == pl.num_programs(1) - 1)
    def _():
        y = o_ref[...] + acc_ref[...]
        o_ref[...] = _norm_rows(y, go_ref[...]) if out_norm else y


def _ffn(res, mix, w_out, g, w1, w2, layer, g_out=None, *, tm, tf):
    m, d = res.shape
    dff = w1.shape[2]
    tm, tf = min(tm, m), min(tf, dff)
    assert m % tm == 0 and dff % tf == 0
    vec = pl.BlockSpec((1, d), lambda i, j: (0, 0))
    return pl.pallas_call(
        functools.partial(_ffn_kernel, out_norm=g_out is not None),
        grid=(m // tm, dff // tf),
        in_specs=[pl.BlockSpec((tm, d), lambda i, j: (i, 0)),
                  pl.BlockSpec((tm, mix.shape[1]), lambda i, j: (i, 0)),
                  pl.BlockSpec(w_out.shape, lambda i, j: (0, 0)), vec,
                  pl.BlockSpec((None, d, tf), lambda i, j: (layer, 0, j)),
                  pl.BlockSpec((None, tf, d), lambda i, j: (layer, j, 0)), vec],
        out_specs=pl.BlockSpec((tm, d), lambda i, j: (i, 0)),
        out_shape=jax.ShapeDtypeStruct((m, d), F32),
        scratch_shapes=[pltpu.VMEM((tm, d), BF16), pltpu.VMEM((tm, d), F32)],
        compiler_params=_cparams(("parallel", "arbitrary")),
        name="ffn",
    )(res, mix, w_out, g, w1, w2, g if g_out is None else g_out)


def _gdn_prompt_kernel(qkv_ref, pba_ref, s0_ref, gp_ref, nw_ref,
                       o_ref, s_out_ref,
                       s_ref, u_ref, w_ref, qd_ref, kd_ref, qk_ref, gl_ref,
                       *, tb, heads, dk, dv):
    t = pl.program_id(1)
    key = heads * dk
    ch = 2 * key + heads * dv
    l = CHUNK

    @pl.when(t == 0)
    def _():
        s_ref[...] = s0_ref[0]

    a_log = gp_ref[0:1, :]
    dt_bias = gp_ref[1:2, :]
    hh = range(heads)
    l2 = 2 * l
    r2 = lax.broadcasted_iota(jnp.int32, (l2, l2), 0)
    c2 = lax.broadcasted_iota(jnp.int32, (l2, l2), 1)
    same = (r2 >= l) == (c2 >= l)
    incl = same & (r2 >= c2)
    strict = same & (r2 > c2)
    tril = incl.astype(F32)
    first = lax.broadcasted_iota(jnp.int32, (l2, 128), 0) < l

    pairs = range(tb // l2)
    rows = [slice(pi * l2, (pi + 1) * l2) for pi in pairs]
    beta_all, gc, gc_t, g_end = [], [], [], []
    for pi in pairs:
        ba = pba_ref[rows[pi], :]
        beta_all.append(_sigmoid(ba))
        g_all = -jnp.exp(a_log) * _softplus(ba + dt_bias)
        gc.append(_dot_hi(tril, g_all))
        gc_t.append(gc[pi].T)
        g_end.append(jnp.where(first, gc[pi][l - 1:l, :], gc[pi][l2 - 1:l2, :]))
        gl_ref[2 * pi:2 * pi + 1, :] = jnp.exp(gc[pi][l - 1:l, :])
        gl_ref[2 * pi + 1:2 * pi + 2, :] = jnp.exp(gc[pi][l2 - 1:l2, :])
    cc = [(pi, h) for pi in pairs for h in hh]
    b_col = [beta_all[pi][:, h:h + 1] for pi, h in cc]
    gi = [gc[pi][:, heads + h:heads + h + 1] for pi, h in cc]
    q = [qkv_ref[rows[pi], h * dk:(h + 1) * dk] for pi, h in cc]
    k = [qkv_ref[rows[pi], key + h * dk:key + (h + 1) * dk] for pi, h in cc]
    v = [qkv_ref[rows[pi], 2 * key + h * dv:2 * key + (h + 1) * dv] for pi, h in cc]
    nc = range(len(cc))
    dmat = [jnp.where(incl, jnp.exp(jnp.where(incl, gi[i] - gc_t[pi][heads + h:heads + h + 1, :], 0.0)), 0.0)
            for i, (pi, h) in enumerate(cc)]
    kb = [k[i] * b_col[i] for i in nc]
    kk = [_dot_nt(kb[i], k[i]) for i in nc]
    qk = [_dot_nt(q[i], k[i]) for i in nc]
    t_inv = _unit_lower_inverse([jnp.where(strict, kk[i] * dmat[i], 0.0) for i in nc], l)
    egi = [jnp.exp(gi[i]) for i in nc]
    sol = [_dot(t_inv[i], jnp.concatenate([v[i] * b_col[i], kb[i] * egi[i]], axis=-1)) for i in nc]
    for i, (pi, h) in enumerate(cc):
        hs = slice(h * dk, (h + 1) * dk)
        u_ref[rows[pi], h * dv:(h + 1) * dv] = sol[i][:, :dv]
        w_ref[rows[pi], hs] = sol[i][:, dv:].astype(BF16)
        qd_ref[rows[pi], hs] = (q[i] * egi[i]).astype(BF16)
        kd_ref[rows[pi], hs] = (k[i] * jnp.exp(g_end[pi][:, heads + h:heads + h + 1] - gi[i])).astype(BF16)
        qkm = jnp.where(incl, qk[i] * dmat[i], 0.0).astype(BF16)
        qk_ref[h, pi * l2:pi * l2 + l, :] = qkm[:l, :l]
        qk_ref[h, pi * l2 + l:(pi + 1) * l2, :] = qkm[l:, l:]

    for ci in range(tb // l):
        rows = slice(ci * l, (ci + 1) * l)
        s = [s_ref[h] for h in hh]
        wq = [_dot(jnp.concatenate([w_ref[rows, h * dk:(h + 1) * dk], qd_ref[rows, h * dk:(h + 1) * dk]], axis=0),
                   s[h]) for h in hh]
        v_new = [u_ref[rows, h * dv:(h + 1) * dv] - wq[h][:l] for h in hh]
        o2 = [_dot(qk_ref[h, rows, :], v_new[h]) for h in hh]
        ds = [_dot_tn(kd_ref[rows, h * dk:(h + 1) * dk], v_new[h]) for h in hh]
        for h in hh:
            s_ref[h] = s[h] * gl_ref[ci:ci + 1, heads + h:heads + h + 1] + ds[h]
            o = wq[h][l:] + o2[h]
            z = qkv_ref[rows, ch + h * dv:ch + (h + 1) * dv]
            o = o * lax.rsqrt(jnp.mean(o * o, axis=-1, keepdims=True) + RMS_EPS) * nw_ref[...]
            o_ref[rows, h * dv:(h + 1) * dv] = o * _silu(z)

    @pl.when(t == pl.num_programs(1) - 1)
    def _():
        s_out_ref[0] = s_ref[...]


def _gdn_prompt(qkvz, pba, s0, gp, nw, *, batch, seq, heads, dk, dv, tb):
    key, val = heads * dk, heads * dv
    ch = 2 * key + val
    tb = min(tb, seq)
    assert seq % tb == 0 and tb % (2 * CHUNK) == 0
    nt = seq // tb
    return pl.pallas_call(
        functools.partial(_gdn_prompt_kernel, tb=tb, heads=heads, dk=dk, dv=dv),
        grid=(batch, nt),
        in_specs=[pl.BlockSpec((tb, ch + val), lambda b, t: (b * nt + t, 0)),
                  pl.BlockSpec((tb, 128), lambda b, t: (b * nt + t, 0)),
                  pl.BlockSpec((1, heads, dk, dv), lambda b, t: (b, 0, 0, 0)),
                  pl.BlockSpec((8, 128), lambda b, t: (0, 0)),
                  pl.BlockSpec((1, dv), lambda b, t: (0, 0))],
        out_specs=[pl.BlockSpec((tb, val), lambda b, t: (b * nt + t, 0)),
                   pl.BlockSpec((1, heads, dk, dv), lambda b, t: (b, 0, 0, 0))],
        out_shape=[jax.ShapeDtypeStruct((batch * seq, val), F32),
                   jax.ShapeDtypeStruct((batch, heads, dk, dv), F32)],
        scratch_shapes=[pltpu.VMEM((heads, dk, dv), F32),
                        pltpu.VMEM((tb, val), F32), pltpu.VMEM((tb, key), BF16),
                        pltpu.VMEM((tb, key), BF16), pltpu.VMEM((tb, key), BF16),
                        pltpu.VMEM((heads, tb, CHUNK), BF16), pltpu.VMEM((max(8, tb // CHUNK), 128), F32)],
        compiler_params=_cparams(("parallel", "arbitrary")),
        name="gdn_prompt",
    )(qkvz, pba, s0, gp, nw)


GDN_PROJ_COLS = 256
GDN_CONV_ROWS = 64


def _gdn_in_proj_kernel(x_ref, g_ref, w_ref, wa_ref, conv0_ref, cw_ref, o_ref, aux_ref, conv_out_ref,
                        xn_ref, carry_ref, *, heads, dk, dv):
    t = pl.program_id(1)
    key = heads * dk
    ch = 2 * key + heads * dv
    tm = x_ref.shape[0]
    n = w_ref.shape[1]

    @pl.when(t == 0)
    def _():
        carry_ref[...] = conv0_ref[0]

    xn_ref[...] = _norm_rows(x_ref[...], g_ref[...]).astype(BF16)
    aux_ref[...] = jnp.dot(xn_ref[...], wa_ref[...], preferred_element_type=F32)
    for c0 in range(0, n, GDN_PROJ_COLS):
        cs = slice(c0, c0 + GDN_PROJ_COLS)
        y = jnp.dot(xn_ref[...], w_ref[:, cs], preferred_element_type=F32)
        if c0 >= ch:
            o_ref[:, cs] = y
            continue
        ext = jnp.concatenate([carry_ref[:, cs], y], axis=0)
        carry_ref[:, cs] = y[tm - 8:, :]
        for r0 in range(0, tm, GDN_CONV_ROWS):
            for j in range(0, GDN_PROJ_COLS, dk):
                cj = slice(c0 + j, c0 + j + dk)
                blk = ext[r0:r0 + GDN_CONV_ROWS + 8, j:j + dk]
                z = blk[8:] * cw_ref[CONV_W - 1:CONV_W, cj]
                for s in range(1, CONV_W):
                    z = z + pltpu.roll(blk, s, 0)[8:] * cw_ref[CONV_W - 1 - s:CONV_W - s, cj]
                z = _silu(z)
                if c0 + j < key:
                    z = z * lax.rsqrt(jnp.sum(z * z, axis=-1, keepdims=True) + 1e-6) * (dk ** -0.5)
                elif c0 + j < 2 * key:
                    z = z * lax.rsqrt(jnp.sum(z * z, axis=-1, keepdims=True) + 1e-6)
                o_ref[r0:r0 + GDN_CONV_ROWS, cj] = z

    @pl.when(t == pl.num_programs(1) - 1)
    def _():
        conv_out_ref[0] = carry_ref[...]


def _gdn_in_proj(x, g, w_main, w_ba, conv0, cw_t, *, batch, seq, heads, dk, dv, tm):
    m, d = x.shape
    key, val = heads * dk, heads * dv
    ch = 2 * key + val
    n = w_main.shape[1]
    assert dk == dv and n % GDN_PROJ_COLS == 0 and ch % GDN_PROJ_COLS == 0 and GDN_PROJ_COLS % dk == 0
    tm = min(tm, seq)
    assert seq % tm == 0 and tm >= 8
    nt = seq // tm
    row = lambda width: pl.BlockSpec((tm, width), lambda b, t: (b * nt + t, 0))
    full = lambda a: pl.BlockSpec(a.shape, lambda b, t: (0,) * a.ndim)
    st = pl.BlockSpec((1, 8, ch), lambda b, t: (b, 0, 0))
    return pl.pallas_call(
        functools.partial(_gdn_in_proj_kernel, heads=heads, dk=dk, dv=dv),
        grid=(batch, nt),
        in_specs=[row(d), full(g), full(w_main), full(w_ba), st, full(cw_t)],
        out_specs=[row(n), row(w_ba.shape[1]), st],
        out_shape=[jax.ShapeDtypeStruct((m, n), F32), jax.ShapeDtypeStruct((m, w_ba.shape[1]), F32),
                   jax.ShapeDtypeStruct((batch, 8, ch), F32)],
        scratch_shapes=[pltpu.VMEM((tm, d), BF16), pltpu.VMEM((8, ch), F32)],
        compiler_params=_cparams(("parallel", "arbitrary")),
        name="gdn_in_proj",
    )(x, g, w_main, w_ba, conv0, cw_t)


def _mlstm_prompt_kernel(pm_ref, pif_ref, bif_ref, c0_ref, n0_ref, m0_ref, nw_ref,
                         o_ref, c_out_ref, n_out_ref, m_out_ref,
                         c_ref, n_ref, m_ref, bc_ref, ni_ref, col_ref, kv_ref, kc_ref, sc_ref,
                         cin_ref, nin_ref, min_ref, *, tb, heads, dk, dv):
    t = pl.program_id(1)
    l = CHUNK
    qk_w = heads * dk
    v_off = 2 * qk_w
    o_off = v_off + heads * dv

    @pl.when(t == 0)
    def _():
        c_ref[...] = c0_ref[0]
        n_ref[...] = n0_ref[0]
        m_ref[...] = m0_ref[0]

    incl, _ = _tri_masks(l)
    tril = incl.astype(F32)
    hh = range(heads)
    nchunk = tb // l
    scale = dk ** -0.5

    crow = [slice(ci * l, (ci + 1) * l) for ci in range(nchunk)]
    gates, bcum, bcum_t, gates_t = [], [], [], []
    for ci in range(nchunk):
        g = pif_ref[crow[ci], :] + bif_ref[...]
        g = GATE_CAP * jnp.tanh(g / GATE_CAP)
        gates.append(g)
        bcum.append(_dot_hi(tril, _log_sigmoid(g)))
        bcum_t.append(bcum[ci].T)
        gates_t.append(g.T)
        bc_ref[crow[ci], :] = bcum[ci]
    cc = [(ci, h) for ci in range(nchunk) for h in hh]
    nc = range(len(cc))
    bi = [bcum[ci][:, heads + h:heads + h + 1] for ci, h in cc]
    b_last = [bcum[ci][l - 1:l, heads + h:heads + h + 1] for ci, h in cc]
    k = [pm_ref[crow[ci], qk_w + h * dk:qk_w + (h + 1) * dk] for ci, h in cc]
    v = [pm_ref[crow[ci], v_off + h * dv:v_off + (h + 1) * dv].astype(BF16) for ci, h in cc]
    qk = [_dot_nt(pm_ref[crow[ci], h * dk:(h + 1) * dk] * scale, k[i]) for i, (ci, h) in enumerate(cc)]
    dlog = [jnp.where(incl, bi[i] - bcum_t[ci][heads + h:heads + h + 1, :] + gates_t[ci][h:h + 1, :], NEG_BIG)
            for i, (ci, h) in enumerate(cc)]
    m_intra = [jnp.max(dlog[i], axis=-1, keepdims=True) for i in nc]
    p = [jnp.where(incl, jnp.exp(dlog[i] - m_intra[i]), 0.0) * qk[i] for i in nc]
    den_intra = [jnp.sum(p[i], axis=-1, keepdims=True) for i in nc]
    num_intra = [_dot(p[i], v[i]) for i in nc]
    a_log = [b_last[i] - bi[i] + gates[ci][:, h:h + 1] for i, (ci, h) in enumerate(cc)]
    m_chunk = [jnp.max(a_log[i], axis=0, keepdims=True) for i in nc]
    kw = [k[i] * jnp.exp(a_log[i] - m_chunk[i]) for i in nc]
    kv_chunk = [_dot_tn(kw[i], v[i]) for i in nc]
    for i, (ci, h) in enumerate(cc):
        ni_ref[crow[ci], h * dv:(h + 1) * dv] = num_intra[i]
        col_ref[crow[ci], h:h + 1] = m_intra[i]
        col_ref[crow[ci], heads + h:heads + h + 1] = den_intra[i]
        kv_ref[i] = kv_chunk[i]
        kc_ref[i:i + 1, :] = jnp.sum(kw[i], axis=0, keepdims=True)
        sc_ref[i:i + 1, 0:1] = m_chunk[i]
        sc_ref[i:i + 1, 1:2] = b_last[i]

    for ci in range(nchunk):
        for h in hh:
            i = ci * heads + h
            c_mat = c_ref[h]
            n_vec = n_ref[h:h + 1, :]
            m_prev = m_ref[h:h + 1, :]
            cin_ref[i] = c_mat
            nin_ref[i:i + 1, :] = n_vec
            min_ref[i:i + 1, :] = m_prev
            m_chunk = sc_ref[i:i + 1, 0:1]
            b_last = sc_ref[i:i + 1, 1:2]
            m_new = jnp.maximum(b_last + m_prev, m_chunk)
            f_s = jnp.exp(b_last + m_prev - m_new)
            i_s = jnp.exp(m_chunk - m_new)
            c_ref[h] = f_s[:, 0:1] * c_mat + i_s[:, 0:1] * kv_ref[i]
            n_ref[h:h + 1, :] = f_s * n_vec + i_s * kc_ref[i:i + 1, :]
            m_ref[h:h + 1, :] = m_new

    q = [pm_ref[crow[ci], h * dk:(h + 1) * dk] * scale for ci, h in cc]
    qc = [_dot(q[i], cin_ref[i]) for i in nc]
    qn = [jnp.sum(q[i] * nin_ref[i:i + 1, :], axis=-1, keepdims=True) for i in nc]
    m_prev = [min_ref[i:i + 1, 0:1] for i in nc]
    bi = [bc_ref[crow[ci], heads + h:heads + h + 1] for ci, h in cc]
    m_in = [col_ref[crow[ci], h:h + 1] for ci, h in cc]
    m_t = [jnp.maximum(bi[i] + m_prev[i], m_in[i]) for i in nc]
    s_inter = [jnp.exp(bi[i] + m_prev[i] - m_t[i]) for i in nc]
    s_intra = [jnp.exp(m_in[i] - m_t[i]) for i in nc]
    den = [s_inter[i] * qn[i] + s_intra[i] * col_ref[crow[ci], heads + h:heads + h + 1]
           for i, (ci, h) in enumerate(cc)]
    h_t = [(s_inter[i] * qc[i] + s_intra[i] * ni_ref[crow[ci], h * dv:(h + 1) * dv])
           / jnp.maximum(jnp.abs(den[i]), jnp.exp(-m_t[i])) for i, (ci, h) in enumerate(cc)]
    ms = [jnp.mean(h_t[i] * h_t[i], axis=-1, keepdims=True) for i in nc]
    for i, (ci, h) in enumerate(cc):
        h_n = h_t[i] * lax.rsqrt(ms[i] + RMS_EPS) * nw_ref[:, h * dv:(h + 1) * dv]
        o_pre = pm_ref[crow[ci], o_off + h * dv:o_off + (h + 1) * dv]
        o_ref[crow[ci], h * dv:(h + 1) * dv] = _sigmoid(o_pre) * h_n

    @pl.when(t == pl.num_programs(1) - 1)
    def _():
        c_out_ref[0] = c_ref[...]
        n_out_ref[0] = n_ref[...]
        m_out_ref[0] = m_ref[...]


def _mlstm_prompt(pm, pif, bif, c0, n0, m0, nw, *, batch, seq, heads, dk, dv, tb):
    width = pm.shape[1]
    val = heads * dv
    tb = min(tb, seq)
    assert seq % tb == 0 and tb % CHUNK == 0
    nt = seq // tb
    nck = (tb // CHUNK) * heads
    return pl.pallas_call(
        functools.partial(_mlstm_prompt_kernel, tb=tb, heads=heads, dk=dk, dv=dv),
        grid=(batch, nt),
        in_specs=[pl.BlockSpec((tb, width), lambda b, t: (b * nt + t, 0)),
                  pl.BlockSpec((tb, 128), lambda b, t: (b * nt + t, 0)),
                  pl.BlockSpec((1, 128), lambda b, t: (0, 0)),
                  pl.BlockSpec((1, heads, dk, dv), lambda b, t: (b, 0, 0, 0)),
                  pl.BlockSpec((1, 8, dk), lambda b, t: (b, 0, 0)),
                  pl.BlockSpec((1, 8, 128), lambda b, t: (b, 0, 0)),
                  pl.BlockSpec((1, val), lambda b, t: (0, 0))],
        out_specs=[pl.BlockSpec((tb, val), lambda b, t: (b * nt + t, 0)),
                   pl.BlockSpec((1, heads, dk, dv), lambda b, t: (b, 0, 0, 0)),
                   pl.BlockSpec((1, 8, dk), lambda b, t: (b, 0, 0)),
                   pl.BlockSpec((1, 8, 128), lambda b, t: (b, 0, 0))],
        out_shape=[jax.ShapeDtypeStruct((batch * seq, val), F32),
                   jax.ShapeDtypeStruct((batch, heads, dk, dv), F32),
                   jax.ShapeDtypeStruct((batch, 8, dk), F32),
                   jax.ShapeDtypeStruct((batch, 8, 128), F32)],
        scratch_shapes=[pltpu.VMEM((heads, dk, dv), F32), pltpu.VMEM((8, dk), F32),
                        pltpu.VMEM((8, 128), F32),
                        pltpu.VMEM((tb, 128), F32), pltpu.VMEM((tb, val), F32), pltpu.VMEM((tb, 128), F32),
                        pltpu.VMEM((nck, dk, dv), F32), pltpu.VMEM((max(8, nck), dk), F32),
                        pltpu.VMEM((max(8, nck), 128), F32),
                        pltpu.VMEM((nck, dk, dv), F32), pltpu.VMEM((max(8, nck), dk), F32),
                        pltpu.VMEM((max(8, nck), 128), F32)],
        compiler_params=_cparams(("parallel", "arbitrary")),
        name="mlstm_prompt",
    )(pm, pif, bif, c0, n0, m0, nw)


def _mlstm_prep(w_in, b_if, norm_w, w_out, *, heads, dk, dv):
    main = 2 * heads * dk + 2 * heads * dv
    return dict(w_main=w_in[:, :main].astype(BF16),
                w_if=_pad_cols(w_in[:, main:], 128).astype(BF16),
                bif=_pad_cols(b_if[None, :], 128), nw=norm_w[None, :], w_out=w_out.astype(BF16))


def _mlstm_prompt_layer(x, g, p, c0, n0, m0, *, batch, seq, heads, dk, dv):
    pm, pif = _in_proj(x, g, p["w_main"], p["w_if"])
    n0p = jnp.pad(n0, ((0, 0), (0, 8 - heads), (0, 0)))
    m0p = jnp.broadcast_to(jnp.pad(m0, ((0, 0), (0, 8 - heads)))[:, :, None], (batch, 8, 128))
    o, c, n, m = _mlstm_prompt(pm, pif, p["bif"], c0, n0p, m0p, p["nw"],
                               batch=batch, seq=seq, heads=heads, dk=dk, dv=dv, tb=TB_MLSTM)
    return o, c, n[:, :heads, :], m[:, :heads, 0]


def _rwkv_proj_body(h, prev, mu_ref, wrkv_ref, w1_ref, w2_ref, a1_ref, a2_ref, g1_ref, g2_ref,
                    vec_ref, r_ref, k_ref, v_ref, lw_ref, kk_ref, a_ref, g_ref):
    xx = prev - h

    def mix(j):
        return (h + xx * mu_ref[j:j + 1, :]).astype(BF16)

    w0, a0, k_k, k_a = (vec_ref[j:j + 1, :] for j in range(4))
    r_ref[...] = jnp.dot(mix(0), wrkv_ref[0], preferred_element_type=F32)
    lora_w = _dot(jnp.tanh(_dot(mix(1), w1_ref[...])), w2_ref[...])
    w_log = -_softplus(-(w0 + lora_w)) - 0.5
    lw_ref[...] = -jnp.exp(w_log)
    k = jnp.dot(mix(2), wrkv_ref[1], preferred_element_type=F32)
    v_ref[...] = jnp.dot(mix(3), wrkv_ref[2], preferred_element_type=F32)
    a = _sigmoid(a0 + _dot(_dot(mix(4), a1_ref[...]), a2_ref[...]))
    g_ref[...] = _dot(_sigmoid(_dot(mix(5), g1_ref[...])), g2_ref[...])
    kk_ref[...] = k * k_k
    k_ref[...] = k * (1.0 + (a - 1.0) * k_a)
    a_ref[...] = a


def _rwkv_proj_sample_kernel(x_ref, gn_ref, prev_ref, *refs):
    h = _norm_rows(x_ref[...], gn_ref[...])
    hn_ref = refs[-1]
    hn_ref[...] = h
    _rwkv_proj_body(h, prev_ref[...], *refs[:-1])


def _rwkv_proj_prompt_kernel(x_ref, gn_ref, shift0_ref, *refs):
    carry_ref = refs[-1]
    shift_out_ref = refs[-2]
    t = pl.program_id(1)

    @pl.when(t == 0)
    def _():
        carry_ref[...] = shift0_ref[0]

    h = _norm_rows(x_ref[...], gn_ref[...])
    rows = h.shape[0]
    first = lax.broadcasted_iota(jnp.int32, h.shape, 0) == 0
    prev = jnp.where(first, carry_ref[0:1, :], pltpu.roll(h, 1, 0))
    carry_ref[0:1, :] = h[rows - 1:rows, :]
    _rwkv_proj_body(h, prev, *refs[:-2])

    @pl.when(t == pl.num_programs(1) - 1)
    def _():
        shift_out_ref[0] = carry_ref[...]


def _rwkv_proj(x, g_norm, shift0, p, *, batch, seq, tm):
    m, d = x.shape
    consts = [p["mu"], p["w_rkv"], p["w1"], p["w2"], p["a1"], p["a2"], p["g1"], p["g2"], p["vec"]]
    if seq == 1:
        row = pl.BlockSpec((m, d), lambda i: (0, 0))
        out = pl.pallas_call(
            _rwkv_proj_sample_kernel,
            grid=(1,),
            in_specs=[row, pl.BlockSpec((1, d), lambda i: (0, 0)), row] + [_const_block(a) for a in consts],
            out_specs=[row] * 8,
            out_shape=[jax.ShapeDtypeStruct((m, d), F32)] * 8,
            compiler_params=_cparams(("arbitrary",)),
            name="rwkv_proj_sample",
        )(x, g_norm, shift0, *consts)
        return out[:7], out[7]
    tm = min(tm, seq)
    assert seq % tm == 0
    nt = seq // tm
    row = pl.BlockSpec((tm, d), lambda b, t: (b * nt + t, 0))
    st = pl.BlockSpec((1, 8, d), lambda b, t: (b, 0, 0))

    def full(a):
        nd = a.ndim
        return pl.BlockSpec(a.shape, lambda b, t: (0,) * nd)

    shift0_p = jnp.pad(shift0[:, None, :], ((0, 0), (0, 7), (0, 0)))
    out = pl.pallas_call(
        _rwkv_proj_prompt_kernel,
        grid=(batch, nt),
        in_specs=[row, full(g_norm), st] + [full(a) for a in consts],
        out_specs=[row] * 7 + [st],
        out_shape=[jax.ShapeDtypeStruct((m, d), F32)] * 7 + [jax.ShapeDtypeStruct((batch, 8, d), F32)],
        scratch_shapes=[pltpu.VMEM((8, d), F32)],
        compiler_params=_cparams(("parallel", "arbitrary")),
        name="rwkv_proj_prompt",
    )(x, g_norm, shift0_p, *consts)
    return out[:7], out[7][:, 0, :]


def _rwkv_prompt_kernel(r_ref, k_ref, v_ref, lw_ref, kk_ref, a_ref, g_ref, s0_ref, hp_ref,
                        o_ref, s_out_ref, s_ref, rr_ref, yy_ref, mx_ref, n0_ref, gw_ref, gb_ref, el_ref,
                        *, tb, heads, hd):
    t = pl.program_id(1)
    l = CHUNK

    @pl.when(t == 0)
    def _():
        s_ref[...] = s0_ref[0]

    hh = range(heads)
    hs = [slice(h * hd, (h + 1) * hd) for h in hh]
    l2 = 2 * l
    r2 = lax.broadcasted_iota(jnp.int32, (l2, l2), 0)
    c2 = lax.broadcasted_iota(jnp.int32, (l2, l2), 1)
    same = (r2 >= l) == (c2 >= l)
    incl = same & (r2 >= c2)
    strict = same & (r2 > c2)
    tril = incl.astype(F32)
    first = lax.broadcasted_iota(jnp.int32, (l2, heads * hd), 0) < l
    first2 = (lax.broadcasted_iota(jnp.int32, (2 * l2, hd), 0) & l) == 0
    zeros = jnp.zeros((l2, hd), F32)

    pairs = range(tb // l2)
    prow = [slice(pi * l2, (pi + 1) * l2) for pi in pairs]
    lwc, e_in, e_prev, e_neg, e_end = [], [], [], [], []
    for pi in pairs:
        lw = lw_ref[prow[pi], :]
        c = _dot_hi(tril, lw)
        lwc.append(c)
        e_in.append(jnp.exp(c))
        e_prev.append(jnp.exp(c - lw))
        e_neg.append(jnp.exp(-c))
        e_end.append(jnp.exp(jnp.where(first, c[l - 1:l, :], c[l2 - 1:l2, :]) - c))
    cc = [(pi, h) for pi in pairs for h in hh]
    nc = range(len(cc))
    r = [r_ref[prow[pi], hs[h]] for pi, h in cc]
    v = [v_ref[prow[pi], hs[h]] for pi, h in cc]
    k = [k_ref[prow[pi], hs[h]] for pi, h in cc]
    kk = [kk_ref[prow[pi], hs[h]] for pi, h in cc]
    kk = [kk[i] * lax.rsqrt(jnp.maximum(jnp.sum(kk[i] * kk[i], axis=-1, keepdims=True), 1e-24)) for i in nc]
    bv = [kk[i] * a_ref[prow[pi], hs[h]] for i, (pi, h) in enumerate(cc)]
    a_t = [-kk[i] * e_prev[pi][:, hs[h]] for i, (pi, h) in enumerate(cc)]
    r_t = [r[i] * e_in[pi][:, hs[h]] for i, (pi, h) in enumerate(cc)]
    gm = [_dot_nt(jnp.concatenate([a_t[i], r_t[i]], axis=0),
                  jnp.concatenate([bv[i] * e_neg[pi][:, hs[h]], k[i] * e_neg[pi][:, hs[h]]], axis=0))
          for i, (pi, h) in enumerate(cc)]
    ak_m = [jnp.where(strict, gm[i][:l2, l2:], 0.0).astype(BF16) for i in nc]
    rbk_m = [jnp.concatenate([jnp.where(incl, gm[i][l2:, :l2], 0.0),
                              jnp.where(incl, gm[i][l2:, l2:], 0.0)], axis=1).astype(BF16) for i in nc]
    t_inv = _unit_lower_inverse([jnp.where(strict, -gm[i][:l2, :l2], 0.0) for i in nc], l)
    vb = [v[i].astype(BF16) for i in nc]
    akv = [_dot(ak_m[i], vb[i]) for i in nc]
    x1 = [_dot(t_inv[i], jnp.concatenate([a_t[i], akv[i]], axis=1)).astype(BF16) for i in nc]
    low = [jnp.concatenate([x1[i], jnp.concatenate([zeros.astype(BF16), vb[i]], axis=1)], axis=0) for i in nc]
    x2 = [_dot(rbk_m[i], low[i]) for i in nc]
    bk = [jnp.concatenate([bv[i] * e_end[pi][:, hs[h]], k[i] * e_end[pi][:, hs[h]]], axis=0)
          for i, (pi, h) in enumerate(cc)]
    bk2 = [jnp.concatenate([jnp.where(first2, bk[i], 0.0), jnp.where(first2, 0.0, bk[i])], axis=1) for i in nc]
    mn = [_dot_tn(low[i], bk2[i]) for i in nc]
    for i, (pi, h) in enumerate(cc):
        for c in range(2):
            crow = slice(pi * l2 + c * l, pi * l2 + (c + 1) * l)
            mx_ref[h, crow, :] = mn[i][:hd, c * hd:(c + 1) * hd].astype(BF16)
            n0_ref[h, crow, :] = mn[i][hd:, c * hd:(c + 1) * hd]
        rr_ref[h, prow[pi], :] = (r_t[i] + x2[i][:, :hd]).astype(BF16)
        yy_ref[h, prow[pi], :] = x2[i][:, hd:]
        g = g_ref[prow[pi], hs[h]]
        bonus = jnp.sum(r[i] * k[i] * hp_ref[0:1, hs[h]], axis=-1, keepdims=True) * v[i]
        gw_ref[h, prow[pi], :] = hp_ref[1:2, hs[h]] * g
        gb_ref[h, prow[pi], :] = (hp_ref[2:3, hs[h]] + bonus) * g
        el_ref[h, 2 * pi:2 * pi + 1, :] = jnp.exp(lwc[pi][l - 1:l, hs[h]])
        el_ref[h, 2 * pi + 1:2 * pi + 2, :] = jnp.exp(lwc[pi][l2 - 1:l2, hs[h]])

    for ci in range(tb // l):
        rows = slice(ci * l, (ci + 1) * l)
        s = [s_ref[h] for h in hh]
        y = [_dot_nt(rr_ref[h, rows, :], s[h]) for h in hh]
        sm = [_dot(s[h], mx_ref[h, rows, :]) for h in hh]
        for h in hh:
            s_ref[h] = s[h] * el_ref[h, ci:ci + 1, :] + sm[h] + n0_ref[h, rows, :]
        y = [y[h] + yy_ref[h, rows, :] for h in hh]
        mean = [jnp.mean(y[h], axis=-1, keepdims=True) for h in hh]
        yc = [y[h] - mean[h] for h in hh]
        var = [jnp.mean(yc[h] * yc[h], axis=-1, keepdims=True) for h in hh]
        for h in hh:
            o_ref[rows, hs[h]] = yc[h] * lax.rsqrt(var[h] + RW_GN_EPS) * gw_ref[h, rows, :] + gb_ref[h, rows, :]

    @pl.when(t == pl.num_programs(1) - 1)
    def _():
        s_out_ref[0] = s_ref[...]


def _rwkv_prompt(r, k, v, lw, kk, a, g, s0, hp, *, batch, seq, heads, hd, tb):
    d = heads * hd
    tb = min(tb, seq)
    assert seq % tb == 0 and tb % CHUNK == 0
    nt = seq // tb
    row = pl.BlockSpec((tb, d), lambda b, t: (b * nt + t, 0))
    st = pl.BlockSpec((1, heads, hd, hd), lambda b, t: (b, 0, 0, 0))
    return pl.pallas_call(
        functools.partial(_rwkv_prompt_kernel, tb=tb, heads=heads, hd=hd),
        grid=(batch, nt),
        in_specs=[row] * 7 + [st, pl.BlockSpec((8, d), lambda b, t: (0, 0))],
        out_specs=[row, st],
        out_shape=[jax.ShapeDtypeStruct((batch * seq, d), F32),
                   jax.ShapeDtypeStruct((batch, heads, hd, hd), F32)],
        scratch_shapes=[pltpu.VMEM((heads, hd, hd), F32),
                        pltpu.VMEM((heads, tb, hd), BF16), pltpu.VMEM((heads, tb, hd), F32),
                        pltpu.VMEM((heads, tb, hd), BF16), pltpu.VMEM((heads, tb, hd), F32),
                        pltpu.VMEM((heads, tb, hd), F32), pltpu.VMEM((heads, tb, hd), F32),
                        pltpu.VMEM((heads, max(8, tb // CHUNK), hd), F32)],
        compiler_params=_cparams(("parallel", "arbitrary")),
        name="rwkv_prompt",
    )(r, k, v, lw, kk, a, g, s0, hp)


def _pad_rows(a, n):
    return jnp.pad(a, ((0, n - a.shape[0]), (0, 0)))


def _rwkv_prep(mu, w_rkv, w_o, w0, w1, w2, a0, a1, a2, g1, g2, k_k, k_a, r_k, lnx_w, lnx_b):
    d = w0.shape[0]
    lw = -(-w1.shape[1] // 128) * 128
    la = -(-a1.shape[1] // 128) * 128
    lg = -(-g1.shape[1] // 128) * 128
    return dict(mu=_pad_rows(mu, 8), w_rkv=w_rkv.astype(BF16), w_o=w_o.astype(BF16),
                w1=_pad_cols(w1, lw).astype(BF16), w2=_pad_rows(w2, lw).astype(BF16),
                a1=_pad_cols(a1, la).astype(BF16), a2=_pad_rows(a2, la).astype(BF16),
                g1=_pad_cols(g1, lg).astype(BF16), g2=_pad_rows(g2, lg).astype(BF16),
                vec=_pad_rows(jnp.stack([w0, a0, k_k, k_a]), 8),
                hp=_pad_rows(jnp.stack([r_k.reshape(d), lnx_w, lnx_b]), 8))


def _rwkv_prompt_layer(x, g_norm, p, shift0, s0, *, batch, seq, heads, hd):
    (r, k, v, lw, kk, a, g), shift = _rwkv_proj(x, g_norm, shift0, p, batch=batch, seq=seq, tm=TM_RWKV_PROJ)
    o, s = _rwkv_prompt(r, k, v, lw, kk, a, g, s0, p["hp"], batch=batch, seq=seq, heads=heads, hd=hd, tb=TB_MIX)
    return o, shift, s


def _gdn_sample_pre_kernel(pm_ref, pba_ref, conv_ref, cw_ref, gp_ref, qkv_ref, conv_out_ref, sc_ref,
                           *, heads, dk, dv):
    key = heads * dk
    ch = 2 * key + heads * dv
    u = pm_ref[:, 0:ch]
    y = u * cw_ref[CONV_W - 1:CONV_W, :]
    for j in range(CONV_W - 1):
        y = y + conv_ref[j] * cw_ref[j:j + 1, :]
        conv_out_ref[j] = conv_ref[j + 1] if j + 1 < CONV_W - 1 else u
    y = _silu(y)
    for c in range(ch // 128):
        cs = slice(c * 128, (c + 1) * 128)
        yc = y[:, cs]
        if c * 128 < key:
            yc = yc * lax.rsqrt(jnp.sum(yc * yc, axis=-1, keepdims=True) + 1e-6) * (dk ** -0.5)
        elif c * 128 < 2 * key:
            yc = yc * lax.rsqrt(jnp.sum(yc * yc, axis=-1, keepdims=True) + 1e-6)
        qkv_ref[:, cs] = yc
    ba = pba_ref[...]
    lane = lax.broadcasted_iota(jnp.int32, ba.shape, 1)
    g = -jnp.exp(gp_ref[0:1, :]) * _softplus(ba + gp_ref[1:2, :])
    sc_ref[...] = jnp.where(lane < heads, _sigmoid(ba), jnp.exp(g))


def _gdn_sample_pre(pm, pba, conv_t, cw_t, gp, *, heads, dk, dv):
    n = pm.shape[0]
    ch = 2 * heads * dk + heads * dv
    return pl.pallas_call(
        functools.partial(_gdn_sample_pre_kernel, heads=heads, dk=dk, dv=dv),
        out_shape=[jax.ShapeDtypeStruct((n, ch), F32),
                   jax.ShapeDtypeStruct((CONV_W - 1, n, ch), F32),
                   jax.ShapeDtypeStruct((n, 128), F32)],
        compiler_params=pltpu.CompilerParams(vmem_limit_bytes=V7X_VMEM_LIMIT),
        name="gdn_sample_pre",
    )(pm, pba, conv_t, cw_t, gp)


SEQ_PER_STEP = 4


def _seq_block(a, nb):
    nd = a.ndim
    return pl.BlockSpec((nb,) + a.shape[1:], lambda b: (b,) + (0,) * (nd - 1))


def _const_block(a):
    nd = a.ndim
    return pl.BlockSpec(a.shape, lambda b: (0,) * nd)


def _gdn_sample_step_kernel(s0_ref, cols_ref, v_ref, z_ref, sc_ref, nw_ref, *refs, heads, nb, n_prev):
    prev_refs = refs[:n_prev]
    s_out_all, o_ref = refs[n_prev:]
    for p, prev_ref in enumerate(prev_refs):
        s_out_all[p] = prev_ref[...]
    s_out_ref = s_out_all.at[n_prev] if n_prev else s_out_all
    hh = range(heads)
    for i in range(nb):
        kc = [cols_ref[i, :, h:h + 1] for h in hh]
        qc = [cols_ref[i, :, heads + h:heads + h + 1] for h in hh]
        s0 = [s0_ref[i, h] for h in hh]
        eg = [sc_ref[i, h:h + 1, 1:2] for h in hh]
        ks = [jnp.sum(kc[h] * s0[h], axis=0, keepdims=True) for h in hh]
        s1 = [eg[h] * s0[h] + kc[h] * (sc_ref[i, h:h + 1, 0:1] * (v_ref[i, h:h + 1, :] - eg[h] * ks[h]))
              for h in hh]
        o = [jnp.sum(qc[h] * s1[h], axis=0, keepdims=True) for h in hh]
        ms = [jnp.mean(o[h] * o[h], axis=-1, keepdims=True) for h in hh]
        for h in hh:
            s_out_ref[i, h] = s1[h]
            o_ref[i, h:h + 1, :] = o[h] * lax.rsqrt(ms[h] + RMS_EPS) * nw_ref[...] * _silu(z_ref[i, h:h + 1, :])


def _gdn_sample_step(s_all, layer, prev_new, cols, v, z, sc, nw):
    n_layers, n, heads, dk, dv = s_all.shape
    nb = SEQ_PER_STEP
    assert n % nb == 0
    last = layer == n_layers - 1
    n_prev = len(prev_new) if last else 0
    one = pl.BlockSpec((nb, heads, dk, dv), lambda b: (b, 0, 0, 0))
    if n_prev:
        out_state = pl.BlockSpec((n_layers, nb, heads, dk, dv), lambda b: (0, b, 0, 0, 0))
        out_shape = jax.ShapeDtypeStruct(s_all.shape, F32)
    else:
        out_state, out_shape = one, jax.ShapeDtypeStruct(s_all.shape[1:], F32)
    return pl.pallas_call(
        functools.partial(_gdn_sample_step_kernel, heads=heads, nb=nb, n_prev=n_prev),
        grid=(n // nb,),
        in_specs=[pl.BlockSpec((None, nb, heads, dk, dv), lambda b: (layer, b, 0, 0, 0)),
                  _seq_block(cols, nb), _seq_block(v, nb), _seq_block(z, nb), _seq_block(sc, nb),
                  pl.BlockSpec((1, dv), lambda b: (0, 0))] + [one] * n_prev,
        out_specs=[out_state, _seq_block(v, nb)],
        out_shape=[out_shape, jax.ShapeDtypeStruct(v.shape, F32)],
        compiler_params=_cparams(("parallel",)),
        name="gdn_sample_step",
    )(s_all, cols, v, z, sc, nw, *(prev_new if n_prev else []))


def _gdn_sample_layer(x, g, p, conv0, s_all, layer, prev_new, *, heads, dk, dv):
    n = x.shape[0]
    key, val = heads * dk, heads * dv
    ch = 2 * key + val
    pm, pba = _in_proj(x, g, p["w_main"], p["w_ba"])
    qkv, conv_t, sc = _gdn_sample_pre(pm, pba, jnp.transpose(conv0, (1, 0, 2)), p["cw_t"], p["gp"],
                                      heads=heads, dk=dk, dv=dv)
    q_c = jnp.transpose(qkv[:, :key].reshape(n, heads, dk), (0, 2, 1))
    k_c = jnp.transpose(qkv[:, key:2 * key].reshape(n, heads, dk), (0, 2, 1))
    cols = jnp.concatenate([k_c, q_c], axis=-1)
    sc3 = jnp.stack([sc[:, :heads], sc[:, heads:2 * heads]], axis=-1)
    s_new, o = _gdn_sample_step(s_all, layer, prev_new, cols, qkv[:, 2 * key:].reshape(n, heads, dv),
                                pm[:, ch:].reshape(n, heads, dv), sc3, p["nw"])
    return o.reshape(n, val), jnp.transpose(conv_t, (1, 0, 2)), s_new


def _mlstm_sample_step_kernel(c0_ref, n0_ref, cols_ref, q_ref, k_ref, v_ref, op_ref, sc_ref, bif_ref, nw_ref,
                              c_out_ref, n_out_ref, m_out_ref, o_ref, *, heads, dk, nb):
    scale = dk ** -0.5
    for i in range(nb):
        gi = sc_ref[i, :, 0:1] + bif_ref[:, 0:1]
        gf = sc_ref[i, :, 1:2] + bif_ref[:, 1:2]
        m0 = sc_ref[i, :, 2:3]
        gi = GATE_CAP * jnp.tanh(gi / GATE_CAP)
        logf = _log_sigmoid(GATE_CAP * jnp.tanh(gf / GATE_CAP))
        m_new = jnp.maximum(logf + m0, gi)
        f_s = jnp.exp(logf + m0 - m_new)
        i_s = jnp.exp(gi - m_new)
        m_out_ref[i] = m_new
        n1 = f_s * n0_ref[i] + i_s * k_ref[i]
        n_out_ref[i] = n1
        den = jnp.sum(q_ref[i] * scale * n1, axis=-1, keepdims=True)
        floor = jnp.exp(-m_new)
        hh = range(heads)
        kc = [cols_ref[i, :, h:h + 1] for h in hh]
        qc = [cols_ref[i, :, heads + h:heads + h + 1] * scale for h in hh]
        c1 = [f_s[h:h + 1, :] * c0_ref[i, h] + i_s[h:h + 1, :] * (kc[h] * v_ref[i, h:h + 1, :]) for h in hh]
        num = [jnp.sum(qc[h] * c1[h], axis=0, keepdims=True) for h in hh]
        h_t = [num[h] / jnp.maximum(jnp.abs(den[h:h + 1, :]), floor[h:h + 1, :]) for h in hh]
        ms = [jnp.mean(h_t[h] * h_t[h], axis=-1, keepdims=True) for h in hh]
        for h in hh:
            c_out_ref[i, h] = c1[h]
            h_n = h_t[h] * lax.rsqrt(ms[h] + RMS_EPS) * nw_ref[h:h + 1, :]
            o_ref[i, h:h + 1, :] = _sigmoid(op_ref[i, h:h + 1, :]) * h_n


def _mlstm_sample_step(c0, n0, cols, q, k, v, o_pre, sc, bif2, nw2):
    n, heads, dk, dv = c0.shape
    nb = SEQ_PER_STEP
    assert n % nb == 0
    full = _const_block

    def blk(a):
        return _seq_block(a, nb)

    m_shape = (n, heads, 1)
    return pl.pallas_call(
        functools.partial(_mlstm_sample_step_kernel, heads=heads, dk=dk, nb=nb),
        grid=(n // nb,),
        in_specs=[blk(c0), blk(n0), blk(cols), blk(q), blk(k), blk(v), blk(o_pre), blk(sc), full(bif2), full(nw2)],
        out_specs=[blk(c0), blk(n0), pl.BlockSpec((nb, heads, 1), lambda b: (b, 0, 0)), blk(v)],
        out_shape=[jax.ShapeDtypeStruct(c0.shape, F32), jax.ShapeDtypeStruct(n0.shape, F32),
                   jax.ShapeDtypeStruct(m_shape, F32), jax.ShapeDtypeStruct(v.shape, F32)],
        compiler_params=_cparams(("parallel",)),
        name="mlstm_sample_step",
    )(c0, n0, cols, q, k, v, o_pre, sc, bif2, nw2)


def _mlstm_sample_layer(x, g, p, c0, n0, m0, *, heads, dk, dv):
    n = x.shape[0]
    qk_w, val = heads * dk, heads * dv
    pm, pif = _in_proj(x, g, p["w_main"], p["w_if"])
    q = pm[:, :qk_w].reshape(n, heads, dk)
    k = pm[:, qk_w:2 * qk_w].reshape(n, heads, dk)
    v = pm[:, 2 * qk_w:2 * qk_w + val].reshape(n, heads, dv)
    o_pre = pm[:, 2 * qk_w + val:].reshape(n, heads, dv)
    cols = jnp.concatenate([jnp.transpose(k, (0, 2, 1)), jnp.transpose(q, (0, 2, 1))], axis=-1)
    sc = jnp.stack([pif[:, :heads], pif[:, heads:2 * heads], m0], axis=-1)
    bif2 = jnp.stack([p["bif"][0, :heads], p["bif"][0, heads:2 * heads]], axis=-1)
    c1, n1, m1, o = _mlstm_sample_step(c0, n0, cols, q, k, v, o_pre, sc, bif2, p["nw"].reshape(heads, dv))
    return o.reshape(n, val), c1, n1, m1[:, :, 0]


RW_ROW_GROUP = 8


def _rwkv_sample_step_kernel(s_ref, r_ref, k_ref, lw_ref, kk_ref, a_ref, v_ref, g_ref, hp_ref,
                             s_out_ref, o_ref, y_ref, *, hd):
    kk = kk_ref[0]
    kk = kk * lax.rsqrt(jnp.maximum(jnp.sum(kk * kk, axis=0, keepdims=True), 1e-24))
    av = -kk
    bv = kk * a_ref[0]
    w = jnp.exp(lw_ref[0])
    r = r_ref[0]
    k = k_ref[0]
    v = v_ref[0]
    for v0 in range(0, hd, RW_ROW_GROUP):
        vv = range(v0, v0 + RW_ROW_GROUP)
        s0 = [s_ref[0, i] for i in vv]
        sa = [jnp.sum(s * av, axis=0, keepdims=True) for s in s0]
        s1 = [s * w + sa_i * bv + v[i:i + 1, :] * k for s, sa_i, i in zip(s0, sa, vv)]
        y = [jnp.sum(s * r, axis=0, keepdims=True) for s in s1]
        for i, s, y_i in zip(vv, s1, y):
            s_out_ref[0, i] = s
            y_ref[i:i + 1, :] = y_i
    y = y_ref[...]
    yc = y - jnp.mean(y, axis=0, keepdims=True)
    var = jnp.mean(yc * yc, axis=0, keepdims=True)
    hp = hp_ref[0]
    yn = yc * lax.rsqrt(var + RW_GN_EPS) * hp[:, 1:2] + hp[:, 2:3]
    bonus = jnp.sum(r * k * hp[:, 0:1], axis=0, keepdims=True)
    o_ref[0] = (yn + bonus * v) * g_ref[0]


def _rwkv_sample_step(s_t, r, k, lw, kk, a, v, g, hp3):
    heads, hd, _, n = s_t.shape

    def blk(z):
        nd = z.ndim
        return pl.BlockSpec((1,) + z.shape[1:], lambda h: (h,) + (0,) * (nd - 1))

    return pl.pallas_call(
        functools.partial(_rwkv_sample_step_kernel, hd=hd),
        grid=(heads,),
        in_specs=[blk(s_t)] + [blk(z) for z in (r, k, lw, kk, a, v, g, hp3)],
        out_specs=[blk(s_t), blk(v)],
        out_shape=[jax.ShapeDtypeStruct(s_t.shape, F32), jax.ShapeDtypeStruct(v.shape, F32)],
        scratch_shapes=[pltpu.VMEM((hd, n), F32)],
        compiler_params=_cparams(("parallel",)),
        name="rwkv_sample_step",
    )(s_t, r, k, lw, kk, a, v, g, hp3)


def _rwkv_sample_layer(x, g_norm, p, shift0, s0, *, heads, hd):
    n, d = x.shape
    (r, k, v, lw, kk, a, g), hn = _rwkv_proj(x, g_norm, shift0, p, batch=n, seq=1, tm=n)

    def lanes(z):
        return z.T.reshape(heads, hd, n)

    hp = p["hp"]
    hp3 = jnp.stack([hp[j].reshape(heads, hd) for j in range(3)], axis=-1)
    s1_t, o_t = _rwkv_sample_step(jnp.transpose(s0, (1, 2, 3, 0)), lanes(r), lanes(k), lanes(lw), lanes(kk),
                                  lanes(a), lanes(v), lanes(g), hp3)
    return o_t.reshape(d, n).T, hn, jnp.transpose(s1_t, (3, 0, 1, 2))


def _pad_cols(a, n):
    return jnp.pad(a, ((0, 0), (0, n - a.shape[1])))


def _gdn_prep(w_in, conv_w, a_log, dt_bias, norm_w, w_out, *, heads, dk, dv):
    key, val = heads * dk, heads * dv
    ch = 2 * key + val
    main = ch + val
    gp = jnp.zeros((8, 128), F32)
    gp = gp.at[0, heads:2 * heads].set(a_log).at[1, heads:2 * heads].set(dt_bias)
    return dict(w_main=w_in[:, :main].astype(BF16),
                w_ba=_pad_cols(w_in[:, main:], 128).astype(BF16),
                cw_t=jnp.pad(conv_w.T, ((0, 8 - CONV_W), (0, 0))),
                gp=gp, nw=norm_w[None, :], w_out=w_out.astype(BF16))


def _gdn_prompt_layer(x, g, p, conv0, s0, *, batch, seq, heads, dk, dv):
    conv0 = jnp.pad(conv0, ((0, 0), (8 - (CONV_W - 1), 0), (0, 0)))
    qkvz, pba, conv = _gdn_in_proj(x, g, p["w_main"], p["w_ba"], conv0, p["cw_t"], batch=batch, seq=seq,
                                   heads=heads, dk=dk, dv=dv, tm=TM_GDN_PROJ)
    o, s = _gdn_prompt(qkvz, pba, s0, p["gp"], p["nw"], batch=batch, seq=seq, heads=heads, dk=dk, dv=dv, tb=TB_GDN)
    return o, conv[:, 8 - (CONV_W - 1):, :], s


def _trunk(x, states, w, *, batch, seq):
    conv_in, gs_in, c_in, n_in, m_in, shift_in, rs_in = states
    depth = w["norm_mix"].shape[0]
    gh, gdk, gdv = gs_in.shape[2:]
    mh, mdk, mdv = c_in.shape[2:]
    rh, rhd = rs_in.shape[2:4]
    prompt = seq > 1
    outs = [[] for _ in range(7)]
    gs_new = []
    for i in range(depth):
        j = i // 3
        g = w["norm_mix"][i][None, :]
        if i % 3 == 0:
            p = w["gdn"][j]
            if prompt:
                mix, cb, s = _gdn_prompt_layer(x, g, p, conv_in[j], gs_in[j], batch=batch, seq=seq,
                                               heads=gh, dk=gdk, dv=gdv)
                outs[1].append(s)
            else:
                mix, cb, s = _gdn_sample_layer(x, g, p, conv_in[j], gs_in, j, gs_new, heads=gh, dk=gdk, dv=gdv)
                gs_new.append(s)
            outs[0].append(cb)
            w_out = p["w_out"]
        elif i % 3 == 1:
            p = w["ml"][j]
            if prompt:
                mix, c, n, m = _mlstm_prompt_layer(x, g, p, c_in[j], n_in[j], m_in[j], batch=batch, seq=seq,
                                                   heads=mh, dk=mdk, dv=mdv)
            else:
                mix, c, n, m = _mlstm_sample_layer(x, g, p, c_in[j], n_in[j], m_in[j], heads=mh, dk=mdk, dv=mdv)
            w_out = p["w_out"]
            outs[2].append(c)
            outs[3].append(n)
            outs[4].append(m)
        else:
            p = w["rw"][j]
            if prompt:
                mix, sh, s = _rwkv_prompt_layer(x, g, p, shift_in[j], rs_in[j], batch=batch, seq=seq,
                                                heads=rh, hd=rhd)
            else:
                mix, sh, s = _rwkv_sample_layer(x, g, p, shift_in[j], rs_in[j], heads=rh, hd=rhd)
            w_out = p["w_o"]
            outs[5].append(sh)
            outs[6].append(s)
        g_out = w["norm_final"][None, :] if i == depth - 1 else None
        x = _ffn(x, mix, w_out, w["norm_ffn"][i][None, :], w["ffn_w1"], w["ffn_w2"], i, g_out,
                 tm=TM_FFN, tf=TF_FFN)
    y = x
    new = [jnp.stack(z, axis=0) if z else None for z in outs]
    if not prompt:
        new[1] = gs_new[-1] if len(gs_new) > 1 else gs_new[0][None]
    return y, tuple(new)


def kernel(x_prompt, x_sample, state_gdn_conv, state_gdn_S, state_mlstm_C, state_mlstm_n, state_mlstm_m, state_rwkv_shift, state_rwkv_S, norm_mix, norm_ffn, norm_final, gdn_w_in, gdn_conv_w, gdn_a_log, gdn_dt_bias, gdn_norm_w, gdn_w_out, ml_w_in, ml_b_if, ml_norm_w, ml_w_out, rw_mu, rw_w_rkv, rw_w_o, rw_w0, rw_w1, rw_w2, rw_a0, rw_a1, rw_a2, rw_g1, rw_g2, rw_k_k, rw_k_a, rw_r_k, rw_lnx_w, rw_lnx_b, ffn_w1, ffn_w2):
    gh, gdk, gdv = state_gdn_S.shape[2:]
    mh, mdk, mdv = state_mlstm_C.shape[2:]
    w = dict(
        norm_mix=norm_mix, norm_ffn=norm_ffn, norm_final=norm_final,
        ffn_w1=ffn_w1.astype(BF16), ffn_w2=ffn_w2.astype(BF16),
        gdn=[_gdn_prep(gdn_w_in[j], gdn_conv_w[j], gdn_a_log[j], gdn_dt_bias[j], gdn_norm_w[j], gdn_w_out[j],
                       heads=gh, dk=gdk, dv=gdv) for j in range(gdn_w_in.shape[0])],
        ml=[_mlstm_prep(ml_w_in[j], ml_b_if[j], ml_norm_w[j], ml_w_out[j], heads=mh, dk=mdk, dv=mdv)
            for j in range(ml_w_in.shape[0])],
        rw=[_rwkv_prep(rw_mu[j], rw_w_rkv[j], rw_w_o[j], rw_w0[j], rw_w1[j], rw_w2[j], rw_a0[j], rw_a1[j],
                       rw_a2[j], rw_g1[j], rw_g2[j], rw_k_k[j], rw_k_a[j], rw_r_k[j], rw_lnx_w[j], rw_lnx_b[j])
            for j in range(rw_mu.shape[0])])
    sample_states = (state_gdn_conv, state_gdn_S, state_mlstm_C, state_mlstm_n, state_mlstm_m,
                     state_rwkv_shift, state_rwkv_S)
    bp, tp, d = x_prompt.shape
    bs, ts, _ = x_sample.shape
    assert ts == 1
    prompt_states = tuple(jnp.zeros((s.shape[0], bp) + s.shape[2:], s.dtype) for s in sample_states)
    y_p, new_p = _trunk(x_prompt.reshape(bp * tp, d), prompt_states, w, batch=bp, seq=tp)
    y_s, new_s = _trunk(x_sample.reshape(bs * ts, d), sample_states, w, batch=bs, seq=ts)
    out = [y_p.reshape(bp, tp, d), y_s.reshape(bs, ts, d)]
    for a, b in zip(new_p, new_s):
        out += [a, b]
    return tuple(out)
```

```python
import functools

import jax
import jax.numpy as jnp
from jax import lax
from jax.experimental import pallas as pl
from jax.experimental.pallas import tpu as pltpu

F32 = jnp.float32
BF16 = jnp.bfloat16

RMS_EPS = 1e-6
NEG_BIG = -1e30
GATE_CAP = 15.0
RW_GN_EPS = 64e-5
CONV_W = 4
CHUNK = 64
V7X_VMEM_LIMIT = 56 * 1024 * 1024
HI = lax.Precision.HIGHEST


def _cparams(sem):
    return pltpu.CompilerParams(dimension_semantics=sem, vmem_limit_bytes=V7X_VMEM_LIMIT)


def _dot(a, b):
    return jnp.dot(a.astype(BF16), b.astype(BF16), preferred_element_type=F32)


def _dot_nt(a, b):
    return lax.dot_general(a.astype(BF16), b.astype(BF16), (((1,), (1,)), ((), ())),
                           preferred_element_type=F32)


def _dot_tn(a, b):
    return lax.dot_general(a.astype(BF16), b.astype(BF16), (((0,), (0,)), ((), ())),
                           preferred_element_type=F32)


def _dot_hi(a, b):
    return jnp.dot(a, b, preferred_element_type=F32, precision=HI)


def _sigmoid(x):
    return 1.0 / (1.0 + jnp.exp(-x))


def _silu(x):
    return x * _sigmoid(x)


def _softplus(x):
    return jnp.maximum(x, 0.0) + jnp.log(1.0 + jnp.exp(-jnp.abs(x)))


def _log_sigmoid(x):
    return -_softplus(-x)


def _tri_masks(l):
    r = lax.broadcasted_iota(jnp.int32, (l, l), 0)
    c = lax.broadcasted_iota(jnp.int32, (l, l), 1)
    return r >= c, r > c


INV_BASE = 16


def _unit_lower_inverse(mats, l):
    n = mats[0].shape[0]
    r = lax.broadcasted_iota(jnp.int32, (n, n), 0)
    c = lax.broadcasted_iota(jnp.int32, (n, n), 1)
    eye = (r == c).astype(F32)
    size = min(INV_BASE, l)
    shift = size.bit_length() - 1
    diag = (r >> shift) == (c >> shift)
    merges = []
    s = size
    while s < l:
        sh = s.bit_length() - 1
        off = ((r >> (sh + 1)) == (c >> (sh + 1))) & ((r >> sh) > (c >> sh))
        merges.append([jnp.where(off, a, 0.0).astype(BF16) for a in mats])
        s *= 2
    t = [eye - jnp.where(diag, a, 0.0) for a in mats]
    tb = [ti.astype(BF16) for ti in t]
    ab = [jnp.where(diag, a, 0.0).astype(BF16) for a in mats]
    p = [_dot(a, a).astype(BF16) for a in ab]
    k = 2
    while k < size:
        if 2 * k < size:
            both = [_dot(pi, jnp.concatenate([ti, pi], axis=1)) for ti, pi in zip(tb, p)]
            t = [ti + bi[:, :n] for ti, bi in zip(t, both)]
            p = [bi[:, n:].astype(BF16) for bi in both]
        else:
            t = [ti + _dot(pi, ti_b) for ti, ti_b, pi in zip(t, tb, p)]
        tb = [ti.astype(BF16) for ti in t]
        k *= 2
    for a_off in merges:
        x = [_dot(a, ti) for a, ti in zip(a_off, tb)]
        y = [_dot(ti, xi) for ti, xi in zip(tb, x)]
        t = [ti - yi for ti, yi in zip(t, y)]
        tb = [ti.astype(BF16) for ti in t]
    return tb


TM_IN_PROJ = 512
TM_GDN_PROJ = 512
TM_FFN, TF_FFN = 1024, 1024
TB_MIX = 256
TB_MLSTM = 512
TM_RWKV_PROJ = 512


def _norm_rows(x, g):
    return x * lax.rsqrt(jnp.mean(x * x, axis=-1, keepdims=True) + RMS_EPS) * g


def _in_proj_kernel(x_ref, g_ref, w_ref, wa_ref, o_ref, oa_ref):
    xn = _norm_rows(x_ref[...], g_ref[...]).astype(BF16)
    oa_ref[...] = jnp.dot(xn, wa_ref[...], preferred_element_type=F32)
    o_ref[...] = jnp.dot(xn, w_ref[...], preferred_element_type=F32)


def _in_proj(x, g, w_main, w_aux):
    m, k = x.shape
    n, na = w_main.shape[1], w_aux.shape[1]
    tm = min(TM_IN_PROJ, m)
    assert m % tm == 0
    return pl.pallas_call(
        _in_proj_kernel,
        grid=(m // tm,),
        in_specs=[pl.BlockSpec((tm, k), lambda i: (i, 0)), pl.BlockSpec((1, k), lambda i: (0, 0)),
                  pl.BlockSpec((k, n), lambda i: (0, 0)), pl.BlockSpec((k, na), lambda i: (0, 0))],
        out_specs=[pl.BlockSpec((tm, n), lambda i: (i, 0)), pl.BlockSpec((tm, na), lambda i: (i, 0))],
        out_shape=[jax.ShapeDtypeStruct((m, n), F32), jax.ShapeDtypeStruct((m, na), F32)],
        compiler_params=_cparams(("parallel",)),
        name="in_proj",
    )(x, g, w_main, w_aux)


def _ffn_kernel(res_ref, mix_ref, wo_ref, g_ref, w1_ref, w2_ref, go_ref, o_ref, xn_ref, acc_ref, *, out_norm):
    f = pl.program_id(1)

    @pl.when(f == 0)
    def _():
        x = res_ref[...] + jnp.dot(mix_ref[...].astype(BF16), wo_ref[...], preferred_element_type=F32)
        o_ref[...] = x
        xn_ref[...] = _norm_rows(x, g_ref[...]).astype(BF16)
        acc_ref[...] = jnp.zeros_like(acc_ref)

    half = w1_ref.shape[1] // 2
    part = None
    for c in range(2):
        h = jnp.dot(xn_ref[...], w1_ref[:, c * half:(c + 1) * half], preferred_element_type=F32)
        a = jnp.square(jnp.maximum(h, 0.0)).astype(BF16)
        p = jnp.dot(a, w2_ref[c * half:(c + 1) * half, :], preferred_element_type=F32)
        part = p if part is None else part + p
    acc_ref[...] += part

    @pl.when(f == pl.num_programs(1) - 1)
    def _():
        y = o_ref[...] + acc_ref[...]
        o_ref[...] = _norm_rows(y, go_ref[...]) if out_norm else y


def _ffn(res, mix, w_out, g, w1, w2, layer, g_out=None, *, tm, tf):
    m, d = res.shape
    dff = w1.shape[2]
    tm, tf = min(tm, m), min(tf, dff)
    assert m % tm == 0 and dff % tf == 0
    vec = pl.BlockSpec((1, d), lambda i, j: (0, 0))
    return pl.pallas_call(
        functools.partial(_ffn_kernel, out_norm=g_out is not None),
        grid=(m // tm, dff // tf),
        in_specs=[pl.BlockSpec((tm, d), lambda i, j: (i, 0)),
                  pl.BlockSpec((tm, mix.shape[1]), lambda i, j: (i, 0)),
                  pl.BlockSpec(w_out.shape, lambda i, j: (0, 0)), vec,
                  pl.BlockSpec((None, d, tf), lambda i, j: (layer, 0, j)),
                  pl.BlockSpec((None, tf, d), lambda i, j: (layer, j, 0)), vec],
        out_specs=pl.BlockSpec((tm, d), lambda i, j: (i, 0)),
        out_shape=jax.ShapeDtypeStruct((m, d), F32),
        scratch_shapes=[pltpu.VMEM((tm, d), BF16), pltpu.VMEM((tm, d), F32)],
        compiler_params=_cparams(("parallel", "arbitrary")),
        name="ffn",
    )(res, mix, w_out, g, w1, w2, g if g_out is None else g_out)


def _gdn_prompt_kernel(qkv_ref, pba_ref, s0_ref, gp_ref, nw_ref,
                       o_ref, s_out_ref,
                       s_ref, u_ref, w_ref, qd_ref, kd_ref, qk_ref, gl_ref,
                       *, tb, heads, dk, dv):
    t = pl.program_id(1)
    key = heads * dk
    ch = 2 * key + heads * dv
    l = CHUNK

    @pl.when(t == 0)
    def _():
        s_ref[...] = s0_ref[0]

    a_log = gp_ref[0:1, :]
    dt_bias = gp_ref[1:2, :]
    hh = range(heads)
    l2 = 2 * l
    r2 = lax.broadcasted_iota(jnp.int32, (l2, l2), 0)
    c2 = lax.broadcasted_iota(jnp.int32, (l2, l2), 1)
    same = (r2 >= l) == (c2 >= l)
    incl = same & (r2 >= c2)
    strict = same & (r2 > c2)
    tril = incl.astype(F32)
    first = lax.broadcasted_iota(jnp.int32, (l2, 128), 0) < l

    pairs = range(tb // l2)
    rows = [slice(pi * l2, (pi + 1) * l2) for pi in pairs]
    beta_all, gc, gc_t, g_end = [], [], [], []
    for pi in pairs:
        ba = pba_ref[rows[pi], :]
        beta_all.append(_sigmoid(ba))
        g_all = -jnp.exp(a_log) * _softplus(ba + dt_bias)
        gc.append(_dot_hi(tril, g_all))
        gc_t.append(gc[pi].T)
        g_end.append(jnp.where(first, gc[pi][l - 1:l, :], gc[pi][l2 - 1:l2, :]))
        gl_ref[2 * pi:2 * pi + 1, :] = jnp.exp(gc[pi][l - 1:l, :])
        gl_ref[2 * pi + 1:2 * pi + 2, :] = jnp.exp(gc[pi][l2 - 1:l2, :])
    cc = [(pi, h) for pi in pairs for h in hh]
    b_col = [beta_all[pi][:, h:h + 1] for pi, h in cc]
    gi = [gc[pi][:, heads + h:heads + h + 1] for pi, h in cc]
    q = [qkv_ref[rows[pi], h * dk:(h + 1) * dk] for pi, h in cc]
    k = [qkv_ref[rows[pi], key + h * dk:key + (h + 1) * dk] for pi, h in cc]
    v = [qkv_ref[rows[pi], 2 * key + h * dv:2 * key + (h + 1) * dv] for pi, h in cc]
    nc = range(len(cc))
    dmat = [jnp.where(incl, jnp.exp(jnp.where(incl, gi[i] - gc_t[pi][heads + h:heads + h + 1, :], 0.0)), 0.0)
            for i, (pi, h) in enumerate(cc)]
    kb = [k[i] * b_col[i] for i in nc]
    kk = [_dot_nt(kb[i], k[i]) for i in nc]
    qk = [_dot_nt(q[i], k[i]) for i in nc]
    t_inv = _unit_lower_inverse([jnp.where(strict, kk[i] * dmat[i], 0.0) for i in nc], l)
    egi = [jnp.exp(gi[i]) for i in nc]
    sol = [_dot(t_inv[i], jnp.concatenate([v[i] * b_col[i], kb[i] * egi[i]], axis=-1)) for i in nc]
    for i, (pi, h) in enumerate(cc):
        hs = slice(h * dk, (h + 1) * dk)
        u_ref[rows[pi], h * dv:(h + 1) * dv] = sol[i][:, :dv]
        w_ref[rows[pi], hs] = sol[i][:, dv:].astype(BF16)
        qd_ref[rows[pi], hs] = (q[i] * egi[i]).astype(BF16)
        kd_ref[rows[pi], hs] = (k[i] * jnp.exp(g_end[pi][:, heads + h:heads + h + 1] - gi[i])).astype(BF16)
        qkm = jnp.where(incl, qk[i] * dmat[i], 0.0).astype(BF16)
        qk_ref[h, pi * l2:pi * l2 + l, :] = qkm[:l, :l]
        qk_ref[h, pi * l2 + l:(pi + 1) * l2, :] = qkm[l:, l:]

    for ci in range(tb // l):
        rows = slice(ci * l, (ci + 1) * l)
        s = [s_ref[h] for h in hh]
        wq = [_dot(jnp.concatenate([w_ref[rows, h * dk:(h + 1) * dk], qd_ref[rows, h * dk:(h + 1) * dk]], axis=0),
                   s[h]) for h in hh]
        v_new = [u_ref[rows, h * dv:(h + 1) * dv] - wq[h][:l] for h in hh]
        o2 = [_dot(qk_ref[h, rows, :], v_new[h]) for h in hh]
        ds = [_dot_tn(kd_ref[rows, h * dk:(h + 1) * dk], v_new[h]) for h in hh]
        for h in hh:
            s_ref[h] = s[h] * gl_ref[ci:ci + 1, heads + h:heads + h + 1] + ds[h]
            o = wq[h][l:] + o2[h]
            z = qkv_ref[rows, ch + h * dv:ch + (h + 1) * dv]
            o = o * lax.rsqrt(jnp.mean(o * o, axis=-1, keepdims=True) + RMS_EPS) * nw_ref[...]
            o_ref[rows, h * dv:(h + 1) * dv] = o * _silu(z)

    @pl.when(t == pl.num_programs(1) - 1)
    def _():
        s_out_ref[0] = s_ref[...]


def _gdn_prompt(qkvz, pba, s0, gp, nw, *, batch, seq, heads, dk, dv, tb):
    key, val = heads * dk, heads * dv
    ch = 2 * key + val
    tb = min(tb, seq)
    assert seq % tb == 0 and tb % (2 * CHUNK) == 0
    nt = seq // tb
    return pl.pallas_call(
        functools.partial(_gdn_prompt_kernel, tb=tb, heads=heads, dk=dk, dv=dv),
        grid=(batch, nt),
        in_specs=[pl.BlockSpec((tb, ch + val), lambda b, t: (b * nt + t, 0)),
                  pl.BlockSpec((tb, 128), lambda b, t: (b * nt + t, 0)),
                  pl.BlockSpec((1, heads, dk, dv), lambda b, t: (b, 0, 0, 0)),
                  pl.BlockSpec((8, 128), lambda b, t: (0, 0)),
                  pl.BlockSpec((1, dv), lambda b, t: (0, 0))],
        out_specs=[pl.BlockSpec((tb, val), lambda b, t: (b * nt + t, 0)),
                   pl.BlockSpec((1, heads, dk, dv), lambda b, t: (b, 0, 0, 0))],
        out_shape=[jax.ShapeDtypeStruct((batch * seq, val), F32),
                   jax.ShapeDtypeStruct((batch, heads, dk, dv), F32)],
        scratch_shapes=[pltpu.VMEM((heads, dk, dv), F32),
                        pltpu.VMEM((tb, val), F32), pltpu.VMEM((tb, key), BF16),
                        pltpu.VMEM((tb, key), BF16), pltpu.VMEM((tb, key), BF16),
                        pltpu.VMEM((heads, tb, CHUNK), BF16), pltpu.VMEM((max(8, tb // CHUNK), 128), F32)],
        compiler_params=_cparams(("parallel", "arbitrary")),
        name="gdn_prompt",
    )(qkvz, pba, s0, gp, nw)


GDN_PROJ_COLS = 256
GDN_CONV_ROWS = 64


def _gdn_in_proj_kernel(x_ref, g_ref, w_ref, wa_ref, conv0_ref, cw_ref, o_ref, aux_ref, conv_out_ref,
                        xn_ref, carry_ref, *, heads, dk, dv):
    t = pl.program_id(1)
    key = heads * dk
    ch = 2 * key + heads * dv
    tm = x_ref.shape[0]
    n = w_ref.shape[1]

    @pl.when(t == 0)
    def _():
        carry_ref[...] = conv0_ref[0]

    xn_ref[...] = _norm_rows(x_ref[...], g_ref[...]).astype(BF16)
    aux_ref[...] = jnp.dot(xn_ref[...], wa_ref[...], preferred_element_type=F32)
    for c0 in range(0, n, GDN_PROJ_COLS):
        cs = slice(c0, c0 + GDN_PROJ_COLS)
        y = jnp.dot(xn_ref[...], w_ref[:, cs], preferred_element_type=F32)
        if c0 >= ch:
            o_ref[:, cs] = y
            continue
        ext = jnp.concatenate([carry_ref[:, cs], y], axis=0)
        carry_ref[:, cs] = y[tm - 8:, :]
        for r0 in range(0, tm, GDN_CONV_ROWS):
            for j in range(0, GDN_PROJ_COLS, dk):
                cj = slice(c0 + j, c0 + j + dk)
                blk = ext[r0:r0 + GDN_CONV_ROWS + 8, j:j + dk]
                z = blk[8:] * cw_ref[CONV_W - 1:CONV_W, cj]
                for s in range(1, CONV_W):
                    z = z + pltpu.roll(blk, s, 0)[8:] * cw_ref[CONV_W - 1 - s:CONV_W - s, cj]
                z = _silu(z)
                if c0 + j < key:
                    z = z * lax.rsqrt(jnp.sum(z * z, axis=-1, keepdims=True) + 1e-6) * (dk ** -0.5)
                elif c0 + j < 2 * key:
                    z = z * lax.rsqrt(jnp.sum(z * z, axis=-1, keepdims=True) + 1e-6)
                o_ref[r0:r0 + GDN_CONV_ROWS, cj] = z

    @pl.when(t == pl.num_programs(1) - 1)
    def _():
        conv_out_ref[0] = carry_ref[...]


def _gdn_in_proj(x, g, w_main, w_ba, conv0, cw_t, *, batch, seq, heads, dk, dv, tm):
    m, d = x.shape
    key, val = heads * dk, heads * dv
    ch = 2 * key + val
    n = w_main.shape[1]
    assert dk == dv and n % GDN_PROJ_COLS == 0 and ch % GDN_PROJ_COLS == 0 and GDN_PROJ_COLS % dk == 0
    tm = min(tm, seq)
    assert seq % tm == 0 and tm >= 8
    nt = seq // tm
    row = lambda width: pl.BlockSpec((tm, width), lambda b, t: (b * nt + t, 0))
    full = lambda a: pl.BlockSpec(a.shape, lambda b, t: (0,) * a.ndim)
    st = pl.BlockSpec((1, 8, ch), lambda b, t: (b, 0, 0))
    return pl.pallas_call(
        functools.partial(_gdn_in_proj_kernel, heads=heads, dk=dk, dv=dv),
        grid=(batch, nt),
        in_specs=[row(d), full(g), full(w_main), full(w_ba), st, full(cw_t)],
        out_specs=[row(n), row(w_ba.shape[1]), st],
        out_shape=[jax.ShapeDtypeStruct((m, n), F32), jax.ShapeDtypeStruct((m, w_ba.shape[1]), F32),
                   jax.ShapeDtypeStruct((batch, 8, ch), F32)],
        scratch_shapes=[pltpu.VMEM((tm, d), BF16), pltpu.VMEM((8, ch), F32)],
        compiler_params=_cparams(("parallel", "arbitrary")),
        name="gdn_in_proj",
    )(x, g, w_main, w_ba, conv0, cw_t)


def _mlstm_prompt_kernel(pm_ref, pif_ref, bif_ref, c0_ref, n0_ref, m0_ref, nw_ref,
                         o_ref, c_out_ref, n_out_ref, m_out_ref,
                         c_ref, n_ref, m_ref, bc_ref, ni_ref, col_ref, kv_ref, kc_ref, sc_ref,
                         cin_ref, nin_ref, min_ref, *, tb, heads, dk, dv):
    t = pl.program_id(1)
    l = CHUNK
    qk_w = heads * dk
    v_off = 2 * qk_w
    o_off = v_off + heads * dv

    @pl.when(t == 0)
    def _():
        c_ref[...] = c0_ref[0]
        n_ref[...] = n0_ref[0]
        m_ref[...] = m0_ref[0]

    incl, _ = _tri_masks(l)
    tril = incl.astype(F32)
    hh = range(heads)
    nchunk = tb // l
    scale = dk ** -0.5

    crow = [slice(ci * l, (ci + 1) * l) for ci in range(nchunk)]
    gates, bcum, bcum_t, gates_t = [], [], [], []
    for ci in range(nchunk):
        g = pif_ref[crow[ci], :] + bif_ref[...]
        g = GATE_CAP * jnp.tanh(g / GATE_CAP)
        gates.append(g)
        bcum.append(_dot_hi(tril, _log_sigmoid(g)))
        bcum_t.append(bcum[ci].T)
        gates_t.append(g.T)
        bc_ref[crow[ci], :] = bcum[ci]
    cc = [(ci, h) for ci in range(nchunk) for h in hh]
    nc = range(len(cc))
    bi = [bcum[ci][:, heads + h:heads + h + 1] for ci, h in cc]
    b_last = [bcum[ci][l - 1:l, heads + h:heads + h + 1] for ci, h in cc]
    k = [pm_ref[crow[ci], qk_w + h * dk:qk_w + (h + 1) * dk] for ci, h in cc]
    v = [pm_ref[crow[ci], v_off + h * dv:v_off + (h + 1) * dv].astype(BF16) for ci, h in cc]
    qk = [_dot_nt(pm_ref[crow[ci], h * dk:(h + 1) * dk] * scale, k[i]) for i, (ci, h) in enumerate(cc)]
    dlog = [jnp.where(incl, bi[i] - bcum_t[ci][heads + h:heads + h + 1, :] + gates_t[ci][h:h + 1, :], NEG_BIG)
            for i, (ci, h) in enumerate(cc)]
    m_intra = [jnp.max(dlog[i], axis=-1, keepdims=True) for i in nc]
    p = [jnp.where(incl, jnp.exp(dlog[i] - m_intra[i]), 0.0) * qk[i] for i in nc]
    den_intra = [jnp.sum(p[i], axis=-1, keepdims=True) for i in nc]
    num_intra = [_dot(p[i], v[i]) for i in nc]
    a_log = [b_last[i] - bi[i] + gates[ci][:, h:h + 1] for i, (ci, h) in enumerate(cc)]
    m_chunk = [jnp.max(a_log[i], axis=0, keepdims=True) for i in nc]
    kw = [k[i] * jnp.exp(a_log[i] - m_chunk[i]) for i in nc]
    kv_chunk = [_dot_tn(kw[i], v[i]) for i in nc]
    for i, (ci, h) in enumerate(cc):
        ni_ref[crow[ci], h * dv:(h + 1) * dv] = num_intra[i]
        col_ref[crow[ci], h:h + 1] = m_intra[i]
        col_ref[crow[ci], heads + h:heads + h + 1] = den_intra[i]
        kv_ref[i] = kv_chunk[i]
        kc_ref[i:i + 1, :] = jnp.sum(kw[i], axis=0, keepdims=True)
        sc_ref[i:i + 1, 0:1] = m_chunk[i]
        sc_ref[i:i + 1, 1:2] = b_last[i]

    for ci in range(nchunk):
        for h in hh:
            i = ci * heads + h
            c_mat = c_ref[h]
            n_vec = n_ref[h:h + 1, :]
            m_prev = m_ref[h:h + 1, :]
            cin_ref[i] = c_mat
            nin_ref[i:i + 1, :] = n_vec
            min_ref[i:i + 1, :] = m_prev
            m_chunk = sc_ref[i:i + 1, 0:1]
            b_last = sc_ref[i:i + 1, 1:2]
            m_new = jnp.maximum(b_last + m_prev, m_chunk)
            f_s = jnp.exp(b_last + m_prev - m_new)
            i_s = jnp.exp(m_chunk - m_new)
            c_ref[h] = f_s[:, 0:1] * c_mat + i_s[:, 0:1] * kv_ref[i]
            n_ref[h:h + 1, :] = f_s * n_vec + i_s * kc_ref[i:i + 1, :]
            m_ref[h:h + 1, :] = m_new

    q = [pm_ref[crow[ci], h * dk:(h + 1) * dk] * scale for ci, h in cc]
    qc = [_dot(q[i], cin_ref[i]) for i in nc]
    qn = [jnp.sum(q[i] * nin_ref[i:i + 1, :], axis=-1, keepdims=True) for i in nc]
    m_prev = [min_ref[i:i + 1, 0:1] for i in nc]
    bi = [bc_ref[crow[ci], heads + h:heads + h + 1] for ci, h in cc]
    m_in = [col_ref[crow[ci], h:h + 1] for ci, h in cc]
    m_t = [jnp.maximum(bi[i] + m_prev[i], m_in[i]) for i in nc]
    s_inter = [jnp.exp(bi[i] + m_prev[i] - m_t[i]) for i in nc]
    s_intra = [jnp.exp(m_in[i] - m_t[i]) for i in nc]
    den = [s_inter[i] * qn[i] + s_intra[i] * col_ref[crow[ci], heads + h:heads + h + 1]
           for i, (ci, h) in enumerate(cc)]
    h_t = [(s_inter[i] * qc[i] + s_intra[i] * ni_ref[crow[ci], h * dv:(h + 1) * dv])
           / jnp.maximum(jnp.abs(den[i]), jnp.exp(-m_t[i])) for i, (ci, h) in enumerate(cc)]
    ms = [jnp.mean(h_t[i] * h_t[i], axis=-1, keepdims=True) for i in nc]
    for i, (ci, h) in enumerate(cc):
        h_n = h_t[i] * lax.rsqrt(ms[i] + RMS_EPS) * nw_ref[:, h * dv:(h + 1) * dv]
        o_pre = pm_ref[crow[ci], o_off + h * dv:o_off + (h + 1) * dv]
        o_ref[crow[ci], h * dv:(h + 1) * dv] = _sigmoid(o_pre) * h_n

    @pl.when(t == pl.num_programs(1) - 1)
    def _():
        c_out_ref[0] = c_ref[...]
        n_out_ref[0] = n_ref[...]
        m_out_ref[0] = m_ref[...]


def _mlstm_prompt(pm, pif, bif, c0, n0, m0, nw, *, batch, seq, heads, dk, dv, tb):
    width = pm.shape[1]
    val = heads * dv
    tb = min(tb, seq)
    assert seq % tb == 0 and tb % CHUNK == 0
    nt = seq // tb
    nck = (tb // CHUNK) * heads
    return pl.pallas_call(
        functools.partial(_mlstm_prompt_kernel, tb=tb, heads=heads, dk=dk, dv=dv),
        grid=(batch, nt),
        in_specs=[pl.BlockSpec((tb, width), lambda b, t: (b * nt + t, 0)),
                  pl.BlockSpec((tb, 128), lambda b, t: (b * nt + t, 0)),
                  pl.BlockSpec((1, 128), lambda b, t: (0, 0)),
                  pl.BlockSpec((1, heads, dk, dv), lambda b, t: (b, 0, 0, 0)),
                  pl.BlockSpec((1, 8, dk), lambda b, t: (b, 0, 0)),
                  pl.BlockSpec((1, 8, 128), lambda b, t: (b, 0, 0)),
                  pl.BlockSpec((1, val), lambda b, t: (0, 0))],
        out_specs=[pl.BlockSpec((tb, val), lambda b, t: (b * nt + t, 0)),
                   pl.BlockSpec((1, heads, dk, dv), lambda b, t: (b, 0, 0, 0)),
                   pl.BlockSpec((1, 8, dk), lambda b, t: (b, 0, 0)),
                   pl.BlockSpec((1, 8, 128), lambda b, t: (b, 0, 0))],
        out_shape=[jax.ShapeDtypeStruct((batch * seq, val), F32),
                   jax.ShapeDtypeStruct((batch, heads, dk, dv), F32),
                   jax.ShapeDtypeStruct((batch, 8, dk), F32),
                   jax.ShapeDtypeStruct((batch, 8, 128), F32)],
        scratch_shapes=[pltpu.VMEM((heads, dk, dv), F32), pltpu.VMEM((8, dk), F32),
                        pltpu.VMEM((8, 128), F32),
                        pltpu.VMEM((tb, 128), F32), pltpu.VMEM((tb, val), F32), pltpu.VMEM((tb, 128), F32),
                        pltpu.VMEM((nck, dk, dv), F32), pltpu.VMEM((max(8, nck), dk), F32),
                        pltpu.VMEM((max(8, nck), 128), F32),
                        pltpu.VMEM((nck, dk, dv), F32), pltpu.VMEM((max(8, nck), dk), F32),
                        pltpu.VMEM((max(8, nck), 128), F32)],
        compiler_params=_cparams(("parallel", "arbitrary")),
        name="mlstm_prompt",
    )(pm, pif, bif, c0, n0, m0, nw)


def _mlstm_prep(w_in, b_if, norm_w, w_out, *, heads, dk, dv):
    main = 2 * heads * dk + 2 * heads * dv
    return dict(w_main=w_in[:, :main].astype(BF16),
                w_if=_pad_cols(w_in[:, main:], 128).astype(BF16),
                bif=_pad_cols(b_if[None, :], 128), nw=norm_w[None, :], w_out=w_out.astype(BF16))


def _mlstm_prompt_layer(x, g, p, c0, n0, m0, *, batch, seq, heads, dk, dv):
    pm, pif = _in_proj(x, g, p["w_main"], p["w_if"])
    n0p = jnp.pad(n0, ((0, 0), (0, 8 - heads), (0, 0)))
    m0p = jnp.broadcast_to(jnp.pad(m0, ((0, 0), (0, 8 - heads)))[:, :, None], (batch, 8, 128))
    o, c, n, m = _mlstm_prompt(pm, pif, p["bif"], c0, n0p, m0p, p["nw"],
                               batch=batch, seq=seq, heads=heads, dk=dk, dv=dv, tb=TB_MLSTM)
    return o, c, n[:, :heads, :], m[:, :heads, 0]


def _rwkv_proj_body(h, prev, mu_ref, wrkv_ref, w1_ref, w2_ref, a1_ref, a2_ref, g1_ref, g2_ref,
                    vec_ref, r_ref, k_ref, v_ref, lw_ref, kk_ref, a_ref, g_ref):
    xx = prev - h

    def mix(j):
        return (h + xx * mu_ref[j:j + 1, :]).astype(BF16)

    w0, a0, k_k, k_a = (vec_ref[j:j + 1, :] for j in range(4))
    r_ref[...] = jnp.dot(mix(0), wrkv_ref[0], preferred_element_type=F32)
    lora_w = _dot(jnp.tanh(_dot(mix(1), w1_ref[...])), w2_ref[...])
    w_log = -_softplus(-(w0 + lora_w)) - 0.5
    lw_ref[...] = -jnp.exp(w_log)
    k = jnp.dot(mix(2), wrkv_ref[1], preferred_element_type=F32)
    v_ref[...] = jnp.dot(mix(3), wrkv_ref[2], preferred_element_type=F32)
    a = _sigmoid(a0 + _dot(_dot(mix(4), a1_ref[...]), a2_ref[...]))
    g_ref[...] = _dot(_sigmoid(_dot(mix(5), g1_ref[...])), g2_ref[...])
    kk_ref[...] = k * k_k
    k_ref[...] = k * (1.0 + (a - 1.0) * k_a)
    a_ref[...] = a


def _rwkv_proj_sample_kernel(x_ref, gn_ref, prev_ref, *refs):
    h = _norm_rows(x_ref[...], gn_ref[...])
    hn_ref = refs[-1]
    hn_ref[...] = h
    _rwkv_proj_body(h, prev_ref[...], *refs[:-1])


def _rwkv_proj_prompt_kernel(x_ref, gn_ref, shift0_ref, *refs):
    carry_ref = refs[-1]
    shift_out_ref = refs[-2]
    t = pl.program_id(1)

    @pl.when(t == 0)
    def _():
        carry_ref[...] = shift0_ref[0]

    h = _norm_rows(x_ref[...], gn_ref[...])
    rows = h.shape[0]
    first = lax.broadcasted_iota(jnp.int32, h.shape, 0) == 0
    prev = jnp.where(first, carry_ref[0:1, :], pltpu.roll(h, 1, 0))
    carry_ref[0:1, :] = h[rows - 1:rows, :]
    _rwkv_proj_body(h, prev, *refs[:-2])

    @pl.when(t == pl.num_programs(1) - 1)
    def _():
        shift_out_ref[0] = carry_ref[...]


def _rwkv_proj(x, g_norm, shift0, p, *, batch, seq, tm):
    m, d = x.shape
    consts = [p["mu"], p["w_rkv"], p["w1"], p["w2"], p["a1"], p["a2"], p["g1"], p["g2"], p["vec"]]
    if seq == 1:
        row = pl.BlockSpec((m, d), lambda i: (0, 0))
        out = pl.pallas_call(
            _rwkv_proj_sample_kernel,
            grid=(1,),
            in_specs=[row, pl.BlockSpec((1, d), lambda i: (0, 0)), row] + [_const_block(a) for a in consts],
            out_specs=[row] * 8,
            out_shape=[jax.ShapeDtypeStruct((m, d), F32)] * 8,
            compiler_params=_cparams(("arbitrary",)),
            name="rwkv_proj_sample",
        )(x, g_norm, shift0, *consts)
        return out[:7], out[7]
    tm = min(tm, seq)
    assert seq % tm == 0
    nt = seq // tm
    row = pl.BlockSpec((tm, d), lambda b, t: (b * nt + t, 0))
    st = pl.BlockSpec((1, 8, d), lambda b, t: (b, 0, 0))

    def full(a):
        nd = a.ndim
        return pl.BlockSpec(a.shape, lambda b, t: (0,) * nd)

    shift0_p = jnp.pad(shift0[:, None, :], ((0, 0), (0, 7), (0, 0)))
    out = pl.pallas_call(
        _rwkv_proj_prompt_kernel,
        grid=(batch, nt),
        in_specs=[row, full(g_norm), st] + [full(a) for a in consts],
        out_specs=[row] * 7 + [st],
        out_shape=[jax.ShapeDtypeStruct((m, d), F32)] * 7 + [jax.ShapeDtypeStruct((batch, 8, d), F32)],
        scratch_shapes=[pltpu.VMEM((8, d), F32)],
        compiler_params=_cparams(("parallel", "arbitrary")),
        name="rwkv_proj_prompt",
    )(x, g_norm, shift0_p, *consts)
    return out[:7], out[7][:, 0, :]


def _rwkv_prompt_kernel(r_ref, k_ref, v_ref, lw_ref, kk_ref, a_ref, g_ref, s0_ref, hp_ref,
                        o_ref, s_out_ref, s_ref, rr_ref, yy_ref, mx_ref, n0_ref, gw_ref, gb_ref, el_ref,
                        *, tb, heads, hd):
    t = pl.program_id(1)
    l = CHUNK

    @pl.when(t == 0)
    def _():
        s_ref[...] = s0_ref[0]

    hh = range(heads)
    hs = [slice(h * hd, (h + 1) * hd) for h in hh]
    l2 = 2 * l
    r2 = lax.broadcasted_iota(jnp.int32, (l2, l2), 0)
    c2 = lax.broadcasted_iota(jnp.int32, (l2, l2), 1)
    same = (r2 >= l) == (c2 >= l)
    incl = same & (r2 >= c2)
    strict = same & (r2 > c2)
    tril = incl.astype(F32)
    first = lax.broadcasted_iota(jnp.int32, (l2, heads * hd), 0) < l
    first2 = (lax.broadcasted_iota(jnp.int32, (2 * l2, hd), 0) & l) == 0
    zeros = jnp.zeros((l2, hd), F32)

    pairs = range(tb // l2)
    prow = [slice(pi * l2, (pi + 1) * l2) for pi in pairs]
    lwc, e_in, e_prev, e_neg, e_end = [], [], [], [], []
    for pi in pairs:
        lw = lw_ref[prow[pi], :]
        c = _dot_hi(tril, lw)
        lwc.append(c)
        e_in.append(jnp.exp(c))
        e_prev.append(jnp.exp(c - lw))
        e_neg.append(jnp.exp(-c))
        e_end.append(jnp.exp(jnp.where(first, c[l - 1:l, :], c[l2 - 1:l2, :]) - c))
    cc = [(pi, h) for pi in pairs for h in hh]
    nc = range(len(cc))
    r = [r_ref[prow[pi], hs[h]] for pi, h in cc]
    v = [v_ref[prow[pi], hs[h]] for pi, h in cc]
    k = [k_ref[prow[pi], hs[h]] for pi, h in cc]
    kk = [kk_ref[prow[pi], hs[h]] for pi, h in cc]
    kk = [kk[i] * lax.rsqrt(jnp.maximum(jnp.sum(kk[i] * kk[i], axis=-1, keepdims=True), 1e-24)) for i in nc]
    bv = [kk[i] * a_ref[prow[pi], hs[h]] for i, (pi, h) in enumerate(cc)]
    a_t = [-kk[i] * e_prev[pi][:, hs[h]] for i, (pi, h) in enumerate(cc)]
    r_t = [r[i] * e_in[pi][:, hs[h]] for i, (pi, h) in enumerate(cc)]
    gm = [_dot_nt(jnp.concatenate([a_t[i], r_t[i]], axis=0),
                  jnp.concatenate([bv[i] * e_neg[pi][:, hs[h]], k[i] * e_neg[pi][:, hs[h]]], axis=0))
          for i, (pi, h) in enumerate(cc)]
    ak_m = [jnp.where(strict, gm[i][:l2, l2:], 0.0).astype(BF16) for i in nc]
    rbk_m = [jnp.concatenate([jnp.where(incl, gm[i][l2:, :l2], 0.0),
                              jnp.where(incl, gm[i][l2:, l2:], 0.0)], axis=1).astype(BF16) for i in nc]
    t_inv = _unit_lower_inverse([jnp.where(strict, -gm[i][:l2, :l2], 0.0) for i in nc], l)
    vb = [v[i].astype(BF16) for i in nc]
    akv = [_dot(ak_m[i], vb[i]) for i in nc]
    x1 = [_dot(t_inv[i], jnp.concatenate([a_t[i], akv[i]], axis=1)).astype(BF16) for i in nc]
    low = [jnp.concatenate([x1[i], jnp.concatenate([zeros.astype(BF16), vb[i]], axis=1)], axis=0) for i in nc]
    x2 = [_dot(rbk_m[i], low[i]) for i in nc]
    bk = [jnp.concatenate([bv[i] * e_end[pi][:, hs[h]], k[i] * e_end[pi][:, hs[h]]], axis=0)
          for i, (pi, h) in enumerate(cc)]
    bk2 = [jnp.concatenate([jnp.where(first2, bk[i], 0.0), jnp.where(first2, 0.0, bk[i])], axis=1) for i in nc]
    mn = [_dot_tn(low[i], bk2[i]) for i in nc]
    for i, (pi, h) in enumerate(cc):
        for c in range(2):
            crow = slice(pi * l2 + c * l, pi * l2 + (c + 1) * l)
            mx_ref[h, crow, :] = mn[i][:hd, c * hd:(c + 1) * hd].astype(BF16)
            n0_ref[h, crow, :] = mn[i][hd:, c * hd:(c + 1) * hd]
        rr_ref[h, prow[pi], :] = (r_t[i] + x2[i][:, :hd]).astype(BF16)
        yy_ref[h, prow[pi], :] = x2[i][:, hd:]
        g = g_ref[prow[pi], hs[h]]
        bonus = jnp.sum(r[i] * k[i] * hp_ref[0:1, hs[h]], axis=-1, keepdims=True) * v[i]
        gw_ref[h, prow[pi], :] = hp_ref[1:2, hs[h]] * g
        gb_ref[h, prow[pi], :] = (hp_ref[2:3, hs[h]] + bonus) * g
        el_ref[h, 2 * pi:2 * pi + 1, :] = jnp.exp(lwc[pi][l - 1:l, hs[h]])
        el_ref[h, 2 * pi + 1:2 * pi + 2, :] = jnp.exp(lwc[pi][l2 - 1:l2, hs[h]])

    for ci in range(tb // l):
        rows = slice(ci * l, (ci + 1) * l)
        s = [s_ref[h] for h in hh]
        y = [_dot_nt(rr_ref[h, rows, :], s[h]) for h in hh]
        sm = [_dot(s[h], mx_ref[h, rows, :]) for h in hh]
        for h in hh:
            s_ref[h] = s[h] * el_ref[h, ci:ci + 1, :] + sm[h] + n0_ref[h, rows, :]
        y = [y[h] + yy_ref[h, rows, :] for h in hh]
        mean = [jnp.mean(y[h], axis=-1, keepdims=True) for h in hh]
        yc = [y[h] - mean[h] for h in hh]
        var = [jnp.mean(yc[h] * yc[h], axis=-1, keepdims=True) for h in hh]
        for h in hh:
            o_ref[rows, hs[h]] = yc[h] * lax.rsqrt(var[h] + RW_GN_EPS) * gw_ref[h, rows, :] + gb_ref[h, rows, :]

    @pl.when(t == pl.num_programs(1) - 1)
    def _():
        s_out_ref[0] = s_ref[...]


def _rwkv_prompt(r, k, v, lw, kk, a, g, s0, hp, *, batch, seq, heads, hd, tb):
    d = heads * hd
    tb = min(tb, seq)
    assert seq % tb == 0 and tb % CHUNK == 0
    nt = seq // tb
    row = pl.BlockSpec((tb, d), lambda b, t: (b * nt + t, 0))
    st = pl.BlockSpec((1, heads, hd, hd), lambda b, t: (b, 0, 0, 0))
    return pl.pallas_call(
        functools.partial(_rwkv_prompt_kernel, tb=tb, heads=heads, hd=hd),
        grid=(batch, nt),
        in_specs=[row] * 7 + [st, pl.BlockSpec((8, d), lambda b, t: (0, 0))],
        out_specs=[row, st],
        out_shape=[jax.ShapeDtypeStruct((batch * seq, d), F32),
                   jax.ShapeDtypeStruct((batch, heads, hd, hd), F32)],
        scratch_shapes=[pltpu.VMEM((heads, hd, hd), F32),
                        pltpu.VMEM((heads, tb, hd), BF16), pltpu.VMEM((heads, tb, hd), F32),
                        pltpu.VMEM((heads, tb, hd), BF16), pltpu.VMEM((heads, tb, hd), F32),
                        pltpu.VMEM((heads, tb, hd), F32), pltpu.VMEM((heads, tb, hd), F32),
                        pltpu.VMEM((heads, max(8, tb // CHUNK), hd), F32)],
        compiler_params=_cparams(("parallel", "arbitrary")),
        name="rwkv_prompt",
    )(r, k, v, lw, kk, a, g, s0, hp)


def _pad_rows(a, n):
    return jnp.pad(a, ((0, n - a.shape[0]), (0, 0)))


def _rwkv_prep(mu, w_rkv, w_o, w0, w1, w2, a0, a1, a2, g1, g2, k_k, k_a, r_k, lnx_w, lnx_b):
    d = w0.shape[0]
    lw = -(-w1.shape[1] // 128) * 128
    la = -(-a1.shape[1] // 128) * 128
    lg = -(-g1.shape[1] // 128) * 128
    return dict(mu=_pad_rows(mu, 8), w_rkv=w_rkv.astype(BF16), w_o=w_o.astype(BF16),
                w1=_pad_cols(w1, lw).astype(BF16), w2=_pad_rows(w2, lw).astype(BF16),
                a1=_pad_cols(a1, la).astype(BF16), a2=_pad_rows(a2, la).astype(BF16),
                g1=_pad_cols(g1, lg).astype(BF16), g2=_pad_rows(g2, lg).astype(BF16),
                vec=_pad_rows(jnp.stack([w0, a0, k_k, k_a]), 8),
                hp=_pad_rows(jnp.stack([r_k.reshape(d), lnx_w, lnx_b]), 8))


def _rwkv_prompt_layer(x, g_norm, p, shift0, s0, *, batch, seq, heads, hd):
    (r, k, v, lw, kk, a, g), shift = _rwkv_proj(x, g_norm, shift0, p, batch=batch, seq=seq, tm=TM_RWKV_PROJ)
    o, s = _rwkv_prompt(r, k, v, lw, kk, a, g, s0, p["hp"], batch=batch, seq=seq, heads=heads, hd=hd, tb=TB_MIX)
    return o, shift, s


def _gdn_sample_pre_kernel(pm_ref, pba_ref, conv_ref, cw_ref, gp_ref, qkv_ref, conv_out_ref, sc_ref,
                           *, heads, dk, dv):
    key = heads * dk
    ch = 2 * key + heads * dv
    u = pm_ref[:, 0:ch]
    y = u * cw_ref[CONV_W - 1:CONV_W, :]
    for j in range(CONV_W - 1):
        y = y + conv_ref[j] * cw_ref[j:j + 1, :]
        conv_out_ref[j] = conv_ref[j + 1] if j + 1 < CONV_W - 1 else u
    y = _silu(y)
    for c in range(ch // 128):
        cs = slice(c * 128, (c + 1) * 128)
        yc = y[:, cs]
        if c * 128 < key:
            yc = yc * lax.rsqrt(jnp.sum(yc * yc, axis=-1, keepdims=True) + 1e-6) * (dk ** -0.5)
        elif c * 128 < 2 * key:
            yc = yc * lax.rsqrt(jnp.sum(yc * yc, axis=-1, keepdims=True) + 1e-6)
        qkv_ref[:, cs] = yc
    ba = pba_ref[...]
    lane = lax.broadcasted_iota(jnp.int32, ba.shape, 1)
    g = -jnp.exp(gp_ref[0:1, :]) * _softplus(ba + gp_ref[1:2, :])
    sc_ref[...] = jnp.where(lane < heads, _sigmoid(ba), jnp.exp(g))


def _gdn_sample_pre(pm, pba, conv_t, cw_t, gp, *, heads, dk, dv):
    n = pm.shape[0]
    ch = 2 * heads * dk + heads * dv
    return pl.pallas_call(
        functools.partial(_gdn_sample_pre_kernel, heads=heads, dk=dk, dv=dv),
        out_shape=[jax.ShapeDtypeStruct((n, ch), F32),
                   jax.ShapeDtypeStruct((CONV_W - 1, n, ch), F32),
                   jax.ShapeDtypeStruct((n, 128), F32)],
        compiler_params=pltpu.CompilerParams(vmem_limit_bytes=V7X_VMEM_LIMIT),
        name="gdn_sample_pre",
    )(pm, pba, conv_t, cw_t, gp)


SEQ_PER_STEP = 4


def _seq_block(a, nb):
    nd = a.ndim
    return pl.BlockSpec((nb,) + a.shape[1:], lambda b: (b,) + (0,) * (nd - 1))


def _const_block(a):
    nd = a.ndim
    return pl.BlockSpec(a.shape, lambda b: (0,) * nd)


def _gdn_sample_step_kernel(s0_ref, cols_ref, v_ref, z_ref, sc_ref, nw_ref, *refs, heads, nb, n_prev):
    prev_refs = refs[:n_prev]
    s_out_all, o_ref = refs[n_prev:]
    for p, prev_ref in enumerate(prev_refs):
        s_out_all[p] = prev_ref[...]
    s_out_ref = s_out_all.at[n_prev] if n_prev else s_out_all
    hh = range(heads)
    for i in range(nb):
        kc = [cols_ref[i, :, h:h + 1] for h in hh]
        qc = [cols_ref[i, :, heads + h:heads + h + 1] for h in hh]
        s0 = [s0_ref[i, h] for h in hh]
        eg = [sc_ref[i, h:h + 1, 1:2] for h in hh]
        ks = [jnp.sum(kc[h] * s0[h], axis=0, keepdims=True) for h in hh]
        s1 = [eg[h] * s0[h] + kc[h] * (sc_ref[i, h:h + 1, 0:1] * (v_ref[i, h:h + 1, :] - eg[h] * ks[h]))
              for h in hh]
        o = [jnp.sum(qc[h] * s1[h], axis=0, keepdims=True) for h in hh]
        ms = [jnp.mean(o[h] * o[h], axis=-1, keepdims=True) for h in hh]
        for h in hh:
            s_out_ref[i, h] = s1[h]
            o_ref[i, h:h + 1, :] = o[h] * lax.rsqrt(ms[h] + RMS_EPS) * nw_ref[...] * _silu(z_ref[i, h:h + 1, :])


def _gdn_sample_step(s_all, layer, prev_new, cols, v, z, sc, nw):
    n_layers, n, heads, dk, dv = s_all.shape
    nb = SEQ_PER_STEP
    assert n % nb == 0
    last = layer == n_layers - 1
    n_prev = len(prev_new) if last else 0
    one = pl.BlockSpec((nb, heads, dk, dv), lambda b: (b, 0, 0, 0))
    if n_prev:
        out_state = pl.BlockSpec((n_layers, nb, heads, dk, dv), lambda b: (0, b, 0, 0, 0))
        out_shape = jax.ShapeDtypeStruct(s_all.shape, F32)
    else:
        out_state, out_shape = one, jax.ShapeDtypeStruct(s_all.shape[1:], F32)
    return pl.pallas_call(
        functools.partial(_gdn_sample_step_kernel, heads=heads, nb=nb, n_prev=n_prev),
        grid=(n // nb,),
        in_specs=[pl.BlockSpec((None, nb, heads, dk, dv), lambda b: (layer, b, 0, 0, 0)),
                  _seq_block(cols, nb), _seq_block(v, nb), _seq_block(z, nb), _seq_block(sc, nb),
                  pl.BlockSpec((1, dv), lambda b: (0, 0))] + [one] * n_prev,
        out_specs=[out_state, _seq_block(v, nb)],
        out_shape=[out_shape, jax.ShapeDtypeStruct(v.shape, F32)],
        compiler_params=_cparams(("parallel",)),
        name="gdn_sample_step",
    )(s_all, cols, v, z, sc, nw, *(prev_new if n_prev else []))


def _gdn_sample_layer(x, g, p, conv0, s_all, layer, prev_new, *, heads, dk, dv):
    n = x.shape[0]
    key, val = heads * dk, heads * dv
    ch = 2 * key + val
    pm, pba = _in_proj(x, g, p["w_main"], p["w_ba"])
    qkv, conv_t, sc = _gdn_sample_pre(pm, pba, jnp.transpose(conv0, (1, 0, 2)), p["cw_t"], p["gp"],
                                      heads=heads, dk=dk, dv=dv)
    q_c = jnp.transpose(qkv[:, :key].reshape(n, heads, dk), (0, 2, 1))
    k_c = jnp.transpose(qkv[:, key:2 * key].reshape(n, heads, dk), (0, 2, 1))
    cols = jnp.concatenate([k_c, q_c], axis=-1)
    sc3 = jnp.stack([sc[:, :heads], sc[:, heads:2 * heads]], axis=-1)
    s_new, o = _gdn_sample_step(s_all, layer, prev_new, cols, qkv[:, 2 * key:].reshape(n, heads, dv),
                                pm[:, ch:].reshape(n, heads, dv), sc3, p["nw"])
    return o.reshape(n, val), jnp.transpose(conv_t, (1, 0, 2)), s_new


def _mlstm_sample_step_kernel(c0_ref, n0_ref, cols_ref, q_ref, k_ref, v_ref, op_ref, sc_ref, bif_ref, nw_ref,
                              c_out_ref, n_out_ref, m_out_ref, o_ref, *, heads, dk, nb):
    scale = dk ** -0.5
    for i in range(nb):
        gi = sc_ref[i, :, 0:1] + bif_ref[:, 0:1]
        gf = sc_ref[i, :, 1:2] + bif_ref[:, 1:2]
        m0 = sc_ref[i, :, 2:3]
        gi = GATE_CAP * jnp.tanh(gi / GATE_CAP)
        logf = _log_sigmoid(GATE_CAP * jnp.tanh(gf / GATE_CAP))
        m_new = jnp.maximum(logf + m0, gi)
        f_s = jnp.exp(logf + m0 - m_new)
        i_s = jnp.exp(gi - m_new)
        m_out_ref[i] = m_new
        n1 = f_s * n0_ref[i] + i_s * k_ref[i]
        n_out_ref[i] = n1
        den = jnp.sum(q_ref[i] * scale * n1, axis=-1, keepdims=True)
        floor = jnp.exp(-m_new)
        hh = range(heads)
        kc = [cols_ref[i, :, h:h + 1] for h in hh]
        qc = [cols_ref[i, :, heads + h:heads + h + 1] * scale for h in hh]
        c1 = [f_s[h:h + 1, :] * c0_ref[i, h] + i_s[h:h + 1, :] * (kc[h] * v_ref[i, h:h + 1, :]) for h in hh]
        num = [jnp.sum(qc[h] * c1[h], axis=0, keepdims=True) for h in hh]
        h_t = [num[h] / jnp.maximum(jnp.abs(den[h:h + 1, :]), floor[h:h + 1, :]) for h in hh]
        ms = [jnp.mean(h_t[h] * h_t[h], axis=-1, keepdims=True) for h in hh]
        for h in hh:
            c_out_ref[i, h] = c1[h]
            h_n = h_t[h] * lax.rsqrt(ms[h] + RMS_EPS) * nw_ref[h:h + 1, :]
            o_ref[i, h:h + 1, :] = _sigmoid(op_ref[i, h:h + 1, :]) * h_n


def _mlstm_sample_step(c0, n0, cols, q, k, v, o_pre, sc, bif2, nw2):
    n, heads, dk, dv = c0.shape
    nb = SEQ_PER_STEP
    assert n % nb == 0
    full = _const_block

    def blk(a):
        return _seq_block(a, nb)

    m_shape = (n, heads, 1)
    return pl.pallas_call(
        functools.partial(_mlstm_sample_step_kernel, heads=heads, dk=dk, nb=nb),
        grid=(n // nb,),
        in_specs=[blk(c0), blk(n0), blk(cols), blk(q), blk(k), blk(v), blk(o_pre), blk(sc), full(bif2), full(nw2)],
        out_specs=[blk(c0), blk(n0), pl.BlockSpec((nb, heads, 1), lambda b: (b, 0, 0)), blk(v)],
        out_shape=[jax.ShapeDtypeStruct(c0.shape, F32), jax.ShapeDtypeStruct(n0.shape, F32),
                   jax.ShapeDtypeStruct(m_shape, F32), jax.ShapeDtypeStruct(v.shape, F32)],
        compiler_params=_cparams(("parallel",)),
        name="mlstm_sample_step",
    )(c0, n0, cols, q, k, v, o_pre, sc, bif2, nw2)


def _mlstm_sample_layer(x, g, p, c0, n0, m0, *, heads, dk, dv):
    n = x.shape[0]
    qk_w, val = heads * dk, heads * dv
    pm, pif = _in_proj(x, g, p["w_main"], p["w_if"])
    q = pm[:, :qk_w].reshape(n, heads, dk)
    k = pm[:, qk_w:2 * qk_w].reshape(n, heads, dk)
    v = pm[:, 2 * qk_w:2 * qk_w + val].reshape(n, heads, dv)
    o_pre = pm[:, 2 * qk_w + val:].reshape(n, heads, dv)
    cols = jnp.concatenate([jnp.transpose(k, (0, 2, 1)), jnp.transpose(q, (0, 2, 1))], axis=-1)
    sc = jnp.stack([pif[:, :heads], pif[:, heads:2 * heads], m0], axis=-1)
    bif2 = jnp.stack([p["bif"][0, :heads], p["bif"][0, heads:2 * heads]], axis=-1)
    c1, n1, m1, o = _mlstm_sample_step(c0, n0, cols, q, k, v, o_pre, sc, bif2, p["nw"].reshape(heads, dv))
    return o.reshape(n, val), c1, n1, m1[:, :, 0]


RW_ROW_GROUP = 8


def _rwkv_sample_step_kernel(s_ref, r_ref, k_ref, lw_ref, kk_ref, a_ref, v_ref, g_ref, hp_ref,
                             s_out_ref, o_ref, y_ref, *, hd):
    kk = kk_ref[0]
    kk = kk * lax.rsqrt(jnp.maximum(jnp.sum(kk * kk, axis=0, keepdims=True), 1e-24))
    av = -kk
    bv = kk * a_ref[0]
    w = jnp.exp(lw_ref[0])
    r = r_ref[0]
    k = k_ref[0]
    v = v_ref[0]
    for v0 in range(0, hd, RW_ROW_GROUP):
        vv = range(v0, v0 + RW_ROW_GROUP)
        s0 = [s_ref[0, i] for i in vv]
        sa = [jnp.sum(s * av, axis=0, keepdims=True) for s in s0]
        s1 = [s * w + sa_i * bv + v[i:i + 1, :] * k for s, sa_i, i in zip(s0, sa, vv)]
        y = [jnp.sum(s * r, axis=0, keepdims=True) for s in s1]
        for i, s, y_i in zip(vv, s1, y):
            s_out_ref[0, i] = s
            y_ref[i:i + 1, :] = y_i
    y = y_ref[...]
    yc = y - jnp.mean(y, axis=0, keepdims=True)
    var = jnp.mean(yc * yc, axis=0, keepdims=True)
    hp = hp_ref[0]
    yn = yc * lax.rsqrt(var + RW_GN_EPS) * hp[:, 1:2] + hp[:, 2:3]
    bonus = jnp.sum(r * k * hp[:, 0:1], axis=0, keepdims=True)
    o_ref[0] = (yn + bonus * v) * g_ref[0]


def _rwkv_sample_step(s_t, r, k, lw, kk, a, v, g, hp3):
    heads, hd, _, n = s_t.shape

    def blk(z):
        nd = z.ndim
        return pl.BlockSpec((1,) + z.shape[1:], lambda h: (h,) + (0,) * (nd - 1))

    return pl.pallas_call(
        functools.partial(_rwkv_sample_step_kernel, hd=hd),
        grid=(heads,),
        in_specs=[blk(s_t)] + [blk(z) for z in (r, k, lw, kk, a, v, g, hp3)],
        out_specs=[blk(s_t), blk(v)],
        out_shape=[jax.ShapeDtypeStruct(s_t.shape, F32), jax.ShapeDtypeStruct(v.shape, F32)],
        scratch_shapes=[pltpu.VMEM((hd, n), F32)],
        compiler_params=_cparams(("parallel",)),
        name="rwkv_sample_step",
    )(s_t, r, k, lw, kk, a, v, g, hp3)


def _rwkv_sample_layer(x, g_norm, p, shift0, s0, *, heads, hd):
    n, d = x.shape
    (r, k, v, lw, kk, a, g), hn = _rwkv_proj(x, g_norm, shift0, p, batch=n, seq=1, tm=n)

    def lanes(z):
        return z.T.reshape(heads, hd, n)

    hp = p["hp"]
    hp3 = jnp.stack([hp[j].reshape(heads, hd) for j in range(3)], axis=-1)
    s1_t, o_t = _rwkv_sample_step(jnp.transpose(s0, (1, 2, 3, 0)), lanes(r), lanes(k), lanes(lw), lanes(kk),
                                  lanes(a), lanes(v), lanes(g), hp3)
    return o_t.reshape(d, n).T, hn, jnp.transpose(s1_t, (3, 0, 1, 2))


def _pad_cols(a, n):
    return jnp.pad(a, ((0, 0), (0, n - a.shape[1])))


def _gdn_prep(w_in, conv_w, a_log, dt_bias, norm_w, w_out, *, heads, dk, dv):
    key, val = heads * dk, heads * dv
    ch = 2 * key + val
    main = ch + val
    gp = jnp.zeros((8, 128), F32)
    gp = gp.at[0, heads:2 * heads].set(a_log).at[1, heads:2 * heads].set(dt_bias)
    return dict(w_main=w_in[:, :main].astype(BF16),
                w_ba=_pad_cols(w_in[:, main:], 128).astype(BF16),
                cw_t=jnp.pad(conv_w.T, ((0, 8 - CONV_W), (0, 0))),
                gp=gp, nw=norm_w[None, :], w_out=w_out.astype(BF16))


def _gdn_prompt_layer(x, g, p, conv0, s0, *, batch, seq, heads, dk, dv):
    conv0 = jnp.pad(conv0, ((0, 0), (8 - (CONV_W - 1), 0), (0, 0)))
    qkvz, pba, conv = _gdn_in_proj(x, g, p["w_main"], p["w_ba"], conv0, p["cw_t"], batch=batch, seq=seq,
                                   heads=heads, dk=dk, dv=dv, tm=TM_GDN_PROJ)
    o, s = _gdn_prompt(qkvz, pba, s0, p["gp"], p["nw"], batch=batch, seq=seq, heads=heads, dk=dk, dv=dv, tb=TB_MIX)
    return o, conv[:, 8 - (CONV_W - 1):, :], s


def _trunk(x, states, w, *, batch, seq):
    conv_in, gs_in, c_in, n_in, m_in, shift_in, rs_in = states
    depth = w["norm_mix"].shape[0]
    gh, gdk, gdv = gs_in.shape[2:]
    mh, mdk, mdv = c_in.shape[2:]
    rh, rhd = rs_in.shape[2:4]
    prompt = seq > 1
    outs = [[] for _ in range(7)]
    gs_new = []
    for i in range(depth):
        j = i // 3
        g = w["norm_mix"][i][None, :]
        if i % 3 == 0:
            p = w["gdn"][j]
            if prompt:
                mix, cb, s = _gdn_prompt_layer(x, g, p, conv_in[j], gs_in[j], batch=batch, seq=seq,
                                               heads=gh, dk=gdk, dv=gdv)
                outs[1].append(s)
            else:
                mix, cb, s = _gdn_sample_layer(x, g, p, conv_in[j], gs_in, j, gs_new, heads=gh, dk=gdk, dv=gdv)
                gs_new.append(s)
            outs[0].append(cb)
            w_out = p["w_out"]
        elif i % 3 == 1:
            p = w["ml"][j]
            if prompt:
                mix, c, n, m = _mlstm_prompt_layer(x, g, p, c_in[j], n_in[j], m_in[j], batch=batch, seq=seq,
                                                   heads=mh, dk=mdk, dv=mdv)
            else:
                mix, c, n, m = _mlstm_sample_layer(x, g, p, c_in[j], n_in[j], m_in[j], heads=mh, dk=mdk, dv=mdv)
            w_out = p["w_out"]
            outs[2].append(c)
            outs[3].append(n)
            outs[4].append(m)
        else:
            p = w["rw"][j]
            if prompt:
                mix, sh, s = _rwkv_prompt_layer(x, g, p, shift_in[j], rs_in[j], batch=batch, seq=seq,
                                                heads=rh, hd=rhd)
            else:
                mix, sh, s = _rwkv_sample_layer(x, g, p, shift_in[j], rs_in[j], heads=rh, hd=rhd)
            w_out = p["w_o"]
            outs[5].append(sh)
            outs[6].append(s)
        g_out = w["norm_final"][None, :] if i == depth - 1 else None
        x = _ffn(x, mix, w_out, w["norm_ffn"][i][None, :], w["ffn_w1"], w["ffn_w2"], i, g_out,
                 tm=TM_FFN, tf=TF_FFN)
    y = x
    new = [jnp.stack(z, axis=0) if z else None for z in outs]
    if not prompt:
        new[1] = gs_new[-1] if len(gs_new) > 1 else gs_new[0][None]
    return y, tuple(new)


def kernel(x_prompt, x_sample, state_gdn_conv, state_gdn_S, state_mlstm_C, state_mlstm_n, state_mlstm_m, state_rwkv_shift, state_rwkv_S, norm_mix, norm_ffn, norm_final, gdn_w_in, gdn_conv_w, gdn_a_log, gdn_dt_bias, gdn_norm_w, gdn_w_out, ml_w_in, ml_b_if, ml_norm_w, ml_w_out, rw_mu, rw_w_rkv, rw_w_o, rw_w0, rw_w1, rw_w2, rw_a0, rw_a1, rw_a2, rw_g1, rw_g2, rw_k_k, rw_k_a, rw_r_k, rw_lnx_w, rw_lnx_b, ffn_w1, ffn_w2):
    gh, gdk, gdv = state_gdn_S.shape[2:]
    mh, mdk, mdv = state_mlstm_C.shape[2:]
    w = dict(
        norm_mix=norm_mix, norm_ffn=norm_ffn, norm_final=norm_final,
        ffn_w1=ffn_w1.astype(BF16), ffn_w2=ffn_w2.astype(BF16),
        gdn=[_gdn_prep(gdn_w_in[j], gdn_conv_w[j], gdn_a_log[j], gdn_dt_bias[j], gdn_norm_w[j], gdn_w_out[j],
                       heads=gh, dk=gdk, dv=gdv) for j in range(gdn_w_in.shape[0])],
        ml=[_mlstm_prep(ml_w_in[j], ml_b_if[j], ml_norm_w[j], ml_w_out[j], heads=mh, dk=mdk, dv=mdv)
            for j in range(ml_w_in.shape[0])],
        rw=[_rwkv_prep(rw_mu[j], rw_w_rkv[j], rw_w_o[j], rw_w0[j], rw_w1[j], rw_w2[j], rw_a0[j], rw_a1[j],
                       rw_a2[j], rw_g1[j], rw_g2[j], rw_k_k[j], rw_k_a[j], rw_r_k[j], rw_lnx_w[j], rw_lnx_b[j])
            for j in range(rw_mu.shape[0])])
    sample_states = (state_gdn_conv, state_gdn_S, state_mlstm_C, state_mlstm_n, state_mlstm_m,
                     state_rwkv_shift, state_rwkv_S)
    bp, tp, d = x_prompt.shape
    bs, ts, _ = x_sample.shape
    assert ts == 1
    prompt_states = tuple(jnp.zeros((s.shape[0], bp) + s.shape[2:], s.dtype) for s in sample_states)
    y_p, new_p = _trunk(x_prompt.reshape(bp * tp, d), prompt_states, w, batch=bp, seq=tp)
    y_s, new_s = _trunk(x_sample.reshape(bs * ts, d), sample_states, w, batch=bs, seq=ts)
    out = [y_p.reshape(bp, tp, d), y_s.reshape(bs, ts, d)]
    for a, b in zip(new_p, new_s):
        out += [a, b]
    return tuple(out)
```
